```python
import jax, jax.numpy as jnp
from jax import lax
import numpy as np

D_MODEL = 1024
BATCH = 8
SEQ = 4096
DEPTH = 1

HEAD_DIM = 64
MIX_WIDTH = D_MODEL
ATTN_WIDTH = MIX_WIDTH // 2
SGU_WIDTH = MIX_WIDTH - ATTN_WIDTH
N_Q_HEADS = ATTN_WIDTH // HEAD_DIM
N_KV_HEADS = 2
Q_PER_KV = N_Q_HEADS // N_KV_HEADS
KV_WIDTH = N_KV_HEADS * HEAD_DIM
N_SGU_HEADS = 8
SGU_HEAD_DIM = SGU_WIDTH // N_SGU_HEADS
WINDOW = 128
BLOCK = 128
CHUNK = 128
NORM_EPS = 1e-5
NEG_INF = -1e30
SPLIT_SIZES = (ATTN_WIDTH, KV_WIDTH, KV_WIDTH, ATTN_WIDTH, SGU_WIDTH, SGU_WIDTH, SGU_WIDTH)
IN_WIDTH = sum(SPLIT_SIZES)

kernel_name = "hybrid_swa_sink_gmlp_parallel_heads"


def rmsnorm(x, g):
    xf = x.astype(jnp.float32)
    y = xf * lax.rsqrt(jnp.mean(xf * xf, axis=-1, keepdims=True) + NORM_EPS)
    return (y * g.astype(jnp.float32)).astype(x.dtype)


def layernorm(x, g, b):
    xf = x.astype(jnp.float32)
    mu = jnp.mean(xf, axis=-1, keepdims=True)
    xc = xf - mu
    y = xc * lax.rsqrt(jnp.mean(xc * xc, axis=-1, keepdims=True) + NORM_EPS)
    return (y * g.astype(jnp.float32) + b.astype(jnp.float32)).astype(x.dtype)


def banded_sink_attention(q, k, v, sinks):
    B, S = q.shape[0], q.shape[1]
    nb = S // BLOCK
    qb = q.reshape(B, nb, BLOCK, N_KV_HEADS, Q_PER_KV, HEAD_DIM)

    def band(t):
        tb = t.reshape(B, nb, BLOCK, N_KV_HEADS, HEAD_DIM)
        prev = jnp.pad(tb, ((0, 0), (1, 0), (0, 0), (0, 0), (0, 0)))[:, :-1]
        return jnp.concatenate([prev, tb], axis=2)

    kb, vb = band(k), band(v)
    scale = HEAD_DIM ** -0.5
    scores = jnp.einsum('bnqhgd,bnkhd->bnhgqk', qb, kb).astype(jnp.float32) * scale
    qi = jnp.arange(BLOCK)[:, None] + BLOCK
    kj = jnp.arange(2 * BLOCK)[None, :]
    diff = qi - kj
    in_window = (diff >= 0) & (diff < WINDOW)
    key_pos = jnp.arange(nb)[:, None, None] * BLOCK - BLOCK + kj[None]
    valid = in_window[None] & (key_pos >= 0)
    scores = jnp.where(valid[None, :, None, None], scores, NEG_INF)
    sink = sinks.astype(jnp.float32).reshape(N_KV_HEADS, Q_PER_KV)[None, None, :, :, None, None]
    m = jnp.maximum(jnp.max(scores, axis=-1, keepdims=True), sink)
    p = jnp.exp(scores - m)
    probs = p / (jnp.sum(p, axis=-1, keepdims=True) + jnp.exp(sink - m))
    out = jnp.einsum('bnhgqk,bnkhd->bnqhgd', probs.astype(vb.dtype), vb)
    return out.reshape(B, S, ATTN_WIDTH)


def chunked_spatial_gating(u, v, w_s, b_s, ln_g, ln_b):
    B, S = u.shape[0], u.shape[1]
    nc = S // CHUNK
    v = layernorm(v, ln_g, ln_b)
    vc = v.reshape(B, nc, CHUNK, N_SGU_HEADS, SGU_HEAD_DIM)
    causal = jnp.tril(jnp.ones((CHUNK, CHUNK), dtype=bool))
    w = jnp.where(causal[None], w_s, jnp.zeros_like(w_s)).astype(vc.dtype)
    mixed = jnp.einsum('hts,bcshd->bcthd', w, vc) + b_s.T.astype(vc.dtype)[None, None, :, :, None]
    return u * mixed.reshape(B, S, SGU_WIDTH)


def _fwd_setup_inputs(seed: int = 0) -> dict:
    key = jax.random.key(seed)
    ks = jax.random.split(key, 12)
    f32 = jnp.float32
    x = jax.random.normal(ks[0], (BATCH, SEQ, D_MODEL), f32)
    norm_g = 1.0 + 0.02 * jax.random.normal(ks[1], (DEPTH, D_MODEL), f32)
    w_in = jax.random.normal(ks[2], (DEPTH, D_MODEL, IN_WIDTH), f32) * D_MODEL ** -0.5
    b_in = 0.02 * jax.random.normal(ks[3], (DEPTH, IN_WIDTH), f32)
    attn_sinks = 0.5 * jax.random.normal(ks[4], (DEPTH, N_Q_HEADS), f32)
    sgu_ln_g = 1.0 + 0.02 * jax.random.normal(ks[5], (DEPTH, SGU_WIDTH), f32)
    sgu_ln_b = 0.02 * jax.random.normal(ks[6], (DEPTH, SGU_WIDTH), f32)
    sgu_w = jax.random.normal(ks[7], (DEPTH, N_SGU_HEADS, CHUNK, CHUNK), f32) * CHUNK ** -0.5
    sgu_b = 1.0 + 0.02 * jax.random.normal(ks[8], (DEPTH, N_SGU_HEADS, CHUNK), f32)
    w_out = jax.random.normal(ks[9], (DEPTH, MIX_WIDTH, D_MODEL), f32) * MIX_WIDTH ** -0.5
    b_out = 0.02 * jax.random.normal(ks[10], (DEPTH, D_MODEL), f32)
    final_norm_g = 1.0 + 0.02 * jax.random.normal(ks[11], (D_MODEL,), f32)
    return {"x": x, "norm_g": norm_g, "w_in": w_in, "b_in": b_in,
            "attn_sinks": attn_sinks, "sgu_ln_g": sgu_ln_g, "sgu_ln_b": sgu_ln_b,
            "sgu_w": sgu_w, "sgu_b": sgu_b, "w_out": w_out, "b_out": b_out,
            "final_norm_g": final_norm_g}


def _fwd_reference(x, norm_g, w_in, b_in, attn_sinks, sgu_ln_g, sgu_ln_b, sgu_w, sgu_b,
              w_out, b_out, final_norm_g):
    split_points = list(np.cumsum(SPLIT_SIZES)[:-1])
    for l in range(DEPTH):
        h = rmsnorm(x, norm_g[l])
        proj = jnp.einsum('bsd,de->bse', h, w_in[l]) + b_in[l]
        q, k, v, z_a, u_s, v_s, z_s = jnp.split(proj, split_points, axis=-1)
        attn = banded_sink_attention(q, k, v, attn_sinks[l]) * jax.nn.silu(z_a)
        u_s = jax.nn.gelu(u_s, approximate=False)
        v_s = jax.nn.gelu(v_s, approximate=False)
        sgu = chunked_spatial_gating(u_s, v_s, sgu_w[l], sgu_b[l], sgu_ln_g[l], sgu_ln_b[l]) * jax.nn.silu(z_s)
        mixed = jnp.concatenate([attn, sgu], axis=-1)
        x = x + jnp.einsum('bse,ed->bsd', mixed, w_out[l]) + b_out[l]
    return rmsnorm(x, final_norm_g)


import jax as _jax
import jax.numpy as _jnp

TWIN_FORMAT = 'train_step'
FWD_PARAMS = ['x', 'norm_g', 'w_in', 'b_in', 'attn_sinks', 'sgu_ln_g', 'sgu_ln_b', 'sgu_w', 'sgu_b', 'w_out', 'b_out', 'final_norm_g']
TWIN_WEIGHTS = ['norm_g', 'w_in', 'b_in', 'attn_sinks', 'sgu_ln_g', 'sgu_ln_b', 'sgu_w', 'sgu_b', 'w_out', 'b_out', 'final_norm_g']
TWIN_DIFF_INPUT = 'x'
TWIN_INPUTS = ['x', 'norm_g', 'w_in', 'b_in', 'attn_sinks', 'sgu_ln_g', 'sgu_ln_b', 'sgu_w', 'sgu_b', 'w_out', 'b_out', 'final_norm_g', 'loss_target', 'm_norm_g', 'm_w_in', 'm_b_in', 'm_attn_sinks', 'm_sgu_ln_g', 'm_sgu_ln_b', 'm_sgu_w', 'm_sgu_b', 'm_w_out', 'm_b_out', 'm_final_norm_g', 'v_norm_g', 'v_w_in', 'v_b_in', 'v_attn_sinks', 'v_sgu_ln_g', 'v_sgu_ln_b', 'v_sgu_w', 'v_sgu_b', 'v_w_out', 'v_b_out', 'v_final_norm_g']
TWIN_OUTPUTS = ['loss', 'grad_x', 'grad_norm_g', 'grad_w_in', 'grad_b_in', 'grad_attn_sinks', 'grad_sgu_ln_g', 'grad_sgu_ln_b', 'grad_sgu_w', 'grad_sgu_b', 'grad_w_out', 'grad_b_out', 'grad_final_norm_g', 'delta_norm_g', 'delta_w_in', 'delta_b_in', 'delta_attn_sinks', 'delta_sgu_ln_g', 'delta_sgu_ln_b', 'delta_sgu_w', 'delta_sgu_b', 'delta_w_out', 'delta_b_out', 'delta_final_norm_g', 'new_m_norm_g', 'new_m_w_in', 'new_m_b_in', 'new_m_attn_sinks', 'new_m_sgu_ln_g', 'new_m_sgu_ln_b', 'new_m_sgu_w', 'new_m_sgu_b', 'new_m_w_out', 'new_m_b_out', 'new_m_final_norm_g', 'new_v_norm_g', 'new_v_w_in', 'new_v_b_in', 'new_v_attn_sinks', 'new_v_sgu_ln_g', 'new_v_sgu_ln_b', 'new_v_sgu_w', 'new_v_sgu_b', 'new_v_w_out', 'new_v_b_out', 'new_v_final_norm_g']
TWIN_LEAF_KINDS = {'loss': 'loss', 'grad_x': 'grad_x', 'grad_norm_g': 'grad_w', 'grad_w_in': 'grad_w', 'grad_b_in': 'grad_w', 'grad_attn_sinks': 'grad_w', 'grad_sgu_ln_g': 'grad_w', 'grad_sgu_ln_b': 'grad_w', 'grad_sgu_w': 'grad_w', 'grad_sgu_b': 'grad_w', 'grad_w_out': 'grad_w', 'grad_b_out': 'grad_w', 'grad_final_norm_g': 'grad_w', 'delta_norm_g': 'delta_w', 'delta_w_in': 'delta_w', 'delta_b_in': 'delta_w', 'delta_attn_sinks': 'delta_w', 'delta_sgu_ln_g': 'delta_w', 'delta_sgu_ln_b': 'delta_w', 'delta_sgu_w': 'delta_w', 'delta_sgu_b': 'delta_w', 'delta_w_out': 'delta_w', 'delta_b_out': 'delta_w', 'delta_final_norm_g': 'delta_w', 'new_m_norm_g': 'new_m', 'new_m_w_in': 'new_m', 'new_m_b_in': 'new_m', 'new_m_attn_sinks': 'new_m', 'new_m_sgu_ln_g': 'new_m', 'new_m_sgu_ln_b': 'new_m', 'new_m_sgu_w': 'new_m', 'new_m_sgu_b': 'new_m', 'new_m_w_out': 'new_m', 'new_m_b_out': 'new_m', 'new_m_final_norm_g': 'new_m', 'new_v_norm_g': 'new_v', 'new_v_w_in': 'new_v', 'new_v_b_in': 'new_v', 'new_v_attn_sinks': 'new_v', 'new_v_sgu_ln_g': 'new_v', 'new_v_sgu_ln_b': 'new_v', 'new_v_sgu_w': 'new_v', 'new_v_sgu_b': 'new_v', 'new_v_w_out': 'new_v', 'new_v_b_out': 'new_v', 'new_v_final_norm_g': 'new_v'}


def _forward(args):
    return _fwd_reference(*[args[k] for k in FWD_PARAMS])


def _output_shape():
    out = _jax.eval_shape(lambda: _forward(_fwd_setup_inputs(0)))
    return out.shape, out.dtype

N_MICROBATCH = 1
ADAM_LR = 0.001
ADAM_B1 = 0.9
ADAM_B2 = 0.999
ADAM_EPS = 1e-08
ADAM_WD = 0.01
ADAM_STEP = 10
PER_EXAMPLE_BATCH_AXIS = {'x': 0, 'loss_target': 0}
SHARED_INPUTS = []
_WEIGHT_DTYPES = {'norm_g': _jnp.float32, 'w_in': _jnp.float32, 'b_in': _jnp.float32, 'attn_sinks': _jnp.float32, 'sgu_ln_g': _jnp.float32, 'sgu_ln_b': _jnp.float32, 'sgu_w': _jnp.float32, 'sgu_b': _jnp.float32, 'w_out': _jnp.float32, 'b_out': _jnp.float32, 'final_norm_g': _jnp.float32}
MOMENT_SCALE = {'norm_g': 9.428363e-02, 'w_in': 5.660754e-02, 'b_in': 6.819831e-02, 'attn_sinks': 1.719500e-02, 'sgu_ln_g': 5.187110e-02, 'sgu_ln_b': 5.078739e-02, 'sgu_w': 3.231549e-02, 'sgu_b': 4.291258e-02, 'w_out': 5.749093e-02, 'b_out': 1.694945e-01, 'final_norm_g': 3.200414e+01}


def _to_microbatches(a, axis):
    t = _jnp.moveaxis(a, axis, 0)
    t = t.reshape((N_MICROBATCH, t.shape[0] // N_MICROBATCH) + t.shape[1:])
    return _jnp.moveaxis(t, 1, axis + 1)


def setup_inputs(seed: int = 0) -> dict:
    inp = _fwd_setup_inputs(seed)
    key = _jax.random.fold_in(_jax.random.key(seed), 7919)
    shape, _ = _output_shape()
    out = dict(inp)
    out["loss_target"] = _jax.random.normal(_jax.random.fold_in(key, 0), shape, _jnp.float32)
    for i, name in enumerate(TWIN_WEIGHTS):
        w = inp[name].astype(_jnp.float32)
        if MOMENT_SCALE is None:
            s = _jnp.sqrt(_jnp.mean(_jnp.square(w)) + 1e-30)
        else:
            s = MOMENT_SCALE[name]
        km, kv = _jax.random.split(_jax.random.fold_in(key, i + 1))
        out[name] = w
        out["m_" + name] = s * _jax.random.normal(km, w.shape, _jnp.float32)
        out["v_" + name] = (s * s) * _jax.random.uniform(kv, w.shape, _jnp.float32, 0.5, 1.5)
    if N_MICROBATCH > 1:
        for name, axis in PER_EXAMPLE_BATCH_AXIS.items():
            out[name] = _to_microbatches(out[name], axis)
    return {'x': out['x'], 'norm_g': out['norm_g'], 'w_in': out['w_in'], 'b_in': out['b_in'], 'attn_sinks': out['attn_sinks'], 'sgu_ln_g': out['sgu_ln_g'], 'sgu_ln_b': out['sgu_ln_b'], 'sgu_w': out['sgu_w'], 'sgu_b': out['sgu_b'], 'w_out': out['w_out'], 'b_out': out['b_out'], 'final_norm_g': out['final_norm_g'], 'loss_target': out['loss_target'], 'm_norm_g': out['m_norm_g'], 'm_w_in': out['m_w_in'], 'm_b_in': out['m_b_in'], 'm_attn_sinks': out['m_attn_sinks'], 'm_sgu_ln_g': out['m_sgu_ln_g'], 'm_sgu_ln_b': out['m_sgu_ln_b'], 'm_sgu_w': out['m_sgu_w'], 'm_sgu_b': out['m_sgu_b'], 'm_w_out': out['m_w_out'], 'm_b_out': out['m_b_out'], 'm_final_norm_g': out['m_final_norm_g'], 'v_norm_g': out['v_norm_g'], 'v_w_in': out['v_w_in'], 'v_b_in': out['v_b_in'], 'v_attn_sinks': out['v_attn_sinks'], 'v_sgu_ln_g': out['v_sgu_ln_g'], 'v_sgu_ln_b': out['v_sgu_ln_b'], 'v_sgu_w': out['v_sgu_w'], 'v_sgu_b': out['v_sgu_b'], 'v_w_out': out['v_w_out'], 'v_b_out': out['v_b_out'], 'v_final_norm_g': out['v_final_norm_g']}


def _loss(weights, diff, rest, loss_target):
    with _jax.named_scope("forward"):
        args = {**rest, TWIN_DIFF_INPUT: diff, **{k: w.astype(_WEIGHT_DTYPES[k]) for k, w in weights.items()}}
        y = _forward(args)
    with _jax.named_scope("loss_head"):
        err = _jnp.square(y.astype(_jnp.float32) - loss_target)
        return 0.5 * _jnp.sum(_jnp.mean(err, axis=-1)) if err.ndim else 0.5 * err


def _adamw(w, g, m, v):
    m = ADAM_B1 * m + (1.0 - ADAM_B1) * g
    v = ADAM_B2 * v + (1.0 - ADAM_B2) * _jnp.square(g)
    m_hat = m / (1.0 - ADAM_B1 ** ADAM_STEP)
    v_hat = v / (1.0 - ADAM_B2 ** ADAM_STEP)
    delta = -ADAM_LR * (m_hat / (_jnp.sqrt(v_hat) + ADAM_EPS) + ADAM_WD * w)
    return delta, m, v


def reference(x, norm_g, w_in, b_in, attn_sinks, sgu_ln_g, sgu_ln_b, sgu_w, sgu_b, w_out, b_out, final_norm_g, loss_target, m_norm_g, m_w_in, m_b_in, m_attn_sinks, m_sgu_ln_g, m_sgu_ln_b, m_sgu_w, m_sgu_b, m_w_out, m_b_out, m_final_norm_g, v_norm_g, v_w_in, v_b_in, v_attn_sinks, v_sgu_ln_g, v_sgu_ln_b, v_sgu_w, v_sgu_b, v_w_out, v_b_out, v_final_norm_g):
    given = dict(x=x, norm_g=norm_g, w_in=w_in, b_in=b_in, attn_sinks=attn_sinks, sgu_ln_g=sgu_ln_g, sgu_ln_b=sgu_ln_b, sgu_w=sgu_w, sgu_b=sgu_b, w_out=w_out, b_out=b_out, final_norm_g=final_norm_g, loss_target=loss_target, m_norm_g=m_norm_g, m_w_in=m_w_in, m_b_in=m_b_in, m_attn_sinks=m_attn_sinks, m_sgu_ln_g=m_sgu_ln_g, m_sgu_ln_b=m_sgu_ln_b, m_sgu_w=m_sgu_w, m_sgu_b=m_sgu_b, m_w_out=m_w_out, m_b_out=m_b_out, m_final_norm_g=m_final_norm_g, v_norm_g=v_norm_g, v_w_in=v_w_in, v_b_in=v_b_in, v_attn_sinks=v_attn_sinks, v_sgu_ln_g=v_sgu_ln_g, v_sgu_ln_b=v_sgu_ln_b, v_sgu_w=v_sgu_w, v_sgu_b=v_sgu_b, v_w_out=v_w_out, v_b_out=v_b_out, v_final_norm_g=v_final_norm_g)
    weights = {n: given[n] for n in TWIN_WEIGHTS}
    shared = {n: given[n] for n in SHARED_INPUTS}
    per_example = {n: given[n] for n in ['x']}
    grad_fn = _jax.value_and_grad(_loss, argnums=(0, 1))

    def one_microbatch(ex, loss_target):
        ex = dict(ex)
        diff = ex.pop(TWIN_DIFF_INPUT)
        return grad_fn(weights, diff, {**shared, **ex}, loss_target)

    if N_MICROBATCH == 1:
        loss, (grad_w, grad_x) = one_microbatch(per_example, given["loss_target"])
    else:
        def body(carry, xs):
            loss_sum, grad_sum = carry
            l_k, (gw_k, gx_k) = one_microbatch(xs[0], xs[1])
            with _jax.named_scope("update"):
                return (loss_sum + l_k, _jax.tree.map(_jnp.add, grad_sum, gw_k)), gx_k

        init = (_jnp.zeros((), _jnp.float32), _jax.tree.map(_jnp.zeros_like, weights))
        (loss, grad_w), grad_x = _jax.lax.scan(body, init, (per_example, given["loss_target"]))
    with _jax.named_scope("update"):
        delta_w, new_m, new_v = {}, {}, {}
        for n in TWIN_WEIGHTS:
            delta_w[n], new_m[n], new_v[n] = _adamw(weights[n], grad_w[n], given["m_" + n], given["v_" + n])
    return (loss, grad_x, *[grad_w[n] for n in TWIN_WEIGHTS], *[delta_w[n] for n in TWIN_WEIGHTS],
            *[new_m[n] for n in TWIN_WEIGHTS], *[new_v[n] for n in TWIN_WEIGHTS])
```

```python
import functools
import math

import jax
import jax.numpy as jnp
from jax import lax
from jax.experimental import pallas as pl
from jax.experimental.pallas import tpu as pltpu

F32 = jnp.float32
MXU_DTYPE = jnp.bfloat16

D_MODEL = 1024
HEAD_DIM = 64
ATTN_W = 512
SGU_W = 512
N_SGU_HEADS = 8
BLOCK = 128
IN_W = 2816
OFF_K, OFF_V, OFF_ZA, OFF_US, OFF_VS, OFF_ZS = 512, 640, 768, 1280, 1792, 2304
NORM_EPS = 1e-5
NEG_INF = -1e30
SCALE = HEAD_DIM ** -0.5
SQRT_HALF = math.sqrt(0.5)
INV_SQRT_2PI = 1.0 / math.sqrt(2.0 * math.pi)

N_CHIPS = 4
N_DEV = 8
W_IN_SHARD = IN_W // N_CHIPS
W_OUT_SHARD = D_MODEL // N_CHIPS
HALF_A = W_IN_SHARD // 2
HALF_B = W_OUT_SHARD // 2

TILE = 256
VMEM_LIMIT = 56 * 1024 * 1024

ADAM_LR, ADAM_B1, ADAM_B2, ADAM_EPS, ADAM_WD, ADAM_STEP = 0.001, 0.9, 0.999, 1e-08, 0.01, 10

R_SGUW = 0
R_VEC = N_SGU_HEADS * BLOCK
R_G1, R_BIN, R_SINK, R_LOSS, R_LNG, R_LNB, R_SGUB, R_BOUT, R_G3 = (R_VEC + o for o in (0, 8, 32, 40, 48, 56, 64, 72, 80))
SMALL_ROWS = R_VEC + 88

MESH = pl.DeviceIdType.MESH


def _mm(a, b):
    return jnp.dot(a, b, preferred_element_type=F32)


def _mm_nt(a, b):
    return lax.dot_general(a, b, (((1,), (1,)), ((), ())), preferred_element_type=F32)


def _mm_tn(a, b):
    return lax.dot_general(a, b, (((0,), (0,)), ((), ())), preferred_element_type=F32)


def _sigmoid(z):
    return 1.0 / (1.0 + jnp.exp(-z))


def _norm_cdf(z):
    return 0.5 * (1.0 + lax.erf(z * SQRT_HALF))


def _norm_pdf(z):
    return jnp.exp(-0.5 * z * z) * INV_SQRT_2PI


def _rows8(v):
    r, n = v.shape
    return jnp.sum(v.reshape(r // 8, 8, n), axis=0)


def _mean_rows(v):
    return jnp.sum(v, axis=1, keepdims=True) * (1.0 / v.shape[1])


def _fused_call(x, tgt, sinks, g1, b_in, ln_g, ln_b, sgu_w, bexp, b_out, g3, win_t, wout):
    seq = x.shape[0]
    t = TILE
    nt = seq // t
    nb = t // BLOCK
    act = MXU_DTYPE

    def body(sinks_ref, x_ref, xh_ref, tgt_ref, g1_ref, bin_ref, lng_ref, lnb_ref, sguw_ref, bexp_ref, bout_ref, g3_ref,
             wint_hbm, wout_hbm,
             dx_ref, h_ref, dproj_ref, mixed_ref, dout_ref, small_ref,
             wint_v, wout_v, wf_v, wb_v, q_s, kf_s, vf_s, k2_s, v2_s, gate_s, p_s, ps_s, o_s, u_s, mix_s, vhat_s, r2_s,
             cdfu_s, cdfv_s, r1_s, doutf_s, dmix_s, dkf_s, dvf_s, carryk_s, carryv_s,
             acc_bin, acc_g1, acc_bout, acc_g3, acc_lng, acc_lnb, acc_dws, acc_dbs, acc_sink, acc_loss, sems):
        i = pl.program_id(0)
        tile = nt - 1 - i
        lane128 = lax.broadcasted_iota(jnp.int32, (BLOCK, BLOCK), 1)
        lo = lane128 < HEAD_DIM

        @pl.when(i == 0)
        def _():
            cp_a = pltpu.make_async_copy(wint_hbm, wint_v, sems.at[0])
            cp_b = pltpu.make_async_copy(wout_hbm, wout_v, sems.at[1])
            cp_a.start()
            cp_b.start()
            for acc in (acc_bin, acc_g1, acc_bout, acc_g3, acc_lng, acc_lnb, acc_dws, acc_dbs, acc_sink, acc_loss,
                        carryk_s, carryv_s):
                acc[...] = jnp.zeros(acc.shape, F32)
            tril = lax.broadcasted_iota(jnp.int32, (BLOCK, BLOCK), 0) >= lane128
            for h in range(N_SGU_HEADS):
                w = jnp.where(tril, sguw_ref[h], 0.0)
                wf_v[h // 2, :, (h % 2) * BLOCK:(h % 2 + 1) * BLOCK] = w.astype(act)
                wb_v[h // 2, :, (h % 2) * BLOCK:(h % 2 + 1) * BLOCK] = w.T.astype(act)
            cp_a.wait()
            cp_b.wait()

        g1 = g1_ref[...]

        def rms(v):
            r = lax.rsqrt(_mean_rows(v * v) + NORM_EPS)
            return r, v * r

        for c in range(nb):
            rows = slice(c * BLOCK, (c + 1) * BLOCK)
            r, xn = rms(x_ref[rows, :])
            r1_s[rows, :] = r
            h_ref[rows, :] = (xn * g1).astype(act)
        _, xnh = rms(xh_ref[...])
        kvh = _mm_nt((xnh * g1).astype(act), wint_v[OFF_K:OFF_ZA, :]) + bin_ref[:, OFF_K:OFF_ZA]
        kf_s[0:BLOCK, :] = kvh[:, :BLOCK]
        vf_s[0:BLOCK, :] = kvh[:, BLOCK:]

        h = h_ref[...]
        q = _mm_nt(h, wint_v[0:OFF_K, :]) + bin_ref[:, 0:OFF_K]
        q_s[...] = (q * SCALE).astype(act)
        kv = _mm_nt(h, wint_v[OFF_K:OFF_ZA, :]) + bin_ref[:, OFF_K:OFF_ZA]
        kf_s[BLOCK:, :] = kv[:, :BLOCK]
        vf_s[BLOCK:, :] = kv[:, BLOCK:]
        for r in range(4):
            cols = slice(OFF_ZA + r * 512, OFF_ZA + (r + 1) * 512)
            gate_s[r] = _mm_nt(h, wint_v[cols, :]) + bin_ref[:, cols]

        lo_kv = lax.broadcasted_iota(jnp.int32, (t + BLOCK, BLOCK), 1) < HEAD_DIM
        for src, dst in ((kf_s, k2_s), (vf_s, v2_s)):
            v = src[...]
            vr = pltpu.roll(v, HEAD_DIM, 1)
            dst[0] = jnp.where(lo_kv, v, vr).astype(act)
            dst[1] = jnp.where(lo_kv, vr, v).astype(act)

        rowi = lax.broadcasted_iota(jnp.int32, (BLOCK, 2 * BLOCK), 0)
        colj = lax.broadcasted_iota(jnp.int32, (BLOCK, 2 * BLOCK), 1)
        in_band = (colj > rowi) & (colj <= rowi + BLOCK)
        row512 = lax.broadcasted_iota(jnp.int32, (4 * BLOCK, 1), 0)

        def stacked_q(b, g):
            parts = []
            for p in range(2):
                slab = q_s[b * BLOCK:(b + 1) * BLOCK, (2 * g + p) * BLOCK:(2 * g + p + 1) * BLOCK]
                parts += [jnp.where(lo, slab, jnp.zeros_like(slab)), jnp.where(lo, jnp.zeros_like(slab), slab)]
            return jnp.concatenate(parts, axis=0)

        def sink_col(g):
            s = [sinks_ref[4 * g + k] for k in range(4)]
            return jnp.where(row512 < BLOCK, s[0], jnp.where(row512 < 2 * BLOCK, s[1], jnp.where(row512 < 3 * BLOCK, s[2], s[3])))

        for b in range(nb):
            band = slice(b * BLOCK, (b + 2) * BLOCK)
            first_key = jnp.where(tile * nb + b > 0, 0, BLOCK)
            valid = in_band & (colj >= first_key)
            valid4 = jnp.concatenate([valid] * 4, axis=0)
            for g in range(2):
                s = _mm_nt(stacked_q(b, g), k2_s[g, band, :])
                s = jnp.where(valid4, s, NEG_INF)
                sk = sink_col(g)
                m = jnp.maximum(jnp.max(s, axis=1, keepdims=True), sk)
                p = jnp.exp(s - m)
                psk = jnp.exp(sk - m)
                inv = 1.0 / (jnp.sum(p, axis=1, keepdims=True) + psk)
                p = p * inv
                p_s[b * 2 + g] = p
                ps_s[b * 2 + g] = psk * inv
                o2 = _mm(p.astype(act), v2_s[g, band, :])
                for pr in range(2):
                    o_s[b * BLOCK:(b + 1) * BLOCK, (2 * g + pr) * BLOCK:(2 * g + pr + 1) * BLOCK] = jnp.where(
                        lo, o2[(2 * pr) * BLOCK:(2 * pr + 1) * BLOCK], o2[(2 * pr + 1) * BLOCK:(2 * pr + 2) * BLOCK])

        lng = lng_ref[...]
        lnb = lnb_ref[...]
        for c in range(nb):
            rows = slice(c * BLOCK, (c + 1) * BLOCK)
            za = gate_s[0, rows, :]
            mixed_ref[rows, 0:ATTN_W] = (o_s[rows, :] * (za * _sigmoid(za))).astype(act)
            us = gate_s[1, rows, :]
            vs = gate_s[2, rows, :]
            zs = gate_s[3, rows, :]
            cu = _norm_cdf(us)
            cv = _norm_cdf(vs)
            cdfu_s[rows, :] = cu
            cdfv_s[rows, :] = cv
            u = us * cu
            vg = vs * cv
            vc = vg - _mean_rows(vg)
            r2 = lax.rsqrt(_mean_rows(vc * vc) + NORM_EPS)
            vhat = vc * r2
            r2_s[rows, :] = r2
            vhat_s[rows, :] = vhat
            u_s[rows, :] = u
            vln = vhat * lng + lnb
            for p in range(4):
                cols = slice(p * BLOCK, (p + 1) * BLOCK)
                slab = vln[:, cols]
                rhs = jnp.concatenate([jnp.where(lo, slab, 0.0), jnp.where(lo, 0.0, slab)], axis=0).astype(act)
                mix_s[rows, cols] = _mm(wf_v[p], rhs) + bexp_ref[:, cols]
            mixed_ref[rows, ATTN_W:] = (u * mix_s[rows, :] * (zs * _sigmoid(zs))).astype(act)

        g3 = g3_ref[...]
        proj_o = _mm(mixed_ref[...], wout_v[...])
        for c in range(nb):
            rows = slice(c * BLOCK, (c + 1) * BLOCK)
            out = x_ref[rows, :] + proj_o[rows, :] + bout_ref[...]
            r3, on = rms(out)
            e = on * g3 - tgt_ref[rows, :]
            e2 = _rows8(e * e)
            acc_loss[...] += sum(e2[:, k * 128:(k + 1) * 128] for k in range(D_MODEL // 128)) * (0.5 / D_MODEL)
            dy = e * (1.0 / D_MODEL)
            acc_g3[...] += _rows8(dy * on)
            don = dy * g3
            dout = r3 * (don - on * _mean_rows(don * on))
            doutf_s[rows, :] = dout
            dout_ref[rows, :] = dout.astype(act)
            acc_bout[...] += _rows8(dout)
        dmix_s[...] = _mm_nt(dout_ref[...], wout_v[...])

        for c in range(nb):
            rows = slice(c * BLOCK, (c + 1) * BLOCK)
            dso = dmix_s[rows, ATTN_W:]
            u = u_s[rows, :]
            mix = mix_s[rows, :]
            zs = gate_s[3, rows, :]
            sg = _sigmoid(zs)
            sgs = zs * sg
            du = dso * mix * sgs
            dmx = dso * u * sgs
            dzs = dso * u * mix * (sg * (1.0 + zs * (1.0 - sg)))
            us = gate_s[1, rows, :]
            dus = du * (cdfu_s[rows, :] + us * _norm_pdf(us))
            vhat = vhat_s[rows, :]
            vln = (vhat * lng + lnb).astype(act)
            dvln_parts = []
            for p in range(4):
                cols = slice(p * BLOCK, (p + 1) * BLOCK)
                slab = dmx[:, cols]
                d_lo = jnp.where(lo, slab, 0.0).astype(act)
                d_hi = jnp.where(lo, 0.0, slab).astype(act)
                dvln_parts.append(_mm(wb_v[p], jnp.concatenate([d_lo, d_hi], axis=0)))
                acc_dws[(2 * p) * BLOCK:(2 * p + 1) * BLOCK, :] += _mm_nt(d_lo, vln[:, cols])
                acc_dws[(2 * p + 1) * BLOCK:(2 * p + 2) * BLOCK, :] += _mm_nt(d_hi, vln[:, cols])
            dvln = jnp.concatenate(dvln_parts, axis=1)
            acc_dbs[...] += dmx
            acc_lng[...] += _rows8(dvln * vhat)
            acc_lnb[...] += _rows8(dvln)
            dvhat = dvln * lng
            dvg = r2_s[rows, :] * (dvhat - _mean_rows(dvhat) - vhat * _mean_rows(dvhat * vhat))
            vs = gate_s[2, rows, :]
            dvs = dvg * (cdfv_s[rows, :] + vs * _norm_pdf(vs))
            for off, val in ((OFF_US, dus), (OFF_VS, dvs), (OFF_ZS, dzs)):
                dproj_ref[rows, off:off + 512] = val.astype(act)
                acc_bin[:, off:off + 512] += _rows8(val)

        dkf_s[...] = jnp.zeros(dkf_s.shape, F32)
        dvf_s[...] = jnp.zeros(dvf_s.shape, F32)
        for b in range(nb):
            rows = slice(b * BLOCK, (b + 1) * BLOCK)
            band = slice(b * BLOCK, (b + 2) * BLOCK)
            za = gate_s[0, rows, :]
            sg = _sigmoid(za)
            dao = dmix_s[rows, 0:ATTN_W]
            o = o_s[rows, :]
            do = dao * (za * sg)
            dza = dao * o * (sg * (1.0 + za * (1.0 - sg)))
            dproj_ref[rows, OFF_ZA:OFF_US] = dza.astype(act)
            acc_bin[:, OFF_ZA:OFF_US] += _rows8(dza)
            for g in range(2):
                do_parts, delta_parts = [], []
                for pr in range(2):
                    cols = slice((2 * g + pr) * BLOCK, (2 * g + pr + 1) * BLOCK)
                    d_pair = do[:, cols]
                    prod = d_pair * o[:, cols]
                    do_parts += [jnp.where(lo, d_pair, 0.0).astype(act), jnp.where(lo, 0.0, d_pair).astype(act)]
                    delta_parts += [jnp.sum(jnp.where(lo, prod, 0.0), axis=1, keepdims=True),
                                    jnp.sum(jnp.where(lo, 0.0, prod), axis=1, keepdims=True)]
                do_st = jnp.concatenate(do_parts, axis=0)
                delta = jnp.concatenate(delta_parts, axis=0)
                p = p_s[b * 2 + g]
                dp = _mm_nt(do_st, v2_s[g, band, :])
                ds = p * (dp - delta)
                sink_t = ps_s[b * 2 + g] * delta
                for k in range(4):
                    acc_sink[4 * g + k:4 * g + k + 1, :] += -jnp.sum(sink_t[k * BLOCK:(k + 1) * BLOCK], axis=0, keepdims=True)
                ds_a = ds.astype(act)
                dq2 = _mm(ds_a, k2_s[g, band, :]) * SCALE
                for pr in range(2):
                    cols = slice((2 * g + pr) * BLOCK, (2 * g + pr + 1) * BLOCK)
                    dq = jnp.where(lo, dq2[(2 * pr) * BLOCK:(2 * pr + 1) * BLOCK], dq2[(2 * pr + 1) * BLOCK:(2 * pr + 2) * BLOCK])
                    dproj_ref[rows, cols] = dq.astype(act)
                    acc_bin[:, cols] += _rows8(dq)
                lo_band = lax.broadcasted_iota(jnp.int32, (2 * BLOCK, BLOCK), 1) < HEAD_DIM
                mine = lo_band if g == 0 else jnp.logical_not(lo_band)
                for acc, lhs, rhs in ((dkf_s, ds_a, stacked_q(b, g)), (dvf_s, p.astype(act), do_st)):
                    d2 = _mm_tn(lhs, rhs)
                    full = d2 + pltpu.roll(d2, HEAD_DIM, 1)
                    acc[band, :] += jnp.where(mine, full, 0.0)
        for acc, carry, off in ((dkf_s, carryk_s, OFF_K), (dvf_s, carryv_s, OFF_V)):
            acc[t:t + BLOCK, :] += carry[...]
            carry[...] = acc[0:BLOCK, :]
            d = acc[BLOCK:, :]
            dproj_ref[:, off:off + BLOCK] = d.astype(act)
            acc_bin[:, off:off + BLOCK] += _rows8(d)

        dh = _mm(dproj_ref[...], wint_v[...])
        for c in range(nb):
            rows = slice(c * BLOCK, (c + 1) * BLOCK)
            r1 = r1_s[rows, :]
            xn = x_ref[rows, :] * r1
            dhc = dh[rows, :]
            acc_g1[...] += _rows8(dhc * xn)
            dxn = dhc * g1
            dx_ref[rows, :] = doutf_s[rows, :] + r1 * (dxn - xn * _mean_rows(dxn * xn))

        @pl.when(i == nt - 1)
        def _():
            tril = lax.broadcasted_iota(jnp.int32, (BLOCK, BLOCK), 0) >= lane128
            for hh in range(N_SGU_HEADS):
                rws = slice(hh * BLOCK, (hh + 1) * BLOCK)
                small_ref[rws, :] = jnp.where(tril, acc_dws[rws, :], 0.0)
            small_ref[R_VEC:, :] = jnp.zeros((SMALL_ROWS - R_VEC, 128), F32)

            def put(row0, acc):
                s = jnp.sum(acc[...], axis=0, keepdims=True)
                for k in range(acc.shape[1] // 128):
                    small_ref[row0 + k:row0 + k + 1, :] = s[:, k * 128:(k + 1) * 128]

            put(R_G1, acc_g1)
            put(R_BIN, acc_bin)
            put(R_LNG, acc_lng)
            put(R_LNB, acc_lnb)
            put(R_BOUT, acc_bout)
            put(R_G3, acc_g3)
            small_ref[R_SINK:R_SINK + 1, :] = jnp.sum(
                jnp.where(lax.broadcasted_iota(jnp.int32, (8, 128), 0) == lax.broadcasted_iota(jnp.int32, (8, 128), 1),
                          acc_sink[...], 0.0), axis=0, keepdims=True)
            small_ref[R_LOSS:R_LOSS + 1, :] = jnp.zeros((1, 128), F32) + jnp.sum(acc_loss[...])
            dbs_t = acc_dbs[...].T
            small_ref[R_SGUB:R_SGUB + 8, :] = jnp.sum(dbs_t.reshape(N_SGU_HEADS, SGU_W // N_SGU_HEADS, BLOCK), axis=1)

    full = lambda shape: pl.BlockSpec(shape, lambda i: (0,) * len(shape))
    tok = lambda w: pl.BlockSpec((t, w), lambda i: (nt - 1 - i, 0))
    in_specs = [
        pl.BlockSpec(memory_space=pltpu.SMEM),
        tok(D_MODEL),
        pl.BlockSpec((BLOCK, D_MODEL), lambda i: (jnp.maximum((nt - 1 - i) * nb - 1, 0), 0)),
        tok(D_MODEL),
        full((1, D_MODEL)), full((1, IN_W)), full((1, SGU_W)), full((1, SGU_W)),
        full((N_SGU_HEADS, BLOCK, BLOCK)), full((BLOCK, SGU_W)), full((1, D_MODEL)), full((1, D_MODEL)),
        pl.BlockSpec(memory_space=pl.ANY), pl.BlockSpec(memory_space=pl.ANY),
    ]
    out_shape = [
        jax.ShapeDtypeStruct((seq, D_MODEL), F32),
        jax.ShapeDtypeStruct((seq, D_MODEL), act),
        jax.ShapeDtypeStruct((seq, IN_W), act),
        jax.ShapeDtypeStruct((seq, D_MODEL), act),
        jax.ShapeDtypeStruct((seq, D_MODEL), act),
        jax.ShapeDtypeStruct((SMALL_ROWS, 128), F32),
    ]
    out_specs = [tok(D_MODEL), tok(D_MODEL), tok(IN_W), tok(D_MODEL), tok(D_MODEL), full((SMALL_ROWS, 128))]
    vm = pltpu.VMEM
    scratch = [
        vm((IN_W, D_MODEL), act), vm((D_MODEL, D_MODEL), act),
        vm((4, BLOCK, 2 * BLOCK), act), vm((4, BLOCK, 2 * BLOCK), act),
        vm((t, ATTN_W), act),
        vm((t + BLOCK, BLOCK), F32), vm((t + BLOCK, BLOCK), F32),
        vm((2, t + BLOCK, BLOCK), act), vm((2, t + BLOCK, BLOCK), act),
        vm((4, t, 512), F32),
        vm((2 * nb, 4 * BLOCK, 2 * BLOCK), F32), vm((2 * nb, 4 * BLOCK, 1), F32),
        vm((t, ATTN_W), F32), vm((t, SGU_W), F32), vm((t, SGU_W), F32), vm((t, SGU_W), F32), vm((t, 1), F32),
        vm((t, SGU_W), F32), vm((t, SGU_W), F32), vm((t, 1), F32),
        vm((t, D_MODEL), F32), vm((t, D_MODEL), F32),
        vm((t + BLOCK, BLOCK), F32), vm((t + BLOCK, BLOCK), F32), vm((BLOCK, BLOCK), F32), vm((BLOCK, BLOCK), F32),
        vm((8, IN_W), F32), vm((8, D_MODEL), F32), vm((8, D_MODEL), F32), vm((8, D_MODEL), F32),
        vm((8, SGU_W), F32), vm((8, SGU_W), F32), vm((N_SGU_HEADS * BLOCK, BLOCK), F32), vm((BLOCK, SGU_W), F32),
        vm((8, 128), F32), vm((8, 128), F32),
        pltpu.SemaphoreType.DMA((2,)),
    ]
    return pl.pallas_call(
        body, name="fused", grid=(nt,), in_specs=in_specs, out_specs=out_specs, out_shape=out_shape,
        scratch_shapes=scratch,
        compiler_params=pltpu.CompilerParams(dimension_semantics=("arbitrary",), vmem_limit_bytes=VMEM_LIMIT),
    )(sinks, x, x, tgt, g1, b_in, ln_g, ln_b, sgu_w, bexp, b_out, g3, win_t, wout)


def _wgrad_call(h, dproj, mixed, dout):
    seq = h.shape[0]
    tk = min(512, seq)
    nk = seq // tk

    def body(h_ref, dproj_ref, mixed_ref, dout_ref, ga_hbm, gb_hbm, acc_a, acc_b, sems):
        k = pl.program_id(0)

        @pl.when(k == 0)
        def _():
            acc_a[...] = jnp.zeros(acc_a.shape, F32)
            acc_b[...] = jnp.zeros(acc_b.shape, F32)

        acc_a[...] += _mm_tn(dproj_ref[...], h_ref[...])
        acc_b[...] += _mm_tn(mixed_ref[...], dout_ref[...])

        @pl.when(k == nk - 1)
        def _():
            cp_a = pltpu.make_async_copy(acc_a, ga_hbm, sems.at[0])
            cp_b = pltpu.make_async_copy(acc_b, gb_hbm, sems.at[1])
            cp_a.start()
            cp_b.start()
            cp_a.wait()
            cp_b.wait()

    tokb = lambda w: pl.BlockSpec((tk, w), lambda k: (k, 0))
    return pl.pallas_call(
        body, name="wgrad", grid=(nk,),
        in_specs=[tokb(D_MODEL), tokb(IN_W), tokb(D_MODEL), tokb(D_MODEL)],
        out_specs=[pl.BlockSpec(memory_space=pl.ANY), pl.BlockSpec(memory_space=pl.ANY)],
        out_shape=[jax.ShapeDtypeStruct((IN_W, D_MODEL), F32), jax.ShapeDtypeStruct((D_MODEL, D_MODEL), F32)],
        scratch_shapes=[pltpu.VMEM((IN_W, D_MODEL), F32), pltpu.VMEM((D_MODEL, D_MODEL), F32), pltpu.SemaphoreType.DMA((2,))],
        compiler_params=pltpu.CompilerParams(dimension_semantics=("arbitrary",), vmem_limit_bytes=VMEM_LIMIT),
    )(h, dproj, mixed, dout)


def _place():
    x, y, c = lax.axis_index("x"), lax.axis_index("y"), lax.axis_index("c")
    chips = [(1 - x, y), (x, 1 - y), (1 - x, 1 - y)]
    return x, y, c, chips


def _gather_call(a_loc, b_loc):
    def body(a_ref, b_ref, ga_ref, gb_ref, send_sems, recv_sems, loc_sems):
        x, y, c, chips = _place()
        me, sibling = (x, y, c), (x, y, 1 - c)
        j = 2 * x + y

        def half(which, cj, hf):
            ref, shard = (ga_ref, W_IN_SHARD) if which == 0 else (gb_ref, W_OUT_SHARD)
            n = shard // 2
            return ref.at[pl.ds(pl.multiple_of(cj * shard + hf * n, 16), n), :]

        def copy(k, src, dst, to):
            return pltpu.make_async_remote_copy(src_ref=src, dst_ref=dst, send_sem=send_sems.at[k], recv_sem=recv_sems.at[k],
                                                device_id=to, device_id_type=MESH)

        own = [pltpu.make_async_copy(a_ref, ga_ref.at[pl.ds(pl.multiple_of(j * W_IN_SHARD, 16), W_IN_SHARD), :], loc_sems.at[0]),
               pltpu.make_async_copy(b_ref, gb_ref.at[pl.ds(pl.multiple_of(j * W_OUT_SHARD, 16), W_OUT_SHARD), :], loc_sems.at[1])]
        for cp in own:
            cp.start()
        srcs = [a_ref.at[pl.ds(pl.multiple_of(c * HALF_A, 16), HALF_A), :], b_ref.at[pl.ds(pl.multiple_of(c * HALF_B, 16), HALF_B), :]]
        first = [copy(2 * r + w, srcs[w], half(w, j, c), (*chip, c)) for r, chip in enumerate(chips) for w in range(2)]
        for cp in first:
            cp.start()
        passed = []
        for r, (cx, cy) in enumerate(chips):
            cj = 2 * cx + cy
            for w in range(2):
                copy(2 * r + w, half(w, cj, c), half(w, cj, c), me).wait_recv()
                fwd = copy(6 + 2 * r + w, half(w, cj, c), half(w, cj, c), sibling)
                fwd.start()
                passed.append(fwd)
        for r, (cx, cy) in enumerate(chips):
            cj = 2 * cx + cy
            for w in range(2):
                copy(6 + 2 * r + w, half(w, cj, 1 - c), half(w, cj, 1 - c), me).wait_recv()
        for cp in first + passed:
            cp.wait_send()
        for cp in own:
            cp.wait()

    hbm = pl.BlockSpec(memory_space=pl.ANY)
    return pl.pallas_call(
        body, name="gather", in_specs=[hbm, hbm], out_specs=[hbm, hbm],
        out_shape=[jax.ShapeDtypeStruct((IN_W, D_MODEL), a_loc.dtype), jax.ShapeDtypeStruct((D_MODEL, D_MODEL), b_loc.dtype)],
        scratch_shapes=[pltpu.SemaphoreType.DMA((12,)), pltpu.SemaphoreType.DMA((12,)), pltpu.SemaphoreType.DMA((2,))],
    )(a_loc, b_loc)


def _reduce_call(ga, gb, small):
    wire = jnp.bfloat16

    def body(ga_hbm, gb_hbm, small_ref, oa_ref, ob_ref, osmall_ref,
             own_a, own_b, sib_a, sib_b, snd_a, snd_b, in_a, in_b, all_small, send_sems, recv_sems, loc_sems):
        x, y, c, chips = _place()
        me, sibling = (x, y, c), (x, y, 1 - c)
        j = 2 * x + y
        dev = 4 * x + 2 * y + c

        def rows_a(cj, hf):
            return ga_hbm.at[pl.ds(pl.multiple_of(cj * W_IN_SHARD + hf * HALF_A, 8), HALF_A), :]

        def rows_b(cj, hf):
            return gb_hbm.at[pl.ds(pl.multiple_of(cj * W_OUT_SHARD + hf * HALF_B, 8), HALF_B), :]

        def copy(k, src, dst, to):
            return pltpu.make_async_remote_copy(src_ref=src, dst_ref=dst, send_sem=send_sems.at[k], recv_sem=recv_sems.at[k],
                                                device_id=to, device_id_type=MESH)

        mine_small = pltpu.make_async_copy(small_ref, all_small.at[dev], loc_sems.at[8])
        mine_small.start()
        sm_first = [copy(16, small_ref, all_small.at[dev], sibling)]
        sm_first += [copy(17 + r, small_ref, all_small.at[dev], (*chip, c)) for r, chip in enumerate(chips)]
        for cp in sm_first:
            cp.start()

        loads, to_sib = [], []
        for cj in range(N_CHIPS):
            loads += [pltpu.make_async_copy(rows_a(cj, c), own_a.at[cj], loc_sems.at[cj]),
                      pltpu.make_async_copy(rows_b(cj, c), own_b.at[cj], loc_sems.at[4 + cj])]
            to_sib += [copy(cj, rows_a(cj, 1 - c), sib_a.at[cj], sibling), copy(4 + cj, rows_b(cj, 1 - c), sib_b.at[cj], sibling)]
        for cp in loads + to_sib:
            cp.start()

        sm_passed = []
        for r, (cx, cy) in enumerate(chips):
            d = 4 * cx + 2 * cy + c
            copy(17 + r, all_small.at[d], all_small.at[d], me).wait_recv()
            fwd = copy(20 + r, all_small.at[d], all_small.at[d], sibling)
            fwd.start()
            sm_passed.append(fwd)

        for cp in loads:
            cp.wait()
        for cj in range(N_CHIPS):
            copy(cj, own_a.at[cj], sib_a.at[cj], me).wait_recv()
            copy(4 + cj, own_b.at[cj], sib_b.at[cj], me).wait_recv()
            own_a[cj] = own_a[cj] + sib_a[cj]
            own_b[cj] = own_b[cj] + sib_b[cj]

        ici = []
        for r, (cx, cy) in enumerate(chips):
            cj = 2 * cx + cy
            snd_a[r] = own_a[cj].astype(wire)
            snd_b[r] = own_b[cj].astype(wire)
            ici += [copy(8 + r, snd_a.at[r], in_a.at[r], (cx, cy, c)), copy(11 + r, snd_b.at[r], in_b.at[r], (cx, cy, c))]
            ici[-2].start()
            ici[-1].start()
        tot_a = own_a[j]
        tot_b = own_b[j]
        for r in range(3):
            copy(8 + r, snd_a.at[r], in_a.at[r], me).wait_recv()
            copy(11 + r, snd_b.at[r], in_b.at[r], me).wait_recv()
            tot_a = tot_a + in_a[r].astype(F32)
            tot_b = tot_b + in_b[r].astype(F32)
        half_a = oa_ref.at[pl.ds(pl.multiple_of(c * HALF_A, 8), HALF_A), :]
        half_b = ob_ref.at[pl.ds(pl.multiple_of(c * HALF_B, 8), HALF_B), :]
        half_a[...] = tot_a
        half_b[...] = tot_b
        back = [copy(14, half_a, half_a, sibling), copy(15, half_b, half_b, sibling)]
        for cp in back:
            cp.start()

        copy(16, small_ref, all_small.at[dev ^ 1], me).wait_recv()
        for r, (cx, cy) in enumerate(chips):
            d = 4 * cx + 2 * cy + (1 - c)
            copy(20 + r, all_small.at[d], all_small.at[d], me).wait_recv()
        mine_small.wait()
        tot = all_small[0]
        for d in range(1, N_DEV):
            tot = tot + all_small[d]
        osmall_ref[...] = tot

        other_a = oa_ref.at[pl.ds(pl.multiple_of((1 - c) * HALF_A, 8), HALF_A), :]
        other_b = ob_ref.at[pl.ds(pl.multiple_of((1 - c) * HALF_B, 8), HALF_B), :]
        copy(14, other_a, other_a, me).wait_recv()
        copy(15, other_b, other_b, me).wait_recv()
        for cp in sm_first + to_sib + sm_passed + ici + back:
            cp.wait_send()

    hbm = pl.BlockSpec(memory_space=pl.ANY)
    vmem = pl.BlockSpec(memory_space=pltpu.VMEM)
    vm = pltpu.VMEM
    return pl.pallas_call(
        body, name="reduce", in_specs=[hbm, hbm, vmem], out_specs=[vmem, vmem, vmem],
        out_shape=[jax.ShapeDtypeStruct((W_IN_SHARD, D_MODEL), F32), jax.ShapeDtypeStruct((W_OUT_SHARD, D_MODEL), F32),
                   jax.ShapeDtypeStruct((SMALL_ROWS, 128), F32)],
        scratch_shapes=[vm((N_CHIPS, HALF_A, D_MODEL), F32), vm((N_CHIPS, HALF_B, D_MODEL), F32),
                        vm((N_CHIPS, HALF_A, D_MODEL), F32), vm((N_CHIPS, HALF_B, D_MODEL), F32),
                        vm((3, HALF_A, D_MODEL), wire), vm((3, HALF_B, D_MODEL), wire),
                        vm((3, HALF_A, D_MODEL), wire), vm((3, HALF_B, D_MODEL), wire),
                        vm((N_DEV, SMALL_ROWS, 128), F32),
                        pltpu.SemaphoreType.DMA((23,)), pltpu.SemaphoreType.DMA((23,)), pltpu.SemaphoreType.DMA((9,))],
        compiler_params=pltpu.CompilerParams(vmem_limit_bytes=VMEM_LIMIT),
    )(ga, gb, small)


def _adamw_call(w, g, m, v, block_rows, name):
    rows, cols = w.shape

    def body(w_ref, g_ref, m_ref, v_ref, d_ref, nm_ref, nv_ref):
        gg = g_ref[...]
        nm = ADAM_B1 * m_ref[...] + (1.0 - ADAM_B1) * gg
        nv = ADAM_B2 * v_ref[...] + (1.0 - ADAM_B2) * (gg * gg)
        m_hat = nm / (1.0 - ADAM_B1 ** ADAM_STEP)
        v_hat = nv / (1.0 - ADAM_B2 ** ADAM_STEP)
        d_ref[...] = -ADAM_LR * (m_hat / (jnp.sqrt(v_hat) + ADAM_EPS) + ADAM_WD * w_ref[...])
        nm_ref[...] = nm
        nv_ref[...] = nv

    spec = pl.BlockSpec((block_rows, cols), lambda i: (i, 0))
    return pl.pallas_call(
        body, name=name, grid=(rows // block_rows,), in_specs=[spec] * 4, out_specs=[spec] * 3,
        out_shape=[jax.ShapeDtypeStruct((rows, cols), F32)] * 3,
        compiler_params=pltpu.CompilerParams(dimension_semantics=("arbitrary",)),
    )(w, g, m, v)


def _pack_vec(norm_g, b_in, sinks, ln_g, ln_b, sgu_b, b_out, fng):
    def part(a, rows):
        a = a.reshape(-1)
        a = jnp.pad(a, (0, rows * 128 - a.shape[0]))
        return a.reshape(rows, 128)

    return jnp.concatenate([part(norm_g, 8), part(b_in, 24), part(sinks, 8), jnp.zeros((8, 128), F32), part(ln_g, 8),
                            part(ln_b, 8), part(sgu_b, 8), part(b_out, 8), part(fng, 8)], axis=0)


def _unpack_vec(p):
    p = p.reshape(-1)
    take = lambda row, n, shape: p[(row - R_VEC) * 128:(row - R_VEC) * 128 + n].reshape(shape)
    return dict(norm_g=take(R_G1, 1024, (1, 1024)), b_in=take(R_BIN, IN_W, (1, IN_W)), attn_sinks=take(R_SINK, 8, (1, 8)),
                sgu_ln_g=take(R_LNG, 512, (1, 512)), sgu_ln_b=take(R_LNB, 512, (1, 512)), sgu_b=take(R_SGUB, 1024, (1, 8, 128)),
                b_out=take(R_BOUT, 1024, (1, 1024)), final_norm_g=take(R_G3, 1024, (1024,)))


def kernel(x, norm_g, w_in, b_in, attn_sinks, sgu_ln_g, sgu_ln_b, sgu_w, sgu_b, w_out, b_out, final_norm_g, loss_target, m_norm_g, m_w_in, m_b_in, m_attn_sinks, m_sgu_ln_g, m_sgu_ln_b, m_sgu_w, m_sgu_b, m_w_out, m_b_out, m_final_norm_g, v_norm_g, v_w_in, v_b_in, v_attn_sinks, v_sgu_ln_g, v_sgu_ln_b, v_sgu_w, v_sgu_b, v_w_out, v_b_out, v_final_norm_g):
    seq = x.shape[1]
    win_t, wout = _gather_call(w_in[0].T.astype(MXU_DTYPE), w_out[0].astype(MXU_DTYPE))
    bexp = jnp.repeat(sgu_b[0].T, SGU_W // N_SGU_HEADS, axis=1)
    dx, h, dproj, mixed, dout, small = _fused_call(
        x[0], loss_target[0], attn_sinks[0], norm_g, b_in, sgu_ln_g, sgu_ln_b, sgu_w[0], bexp, b_out,
        final_norm_g.reshape(1, D_MODEL), win_t, wout)
    ga, gb = _wgrad_call(h, dproj, mixed, dout)
    ga_t, g_w_out, small = _reduce_call(ga, gb, small)
    g_w_in = ga_t.T

    names = ["norm_g", "w_in", "b_in", "attn_sinks", "sgu_ln_g", "sgu_ln_b", "sgu_w", "sgu_b", "w_out", "b_out", "final_norm_g"]
    grads = _unpack_vec(small[R_VEC:])
    grads["w_in"] = g_w_in[None]
    grads["w_out"] = g_w_out[None]
    grads["sgu_w"] = small[:R_VEC].reshape(1, N_SGU_HEADS, BLOCK, BLOCK)
    loss = small[R_LOSS, 0]

    delta, new_m, new_v = {}, {}, {}
    big = {"w_in": (w_in, m_w_in, v_w_in, 128), "w_out": (w_out, m_w_out, v_w_out, 128), "sgu_w": (sgu_w, m_sgu_w, v_sgu_w, 512)}
    for n, (w, m, v, br) in big.items():
        shape2 = (-1, w.shape[-1])
        d, nm, nv = _adamw_call(w.reshape(shape2), grads[n].reshape(shape2), m.reshape(shape2), v.reshape(shape2), br, "adamw_" + n)
        delta[n], new_m[n], new_v[n] = d.reshape(w.shape), nm.reshape(w.shape), nv.reshape(w.shape)
    vec = lambda p: _pack_vec(p["norm_g"], p["b_in"], p["attn_sinks"], p["sgu_ln_g"], p["sgu_ln_b"], p["sgu_b"], p["b_out"], p["final_norm_g"])
    w_small = dict(norm_g=norm_g, b_in=b_in, attn_sinks=attn_sinks, sgu_ln_g=sgu_ln_g, sgu_ln_b=sgu_ln_b, sgu_b=sgu_b, b_out=b_out, final_norm_g=final_norm_g)
    m_small = dict(norm_g=m_norm_g, b_in=m_b_in, attn_sinks=m_attn_sinks, sgu_ln_g=m_sgu_ln_g, sgu_ln_b=m_sgu_ln_b, sgu_b=m_sgu_b, b_out=m_b_out, final_norm_g=m_final_norm_g)
    v_small = dict(norm_g=v_norm_g, b_in=v_b_in, attn_sinks=v_attn_sinks, sgu_ln_g=v_sgu_ln_g, sgu_ln_b=v_sgu_ln_b, sgu_b=v_sgu_b, b_out=v_b_out, final_norm_g=v_final_norm_g)
    g_vec = small[R_VEC:]
    d, nm, nv = _adamw_call(vec(w_small), g_vec, vec(m_small), vec(v_small), SMALL_ROWS - R_VEC, "adamw_vec")
    for out, packed in ((delta, d), (new_m, nm), (new_v, nv)):
        out.update(_unpack_vec(packed))

    return (loss, dx[None], *[grads[n] for n in names], *[delta[n] for n in names], *[new_m[n] for n in names],
            *[new_v[n] for n in names])
```

```python
import functools
import math

import jax
import jax.numpy as jnp
from jax import lax
from jax.experimental import pallas as pl
from jax.experimental.pallas import tpu as pltpu

F32 = jnp.float32
MXU_DTYPE = jnp.bfloat16

D_MODEL = 1024
HEAD_DIM = 64
ATTN_W = 512
SGU_W = 512
N_SGU_HEADS = 8
BLOCK = 128
IN_W = 2816
OFF_K, OFF_V, OFF_ZA, OFF_US, OFF_VS, OFF_ZS = 512, 640, 768, 1280, 1792, 2304
NORM_EPS = 1e-5
NEG_INF = -1e30
SCALE = HEAD_DIM ** -0.5
SQRT_HALF = math.sqrt(0.5)
INV_SQRT_2PI = 1.0 / math.sqrt(2.0 * math.pi)

N_CHIPS = 4
N_DEV = 8
W_IN_SHARD = IN_W // N_CHIPS
W_OUT_SHARD = D_MODEL // N_CHIPS
HALF_A = W_IN_SHARD // 2
HALF_B = W_OUT_SHARD // 2

TILE = 256
VMEM_LIMIT = 56 * 1024 * 1024

ADAM_LR, ADAM_B1, ADAM_B2, ADAM_EPS, ADAM_WD, ADAM_STEP = 0.001, 0.9, 0.999, 1e-08, 0.01, 10

SW_ROWS = N_SGU_HEADS * BLOCK
R_G1, R_BIN, R_SINK, R_LOSS, R_LNG, R_LNB, R_SGUB, R_BOUT, R_G3 = 0, 8, 32, 40, 48, 56, 64, 72, 80
VEC_ROWS = 88

MESH = pl.DeviceIdType.MESH


def _mm(a, b):
    return jnp.dot(a, b, preferred_element_type=F32)


def _mm_nt(a, b):
    return lax.dot_general(a, b, (((1,), (1,)), ((), ())), preferred_element_type=F32)


def _mm_tn(a, b):
    return lax.dot_general(a, b, (((0,), (0,)), ((), ())), preferred_element_type=F32)


def _sigmoid(z):
    return 1.0 / (1.0 + jnp.exp(-z))


def _norm_cdf(z):
    return 0.5 * (1.0 + lax.erf(z * SQRT_HALF))


def _norm_pdf(z):
    return jnp.exp(-0.5 * z * z) * INV_SQRT_2PI


def _rows8(v):
    r, n = v.shape
    return jnp.sum(v.reshape(r // 8, 8, n), axis=0)


def _mean_rows(v):
    return jnp.sum(v, axis=1, keepdims=True) * (1.0 / v.shape[1])


def _fused_call(x, tgt, sinks, g1, b_in, ln_g, ln_b, sgu_w, bexp, b_out, g3, win_t, wout):
    seq = x.shape[0]
    t = TILE
    nt = seq // t
    nb = t // BLOCK
    act = MXU_DTYPE

    def body(sinks_ref, x_ref, xh_ref, tgt_ref, g1_ref, bin_ref, lng_ref, lnb_ref, sguw_ref, bexp_ref, bout_ref, g3_ref,
             wint_hbm, wout_hbm,
             dx_ref, h_ref, dproj_ref, mixed_ref, dout_ref, sw_ref, vec_ref,
             wint_v, wout_v, wf_v, wb_v, q_s, kf_s, vf_s, k2_s, v2_s, gate_s, p_s, ps_s, o_s, u_s, mix_s, vhat_s, r2_s,
             cdfu_s, cdfv_s, r1_s, doutf_s, dmix_s, dkf_s, dvf_s, carryk_s, carryv_s,
             acc_bin, acc_g1, acc_bout, acc_g3, acc_lng, acc_lnb, acc_dws, acc_dbs, acc_sink, acc_loss, sems):
        i = pl.program_id(0)
        tile = nt - 1 - i
        lane128 = lax.broadcasted_iota(jnp.int32, (BLOCK, BLOCK), 1)
        lo = lane128 < HEAD_DIM

        @pl.when(i == 0)
        def _():
            cp_a = pltpu.make_async_copy(wint_hbm, wint_v, sems.at[0])
            cp_b = pltpu.make_async_copy(wout_hbm, wout_v, sems.at[1])
            cp_a.start()
            cp_b.start()
            for acc in (acc_bin, acc_g1, acc_bout, acc_g3, acc_lng, acc_lnb, acc_dws, acc_dbs, acc_sink, acc_loss,
                        carryk_s, carryv_s):
                acc[...] = jnp.zeros(acc.shape, F32)
            tril = lax.broadcasted_iota(jnp.int32, (BLOCK, BLOCK), 0) >= lane128
            for h in range(N_SGU_HEADS):
                w = jnp.where(tril, sguw_ref[h], 0.0)
                wf_v[h // 2, :, (h % 2) * BLOCK:(h % 2 + 1) * BLOCK] = w.astype(act)
                wb_v[h // 2, :, (h % 2) * BLOCK:(h % 2 + 1) * BLOCK] = w.T.astype(act)
            cp_a.wait()
            cp_b.wait()

        g1 = g1_ref[...]

        def rms(v):
            r = lax.rsqrt(_mean_rows(v * v) + NORM_EPS)
            return r, v * r

        for c in range(nb):
            rows = slice(c * BLOCK, (c + 1) * BLOCK)
            r, xn = rms(x_ref[rows, :])
            r1_s[rows, :] = r
            h_ref[rows, :] = (xn * g1).astype(act)
        _, xnh = rms(xh_ref[...])
        kvh = _mm_nt((xnh * g1).astype(act), wint_v[OFF_K:OFF_ZA, :]) + bin_ref[:, OFF_K:OFF_ZA]
        kf_s[0:BLOCK, :] = kvh[:, :BLOCK]
        vf_s[0:BLOCK, :] = kvh[:, BLOCK:]

        h = h_ref[...]
        q = _mm_nt(h, wint_v[0:OFF_K, :]) + bin_ref[:, 0:OFF_K]
        q_s[...] = (q * SCALE).astype(act)
        kv = _mm_nt(h, wint_v[OFF_K:OFF_ZA, :]) + bin_ref[:, OFF_K:OFF_ZA]
        kf_s[BLOCK:, :] = kv[:, :BLOCK]
        vf_s[BLOCK:, :] = kv[:, BLOCK:]
        for r in range(4):
            cols = slice(OFF_ZA + r * 512, OFF_ZA + (r + 1) * 512)
            gate_s[r] = _mm_nt(h, wint_v[cols, :]) + bin_ref[:, cols]

        lo_kv = lax.broadcasted_iota(jnp.int32, (t + BLOCK, BLOCK), 1) < HEAD_DIM
        for src, dst in ((kf_s, k2_s), (vf_s, v2_s)):
            v = src[...]
            vr = pltpu.roll(v, HEAD_DIM, 1)
            dst[0] = jnp.where(lo_kv, v, vr).astype(act)
            dst[1] = jnp.where(lo_kv, vr, v).astype(act)

        rowi = lax.broadcasted_iota(jnp.int32, (BLOCK, 2 * BLOCK), 0)
        colj = lax.broadcasted_iota(jnp.int32, (BLOCK, 2 * BLOCK), 1)
        in_band = (colj > rowi) & (colj <= rowi + BLOCK)
        row512 = lax.broadcasted_iota(jnp.int32, (4 * BLOCK, 1), 0)

        def stacked_q(b, g):
            parts = []
            for p in range(2):
                slab = q_s[b * BLOCK:(b + 1) * BLOCK, (2 * g + p) * BLOCK:(2 * g + p + 1) * BLOCK]
                parts += [jnp.where(lo, slab, jnp.zeros_like(slab)), jnp.where(lo, jnp.zeros_like(slab), slab)]
            return jnp.concatenate(parts, axis=0)

        def sink_col(g):
            s = [sinks_ref[4 * g + k] for k in range(4)]
            return jnp.where(row512 < BLOCK, s[0], jnp.where(row512 < 2 * BLOCK, s[1], jnp.where(row512 < 3 * BLOCK, s[2], s[3])))

        for b in range(nb):
            band = slice(b * BLOCK, (b + 2) * BLOCK)
            first_key = jnp.where(tile * nb + b > 0, 0, BLOCK)
            valid = in_band & (colj >= first_key)
            valid4 = jnp.concatenate([valid] * 4, axis=0)
            for g in range(2):
                s = _mm_nt(stacked_q(b, g), k2_s[g, band, :])
                s = jnp.where(valid4, s, NEG_INF)
                sk = sink_col(g)
                m = jnp.maximum(jnp.max(s, axis=1, keepdims=True), sk)
                p = jnp.exp(s - m)
                psk = jnp.exp(sk - m)
                inv = 1.0 / (jnp.sum(p, axis=1, keepdims=True) + psk)
                p = p * inv
                p_s[b * 2 + g] = p
                ps_s[b * 2 + g] = psk * inv
                o2 = _mm(p.astype(act), v2_s[g, band, :])
                for pr in range(2):
                    o_s[b * BLOCK:(b + 1) * BLOCK, (2 * g + pr) * BLOCK:(2 * g + pr + 1) * BLOCK] = jnp.where(
                        lo, o2[(2 * pr) * BLOCK:(2 * pr + 1) * BLOCK], o2[(2 * pr + 1) * BLOCK:(2 * pr + 2) * BLOCK])

        lng = lng_ref[...]
        lnb = lnb_ref[...]
        for c in range(nb):
            rows = slice(c * BLOCK, (c + 1) * BLOCK)
            za = gate_s[0, rows, :]
            mixed_ref[rows, 0:ATTN_W] = (o_s[rows, :] * (za * _sigmoid(za))).astype(act)
            us = gate_s[1, rows, :]
            vs = gate_s[2, rows, :]
            zs = gate_s[3, rows, :]
            cu = _norm_cdf(us)
            cv = _norm_cdf(vs)
            cdfu_s[rows, :] = cu
            cdfv_s[rows, :] = cv
            u = us * cu
            vg = vs * cv
            vc = vg - _mean_rows(vg)
            r2 = lax.rsqrt(_mean_rows(vc * vc) + NORM_EPS)
            vhat = vc * r2
            r2_s[rows, :] = r2
            vhat_s[rows, :] = vhat
            u_s[rows, :] = u
            vln = vhat * lng + lnb
            for p in range(4):
                cols = slice(p * BLOCK, (p + 1) * BLOCK)
                slab = vln[:, cols]
                rhs = jnp.concatenate([jnp.where(lo, slab, 0.0), jnp.where(lo, 0.0, slab)], axis=0).astype(act)
                mix_s[rows, cols] = _mm(wf_v[p], rhs) + bexp_ref[:, cols]
            mixed_ref[rows, ATTN_W:] = (u * mix_s[rows, :] * (zs * _sigmoid(zs))).astype(act)

        g3 = g3_ref[...]
        proj_o = _mm(mixed_ref[...], wout_v[...])
        for c in range(nb):
            rows = slice(c * BLOCK, (c + 1) * BLOCK)
            out = x_ref[rows, :] + proj_o[rows, :] + bout_ref[...]
            r3, on = rms(out)
            e = on * g3 - tgt_ref[rows, :]
            e2 = _rows8(e * e)
            acc_loss[...] += sum(e2[:, k * 128:(k + 1) * 128] for k in range(D_MODEL // 128)) * (0.5 / D_MODEL)
            dy = e * (1.0 / D_MODEL)
            acc_g3[...] += _rows8(dy * on)
            don = dy * g3
            dout = r3 * (don - on * _mean_rows(don * on))
            doutf_s[rows, :] = dout
            dout_ref[rows, :] = dout.astype(act)
            acc_bout[...] += _rows8(dout)
        dmix_s[...] = _mm_nt(dout_ref[...], wout_v[...])

        for c in range(nb):
            rows = slice(c * BLOCK, (c + 1) * BLOCK)
            dso = dmix_s[rows, ATTN_W:]
            u = u_s[rows, :]
            mix = mix_s[rows, :]
            zs = gate_s[3, rows, :]
            sg = _sigmoid(zs)
            sgs = zs * sg
            du = dso * mix * sgs
            dmx = dso * u * sgs
            dzs = dso * u * mix * (sg * (1.0 + zs * (1.0 - sg)))
            us = gate_s[1, rows, :]
            dus = du * (cdfu_s[rows, :] + us * _norm_pdf(us))
            vhat = vhat_s[rows, :]
            vln = (vhat * lng + lnb).astype(act)
            dvln_parts = []
            for p in range(4):
                cols = slice(p * BLOCK, (p + 1) * BLOCK)
                slab = dmx[:, cols]
                d_lo = jnp.where(lo, slab, 0.0).astype(act)
                d_hi = jnp.where(lo, 0.0, slab).astype(act)
                dvln_parts.append(_mm(wb_v[p], jnp.concatenate([d_lo, d_hi], axis=0)))
                acc_dws[(2 * p) * BLOCK:(2 * p + 1) * BLOCK, :] += _mm_nt(d_lo, vln[:, cols])
                acc_dws[(2 * p + 1) * BLOCK:(2 * p + 2) * BLOCK, :] += _mm_nt(d_hi, vln[:, cols])
            dvln = jnp.concatenate(dvln_parts, axis=1)
            acc_dbs[...] += dmx
            acc_lng[...] += _rows8(dvln * vhat)
            acc_lnb[...] += _rows8(dvln)
            dvhat = dvln * lng
            dvg = r2_s[rows, :] * (dvhat - _mean_rows(dvhat) - vhat * _mean_rows(dvhat * vhat))
            vs = gate_s[2, rows, :]
            dvs = dvg * (cdfv_s[rows, :] + vs * _norm_pdf(vs))
            for off, val in ((OFF_US, dus), (OFF_VS, dvs), (OFF_ZS, dzs)):
                dproj_ref[rows, off:off + 512] = val.astype(act)
                acc_bin[:, off:off + 512] += _rows8(val)

        dkf_s[...] = jnp.zeros(dkf_s.shape, F32)
        dvf_s[...] = jnp.zeros(dvf_s.shape, F32)
        for b in range(nb):
            rows = slice(b * BLOCK, (b + 1) * BLOCK)
            band = slice(b * BLOCK, (b + 2) * BLOCK)
            za = gate_s[0, rows, :]
            sg = _sigmoid(za)
            dao = dmix_s[rows, 0:ATTN_W]
            o = o_s[rows, :]
            do = dao * (za * sg)
            dza = dao * o * (sg * (1.0 + za * (1.0 - sg)))
            dproj_ref[rows, OFF_ZA:OFF_US] = dza.astype(act)
            acc_bin[:, OFF_ZA:OFF_US] += _rows8(dza)
            for g in range(2):
                do_parts, delta_parts = [], []
                for pr in range(2):
                    cols = slice((2 * g + pr) * BLOCK, (2 * g + pr + 1) * BLOCK)
                    d_pair = do[:, cols]
                    prod = d_pair * o[:, cols]
                    do_parts += [jnp.where(lo, d_pair, 0.0).astype(act), jnp.where(lo, 0.0, d_pair).astype(act)]
                    delta_parts += [jnp.sum(jnp.where(lo, prod, 0.0), axis=1, keepdims=True),
                                    jnp.sum(jnp.where(lo, 0.0, prod), axis=1, keepdims=True)]
                do_st = jnp.concatenate(do_parts, axis=0)
                delta = jnp.concatenate(delta_parts, axis=0)
                p = p_s[b * 2 + g]
                dp = _mm_nt(do_st, v2_s[g, band, :])
                ds = p * (dp - delta)
                sink_t = ps_s[b * 2 + g] * delta
                for k in range(4):
                    acc_sink[4 * g + k:4 * g + k + 1, :] += -jnp.sum(sink_t[k * BLOCK:(k + 1) * BLOCK], axis=0, keepdims=True)
                ds_a = ds.astype(act)
                dq2 = _mm(ds_a, k2_s[g, band, :]) * SCALE
                for pr in range(2):
                    cols = slice((2 * g + pr) * BLOCK, (2 * g + pr + 1) * BLOCK)
                    dq = jnp.where(lo, dq2[(2 * pr) * BLOCK:(2 * pr + 1) * BLOCK], dq2[(2 * pr + 1) * BLOCK:(2 * pr + 2) * BLOCK])
                    dproj_ref[rows, cols] = dq.astype(act)
                    acc_bin[:, cols] += _rows8(dq)
                lo_band = lax.broadcasted_iota(jnp.int32, (2 * BLOCK, BLOCK), 1) < HEAD_DIM
                mine = lo_band if g == 0 else jnp.logical_not(lo_band)
                for acc, lhs, rhs in ((dkf_s, ds_a, stacked_q(b, g)), (dvf_s, p.astype(act), do_st)):
                    d2 = _mm_tn(lhs, rhs)
                    full = d2 + pltpu.roll(d2, HEAD_DIM, 1)
                    acc[band, :] += jnp.where(mine, full, 0.0)
        for acc, carry, off in ((dkf_s, carryk_s, OFF_K), (dvf_s, carryv_s, OFF_V)):
            acc[t:t + BLOCK, :] += carry[...]
            carry[...] = acc[0:BLOCK, :]
            d = acc[BLOCK:, :]
            dproj_ref[:, off:off + BLOCK] = d.astype(act)
            acc_bin[:, off:off + BLOCK] += _rows8(d)

        dh = _mm(dproj_ref[...], wint_v[...])
        for c in range(nb):
            rows = slice(c * BLOCK, (c + 1) * BLOCK)
            r1 = r1_s[rows, :]
            xn = x_ref[rows, :] * r1
            dhc = dh[rows, :]
            acc_g1[...] += _rows8(dhc * xn)
            dxn = dhc * g1
            dx_ref[rows, :] = doutf_s[rows, :] + r1 * (dxn - xn * _mean_rows(dxn * xn))

        @pl.when(i == nt - 1)
        def _():
            tril = lax.broadcasted_iota(jnp.int32, (BLOCK, BLOCK), 0) >= lane128
            for hh in range(N_SGU_HEADS):
                rws = slice(hh * BLOCK, (hh + 1) * BLOCK)
                sw_ref[rws, :] = jnp.where(tril, acc_dws[rws, :], 0.0)
            vec_ref[...] = jnp.zeros((VEC_ROWS, 128), F32)

            def put(row0, acc):
                s = jnp.sum(acc[...], axis=0, keepdims=True)
                for k in range(acc.shape[1] // 128):
                    vec_ref[row0 + k:row0 + k + 1, :] = s[:, k * 128:(k + 1) * 128]

            put(R_G1, acc_g1)
            put(R_BIN, acc_bin)
            put(R_LNG, acc_lng)
            put(R_LNB, acc_lnb)
            put(R_BOUT, acc_bout)
            put(R_G3, acc_g3)
            vec_ref[R_SINK:R_SINK + 1, :] = jnp.sum(
                jnp.where(lax.broadcasted_iota(jnp.int32, (8, 128), 0) == lax.broadcasted_iota(jnp.int32, (8, 128), 1),
                          acc_sink[...], 0.0), axis=0, keepdims=True)
            vec_ref[R_LOSS:R_LOSS + 1, :] = jnp.zeros((1, 128), F32) + jnp.sum(acc_loss[...])
            dbs_t = acc_dbs[...].T
            vec_ref[R_SGUB:R_SGUB + 8, :] = jnp.sum(dbs_t.reshape(N_SGU_HEADS, SGU_W // N_SGU_HEADS, BLOCK), axis=1)

    full = lambda shape: pl.BlockSpec(shape, lambda i: (0,) * len(shape))
    tok = lambda w: pl.BlockSpec((t, w), lambda i: (nt - 1 - i, 0))
    in_specs = [
        pl.BlockSpec(memory_space=pltpu.SMEM),
        tok(D_MODEL),
        pl.BlockSpec((BLOCK, D_MODEL), lambda i: (jnp.maximum((nt - 1 - i) * nb - 1, 0), 0)),
        tok(D_MODEL),
        full((1, D_MODEL)), full((1, IN_W)), full((1, SGU_W)), full((1, SGU_W)),
        full((N_SGU_HEADS, BLOCK, BLOCK)), full((BLOCK, SGU_W)), full((1, D_MODEL)), full((1, D_MODEL)),
        pl.BlockSpec(memory_space=pl.ANY), pl.BlockSpec(memory_space=pl.ANY),
    ]
    out_shape = [
        jax.ShapeDtypeStruct((seq, D_MODEL), F32),
        jax.ShapeDtypeStruct((seq, D_MODEL), act),
        jax.ShapeDtypeStruct((seq, IN_W), act),
        jax.ShapeDtypeStruct((seq, D_MODEL), act),
        jax.ShapeDtypeStruct((seq, D_MODEL), act),
        jax.ShapeDtypeStruct((SW_ROWS, 128), F32),
        jax.ShapeDtypeStruct((VEC_ROWS, 128), F32),
    ]
    out_specs = [tok(D_MODEL), tok(D_MODEL), tok(IN_W), tok(D_MODEL), tok(D_MODEL), full((SW_ROWS, 128)), full((VEC_ROWS, 128))]
    vm = pltpu.VMEM
    scratch = [
        vm((IN_W, D_MODEL), act), vm((D_MODEL, D_MODEL), act),
        vm((4, BLOCK, 2 * BLOCK), act), vm((4, BLOCK, 2 * BLOCK), act),
        vm((t, ATTN_W), act),
        vm((t + BLOCK, BLOCK), F32), vm((t + BLOCK, BLOCK), F32),
        vm((2, t + BLOCK, BLOCK), act), vm((2, t + BLOCK, BLOCK), act),
        vm((4, t, 512), F32),
        vm((2 * nb, 4 * BLOCK, 2 * BLOCK), F32), vm((2 * nb, 4 * BLOCK, 1), F32),
        vm((t, ATTN_W), F32), vm((t, SGU_W), F32), vm((t, SGU_W), F32), vm((t, SGU_W), F32), vm((t, 1), F32),
        vm((t, SGU_W), F32), vm((t, SGU_W), F32), vm((t, 1), F32),
        vm((t, D_MODEL), F32), vm((t, D_MODEL), F32),
        vm((t + BLOCK, BLOCK), F32), vm((t + BLOCK, BLOCK), F32), vm((BLOCK, BLOCK), F32), vm((BLOCK, BLOCK), F32),
        vm((8, IN_W), F32), vm((8, D_MODEL), F32), vm((8, D_MODEL), F32), vm((8, D_MODEL), F32),
        vm((8, SGU_W), F32), vm((8, SGU_W), F32), vm((N_SGU_HEADS * BLOCK, BLOCK), F32), vm((BLOCK, SGU_W), F32),
        vm((8, 128), F32), vm((8, 128), F32),
        pltpu.SemaphoreType.DMA((2,)),
    ]
    return pl.pallas_call(
        body, name="fused", grid=(nt,), in_specs=in_specs, out_specs=out_specs, out_shape=out_shape,
        scratch_shapes=scratch,
        compiler_params=pltpu.CompilerParams(dimension_semantics=("arbitrary",), vmem_limit_bytes=VMEM_LIMIT),
    )(sinks, x, x, tgt, g1, b_in, ln_g, ln_b, sgu_w, bexp, b_out, g3, win_t, wout)


def _wgrad_call(h, dproj, mixed, dout):
    seq = h.shape[0]
    tk = min(512, seq)
    nk = seq // tk

    def body(h_ref, dproj_ref, mixed_ref, dout_ref, ga_hbm, gb_hbm, acc_a, acc_b, sems):
        k = pl.program_id(0)

        @pl.when(k == 0)
        def _():
            acc_a[...] = jnp.zeros(acc_a.shape, F32)
            acc_b[...] = jnp.zeros(acc_b.shape, F32)

        acc_a[...] += _mm_tn(dproj_ref[...], h_ref[...])
        acc_b[...] += _mm_tn(mixed_ref[...], dout_ref[...])

        @pl.when(k == nk - 1)
        def _():
            cp_a = pltpu.make_async_copy(acc_a, ga_hbm, sems.at[0])
            cp_b = pltpu.make_async_copy(acc_b, gb_hbm, sems.at[1])
            cp_a.start()
            cp_b.start()
            cp_a.wait()
            cp_b.wait()

    tokb = lambda w: pl.BlockSpec((tk, w), lambda k: (k, 0))
    return pl.pallas_call(
        body, name="wgrad", grid=(nk,),
        in_specs=[tokb(D_MODEL), tokb(IN_W), tokb(D_MODEL), tokb(D_MODEL)],
        out_specs=[pl.BlockSpec(memory_space=pl.ANY), pl.BlockSpec(memory_space=pl.ANY)],
        out_shape=[jax.ShapeDtypeStruct((IN_W, D_MODEL), F32), jax.ShapeDtypeStruct((D_MODEL, D_MODEL), F32)],
        scratch_shapes=[pltpu.VMEM((IN_W, D_MODEL), F32), pltpu.VMEM((D_MODEL, D_MODEL), F32), pltpu.SemaphoreType.DMA((2,))],
        compiler_params=pltpu.CompilerParams(dimension_semantics=("arbitrary",), vmem_limit_bytes=VMEM_LIMIT),
    )(h, dproj, mixed, dout)


def _place():
    x, y, c = lax.axis_index("x"), lax.axis_index("y"), lax.axis_index("c")
    chips = [(1 - x, y), (x, 1 - y), (1 - x, 1 - y)]
    return x, y, c, chips


def _gather_call(a_loc, b_loc):
    def body(a_ref, b_ref, ga_ref, gb_ref, send_sems, recv_sems, loc_sems):
        x, y, c, chips = _place()
        me, sibling = (x, y, c), (x, y, 1 - c)
        j = 2 * x + y

        def half(which, cj, hf):
            ref, shard = (ga_ref, W_IN_SHARD) if which == 0 else (gb_ref, W_OUT_SHARD)
            n = shard // 2
            return ref.at[pl.ds(pl.multiple_of(cj * shard + hf * n, 16), n), :]

        def copy(k, src, dst, to):
            return pltpu.make_async_remote_copy(src_ref=src, dst_ref=dst, send_sem=send_sems.at[k], recv_sem=recv_sems.at[k],
                                                device_id=to, device_id_type=MESH)

        own = [pltpu.make_async_copy(a_ref, ga_ref.at[pl.ds(pl.multiple_of(j * W_IN_SHARD, 16), W_IN_SHARD), :], loc_sems.at[0]),
               pltpu.make_async_copy(b_ref, gb_ref.at[pl.ds(pl.multiple_of(j * W_OUT_SHARD, 16), W_OUT_SHARD), :], loc_sems.at[1])]
        for cp in own:
            cp.start()
        srcs = [a_ref.at[pl.ds(pl.multiple_of(c * HALF_A, 16), HALF_A), :], b_ref.at[pl.ds(pl.multiple_of(c * HALF_B, 16), HALF_B), :]]
        first = [copy(2 * r + w, srcs[w], half(w, j, c), (*chip, c)) for r, chip in enumerate(chips) for w in range(2)]
        for cp in first:
            cp.start()
        passed = []
        for r, (cx, cy) in enumerate(chips):
            cj = 2 * cx + cy
            for w in range(2):
                copy(2 * r + w, half(w, cj, c), half(w, cj, c), me).wait_recv()
                fwd = copy(6 + 2 * r + w, half(w, cj, c), half(w, cj, c), sibling)
                fwd.start()
                passed.append(fwd)
        for r, (cx, cy) in enumerate(chips):
            cj = 2 * cx + cy
            for w in range(2):
                copy(6 + 2 * r + w, half(w, cj, 1 - c), half(w, cj, 1 - c), me).wait_recv()
        for cp in first + passed:
            cp.wait_send()
        for cp in own:
            cp.wait()

    hbm = pl.BlockSpec(memory_space=pl.ANY)
    return pl.pallas_call(
        body, name="gather", in_specs=[hbm, hbm], out_specs=[hbm, hbm],
        out_shape=[jax.ShapeDtypeStruct((IN_W, D_MODEL), a_loc.dtype), jax.ShapeDtypeStruct((D_MODEL, D_MODEL), b_loc.dtype)],
        scratch_shapes=[pltpu.SemaphoreType.DMA((12,)), pltpu.SemaphoreType.DMA((12,)), pltpu.SemaphoreType.DMA((2,))],
    )(a_loc, b_loc)


def _reduce_call(ga, gb, sw, vec):
    wire = jnp.bfloat16
    half_sw = SW_ROWS // 2

    def body(ga_hbm, gb_hbm, sw_ref, small_ref, oa_ref, ob_ref, osw_ref, osmall_ref,
             own_a, own_b, sib_a, sib_b, snd_a, snd_b, in_a, in_b, all_small, sw_sib, sw_chips,
             send_sems, recv_sems, loc_sems):
        x, y, c, chips = _place()
        me, sibling = (x, y, c), (x, y, 1 - c)
        j = 2 * x + y
        dev = 4 * x + 2 * y + c

        def rows_a(cj, hf):
            return ga_hbm.at[pl.ds(pl.multiple_of(cj * W_IN_SHARD + hf * HALF_A, 8), HALF_A), :]

        def rows_b(cj, hf):
            return gb_hbm.at[pl.ds(pl.multiple_of(cj * W_OUT_SHARD + hf * HALF_B, 8), HALF_B), :]

        def copy(k, src, dst, to):
            return pltpu.make_async_remote_copy(src_ref=src, dst_ref=dst, send_sem=send_sems.at[k], recv_sem=recv_sems.at[k],
                                                device_id=to, device_id_type=MESH)

        mine_small = pltpu.make_async_copy(small_ref, all_small.at[dev], loc_sems.at[8])
        mine_small.start()
        sm_first = [copy(16, small_ref, all_small.at[dev], sibling)]
        sm_first += [copy(17 + r, small_ref, all_small.at[dev], (*chip, c)) for r, chip in enumerate(chips)]
        for cp in sm_first:
            cp.start()

        def sw_rows(ref, hf):
            return ref.at[pl.ds(pl.multiple_of(hf * half_sw, 8), half_sw), :]

        sw_to_sib = copy(23, sw_rows(sw_ref, 1 - c), sw_sib, sibling)
        sw_to_sib.start()

        loads, to_sib = [], []
        for cj in range(N_CHIPS):
            loads += [pltpu.make_async_copy(rows_a(cj, c), own_a.at[cj], loc_sems.at[cj]),
                      pltpu.make_async_copy(rows_b(cj, c), own_b.at[cj], loc_sems.at[4 + cj])]
            to_sib += [copy(cj, rows_a(cj, 1 - c), sib_a.at[cj], sibling), copy(4 + cj, rows_b(cj, 1 - c), sib_b.at[cj], sibling)]
        for cp in loads + to_sib:
            cp.start()

        sm_passed = []
        for r, (cx, cy) in enumerate(chips):
            d = 4 * cx + 2 * cy + c
            copy(17 + r, all_small.at[d], all_small.at[d], me).wait_recv()
            fwd = copy(20 + r, all_small.at[d], all_small.at[d], sibling)
            fwd.start()
            sm_passed.append(fwd)

        copy(23, sw_sib, sw_sib, me).wait_recv()
        sw_chips[j] = (sw_rows(sw_ref, c)[...] + sw_sib[...]).astype(wire)
        sw_ici = [copy(24 + r, sw_chips.at[j], sw_chips.at[j], (*chip, c)) for r, chip in enumerate(chips)]
        for cp in sw_ici:
            cp.start()

        for cp in loads:
            cp.wait()
        for cj in range(N_CHIPS):
            copy(cj, own_a.at[cj], sib_a.at[cj], me).wait_recv()
            copy(4 + cj, own_b.at[cj], sib_b.at[cj], me).wait_recv()
            own_a[cj] = own_a[cj] + sib_a[cj]
            own_b[cj] = own_b[cj] + sib_b[cj]

        ici = []
        for r, (cx, cy) in enumerate(chips):
            cj = 2 * cx + cy
            snd_a[r] = own_a[cj].astype(wire)
            snd_b[r] = own_b[cj].astype(wire)
            ici += [copy(8 + r, snd_a.at[r], in_a.at[r], (cx, cy, c)), copy(11 + r, snd_b.at[r], in_b.at[r], (cx, cy, c))]
            ici[-2].start()
            ici[-1].start()
        tot_a = own_a[j]
        tot_b = own_b[j]
        for r in range(3):
            copy(8 + r, snd_a.at[r], in_a.at[r], me).wait_recv()
            copy(11 + r, snd_b.at[r], in_b.at[r], me).wait_recv()
            tot_a = tot_a + in_a[r].astype(F32)
            tot_b = tot_b + in_b[r].astype(F32)
        half_a = oa_ref.at[pl.ds(pl.multiple_of(c * HALF_A, 8), HALF_A), :]
        half_b = ob_ref.at[pl.ds(pl.multiple_of(c * HALF_B, 8), HALF_B), :]
        half_a[...] = tot_a
        half_b[...] = tot_b
        back = [copy(14, half_a, half_a, sibling), copy(15, half_b, half_b, sibling)]
        for cp in back:
            cp.start()

        copy(16, small_ref, all_small.at[dev ^ 1], me).wait_recv()
        for r, (cx, cy) in enumerate(chips):
            d = 4 * cx + 2 * cy + (1 - c)
            copy(20 + r, all_small.at[d], all_small.at[d], me).wait_recv()
        mine_small.wait()
        tot = all_small[0]
        for d in range(1, N_DEV):
            tot = tot + all_small[d]
        osmall_ref[...] = tot

        for r, (cx, cy) in enumerate(chips):
            cj = 2 * cx + cy
            copy(24 + r, sw_chips.at[cj], sw_chips.at[cj], me).wait_recv()
        tot_sw = sw_chips[0].astype(F32)
        for k in range(1, N_CHIPS):
            tot_sw = tot_sw + sw_chips[k].astype(F32)
        sw_rows(osw_ref, c)[...] = tot_sw
        sw_back = copy(27, sw_rows(osw_ref, c), sw_rows(osw_ref, c), sibling)
        sw_back.start()

        other_a = oa_ref.at[pl.ds(pl.multiple_of((1 - c) * HALF_A, 8), HALF_A), :]
        other_b = ob_ref.at[pl.ds(pl.multiple_of((1 - c) * HALF_B, 8), HALF_B), :]
        copy(14, other_a, other_a, me).wait_recv()
        copy(15, other_b, other_b, me).wait_recv()
        copy(27, sw_rows(osw_ref, 1 - c), sw_rows(osw_ref, 1 - c), me).wait_recv()
        for cp in sm_first + to_sib + sm_passed + ici + back + sw_ici + [sw_to_sib, sw_back]:
            cp.wait_send()

    hbm = pl.BlockSpec(memory_space=pl.ANY)
    vmem = pl.BlockSpec(memory_space=pltpu.VMEM)
    vm = pltpu.VMEM
    return pl.pallas_call(
        body, name="reduce", in_specs=[hbm, hbm, vmem, vmem], out_specs=[vmem, vmem, vmem, vmem],
        out_shape=[jax.ShapeDtypeStruct((W_IN_SHARD, D_MODEL), F32), jax.ShapeDtypeStruct((W_OUT_SHARD, D_MODEL), F32),
                   jax.ShapeDtypeStruct((SW_ROWS, 128), F32), jax.ShapeDtypeStruct((VEC_ROWS, 128), F32)],
        scratch_shapes=[vm((N_CHIPS, HALF_A, D_MODEL), F32), vm((N_CHIPS, HALF_B, D_MODEL), F32),
                        vm((N_CHIPS, HALF_A, D_MODEL), F32), vm((N_CHIPS, HALF_B, D_MODEL), F32),
                        vm((3, HALF_A, D_MODEL), wire), vm((3, HALF_B, D_MODEL), wire),
                        vm((3, HALF_A, D_MODEL), wire), vm((3, HALF_B, D_MODEL), wire),
                        vm((N_DEV, VEC_ROWS, 128), F32), vm((half_sw, 128), F32), vm((N_CHIPS, half_sw, 128), wire),
                        pltpu.SemaphoreType.DMA((28,)), pltpu.SemaphoreType.DMA((28,)), pltpu.SemaphoreType.DMA((9,))],
        compiler_params=pltpu.CompilerParams(vmem_limit_bytes=VMEM_LIMIT),
    )(ga, gb, sw, vec)


def _adamw(w, g, m, v):
    nm = ADAM_B1 * m + (1.0 - ADAM_B1) * g
    nv = ADAM_B2 * v + (1.0 - ADAM_B2) * (g * g)
    m_hat = nm / (1.0 - ADAM_B1 ** ADAM_STEP)
    v_hat = nv / (1.0 - ADAM_B2 ** ADAM_STEP)
    return -ADAM_LR * (m_hat / (jnp.sqrt(v_hat) + ADAM_EPS) + ADAM_WD * w), nm, nv


def _adamw_call(w, g, m, v, block_rows, name):
    rows, cols = w.shape

    def body(w_ref, g_ref, m_ref, v_ref, go_ref, d_ref, nm_ref, nv_ref):
        gg = g_ref[...]
        go_ref[...] = gg
        d_ref[...], nm_ref[...], nv_ref[...] = _adamw(w_ref[...], gg, m_ref[...], v_ref[...])

    spec = pl.BlockSpec((block_rows, cols), lambda i: (i, 0))
    return pl.pallas_call(
        body, name=name, grid=(rows // block_rows,), in_specs=[spec] * 4, out_specs=[spec] * 4,
        out_shape=[jax.ShapeDtypeStruct((rows, cols), F32)] * 4,
        compiler_params=pltpu.CompilerParams(dimension_semantics=("arbitrary",)),
    )(w, g, m, v)


_SMALL = (("norm_g", (1, D_MODEL), R_G1), ("b_in", (1, IN_W), R_BIN), ("attn_sinks", (1, 8), R_SINK),
          ("sgu_ln_g", (1, SGU_W), R_LNG), ("sgu_ln_b", (1, SGU_W), R_LNB), ("sgu_b", (N_SGU_HEADS, BLOCK), R_SGUB),
          ("b_out", (1, D_MODEL), R_BOUT), ("final_norm_g", (1, D_MODEL), R_G3))


def _adamw_small_call(sw_g, vec_g, sgu_w3, ws, ms, vs):
    n = len(_SMALL)

    def body(*refs):
        sw_ref, vec_ref = refs[0], refs[1]
        w3 = refs[2:5]
        w_refs, m_refs, v_refs = refs[5:5 + n], refs[5 + n:5 + 2 * n], refs[5 + 2 * n:5 + 3 * n]
        outs = refs[5 + 3 * n:]
        outs[0][...] = vec_ref[R_LOSS:R_LOSS + 1, 0:1]
        g = sw_ref[...]
        outs[1][...] = g
        outs[2][...], outs[3][...], outs[4][...] = _adamw(w3[0][...], g, w3[1][...], w3[2][...])
        for k, (_, shape, row) in enumerate(_SMALL):
            if shape[0] == 1 and shape[1] >= 128:
                g = jnp.concatenate([vec_ref[row + q:row + q + 1, :] for q in range(shape[1] // 128)], axis=1)
            else:
                g = vec_ref[row:row + shape[0], 0:shape[1]]
            o = outs[5 + 4 * k:9 + 4 * k]
            o[0][...] = g
            o[1][...], o[2][...], o[3][...] = _adamw(w_refs[k][...], g, m_refs[k][...], v_refs[k][...])

    vmem = pl.BlockSpec(memory_space=pltpu.VMEM)
    out_shape = [jax.ShapeDtypeStruct((1, 1), F32)] + [jax.ShapeDtypeStruct((SW_ROWS, 128), F32)] * 4
    for _, shape, _ in _SMALL:
        out_shape += [jax.ShapeDtypeStruct(shape, F32)] * 4
    args = [sw_g, vec_g, *sgu_w3, *ws, *ms, *vs]
    return pl.pallas_call(
        body, name="adamw_small", in_specs=[vmem] * len(args), out_specs=[vmem] * len(out_shape), out_shape=out_shape,
    )(*args)


def kernel(x, norm_g, w_in, b_in, attn_sinks, sgu_ln_g, sgu_ln_b, sgu_w, sgu_b, w_out, b_out, final_norm_g, loss_target, m_norm_g, m_w_in, m_b_in, m_attn_sinks, m_sgu_ln_g, m_sgu_ln_b, m_sgu_w, m_sgu_b, m_w_out, m_b_out, m_final_norm_g, v_norm_g, v_w_in, v_b_in, v_attn_sinks, v_sgu_ln_g, v_sgu_ln_b, v_sgu_w, v_sgu_b, v_w_out, v_b_out, v_final_norm_g):
    seq = x.shape[1]
    win_t, wout = _gather_call(w_in[0].T.astype(MXU_DTYPE), w_out[0].astype(MXU_DTYPE))
    bexp = jnp.repeat(sgu_b[0].T, SGU_W // N_SGU_HEADS, axis=1)
    dx, h, dproj, mixed, dout, sw, vec = _fused_call(
        x[0], loss_target[0], attn_sinks[0], norm_g, b_in, sgu_ln_g, sgu_ln_b, sgu_w[0], bexp, b_out,
        final_norm_g.reshape(1, D_MODEL), win_t, wout)
    ga, gb = _wgrad_call(h, dproj, mixed, dout)
    ga_t, g_w_out, sw, vec = _reduce_call(ga, gb, sw, vec)

    names = ["norm_g", "w_in", "b_in", "attn_sinks", "sgu_ln_g", "sgu_ln_b", "sgu_w", "sgu_b", "w_out", "b_out", "final_norm_g"]
    res = {}
    res["w_in"] = [a.T[None] for a in _adamw_call(w_in[0].T, ga_t, m_w_in[0].T, v_w_in[0].T, W_IN_SHARD // 4, "adamw_w_in")]
    res["w_out"] = [a[None] for a in _adamw_call(w_out[0], g_w_out, m_w_out[0], v_w_out[0], HALF_B, "adamw_w_out")]
    given = dict(norm_g=(norm_g, m_norm_g, v_norm_g), b_in=(b_in, m_b_in, v_b_in), attn_sinks=(attn_sinks, m_attn_sinks, v_attn_sinks),
                 sgu_ln_g=(sgu_ln_g, m_sgu_ln_g, v_sgu_ln_g), sgu_ln_b=(sgu_ln_b, m_sgu_ln_b, v_sgu_ln_b),
                 sgu_b=(sgu_b, m_sgu_b, v_sgu_b), b_out=(b_out, m_b_out, v_b_out),
                 final_norm_g=(final_norm_g, m_final_norm_g, v_final_norm_g))
    wmv = [[given[n][k].reshape(shape) for n, shape, _ in _SMALL] for k in range(3)]
    outs = _adamw_small_call(sw, vec, [a.reshape(SW_ROWS, BLOCK) for a in (sgu_w, m_sgu_w, v_sgu_w)], *wmv)
    loss = outs[0].reshape(())
    res["sgu_w"] = [a.reshape(sgu_w.shape) for a in outs[1:5]]
    for k, (n, _, _) in enumerate(_SMALL):
        res[n] = [a.reshape(given[n][0].shape) for a in outs[5 + 4 * k:9 + 4 * k]]

    return (loss, dx[None], *[res[n][0] for n in names], *[res[n][1] for n in names], *[res[n][2] for n in names],
            *[res[n][3] for n in names])
```

```python
import functools
import math

import jax
import jax.numpy as jnp
from jax import lax
from jax.experimental import pallas as pl
from jax.experimental.pallas import tpu as pltpu

F32 = jnp.float32
MXU_DTYPE = jnp.bfloat16

D_MODEL = 1024
HEAD_DIM = 64
ATTN_W = 512
SGU_W = 512
N_SGU_HEADS = 8
BLOCK = 128
IN_W = 2816
OFF_K, OFF_V, OFF_ZA, OFF_US, OFF_VS, OFF_ZS = 512, 640, 768, 1280, 1792, 2304
NORM_EPS = 1e-5
NEG_INF = -1e30
SCALE = HEAD_DIM ** -0.5
SQRT_HALF = math.sqrt(0.5)
INV_SQRT_2PI = 1.0 / math.sqrt(2.0 * math.pi)

N_CHIPS = 4
N_DEV = 8
W_IN_SHARD = IN_W // N_CHIPS
W_OUT_SHARD = D_MODEL // N_CHIPS
HALF_A = W_IN_SHARD // 2
HALF_B = W_OUT_SHARD // 2

TILE = 256
VMEM_LIMIT = 56 * 1024 * 1024

ADAM_LR, ADAM_B1, ADAM_B2, ADAM_EPS, ADAM_WD, ADAM_STEP = 0.001, 0.9, 0.999, 1e-08, 0.01, 10

SW_ROWS = N_SGU_HEADS * BLOCK
R_G1, R_BIN, R_SINK, R_LOSS, R_LNG, R_LNB, R_SGUB, R_BOUT, R_G3 = 0, 8, 32, 40, 48, 56, 64, 72, 80
VEC_ROWS = 88

MESH = pl.DeviceIdType.MESH


def _mm(a, b):
    return jnp.dot(a, b, preferred_element_type=F32)


def _mm_nt(a, b):
    return lax.dot_general(a, b, (((1,), (1,)), ((), ())), preferred_element_type=F32)


def _mm_tn(a, b):
    return lax.dot_general(a, b, (((0,), (0,)), ((), ())), preferred_element_type=F32)


def _sigmoid(z):
    return 1.0 / (1.0 + jnp.exp(-z))


def _norm_cdf(z):
    return 0.5 * (1.0 + lax.erf(z * SQRT_HALF))


def _norm_pdf(z):
    return jnp.exp(-0.5 * z * z) * INV_SQRT_2PI


def _rows8(v):
    r, n = v.shape
    return jnp.sum(v.reshape(r // 8, 8, n), axis=0)


def _mean_rows(v):
    return jnp.sum(v, axis=1, keepdims=True) * (1.0 / v.shape[1])


def _fused_call(x, tgt, sinks, g1, b_in, ln_g, ln_b, sgu_w, bexp, b_out, g3, win_t, wout):
    seq = x.shape[0]
    t = TILE
    nt = seq // t
    nb = t // BLOCK
    act = MXU_DTYPE

    def body(sinks_ref, x_ref, xh_ref, tgt_ref, g1_ref, bin_ref, lng_ref, lnb_ref, sguw_ref, bexp_ref, bout_ref, g3_ref,
             wint_hbm, wout_hbm,
             dx_ref, h_ref, dprojt_ref, mixedt_ref, dout_ref, sw_ref, vec_ref,
             dproj_ref, mixed_ref, wint_v, wout_v, wf_v, wb_v, q_s, kf_s, vf_s, k2_s, v2_s, gate_s, p_s, ps_s, o_s, u_s, mix_s, vhat_s, r2_s,
             cdfu_s, cdfv_s, r1_s, doutf_s, dmix_s, dkf_s, dvf_s, carryk_s, carryv_s,
             acc_bin, acc_g1, acc_bout, acc_g3, acc_lng, acc_lnb, acc_dws, acc_dbs, acc_sink, acc_loss, sems):
        i = pl.program_id(0)
        tile = nt - 1 - i
        lane128 = lax.broadcasted_iota(jnp.int32, (BLOCK, BLOCK), 1)
        lo = lane128 < HEAD_DIM

        @pl.when(i == 0)
        def _():
            cp_a = pltpu.make_async_copy(wint_hbm, wint_v, sems.at[0])
            cp_b = pltpu.make_async_copy(wout_hbm, wout_v, sems.at[1])
            cp_a.start()
            cp_b.start()
            for acc in (acc_bin, acc_g1, acc_bout, acc_g3, acc_lng, acc_lnb, acc_dws, acc_dbs, acc_sink, acc_loss,
                        carryk_s, carryv_s):
                acc[...] = jnp.zeros(acc.shape, F32)
            tril = lax.broadcasted_iota(jnp.int32, (BLOCK, BLOCK), 0) >= lane128
            for h in range(N_SGU_HEADS):
                w = jnp.where(tril, sguw_ref[h], 0.0)
                wf_v[h // 2, :, (h % 2) * BLOCK:(h % 2 + 1) * BLOCK] = w.astype(act)
                wb_v[h // 2, :, (h % 2) * BLOCK:(h % 2 + 1) * BLOCK] = w.T.astype(act)
            cp_a.wait()
            cp_b.wait()

        g1 = g1_ref[...]

        def rms(v):
            r = lax.rsqrt(_mean_rows(v * v) + NORM_EPS)
            return r, v * r

        for c in range(nb):
            rows = slice(c * BLOCK, (c + 1) * BLOCK)
            r, xn = rms(x_ref[rows, :])
            r1_s[rows, :] = r
            h_ref[rows, :] = (xn * g1).astype(act)
        _, xnh = rms(xh_ref[...])
        kvh = _mm_nt((xnh * g1).astype(act), wint_v[OFF_K:OFF_ZA, :]) + bin_ref[:, OFF_K:OFF_ZA]
        kf_s[0:BLOCK, :] = kvh[:, :BLOCK]
        vf_s[0:BLOCK, :] = kvh[:, BLOCK:]

        h = h_ref[...]
        q = _mm_nt(h, wint_v[0:OFF_K, :]) + bin_ref[:, 0:OFF_K]
        q_s[...] = (q * SCALE).astype(act)
        kv = _mm_nt(h, wint_v[OFF_K:OFF_ZA, :]) + bin_ref[:, OFF_K:OFF_ZA]
        kf_s[BLOCK:, :] = kv[:, :BLOCK]
        vf_s[BLOCK:, :] = kv[:, BLOCK:]
        for r in range(4):
            cols = slice(OFF_ZA + r * 512, OFF_ZA + (r + 1) * 512)
            gate_s[r] = _mm_nt(h, wint_v[cols, :]) + bin_ref[:, cols]

        lo_kv = lax.broadcasted_iota(jnp.int32, (t + BLOCK, BLOCK), 1) < HEAD_DIM
        for src, dst in ((kf_s, k2_s), (vf_s, v2_s)):
            v = src[...]
            vr = pltpu.roll(v, HEAD_DIM, 1)
            dst[0] = jnp.where(lo_kv, v, vr).astype(act)
            dst[1] = jnp.where(lo_kv, vr, v).astype(act)

        rowi = lax.broadcasted_iota(jnp.int32, (BLOCK, 2 * BLOCK), 0)
        colj = lax.broadcasted_iota(jnp.int32, (BLOCK, 2 * BLOCK), 1)
        in_band = (colj > rowi) & (colj <= rowi + BLOCK)
        row512 = lax.broadcasted_iota(jnp.int32, (4 * BLOCK, 1), 0)

        def stacked_q(b, g):
            parts = []
            for p in range(2):
                slab = q_s[b * BLOCK:(b + 1) * BLOCK, (2 * g + p) * BLOCK:(2 * g + p + 1) * BLOCK]
                parts += [jnp.where(lo, slab, jnp.zeros_like(slab)), jnp.where(lo, jnp.zeros_like(slab), slab)]
            return jnp.concatenate(parts, axis=0)

        def sink_col(g):
            s = [sinks_ref[4 * g + k] for k in range(4)]
            return jnp.where(row512 < BLOCK, s[0], jnp.where(row512 < 2 * BLOCK, s[1], jnp.where(row512 < 3 * BLOCK, s[2], s[3])))

        for b in range(nb):
            band = slice(b * BLOCK, (b + 2) * BLOCK)
            first_key = jnp.where(tile * nb + b > 0, 0, BLOCK)
            valid = in_band & (colj >= first_key)
            valid4 = jnp.concatenate([valid] * 4, axis=0)
            for g in range(2):
                s = _mm_nt(stacked_q(b, g), k2_s[g, band, :])
                s = jnp.where(valid4, s, NEG_INF)
                sk = sink_col(g)
                m = jnp.maximum(jnp.max(s, axis=1, keepdims=True), sk)
                p = jnp.exp(s - m)
                psk = jnp.exp(sk - m)
                inv = 1.0 / (jnp.sum(p, axis=1, keepdims=True) + psk)
                p = p * inv
                p_s[b * 2 + g] = p
                ps_s[b * 2 + g] = psk * inv
                o2 = _mm(p.astype(act), v2_s[g, band, :])
                for pr in range(2):
                    o_s[b * BLOCK:(b + 1) * BLOCK, (2 * g + pr) * BLOCK:(2 * g + pr + 1) * BLOCK] = jnp.where(
                        lo, o2[(2 * pr) * BLOCK:(2 * pr + 1) * BLOCK], o2[(2 * pr + 1) * BLOCK:(2 * pr + 2) * BLOCK])

        lng = lng_ref[...]
        lnb = lnb_ref[...]
        for c in range(nb):
            rows = slice(c * BLOCK, (c + 1) * BLOCK)
            za = gate_s[0, rows, :]
            mixed_ref[rows, 0:ATTN_W] = (o_s[rows, :] * (za * _sigmoid(za))).astype(act)
            us = gate_s[1, rows, :]
            vs = gate_s[2, rows, :]
            zs = gate_s[3, rows, :]
            cu = _norm_cdf(us)
            cv = _norm_cdf(vs)
            cdfu_s[rows, :] = cu
            cdfv_s[rows, :] = cv
            u = us * cu
            vg = vs * cv
            vc = vg - _mean_rows(vg)
            r2 = lax.rsqrt(_mean_rows(vc * vc) + NORM_EPS)
            vhat = vc * r2
            r2_s[rows, :] = r2
            vhat_s[rows, :] = vhat
            u_s[rows, :] = u
            vln = vhat * lng + lnb
            for p in range(4):
                cols = slice(p * BLOCK, (p + 1) * BLOCK)
                slab = vln[:, cols]
                rhs = jnp.concatenate([jnp.where(lo, slab, 0.0), jnp.where(lo, 0.0, slab)], axis=0).astype(act)
                mix_s[rows, cols] = _mm(wf_v[p], rhs) + bexp_ref[:, cols]
            mixed_ref[rows, ATTN_W:] = (u * mix_s[rows, :] * (zs * _sigmoid(zs))).astype(act)

        g3 = g3_ref[...]
        proj_o = _mm(mixed_ref[...], wout_v[...])
        for c in range(nb):
            rows = slice(c * BLOCK, (c + 1) * BLOCK)
            out = x_ref[rows, :] + proj_o[rows, :] + bout_ref[...]
            r3, on = rms(out)
            e = on * g3 - tgt_ref[rows, :]
            e2 = _rows8(e * e)
            acc_loss[...] += sum(e2[:, k * 128:(k + 1) * 128] for k in range(D_MODEL // 128)) * (0.5 / D_MODEL)
            dy = e * (1.0 / D_MODEL)
            acc_g3[...] += _rows8(dy * on)
            don = dy * g3
            dout = r3 * (don - on * _mean_rows(don * on))
            doutf_s[rows, :] = dout
            dout_ref[rows, :] = dout.astype(act)
            acc_bout[...] += _rows8(dout)
        dmix_s[...] = _mm_nt(dout_ref[...], wout_v[...])

        for c in range(nb):
            rows = slice(c * BLOCK, (c + 1) * BLOCK)
            dso = dmix_s[rows, ATTN_W:]
            u = u_s[rows, :]
            mix = mix_s[rows, :]
            zs = gate_s[3, rows, :]
            sg = _sigmoid(zs)
            sgs = zs * sg
            du = dso * mix * sgs
            dmx = dso * u * sgs
            dzs = dso * u * mix * (sg * (1.0 + zs * (1.0 - sg)))
            us = gate_s[1, rows, :]
            dus = du * (cdfu_s[rows, :] + us * _norm_pdf(us))
            vhat = vhat_s[rows, :]
            vln = (vhat * lng + lnb).astype(act)
            dvln_parts = []
            for p in range(4):
                cols = slice(p * BLOCK, (p + 1) * BLOCK)
                slab = dmx[:, cols]
                d_lo = jnp.where(lo, slab, 0.0).astype(act)
                d_hi = jnp.where(lo, 0.0, slab).astype(act)
                dvln_parts.append(_mm(wb_v[p], jnp.concatenate([d_lo, d_hi], axis=0)))
                acc_dws[(2 * p) * BLOCK:(2 * p + 1) * BLOCK, :] += _mm_nt(d_lo, vln[:, cols])
                acc_dws[(2 * p + 1) * BLOCK:(2 * p + 2) * BLOCK, :] += _mm_nt(d_hi, vln[:, cols])
            dvln = jnp.concatenate(dvln_parts, axis=1)
            acc_dbs[...] += dmx
            acc_lng[...] += _rows8(dvln * vhat)
            acc_lnb[...] += _rows8(dvln)
            dvhat = dvln * lng
            dvg = r2_s[rows, :] * (dvhat - _mean_rows(dvhat) - vhat * _mean_rows(dvhat * vhat))
            vs = gate_s[2, rows, :]
            dvs = dvg * (cdfv_s[rows, :] + vs * _norm_pdf(vs))
            for off, val in ((OFF_US, dus), (OFF_VS, dvs), (OFF_ZS, dzs)):
                dproj_ref[rows, off:off + 512] = val.astype(act)
                acc_bin[:, off:off + 512] += _rows8(val)

        dkf_s[...] = jnp.zeros(dkf_s.shape, F32)
        dvf_s[...] = jnp.zeros(dvf_s.shape, F32)
        for b in range(nb):
            rows = slice(b * BLOCK, (b + 1) * BLOCK)
            band = slice(b * BLOCK, (b + 2) * BLOCK)
            za = gate_s[0, rows, :]
            sg = _sigmoid(za)
            dao = dmix_s[rows, 0:ATTN_W]
            o = o_s[rows, :]
            do = dao * (za * sg)
            dza = dao * o * (sg * (1.0 + za * (1.0 - sg)))
            dproj_ref[rows, OFF_ZA:OFF_US] = dza.astype(act)
            acc_bin[:, OFF_ZA:OFF_US] += _rows8(dza)
            for g in range(2):
                do_parts, delta_parts = [], []
                for pr in range(2):
                    cols = slice((2 * g + pr) * BLOCK, (2 * g + pr + 1) * BLOCK)
                    d_pair = do[:, cols]
                    prod = d_pair * o[:, cols]
                    do_parts += [jnp.where(lo, d_pair, 0.0).astype(act), jnp.where(lo, 0.0, d_pair).astype(act)]
                    delta_parts += [jnp.sum(jnp.where(lo, prod, 0.0), axis=1, keepdims=True),
                                    jnp.sum(jnp.where(lo, 0.0, prod), axis=1, keepdims=True)]
                do_st = jnp.concatenate(do_parts, axis=0)
                delta = jnp.concatenate(delta_parts, axis=0)
                p = p_s[b * 2 + g]
                dp = _mm_nt(do_st, v2_s[g, band, :])
                ds = p * (dp - delta)
                sink_t = ps_s[b * 2 + g] * delta
                for k in range(4):
                    acc_sink[4 * g + k:4 * g + k + 1, :] += -jnp.sum(sink_t[k * BLOCK:(k + 1) * BLOCK], axis=0, keepdims=True)
                ds_a = ds.astype(act)
                dq2 = _mm(ds_a, k2_s[g, band, :]) * SCALE
                for pr in range(2):
                    cols = slice((2 * g + pr) * BLOCK, (2 * g + pr + 1) * BLOCK)
                    dq = jnp.where(lo, dq2[(2 * pr) * BLOCK:(2 * pr + 1) * BLOCK], dq2[(2 * pr + 1) * BLOCK:(2 * pr + 2) * BLOCK])
                    dproj_ref[rows, cols] = dq.astype(act)
                    acc_bin[:, cols] += _rows8(dq)
                lo_band = lax.broadcasted_iota(jnp.int32, (2 * BLOCK, BLOCK), 1) < HEAD_DIM
                mine = lo_band if g == 0 else jnp.logical_not(lo_band)
                for acc, lhs, rhs in ((dkf_s, ds_a, stacked_q(b, g)), (dvf_s, p.astype(act), do_st)):
                    d2 = _mm_tn(lhs, rhs)
                    full = d2 + pltpu.roll(d2, HEAD_DIM, 1)
                    acc[band, :] += jnp.where(mine, full, 0.0)
        for acc, carry, off in ((dkf_s, carryk_s, OFF_K), (dvf_s, carryv_s, OFF_V)):
            acc[t:t + BLOCK, :] += carry[...]
            carry[...] = acc[0:BLOCK, :]
            d = acc[BLOCK:, :]
            dproj_ref[:, off:off + BLOCK] = d.astype(act)
            acc_bin[:, off:off + BLOCK] += _rows8(d)

        dh = _mm(dproj_ref[...], wint_v[...])
        dprojt_ref[...] = dproj_ref[...].T
        mixedt_ref[...] = mixed_ref[...].T
        for c in range(nb):
            rows = slice(c * BLOCK, (c + 1) * BLOCK)
            r1 = r1_s[rows, :]
            xn = x_ref[rows, :] * r1
            dhc = dh[rows, :]
            acc_g1[...] += _rows8(dhc * xn)
            dxn = dhc * g1
            dx_ref[rows, :] = doutf_s[rows, :] + r1 * (dxn - xn * _mean_rows(dxn * xn))

        @pl.when(i == nt - 1)
        def _():
            tril = lax.broadcasted_iota(jnp.int32, (BLOCK, BLOCK), 0) >= lane128
            for hh in range(N_SGU_HEADS):
                rws = slice(hh * BLOCK, (hh + 1) * BLOCK)
                sw_ref[rws, :] = jnp.where(tril, acc_dws[rws, :], 0.0)
            vec_ref[...] = jnp.zeros((VEC_ROWS, 128), F32)

            def put(row0, acc):
                s = jnp.sum(acc[...], axis=0, keepdims=True)
                for k in range(acc.shape[1] // 128):
                    vec_ref[row0 + k:row0 + k + 1, :] = s[:, k * 128:(k + 1) * 128]

            put(R_G1, acc_g1)
            put(R_BIN, acc_bin)
            put(R_LNG, acc_lng)
            put(R_LNB, acc_lnb)
            put(R_BOUT, acc_bout)
            put(R_G3, acc_g3)
            vec_ref[R_SINK:R_SINK + 1, :] = jnp.sum(
                jnp.where(lax.broadcasted_iota(jnp.int32, (8, 128), 0) == lax.broadcasted_iota(jnp.int32, (8, 128), 1),
                          acc_sink[...], 0.0), axis=0, keepdims=True)
            vec_ref[R_LOSS:R_LOSS + 1, :] = jnp.zeros((1, 128), F32) + jnp.sum(acc_loss[...])
            dbs_t = acc_dbs[...].T
            vec_ref[R_SGUB:R_SGUB + 8, :] = jnp.sum(dbs_t.reshape(N_SGU_HEADS, SGU_W // N_SGU_HEADS, BLOCK), axis=1)

    full = lambda shape: pl.BlockSpec(shape, lambda i: (0,) * len(shape))
    tok = lambda w: pl.BlockSpec((t, w), lambda i: (nt - 1 - i, 0))
    in_specs = [
        pl.BlockSpec(memory_space=pltpu.SMEM),
        tok(D_MODEL),
        pl.BlockSpec((BLOCK, D_MODEL), lambda i: (jnp.maximum((nt - 1 - i) * nb - 1, 0), 0)),
        tok(D_MODEL),
        full((1, D_MODEL)), full((1, IN_W)), full((1, SGU_W)), full((1, SGU_W)),
        full((N_SGU_HEADS, BLOCK, BLOCK)), full((BLOCK, SGU_W)), full((1, D_MODEL)), full((1, D_MODEL)),
        pl.BlockSpec(memory_space=pl.ANY), pl.BlockSpec(memory_space=pl.ANY),
    ]
    out_shape = [
        jax.ShapeDtypeStruct((seq, D_MODEL), F32),
        jax.ShapeDtypeStruct((seq, D_MODEL), act),
        jax.ShapeDtypeStruct((IN_W, seq), act),
        jax.ShapeDtypeStruct((D_MODEL, seq), act),
        jax.ShapeDtypeStruct((seq, D_MODEL), act),
        jax.ShapeDtypeStruct((SW_ROWS, 128), F32),
        jax.ShapeDtypeStruct((VEC_ROWS, 128), F32),
    ]
    tok_t = lambda w: pl.BlockSpec((w, t), lambda i: (0, nt - 1 - i))
    out_specs = [tok(D_MODEL), tok(D_MODEL), tok_t(IN_W), tok_t(D_MODEL), tok(D_MODEL), full((SW_ROWS, 128)), full((VEC_ROWS, 128))]
    vm = pltpu.VMEM
    scratch = [
        vm((t, IN_W), act), vm((t, D_MODEL), act),
        vm((IN_W, D_MODEL), act), vm((D_MODEL, D_MODEL), act),
        vm((4, BLOCK, 2 * BLOCK), act), vm((4, BLOCK, 2 * BLOCK), act),
        vm((t, ATTN_W), act),
        vm((t + BLOCK, BLOCK), F32), vm((t + BLOCK, BLOCK), F32),
        vm((2, t + BLOCK, BLOCK), act), vm((2, t + BLOCK, BLOCK), act),
        vm((4, t, 512), F32),
        vm((2 * nb, 4 * BLOCK, 2 * BLOCK), F32), vm((2 * nb, 4 * BLOCK, 1), F32),
        vm((t, ATTN_W), F32), vm((t, SGU_W), F32), vm((t, SGU_W), F32), vm((t, SGU_W), F32), vm((t, 1), F32),
        vm((t, SGU_W), F32), vm((t, SGU_W), F32), vm((t, 1), F32),
        vm((t, D_MODEL), F32), vm((t, D_MODEL), F32),
        vm((t + BLOCK, BLOCK), F32), vm((t + BLOCK, BLOCK), F32), vm((BLOCK, BLOCK), F32), vm((BLOCK, BLOCK), F32),
        vm((8, IN_W), F32), vm((8, D_MODEL), F32), vm((8, D_MODEL), F32), vm((8, D_MODEL), F32),
        vm((8, SGU_W), F32), vm((8, SGU_W), F32), vm((N_SGU_HEADS * BLOCK, BLOCK), F32), vm((BLOCK, SGU_W), F32),
        vm((8, 128), F32), vm((8, 128), F32),
        pltpu.SemaphoreType.DMA((2,)),
    ]
    return pl.pallas_call(
        body, name="fused", grid=(nt,), in_specs=in_specs, out_specs=out_specs, out_shape=out_shape,
        scratch_shapes=scratch,
        compiler_params=pltpu.CompilerParams(dimension_semantics=("arbitrary",), vmem_limit_bytes=VMEM_LIMIT),
    )(sinks, x, x, tgt, g1, b_in, ln_g, ln_b, sgu_w, bexp, b_out, g3, win_t, wout)


def _place():
    x, y, c = lax.axis_index("x"), lax.axis_index("y"), lax.axis_index("c")
    chips = [(1 - x, y), (x, 1 - y), (1 - x, 1 - y)]
    return x, y, c, chips


def _gather_call(a_loc, b_loc):
    def body(a_ref, b_ref, ga_ref, gb_ref, send_sems, recv_sems, loc_sems):
        x, y, c, chips = _place()
        me, sibling = (x, y, c), (x, y, 1 - c)
        j = 2 * x + y

        def half(which, cj, hf):
            ref, shard = (ga_ref, W_IN_SHARD) if which == 0 else (gb_ref, W_OUT_SHARD)
            n = shard // 2
            return ref.at[pl.ds(pl.multiple_of(cj * shard + hf * n, 16), n), :]

        def copy(k, src, dst, to):
            return pltpu.make_async_remote_copy(src_ref=src, dst_ref=dst, send_sem=send_sems.at[k], recv_sem=recv_sems.at[k],
                                                device_id=to, device_id_type=MESH)

        own = [pltpu.make_async_copy(a_ref, ga_ref.at[pl.ds(pl.multiple_of(j * W_IN_SHARD, 16), W_IN_SHARD), :], loc_sems.at[0]),
               pltpu.make_async_copy(b_ref, gb_ref.at[pl.ds(pl.multiple_of(j * W_OUT_SHARD, 16), W_OUT_SHARD), :], loc_sems.at[1])]
        for cp in own:
            cp.start()
        srcs = [a_ref.at[pl.ds(pl.multiple_of(c * HALF_A, 16), HALF_A), :], b_ref.at[pl.ds(pl.multiple_of(c * HALF_B, 16), HALF_B), :]]
        first = [copy(2 * r + w, srcs[w], half(w, j, c), (*chip, c)) for r, chip in enumerate(chips) for w in range(2)]
        for cp in first:
            cp.start()
        passed = []
        for r, (cx, cy) in enumerate(chips):
            cj = 2 * cx + cy
            for w in range(2):
                copy(2 * r + w, half(w, cj, c), half(w, cj, c), me).wait_recv()
                fwd = copy(6 + 2 * r + w, half(w, cj, c), half(w, cj, c), sibling)
                fwd.start()
                passed.append(fwd)
        for r, (cx, cy) in enumerate(chips):
            cj = 2 * cx + cy
            for w in range(2):
                copy(6 + 2 * r + w, half(w, cj, 1 - c), half(w, cj, 1 - c), me).wait_recv()
        for cp in first + passed:
            cp.wait_send()
        for cp in own:
            cp.wait()

    hbm = pl.BlockSpec(memory_space=pl.ANY)
    return pl.pallas_call(
        body, name="gather", in_specs=[hbm, hbm], out_specs=[hbm, hbm],
        out_shape=[jax.ShapeDtypeStruct((IN_W, D_MODEL), a_loc.dtype), jax.ShapeDtypeStruct((D_MODEL, D_MODEL), b_loc.dtype)],
        scratch_shapes=[pltpu.SemaphoreType.DMA((12,)), pltpu.SemaphoreType.DMA((12,)), pltpu.SemaphoreType.DMA((2,))],
    )(a_loc, b_loc)


def _wgrad_reduce_call(dproj_t, h, mixed_t, dout, sw, vec):
    wire = jnp.bfloat16
    half_sw = SW_ROWS // 2
    seq = h.shape[0]
    tk = min(1024, seq)
    nk = seq // tk
    x, y = lax.axis_index("x"), lax.axis_index("y")
    order = jnp.stack([2 * (1 - x) + (1 - y), 2 * (1 - x) + y, 2 * x + (1 - y), 2 * x + y]).astype(jnp.int32)

    def body(order_ref, dpt_ref, h_ref, mxt_ref, dout_ref, sw_ref, small_ref, oa_ref, ob_ref, osw_ref, osmall_ref,
             acc_a, acc_b, sib_a, sib_b, snd_a, snd_b, in_a, in_b, fin_a, fin_b, all_small, sw_sib, sw_chips, sw_fin,
             send_sems, recv_sems, loc_sem):
        x, y, c = lax.axis_index("x"), lax.axis_index("y"), lax.axis_index("c")
        me, sibling = (x, y, c), (x, y, 1 - c)
        steps = [(1 - x, 1 - y), (1 - x, y), (x, 1 - y)]
        j = 2 * x + y
        dev = 4 * x + 2 * y + c
        b, k = pl.program_id(0), pl.program_id(1)

        def copy(n, src, dst, to):
            return pltpu.make_async_remote_copy(src_ref=src, dst_ref=dst, send_sem=send_sems.at[n], recv_sem=recv_sems.at[n],
                                                device_id=to, device_id_type=MESH)

        def half(ref, slot, hf, n):
            return ref.at[slot, pl.ds(pl.multiple_of(hf * n, 8), n), :]

        def sw_rows(ref, hf):
            return ref.at[pl.ds(pl.multiple_of(hf * half_sw, 8), half_sw), :]

        sm_first = [copy(16, small_ref, all_small.at[dev], sibling)]
        sm_first += [copy(17 + r, small_ref, all_small.at[dev], (*chip, c)) for r, chip in enumerate(steps)]
        sw_to_sib = copy(23, sw_rows(sw_ref, 1 - c), sw_sib, sibling)

        @pl.when((b == 0) & (k == 0))
        def _():
            all_small[dev] = small_ref[...]
            for cp in sm_first + [sw_to_sib]:
                cp.start()

        pa = _mm(dpt_ref[...], h_ref[...])
        pb = _mm(mxt_ref[...], dout_ref[...])
        slot = b % 2

        @pl.when(k == 0)
        def _():
            acc_a[slot] = pa
            acc_b[slot] = pb

        @pl.when(k != 0)
        def _():
            acc_a[slot] += pa
            acc_b[slot] += pb

        def to_sibling(s):
            return [copy(s, half(acc_a, s % 2, 1 - c, HALF_A), sib_a.at[s], sibling),
                    copy(4 + s, half(acc_b, s % 2, 1 - c, HALF_B), sib_b.at[s], sibling)]

        def chip_partial(s):
            for cp in to_sibling(s):
                cp.wait_send()
            copy(s, sib_a.at[s], sib_a.at[s], me).wait_recv()
            copy(4 + s, sib_b.at[s], sib_b.at[s], me).wait_recv()
            return (half(acc_a, s % 2, c, HALF_A)[...] + sib_a[s], half(acc_b, s % 2, c, HALF_B)[...] + sib_b[s])

        def to_owner(s):
            return [copy(8 + s, snd_a.at[s], in_a.at[s], (*steps[s], c)), copy(11 + s, snd_b.at[s], in_b.at[s], (*steps[s], c))]

        for s in range(N_CHIPS):
            if s >= 1:
                @pl.when((b == s) & (k == 0))
                def _():
                    ta, tb = chip_partial(s - 1)
                    snd_a[s - 1] = ta.astype(wire)
                    snd_b[s - 1] = tb.astype(wire)
                    for cp in to_owner(s - 1):
                        cp.start()

            @pl.when((b == s) & (k == nk - 1))
            def _():
                for cp in to_sibling(s):
                    cp.start()

        @pl.when((b == N_CHIPS - 1) & (k == nk - 1))
        def _():
            sm_passed = []
            for r, (cx, cy) in enumerate(steps):
                d = 4 * cx + 2 * cy + c
                copy(17 + r, all_small.at[d], all_small.at[d], me).wait_recv()
                fwd = copy(20 + r, all_small.at[d], all_small.at[d], sibling)
                fwd.start()
                sm_passed.append(fwd)
            copy(23, sw_sib, sw_sib, me).wait_recv()
            sw_chips[j] = (sw_rows(sw_ref, c)[...] + sw_sib[...]).astype(wire)
            sw_ici = [copy(24 + r, sw_chips.at[j], sw_chips.at[j], (*chip, c)) for r, chip in enumerate(steps)]
            for cp in sw_ici:
                cp.start()

            tot_a, tot_b = chip_partial(N_CHIPS - 1)
            for s in range(3):
                copy(8 + s, in_a.at[s], in_a.at[s], me).wait_recv()
                copy(11 + s, in_b.at[s], in_b.at[s], me).wait_recv()
                tot_a = tot_a + in_a[s].astype(F32)
                tot_b = tot_b + in_b[s].astype(F32)
            mine_a = oa_ref.at[pl.ds(pl.multiple_of(c * HALF_A, 8), HALF_A), :]
            mine_b = ob_ref.at[pl.ds(pl.multiple_of(c * HALF_B, 8), HALF_B), :]
            mine_a[...] = tot_a
            mine_b[...] = tot_b
            back = [copy(14, mine_a, fin_a, sibling), copy(15, mine_b, fin_b, sibling)]
            for cp in back:
                cp.start()

            copy(16, small_ref, all_small.at[dev ^ 1], me).wait_recv()
            for r, (cx, cy) in enumerate(steps):
                d = 4 * cx + 2 * cy + (1 - c)
                copy(20 + r, all_small.at[d], all_small.at[d], me).wait_recv()
            tot = all_small[0]
            for d in range(1, N_DEV):
                tot = tot + all_small[d]
            osmall_ref[...] = tot

            for r, (cx, cy) in enumerate(steps):
                cj = 2 * cx + cy
                copy(24 + r, sw_chips.at[cj], sw_chips.at[cj], me).wait_recv()
            tot_sw = sw_chips[0].astype(F32)
            for q in range(1, N_CHIPS):
                tot_sw = tot_sw + sw_chips[q].astype(F32)
            sw_rows(osw_ref, c)[...] = tot_sw
            sw_back = copy(27, sw_rows(osw_ref, c), sw_fin, sibling)
            sw_back.start()

            copy(14, fin_a, fin_a, me).wait_recv()
            copy(15, fin_b, fin_b, me).wait_recv()
            copy(27, sw_fin, sw_fin, me).wait_recv()
            oa_ref[pl.ds(pl.multiple_of((1 - c) * HALF_A, 8), HALF_A), :] = fin_a[...]
            ob_ref[pl.ds(pl.multiple_of((1 - c) * HALF_B, 8), HALF_B), :] = fin_b[...]
            sw_rows(osw_ref, 1 - c)[...] = sw_fin[...]
            sends = sm_first + sm_passed + sw_ici + back + [sw_to_sib, sw_back]
            for s in range(3):
                sends += to_owner(s)
            for cp in sends:
                cp.wait_send()

    vmem = pl.BlockSpec(memory_space=pltpu.VMEM)
    vm = pltpu.VMEM
    grid_spec = pltpu.PrefetchScalarGridSpec(
        num_scalar_prefetch=1, grid=(N_CHIPS, nk),
        in_specs=[pl.BlockSpec((W_IN_SHARD, tk), lambda b, k, o: (o[b], k)), pl.BlockSpec((tk, D_MODEL), lambda b, k, o: (k, 0)),
                  pl.BlockSpec((W_OUT_SHARD, tk), lambda b, k, o: (o[b], k)), pl.BlockSpec((tk, D_MODEL), lambda b, k, o: (k, 0)),
                  vmem, vmem],
        out_specs=[vmem, vmem, vmem, vmem],
        scratch_shapes=[vm((2, W_IN_SHARD, D_MODEL), F32), vm((2, W_OUT_SHARD, D_MODEL), F32),
                        vm((N_CHIPS, HALF_A, D_MODEL), F32), vm((N_CHIPS, HALF_B, D_MODEL), F32),
                        vm((3, HALF_A, D_MODEL), wire), vm((3, HALF_B, D_MODEL), wire),
                        vm((3, HALF_A, D_MODEL), wire), vm((3, HALF_B, D_MODEL), wire),
                        vm((HALF_A, D_MODEL), F32), vm((HALF_B, D_MODEL), F32),
                        vm((N_DEV, VEC_ROWS, 128), F32), vm((half_sw, 128), F32), vm((N_CHIPS, half_sw, 128), wire),
                        vm((half_sw, 128), F32),
                        pltpu.SemaphoreType.DMA((28,)), pltpu.SemaphoreType.DMA((28,)), pltpu.SemaphoreType.DMA((1,))])
    return pl.pallas_call(
        body, name="wgrad_reduce", grid_spec=grid_spec,
        out_shape=[jax.ShapeDtypeStruct((W_IN_SHARD, D_MODEL), F32), jax.ShapeDtypeStruct((W_OUT_SHARD, D_MODEL), F32),
                   jax.ShapeDtypeStruct((SW_ROWS, 128), F32), jax.ShapeDtypeStruct((VEC_ROWS, 128), F32)],
        compiler_params=pltpu.CompilerParams(dimension_semantics=("arbitrary", "arbitrary"), vmem_limit_bytes=VMEM_LIMIT),
    )(order, dproj_t, h, mixed_t, dout, sw, vec)


def _adamw(w, g, m, v):
    nm = ADAM_B1 * m + (1.0 - ADAM_B1) * g
    nv = ADAM_B2 * v + (1.0 - ADAM_B2) * (g * g)
    m_hat = nm / (1.0 - ADAM_B1 ** ADAM_STEP)
    v_hat = nv / (1.0 - ADAM_B2 ** ADAM_STEP)
    return -ADAM_LR * (m_hat / (jnp.sqrt(v_hat) + ADAM_EPS) + ADAM_WD * w), nm, nv


def _adamw_call(w, g, m, v, block_rows, name):
    rows, cols = w.shape

    def body(w_ref, g_ref, m_ref, v_ref, go_ref, d_ref, nm_ref, nv_ref):
        gg = g_ref[...]
        go_ref[...] = gg
        d_ref[...], nm_ref[...], nv_ref[...] = _adamw(w_ref[...], gg, m_ref[...], v_ref[...])

    spec = pl.BlockSpec((block_rows, cols), lambda i: (i, 0))
    return pl.pallas_call(
        body, name=name, grid=(rows // block_rows,), in_specs=[spec] * 4, out_specs=[spec] * 4,
        out_shape=[jax.ShapeDtypeStruct((rows, cols), F32)] * 4,
        compiler_params=pltpu.CompilerParams(dimension_semantics=("arbitrary",)),
    )(w, g, m, v)


_SMALL = (("norm_g", (1, D_MODEL), R_G1), ("b_in", (1, IN_W), R_BIN), ("attn_sinks", (1, 8), R_SINK),
          ("sgu_ln_g", (1, SGU_W), R_LNG), ("sgu_ln_b", (1, SGU_W), R_LNB), ("sgu_b", (N_SGU_HEADS, BLOCK), R_SGUB),
          ("b_out", (1, D_MODEL), R_BOUT), ("final_norm_g", (1, D_MODEL), R_G3))


def _adamw_small_call(sw_g, vec_g, sgu_w3, ws, ms, vs):
    n = len(_SMALL)

    def body(*refs):
        sw_ref, vec_ref = refs[0], refs[1]
        w3 = refs[2:5]
        w_refs, m_refs, v_refs = refs[5:5 + n], refs[5 + n:5 + 2 * n], refs[5 + 2 * n:5 + 3 * n]
        outs = refs[5 + 3 * n:]
        outs[0][...] = vec_ref[R_LOSS:R_LOSS + 1, 0:1]
        g = sw_ref[...]
        outs[1][...] = g
        outs[2][...], outs[3][...], outs[4][...] = _adamw(w3[0][...], g, w3[1][...], w3[2][...])
        for k, (_, shape, row) in enumerate(_SMALL):
            if shape[0] == 1 and shape[1] >= 128:
                g = jnp.concatenate([vec_ref[row + q:row + q + 1, :] for q in range(shape[1] // 128)], axis=1)
            else:
                g = vec_ref[row:row + shape[0], 0:shape[1]]
            o = outs[5 + 4 * k:9 + 4 * k]
            o[0][...] = g
            o[1][...], o[2][...], o[3][...] = _adamw(w_refs[k][...], g, m_refs[k][...], v_refs[k][...])

    vmem = pl.BlockSpec(memory_space=pltpu.VMEM)
    out_shape = [jax.ShapeDtypeStruct((1, 1), F32)] + [jax.ShapeDtypeStruct((SW_ROWS, 128), F32)] * 4
    for _, shape, _ in _SMALL:
        out_shape += [jax.ShapeDtypeStruct(shape, F32)] * 4
    args = [sw_g, vec_g, *sgu_w3, *ws, *ms, *vs]
    return pl.pallas_call(
        body, name="adamw_small", in_specs=[vmem] * len(args), out_specs=[vmem] * len(out_shape), out_shape=out_shape,
    )(*args)


def kernel(x, norm_g, w_in, b_in, attn_sinks, sgu_ln_g, sgu_ln_b, sgu_w, sgu_b, w_out, b_out, final_norm_g, loss_target, m_norm_g, m_w_in, m_b_in, m_attn_sinks, m_sgu_ln_g, m_sgu_ln_b, m_sgu_w, m_sgu_b, m_w_out, m_b_out, m_final_norm_g, v_norm_g, v_w_in, v_b_in, v_attn_sinks, v_sgu_ln_g, v_sgu_ln_b, v_sgu_w, v_sgu_b, v_w_out, v_b_out, v_final_norm_g):
    seq = x.shape[1]
    win_t, wout = _gather_call(w_in[0].T.astype(MXU_DTYPE), w_out[0].astype(MXU_DTYPE))
    bexp = jnp.repeat(sgu_b[0].T, SGU_W // N_SGU_HEADS, axis=1)
    dx, h, dproj_t, mixed_t, dout, sw, vec = _fused_call(
        x[0], loss_target[0], attn_sinks[0], norm_g, b_in, sgu_ln_g, sgu_ln_b, sgu_w[0], bexp, b_out,
        final_norm_g.reshape(1, D_MODEL), win_t, wout)
    ga_t, g_w_out, sw, vec = _wgrad_reduce_call(dproj_t, h, mixed_t, dout, sw, vec)

    names = ["norm_g", "w_in", "b_in", "attn_sinks", "sgu_ln_g", "sgu_ln_b", "sgu_w", "sgu_b", "w_out", "b_out", "final_norm_g"]
    res = {}
    res["w_in"] = [a.T[None] for a in _adamw_call(w_in[0].T, ga_t, m_w_in[0].T, v_w_in[0].T, W_IN_SHARD // 4, "adamw_w_in")]
    res["w_out"] = [a[None] for a in _adamw_call(w_out[0], g_w_out, m_w_out[0], v_w_out[0], HALF_B, "adamw_w_out")]
    given = dict(norm_g=(norm_g, m_norm_g, v_norm_g), b_in=(b_in, m_b_in, v_b_in), attn_sinks=(attn_sinks, m_attn_sinks, v_attn_sinks),
                 sgu_ln_g=(sgu_ln_g, m_sgu_ln_g, v_sgu_ln_g), sgu_ln_b=(sgu_ln_b, m_sgu_ln_b, v_sgu_ln_b),
                 sgu_b=(sgu_b, m_sgu_b, v_sgu_b), b_out=(b_out, m_b_out, v_b_out),
                 final_norm_g=(final_norm_g, m_final_norm_g, v_final_norm_g))
    wmv = [[given[n][k].reshape(shape) for n, shape, _ in _SMALL] for k in range(3)]
    outs = _adamw_small_call(sw, vec, [a.reshape(SW_ROWS, BLOCK) for a in (sgu_w, m_sgu_w, v_sgu_w)], *wmv)
    loss = outs[0].reshape(())
    res["sgu_w"] = [a.reshape(sgu_w.shape) for a in outs[1:5]]
    for k, (n, _, _) in enumerate(_SMALL):
        res[n] = [a.reshape(given[n][0].shape) for a in outs[5 + 4 * k:9 + 4 * k]]

    return (loss, dx[None], *[res[n][0] for n in names], *[res[n][1] for n in names], *[res[n][2] for n in names],
            *[res[n][3] for n in names])
```

```python
import functools
import math

import jax
import jax.numpy as jnp
from jax import lax
from jax.experimental import pallas as pl
from jax.experimental.pallas import tpu as pltpu

F32 = jnp.float32
MXU_DTYPE = jnp.bfloat16

D_MODEL = 1024
HEAD_DIM = 64
ATTN_W = 512
SGU_W = 512
N_SGU_HEADS = 8
BLOCK = 128
IN_W = 2816
OFF_K, OFF_V, OFF_ZA, OFF_US, OFF_VS, OFF_ZS = 512, 640, 768, 1280, 1792, 2304
NORM_EPS = 1e-5
NEG_INF = -1e30
SCALE = HEAD_DIM ** -0.5
SQRT_HALF = math.sqrt(0.5)
INV_SQRT_2PI = 1.0 / math.sqrt(2.0 * math.pi)

N_CHIPS = 4
N_DEV = 8
W_IN_SHARD = IN_W // N_CHIPS
W_OUT_SHARD = D_MODEL // N_CHIPS
HALF_A = W_IN_SHARD // 2
HALF_B = W_OUT_SHARD // 2

TILE = 256
ROW_CHUNK = 32
VMEM_LIMIT = 56 * 1024 * 1024

ADAM_LR, ADAM_B1, ADAM_B2, ADAM_EPS, ADAM_WD, ADAM_STEP = 0.001, 0.9, 0.999, 1e-08, 0.01, 10

SW_ROWS = N_SGU_HEADS * BLOCK
R_G1, R_BIN, R_SINK, R_LOSS, R_LNG, R_LNB, R_SGUB, R_BOUT, R_G3 = 0, 8, 32, 40, 48, 56, 64, 72, 80
VEC_ROWS = 88

MESH = pl.DeviceIdType.MESH


def _mm(a, b):
    return jnp.dot(a, b, preferred_element_type=F32)


def _mm_nt(a, b):
    return lax.dot_general(a, b, (((1,), (1,)), ((), ())), preferred_element_type=F32)


def _mm_tn(a, b):
    return lax.dot_general(a, b, (((0,), (0,)), ((), ())), preferred_element_type=F32)


def _sigmoid(z):
    return 1.0 / (1.0 + jnp.exp(-z))


def _norm_cdf(z):
    return 0.5 * (1.0 + lax.erf(z * SQRT_HALF))


def _norm_pdf(z):
    return jnp.exp(-0.5 * z * z) * INV_SQRT_2PI


def _rows8(v):
    r, n = v.shape
    return jnp.sum(v.reshape(r // 8, 8, n), axis=0)


def _mean_rows(v):
    return jnp.sum(v, axis=1, keepdims=True) * (1.0 / v.shape[1])


def _fused_call(x, tgt, sinks, g1, b_in, ln_g, ln_b, sgu_w, bexp, b_out, g3, win_t, wout):
    seq = x.shape[0]
    t = TILE
    nt = seq // t
    nb = t // BLOCK
    act = MXU_DTYPE

    def body(sinks_ref, x_ref, xh_ref, tgt_ref, g1_ref, bin_ref, lng_ref, lnb_ref, sguw_ref, bexp_ref, bout_ref, g3_ref,
             wint_hbm, wout_hbm,
             dx_ref, h_ref, dprojt_ref, mixedt_ref, dout_ref, sw_ref, vec_ref,
             dproj_ref, mixed_ref, wint_v, wout_v, wf_v, wb_v, q_s, kf_s, vf_s, k2_s, v2_s, gate_s, p_s, ps_s, o_s, u_s, mix_s, vhat_s, r2_s,
             cdfu_s, cdfv_s, r1_s, doutf_s, dmix_s, dkf_s, dvf_s, carryk_s, carryv_s,
             hh_s, bias_s, pb_s, rhs_s, vlnb_s, dvln_s, dos_s, delta_s, dsb_s,
             acc_bin, acc_g1, acc_bout, acc_g3, acc_lng, acc_lnb, acc_dws, acc_dbs, acc_sink, acc_loss, sems):
        i = pl.program_id(0)
        tile = nt - 1 - i
        lane128 = lax.broadcasted_iota(jnp.int32, (BLOCK, BLOCK), 1)
        lo = lane128 < HEAD_DIM

        @pl.when(i == 0)
        def _():
            cp_a = pltpu.make_async_copy(wint_hbm, wint_v, sems.at[0])
            cp_b = pltpu.make_async_copy(wout_hbm, wout_v, sems.at[1])
            cp_a.start()
            cp_b.start()
            for acc in (acc_bin, acc_g1, acc_bout, acc_g3, acc_lng, acc_lnb, acc_dws, acc_dbs, acc_sink, acc_loss,
                        carryk_s, carryv_s):
                acc[...] = jnp.zeros(acc.shape, F32)
            tril = lax.broadcasted_iota(jnp.int32, (BLOCK, BLOCK), 0) >= lane128
            for h in range(N_SGU_HEADS):
                w = jnp.where(tril, sguw_ref[h], 0.0)
                wf_v[h // 2, :, (h % 2) * BLOCK:(h % 2 + 1) * BLOCK] = w.astype(act)
                wb_v[h // 2, :, (h % 2) * BLOCK:(h % 2 + 1) * BLOCK] = w.T.astype(act)
            cp_a.wait()
            cp_b.wait()

        g1 = g1_ref[...]
        lo_rc = lax.broadcasted_iota(jnp.int32, (ROW_CHUNK, BLOCK), 1) < HEAD_DIM

        def rms(v):
            r = lax.rsqrt(_mean_rows(v * v) + NORM_EPS)
            return r, v * r

        def chunks(n, step=ROW_CHUNK):
            return [slice(r0, r0 + step) for r0 in range(0, n, step)]

        for rows in chunks(t):
            r, xn = rms(x_ref[rows, :])
            r1_s[rows, :] = r
            h_ref[rows, :] = (xn * g1).astype(act)
        for rows in chunks(BLOCK):
            _, xnh = rms(xh_ref[rows, :])
            hh_s[rows, :] = (xnh * g1).astype(act)
        kvh = _mm_nt(hh_s[...], wint_v[OFF_K:OFF_ZA, :]) + bin_ref[:, OFF_K:OFF_ZA]
        kf_s[0:BLOCK, :] = kvh[:, :BLOCK]
        vf_s[0:BLOCK, :] = kvh[:, BLOCK:]

        h = h_ref[...]
        q = _mm_nt(h, wint_v[0:OFF_K, :]) + bin_ref[:, 0:OFF_K]
        q_s[...] = (q * SCALE).astype(act)
        kv = _mm_nt(h, wint_v[OFF_K:OFF_ZA, :]) + bin_ref[:, OFF_K:OFF_ZA]
        kf_s[BLOCK:, :] = kv[:, :BLOCK]
        vf_s[BLOCK:, :] = kv[:, BLOCK:]
        for r in range(4):
            cols = slice(OFF_ZA + r * 512, OFF_ZA + (r + 1) * 512)
            gate_s[r] = _mm_nt(h, wint_v[cols, :]) + bin_ref[:, cols]

        lo_kv = lax.broadcasted_iota(jnp.int32, (t + BLOCK, BLOCK), 1) < HEAD_DIM
        for src, dst in ((kf_s, k2_s), (vf_s, v2_s)):
            v = src[...]
            vr = pltpu.roll(v, HEAD_DIM, 1)
            dst[0] = jnp.where(lo_kv, v, vr).astype(act)
            dst[1] = jnp.where(lo_kv, vr, v).astype(act)

        rowi = lax.broadcasted_iota(jnp.int32, (BLOCK, 2 * BLOCK), 0)
        colj = lax.broadcasted_iota(jnp.int32, (BLOCK, 2 * BLOCK), 1)
        in_band = (colj > rowi) & (colj <= rowi + BLOCK)

        def stacked_q(b, g):
            parts = []
            for p in range(2):
                slab = q_s[b * BLOCK:(b + 1) * BLOCK, (2 * g + p) * BLOCK:(2 * g + p + 1) * BLOCK]
                parts += [jnp.where(lo, slab, jnp.zeros_like(slab)), jnp.where(lo, jnp.zeros_like(slab), slab)]
            return jnp.concatenate(parts, axis=0)

        for b in range(nb):
            band = slice(b * BLOCK, (b + 2) * BLOCK)
            first_key = jnp.where(tile * nb + b > 0, 0, BLOCK)
            bias_s[...] = jnp.where(in_band & (colj >= first_key), 0.0, NEG_INF)
            for g in range(2):
                s_all = _mm_nt(stacked_q(b, g), k2_s[g, band, :])
                for hh in range(4):
                    sk = sinks_ref[4 * g + hh]
                    for rc in chunks(BLOCK, 2 * ROW_CHUNK):
                        rows = slice(hh * BLOCK + rc.start, hh * BLOCK + rc.stop)
                        s = s_all[rows, :] + bias_s[rc, :]
                        m = jnp.maximum(jnp.max(s, axis=1, keepdims=True), sk)
                        p = jnp.exp(s - m)
                        psk = jnp.exp(sk - m)
                        inv = 1.0 / (jnp.sum(p, axis=1, keepdims=True) + psk)
                        p = p * inv
                        p_s[b * 2 + g, rows, :] = p
                        pb_s[b * 2 + g, rows, :] = p.astype(act)
                        ps_s[b * 2 + g, rows, :] = psk * inv
                o2 = _mm(pb_s[b * 2 + g], v2_s[g, band, :])
                for pr in range(2):
                    o_s[b * BLOCK:(b + 1) * BLOCK, (2 * g + pr) * BLOCK:(2 * g + pr + 1) * BLOCK] = jnp.where(
                        lo, o2[(2 * pr) * BLOCK:(2 * pr + 1) * BLOCK], o2[(2 * pr + 1) * BLOCK:(2 * pr + 2) * BLOCK])

        lng = lng_ref[...]
        lnb = lnb_ref[...]
        def split_pairs(val, brow):
            for p in range(4):
                slab = val[:, p * BLOCK:(p + 1) * BLOCK]
                rhs_s[p, brow, :] = jnp.where(lo_rc, slab, 0.0).astype(act)
                rhs_s[p, BLOCK + brow.start:BLOCK + brow.stop, :] = jnp.where(lo_rc, 0.0, slab).astype(act)

        for c in range(nb):
            blk = slice(c * BLOCK, (c + 1) * BLOCK)
            for brow in chunks(BLOCK):
                rows = slice(c * BLOCK + brow.start, c * BLOCK + brow.stop)
                za = gate_s[0, rows, :]
                mixed_ref[rows, 0:ATTN_W] = (o_s[rows, :] * (za * _sigmoid(za))).astype(act)
                us = gate_s[1, rows, :]
                vs = gate_s[2, rows, :]
                cu = _norm_cdf(us)
                cv = _norm_cdf(vs)
                cdfu_s[rows, :] = cu
                cdfv_s[rows, :] = cv
                u_s[rows, :] = us * cu
                vg = vs * cv
                vc = vg - _mean_rows(vg)
                r2 = lax.rsqrt(_mean_rows(vc * vc) + NORM_EPS)
                vhat = vc * r2
                r2_s[rows, :] = r2
                vhat_s[rows, :] = vhat
                split_pairs(vhat * lng + lnb, brow)
            for p in range(4):
                cols = slice(p * BLOCK, (p + 1) * BLOCK)
                mix_s[blk, cols] = _mm(wf_v[p], rhs_s[p]) + bexp_ref[:, cols]
            for brow in chunks(BLOCK):
                rows = slice(c * BLOCK + brow.start, c * BLOCK + brow.stop)
                zs = gate_s[3, rows, :]
                mixed_ref[rows, ATTN_W:] = (u_s[rows, :] * mix_s[rows, :] * (zs * _sigmoid(zs))).astype(act)

        g3 = g3_ref[...]
        proj_o = _mm(mixed_ref[...], wout_v[...])
        for rows in chunks(t):
            out = x_ref[rows, :] + proj_o[rows, :] + bout_ref[...]
            r3, on = rms(out)
            e = on * g3 - tgt_ref[rows, :]
            e2 = _rows8(e * e)
            acc_loss[...] += sum(e2[:, k * 128:(k + 1) * 128] for k in range(D_MODEL // 128)) * (0.5 / D_MODEL)
            dy = e * (1.0 / D_MODEL)
            acc_g3[...] += _rows8(dy * on)
            don = dy * g3
            dout = r3 * (don - on * _mean_rows(don * on))
            doutf_s[rows, :] = dout
            dout_ref[rows, :] = dout.astype(act)
            acc_bout[...] += _rows8(dout)
        dmix_s[...] = _mm_nt(dout_ref[...], wout_v[...])

        def emit_dproj(rows, off, val):
            dproj_ref[rows, off:off + val.shape[1]] = val.astype(act)
            acc_bin[:, off:off + val.shape[1]] += _rows8(val)

        for c in range(nb):
            for brow in chunks(BLOCK):
                rows = slice(c * BLOCK + brow.start, c * BLOCK + brow.stop)
                dso = dmix_s[rows, ATTN_W:]
                u = u_s[rows, :]
                mix = mix_s[rows, :]
                zs = gate_s[3, rows, :]
                sg = _sigmoid(zs)
                sgs = zs * sg
                du = dso * mix * sgs
                dmx = dso * u * sgs
                emit_dproj(rows, OFF_ZS, dso * u * mix * (sg * (1.0 + zs * (1.0 - sg))))
                us = gate_s[1, rows, :]
                emit_dproj(rows, OFF_US, du * (cdfu_s[rows, :] + us * _norm_pdf(us)))
                vlnb_s[brow, :] = (vhat_s[rows, :] * lng + lnb).astype(act)
                split_pairs(dmx, brow)
                acc_dbs[brow, :] += dmx
            for p in range(4):
                cols = slice(p * BLOCK, (p + 1) * BLOCK)
                dvln_s[:, cols] = _mm(wb_v[p], rhs_s[p])
                acc_dws[(2 * p) * BLOCK:(2 * p + 1) * BLOCK, :] += _mm_nt(rhs_s[p, 0:BLOCK, :], vlnb_s[:, cols])
                acc_dws[(2 * p + 1) * BLOCK:(2 * p + 2) * BLOCK, :] += _mm_nt(rhs_s[p, BLOCK:, :], vlnb_s[:, cols])
            for brow in chunks(BLOCK):
                rows = slice(c * BLOCK + brow.start, c * BLOCK + brow.stop)
                dvln = dvln_s[brow, :]
                vhat = vhat_s[rows, :]
                acc_lng[...] += _rows8(dvln * vhat)
                acc_lnb[...] += _rows8(dvln)
                dvhat = dvln * lng
                dvg = r2_s[rows, :] * (dvhat - _mean_rows(dvhat) - vhat * _mean_rows(dvhat * vhat))
                vs = gate_s[2, rows, :]
                emit_dproj(rows, OFF_VS, dvg * (cdfv_s[rows, :] + vs * _norm_pdf(vs)))

        dkf_s[...] = jnp.zeros(dkf_s.shape, F32)
        dvf_s[...] = jnp.zeros(dvf_s.shape, F32)
        for b in range(nb):
            blk = slice(b * BLOCK, (b + 1) * BLOCK)
            band = slice(b * BLOCK, (b + 2) * BLOCK)
            for brow in chunks(BLOCK):
                rows = slice(b * BLOCK + brow.start, b * BLOCK + brow.stop)
                za = gate_s[0, rows, :]
                sg = _sigmoid(za)
                dao = dmix_s[rows, 0:ATTN_W]
                o = o_s[rows, :]
                do = dao * (za * sg)
                emit_dproj(rows, OFF_ZA, dao * o * (sg * (1.0 + za * (1.0 - sg))))
                for sl in range(4):
                    g, pr = divmod(sl, 2)
                    cols = slice(sl * BLOCK, (sl + 1) * BLOCK)
                    d_pair = do[:, cols]
                    prod = d_pair * o[:, cols]
                    for k in range(2):
                        hrow = slice((2 * pr + k) * BLOCK + brow.start, (2 * pr + k) * BLOCK + brow.stop)
                        own = (lambda v: jnp.where(lo_rc, v, 0.0)) if k == 0 else (lambda v: jnp.where(lo_rc, 0.0, v))
                        dos_s[g, hrow, :] = own(d_pair).astype(act)
                        delta_s[g, hrow, :] = jnp.sum(own(prod), axis=1, keepdims=True)
            for g in range(2):
                dp_all = _mm_nt(dos_s[g], v2_s[g, band, :])
                for hh in range(4):
                    sink_t = jnp.zeros((1, 1), F32)
                    for rc in chunks(BLOCK, 2 * ROW_CHUNK):
                        rows = slice(hh * BLOCK + rc.start, hh * BLOCK + rc.stop)
                        delta = delta_s[g, rows, :]
                        dsb_s[rows, :] = (p_s[b * 2 + g, rows, :] * (dp_all[rows, :] - delta)).astype(act)
                        sink_t = sink_t + jnp.sum(ps_s[b * 2 + g, rows, :] * delta, axis=0, keepdims=True)
                    acc_sink[4 * g + hh:4 * g + hh + 1, :] += -sink_t
                dq2 = _mm(dsb_s[...], k2_s[g, band, :]) * SCALE
                for pr in range(2):
                    dq = jnp.where(lo, dq2[(2 * pr) * BLOCK:(2 * pr + 1) * BLOCK], dq2[(2 * pr + 1) * BLOCK:(2 * pr + 2) * BLOCK])
                    emit_dproj(blk, (2 * g + pr) * BLOCK, dq)
                lo_band = lax.broadcasted_iota(jnp.int32, (2 * BLOCK, BLOCK), 1) < HEAD_DIM
                mine = lo_band if g == 0 else jnp.logical_not(lo_band)
                for acc, lhs, rhs in ((dkf_s, dsb_s[...], stacked_q(b, g)), (dvf_s, pb_s[b * 2 + g], dos_s[g])):
                    d2 = _mm_tn(lhs, rhs)
                    full = d2 + pltpu.roll(d2, HEAD_DIM, 1)
                    acc[band, :] += jnp.where(mine, full, 0.0)
        for acc, carry, off in ((dkf_s, carryk_s, OFF_K), (dvf_s, carryv_s, OFF_V)):
            acc[t:t + BLOCK, :] += carry[...]
            carry[...] = acc[0:BLOCK, :]
            emit_dproj(slice(0, t), off, acc[BLOCK:, :])

        dh = _mm(dproj_ref[...], wint_v[...])
        dprojt_ref[...] = dproj_ref[...].T
        mixedt_ref[...] = mixed_ref[...].T
        for rows in chunks(t):
            r1 = r1_s[rows, :]
            xn = x_ref[rows, :] * r1
            dhc = dh[rows, :]
            acc_g1[...] += _rows8(dhc * xn)
            dxn = dhc * g1
            dx_ref[rows, :] = doutf_s[rows, :] + r1 * (dxn - xn * _mean_rows(dxn * xn))

        @pl.when(i == nt - 1)
        def _():
            tril = lax.broadcasted_iota(jnp.int32, (BLOCK, BLOCK), 0) >= lane128
            for hh in range(N_SGU_HEADS):
                rws = slice(hh * BLOCK, (hh + 1) * BLOCK)
                sw_ref[rws, :] = jnp.where(tril, acc_dws[rws, :], 0.0)
            vec_ref[...] = jnp.zeros((VEC_ROWS, 128), F32)

            def put(row0, acc):
                s = jnp.sum(acc[...], axis=0, keepdims=True)
                for k in range(acc.shape[1] // 128):
                    vec_ref[row0 + k:row0 + k + 1, :] = s[:, k * 128:(k + 1) * 128]

            put(R_G1, acc_g1)
            put(R_BIN, acc_bin)
            put(R_LNG, acc_lng)
            put(R_LNB, acc_lnb)
            put(R_BOUT, acc_bout)
            put(R_G3, acc_g3)
            vec_ref[R_SINK:R_SINK + 1, :] = jnp.sum(
                jnp.where(lax.broadcasted_iota(jnp.int32, (8, 128), 0) == lax.broadcasted_iota(jnp.int32, (8, 128), 1),
                          acc_sink[...], 0.0), axis=0, keepdims=True)
            vec_ref[R_LOSS:R_LOSS + 1, :] = jnp.zeros((1, 128), F32) + jnp.sum(acc_loss[...])
            dbs_t = acc_dbs[...].T
            vec_ref[R_SGUB:R_SGUB + 8, :] = jnp.sum(dbs_t.reshape(N_SGU_HEADS, SGU_W // N_SGU_HEADS, BLOCK), axis=1)

    full = lambda shape: pl.BlockSpec(shape, lambda i: (0,) * len(shape))
    tok = lambda w: pl.BlockSpec((t, w), lambda i: (nt - 1 - i, 0))
    in_specs = [
        pl.BlockSpec(memory_space=pltpu.SMEM),
        tok(D_MODEL),
        pl.BlockSpec((BLOCK, D_MODEL), lambda i: (jnp.maximum((nt - 1 - i) * nb - 1, 0), 0)),
        tok(D_MODEL),
        full((1, D_MODEL)), full((1, IN_W)), full((1, SGU_W)), full((1, SGU_W)),
        full((N_SGU_HEADS, BLOCK, BLOCK)), full((BLOCK, SGU_W)), full((1, D_MODEL)), full((1, D_MODEL)),
        pl.BlockSpec(memory_space=pl.ANY), pl.BlockSpec(memory_space=pl.ANY),
    ]
    out_shape = [
        jax.ShapeDtypeStruct((seq, D_MODEL), F32),
        jax.ShapeDtypeStruct((seq, D_MODEL), act),
        jax.ShapeDtypeStruct((IN_W, seq), act),
        jax.ShapeDtypeStruct((D_MODEL, seq), act),
        jax.ShapeDtypeStruct((seq, D_MODEL), act),
        jax.ShapeDtypeStruct((SW_ROWS, 128), F32),
        jax.ShapeDtypeStruct((VEC_ROWS, 128), F32),
    ]
    tok_t = lambda w: pl.BlockSpec((w, t), lambda i: (0, nt - 1 - i))
    out_specs = [tok(D_MODEL), tok(D_MODEL), tok_t(IN_W), tok_t(D_MODEL), tok(D_MODEL), full((SW_ROWS, 128)), full((VEC_ROWS, 128))]
    vm = pltpu.VMEM
    scratch = [
        vm((t, IN_W), act), vm((t, D_MODEL), act),
        vm((IN_W, D_MODEL), act), vm((D_MODEL, D_MODEL), act),
        vm((4, BLOCK, 2 * BLOCK), act), vm((4, BLOCK, 2 * BLOCK), act),
        vm((t, ATTN_W), act),
        vm((t + BLOCK, BLOCK), F32), vm((t + BLOCK, BLOCK), F32),
        vm((2, t + BLOCK, BLOCK), act), vm((2, t + BLOCK, BLOCK), act),
        vm((4, t, 512), F32),
        vm((2 * nb, 4 * BLOCK, 2 * BLOCK), F32), vm((2 * nb, 4 * BLOCK, 1), F32),
        vm((t, ATTN_W), F32), vm((t, SGU_W), F32), vm((t, SGU_W), F32), vm((t, SGU_W), F32), vm((t, 1), F32),
        vm((t, SGU_W), F32), vm((t, SGU_W), F32), vm((t, 1), F32),
        vm((t, D_MODEL), F32), vm((t, D_MODEL), F32),
        vm((t + BLOCK, BLOCK), F32), vm((t + BLOCK, BLOCK), F32), vm((BLOCK, BLOCK), F32), vm((BLOCK, BLOCK), F32),
        vm((BLOCK, D_MODEL), act), vm((BLOCK, 2 * BLOCK), F32), vm((2 * nb, 4 * BLOCK, 2 * BLOCK), act),
        vm((4, 2 * BLOCK, BLOCK), act), vm((BLOCK, SGU_W), act), vm((BLOCK, SGU_W), F32),
        vm((2, 4 * BLOCK, BLOCK), act), vm((2, 4 * BLOCK, 1), F32), vm((4 * BLOCK, 2 * BLOCK), act),
        vm((8, IN_W), F32), vm((8, D_MODEL), F32), vm((8, D_MODEL), F32), vm((8, D_MODEL), F32),
        vm((8, SGU_W), F32), vm((8, SGU_W), F32), vm((N_SGU_HEADS * BLOCK, BLOCK), F32), vm((BLOCK, SGU_W), F32),
        vm((8, 128), F32), vm((8, 128), F32),
        pltpu.SemaphoreType.DMA((2,)),
    ]
    return pl.pallas_call(
        body, name="fused", grid=(nt,), in_specs=in_specs, out_specs=out_specs, out_shape=out_shape,
        scratch_shapes=scratch,
        compiler_params=pltpu.CompilerParams(dimension_semantics=("arbitrary",), vmem_limit_bytes=VMEM_LIMIT),
    )(sinks, x, x, tgt, g1, b_in, ln_g, ln_b, sgu_w, bexp, b_out, g3, win_t, wout)


def _place():
    x, y, c = lax.axis_index("x"), lax.axis_index("y"), lax.axis_index("c")
    chips = [(1 - x, y), (x, 1 - y), (1 - x, 1 - y)]
    return x, y, c, chips


def _gather_call(a_loc, b_loc):
    def body(a_ref, b_ref, ga_ref, gb_ref, send_sems, recv_sems, loc_sems):
        x, y, c, chips = _place()
        me, sibling = (x, y, c), (x, y, 1 - c)
        j = 2 * x + y

        def half(which, cj, hf):
            ref, shard = (ga_ref, W_IN_SHARD) if which == 0 else (gb_ref, W_OUT_SHARD)
            n = shard // 2
            return ref.at[pl.ds(pl.multiple_of(cj * shard + hf * n, 16), n), :]

        def copy(k, src, dst, to):
            return pltpu.make_async_remote_copy(src_ref=src, dst_ref=dst, send_sem=send_sems.at[k], recv_sem=recv_sems.at[k],
                                                device_id=to, device_id_type=MESH)

        own = [pltpu.make_async_copy(a_ref, ga_ref.at[pl.ds(pl.multiple_of(j * W_IN_SHARD, 16), W_IN_SHARD), :], loc_sems.at[0]),
               pltpu.make_async_copy(b_ref, gb_ref.at[pl.ds(pl.multiple_of(j * W_OUT_SHARD, 16), W_OUT_SHARD), :], loc_sems.at[1])]
        for cp in own:
            cp.start()
        srcs = [a_ref.at[pl.ds(pl.multiple_of(c * HALF_A, 16), HALF_A), :], b_ref.at[pl.ds(pl.multiple_of(c * HALF_B, 16), HALF_B), :]]
        first = [copy(2 * r + w, srcs[w], half(w, j, c), (*chip, c)) for r, chip in enumerate(chips) for w in range(2)]
        for cp in first:
            cp.start()
        passed = []
        for r, (cx, cy) in enumerate(chips):
            cj = 2 * cx + cy
            for w in range(2):
                copy(2 * r + w, half(w, cj, c), half(w, cj, c), me).wait_recv()
                fwd = copy(6 + 2 * r + w, half(w, cj, c), half(w, cj, c), sibling)
                fwd.start()
                passed.append(fwd)
        for r, (cx, cy) in enumerate(chips):
            cj = 2 * cx + cy
            for w in range(2):
                copy(6 + 2 * r + w, half(w, cj, 1 - c), half(w, cj, 1 - c), me).wait_recv()
        for cp in first + passed:
            cp.wait_send()
        for cp in own:
            cp.wait()

    hbm = pl.BlockSpec(memory_space=pl.ANY)
    return pl.pallas_call(
        body, name="gather", in_specs=[hbm, hbm], out_specs=[hbm, hbm],
        out_shape=[jax.ShapeDtypeStruct((IN_W, D_MODEL), a_loc.dtype), jax.ShapeDtypeStruct((D_MODEL, D_MODEL), b_loc.dtype)],
        scratch_shapes=[pltpu.SemaphoreType.DMA((12,)), pltpu.SemaphoreType.DMA((12,)), pltpu.SemaphoreType.DMA((2,))],
    )(a_loc, b_loc)


def _wgrad_reduce_call(dproj_t, h, mixed_t, dout, sw, vec):
    wire = jnp.bfloat16
    half_sw = SW_ROWS // 2
    seq = h.shape[0]
    tk = min(1024, seq)
    nk = seq // tk
    n_steps = 2 * N_CHIPS
    rel_of = lambda s: s % 3 if s < 6 else 3
    half_of = lambda s: s // 3 if s < 6 else s - 6
    x, y = lax.axis_index("x"), lax.axis_index("y")
    chip_of = [2 * (1 - x) + (1 - y), 2 * (1 - x) + y, 2 * x + (1 - y), 2 * x + y]
    order = jnp.stack([2 * chip_of[rel_of(s)] + half_of(s) for s in range(n_steps)]).astype(jnp.int32)

    def body(order_ref, dpt_ref, h_ref, mxt_ref, dout_ref, sw_ref, small_ref, oa_ref, ob_ref, osw_ref, osmall_ref,
             acc_a, acc_b, sib_a, sib_b, snd_a, snd_b, in_a, in_b, own_a, own_b, fin_a, fin_b,
             all_small, sw_sib, sw_chips, sw_fin, send_sems, recv_sems, loc_sem):
        x, y, c = lax.axis_index("x"), lax.axis_index("y"), lax.axis_index("c")
        me, sibling = (x, y, c), (x, y, 1 - c)
        steps = [(1 - x, 1 - y), (1 - x, y), (x, 1 - y)]
        j = 2 * x + y
        dev = 4 * x + 2 * y + c
        b, k = pl.program_id(0), pl.program_id(1)

        def copy(n, src, dst, to):
            return pltpu.make_async_remote_copy(src_ref=src, dst_ref=dst, send_sem=send_sems.at[n], recv_sem=recv_sems.at[n],
                                                device_id=to, device_id_type=MESH)

        def half(ref, slot, hf, n):
            return ref.at[slot, pl.ds(pl.multiple_of(hf * n, 8), n), :]

        def sw_rows(ref, hf):
            return ref.at[pl.ds(pl.multiple_of(hf * half_sw, 8), half_sw), :]

        sm_first = [copy(16, small_ref, all_small.at[dev], sibling)]
        sm_first += [copy(17 + r, small_ref, all_small.at[dev], (*chip, c)) for r, chip in enumerate(steps)]
        sw_to_sib = copy(23, sw_rows(sw_ref, 1 - c), sw_sib, sibling)

        @pl.when((b == 0) & (k == 0))
        def _():
            all_small[dev] = small_ref[...]
            for cp in sm_first + [sw_to_sib]:
                cp.start()

        pa = _mm(dpt_ref[...], h_ref[...])
        pb = _mm(mxt_ref[...], dout_ref[...])
        slot = b % 2

        @pl.when(k == 0)
        def _():
            acc_a[slot] = pa
            acc_b[slot] = pb

        @pl.when(k != 0)
        def _():
            acc_a[slot] += pa
            acc_b[slot] += pb

        def to_sibling(s):
            r = rel_of(s)
            return [copy(r, acc_a.at[s % 2], sib_a.at[r], sibling), copy(4 + r, acc_b.at[s % 2], sib_b.at[r], sibling)]

        def chip_partial(s):
            r = rel_of(s)
            copy(r, sib_a.at[r], sib_a.at[r], me).wait_recv()
            copy(4 + r, sib_b.at[r], sib_b.at[r], me).wait_recv()
            return acc_a[s % 2] + sib_a[r], acc_b[s % 2] + sib_b[r]

        def to_owner(r):
            return [copy(8 + r, snd_a.at[r], in_a.at[r], (*steps[r], c)), copy(11 + r, snd_b.at[r], in_b.at[r], (*steps[r], c))]

        for s in range(n_steps):
            if s >= 1:
                sp = s - 1

                @pl.when((b == s) & (k == 0) & (c == half_of(sp)))
                def _():
                    ta, tb = chip_partial(sp)
                    r = rel_of(sp)
                    if r < 3:
                        snd_a[r] = ta.astype(wire)
                        snd_b[r] = tb.astype(wire)
                        for cp in to_owner(r):
                            cp.start()
                    else:
                        own_a[...] = ta
                        own_b[...] = tb

                @pl.when((b == s) & (k == nk - 1) & (c != half_of(sp)))
                def _():
                    for cp in to_sibling(sp):
                        cp.wait_send()

            @pl.when((b == s) & (k == nk - 1) & (c != half_of(s)))
            def _():
                for cp in to_sibling(s):
                    cp.start()

        @pl.when((b == n_steps - 1) & (k == nk - 1))
        def _():
            sm_passed = []
            for r, (cx, cy) in enumerate(steps):
                d = 4 * cx + 2 * cy + c
                copy(17 + r, all_small.at[d], all_small.at[d], me).wait_recv()
                fwd = copy(20 + r, all_small.at[d], all_small.at[d], sibling)
                fwd.start()
                sm_passed.append(fwd)
            copy(23, sw_sib, sw_sib, me).wait_recv()
            sw_chips[j] = (sw_rows(sw_ref, c)[...] + sw_sib[...]).astype(wire)
            sw_ici = [copy(24 + r, sw_chips.at[j], sw_chips.at[j], (*chip, c)) for r, chip in enumerate(steps)]
            for cp in sw_ici:
                cp.start()

            last = n_steps - 1

            @pl.when(c == half_of(last))
            def _():
                own_a[...], own_b[...] = chip_partial(last)

            @pl.when(c != half_of(last))
            def _():
                for cp in to_sibling(last):
                    cp.wait_send()

            tot_a = own_a[...]
            tot_b = own_b[...]
            for s in range(3):
                copy(8 + s, in_a.at[s], in_a.at[s], me).wait_recv()
                copy(11 + s, in_b.at[s], in_b.at[s], me).wait_recv()
                tot_a = tot_a + in_a[s].astype(F32)
                tot_b = tot_b + in_b[s].astype(F32)
            mine_a = oa_ref.at[pl.ds(pl.multiple_of(c * HALF_A, 8), HALF_A), :]
            mine_b = ob_ref.at[pl.ds(pl.multiple_of(c * HALF_B, 8), HALF_B), :]
            mine_a[...] = tot_a
            mine_b[...] = tot_b
            back = [copy(14, mine_a, fin_a, sibling), copy(15, mine_b, fin_b, sibling)]
            for cp in back:
                cp.start()

            copy(16, small_ref, all_small.at[dev ^ 1], me).wait_recv()
            for r, (cx, cy) in enumerate(steps):
                d = 4 * cx + 2 * cy + (1 - c)
                copy(20 + r, all_small.at[d], all_small.at[d], me).wait_recv()
            tot = all_small[0]
            for d in range(1, N_DEV):
                tot = tot + all_small[d]
            osmall_ref[...] = tot

            for r, (cx, cy) in enumerate(steps):
                cj = 2 * cx + cy
                copy(24 + r, sw_chips.at[cj], sw_chips.at[cj], me).wait_recv()
            tot_sw = sw_chips[0].astype(F32)
            for q in range(1, N_CHIPS):
                tot_sw = tot_sw + sw_chips[q].astype(F32)
            sw_rows(osw_ref, c)[...] = tot_sw
            sw_back = copy(27, sw_rows(osw_ref, c), sw_fin, sibling)
            sw_back.start()

            copy(14, fin_a, fin_a, me).wait_recv()
            copy(15, fin_b, fin_b, me).wait_recv()
            copy(27, sw_fin, sw_fin, me).wait_recv()
            oa_ref[pl.ds(pl.multiple_of((1 - c) * HALF_A, 8), HALF_A), :] = fin_a[...]
            ob_ref[pl.ds(pl.multiple_of((1 - c) * HALF_B, 8), HALF_B), :] = fin_b[...]
            sw_rows(osw_ref, 1 - c)[...] = sw_fin[...]
            sends = sm_first + sm_passed + sw_ici + back + [sw_to_sib, sw_back]
            for s in range(3):
                sends += to_owner(s)
            for cp in sends:
                cp.wait_send()

    vmem = pl.BlockSpec(memory_space=pltpu.VMEM)
    vm = pltpu.VMEM
    grid_spec = pltpu.PrefetchScalarGridSpec(
        num_scalar_prefetch=1, grid=(n_steps, nk),
        in_specs=[pl.BlockSpec((HALF_A, tk), lambda b, k, o: (o[b], k)), pl.BlockSpec((tk, D_MODEL), lambda b, k, o: (k, 0)),
                  pl.BlockSpec((HALF_B, tk), lambda b, k, o: (o[b], k)), pl.BlockSpec((tk, D_MODEL), lambda b, k, o: (k, 0)),
                  vmem, vmem],
        out_specs=[vmem, vmem, vmem, vmem],
        scratch_shapes=[vm((2, HALF_A, D_MODEL), F32), vm((2, HALF_B, D_MODEL), F32),
                        vm((N_CHIPS, HALF_A, D_MODEL), F32), vm((N_CHIPS, HALF_B, D_MODEL), F32),
                        vm((3, HALF_A, D_MODEL), wire), vm((3, HALF_B, D_MODEL), wire),
                        vm((3, HALF_A, D_MODEL), wire), vm((3, HALF_B, D_MODEL), wire),
                        vm((HALF_A, D_MODEL), F32), vm((HALF_B, D_MODEL), F32),
                        vm((HALF_A, D_MODEL), F32), vm((HALF_B, D_MODEL), F32),
                        vm((N_DEV, VEC_ROWS, 128), F32), vm((half_sw, 128), F32), vm((N_CHIPS, half_sw, 128), wire),
                        vm((half_sw, 128), F32),
                        pltpu.SemaphoreType.DMA((28,)), pltpu.SemaphoreType.DMA((28,)), pltpu.SemaphoreType.DMA((1,))])
    return pl.pallas_call(
        body, name="wgrad_reduce", grid_spec=grid_spec,
        out_shape=[jax.ShapeDtypeStruct((W_IN_SHARD, D_MODEL), F32), jax.ShapeDtypeStruct((W_OUT_SHARD, D_MODEL), F32),
                   jax.ShapeDtypeStruct((SW_ROWS, 128), F32), jax.ShapeDtypeStruct((VEC_ROWS, 128), F32)],
        compiler_params=pltpu.CompilerParams(dimension_semantics=("arbitrary", "arbitrary"), vmem_limit_bytes=VMEM_LIMIT),
    )(order, dproj_t, h, mixed_t, dout, sw, vec)


def _adamw(w, g, m, v):
    nm = ADAM_B1 * m + (1.0 - ADAM_B1) * g
    nv = ADAM_B2 * v + (1.0 - ADAM_B2) * (g * g)
    m_hat = nm / (1.0 - ADAM_B1 ** ADAM_STEP)
    v_hat = nv / (1.0 - ADAM_B2 ** ADAM_STEP)
    return -ADAM_LR * (m_hat / (jnp.sqrt(v_hat) + ADAM_EPS) + ADAM_WD * w), nm, nv


def _adamw_call(w, g, m, v, block_rows, name):
    rows, cols = w.shape

    def body(w_ref, g_ref, m_ref, v_ref, go_ref, d_ref, nm_ref, nv_ref):
        gg = g_ref[...]
        go_ref[...] = gg
        d_ref[...], nm_ref[...], nv_ref[...] = _adamw(w_ref[...], gg, m_ref[...], v_ref[...])

    spec = pl.BlockSpec((block_rows, cols), lambda i: (i, 0))
    return pl.pallas_call(
        body, name=name, grid=(rows // block_rows,), in_specs=[spec] * 4, out_specs=[spec] * 4,
        out_shape=[jax.ShapeDtypeStruct((rows, cols), F32)] * 4,
        compiler_params=pltpu.CompilerParams(dimension_semantics=("arbitrary",)),
    )(w, g, m, v)


_SMALL = (("norm_g", (1, D_MODEL), R_G1), ("b_in", (1, IN_W), R_BIN), ("attn_sinks", (1, 8), R_SINK),
          ("sgu_ln_g", (1, SGU_W), R_LNG), ("sgu_ln_b", (1, SGU_W), R_LNB), ("sgu_b", (N_SGU_HEADS, BLOCK), R_SGUB),
          ("b_out", (1, D_MODEL), R_BOUT), ("final_norm_g", (1, D_MODEL), R_G3))


def _adamw_small_call(sw_g, vec_g, sgu_w3, ws, ms, vs):
    n = len(_SMALL)

    def body(*refs):
        sw_ref, vec_ref = refs[0], refs[1]
        w3 = refs[2:5]
        w_refs, m_refs, v_refs = refs[5:5 + n], refs[5 + n:5 + 2 * n], refs[5 + 2 * n:5 + 3 * n]
        outs = refs[5 + 3 * n:]
        outs[0][...] = vec_ref[R_LOSS:R_LOSS + 1, 0:1]
        g = sw_ref[...]
        outs[1][...] = g
        outs[2][...], outs[3][...], outs[4][...] = _adamw(w3[0][...], g, w3[1][...], w3[2][...])
        for k, (_, shape, row) in enumerate(_SMALL):
            if shape[0] == 1 and shape[1] >= 128:
                g = jnp.concatenate([vec_ref[row + q:row + q + 1, :] for q in range(shape[1] // 128)], axis=1)
            else:
                g = vec_ref[row:row + shape[0], 0:shape[1]]
            o = outs[5 + 4 * k:9 + 4 * k]
            o[0][...] = g
            o[1][...], o[2][...], o[3][...] = _adamw(w_refs[k][...], g, m_refs[k][...], v_refs[k][...])

    vmem = pl.BlockSpec(memory_space=pltpu.VMEM)
    out_shape = [jax.ShapeDtypeStruct((1, 1), F32)] + [jax.ShapeDtypeStruct((SW_ROWS, 128), F32)] * 4
    for _, shape, _ in _SMALL:
        out_shape += [jax.ShapeDtypeStruct(shape, F32)] * 4
    args = [sw_g, vec_g, *sgu_w3, *ws, *ms, *vs]
    return pl.pallas_call(
        body, name="adamw_small", in_specs=[vmem] * len(args), out_specs=[vmem] * len(out_shape), out_shape=out_shape,
    )(*args)


def kernel(x, norm_g, w_in, b_in, attn_sinks, sgu_ln_g, sgu_ln_b, sgu_w, sgu_b, w_out, b_out, final_norm_g, loss_target, m_norm_g, m_w_in, m_b_in, m_attn_sinks, m_sgu_ln_g, m_sgu_ln_b, m_sgu_w, m_sgu_b, m_w_out, m_b_out, m_final_norm_g, v_norm_g, v_w_in, v_b_in, v_attn_sinks, v_sgu_ln_g, v_sgu_ln_b, v_sgu_w, v_sgu_b, v_w_out, v_b_out, v_final_norm_g):
    seq = x.shape[1]
    win_t, wout = _gather_call(w_in[0].T.astype(MXU_DTYPE), w_out[0].astype(MXU_DTYPE))
    bexp = jnp.repeat(sgu_b[0].T, SGU_W // N_SGU_HEADS, axis=1)
    dx, h, dproj_t, mixed_t, dout, sw, vec = _fused_call(
        x[0], loss_target[0], attn_sinks[0], norm_g, b_in, sgu_ln_g, sgu_ln_b, sgu_w[0], bexp, b_out,
        final_norm_g.reshape(1, D_MODEL), win_t, wout)
    ga_t, g_w_out, sw, vec = _wgrad_reduce_call(dproj_t, h, mixed_t, dout, sw, vec)

    names = ["norm_g", "w_in", "b_in", "attn_sinks", "sgu_ln_g", "sgu_ln_b", "sgu_w", "sgu_b", "w_out", "b_out", "final_norm_g"]
    res = {}
    res["w_in"] = [a.T[None] for a in _adamw_call(w_in[0].T, ga_t, m_w_in[0].T, v_w_in[0].T, W_IN_SHARD // 4, "adamw_w_in")]
    res["w_out"] = [a[None] for a in _adamw_call(w_out[0], g_w_out, m_w_out[0], v_w_out[0], HALF_B, "adamw_w_out")]
    given = dict(norm_g=(norm_g, m_norm_g, v_norm_g), b_in=(b_in, m_b_in, v_b_in), attn_sinks=(attn_sinks, m_attn_sinks, v_attn_sinks),
                 sgu_ln_g=(sgu_ln_g, m_sgu_ln_g, v_sgu_ln_g), sgu_ln_b=(sgu_ln_b, m_sgu_ln_b, v_sgu_ln_b),
                 sgu_b=(sgu_b, m_sgu_b, v_sgu_b), b_out=(b_out, m_b_out, v_b_out),
                 final_norm_g=(final_norm_g, m_final_norm_g, v_final_norm_g))
    wmv = [[given[n][k].reshape(shape) for n, shape, _ in _SMALL] for k in range(3)]
    outs = _adamw_small_call(sw, vec, [a.reshape(SW_ROWS, BLOCK) for a in (sgu_w, m_sgu_w, v_sgu_w)], *wmv)
    loss = outs[0].reshape(())
    res["sgu_w"] = [a.reshape(sgu_w.shape) for a in outs[1:5]]
    for k, (n, _, _) in enumerate(_SMALL):
        res[n] = [a.reshape(given[n][0].shape) for a in outs[5 + 4 * k:9 + 4 * k]]

    return (loss, dx[None], *[res[n][0] for n in names], *[res[n][1] for n in names], *[res[n][2] for n in names],
            *[res[n][3] for n in names])
```

```python
import functools
import math

import jax
import jax.numpy as jnp
from jax import lax
from jax.experimental import pallas as pl
from jax.experimental.pallas import tpu as pltpu

F32 = jnp.float32
MXU_DTYPE = jnp.bfloat16

D_MODEL = 1024
HEAD_DIM = 64
ATTN_W = 512
SGU_W = 512
N_SGU_HEADS = 8
BLOCK = 128
IN_W = 2816
OFF_K, OFF_V, OFF_ZA, OFF_US, OFF_VS, OFF_ZS = 512, 640, 768, 1280, 1792, 2304
NORM_EPS = 1e-5
NEG_INF = -1e30
SCALE = HEAD_DIM ** -0.5
SQRT_HALF = math.sqrt(0.5)
INV_SQRT_2PI = 1.0 / math.sqrt(2.0 * math.pi)

N_CHIPS = 4
N_DEV = 8
W_IN_SHARD = IN_W // N_CHIPS
W_OUT_SHARD = D_MODEL // N_CHIPS
HALF_A = W_IN_SHARD // 2
HALF_B = W_OUT_SHARD // 2

TILE = 256
VMEM_LIMIT = 56 * 1024 * 1024

ADAM_LR, ADAM_B1, ADAM_B2, ADAM_EPS, ADAM_WD, ADAM_STEP = 0.001, 0.9, 0.999, 1e-08, 0.01, 10

SW_ROWS = N_SGU_HEADS * BLOCK
R_G1, R_BIN, R_SINK, R_LOSS, R_LNG, R_LNB, R_SGUB, R_BOUT, R_G3 = 0, 8, 32, 40, 48, 56, 64, 72, 80
VEC_ROWS = 88

MESH = pl.DeviceIdType.MESH


def _mm(a, b):
    return jnp.dot(a, b, preferred_element_type=F32)


def _mm_nt(a, b):
    return lax.dot_general(a, b, (((1,), (1,)), ((), ())), preferred_element_type=F32)


def _mm_tn(a, b):
    return lax.dot_general(a, b, (((0,), (0,)), ((), ())), preferred_element_type=F32)


def _sigmoid(z):
    return 1.0 / (1.0 + jnp.exp(-z))


def _norm_cdf(z):
    return 0.5 * (1.0 + lax.erf(z * SQRT_HALF))


def _norm_pdf(z):
    return jnp.exp(-0.5 * z * z) * INV_SQRT_2PI


def _rows8(v):
    r, n = v.shape
    return jnp.sum(v.reshape(r // 8, 8, n), axis=0)


def _mean_rows(v):
    return jnp.sum(v, axis=1, keepdims=True) * (1.0 / v.shape[1])


def _fused_call(x, tgt, sinks, g1, b_in, ln_g, ln_b, sgu_w, bexp, b_out, g3, win_t, wout):
    seq = x.shape[0]
    t = TILE
    nt = seq // t
    nb = t // BLOCK
    act = MXU_DTYPE

    def body(sinks_ref, x_ref, xh_ref, tgt_ref, g1_ref, bin_ref, lng_ref, lnb_ref, sguw_ref, bexp_ref, bout_ref, g3_ref,
             wint_hbm, wout_hbm,
             dx_ref, h_ref, dprojt_ref, mixedt_ref, dout_ref, sw_ref, vec_ref,
             dproj_ref, mixed_ref, wint_v, wout_v, wf_v, wb_v, q_s, kf_s, vf_s, k2_s, v2_s, gate_s, p_s, ps_s, o_s, u_s, mix_s, vhat_s, r2_s,
             cdfu_s, cdfv_s, r1_s, doutf_s, dmix_s, dkf_s, dvf_s, carryk_s, carryv_s,
             acc_bin, acc_g1, acc_bout, acc_g3, acc_lng, acc_lnb, acc_dws, acc_dbs, acc_sink, acc_loss, sems):
        i = pl.program_id(0)
        tile = nt - 1 - i
        lane128 = lax.broadcasted_iota(jnp.int32, (BLOCK, BLOCK), 1)
        lo = lane128 < HEAD_DIM

        @pl.when(i == 0)
        def _():
            cp_a = pltpu.make_async_copy(wint_hbm, wint_v, sems.at[0])
            cp_b = pltpu.make_async_copy(wout_hbm, wout_v, sems.at[1])
            cp_a.start()
            cp_b.start()
            for acc in (acc_bin, acc_g1, acc_bout, acc_g3, acc_lng, acc_lnb, acc_dws, acc_dbs, acc_sink, acc_loss,
                        carryk_s, carryv_s):
                acc[...] = jnp.zeros(acc.shape, F32)
            tril = lax.broadcasted_iota(jnp.int32, (BLOCK, BLOCK), 0) >= lane128
            for h in range(N_SGU_HEADS):
                w = jnp.where(tril, sguw_ref[h], 0.0)
                wf_v[h // 2, :, (h % 2) * BLOCK:(h % 2 + 1) * BLOCK] = w.astype(act)
                wb_v[h // 2, :, (h % 2) * BLOCK:(h % 2 + 1) * BLOCK] = w.T.astype(act)
            cp_a.wait()
            cp_b.wait()

        g1 = g1_ref[...]

        def rms(v):
            r = lax.rsqrt(_mean_rows(v * v) + NORM_EPS)
            return r, v * r

        for c in range(nb):
            rows = slice(c * BLOCK, (c + 1) * BLOCK)
            r, xn = rms(x_ref[rows, :])
            r1_s[rows, :] = r
            h_ref[rows, :] = (xn * g1).astype(act)
        _, xnh = rms(xh_ref[...])
        kvh = _mm_nt((xnh * g1).astype(act), wint_v[OFF_K:OFF_ZA, :]) + bin_ref[:, OFF_K:OFF_ZA]
        kf_s[0:BLOCK, :] = kvh[:, :BLOCK]
        vf_s[0:BLOCK, :] = kvh[:, BLOCK:]

        h = h_ref[...]
        q = _mm_nt(h, wint_v[0:OFF_K, :]) + bin_ref[:, 0:OFF_K]
        q_s[...] = (q * SCALE).astype(act)
        kv = _mm_nt(h, wint_v[OFF_K:OFF_ZA, :]) + bin_ref[:, OFF_K:OFF_ZA]
        kf_s[BLOCK:, :] = kv[:, :BLOCK]
        vf_s[BLOCK:, :] = kv[:, BLOCK:]
        for r in range(4):
            cols = slice(OFF_ZA + r * 512, OFF_ZA + (r + 1) * 512)
            gate_s[r] = _mm_nt(h, wint_v[cols, :]) + bin_ref[:, cols]

        lo_kv = lax.broadcasted_iota(jnp.int32, (t + BLOCK, BLOCK), 1) < HEAD_DIM
        for src, dst in ((kf_s, k2_s), (vf_s, v2_s)):
            v = src[...]
            vr = pltpu.roll(v, HEAD_DIM, 1)
            dst[0] = jnp.where(lo_kv, v, vr).astype(act)
            dst[1] = jnp.where(lo_kv, vr, v).astype(act)

        rowi = lax.broadcasted_iota(jnp.int32, (BLOCK, 2 * BLOCK), 0)
        colj = lax.broadcasted_iota(jnp.int32, (BLOCK, 2 * BLOCK), 1)
        in_band = (colj > rowi) & (colj <= rowi + BLOCK)
        row512 = lax.broadcasted_iota(jnp.int32, (4 * BLOCK, 1), 0)

        def stacked_q(b, g):
            parts = []
            for p in range(2):
                slab = q_s[b * BLOCK:(b + 1) * BLOCK, (2 * g + p) * BLOCK:(2 * g + p + 1) * BLOCK]
                parts += [jnp.where(lo, slab, jnp.zeros_like(slab)), jnp.where(lo, jnp.zeros_like(slab), slab)]
            return jnp.concatenate(parts, axis=0)

        def sink_col(g):
            s = [sinks_ref[4 * g + k] for k in range(4)]
            return jnp.where(row512 < BLOCK, s[0], jnp.where(row512 < 2 * BLOCK, s[1], jnp.where(row512 < 3 * BLOCK, s[2], s[3])))

        for b in range(nb):
            band = slice(b * BLOCK, (b + 2) * BLOCK)
            first_key = jnp.where(tile * nb + b > 0, 0, BLOCK)
            valid = in_band & (colj >= first_key)
            valid4 = jnp.concatenate([valid] * 4, axis=0)
            for g in range(2):
                s = _mm_nt(stacked_q(b, g), k2_s[g, band, :])
                s = jnp.where(valid4, s, NEG_INF)
                sk = sink_col(g)
                m = jnp.maximum(jnp.max(s, axis=1, keepdims=True), sk)
                p = jnp.exp(s - m)
                psk = jnp.exp(sk - m)
                inv = 1.0 / (jnp.sum(p, axis=1, keepdims=True) + psk)
                p = p * inv
                p_s[b * 2 + g] = p
                ps_s[b * 2 + g] = psk * inv
                o2 = _mm(p.astype(act), v2_s[g, band, :])
                for pr in range(2):
                    o_s[b * BLOCK:(b + 1) * BLOCK, (2 * g + pr) * BLOCK:(2 * g + pr + 1) * BLOCK] = jnp.where(
                        lo, o2[(2 * pr) * BLOCK:(2 * pr + 1) * BLOCK], o2[(2 * pr + 1) * BLOCK:(2 * pr + 2) * BLOCK])

        lng = lng_ref[...]
        lnb = lnb_ref[...]
        for c in range(nb):
            rows = slice(c * BLOCK, (c + 1) * BLOCK)
            za = gate_s[0, rows, :]
            mixed_ref[rows, 0:ATTN_W] = (o_s[rows, :] * (za * _sigmoid(za))).astype(act)
            us = gate_s[1, rows, :]
            vs = gate_s[2, rows, :]
            zs = gate_s[3, rows, :]
            cu = _norm_cdf(us)
            cv = _norm_cdf(vs)
            cdfu_s[rows, :] = cu
            cdfv_s[rows, :] = cv
            u = us * cu
            vg = vs * cv
            vc = vg - _mean_rows(vg)
            r2 = lax.rsqrt(_mean_rows(vc * vc) + NORM_EPS)
            vhat = vc * r2
            r2_s[rows, :] = r2
            vhat_s[rows, :] = vhat
            u_s[rows, :] = u
            vln = vhat * lng + lnb
            for p in range(4):
                cols = slice(p * BLOCK, (p + 1) * BLOCK)
                slab = vln[:, cols]
                rhs = jnp.concatenate([jnp.where(lo, slab, 0.0), jnp.where(lo, 0.0, slab)], axis=0).astype(act)
                mix_s[rows, cols] = _mm(wf_v[p], rhs) + bexp_ref[:, cols]
            mixed_ref[rows, ATTN_W:] = (u * mix_s[rows, :] * (zs * _sigmoid(zs))).astype(act)

        g3 = g3_ref[...]
        proj_o = _mm(mixed_ref[...], wout_v[...])
        for c in range(nb):
            rows = slice(c * BLOCK, (c + 1) * BLOCK)
            out = x_ref[rows, :] + proj_o[rows, :] + bout_ref[...]
            r3, on = rms(out)
            e = on * g3 - tgt_ref[rows, :]
            e2 = _rows8(e * e)
            acc_loss[...] += sum(e2[:, k * 128:(k + 1) * 128] for k in range(D_MODEL // 128)) * (0.5 / D_MODEL)
            dy = e * (1.0 / D_MODEL)
            acc_g3[...] += _rows8(dy * on)
            don = dy * g3
            dout = r3 * (don - on * _mean_rows(don * on))
            doutf_s[rows, :] = dout
            dout_ref[rows, :] = dout.astype(act)
            acc_bout[...] += _rows8(dout)
        dmix_s[...] = _mm_nt(dout_ref[...], wout_v[...])

        for c in range(nb):
            rows = slice(c * BLOCK, (c + 1) * BLOCK)
            dso = dmix_s[rows, ATTN_W:]
            u = u_s[rows, :]
            mix = mix_s[rows, :]
            zs = gate_s[3, rows, :]
            sg = _sigmoid(zs)
            sgs = zs * sg
            du = dso * mix * sgs
            dmx = dso * u * sgs
            dzs = dso * u * mix * (sg * (1.0 + zs * (1.0 - sg)))
            us = gate_s[1, rows, :]
            dus = du * (cdfu_s[rows, :] + us * _norm_pdf(us))
            vhat = vhat_s[rows, :]
            vln = (vhat * lng + lnb).astype(act)
            dvln_parts = []
            for p in range(4):
                cols = slice(p * BLOCK, (p + 1) * BLOCK)
                slab = dmx[:, cols]
                d_lo = jnp.where(lo, slab, 0.0).astype(act)
                d_hi = jnp.where(lo, 0.0, slab).astype(act)
                dvln_parts.append(_mm(wb_v[p], jnp.concatenate([d_lo, d_hi], axis=0)))
                acc_dws[(2 * p) * BLOCK:(2 * p + 1) * BLOCK, :] += _mm_nt(d_lo, vln[:, cols])
                acc_dws[(2 * p + 1) * BLOCK:(2 * p + 2) * BLOCK, :] += _mm_nt(d_hi, vln[:, cols])
            dvln = jnp.concatenate(dvln_parts, axis=1)
            acc_dbs[...] += dmx
            acc_lng[...] += _rows8(dvln * vhat)
            acc_lnb[...] += _rows8(dvln)
            dvhat = dvln * lng
            dvg = r2_s[rows, :] * (dvhat - _mean_rows(dvhat) - vhat * _mean_rows(dvhat * vhat))
            vs = gate_s[2, rows, :]
            dvs = dvg * (cdfv_s[rows, :] + vs * _norm_pdf(vs))
            for off, val in ((OFF_US, dus), (OFF_VS, dvs), (OFF_ZS, dzs)):
                dproj_ref[rows, off:off + 512] = val.astype(act)
                acc_bin[:, off:off + 512] += _rows8(val)

        dkf_s[...] = jnp.zeros(dkf_s.shape, F32)
        dvf_s[...] = jnp.zeros(dvf_s.shape, F32)
        for b in range(nb):
            rows = slice(b * BLOCK, (b + 1) * BLOCK)
            band = slice(b * BLOCK, (b + 2) * BLOCK)
            za = gate_s[0, rows, :]
            sg = _sigmoid(za)
            dao = dmix_s[rows, 0:ATTN_W]
            o = o_s[rows, :]
            do = dao * (za * sg)
            dza = dao * o * (sg * (1.0 + za * (1.0 - sg)))
            dproj_ref[rows, OFF_ZA:OFF_US] = dza.astype(act)
            acc_bin[:, OFF_ZA:OFF_US] += _rows8(dza)
            for g in range(2):
                do_parts, delta_parts = [], []
                for pr in range(2):
                    cols = slice((2 * g + pr) * BLOCK, (2 * g + pr + 1) * BLOCK)
                    d_pair = do[:, cols]
                    prod = d_pair * o[:, cols]
                    do_parts += [jnp.where(lo, d_pair, 0.0).astype(act), jnp.where(lo, 0.0, d_pair).astype(act)]
                    delta_parts += [jnp.sum(jnp.where(lo, prod, 0.0), axis=1, keepdims=True),
                                    jnp.sum(jnp.where(lo, 0.0, prod), axis=1, keepdims=True)]
                do_st = jnp.concatenate(do_parts, axis=0)
                delta = jnp.concatenate(delta_parts, axis=0)
                p = p_s[b * 2 + g]
                dp = _mm_nt(do_st, v2_s[g, band, :])
                ds = p * (dp - delta)
                sink_t = ps_s[b * 2 + g] * delta
                for k in range(4):
                    acc_sink[4 * g + k:4 * g + k + 1, :] += -jnp.sum(sink_t[k * BLOCK:(k + 1) * BLOCK], axis=0, keepdims=True)
                ds_a = ds.astype(act)
                dq2 = _mm(ds_a, k2_s[g, band, :]) * SCALE
                for pr in range(2):
                    cols = slice((2 * g + pr) * BLOCK, (2 * g + pr + 1) * BLOCK)
                    dq = jnp.where(lo, dq2[(2 * pr) * BLOCK:(2 * pr + 1) * BLOCK], dq2[(2 * pr + 1) * BLOCK:(2 * pr + 2) * BLOCK])
                    dproj_ref[rows, cols] = dq.astype(act)
                    acc_bin[:, cols] += _rows8(dq)
                lo_band = lax.broadcasted_iota(jnp.int32, (2 * BLOCK, BLOCK), 1) < HEAD_DIM
                mine = lo_band if g == 0 else jnp.logical_not(lo_band)
                for acc, lhs, rhs in ((dkf_s, ds_a, stacked_q(b, g)), (dvf_s, p.astype(act), do_st)):
                    d2 = _mm_tn(lhs, rhs)
                    full = d2 + pltpu.roll(d2, HEAD_DIM, 1)
                    acc[band, :] += jnp.where(mine, full, 0.0)
        for acc, carry, off in ((dkf_s, carryk_s, OFF_K), (dvf_s, carryv_s, OFF_V)):
            acc[t:t + BLOCK, :] += carry[...]
            carry[...] = acc[0:BLOCK, :]
            d = acc[BLOCK:, :]
            dproj_ref[:, off:off + BLOCK] = d.astype(act)
            acc_bin[:, off:off + BLOCK] += _rows8(d)

        dh = _mm(dproj_ref[...], wint_v[...])
        dprojt_ref[...] = dproj_ref[...].T
        mixedt_ref[...] = mixed_ref[...].T
        for c in range(nb):
            rows = slice(c * BLOCK, (c + 1) * BLOCK)
            r1 = r1_s[rows, :]
            xn = x_ref[rows, :] * r1
            dhc = dh[rows, :]
            acc_g1[...] += _rows8(dhc * xn)
            dxn = dhc * g1
            dx_ref[rows, :] = doutf_s[rows, :] + r1 * (dxn - xn * _mean_rows(dxn * xn))

        @pl.when(i == nt - 1)
        def _():
            tril = lax.broadcasted_iota(jnp.int32, (BLOCK, BLOCK), 0) >= lane128
            for hh in range(N_SGU_HEADS):
                rws = slice(hh * BLOCK, (hh + 1) * BLOCK)
                sw_ref[rws, :] = jnp.where(tril, acc_dws[rws, :], 0.0)
            vec_ref[...] = jnp.zeros((VEC_ROWS, 128), F32)

            def put(row0, acc):
                s = jnp.sum(acc[...], axis=0, keepdims=True)
                for k in range(acc.shape[1] // 128):
                    vec_ref[row0 + k:row0 + k + 1, :] = s[:, k * 128:(k + 1) * 128]

            put(R_G1, acc_g1)
            put(R_BIN, acc_bin)
            put(R_LNG, acc_lng)
            put(R_LNB, acc_lnb)
            put(R_BOUT, acc_bout)
            put(R_G3, acc_g3)
            vec_ref[R_SINK:R_SINK + 1, :] = jnp.sum(
                jnp.where(lax.broadcasted_iota(jnp.int32, (8, 128), 0) == lax.broadcasted_iota(jnp.int32, (8, 128), 1),
                          acc_sink[...], 0.0), axis=0, keepdims=True)
            vec_ref[R_LOSS:R_LOSS + 1, :] = jnp.zeros((1, 128), F32) + jnp.sum(acc_loss[...])
            dbs_t = acc_dbs[...].T
            vec_ref[R_SGUB:R_SGUB + 8, :] = jnp.sum(dbs_t.reshape(N_SGU_HEADS, SGU_W // N_SGU_HEADS, BLOCK), axis=1)

    full = lambda shape: pl.BlockSpec(shape, lambda i: (0,) * len(shape))
    tok = lambda w: pl.BlockSpec((t, w), lambda i: (nt - 1 - i, 0))
    in_specs = [
        pl.BlockSpec(memory_space=pltpu.SMEM),
        tok(D_MODEL),
        pl.BlockSpec((BLOCK, D_MODEL), lambda i: (jnp.maximum((nt - 1 - i) * nb - 1, 0), 0)),
        tok(D_MODEL),
        full((1, D_MODEL)), full((1, IN_W)), full((1, SGU_W)), full((1, SGU_W)),
        full((N_SGU_HEADS, BLOCK, BLOCK)), full((BLOCK, SGU_W)), full((1, D_MODEL)), full((1, D_MODEL)),
        pl.BlockSpec(memory_space=pl.ANY), pl.BlockSpec(memory_space=pl.ANY),
    ]
    out_shape = [
        jax.ShapeDtypeStruct((seq, D_MODEL), F32),
        jax.ShapeDtypeStruct((seq, D_MODEL), act),
        jax.ShapeDtypeStruct((IN_W, seq), act),
        jax.ShapeDtypeStruct((D_MODEL, seq), act),
        jax.ShapeDtypeStruct((seq, D_MODEL), act),
        jax.ShapeDtypeStruct((SW_ROWS, 128), F32),
        jax.ShapeDtypeStruct((VEC_ROWS, 128), F32),
    ]
    tok_t = lambda w: pl.BlockSpec((w, t), lambda i: (0, nt - 1 - i))
    out_specs = [tok(D_MODEL), tok(D_MODEL), tok_t(IN_W), tok_t(D_MODEL), tok(D_MODEL), full((SW_ROWS, 128)), full((VEC_ROWS, 128))]
    vm = pltpu.VMEM
    scratch = [
        vm((t, IN_W), act), vm((t, D_MODEL), act),
        vm((IN_W, D_MODEL), act), vm((D_MODEL, D_MODEL), act),
        vm((4, BLOCK, 2 * BLOCK), act), vm((4, BLOCK, 2 * BLOCK), act),
        vm((t, ATTN_W), act),
        vm((t + BLOCK, BLOCK), F32), vm((t + BLOCK, BLOCK), F32),
        vm((2, t + BLOCK, BLOCK), act), vm((2, t + BLOCK, BLOCK), act),
        vm((4, t, 512), F32),
        vm((2 * nb, 4 * BLOCK, 2 * BLOCK), F32), vm((2 * nb, 4 * BLOCK, 1), F32),
        vm((t, ATTN_W), F32), vm((t, SGU_W), F32), vm((t, SGU_W), F32), vm((t, SGU_W), F32), vm((t, 1), F32),
        vm((t, SGU_W), F32), vm((t, SGU_W), F32), vm((t, 1), F32),
        vm((t, D_MODEL), F32), vm((t, D_MODEL), F32),
        vm((t + BLOCK, BLOCK), F32), vm((t + BLOCK, BLOCK), F32), vm((BLOCK, BLOCK), F32), vm((BLOCK, BLOCK), F32),
        vm((8, IN_W), F32), vm((8, D_MODEL), F32), vm((8, D_MODEL), F32), vm((8, D_MODEL), F32),
        vm((8, SGU_W), F32), vm((8, SGU_W), F32), vm((N_SGU_HEADS * BLOCK, BLOCK), F32), vm((BLOCK, SGU_W), F32),
        vm((8, 128), F32), vm((8, 128), F32),
        pltpu.SemaphoreType.DMA((2,)),
    ]
    return pl.pallas_call(
        body, name="fused", grid=(nt,), in_specs=in_specs, out_specs=out_specs, out_shape=out_shape,
        scratch_shapes=scratch,
        compiler_params=pltpu.CompilerParams(dimension_semantics=("arbitrary",), vmem_limit_bytes=VMEM_LIMIT),
    )(sinks, x, x, tgt, g1, b_in, ln_g, ln_b, sgu_w, bexp, b_out, g3, win_t, wout)


def _place():
    x, y, c = lax.axis_index("x"), lax.axis_index("y"), lax.axis_index("c")
    chips = [(1 - x, y), (x, 1 - y), (1 - x, 1 - y)]
    return x, y, c, chips


def _gather_call(a_loc, b_loc):
    def body(a_ref, b_ref, ga_ref, gb_ref, send_sems, recv_sems, loc_sems):
        x, y, c, chips = _place()
        me, sibling = (x, y, c), (x, y, 1 - c)
        j = 2 * x + y

        def half(which, cj, hf):
            ref, shard = (ga_ref, W_IN_SHARD) if which == 0 else (gb_ref, W_OUT_SHARD)
            n = shard // 2
            return ref.at[pl.ds(pl.multiple_of(cj * shard + hf * n, 16), n), :]

        def copy(k, src, dst, to):
            return pltpu.make_async_remote_copy(src_ref=src, dst_ref=dst, send_sem=send_sems.at[k], recv_sem=recv_sems.at[k],
                                                device_id=to, device_id_type=MESH)

        own = [pltpu.make_async_copy(a_ref, ga_ref.at[pl.ds(pl.multiple_of(j * W_IN_SHARD, 16), W_IN_SHARD), :], loc_sems.at[0]),
               pltpu.make_async_copy(b_ref, gb_ref.at[pl.ds(pl.multiple_of(j * W_OUT_SHARD, 16), W_OUT_SHARD), :], loc_sems.at[1])]
        for cp in own:
            cp.start()
        srcs = [a_ref.at[pl.ds(pl.multiple_of(c * HALF_A, 16), HALF_A), :], b_ref.at[pl.ds(pl.multiple_of(c * HALF_B, 16), HALF_B), :]]
        first = [copy(2 * r + w, srcs[w], half(w, j, c), (*chip, c)) for r, chip in enumerate(chips) for w in range(2)]
        for cp in first:
            cp.start()
        passed = []
        for r, (cx, cy) in enumerate(chips):
            cj = 2 * cx + cy
            for w in range(2):
                copy(2 * r + w, half(w, cj, c), half(w, cj, c), me).wait_recv()
                fwd = copy(6 + 2 * r + w, half(w, cj, c), half(w, cj, c), sibling)
                fwd.start()
                passed.append(fwd)
        for r, (cx, cy) in enumerate(chips):
            cj = 2 * cx + cy
            for w in range(2):
                copy(6 + 2 * r + w, half(w, cj, 1 - c), half(w, cj, 1 - c), me).wait_recv()
        for cp in first + passed:
            cp.wait_send()
        for cp in own:
            cp.wait()

    hbm = pl.BlockSpec(memory_space=pl.ANY)
    return pl.pallas_call(
        body, name="gather", in_specs=[hbm, hbm], out_specs=[hbm, hbm],
        out_shape=[jax.ShapeDtypeStruct((IN_W, D_MODEL), a_loc.dtype), jax.ShapeDtypeStruct((D_MODEL, D_MODEL), b_loc.dtype)],
        scratch_shapes=[pltpu.SemaphoreType.DMA((12,)), pltpu.SemaphoreType.DMA((12,)), pltpu.SemaphoreType.DMA((2,))],
    )(a_loc, b_loc)


def _wgrad_reduce_call(dproj_t, h, mixed_t, dout, sw, vec):
    wire = jnp.bfloat16
    half_sw = SW_ROWS // 2
    seq = h.shape[0]
    tk = min(1024, seq)
    nk = seq // tk
    n_steps = 2 * N_CHIPS
    rel_of = lambda s: s % 3 if s < 6 else 3
    half_of = lambda s: s // 3 if s < 6 else s - 6
    x, y = lax.axis_index("x"), lax.axis_index("y")
    chip_of = [2 * (1 - x) + (1 - y), 2 * (1 - x) + y, 2 * x + (1 - y), 2 * x + y]
    order = jnp.stack([2 * chip_of[rel_of(s)] + half_of(s) for s in range(n_steps)]).astype(jnp.int32)

    def body(order_ref, dpt_ref, h_hbm, mxt_ref, dout_hbm, sw_ref, small_ref, oa_ref, ob_ref, osw_ref, osmall_ref,
             h_v, dout_v, acc_a, acc_b, sib_a, sib_b, snd_a, snd_b, in_a, in_b, own_a, own_b, fin_a, fin_b,
             all_small, sw_sib, sw_chips, sw_fin, send_sems, recv_sems, loc_sems):
        x, y, c = lax.axis_index("x"), lax.axis_index("y"), lax.axis_index("c")
        me, sibling = (x, y, c), (x, y, 1 - c)
        steps = [(1 - x, 1 - y), (1 - x, y), (x, 1 - y)]
        j = 2 * x + y
        dev = 4 * x + 2 * y + c
        b, k = pl.program_id(0), pl.program_id(1)

        def copy(n, src, dst, to):
            return pltpu.make_async_remote_copy(src_ref=src, dst_ref=dst, send_sem=send_sems.at[n], recv_sem=recv_sems.at[n],
                                                device_id=to, device_id_type=MESH)

        def sw_rows(ref, hf):
            return ref.at[pl.ds(pl.multiple_of(hf * half_sw, 8), half_sw), :]

        sm_first = [copy(16, small_ref, all_small.at[dev], sibling)]
        sm_first += [copy(17 + r, small_ref, all_small.at[dev], (*chip, c)) for r, chip in enumerate(steps)]
        sw_to_sib = copy(23, sw_rows(sw_ref, 1 - c), sw_sib, sibling)

        @pl.when((b == 0) & (k == 0))
        def _():
            all_small[dev] = small_ref[...]
            for cp in sm_first + [sw_to_sib]:
                cp.start()

        def load(kk):
            rows = pl.ds(kk * tk, tk)
            return [pltpu.make_async_copy(h_hbm.at[rows, :], h_v.at[rows, :], loc_sems.at[kk]),
                    pltpu.make_async_copy(dout_hbm.at[rows, :], dout_v.at[rows, :], loc_sems.at[nk + kk])]

        for kk in range(nk):
            @pl.when((b == 0) & (k == 0))
            def _():
                for cp in load(kk):
                    cp.start()

        for kk in range(nk):
            @pl.when((b == 0) & (k == kk))
            def _():
                for cp in load(kk):
                    cp.wait()

        tok = pl.ds(pl.multiple_of(k * tk, tk), tk)
        pa = _mm(dpt_ref[...], h_v[tok, :])
        pb = _mm(mxt_ref[...], dout_v[tok, :])
        slot = b % 2

        @pl.when(k == 0)
        def _():
            acc_a[slot] = pa
            acc_b[slot] = pb

        @pl.when(k != 0)
        def _():
            acc_a[slot] += pa
            acc_b[slot] += pb

        def to_sibling(s):
            r = rel_of(s)
            return [copy(r, acc_a.at[s % 2], sib_a.at[r], sibling), copy(4 + r, acc_b.at[s % 2], sib_b.at[r], sibling)]

        def chip_partial(s):
            r = rel_of(s)
            copy(r, sib_a.at[r], sib_a.at[r], me).wait_recv()
            copy(4 + r, sib_b.at[r], sib_b.at[r], me).wait_recv()
            return acc_a[s % 2] + sib_a[r], acc_b[s % 2] + sib_b[r]

        def to_owner(r):
            return [copy(8 + r, snd_a.at[r], in_a.at[r], (*steps[r], c)), copy(11 + r, snd_b.at[r], in_b.at[r], (*steps[r], c))]

        for s in range(n_steps):
            if s >= 1:
                sp = s - 1

                @pl.when((b == s) & (k == 0) & (c == half_of(sp)))
                def _():
                    ta, tb = chip_partial(sp)
                    r = rel_of(sp)
                    if r < 3:
                        snd_a[r] = ta.astype(wire)
                        snd_b[r] = tb.astype(wire)
                        for cp in to_owner(r):
                            cp.start()
                    else:
                        own_a[...] = ta
                        own_b[...] = tb

                @pl.when((b == s) & (k == nk - 1) & (c != half_of(sp)))
                def _():
                    for cp in to_sibling(sp):
                        cp.wait_send()

            @pl.when((b == s) & (k == nk - 1) & (c != half_of(s)))
            def _():
                for cp in to_sibling(s):
                    cp.start()

        @pl.when((b == n_steps - 1) & (k == nk - 1))
        def _():
            sm_passed = []
            for r, (cx, cy) in enumerate(steps):
                d = 4 * cx + 2 * cy + c
                copy(17 + r, all_small.at[d], all_small.at[d], me).wait_recv()
                fwd = copy(20 + r, all_small.at[d], all_small.at[d], sibling)
                fwd.start()
                sm_passed.append(fwd)
            copy(23, sw_sib, sw_sib, me).wait_recv()
            sw_chips[j] = (sw_rows(sw_ref, c)[...] + sw_sib[...]).astype(wire)
            sw_ici = [copy(24 + r, sw_chips.at[j], sw_chips.at[j], (*chip, c)) for r, chip in enumerate(steps)]
            for cp in sw_ici:
                cp.start()

            last = n_steps - 1

            @pl.when(c == half_of(last))
            def _():
                own_a[...], own_b[...] = chip_partial(last)

            @pl.when(c != half_of(last))
            def _():
                for cp in to_sibling(last):
                    cp.wait_send()

            tot_a = own_a[...]
            tot_b = own_b[...]
            for s in range(3):
                copy(8 + s, in_a.at[s], in_a.at[s], me).wait_recv()
                copy(11 + s, in_b.at[s], in_b.at[s], me).wait_recv()
                tot_a = tot_a + in_a[s].astype(F32)
                tot_b = tot_b + in_b[s].astype(F32)
            mine_a = oa_ref.at[pl.ds(pl.multiple_of(c * HALF_A, 8), HALF_A), :]
            mine_b = ob_ref.at[pl.ds(pl.multiple_of(c * HALF_B, 8), HALF_B), :]
            mine_a[...] = tot_a
            mine_b[...] = tot_b
            back = [copy(14, mine_a, fin_a, sibling), copy(15, mine_b, fin_b, sibling)]
            for cp in back:
                cp.start()

            copy(16, small_ref, all_small.at[dev ^ 1], me).wait_recv()
            for r, (cx, cy) in enumerate(steps):
                d = 4 * cx + 2 * cy + (1 - c)
                copy(20 + r, all_small.at[d], all_small.at[d], me).wait_recv()
            tot = all_small[0]
            for d in range(1, N_DEV):
                tot = tot + all_small[d]
            osmall_ref[...] = tot

            for r, (cx, cy) in enumerate(steps):
                cj = 2 * cx + cy
                copy(24 + r, sw_chips.at[cj], sw_chips.at[cj], me).wait_recv()
            tot_sw = sw_chips[0].astype(F32)
            for q in range(1, N_CHIPS):
                tot_sw = tot_sw + sw_chips[q].astype(F32)
            sw_rows(osw_ref, c)[...] = tot_sw
            sw_back = copy(27, sw_rows(osw_ref, c), sw_fin, sibling)
            sw_back.start()

            copy(14, fin_a, fin_a, me).wait_recv()
            copy(15, fin_b, fin_b, me).wait_recv()
            copy(27, sw_fin, sw_fin, me).wait_recv()
            oa_ref[pl.ds(pl.multiple_of((1 - c) * HALF_A, 8), HALF_A), :] = fin_a[...]
            ob_ref[pl.ds(pl.multiple_of((1 - c) * HALF_B, 8), HALF_B), :] = fin_b[...]
            sw_rows(osw_ref, 1 - c)[...] = sw_fin[...]
            sends = sm_first + sm_passed + sw_ici + back + [sw_to_sib, sw_back]
            for s in range(3):
                sends += to_owner(s)
            for cp in sends:
                cp.wait_send()

    vmem = pl.BlockSpec(memory_space=pltpu.VMEM)
    vm = pltpu.VMEM
    grid_spec = pltpu.PrefetchScalarGridSpec(
        num_scalar_prefetch=1, grid=(n_steps, nk),
        in_specs=[pl.BlockSpec((HALF_A, tk), lambda b, k, o: (o[b], k)), pl.BlockSpec(memory_space=pl.ANY),
                  pl.BlockSpec((HALF_B, tk), lambda b, k, o: (o[b], k)), pl.BlockSpec(memory_space=pl.ANY),
                  vmem, vmem],
        out_specs=[vmem, vmem, vmem, vmem],
        scratch_shapes=[vm((seq, D_MODEL), h.dtype), vm((seq, D_MODEL), dout.dtype),
                        vm((2, HALF_A, D_MODEL), F32), vm((2, HALF_B, D_MODEL), F32),
                        vm((N_CHIPS, HALF_A, D_MODEL), F32), vm((N_CHIPS, HALF_B, D_MODEL), F32),
                        vm((3, HALF_A, D_MODEL), wire), vm((3, HALF_B, D_MODEL), wire),
                        vm((3, HALF_A, D_MODEL), wire), vm((3, HALF_B, D_MODEL), wire),
                        vm((HALF_A, D_MODEL), F32), vm((HALF_B, D_MODEL), F32),
                        vm((HALF_A, D_MODEL), F32), vm((HALF_B, D_MODEL), F32),
                        vm((N_DEV, VEC_ROWS, 128), F32), vm((half_sw, 128), F32), vm((N_CHIPS, half_sw, 128), wire),
                        vm((half_sw, 128), F32),
                        pltpu.SemaphoreType.DMA((28,)), pltpu.SemaphoreType.DMA((28,)), pltpu.SemaphoreType.DMA((2 * nk,))])
    return pl.pallas_call(
        body, name="wgrad_reduce", grid_spec=grid_spec,
        out_shape=[jax.ShapeDtypeStruct((W_IN_SHARD, D_MODEL), F32), jax.ShapeDtypeStruct((W_OUT_SHARD, D_MODEL), F32),
                   jax.ShapeDtypeStruct((SW_ROWS, 128), F32), jax.ShapeDtypeStruct((VEC_ROWS, 128), F32)],
        compiler_params=pltpu.CompilerParams(dimension_semantics=("arbitrary", "arbitrary"), vmem_limit_bytes=VMEM_LIMIT),
    )(order, dproj_t, h, mixed_t, dout, sw, vec)


def _adamw(w, g, m, v):
    nm = ADAM_B1 * m + (1.0 - ADAM_B1) * g
    nv = ADAM_B2 * v + (1.0 - ADAM_B2) * (g * g)
    m_hat = nm / (1.0 - ADAM_B1 ** ADAM_STEP)
    v_hat = nv / (1.0 - ADAM_B2 ** ADAM_STEP)
    return -ADAM_LR * (m_hat / (jnp.sqrt(v_hat) + ADAM_EPS) + ADAM_WD * w), nm, nv


def _adamw_call(w, g, m, v, block_rows, name):
    rows, cols = w.shape

    def body(w_ref, g_ref, m_ref, v_ref, go_ref, d_ref, nm_ref, nv_ref):
        gg = g_ref[...]
        go_ref[...] = gg
        d_ref[...], nm_ref[...], nv_ref[...] = _adamw(w_ref[...], gg, m_ref[...], v_ref[...])

    spec = pl.BlockSpec((block_rows, cols), lambda i: (i, 0))
    return pl.pallas_call(
        body, name=name, grid=(rows // block_rows,), in_specs=[spec] * 4, out_specs=[spec] * 4,
        out_shape=[jax.ShapeDtypeStruct((rows, cols), F32)] * 4,
        compiler_params=pltpu.CompilerParams(dimension_semantics=("arbitrary",)),
    )(w, g, m, v)


_SMALL = (("norm_g", (1, D_MODEL), R_G1), ("b_in", (1, IN_W), R_BIN), ("attn_sinks", (1, 8), R_SINK),
          ("sgu_ln_g", (1, SGU_W), R_LNG), ("sgu_ln_b", (1, SGU_W), R_LNB), ("sgu_b", (N_SGU_HEADS, BLOCK), R_SGUB),
          ("b_out", (1, D_MODEL), R_BOUT), ("final_norm_g", (1, D_MODEL), R_G3))


def _adamw_small_call(sw_g, vec_g, sgu_w3, ws, ms, vs):
    n = len(_SMALL)

    def body(*refs):
        sw_ref, vec_ref = refs[0], refs[1]
        w3 = refs[2:5]
        w_refs, m_refs, v_refs = refs[5:5 + n], refs[5 + n:5 + 2 * n], refs[5 + 2 * n:5 + 3 * n]
        outs = refs[5 + 3 * n:]
        outs[0][...] = vec_ref[R_LOSS:R_LOSS + 1, 0:1]
        g = sw_ref[...]
        outs[1][...] = g
        outs[2][...], outs[3][...], outs[4][...] = _adamw(w3[0][...], g, w3[1][...], w3[2][...])
        for k, (_, shape, row) in enumerate(_SMALL):
            if shape[0] == 1 and shape[1] >= 128:
                g = jnp.concatenate([vec_ref[row + q:row + q + 1, :] for q in range(shape[1] // 128)], axis=1)
            else:
                g = vec_ref[row:row + shape[0], 0:shape[1]]
            o = outs[5 + 4 * k:9 + 4 * k]
            o[0][...] = g
            o[1][...], o[2][...], o[3][...] = _adamw(w_refs[k][...], g, m_refs[k][...], v_refs[k][...])

    vmem = pl.BlockSpec(memory_space=pltpu.VMEM)
    out_shape = [jax.ShapeDtypeStruct((1, 1), F32)] + [jax.ShapeDtypeStruct((SW_ROWS, 128), F32)] * 4
    for _, shape, _ in _SMALL:
        out_shape += [jax.ShapeDtypeStruct(shape, F32)] * 4
    args = [sw_g, vec_g, *sgu_w3, *ws, *ms, *vs]
    return pl.pallas_call(
        body, name="adamw_small", in_specs=[vmem] * len(args), out_specs=[vmem] * len(out_shape), out_shape=out_shape,
    )(*args)


def kernel(x, norm_g, w_in, b_in, attn_sinks, sgu_ln_g, sgu_ln_b, sgu_w, sgu_b, w_out, b_out, final_norm_g, loss_target, m_norm_g, m_w_in, m_b_in, m_attn_sinks, m_sgu_ln_g, m_sgu_ln_b, m_sgu_w, m_sgu_b, m_w_out, m_b_out, m_final_norm_g, v_norm_g, v_w_in, v_b_in, v_attn_sinks, v_sgu_ln_g, v_sgu_ln_b, v_sgu_w, v_sgu_b, v_w_out, v_b_out, v_final_norm_g):
    seq = x.shape[1]
    win_t, wout = _gather_call(w_in[0].T.astype(MXU_DTYPE), w_out[0].astype(MXU_DTYPE))
    bexp = jnp.repeat(sgu_b[0].T, SGU_W // N_SGU_HEADS, axis=1)
    dx, h, dproj_t, mixed_t, dout, sw, vec = _fused_call(
        x[0], loss_target[0], attn_sinks[0], norm_g, b_in, sgu_ln_g, sgu_ln_b, sgu_w[0], bexp, b_out,
        final_norm_g.reshape(1, D_MODEL), win_t, wout)
    ga_t, g_w_out, sw, vec = _wgrad_reduce_call(dproj_t, h, mixed_t, dout, sw, vec)

    names = ["norm_g", "w_in", "b_in", "attn_sinks", "sgu_ln_g", "sgu_ln_b", "sgu_w", "sgu_b", "w_out", "b_out", "final_norm_g"]
    res = {}
    res["w_in"] = [a.T[None] for a in _adamw_call(w_in[0].T, ga_t, m_w_in[0].T, v_w_in[0].T, W_IN_SHARD // 4, "adamw_w_in")]
    res["w_out"] = [a[None] for a in _adamw_call(w_out[0], g_w_out, m_w_out[0], v_w_out[0], HALF_B, "adamw_w_out")]
    given = dict(norm_g=(norm_g, m_norm_g, v_norm_g), b_in=(b_in, m_b_in, v_b_in), attn_sinks=(attn_sinks, m_attn_sinks, v_attn_sinks),
                 sgu_ln_g=(sgu_ln_g, m_sgu_ln_g, v_sgu_ln_g), sgu_ln_b=(sgu_ln_b, m_sgu_ln_b, v_sgu_ln_b),
                 sgu_b=(sgu_b, m_sgu_b, v_sgu_b), b_out=(b_out, m_b_out, v_b_out),
                 final_norm_g=(final_norm_g, m_final_norm_g, v_final_norm_g))
    wmv = [[given[n][k].reshape(shape) for n, shape, _ in _SMALL] for k in range(3)]
    outs = _adamw_small_call(sw, vec, [a.reshape(SW_ROWS, BLOCK) for a in (sgu_w, m_sgu_w, v_sgu_w)], *wmv)
    loss = outs[0].reshape(())
    res["sgu_w"] = [a.reshape(sgu_w.shape) for a in outs[1:5]]
    for k, (n, _, _) in enumerate(_SMALL):
        res[n] = [a.reshape(given[n][0].shape) for a in outs[5 + 4 * k:9 + 4 * k]]

    return (loss, dx[None], *[res[n][0] for n in names], *[res[n][1] for n in names], *[res[n][2] for n in names],
            *[res[n][3] for n in names])
```

```python
import functools
import math

import jax
import jax.numpy as jnp
from jax import lax
from jax.experimental import pallas as pl
from jax.experimental.pallas import tpu as pltpu

F32 = jnp.float32
MXU_DTYPE = jnp.bfloat16

D_MODEL = 1024
HEAD_DIM = 64
ATTN_W = 512
SGU_W = 512
N_SGU_HEADS = 8
BLOCK = 128
IN_W = 2816
OFF_K, OFF_V, OFF_ZA, OFF_US, OFF_VS, OFF_ZS = 512, 640, 768, 1280, 1792, 2304
NORM_EPS = 1e-5
NEG_INF = -1e30
SCALE = HEAD_DIM ** -0.5
SQRT_HALF = math.sqrt(0.5)
INV_SQRT_2PI = 1.0 / math.sqrt(2.0 * math.pi)

N_CHIPS = 4
N_DEV = 8
W_IN_SHARD = IN_W // N_CHIPS
W_OUT_SHARD = D_MODEL // N_CHIPS
HALF_A = W_IN_SHARD // 2
HALF_B = W_OUT_SHARD // 2

TILE = 256
VMEM_LIMIT = 56 * 1024 * 1024

ADAM_LR, ADAM_B1, ADAM_B2, ADAM_EPS, ADAM_WD, ADAM_STEP = 0.001, 0.9, 0.999, 1e-08, 0.01, 10

SW_ROWS = N_SGU_HEADS * BLOCK
R_G1, R_BIN, R_SINK, R_LOSS, R_LNG, R_LNB, R_SGUB, R_BOUT, R_G3 = 0, 8, 32, 40, 48, 56, 64, 72, 80
VEC_ROWS = 88

MESH = pl.DeviceIdType.MESH


def _mm(a, b):
    return jnp.dot(a, b, preferred_element_type=F32)


def _mm_nt(a, b):
    return lax.dot_general(a, b, (((1,), (1,)), ((), ())), preferred_element_type=F32)


def _mm_tn(a, b):
    return lax.dot_general(a, b, (((0,), (0,)), ((), ())), preferred_element_type=F32)


def _sigmoid(z):
    return 1.0 / (1.0 + jnp.exp(-z))


def _norm_cdf(z):
    return 0.5 * (1.0 + lax.erf(z * SQRT_HALF))


def _norm_pdf(z):
    return jnp.exp(-0.5 * z * z) * INV_SQRT_2PI


def _rows8(v):
    r, n = v.shape
    return jnp.sum(v.reshape(r // 8, 8, n), axis=0)


def _mean_rows(v):
    return jnp.sum(v, axis=1, keepdims=True) * (1.0 / v.shape[1])


def _fused_call(x, tgt, sinks, g1, b_in, ln_g, ln_b, sgu_w, bexp, b_out, g3, win_t, wout):
    seq = x.shape[0]
    t = TILE
    nt = seq // t
    nb = t // BLOCK
    act = MXU_DTYPE

    def body(sinks_ref, x_ref, xh_ref, tgt_ref, g1_ref, bin_ref, lng_ref, lnb_ref, sguw_ref, bexp_ref, bout_ref, g3_ref,
             wint_hbm, wout_hbm,
             dx_ref, h_ref, dprojt_ref, mixedt_ref, dout_ref, sw_ref, vec_ref,
             dproj_ref, mixed_ref, wint_v, wout_v, wf_v, wb_v, q_s, kf_s, vf_s, k2_s, v2_s, gate_s, p_s, ps_s, o_s, u_s, mix_s, vhat_s, r2_s,
             cdfu_s, cdfv_s, r1_s, doutf_s, dmix_s, dkf_s, dvf_s, carryk_s, carryv_s,
             acc_bin, acc_g1, acc_bout, acc_g3, acc_lng, acc_lnb, acc_dws, acc_dbs, acc_sink, acc_loss, sems):
        i = pl.program_id(0)
        tile = nt - 1 - i
        lane128 = lax.broadcasted_iota(jnp.int32, (BLOCK, BLOCK), 1)
        lo = lane128 < HEAD_DIM

        @pl.when(i == 0)
        def _():
            cp_a = pltpu.make_async_copy(wint_hbm, wint_v, sems.at[0])
            cp_b = pltpu.make_async_copy(wout_hbm, wout_v, sems.at[1])
            cp_a.start()
            cp_b.start()
            for acc in (acc_bin, acc_g1, acc_bout, acc_g3, acc_lng, acc_lnb, acc_dws, acc_dbs, acc_sink, acc_loss,
                        carryk_s, carryv_s):
                acc[...] = jnp.zeros(acc.shape, F32)
            tril = lax.broadcasted_iota(jnp.int32, (BLOCK, BLOCK), 0) >= lane128
            for h in range(N_SGU_HEADS):
                w = jnp.where(tril, sguw_ref[h], 0.0)
                wf_v[h // 2, :, (h % 2) * BLOCK:(h % 2 + 1) * BLOCK] = w.astype(act)
                wb_v[h // 2, :, (h % 2) * BLOCK:(h % 2 + 1) * BLOCK] = w.T.astype(act)
            cp_a.wait()
            cp_b.wait()

        g1 = g1_ref[...]

        def rms(v):
            r = lax.rsqrt(_mean_rows(v * v) + NORM_EPS)
            return r, v * r

        for c in range(nb):
            rows = slice(c * BLOCK, (c + 1) * BLOCK)
            r, xn = rms(x_ref[rows, :])
            r1_s[rows, :] = r
            h_ref[rows, :] = (xn * g1).astype(act)
        _, xnh = rms(xh_ref[...])
        kvh = _mm_nt((xnh * g1).astype(act), wint_v[OFF_K:OFF_ZA, :]) + bin_ref[:, OFF_K:OFF_ZA]
        kf_s[0:BLOCK, :] = kvh[:, :BLOCK]
        vf_s[0:BLOCK, :] = kvh[:, BLOCK:]

        h = h_ref[...]
        q = _mm_nt(h, wint_v[0:OFF_K, :]) + bin_ref[:, 0:OFF_K]
        q_s[...] = (q * SCALE).astype(act)
        kv = _mm_nt(h, wint_v[OFF_K:OFF_ZA, :]) + bin_ref[:, OFF_K:OFF_ZA]
        kf_s[BLOCK:, :] = kv[:, :BLOCK]
        vf_s[BLOCK:, :] = kv[:, BLOCK:]
        for r in range(4):
            cols = slice(OFF_ZA + r * 512, OFF_ZA + (r + 1) * 512)
            gate_s[r] = _mm_nt(h, wint_v[cols, :]) + bin_ref[:, cols]

        lo_kv = lax.broadcasted_iota(jnp.int32, (t + BLOCK, BLOCK), 1) < HEAD_DIM
        for src, dst in ((kf_s, k2_s), (vf_s, v2_s)):
            v = src[...]
            vr = pltpu.roll(v, HEAD_DIM, 1)
            dst[0] = jnp.where(lo_kv, v, vr).astype(act)
            dst[1] = jnp.where(lo_kv, vr, v).astype(act)

        rowi = lax.broadcasted_iota(jnp.int32, (BLOCK, 2 * BLOCK), 0)
        colj = lax.broadcasted_iota(jnp.int32, (BLOCK, 2 * BLOCK), 1)
        in_band = (colj > rowi) & (colj <= rowi + BLOCK)
        row512 = lax.broadcasted_iota(jnp.int32, (4 * BLOCK, 1), 0)

        def stacked_q(b, g):
            parts = []
            for p in range(2):
                slab = q_s[b * BLOCK:(b + 1) * BLOCK, (2 * g + p) * BLOCK:(2 * g + p + 1) * BLOCK]
                parts += [jnp.where(lo, slab, jnp.zeros_like(slab)), jnp.where(lo, jnp.zeros_like(slab), slab)]
            return jnp.concatenate(parts, axis=0)

        def sink_col(g):
            s = [sinks_ref[4 * g + k] for k in range(4)]
            return jnp.where(row512 < BLOCK, s[0], jnp.where(row512 < 2 * BLOCK, s[1], jnp.where(row512 < 3 * BLOCK, s[2], s[3])))

        for b in range(nb):
            band = slice(b * BLOCK, (b + 2) * BLOCK)
            first_key = jnp.where(tile * nb + b > 0, 0, BLOCK)
            valid = in_band & (colj >= first_key)
            valid4 = jnp.concatenate([valid] * 4, axis=0)
            for g in range(2):
                s = _mm_nt(stacked_q(b, g), k2_s[g, band, :])
                s = jnp.where(valid4, s, NEG_INF)
                sk = sink_col(g)
                m = jnp.maximum(jnp.max(s, axis=1, keepdims=True), sk)
                p = jnp.exp(s - m)
                psk = jnp.exp(sk - m)
                inv = 1.0 / (jnp.sum(p, axis=1, keepdims=True) + psk)
                p = p * inv
                p_s[b * 2 + g] = p
                ps_s[b * 2 + g] = psk * inv
                o2 = _mm(p.astype(act), v2_s[g, band, :])
                for pr in range(2):
                    o_s[b * BLOCK:(b + 1) * BLOCK, (2 * g + pr) * BLOCK:(2 * g + pr + 1) * BLOCK] = jnp.where(
                        lo, o2[(2 * pr) * BLOCK:(2 * pr + 1) * BLOCK], o2[(2 * pr + 1) * BLOCK:(2 * pr + 2) * BLOCK])

        lng = lng_ref[...]
        lnb = lnb_ref[...]
        for c in range(nb):
            rows = slice(c * BLOCK, (c + 1) * BLOCK)
            za = gate_s[0, rows, :]
            mixed_ref[rows, 0:ATTN_W] = (o_s[rows, :] * (za * _sigmoid(za))).astype(act)
            us = gate_s[1, rows, :]
            vs = gate_s[2, rows, :]
            zs = gate_s[3, rows, :]
            cu = _norm_cdf(us)
            cv = _norm_cdf(vs)
            cdfu_s[rows, :] = cu
            cdfv_s[rows, :] = cv
            u = us * cu
            vg = vs * cv
            vc = vg - _mean_rows(vg)
            r2 = lax.rsqrt(_mean_rows(vc * vc) + NORM_EPS)
            vhat = vc * r2
            r2_s[rows, :] = r2
            vhat_s[rows, :] = vhat
            u_s[rows, :] = u
            vln = vhat * lng + lnb
            for p in range(4):
                cols = slice(p * BLOCK, (p + 1) * BLOCK)
                slab = vln[:, cols]
                rhs = jnp.concatenate([jnp.where(lo, slab, 0.0), jnp.where(lo, 0.0, slab)], axis=0).astype(act)
                mix_s[rows, cols] = _mm(wf_v[p], rhs) + bexp_ref[:, cols]
            mixed_ref[rows, ATTN_W:] = (u * mix_s[rows, :] * (zs * _sigmoid(zs))).astype(act)

        g3 = g3_ref[...]
        proj_o = _mm(mixed_ref[...], wout_v[...])
        for c in range(nb):
            rows = slice(c * BLOCK, (c + 1) * BLOCK)
            out = x_ref[rows, :] + proj_o[rows, :] + bout_ref[...]
            r3, on = rms(out)
            e = on * g3 - tgt_ref[rows, :]
            e2 = _rows8(e * e)
            acc_loss[...] += sum(e2[:, k * 128:(k + 1) * 128] for k in range(D_MODEL // 128)) * (0.5 / D_MODEL)
            dy = e * (1.0 / D_MODEL)
            acc_g3[...] += _rows8(dy * on)
            don = dy * g3
            dout = r3 * (don - on * _mean_rows(don * on))
            doutf_s[rows, :] = dout
            dout_ref[rows, :] = dout.astype(act)
            acc_bout[...] += _rows8(dout)
        dmix_s[...] = _mm_nt(dout_ref[...], wout_v[...])

        for c in range(nb):
            rows = slice(c * BLOCK, (c + 1) * BLOCK)
            dso = dmix_s[rows, ATTN_W:]
            u = u_s[rows, :]
            mix = mix_s[rows, :]
            zs = gate_s[3, rows, :]
            sg = _sigmoid(zs)
            sgs = zs * sg
            du = dso * mix * sgs
            dmx = dso * u * sgs
            dzs = dso * u * mix * (sg * (1.0 + zs * (1.0 - sg)))
            us = gate_s[1, rows, :]
            dus = du * (cdfu_s[rows, :] + us * _norm_pdf(us))
            vhat = vhat_s[rows, :]
            vln = (vhat * lng + lnb).astype(act)
            dvln_parts = []
            for p in range(4):
                cols = slice(p * BLOCK, (p + 1) * BLOCK)
                slab = dmx[:, cols]
                d_lo = jnp.where(lo, slab, 0.0).astype(act)
                d_hi = jnp.where(lo, 0.0, slab).astype(act)
                dvln_parts.append(_mm(wb_v[p], jnp.concatenate([d_lo, d_hi], axis=0)))
                acc_dws[(2 * p) * BLOCK:(2 * p + 1) * BLOCK, :] += _mm_nt(d_lo, vln[:, cols])
                acc_dws[(2 * p + 1) * BLOCK:(2 * p + 2) * BLOCK, :] += _mm_nt(d_hi, vln[:, cols])
            dvln = jnp.concatenate(dvln_parts, axis=1)
            acc_dbs[...] += dmx
            acc_lng[...] += _rows8(dvln * vhat)
            acc_lnb[...] += _rows8(dvln)
            dvhat = dvln * lng
            dvg = r2_s[rows, :] * (dvhat - _mean_rows(dvhat) - vhat * _mean_rows(dvhat * vhat))
            vs = gate_s[2, rows, :]
            dvs = dvg * (cdfv_s[rows, :] + vs * _norm_pdf(vs))
            for off, val in ((OFF_US, dus), (OFF_VS, dvs), (OFF_ZS, dzs)):
                dproj_ref[rows, off:off + 512] = val.astype(act)
                acc_bin[:, off:off + 512] += _rows8(val)

        dkf_s[...] = jnp.zeros(dkf_s.shape, F32)
        dvf_s[...] = jnp.zeros(dvf_s.shape, F32)
        for b in range(nb):
            rows = slice(b * BLOCK, (b + 1) * BLOCK)
            band = slice(b * BLOCK, (b + 2) * BLOCK)
            za = gate_s[0, rows, :]
            sg = _sigmoid(za)
            dao = dmix_s[rows, 0:ATTN_W]
            o = o_s[rows, :]
            do = dao * (za * sg)
            dza = dao * o * (sg * (1.0 + za * (1.0 - sg)))
            dproj_ref[rows, OFF_ZA:OFF_US] = dza.astype(act)
            acc_bin[:, OFF_ZA:OFF_US] += _rows8(dza)
            for g in range(2):
                do_parts, delta_parts = [], []
                for pr in range(2):
                    cols = slice((2 * g + pr) * BLOCK, (2 * g + pr + 1) * BLOCK)
                    d_pair = do[:, cols]
                    prod = d_pair * o[:, cols]
                    do_parts += [jnp.where(lo, d_pair, 0.0).astype(act), jnp.where(lo, 0.0, d_pair).astype(act)]
                    delta_parts += [jnp.sum(jnp.where(lo, prod, 0.0), axis=1, keepdims=True),
                                    jnp.sum(jnp.where(lo, 0.0, prod), axis=1, keepdims=True)]
                do_st = jnp.concatenate(do_parts, axis=0)
                delta = jnp.concatenate(delta_parts, axis=0)
                p = p_s[b * 2 + g]
                dp = _mm_nt(do_st, v2_s[g, band, :])
                ds = p * (dp - delta)
                sink_t = ps_s[b * 2 + g] * delta
                for k in range(4):
                    acc_sink[4 * g + k:4 * g + k + 1, :] += -jnp.sum(sink_t[k * BLOCK:(k + 1) * BLOCK], axis=0, keepdims=True)
                ds_a = ds.astype(act)
                dq2 = _mm(ds_a, k2_s[g, band, :]) * SCALE
                for pr in range(2):
                    cols = slice((2 * g + pr) * BLOCK, (2 * g + pr + 1) * BLOCK)
                    dq = jnp.where(lo, dq2[(2 * pr) * BLOCK:(2 * pr + 1) * BLOCK], dq2[(2 * pr + 1) * BLOCK:(2 * pr + 2) * BLOCK])
                    dproj_ref[rows, cols] = dq.astype(act)
                    acc_bin[:, cols] += _rows8(dq)
                lo_band = lax.broadcasted_iota(jnp.int32, (2 * BLOCK, BLOCK), 1) < HEAD_DIM
                mine = lo_band if g == 0 else jnp.logical_not(lo_band)
                for acc, lhs, rhs in ((dkf_s, ds_a, stacked_q(b, g)), (dvf_s, p.astype(act), do_st)):
                    d2 = _mm_tn(lhs, rhs)
                    full = d2 + pltpu.roll(d2, HEAD_DIM, 1)
                    acc[band, :] += jnp.where(mine, full, 0.0)
        for acc, carry, off in ((dkf_s, carryk_s, OFF_K), (dvf_s, carryv_s, OFF_V)):
            acc[t:t + BLOCK, :] += carry[...]
            carry[...] = acc[0:BLOCK, :]
            d = acc[BLOCK:, :]
            dproj_ref[:, off:off + BLOCK] = d.astype(act)
            acc_bin[:, off:off + BLOCK] += _rows8(d)

        dh = _mm(dproj_ref[...], wint_v[...])
        dprojt_ref[...] = dproj_ref[...].T
        mixedt_ref[...] = mixed_ref[...].T
        for c in range(nb):
            rows = slice(c * BLOCK, (c + 1) * BLOCK)
            r1 = r1_s[rows, :]
            xn = x_ref[rows, :] * r1
            dhc = dh[rows, :]
            acc_g1[...] += _rows8(dhc * xn)
            dxn = dhc * g1
            dx_ref[rows, :] = doutf_s[rows, :] + r1 * (dxn - xn * _mean_rows(dxn * xn))

        @pl.when(i == nt - 1)
        def _():
            tril = lax.broadcasted_iota(jnp.int32, (BLOCK, BLOCK), 0) >= lane128
            for hh in range(N_SGU_HEADS):
                rws = slice(hh * BLOCK, (hh + 1) * BLOCK)
                sw_ref[rws, :] = jnp.where(tril, acc_dws[rws, :], 0.0)
            vec_ref[...] = jnp.zeros((VEC_ROWS, 128), F32)

            def put(row0, acc):
                s = jnp.sum(acc[...], axis=0, keepdims=True)
                for k in range(acc.shape[1] // 128):
                    vec_ref[row0 + k:row0 + k + 1, :] = s[:, k * 128:(k + 1) * 128]

            put(R_G1, acc_g1)
            put(R_BIN, acc_bin)
            put(R_LNG, acc_lng)
            put(R_LNB, acc_lnb)
            put(R_BOUT, acc_bout)
            put(R_G3, acc_g3)
            vec_ref[R_SINK:R_SINK + 1, :] = jnp.sum(
                jnp.where(lax.broadcasted_iota(jnp.int32, (8, 128), 0) == lax.broadcasted_iota(jnp.int32, (8, 128), 1),
                          acc_sink[...], 0.0), axis=0, keepdims=True)
            vec_ref[R_LOSS:R_LOSS + 1, :] = jnp.zeros((1, 128), F32) + jnp.sum(acc_loss[...])
            dbs_t = acc_dbs[...].T
            vec_ref[R_SGUB:R_SGUB + 8, :] = jnp.sum(dbs_t.reshape(N_SGU_HEADS, SGU_W // N_SGU_HEADS, BLOCK), axis=1)

    full = lambda shape: pl.BlockSpec(shape, lambda i: (0,) * len(shape))
    tok = lambda w: pl.BlockSpec((t, w), lambda i: (nt - 1 - i, 0))
    in_specs = [
        pl.BlockSpec(memory_space=pltpu.SMEM),
        tok(D_MODEL),
        pl.BlockSpec((BLOCK, D_MODEL), lambda i: (jnp.maximum((nt - 1 - i) * nb - 1, 0), 0)),
        tok(D_MODEL),
        full((1, D_MODEL)), full((1, IN_W)), full((1, SGU_W)), full((1, SGU_W)),
        full((N_SGU_HEADS, BLOCK, BLOCK)), full((BLOCK, SGU_W)), full((1, D_MODEL)), full((1, D_MODEL)),
        pl.BlockSpec(memory_space=pl.ANY), pl.BlockSpec(memory_space=pl.ANY),
    ]
    out_shape = [
        jax.ShapeDtypeStruct((seq, D_MODEL), F32),
        jax.ShapeDtypeStruct((seq, D_MODEL), act),
        jax.ShapeDtypeStruct((IN_W, seq), act),
        jax.ShapeDtypeStruct((D_MODEL, seq), act),
        jax.ShapeDtypeStruct((seq, D_MODEL), act),
        jax.ShapeDtypeStruct((SW_ROWS, 128), F32),
        jax.ShapeDtypeStruct((VEC_ROWS, 128), F32),
    ]
    tok_t = lambda w: pl.BlockSpec((w, t), lambda i: (0, nt - 1 - i))
    out_specs = [tok(D_MODEL), tok(D_MODEL), tok_t(IN_W), tok_t(D_MODEL), tok(D_MODEL), full((SW_ROWS, 128)), full((VEC_ROWS, 128))]
    vm = pltpu.VMEM
    scratch = [
        vm((t, IN_W), act), vm((t, D_MODEL), act),
        vm((IN_W, D_MODEL), act), vm((D_MODEL, D_MODEL), act),
        vm((4, BLOCK, 2 * BLOCK), act), vm((4, BLOCK, 2 * BLOCK), act),
        vm((t, ATTN_W), act),
        vm((t + BLOCK, BLOCK), F32), vm((t + BLOCK, BLOCK), F32),
        vm((2, t + BLOCK, BLOCK), act), vm((2, t + BLOCK, BLOCK), act),
        vm((4, t, 512), F32),
        vm((2 * nb, 4 * BLOCK, 2 * BLOCK), F32), vm((2 * nb, 4 * BLOCK, 1), F32),
        vm((t, ATTN_W), F32), vm((t, SGU_W), F32), vm((t, SGU_W), F32), vm((t, SGU_W), F32), vm((t, 1), F32),
        vm((t, SGU_W), F32), vm((t, SGU_W), F32), vm((t, 1), F32),
        vm((t, D_MODEL), F32), vm((t, D_MODEL), F32),
        vm((t + BLOCK, BLOCK), F32), vm((t + BLOCK, BLOCK), F32), vm((BLOCK, BLOCK), F32), vm((BLOCK, BLOCK), F32),
        vm((8, IN_W), F32), vm((8, D_MODEL), F32), vm((8, D_MODEL), F32), vm((8, D_MODEL), F32),
        vm((8, SGU_W), F32), vm((8, SGU_W), F32), vm((N_SGU_HEADS * BLOCK, BLOCK), F32), vm((BLOCK, SGU_W), F32),
        vm((8, 128), F32), vm((8, 128), F32),
        pltpu.SemaphoreType.DMA((2,)),
    ]
    return pl.pallas_call(
        body, name="fused", grid=(nt,), in_specs=in_specs, out_specs=out_specs, out_shape=out_shape,
        scratch_shapes=scratch,
        compiler_params=pltpu.CompilerParams(dimension_semantics=("arbitrary",), vmem_limit_bytes=VMEM_LIMIT),
    )(sinks, x, x, tgt, g1, b_in, ln_g, ln_b, sgu_w, bexp, b_out, g3, win_t, wout)


def _place():
    x, y, c = lax.axis_index("x"), lax.axis_index("y"), lax.axis_index("c")
    chips = [(1 - x, y), (x, 1 - y), (1 - x, 1 - y)]
    return x, y, c, chips


def _gather_call(a_loc, b_loc):
    def body(a_ref, b_ref, ga_ref, gb_ref, send_sems, recv_sems, loc_sems):
        x, y, c = lax.axis_index("x"), lax.axis_index("y"), lax.axis_index("c")
        me, sibling, x_nbr, y_nbr = (x, y, c), (x, y, 1 - c), (1 - x, y, c), (x, 1 - y, c)
        j, j_x, j_y, j_d = 2 * x + y, 2 * (1 - x) + y, 2 * x + (1 - y), 2 * (1 - x) + (1 - y)

        def quarter(w, ref, chip, hf, q):
            shard = (W_IN_SHARD, W_OUT_SHARD)[w]
            base = 0 if chip is None else chip * shard
            return ref.at[pl.ds(pl.multiple_of(base + hf * (shard // 2) + q * (shard // 4), 16), shard // 4), :]

        def copy(k, src, dst, to):
            return pltpu.make_async_remote_copy(src_ref=src, dst_ref=dst, send_sem=send_sems.at[k], recv_sem=recv_sems.at[k],
                                                device_id=to, device_id_type=MESH)

        own = [pltpu.make_async_copy(a_ref, ga_ref.at[pl.ds(pl.multiple_of(j * W_IN_SHARD, 16), W_IN_SHARD), :], loc_sems.at[0]),
               pltpu.make_async_copy(b_ref, gb_ref.at[pl.ds(pl.multiple_of(j * W_OUT_SHARD, 16), W_OUT_SHARD), :], loc_sems.at[1])]
        for cp in own:
            cp.start()

        sent = []
        for w, (loc, out) in enumerate(((a_ref, ga_ref), (b_ref, gb_ref))):
            for k, (q, to) in enumerate(((0, x_nbr), (1, x_nbr), (1, y_nbr), (0, y_nbr))):
                sent.append(copy(12 * w + k, quarter(w, loc, None, c, q), quarter(w, out, j, c, q), to))
                sent[-1].start()

        def landed(w, out, k, chip, q, onward=None):
            piece = quarter(w, out, chip, c, q)
            copy(12 * w + k, piece, piece, me).wait_recv()
            if onward is not None:
                sent.append(copy(12 * w + onward[0], piece, piece, onward[1]))
                sent[-1].start()
            sent.append(copy(12 * w + 6 + k, piece, piece, sibling))
            sent[-1].start()

        arrays = ((0, ga_ref), (1, gb_ref))
        for w, out in arrays:
            landed(w, out, 0, j_x, 0, onward=(4, y_nbr))
            landed(w, out, 2, j_y, 1, onward=(5, x_nbr))
        for w, out in arrays:
            landed(w, out, 1, j_x, 1)
            landed(w, out, 3, j_y, 0)
            landed(w, out, 4, j_d, 0)
            landed(w, out, 5, j_d, 1)
        for w, out in arrays:
            for k, (chip, q) in enumerate(((j_x, 0), (j_x, 1), (j_y, 1), (j_y, 0), (j_d, 0), (j_d, 1))):
                piece = quarter(w, out, chip, 1 - c, q)
                copy(12 * w + 6 + k, piece, piece, me).wait_recv()
        for cp in sent:
            cp.wait_send()
        for cp in own:
            cp.wait()

    hbm = pl.BlockSpec(memory_space=pl.ANY)
    return pl.pallas_call(
        body, name="gather", in_specs=[hbm, hbm], out_specs=[hbm, hbm],
        out_shape=[jax.ShapeDtypeStruct((IN_W, D_MODEL), a_loc.dtype), jax.ShapeDtypeStruct((D_MODEL, D_MODEL), b_loc.dtype)],
        scratch_shapes=[pltpu.SemaphoreType.DMA((24,)), pltpu.SemaphoreType.DMA((24,)), pltpu.SemaphoreType.DMA((2,))],
    )(a_loc, b_loc)


def _wgrad_reduce_call(dproj_t, h, mixed_t, dout, sw, vec):
    wire = jnp.bfloat16
    half_sw = SW_ROWS // 2
    seq = h.shape[0]
    tk = min(1024, seq)
    nk = seq // tk
    n_steps = 2 * N_CHIPS
    rel_of = lambda s: s % 3 if s < 6 else 3
    half_of = lambda s: s // 3 if s < 6 else s - 6
    x, y = lax.axis_index("x"), lax.axis_index("y")
    chip_of = [2 * (1 - x) + (1 - y), 2 * (1 - x) + y, 2 * x + (1 - y), 2 * x + y]
    order = jnp.stack([2 * chip_of[rel_of(s)] + half_of(s) for s in range(n_steps)]).astype(jnp.int32)

    def body(order_ref, dpt_ref, h_hbm, mxt_ref, dout_hbm, sw_ref, small_ref, oa_ref, ob_ref, osw_ref, osmall_ref,
             h_v, dout_v, acc_a, acc_b, sib_a, sib_b, snd_a, snd_b, in_a, in_b, own_a, own_b, fin_a, fin_b,
             all_small, sw_sib, sw_chips, sw_fin, send_sems, recv_sems, loc_sems):
        x, y, c = lax.axis_index("x"), lax.axis_index("y"), lax.axis_index("c")
        me, sibling = (x, y, c), (x, y, 1 - c)
        steps = [(1 - x, 1 - y), (1 - x, y), (x, 1 - y)]
        j = 2 * x + y
        dev = 4 * x + 2 * y + c
        b, k = pl.program_id(0), pl.program_id(1)

        def copy(n, src, dst, to):
            return pltpu.make_async_remote_copy(src_ref=src, dst_ref=dst, send_sem=send_sems.at[n], recv_sem=recv_sems.at[n],
                                                device_id=to, device_id_type=MESH)

        def sw_rows(ref, hf):
            return ref.at[pl.ds(pl.multiple_of(hf * half_sw, 8), half_sw), :]

        sm_first = [copy(16, small_ref, all_small.at[dev], sibling)]
        sm_first += [copy(17 + r, small_ref, all_small.at[dev], (*chip, c)) for r, chip in enumerate(steps)]
        sw_to_sib = copy(23, sw_rows(sw_ref, 1 - c), sw_sib, sibling)

        @pl.when((b == 0) & (k == 0))
        def _():
            all_small[dev] = small_ref[...]
            for cp in sm_first + [sw_to_sib]:
                cp.start()

        def load(kk):
            rows = pl.ds(kk * tk, tk)
            return [pltpu.make_async_copy(h_hbm.at[rows, :], h_v.at[rows, :], loc_sems.at[kk]),
                    pltpu.make_async_copy(dout_hbm.at[rows, :], dout_v.at[rows, :], loc_sems.at[nk + kk])]

        for kk in range(nk):
            @pl.when((b == 0) & (k == 0))
            def _():
                for cp in load(kk):
                    cp.start()

        for kk in range(nk):
            @pl.when((b == 0) & (k == kk))
            def _():
                for cp in load(kk):
                    cp.wait()

        tok = pl.ds(pl.multiple_of(k * tk, tk), tk)
        pa = _mm(dpt_ref[...], h_v[tok, :])
        pb = _mm(mxt_ref[...], dout_v[tok, :])
        slot = b % 2

        @pl.when(k == 0)
        def _():
            acc_a[slot] = pa
            acc_b[slot] = pb

        @pl.when(k != 0)
        def _():
            acc_a[slot] += pa
            acc_b[slot] += pb

        def to_sibling(s):
            r = rel_of(s)
            return [copy(r, acc_a.at[s % 2], sib_a.at[r], sibling), copy(4 + r, acc_b.at[s % 2], sib_b.at[r], sibling)]

        def chip_partial(s):
            r = rel_of(s)
            copy(r, sib_a.at[r], sib_a.at[r], me).wait_recv()
            copy(4 + r, sib_b.at[r], sib_b.at[r], me).wait_recv()
            return acc_a[s % 2] + sib_a[r], acc_b[s % 2] + sib_b[r]

        def to_owner(r):
            return [copy(8 + r, snd_a.at[r], in_a.at[r], (*steps[r], c)), copy(11 + r, snd_b.at[r], in_b.at[r], (*steps[r], c))]

        for s in range(n_steps):
            if s >= 1:
                sp = s - 1

                @pl.when((b == s) & (k == 0) & (c == half_of(sp)))
                def _():
                    ta, tb = chip_partial(sp)
                    r = rel_of(sp)
                    if r < 3:
                        snd_a[r] = ta.astype(wire)
                        snd_b[r] = tb.astype(wire)
                        for cp in to_owner(r):
                            cp.start()
                    else:
                        own_a[...] = ta
                        own_b[...] = tb

                @pl.when((b == s) & (k == nk - 1) & (c != half_of(sp)))
                def _():
                    for cp in to_sibling(sp):
                        cp.wait_send()

            @pl.when((b == s) & (k == nk - 1) & (c != half_of(s)))
            def _():
                for cp in to_sibling(s):
                    cp.start()

        @pl.when((b == n_steps - 1) & (k == nk - 1))
        def _():
            sm_passed = []
            for r, (cx, cy) in enumerate(steps):
                d = 4 * cx + 2 * cy + c
                copy(17 + r, all_small.at[d], all_small.at[d], me).wait_recv()
                fwd = copy(20 + r, all_small.at[d], all_small.at[d], sibling)
                fwd.start()
                sm_passed.append(fwd)
            copy(23, sw_sib, sw_sib, me).wait_recv()
            sw_chips[j] = (sw_rows(sw_ref, c)[...] + sw_sib[...]).astype(wire)
            sw_ici = [copy(24 + r, sw_chips.at[j], sw_chips.at[j], (*chip, c)) for r, chip in enumerate(steps)]
            for cp in sw_ici:
                cp.start()

            last = n_steps - 1

            @pl.when(c == half_of(last))
            def _():
                own_a[...], own_b[...] = chip_partial(last)

            @pl.when(c != half_of(last))
            def _():
                for cp in to_sibling(last):
                    cp.wait_send()

            tot_a = own_a[...]
            tot_b = own_b[...]
            for s in range(3):
                copy(8 + s, in_a.at[s], in_a.at[s], me).wait_recv()
                copy(11 + s, in_b.at[s], in_b.at[s], me).wait_recv()
                tot_a = tot_a + in_a[s].astype(F32)
                tot_b = tot_b + in_b[s].astype(F32)
            mine_a = oa_ref.at[pl.ds(pl.multiple_of(c * HALF_A, 8), HALF_A), :]
            mine_b = ob_ref.at[pl.ds(pl.multiple_of(c * HALF_B, 8), HALF_B), :]
            mine_a[...] = tot_a
            mine_b[...] = tot_b
            back = [copy(14, mine_a, fin_a, sibling), copy(15, mine_b, fin_b, sibling)]
            for cp in back:
                cp.start()

            copy(16, small_ref, all_small.at[dev ^ 1], me).wait_recv()
            for r, (cx, cy) in enumerate(steps):
                d = 4 * cx + 2 * cy + (1 - c)
                copy(20 + r, all_small.at[d], all_small.at[d], me).wait_recv()
            tot = all_small[0]
            for d in range(1, N_DEV):
                tot = tot + all_small[d]
            osmall_ref[...] = tot

            for r, (cx, cy) in enumerate(steps):
                cj = 2 * cx + cy
                copy(24 + r, sw_chips.at[cj], sw_chips.at[cj], me).wait_recv()
            tot_sw = sw_chips[0].astype(F32)
            for q in range(1, N_CHIPS):
                tot_sw = tot_sw + sw_chips[q].astype(F32)
            sw_rows(osw_ref, c)[...] = tot_sw
            sw_back = copy(27, sw_rows(osw_ref, c), sw_fin, sibling)
            sw_back.start()

            copy(14, fin_a, fin_a, me).wait_recv()
            copy(15, fin_b, fin_b, me).wait_recv()
            copy(27, sw_fin, sw_fin, me).wait_recv()
            oa_ref[pl.ds(pl.multiple_of((1 - c) * HALF_A, 8), HALF_A), :] = fin_a[...]
            ob_ref[pl.ds(pl.multiple_of((1 - c) * HALF_B, 8), HALF_B), :] = fin_b[...]
            sw_rows(osw_ref, 1 - c)[...] = sw_fin[...]
            sends = sm_first + sm_passed + sw_ici + back + [sw_to_sib, sw_back]
            for s in range(3):
                sends += to_owner(s)
            for cp in sends:
                cp.wait_send()

    vmem = pl.BlockSpec(memory_space=pltpu.VMEM)
    vm = pltpu.VMEM
    grid_spec = pltpu.PrefetchScalarGridSpec(
        num_scalar_prefetch=1, grid=(n_steps, nk),
        in_specs=[pl.BlockSpec((HALF_A, tk), lambda b, k, o: (o[b], k)), pl.BlockSpec(memory_space=pl.ANY),
                  pl.BlockSpec((HALF_B, tk), lambda b, k, o: (o[b], k)), pl.BlockSpec(memory_space=pl.ANY),
                  vmem, vmem],
        out_specs=[vmem, vmem, vmem, vmem],
        scratch_shapes=[vm((seq, D_MODEL), h.dtype), vm((seq, D_MODEL), dout.dtype),
                        vm((2, HALF_A, D_MODEL), F32), vm((2, HALF_B, D_MODEL), F32),
                        vm((N_CHIPS, HALF_A, D_MODEL), F32), vm((N_CHIPS, HALF_B, D_MODEL), F32),
                        vm((3, HALF_A, D_MODEL), wire), vm((3, HALF_B, D_MODEL), wire),
                        vm((3, HALF_A, D_MODEL), wire), vm((3, HALF_B, D_MODEL), wire),
                        vm((HALF_A, D_MODEL), F32), vm((HALF_B, D_MODEL), F32),
                        vm((HALF_A, D_MODEL), F32), vm((HALF_B, D_MODEL), F32),
                        vm((N_DEV, VEC_ROWS, 128), F32), vm((half_sw, 128), F32), vm((N_CHIPS, half_sw, 128), wire),
                        vm((half_sw, 128), F32),
                        pltpu.SemaphoreType.DMA((28,)), pltpu.SemaphoreType.DMA((28,)), pltpu.SemaphoreType.DMA((2 * nk,))])
    return pl.pallas_call(
        body, name="wgrad_reduce", grid_spec=grid_spec,
        out_shape=[jax.ShapeDtypeStruct((W_IN_SHARD, D_MODEL), F32), jax.ShapeDtypeStruct((W_OUT_SHARD, D_MODEL), F32),
                   jax.ShapeDtypeStruct((SW_ROWS, 128), F32), jax.ShapeDtypeStruct((VEC_ROWS, 128), F32)],
        compiler_params=pltpu.CompilerParams(dimension_semantics=("arbitrary", "arbitrary"), vmem_limit_bytes=VMEM_LIMIT),
    )(order, dproj_t, h, mixed_t, dout, sw, vec)


def _adamw(w, g, m, v):
    nm = ADAM_B1 * m + (1.0 - ADAM_B1) * g
    nv = ADAM_B2 * v + (1.0 - ADAM_B2) * (g * g)
    m_hat = nm / (1.0 - ADAM_B1 ** ADAM_STEP)
    v_hat = nv / (1.0 - ADAM_B2 ** ADAM_STEP)
    return -ADAM_LR * (m_hat / (jnp.sqrt(v_hat) + ADAM_EPS) + ADAM_WD * w), nm, nv


def _adamw_call(w, g, m, v, block_rows, name):
    rows, cols = w.shape

    def body(w_ref, g_ref, m_ref, v_ref, go_ref, d_ref, nm_ref, nv_ref):
        gg = g_ref[...]
        go_ref[...] = gg
        d_ref[...], nm_ref[...], nv_ref[...] = _adamw(w_ref[...], gg, m_ref[...], v_ref[...])

    spec = pl.BlockSpec((block_rows, cols), lambda i: (i, 0))
    return pl.pallas_call(
        body, name=name, grid=(rows // block_rows,), in_specs=[spec] * 4, out_specs=[spec] * 4,
        out_shape=[jax.ShapeDtypeStruct((rows, cols), F32)] * 4,
        compiler_params=pltpu.CompilerParams(dimension_semantics=("arbitrary",)),
    )(w, g, m, v)


_SMALL = (("norm_g", (1, D_MODEL), R_G1), ("b_in", (1, IN_W), R_BIN), ("attn_sinks", (1, 8), R_SINK),
          ("sgu_ln_g", (1, SGU_W), R_LNG), ("sgu_ln_b", (1, SGU_W), R_LNB), ("sgu_b", (N_SGU_HEADS, BLOCK), R_SGUB),
          ("b_out", (1, D_MODEL), R_BOUT), ("final_norm_g", (1, D_MODEL), R_G3))


def _adamw_small_call(sw_g, vec_g, sgu_w3, ws, ms, vs):
    n = len(_SMALL)

    def body(*refs):
        sw_ref, vec_ref = refs[0], refs[1]
        w3 = refs[2:5]
        w_refs, m_refs, v_refs = refs[5:5 + n], refs[5 + n:5 + 2 * n], refs[5 + 2 * n:5 + 3 * n]
        outs = refs[5 + 3 * n:]
        outs[0][...] = vec_ref[R_LOSS:R_LOSS + 1, 0:1]
        g = sw_ref[...]
        outs[1][...] = g
        outs[2][...], outs[3][...], outs[4][...] = _adamw(w3[0][...], g, w3[1][...], w3[2][...])
        for k, (_, shape, row) in enumerate(_SMALL):
            if shape[0] == 1 and shape[1] >= 128:
                g = jnp.concatenate([vec_ref[row + q:row + q + 1, :] for q in range(shape[1] // 128)], axis=1)
            else:
                g = vec_ref[row:row + shape[0], 0:shape[1]]
            o = outs[5 + 4 * k:9 + 4 * k]
            o[0][...] = g
            o[1][...], o[2][...], o[3][...] = _adamw(w_refs[k][...], g, m_refs[k][...], v_refs[k][...])

    vmem = pl.BlockSpec(memory_space=pltpu.VMEM)
    out_shape = [jax.ShapeDtypeStruct((1, 1), F32)] + [jax.ShapeDtypeStruct((SW_ROWS, 128), F32)] * 4
    for _, shape, _ in _SMALL:
        out_shape += [jax.ShapeDtypeStruct(shape, F32)] * 4
    args = [sw_g, vec_g, *sgu_w3, *ws, *ms, *vs]
    return pl.pallas_call(
        body, name="adamw_small", in_specs=[vmem] * len(args), out_specs=[vmem] * len(out_shape), out_shape=out_shape,
    )(*args)


def kernel(x, norm_g, w_in, b_in, attn_sinks, sgu_ln_g, sgu_ln_b, sgu_w, sgu_b, w_out, b_out, final_norm_g, loss_target, m_norm_g, m_w_in, m_b_in, m_attn_sinks, m_sgu_ln_g, m_sgu_ln_b, m_sgu_w, m_sgu_b, m_w_out, m_b_out, m_final_norm_g, v_norm_g, v_w_in, v_b_in, v_attn_sinks, v_sgu_ln_g, v_sgu_ln_b, v_sgu_w, v_sgu_b, v_w_out, v_b_out, v_final_norm_g):
    seq = x.shape[1]
    win_t, wout = _gather_call(w_in[0].T.astype(MXU_DTYPE), w_out[0].astype(MXU_DTYPE))
    bexp = jnp.repeat(sgu_b[0].T, SGU_W // N_SGU_HEADS, axis=1)
    dx, h, dproj_t, mixed_t, dout, sw, vec = _fused_call(
        x[0], loss_target[0], attn_sinks[0], norm_g, b_in, sgu_ln_g, sgu_ln_b, sgu_w[0], bexp, b_out,
        final_norm_g.reshape(1, D_MODEL), win_t, wout)
    ga_t, g_w_out, sw, vec = _wgrad_reduce_call(dproj_t, h, mixed_t, dout, sw, vec)

    names = ["norm_g", "w_in", "b_in", "attn_sinks", "sgu_ln_g", "sgu_ln_b", "sgu_w", "sgu_b", "w_out", "b_out", "final_norm_g"]
    res = {}
    res["w_in"] = [a.T[None] for a in _adamw_call(w_in[0].T, ga_t, m_w_in[0].T, v_w_in[0].T, W_IN_SHARD // 4, "adamw_w_in")]
    res["w_out"] = [a[None] for a in _adamw_call(w_out[0], g_w_out, m_w_out[0], v_w_out[0], HALF_B, "adamw_w_out")]
    given = dict(norm_g=(norm_g, m_norm_g, v_norm_g), b_in=(b_in, m_b_in, v_b_in), attn_sinks=(attn_sinks, m_attn_sinks, v_attn_sinks),
                 sgu_ln_g=(sgu_ln_g, m_sgu_ln_g, v_sgu_ln_g), sgu_ln_b=(sgu_ln_b, m_sgu_ln_b, v_sgu_ln_b),
                 sgu_b=(sgu_b, m_sgu_b, v_sgu_b), b_out=(b_out, m_b_out, v_b_out),
                 final_norm_g=(final_norm_g, m_final_norm_g, v_final_norm_g))
    wmv = [[given[n][k].reshape(shape) for n, shape, _ in _SMALL] for k in range(3)]
    outs = _adamw_small_call(sw, vec, [a.reshape(SW_ROWS, BLOCK) for a in (sgu_w, m_sgu_w, v_sgu_w)], *wmv)
    loss = outs[0].reshape(())
    res["sgu_w"] = [a.reshape(sgu_w.shape) for a in outs[1:5]]
    for k, (n, _, _) in enumerate(_SMALL):
        res[n] = [a.reshape(given[n][0].shape) for a in outs[5 + 4 * k:9 + 4 * k]]

    return (loss, dx[None], *[res[n][0] for n in names], *[res[n][1] for n in names], *[res[n][2] for n in names],
            *[res[n][3] for n in names])
```

```python
import functools
import math

import jax
import jax.numpy as jnp
from jax import lax
from jax.experimental import pallas as pl
from jax.experimental.pallas import tpu as pltpu

F32 = jnp.float32
MXU_DTYPE = jnp.bfloat16

D_MODEL = 1024
HEAD_DIM = 64
ATTN_W = 512
SGU_W = 512
N_SGU_HEADS = 8
BLOCK = 128
IN_W = 2816
OFF_K, OFF_V, OFF_ZA, OFF_US, OFF_VS, OFF_ZS = 512, 640, 768, 1280, 1792, 2304
NORM_EPS = 1e-5
NEG_INF = -1e30
SCALE = HEAD_DIM ** -0.5
SQRT_HALF = math.sqrt(0.5)
INV_SQRT_2PI = 1.0 / math.sqrt(2.0 * math.pi)

N_CHIPS = 4
N_DEV = 8
W_IN_SHARD = IN_W // N_CHIPS
W_OUT_SHARD = D_MODEL // N_CHIPS
HALF_A = W_IN_SHARD // 2
HALF_B = W_OUT_SHARD // 2

TILE = 256
VMEM_LIMIT = 56 * 1024 * 1024

ADAM_LR, ADAM_B1, ADAM_B2, ADAM_EPS, ADAM_WD, ADAM_STEP = 0.001, 0.9, 0.999, 1e-08, 0.01, 10

SW_ROWS = N_SGU_HEADS * BLOCK
R_G1, R_BIN, R_SINK, R_LOSS, R_LNG, R_LNB, R_SGUB, R_BOUT, R_G3 = 0, 8, 32, 40, 48, 56, 64, 72, 80
VEC_ROWS = 88

MESH = pl.DeviceIdType.MESH


def _mm(a, b):
    return jnp.dot(a, b, preferred_element_type=F32)


def _mm_nt(a, b):
    return lax.dot_general(a, b, (((1,), (1,)), ((), ())), preferred_element_type=F32)


def _mm_tn(a, b):
    return lax.dot_general(a, b, (((0,), (0,)), ((), ())), preferred_element_type=F32)


def _sigmoid(z):
    return 1.0 / (1.0 + jnp.exp(-z))


def _norm_cdf(z):
    return 0.5 * (1.0 + lax.erf(z * SQRT_HALF))


def _norm_pdf(z):
    return jnp.exp(-0.5 * z * z) * INV_SQRT_2PI


def _rows8(v):
    r, n = v.shape
    return jnp.sum(v.reshape(r // 8, 8, n), axis=0)


def _mean_rows(v):
    return jnp.sum(v, axis=1, keepdims=True) * (1.0 / v.shape[1])


def _fused_call(x, proj_t, tgt, sinks, g1, b_in, ln_g, ln_b, sgu_w, bexp, b_out, g3, win_t, wout):
    seq = x.shape[0]
    t = TILE
    nt = seq // t
    nb = t // BLOCK
    act = MXU_DTYPE

    def body(sinks_ref, x_ref, pt_ref, kvh_ref, tgt_ref, g1_ref, bin_ref, lng_ref, lnb_ref, sguw_ref, bexp_ref, bout_ref, g3_ref,
             wint_hbm, wout_hbm,
             dx_ref, dprojt_ref, mixedt_ref, dout_ref, sw_ref, vec_ref,
             dproj_ref, mixed_ref, wint_v, wout_v, wf_v, wb_v, q_s, kf_s, vf_s, k2_s, v2_s, gate_s, p_s, ps_s, o_s, u_s, mix_s, vhat_s, r2_s,
             cdfu_s, cdfv_s, r1_s, doutf_s, dmix_s, dkf_s, dvf_s, carryk_s, carryv_s,
             acc_bin, acc_g1, acc_bout, acc_g3, acc_lng, acc_lnb, acc_dws, acc_dbs, acc_sink, acc_loss, sems):
        i = pl.program_id(0)
        tile = nt - 1 - i
        lane128 = lax.broadcasted_iota(jnp.int32, (BLOCK, BLOCK), 1)
        lo = lane128 < HEAD_DIM

        @pl.when(i == 0)
        def _():
            cp_a = pltpu.make_async_copy(wint_hbm, wint_v, sems.at[0])
            cp_b = pltpu.make_async_copy(wout_hbm, wout_v, sems.at[1])
            cp_a.start()
            cp_b.start()
            for acc in (acc_bin, acc_g1, acc_bout, acc_g3, acc_lng, acc_lnb, acc_dws, acc_dbs, acc_sink, acc_loss,
                        carryk_s, carryv_s):
                acc[...] = jnp.zeros(acc.shape, F32)
            tril = lax.broadcasted_iota(jnp.int32, (BLOCK, BLOCK), 0) >= lane128
            for h in range(N_SGU_HEADS):
                w = jnp.where(tril, sguw_ref[h], 0.0)
                wf_v[h // 2, :, (h % 2) * BLOCK:(h % 2 + 1) * BLOCK] = w.astype(act)
                wb_v[h // 2, :, (h % 2) * BLOCK:(h % 2 + 1) * BLOCK] = w.T.astype(act)
            cp_a.wait()
            cp_b.wait()

        g1 = g1_ref[...]

        def rms(v):
            r = lax.rsqrt(_mean_rows(v * v) + NORM_EPS)
            return r, v * r

        for c in range(nb):
            rows = slice(c * BLOCK, (c + 1) * BLOCK)
            xv = x_ref[rows, :]
            r1_s[rows, :] = lax.rsqrt(_mean_rows(xv * xv) + NORM_EPS)
        kvh = kvh_ref[...].T + bin_ref[:, OFF_K:OFF_ZA]
        kf_s[0:BLOCK, :] = kvh[:, :BLOCK]
        vf_s[0:BLOCK, :] = kvh[:, BLOCK:]

        q = pt_ref[0:OFF_K, :].T + bin_ref[:, 0:OFF_K]
        q_s[...] = (q * SCALE).astype(act)
        kv = pt_ref[OFF_K:OFF_ZA, :].T + bin_ref[:, OFF_K:OFF_ZA]
        kf_s[BLOCK:, :] = kv[:, :BLOCK]
        vf_s[BLOCK:, :] = kv[:, BLOCK:]
        for r in range(4):
            cols = slice(OFF_ZA + r * 512, OFF_ZA + (r + 1) * 512)
            gate_s[r] = pt_ref[cols, :].T + bin_ref[:, cols]

        lo_kv = lax.broadcasted_iota(jnp.int32, (t + BLOCK, BLOCK), 1) < HEAD_DIM
        for src, dst in ((kf_s, k2_s), (vf_s, v2_s)):
            v = src[...]
            vr = pltpu.roll(v, HEAD_DIM, 1)
            dst[0] = jnp.where(lo_kv, v, vr).astype(act)
            dst[1] = jnp.where(lo_kv, vr, v).astype(act)

        rowi = lax.broadcasted_iota(jnp.int32, (BLOCK, 2 * BLOCK), 0)
        colj = lax.broadcasted_iota(jnp.int32, (BLOCK, 2 * BLOCK), 1)
        in_band = (colj > rowi) & (colj <= rowi + BLOCK)
        row512 = lax.broadcasted_iota(jnp.int32, (4 * BLOCK, 1), 0)

        def stacked_q(b, g):
            parts = []
            for p in range(2):
                slab = q_s[b * BLOCK:(b + 1) * BLOCK, (2 * g + p) * BLOCK:(2 * g + p + 1) * BLOCK]
                parts += [jnp.where(lo, slab, jnp.zeros_like(slab)), jnp.where(lo, jnp.zeros_like(slab), slab)]
            return jnp.concatenate(parts, axis=0)

        def sink_col(g):
            s = [sinks_ref[4 * g + k] for k in range(4)]
            return jnp.where(row512 < BLOCK, s[0], jnp.where(row512 < 2 * BLOCK, s[1], jnp.where(row512 < 3 * BLOCK, s[2], s[3])))

        for b in range(nb):
            band = slice(b * BLOCK, (b + 2) * BLOCK)
            first_key = jnp.where(tile * nb + b > 0, 0, BLOCK)
            valid = in_band & (colj >= first_key)
            valid4 = jnp.concatenate([valid] * 4, axis=0)
            for g in range(2):
                s = _mm_nt(stacked_q(b, g), k2_s[g, band, :])
                s = jnp.where(valid4, s, NEG_INF)
                sk = sink_col(g)
                m = jnp.maximum(jnp.max(s, axis=1, keepdims=True), sk)
                p = jnp.exp(s - m)
                psk = jnp.exp(sk - m)
                inv = 1.0 / (jnp.sum(p, axis=1, keepdims=True) + psk)
                p = p * inv
                p_s[b * 2 + g] = p
                ps_s[b * 2 + g] = psk * inv
                o2 = _mm(p.astype(act), v2_s[g, band, :])
                for pr in range(2):
                    o_s[b * BLOCK:(b + 1) * BLOCK, (2 * g + pr) * BLOCK:(2 * g + pr + 1) * BLOCK] = jnp.where(
                        lo, o2[(2 * pr) * BLOCK:(2 * pr + 1) * BLOCK], o2[(2 * pr + 1) * BLOCK:(2 * pr + 2) * BLOCK])

        lng = lng_ref[...]
        lnb = lnb_ref[...]
        for c in range(nb):
            rows = slice(c * BLOCK, (c + 1) * BLOCK)
            za = gate_s[0, rows, :]
            mixed_ref[rows, 0:ATTN_W] = (o_s[rows, :] * (za * _sigmoid(za))).astype(act)
            us = gate_s[1, rows, :]
            vs = gate_s[2, rows, :]
            zs = gate_s[3, rows, :]
            cu = _norm_cdf(us)
            cv = _norm_cdf(vs)
            cdfu_s[rows, :] = cu
            cdfv_s[rows, :] = cv
            u = us * cu
            vg = vs * cv
            vc = vg - _mean_rows(vg)
            r2 = lax.rsqrt(_mean_rows(vc * vc) + NORM_EPS)
            vhat = vc * r2
            r2_s[rows, :] = r2
            vhat_s[rows, :] = vhat
            u_s[rows, :] = u
            vln = vhat * lng + lnb
            for p in range(4):
                cols = slice(p * BLOCK, (p + 1) * BLOCK)
                slab = vln[:, cols]
                rhs = jnp.concatenate([jnp.where(lo, slab, 0.0), jnp.where(lo, 0.0, slab)], axis=0).astype(act)
                mix_s[rows, cols] = _mm(wf_v[p], rhs) + bexp_ref[:, cols]
            mixed_ref[rows, ATTN_W:] = (u * mix_s[rows, :] * (zs * _sigmoid(zs))).astype(act)

        g3 = g3_ref[...]
        proj_o = _mm(mixed_ref[...], wout_v[...])
        for c in range(nb):
            rows = slice(c * BLOCK, (c + 1) * BLOCK)
            out = x_ref[rows, :] + proj_o[rows, :] + bout_ref[...]
            r3, on = rms(out)
            e = on * g3 - tgt_ref[rows, :]
            e2 = _rows8(e * e)
            acc_loss[...] += sum(e2[:, k * 128:(k + 1) * 128] for k in range(D_MODEL // 128)) * (0.5 / D_MODEL)
            dy = e * (1.0 / D_MODEL)
            acc_g3[...] += _rows8(dy * on)
            don = dy * g3
            dout = r3 * (don - on * _mean_rows(don * on))
            doutf_s[rows, :] = dout
            dout_ref[rows, :] = dout.astype(act)
            acc_bout[...] += _rows8(dout)
        dmix_s[...] = _mm_nt(dout_ref[...], wout_v[...])

        for c in range(nb):
            rows = slice(c * BLOCK, (c + 1) * BLOCK)
            dso = dmix_s[rows, ATTN_W:]
            u = u_s[rows, :]
            mix = mix_s[rows, :]
            zs = gate_s[3, rows, :]
            sg = _sigmoid(zs)
            sgs = zs * sg
            du = dso * mix * sgs
            dmx = dso * u * sgs
            dzs = dso * u * mix * (sg * (1.0 + zs * (1.0 - sg)))
            us = gate_s[1, rows, :]
            dus = du * (cdfu_s[rows, :] + us * _norm_pdf(us))
            vhat = vhat_s[rows, :]
            vln = (vhat * lng + lnb).astype(act)
            dvln_parts = []
            for p in range(4):
                cols = slice(p * BLOCK, (p + 1) * BLOCK)
                slab = dmx[:, cols]
                d_lo = jnp.where(lo, slab, 0.0).astype(act)
                d_hi = jnp.where(lo, 0.0, slab).astype(act)
                dvln_parts.append(_mm(wb_v[p], jnp.concatenate([d_lo, d_hi], axis=0)))
                acc_dws[(2 * p) * BLOCK:(2 * p + 1) * BLOCK, :] += _mm_nt(d_lo, vln[:, cols])
                acc_dws[(2 * p + 1) * BLOCK:(2 * p + 2) * BLOCK, :] += _mm_nt(d_hi, vln[:, cols])
            dvln = jnp.concatenate(dvln_parts, axis=1)
            acc_dbs[...] += dmx
            acc_lng[...] += _rows8(dvln * vhat)
            acc_lnb[...] += _rows8(dvln)
            dvhat = dvln * lng
            dvg = r2_s[rows, :] * (dvhat - _mean_rows(dvhat) - vhat * _mean_rows(dvhat * vhat))
            vs = gate_s[2, rows, :]
            dvs = dvg * (cdfv_s[rows, :] + vs * _norm_pdf(vs))
            for off, val in ((OFF_US, dus), (OFF_VS, dvs), (OFF_ZS, dzs)):
                dproj_ref[rows, off:off + 512] = val.astype(act)
                acc_bin[:, off:off + 512] += _rows8(val)

        dkf_s[...] = jnp.zeros(dkf_s.shape, F32)
        dvf_s[...] = jnp.zeros(dvf_s.shape, F32)
        for b in range(nb):
            rows = slice(b * BLOCK, (b + 1) * BLOCK)
            band = slice(b * BLOCK, (b + 2) * BLOCK)
            za = gate_s[0, rows, :]
            sg = _sigmoid(za)
            dao = dmix_s[rows, 0:ATTN_W]
            o = o_s[rows, :]
            do = dao * (za * sg)
            dza = dao * o * (sg * (1.0 + za * (1.0 - sg)))
            dproj_ref[rows, OFF_ZA:OFF_US] = dza.astype(act)
            acc_bin[:, OFF_ZA:OFF_US] += _rows8(dza)
            for g in range(2):
                do_parts, delta_parts = [], []
                for pr in range(2):
                    cols = slice((2 * g + pr) * BLOCK, (2 * g + pr + 1) * BLOCK)
                    d_pair = do[:, cols]
                    prod = d_pair * o[:, cols]
                    do_parts += [jnp.where(lo, d_pair, 0.0).astype(act), jnp.where(lo, 0.0, d_pair).astype(act)]
                    delta_parts += [jnp.sum(jnp.where(lo, prod, 0.0), axis=1, keepdims=True),
                                    jnp.sum(jnp.where(lo, 0.0, prod), axis=1, keepdims=True)]
                do_st = jnp.concatenate(do_parts, axis=0)
                delta = jnp.concatenate(delta_parts, axis=0)
                p = p_s[b * 2 + g]
                dp = _mm_nt(do_st, v2_s[g, band, :])
                ds = p * (dp - delta)
                sink_t = ps_s[b * 2 + g] * delta
                for k in range(4):
                    acc_sink[4 * g + k:4 * g + k + 1, :] += -jnp.sum(sink_t[k * BLOCK:(k + 1) * BLOCK], axis=0, keepdims=True)
                ds_a = ds.astype(act)
                dq2 = _mm(ds_a, k2_s[g, band, :]) * SCALE
                for pr in range(2):
                    cols = slice((2 * g + pr) * BLOCK, (2 * g + pr + 1) * BLOCK)
                    dq = jnp.where(lo, dq2[(2 * pr) * BLOCK:(2 * pr + 1) * BLOCK], dq2[(2 * pr + 1) * BLOCK:(2 * pr + 2) * BLOCK])
                    dproj_ref[rows, cols] = dq.astype(act)
                    acc_bin[:, cols] += _rows8(dq)
                lo_band = lax.broadcasted_iota(jnp.int32, (2 * BLOCK, BLOCK), 1) < HEAD_DIM
                mine = lo_band if g == 0 else jnp.logical_not(lo_band)
                for acc, lhs, rhs in ((dkf_s, ds_a, stacked_q(b, g)), (dvf_s, p.astype(act), do_st)):
                    d2 = _mm_tn(lhs, rhs)
                    full = d2 + pltpu.roll(d2, HEAD_DIM, 1)
                    acc[band, :] += jnp.where(mine, full, 0.0)
        for acc, carry, off in ((dkf_s, carryk_s, OFF_K), (dvf_s, carryv_s, OFF_V)):
            acc[t:t + BLOCK, :] += carry[...]
            carry[...] = acc[0:BLOCK, :]
            d = acc[BLOCK:, :]
            dproj_ref[:, off:off + BLOCK] = d.astype(act)
            acc_bin[:, off:off + BLOCK] += _rows8(d)

        dh = _mm(dproj_ref[...], wint_v[...])
        dprojt_ref[...] = dproj_ref[...].T
        mixedt_ref[...] = mixed_ref[...].T
        for c in range(nb):
            rows = slice(c * BLOCK, (c + 1) * BLOCK)
            r1 = r1_s[rows, :]
            xn = x_ref[rows, :] * r1
            dhc = dh[rows, :]
            acc_g1[...] += _rows8(dhc * xn)
            dxn = dhc * g1
            dx_ref[rows, :] = doutf_s[rows, :] + r1 * (dxn - xn * _mean_rows(dxn * xn))

        @pl.when(i == nt - 1)
        def _():
            tril = lax.broadcasted_iota(jnp.int32, (BLOCK, BLOCK), 0) >= lane128
            for hh in range(N_SGU_HEADS):
                rws = slice(hh * BLOCK, (hh + 1) * BLOCK)
                sw_ref[rws, :] = jnp.where(tril, acc_dws[rws, :], 0.0)
            vec_ref[...] = jnp.zeros((VEC_ROWS, 128), F32)

            def put(row0, acc):
                s = jnp.sum(acc[...], axis=0, keepdims=True)
                for k in range(acc.shape[1] // 128):
                    vec_ref[row0 + k:row0 + k + 1, :] = s[:, k * 128:(k + 1) * 128]

            put(R_G1, acc_g1)
            put(R_BIN, acc_bin)
            put(R_LNG, acc_lng)
            put(R_LNB, acc_lnb)
            put(R_BOUT, acc_bout)
            put(R_G3, acc_g3)
            vec_ref[R_SINK:R_SINK + 1, :] = jnp.sum(
                jnp.where(lax.broadcasted_iota(jnp.int32, (8, 128), 0) == lax.broadcasted_iota(jnp.int32, (8, 128), 1),
                          acc_sink[...], 0.0), axis=0, keepdims=True)
            vec_ref[R_LOSS:R_LOSS + 1, :] = jnp.zeros((1, 128), F32) + jnp.sum(acc_loss[...])
            dbs_t = acc_dbs[...].T
            vec_ref[R_SGUB:R_SGUB + 8, :] = jnp.sum(dbs_t.reshape(N_SGU_HEADS, SGU_W // N_SGU_HEADS, BLOCK), axis=1)

    full = lambda shape: pl.BlockSpec(shape, lambda i: (0,) * len(shape))
    tok = lambda w: pl.BlockSpec((t, w), lambda i: (nt - 1 - i, 0))
    in_specs = [
        pl.BlockSpec(memory_space=pltpu.SMEM),
        tok(D_MODEL),
        pl.BlockSpec((IN_W, t), lambda i: (0, nt - 1 - i)),
        pl.BlockSpec((2 * BLOCK, BLOCK), lambda i: (OFF_K // (2 * BLOCK), jnp.maximum((nt - 1 - i) * nb - 1, 0))),
        tok(D_MODEL),
        full((1, D_MODEL)), full((1, IN_W)), full((1, SGU_W)), full((1, SGU_W)),
        full((N_SGU_HEADS, BLOCK, BLOCK)), full((BLOCK, SGU_W)), full((1, D_MODEL)), full((1, D_MODEL)),
        pl.BlockSpec(memory_space=pl.ANY), pl.BlockSpec(memory_space=pl.ANY),
    ]
    out_shape = [
        jax.ShapeDtypeStruct((seq, D_MODEL), F32),
        jax.ShapeDtypeStruct((IN_W, seq), act),
        jax.ShapeDtypeStruct((D_MODEL, seq), act),
        jax.ShapeDtypeStruct((seq, D_MODEL), act),
        jax.ShapeDtypeStruct((SW_ROWS, 128), F32),
        jax.ShapeDtypeStruct((VEC_ROWS, 128), F32),
    ]
    tok_t = lambda w: pl.BlockSpec((w, t), lambda i: (0, nt - 1 - i))
    out_specs = [tok(D_MODEL), tok_t(IN_W), tok_t(D_MODEL), tok(D_MODEL), full((SW_ROWS, 128)), full((VEC_ROWS, 128))]
    vm = pltpu.VMEM
    scratch = [
        vm((t, IN_W), act), vm((t, D_MODEL), act),
        vm((IN_W, D_MODEL), act), vm((D_MODEL, D_MODEL), act),
        vm((4, BLOCK, 2 * BLOCK), act), vm((4, BLOCK, 2 * BLOCK), act),
        vm((t, ATTN_W), act),
        vm((t + BLOCK, BLOCK), F32), vm((t + BLOCK, BLOCK), F32),
        vm((2, t + BLOCK, BLOCK), act), vm((2, t + BLOCK, BLOCK), act),
        vm((4, t, 512), F32),
        vm((2 * nb, 4 * BLOCK, 2 * BLOCK), F32), vm((2 * nb, 4 * BLOCK, 1), F32),
        vm((t, ATTN_W), F32), vm((t, SGU_W), F32), vm((t, SGU_W), F32), vm((t, SGU_W), F32), vm((t, 1), F32),
        vm((t, SGU_W), F32), vm((t, SGU_W), F32), vm((t, 1), F32),
        vm((t, D_MODEL), F32), vm((t, D_MODEL), F32),
        vm((t + BLOCK, BLOCK), F32), vm((t + BLOCK, BLOCK), F32), vm((BLOCK, BLOCK), F32), vm((BLOCK, BLOCK), F32),
        vm((8, IN_W), F32), vm((8, D_MODEL), F32), vm((8, D_MODEL), F32), vm((8, D_MODEL), F32),
        vm((8, SGU_W), F32), vm((8, SGU_W), F32), vm((N_SGU_HEADS * BLOCK, BLOCK), F32), vm((BLOCK, SGU_W), F32),
        vm((8, 128), F32), vm((8, 128), F32),
        pltpu.SemaphoreType.DMA((2,)),
    ]
    return pl.pallas_call(
        body, name="fused", grid=(nt,), in_specs=in_specs, out_specs=out_specs, out_shape=out_shape,
        scratch_shapes=scratch,
        compiler_params=pltpu.CompilerParams(dimension_semantics=("arbitrary",), vmem_limit_bytes=VMEM_LIMIT),
    )(sinks, x, proj_t, proj_t, tgt, g1, b_in, ln_g, ln_b, sgu_w, bexp, b_out, g3, win_t, wout)


def _place():
    x, y, c = lax.axis_index("x"), lax.axis_index("y"), lax.axis_index("c")
    chips = [(1 - x, y), (x, 1 - y), (1 - x, 1 - y)]
    return x, y, c, chips


def _gather_proj_call(x, g1, a_loc, b_loc):
    seq = x.shape[0]
    act = a_loc.dtype
    xc = min(256, seq)
    tok = min(1024, seq)
    n_xc, n_tok = seq // xc, seq // tok

    def body(x_hbm, g1_ref, a_ref, b_ref, ga_ref, gb_ref, h_hbm, pt_hbm, h_v, x_buf, w_buf, out_buf,
             send_sems, recv_sems, loc_sems):
        x, y, c, chips = _place()
        me, sibling = (x, y, c), (x, y, 1 - c)
        j = 2 * x + y

        def half(which, cj, hf):
            ref, shard = (ga_ref, W_IN_SHARD) if which == 0 else (gb_ref, W_OUT_SHARD)
            n = shard // 2
            return ref.at[pl.ds(pl.multiple_of(cj * shard + hf * n, 16), n), :]

        def copy(k, src, dst, to):
            return pltpu.make_async_remote_copy(src_ref=src, dst_ref=dst, send_sem=send_sems.at[k], recv_sem=recv_sems.at[k],
                                                device_id=to, device_id_type=MESH)

        own = [pltpu.make_async_copy(a_ref, ga_ref.at[pl.ds(pl.multiple_of(j * W_IN_SHARD, 16), W_IN_SHARD), :], loc_sems.at[0]),
               pltpu.make_async_copy(b_ref, gb_ref.at[pl.ds(pl.multiple_of(j * W_OUT_SHARD, 16), W_OUT_SHARD), :], loc_sems.at[1])]
        for cp in own:
            cp.start()
        srcs = [a_ref.at[pl.ds(pl.multiple_of(c * HALF_A, 16), HALF_A), :], b_ref.at[pl.ds(pl.multiple_of(c * HALF_B, 16), HALF_B), :]]
        sent = [copy(2 * r + w, srcs[w], half(w, j, c), (*chip, c)) for r, chip in enumerate(chips) for w in range(2)]
        for cp in sent:
            cp.start()

        def x_load(n):
            return pltpu.make_async_copy(x_hbm.at[pl.ds(n * xc, xc), :], x_buf.at[n % 2], loc_sems.at[2 + n % 2])

        g1 = g1_ref[...]
        x_load(0).start()
        for n in range(n_xc):
            if n + 1 < n_xc:
                x_load(n + 1).start()
            x_load(n).wait()
            xv = x_buf[n % 2]
            r = lax.rsqrt(_mean_rows(xv * xv) + NORM_EPS)
            h_v[n * xc:(n + 1) * xc, :] = (xv * r * g1).astype(act)
        h_out = pltpu.make_async_copy(h_v, h_hbm, loc_sems.at[4])
        h_out.start()

        stores = []

        def project(n, w_src, cj, hf):
            w_load = pltpu.make_async_copy(w_src, w_buf, loc_sems.at[5])
            w_load.start()
            w_load.wait()
            for q in range(n_tok):
                slot = (n * n_tok + q) % 2
                if len(stores) >= 2:
                    stores[-2].wait()
                out_buf[slot] = _mm_nt(w_buf[...], h_v[q * tok:(q + 1) * tok, :])
                dst = pt_hbm.at[pl.ds(pl.multiple_of(cj * W_IN_SHARD + hf * HALF_A, 16), HALF_A), pl.ds(q * tok, tok)]
                stores.append(pltpu.make_async_copy(out_buf.at[slot], dst, loc_sems.at[6 + slot]))
                stores[-1].start()

        project(0, srcs[0], j, c)
        project(1, a_ref.at[pl.ds(pl.multiple_of((1 - c) * HALF_A, 16), HALF_A), :], j, 1 - c)
        for r, (cx, cy) in enumerate(chips):
            cj = 2 * cx + cy
            for w in range(2):
                copy(2 * r + w, half(w, cj, c), half(w, cj, c), me).wait_recv()
                sent.append(copy(6 + 2 * r + w, half(w, cj, c), half(w, cj, c), sibling))
                sent[-1].start()
            project(2 + r, half(0, cj, c), cj, c)
        for r, (cx, cy) in enumerate(chips):
            cj = 2 * cx + cy
            for w in range(2):
                copy(6 + 2 * r + w, half(w, cj, 1 - c), half(w, cj, 1 - c), me).wait_recv()
            project(5 + r, half(0, cj, 1 - c), cj, 1 - c)

        for cp in stores[-2:]:
            cp.wait()
        h_out.wait()
        for cp in sent:
            cp.wait_send()
        for cp in own:
            cp.wait()

    hbm = pl.BlockSpec(memory_space=pl.ANY)
    vm = pltpu.VMEM
    return pl.pallas_call(
        body, name="gather_proj", in_specs=[hbm, pl.BlockSpec(memory_space=vm), hbm, hbm], out_specs=[hbm, hbm, hbm, hbm],
        out_shape=[jax.ShapeDtypeStruct((IN_W, D_MODEL), act), jax.ShapeDtypeStruct((D_MODEL, D_MODEL), b_loc.dtype),
                   jax.ShapeDtypeStruct((seq, D_MODEL), act), jax.ShapeDtypeStruct((IN_W, seq), F32)],
        scratch_shapes=[vm((seq, D_MODEL), act), vm((2, xc, D_MODEL), F32), vm((HALF_A, D_MODEL), act), vm((2, HALF_A, tok), F32),
                        pltpu.SemaphoreType.DMA((12,)), pltpu.SemaphoreType.DMA((12,)), pltpu.SemaphoreType.DMA((8,))],
        compiler_params=pltpu.CompilerParams(vmem_limit_bytes=VMEM_LIMIT),
    )(x, g1, a_loc, b_loc)


def _wgrad_reduce_call(dproj_t, h, mixed_t, dout, sw, vec):
    wire = jnp.bfloat16
    half_sw = SW_ROWS // 2
    seq = h.shape[0]
    tk = min(1024, seq)
    nk = seq // tk
    n_steps = 2 * N_CHIPS
    rel_of = lambda s: s % 3 if s < 6 else 3
    half_of = lambda s: s // 3 if s < 6 else s - 6
    x, y = lax.axis_index("x"), lax.axis_index("y")
    chip_of = [2 * (1 - x) + (1 - y), 2 * (1 - x) + y, 2 * x + (1 - y), 2 * x + y]
    order = jnp.stack([2 * chip_of[rel_of(s)] + half_of(s) for s in range(n_steps)]).astype(jnp.int32)

    def body(order_ref, dpt_ref, h_hbm, mxt_ref, dout_hbm, sw_ref, small_ref, oa_ref, ob_ref, osw_ref, osmall_ref,
             h_v, dout_v, acc_a, acc_b, sib_a, sib_b, snd_a, snd_b, in_a, in_b, own_a, own_b, fin_a, fin_b,
             all_small, sw_sib, sw_chips, sw_fin, send_sems, recv_sems, loc_sems):
        x, y, c = lax.axis_index("x"), lax.axis_index("y"), lax.axis_index("c")
        me, sibling = (x, y, c), (x, y, 1 - c)
        steps = [(1 - x, 1 - y), (1 - x, y), (x, 1 - y)]
        j = 2 * x + y
        dev = 4 * x + 2 * y + c
        b, k = pl.program_id(0), pl.program_id(1)

        def copy(n, src, dst, to):
            return pltpu.make_async_remote_copy(src_ref=src, dst_ref=dst, send_sem=send_sems.at[n], recv_sem=recv_sems.at[n],
                                                device_id=to, device_id_type=MESH)

        def sw_rows(ref, hf):
            return ref.at[pl.ds(pl.multiple_of(hf * half_sw, 8), half_sw), :]

        sm_first = [copy(16, small_ref, all_small.at[dev], sibling)]
        sm_first += [copy(17 + r, small_ref, all_small.at[dev], (*chip, c)) for r, chip in enumerate(steps)]
        sw_to_sib = copy(23, sw_rows(sw_ref, 1 - c), sw_sib, sibling)

        @pl.when((b == 0) & (k == 0))
        def _():
            all_small[dev] = small_ref[...]
            for cp in sm_first + [sw_to_sib]:
                cp.start()

        def load(kk):
            rows = pl.ds(kk * tk, tk)
            return [pltpu.make_async_copy(h_hbm.at[rows, :], h_v.at[rows, :], loc_sems.at[kk]),
                    pltpu.make_async_copy(dout_hbm.at[rows, :], dout_v.at[rows, :], loc_sems.at[nk + kk])]

        for kk in range(nk):
            @pl.when((b == 0) & (k == 0))
            def _():
                for cp in load(kk):
                    cp.start()

        for kk in range(nk):
            @pl.when((b == 0) & (k == kk))
            def _():
                for cp in load(kk):
                    cp.wait()

        tok = pl.ds(pl.multiple_of(k * tk, tk), tk)
        pa = _mm(dpt_ref[...], h_v[tok, :])
        pb = _mm(mxt_ref[...], dout_v[tok, :])
        slot = b % 2

        @pl.when(k == 0)
        def _():
            acc_a[slot] = pa
            acc_b[slot] = pb

        @pl.when(k != 0)
        def _():
            acc_a[slot] += pa
            acc_b[slot] += pb

        def to_sibling(s):
            r = rel_of(s)
            return [copy(r, acc_a.at[s % 2], sib_a.at[r], sibling), copy(4 + r, acc_b.at[s % 2], sib_b.at[r], sibling)]

        def chip_partial(s):
            r = rel_of(s)
            copy(r, sib_a.at[r], sib_a.at[r], me).wait_recv()
            copy(4 + r, sib_b.at[r], sib_b.at[r], me).wait_recv()
            return acc_a[s % 2] + sib_a[r], acc_b[s % 2] + sib_b[r]

        def to_owner(r):
            return [copy(8 + r, snd_a.at[r], in_a.at[r], (*steps[r], c)), copy(11 + r, snd_b.at[r], in_b.at[r], (*steps[r], c))]

        for s in range(n_steps):
            if s >= 1:
                sp = s - 1

                @pl.when((b == s) & (k == 0) & (c == half_of(sp)))
                def _():
                    ta, tb = chip_partial(sp)
                    r = rel_of(sp)
                    if r < 3:
                        snd_a[r] = ta.astype(wire)
                        snd_b[r] = tb.astype(wire)
                        for cp in to_owner(r):
                            cp.start()
                    else:
                        own_a[...] = ta
                        own_b[...] = tb

                @pl.when((b == s) & (k == nk - 1) & (c != half_of(sp)))
                def _():
                    for cp in to_sibling(sp):
                        cp.wait_send()

            @pl.when((b == s) & (k == nk - 1) & (c != half_of(s)))
            def _():
                for cp in to_sibling(s):
                    cp.start()

        @pl.when((b == n_steps - 1) & (k == nk - 1))
        def _():
            sm_passed = []
            for r, (cx, cy) in enumerate(steps):
                d = 4 * cx + 2 * cy + c
                copy(17 + r, all_small.at[d], all_small.at[d], me).wait_recv()
                fwd = copy(20 + r, all_small.at[d], all_small.at[d], sibling)
                fwd.start()
                sm_passed.append(fwd)
            copy(23, sw_sib, sw_sib, me).wait_recv()
            sw_chips[j] = (sw_rows(sw_ref, c)[...] + sw_sib[...]).astype(wire)
            sw_ici = [copy(24 + r, sw_chips.at[j], sw_chips.at[j], (*chip, c)) for r, chip in enumerate(steps)]
            for cp in sw_ici:
                cp.start()

            last = n_steps - 1

            @pl.when(c == half_of(last))
            def _():
                own_a[...], own_b[...] = chip_partial(last)

            @pl.when(c != half_of(last))
            def _():
                for cp in to_sibling(last):
                    cp.wait_send()

            tot_a = own_a[...]
            tot_b = own_b[...]
            for s in range(3):
                copy(8 + s, in_a.at[s], in_a.at[s], me).wait_recv()
                copy(11 + s, in_b.at[s], in_b.at[s], me).wait_recv()
                tot_a = tot_a + in_a[s].astype(F32)
                tot_b = tot_b + in_b[s].astype(F32)
            mine_a = oa_ref.at[pl.ds(pl.multiple_of(c * HALF_A, 8), HALF_A), :]
            mine_b = ob_ref.at[pl.ds(pl.multiple_of(c * HALF_B, 8), HALF_B), :]
            mine_a[...] = tot_a
            mine_b[...] = tot_b
            back = [copy(14, mine_a, fin_a, sibling), copy(15, mine_b, fin_b, sibling)]
            for cp in back:
                cp.start()

            copy(16, small_ref, all_small.at[dev ^ 1], me).wait_recv()
            for r, (cx, cy) in enumerate(steps):
                d = 4 * cx + 2 * cy + (1 - c)
                copy(20 + r, all_small.at[d], all_small.at[d], me).wait_recv()
            tot = all_small[0]
            for d in range(1, N_DEV):
                tot = tot + all_small[d]
            osmall_ref[...] = tot

            for r, (cx, cy) in enumerate(steps):
                cj = 2 * cx + cy
                copy(24 + r, sw_chips.at[cj], sw_chips.at[cj], me).wait_recv()
            tot_sw = sw_chips[0].astype(F32)
            for q in range(1, N_CHIPS):
                tot_sw = tot_sw + sw_chips[q].astype(F32)
            sw_rows(osw_ref, c)[...] = tot_sw
            sw_back = copy(27, sw_rows(osw_ref, c), sw_fin, sibling)
            sw_back.start()

            copy(14, fin_a, fin_a, me).wait_recv()
            copy(15, fin_b, fin_b, me).wait_recv()
            copy(27, sw_fin, sw_fin, me).wait_recv()
            oa_ref[pl.ds(pl.multiple_of((1 - c) * HALF_A, 8), HALF_A), :] = fin_a[...]
            ob_ref[pl.ds(pl.multiple_of((1 - c) * HALF_B, 8), HALF_B), :] = fin_b[...]
            sw_rows(osw_ref, 1 - c)[...] = sw_fin[...]
            sends = sm_first + sm_passed + sw_ici + back + [sw_to_sib, sw_back]
            for s in range(3):
                sends += to_owner(s)
            for cp in sends:
                cp.wait_send()

    vmem = pl.BlockSpec(memory_space=pltpu.VMEM)
    vm = pltpu.VMEM
    grid_spec = pltpu.PrefetchScalarGridSpec(
        num_scalar_prefetch=1, grid=(n_steps, nk),
        in_specs=[pl.BlockSpec((HALF_A, tk), lambda b, k, o: (o[b], k)), pl.BlockSpec(memory_space=pl.ANY),
                  pl.BlockSpec((HALF_B, tk), lambda b, k, o: (o[b], k)), pl.BlockSpec(memory_space=pl.ANY),
                  vmem, vmem],
        out_specs=[vmem, vmem, vmem, vmem],
        scratch_shapes=[vm((seq, D_MODEL), h.dtype), vm((seq, D_MODEL), dout.dtype),
                        vm((2, HALF_A, D_MODEL), F32), vm((2, HALF_B, D_MODEL), F32),
                        vm((N_CHIPS, HALF_A, D_MODEL), F32), vm((N_CHIPS, HALF_B, D_MODEL), F32),
                        vm((3, HALF_A, D_MODEL), wire), vm((3, HALF_B, D_MODEL), wire),
                        vm((3, HALF_A, D_MODEL), wire), vm((3, HALF_B, D_MODEL), wire),
                        vm((HALF_A, D_MODEL), F32), vm((HALF_B, D_MODEL), F32),
                        vm((HALF_A, D_MODEL), F32), vm((HALF_B, D_MODEL), F32),
                        vm((N_DEV, VEC_ROWS, 128), F32), vm((half_sw, 128), F32), vm((N_CHIPS, half_sw, 128), wire),
                        vm((half_sw, 128), F32),
                        pltpu.SemaphoreType.DMA((28,)), pltpu.SemaphoreType.DMA((28,)), pltpu.SemaphoreType.DMA((2 * nk,))])
    return pl.pallas_call(
        body, name="wgrad_reduce", grid_spec=grid_spec,
        out_shape=[jax.ShapeDtypeStruct((W_IN_SHARD, D_MODEL), F32), jax.ShapeDtypeStruct((W_OUT_SHARD, D_MODEL), F32),
                   jax.ShapeDtypeStruct((SW_ROWS, 128), F32), jax.ShapeDtypeStruct((VEC_ROWS, 128), F32)],
        compiler_params=pltpu.CompilerParams(dimension_semantics=("arbitrary", "arbitrary"), vmem_limit_bytes=VMEM_LIMIT),
    )(order, dproj_t, h, mixed_t, dout, sw, vec)


def _adamw(w, g, m, v):
    nm = ADAM_B1 * m + (1.0 - ADAM_B1) * g
    nv = ADAM_B2 * v + (1.0 - ADAM_B2) * (g * g)
    m_hat = nm / (1.0 - ADAM_B1 ** ADAM_STEP)
    v_hat = nv / (1.0 - ADAM_B2 ** ADAM_STEP)
    return -ADAM_LR * (m_hat / (jnp.sqrt(v_hat) + ADAM_EPS) + ADAM_WD * w), nm, nv


def _adamw_call(w, g, m, v, block_rows, name):
    rows, cols = w.shape

    def body(w_ref, g_ref, m_ref, v_ref, go_ref, d_ref, nm_ref, nv_ref):
        gg = g_ref[...]
        go_ref[...] = gg
        d_ref[...], nm_ref[...], nv_ref[...] = _adamw(w_ref[...], gg, m_ref[...], v_ref[...])

    spec = pl.BlockSpec((block_rows, cols), lambda i: (i, 0))
    return pl.pallas_call(
        body, name=name, grid=(rows // block_rows,), in_specs=[spec] * 4, out_specs=[spec] * 4,
        out_shape=[jax.ShapeDtypeStruct((rows, cols), F32)] * 4,
        compiler_params=pltpu.CompilerParams(dimension_semantics=("arbitrary",)),
    )(w, g, m, v)


_SMALL = (("norm_g", (1, D_MODEL), R_G1), ("b_in", (1, IN_W), R_BIN), ("attn_sinks", (1, 8), R_SINK),
          ("sgu_ln_g", (1, SGU_W), R_LNG), ("sgu_ln_b", (1, SGU_W), R_LNB), ("sgu_b", (N_SGU_HEADS, BLOCK), R_SGUB),
          ("b_out", (1, D_MODEL), R_BOUT), ("final_norm_g", (1, D_MODEL), R_G3))


def _adamw_small_call(sw_g, vec_g, sgu_w3, ws, ms, vs):
    n = len(_SMALL)

    def body(*refs):
        sw_ref, vec_ref = refs[0], refs[1]
        w3 = refs[2:5]
        w_refs, m_refs, v_refs = refs[5:5 + n], refs[5 + n:5 + 2 * n], refs[5 + 2 * n:5 + 3 * n]
        outs = refs[5 + 3 * n:]
        outs[0][...] = vec_ref[R_LOSS:R_LOSS + 1, 0:1]
        g = sw_ref[...]
        outs[1][...] = g
        outs[2][...], outs[3][...], outs[4][...] = _adamw(w3[0][...], g, w3[1][...], w3[2][...])
        for k, (_, shape, row) in enumerate(_SMALL):
            if shape[0] == 1 and shape[1] >= 128:
                g = jnp.concatenate([vec_ref[row + q:row + q + 1, :] for q in range(shape[1] // 128)], axis=1)
            else:
                g = vec_ref[row:row + shape[0], 0:shape[1]]
            o = outs[5 + 4 * k:9 + 4 * k]
            o[0][...] = g
            o[1][...], o[2][...], o[3][...] = _adamw(w_refs[k][...], g, m_refs[k][...], v_refs[k][...])

    vmem = pl.BlockSpec(memory_space=pltpu.VMEM)
    out_shape = [jax.ShapeDtypeStruct((1, 1), F32)] + [jax.ShapeDtypeStruct((SW_ROWS, 128), F32)] * 4
    for _, shape, _ in _SMALL:
        out_shape += [jax.ShapeDtypeStruct(shape, F32)] * 4
    args = [sw_g, vec_g, *sgu_w3, *ws, *ms, *vs]
    return pl.pallas_call(
        body, name="adamw_small", in_specs=[vmem] * len(args), out_specs=[vmem] * len(out_shape), out_shape=out_shape,
    )(*args)


def kernel(x, norm_g, w_in, b_in, attn_sinks, sgu_ln_g, sgu_ln_b, sgu_w, sgu_b, w_out, b_out, final_norm_g, loss_target, m_norm_g, m_w_in, m_b_in, m_attn_sinks, m_sgu_ln_g, m_sgu_ln_b, m_sgu_w, m_sgu_b, m_w_out, m_b_out, m_final_norm_g, v_norm_g, v_w_in, v_b_in, v_attn_sinks, v_sgu_ln_g, v_sgu_ln_b, v_sgu_w, v_sgu_b, v_w_out, v_b_out, v_final_norm_g):
    seq = x.shape[1]
    win_t, wout, h, proj_t = _gather_proj_call(x[0], norm_g, w_in[0].T.astype(MXU_DTYPE), w_out[0].astype(MXU_DTYPE))
    bexp = jnp.repeat(sgu_b[0].T, SGU_W // N_SGU_HEADS, axis=1)
    dx, dproj_t, mixed_t, dout, sw, vec = _fused_call(
        x[0], proj_t, loss_target[0], attn_sinks[0], norm_g, b_in, sgu_ln_g, sgu_ln_b, sgu_w[0], bexp, b_out,
        final_norm_g.reshape(1, D_MODEL), win_t, wout)
    ga_t, g_w_out, sw, vec = _wgrad_reduce_call(dproj_t, h, mixed_t, dout, sw, vec)

    names = ["norm_g", "w_in", "b_in", "attn_sinks", "sgu_ln_g", "sgu_ln_b", "sgu_w", "sgu_b", "w_out", "b_out", "final_norm_g"]
    res = {}
    res["w_in"] = [a.T[None] for a in _adamw_call(w_in[0].T, ga_t, m_w_in[0].T, v_w_in[0].T, W_IN_SHARD // 4, "adamw_w_in")]
    res["w_out"] = [a[None] for a in _adamw_call(w_out[0], g_w_out, m_w_out[0], v_w_out[0], HALF_B, "adamw_w_out")]
    given = dict(norm_g=(norm_g, m_norm_g, v_norm_g), b_in=(b_in, m_b_in, v_b_in), attn_sinks=(attn_sinks, m_attn_sinks, v_attn_sinks),
                 sgu_ln_g=(sgu_ln_g, m_sgu_ln_g, v_sgu_ln_g), sgu_ln_b=(sgu_ln_b, m_sgu_ln_b, v_sgu_ln_b),
                 sgu_b=(sgu_b, m_sgu_b, v_sgu_b), b_out=(b_out, m_b_out, v_b_out),
                 final_norm_g=(final_norm_g, m_final_norm_g, v_final_norm_g))
    wmv = [[given[n][k].reshape(shape) for n, shape, _ in _SMALL] for k in range(3)]
    outs = _adamw_small_call(sw, vec, [a.reshape(SW_ROWS, BLOCK) for a in (sgu_w, m_sgu_w, v_sgu_w)], *wmv)
    loss = outs[0].reshape(())
    res["sgu_w"] = [a.reshape(sgu_w.shape) for a in outs[1:5]]
    for k, (n, _, _) in enumerate(_SMALL):
        res[n] = [a.reshape(given[n][0].shape) for a in outs[5 + 4 * k:9 + 4 * k]]

    return (loss, dx[None], *[res[n][0] for n in names], *[res[n][1] for n in names], *[res[n][2] for n in names],
            *[res[n][3] for n in names])
```

```python
import functools
import math

import jax
import jax.numpy as jnp
from jax import lax
from jax.experimental import pallas as pl
from jax.experimental.pallas import tpu as pltpu

F32 = jnp.float32
MXU_DTYPE = jnp.bfloat16

D_MODEL = 1024
HEAD_DIM = 64
ATTN_W = 512
SGU_W = 512
N_SGU_HEADS = 8
BLOCK = 128
IN_W = 2816
OFF_K, OFF_V, OFF_ZA, OFF_US, OFF_VS, OFF_ZS = 512, 640, 768, 1280, 1792, 2304
NORM_EPS = 1e-5
NEG_INF = -1e30
SCALE = HEAD_DIM ** -0.5
SQRT_HALF = math.sqrt(0.5)
INV_SQRT_2PI = 1.0 / math.sqrt(2.0 * math.pi)

N_CHIPS = 4
N_DEV = 8
W_IN_SHARD = IN_W // N_CHIPS
W_OUT_SHARD = D_MODEL // N_CHIPS
HALF_A = W_IN_SHARD // 2
HALF_B = W_OUT_SHARD // 2

TILE = 256
VMEM_LIMIT = 56 * 1024 * 1024

ADAM_LR, ADAM_B1, ADAM_B2, ADAM_EPS, ADAM_WD, ADAM_STEP = 0.001, 0.9, 0.999, 1e-08, 0.01, 10

SW_ROWS = N_SGU_HEADS * BLOCK
R_G1, R_BIN, R_SINK, R_LOSS, R_LNG, R_LNB, R_SGUB, R_BOUT, R_G3 = 0, 8, 32, 40, 48, 56, 64, 72, 80
VEC_ROWS = 88

MESH = pl.DeviceIdType.MESH


def _mm(a, b):
    return jnp.dot(a, b, preferred_element_type=F32)


def _mm_nt(a, b):
    return lax.dot_general(a, b, (((1,), (1,)), ((), ())), preferred_element_type=F32)


def _mm_tn(a, b):
    return lax.dot_general(a, b, (((0,), (0,)), ((), ())), preferred_element_type=F32)


def _sigmoid(z):
    return 1.0 / (1.0 + jnp.exp(-z))


def _norm_cdf(z):
    return 0.5 * (1.0 + lax.erf(z * SQRT_HALF))


def _norm_pdf(z):
    return jnp.exp(-0.5 * z * z) * INV_SQRT_2PI


def _rows8(v):
    r, n = v.shape
    return jnp.sum(v.reshape(r // 8, 8, n), axis=0)


def _mean_rows(v):
    return jnp.sum(v, axis=1, keepdims=True) * (1.0 / v.shape[1])


def _fused_call(x, tgt, sinks, g1, b_in, ln_g, ln_b, sgu_w, bexp, b_out, g3, win_t, wout):
    seq = x.shape[0]
    t = TILE
    nt = seq // t
    nb = t // BLOCK
    act = MXU_DTYPE

    def body(sinks_ref, x_ref, xp_ref, xh_ref, tgt_ref, g1_ref, bin_ref, lng_ref, lnb_ref, sguw_ref, bexp_ref, bout_ref, g3_ref,
             wint_hbm, wout_hbm,
             dx_hbm, h_ref, dprojt_ref, mixedt_ref, dout_ref, sw_ref, vec_ref,
             dproj_ref, mixed_ref, wint_v, wout_v, wf_v, wb_v, q_s, kf_s, vf_s, k2_s, v2_s, gate_s, p_s, ps_s, o_s, u_s, mix_s, vhat_s, r2_s,
             cdfu_s, cdfv_s, r1_s, doutf_s, dmix_s, dkf_s, dvf_s, carryk_s, carryv_s,
             acc_bin, acc_g1, acc_bout, acc_g3, acc_lng, acc_lnb, acc_dws, acc_dbs, acc_sink, acc_loss,
             dh_s, r1p_s, dx_s, sems):
        i = pl.program_id(0)
        tile = nt - 1 - i
        lane128 = lax.broadcasted_iota(jnp.int32, (BLOCK, BLOCK), 1)
        lo = lane128 < HEAD_DIM

        @pl.when(i == 0)
        def _():
            cp_a = pltpu.make_async_copy(wint_hbm, wint_v, sems.at[0])
            cp_b = pltpu.make_async_copy(wout_hbm, wout_v, sems.at[1])
            cp_a.start()
            cp_b.start()
            for acc in (acc_bin, acc_g1, acc_bout, acc_g3, acc_lng, acc_lnb, acc_dws, acc_dbs, acc_sink, acc_loss,
                        carryk_s, carryv_s, dh_s, r1p_s, doutf_s):
                acc[...] = jnp.zeros(acc.shape, F32)
            tril = lax.broadcasted_iota(jnp.int32, (BLOCK, BLOCK), 0) >= lane128
            for h in range(N_SGU_HEADS):
                w = jnp.where(tril, sguw_ref[h], 0.0)
                wf_v[h // 2, :, (h % 2) * BLOCK:(h % 2 + 1) * BLOCK] = w.astype(act)
                wb_v[h // 2, :, (h % 2) * BLOCK:(h % 2 + 1) * BLOCK] = w.T.astype(act)
            cp_a.wait()
            cp_b.wait()

        g1 = g1_ref[...]

        def rms(v):
            r = lax.rsqrt(_mean_rows(v * v) + NORM_EPS)
            return r, v * r

        def dx_out(tile_idx):
            return pltpu.make_async_copy(dx_s, dx_hbm.at[pl.ds(pl.multiple_of(tile_idx * t, t), t), :], sems.at[2])

        def norm_bwd(x_src, r1_src):
            for c in range(nb):
                rows = slice(c * BLOCK, (c + 1) * BLOCK)
                r1 = r1_src[rows, :]
                xn = x_src[rows, :] * r1
                dhc = dh_s[rows, :]
                acc_g1[...] += _rows8(dhc * xn)
                dxn = dhc * g1
                dx_s[rows, :] = doutf_s[rows, :] + r1 * (dxn - xn * _mean_rows(dxn * xn))

        @pl.when(i > 1)
        def _():
            dx_out(tile + 2).wait()

        for c in range(nb):
            rows = slice(c * BLOCK, (c + 1) * BLOCK)
            r, xn = rms(x_ref[rows, :])
            r1_s[rows, :] = r
            h_ref[rows, :] = (xn * g1).astype(act)
        norm_bwd(xp_ref, r1p_s)
        _, xnh = rms(xh_ref[...])
        kvh = _mm_nt((xnh * g1).astype(act), wint_v[OFF_K:OFF_ZA, :]) + bin_ref[:, OFF_K:OFF_ZA]
        kf_s[0:BLOCK, :] = kvh[:, :BLOCK]
        vf_s[0:BLOCK, :] = kvh[:, BLOCK:]

        h = h_ref[...]
        q = _mm_nt(h, wint_v[0:OFF_K, :]) + bin_ref[:, 0:OFF_K]
        q_s[...] = (q * SCALE).astype(act)
        kv = _mm_nt(h, wint_v[OFF_K:OFF_ZA, :]) + bin_ref[:, OFF_K:OFF_ZA]
        kf_s[BLOCK:, :] = kv[:, :BLOCK]
        vf_s[BLOCK:, :] = kv[:, BLOCK:]
        for r in range(4):
            cols = slice(OFF_ZA + r * 512, OFF_ZA + (r + 1) * 512)
            gate_s[r] = _mm_nt(h, wint_v[cols, :]) + bin_ref[:, cols]

        lo_kv = lax.broadcasted_iota(jnp.int32, (t + BLOCK, BLOCK), 1) < HEAD_DIM
        for src, dst in ((kf_s, k2_s), (vf_s, v2_s)):
            v = src[...]
            vr = pltpu.roll(v, HEAD_DIM, 1)
            dst[0] = jnp.where(lo_kv, v, vr).astype(act)
            dst[1] = jnp.where(lo_kv, vr, v).astype(act)

        rowi = lax.broadcasted_iota(jnp.int32, (BLOCK, 2 * BLOCK), 0)
        colj = lax.broadcasted_iota(jnp.int32, (BLOCK, 2 * BLOCK), 1)
        in_band = (colj > rowi) & (colj <= rowi + BLOCK)
        row512 = lax.broadcasted_iota(jnp.int32, (4 * BLOCK, 1), 0)

        def stacked_q(b, g):
            parts = []
            for p in range(2):
                slab = q_s[b * BLOCK:(b + 1) * BLOCK, (2 * g + p) * BLOCK:(2 * g + p + 1) * BLOCK]
                parts += [jnp.where(lo, slab, jnp.zeros_like(slab)), jnp.where(lo, jnp.zeros_like(slab), slab)]
            return jnp.concatenate(parts, axis=0)

        def sink_col(g):
            s = [sinks_ref[4 * g + k] for k in range(4)]
            return jnp.where(row512 < BLOCK, s[0], jnp.where(row512 < 2 * BLOCK, s[1], jnp.where(row512 < 3 * BLOCK, s[2], s[3])))

        for b in range(nb):
            band = slice(b * BLOCK, (b + 2) * BLOCK)
            first_key = jnp.where(tile * nb + b > 0, 0, BLOCK)
            valid = in_band & (colj >= first_key)
            valid4 = jnp.concatenate([valid] * 4, axis=0)
            for g in range(2):
                s = _mm_nt(stacked_q(b, g), k2_s[g, band, :])
                s = jnp.where(valid4, s, NEG_INF)
                sk = sink_col(g)
                m = jnp.maximum(jnp.max(s, axis=1, keepdims=True), sk)
                p = jnp.exp(s - m)
                psk = jnp.exp(sk - m)
                inv = 1.0 / (jnp.sum(p, axis=1, keepdims=True) + psk)
                p = p * inv
                p_s[b * 2 + g] = p
                ps_s[b * 2 + g] = psk * inv
                o2 = _mm(p.astype(act), v2_s[g, band, :])
                for pr in range(2):
                    o_s[b * BLOCK:(b + 1) * BLOCK, (2 * g + pr) * BLOCK:(2 * g + pr + 1) * BLOCK] = jnp.where(
                        lo, o2[(2 * pr) * BLOCK:(2 * pr + 1) * BLOCK], o2[(2 * pr + 1) * BLOCK:(2 * pr + 2) * BLOCK])

        lng = lng_ref[...]
        lnb = lnb_ref[...]
        for c in range(nb):
            rows = slice(c * BLOCK, (c + 1) * BLOCK)
            za = gate_s[0, rows, :]
            mixed_ref[rows, 0:ATTN_W] = (o_s[rows, :] * (za * _sigmoid(za))).astype(act)
            us = gate_s[1, rows, :]
            vs = gate_s[2, rows, :]
            zs = gate_s[3, rows, :]
            cu = _norm_cdf(us)
            cv = _norm_cdf(vs)
            cdfu_s[rows, :] = cu
            cdfv_s[rows, :] = cv
            u = us * cu
            vg = vs * cv
            vc = vg - _mean_rows(vg)
            r2 = lax.rsqrt(_mean_rows(vc * vc) + NORM_EPS)
            vhat = vc * r2
            r2_s[rows, :] = r2
            vhat_s[rows, :] = vhat
            u_s[rows, :] = u
            vln = vhat * lng + lnb
            for p in range(4):
                cols = slice(p * BLOCK, (p + 1) * BLOCK)
                slab = vln[:, cols]
                rhs = jnp.concatenate([jnp.where(lo, slab, 0.0), jnp.where(lo, 0.0, slab)], axis=0).astype(act)
                mix_s[rows, cols] = _mm(wf_v[p], rhs) + bexp_ref[:, cols]
            mixed_ref[rows, ATTN_W:] = (u * mix_s[rows, :] * (zs * _sigmoid(zs))).astype(act)

        g3 = g3_ref[...]
        proj_o = _mm(mixed_ref[...], wout_v[...])
        for c in range(nb):
            rows = slice(c * BLOCK, (c + 1) * BLOCK)
            out = x_ref[rows, :] + proj_o[rows, :] + bout_ref[...]
            r3, on = rms(out)
            e = on * g3 - tgt_ref[rows, :]
            e2 = _rows8(e * e)
            acc_loss[...] += sum(e2[:, k * 128:(k + 1) * 128] for k in range(D_MODEL // 128)) * (0.5 / D_MODEL)
            dy = e * (1.0 / D_MODEL)
            acc_g3[...] += _rows8(dy * on)
            don = dy * g3
            dout = r3 * (don - on * _mean_rows(don * on))
            doutf_s[rows, :] = dout
            dout_ref[rows, :] = dout.astype(act)
            acc_bout[...] += _rows8(dout)
        dmix_s[...] = _mm_nt(dout_ref[...], wout_v[...])

        for c in range(nb):
            rows = slice(c * BLOCK, (c + 1) * BLOCK)
            dso = dmix_s[rows, ATTN_W:]
            u = u_s[rows, :]
            mix = mix_s[rows, :]
            zs = gate_s[3, rows, :]
            sg = _sigmoid(zs)
            sgs = zs * sg
            du = dso * mix * sgs
            dmx = dso * u * sgs
            dzs = dso * u * mix * (sg * (1.0 + zs * (1.0 - sg)))
            us = gate_s[1, rows, :]
            dus = du * (cdfu_s[rows, :] + us * _norm_pdf(us))
            vhat = vhat_s[rows, :]
            vln = (vhat * lng + lnb).astype(act)
            dvln_parts = []
            for p in range(4):
                cols = slice(p * BLOCK, (p + 1) * BLOCK)
                slab = dmx[:, cols]
                d_lo = jnp.where(lo, slab, 0.0).astype(act)
                d_hi = jnp.where(lo, 0.0, slab).astype(act)
                dvln_parts.append(_mm(wb_v[p], jnp.concatenate([d_lo, d_hi], axis=0)))
                acc_dws[(2 * p) * BLOCK:(2 * p + 1) * BLOCK, :] += _mm_nt(d_lo, vln[:, cols])
                acc_dws[(2 * p + 1) * BLOCK:(2 * p + 2) * BLOCK, :] += _mm_nt(d_hi, vln[:, cols])
            dvln = jnp.concatenate(dvln_parts, axis=1)
            acc_dbs[...] += dmx
            acc_lng[...] += _rows8(dvln * vhat)
            acc_lnb[...] += _rows8(dvln)
            dvhat = dvln * lng
            dvg = r2_s[rows, :] * (dvhat - _mean_rows(dvhat) - vhat * _mean_rows(dvhat * vhat))
            vs = gate_s[2, rows, :]
            dvs = dvg * (cdfv_s[rows, :] + vs * _norm_pdf(vs))
            for off, val in ((OFF_US, dus), (OFF_VS, dvs), (OFF_ZS, dzs)):
                dproj_ref[rows, off:off + 512] = val.astype(act)
                acc_bin[:, off:off + 512] += _rows8(val)

        dkf_s[...] = jnp.zeros(dkf_s.shape, F32)
        dvf_s[...] = jnp.zeros(dvf_s.shape, F32)
        for b in range(nb):
            rows = slice(b * BLOCK, (b + 1) * BLOCK)
            band = slice(b * BLOCK, (b + 2) * BLOCK)
            za = gate_s[0, rows, :]
            sg = _sigmoid(za)
            dao = dmix_s[rows, 0:ATTN_W]
            o = o_s[rows, :]
            do = dao * (za * sg)
            dza = dao * o * (sg * (1.0 + za * (1.0 - sg)))
            dproj_ref[rows, OFF_ZA:OFF_US] = dza.astype(act)
            acc_bin[:, OFF_ZA:OFF_US] += _rows8(dza)
            for g in range(2):
                do_parts, delta_parts = [], []
                for pr in range(2):
                    cols = slice((2 * g + pr) * BLOCK, (2 * g + pr + 1) * BLOCK)
                    d_pair = do[:, cols]
                    prod = d_pair * o[:, cols]
                    do_parts += [jnp.where(lo, d_pair, 0.0).astype(act), jnp.where(lo, 0.0, d_pair).astype(act)]
                    delta_parts += [jnp.sum(jnp.where(lo, prod, 0.0), axis=1, keepdims=True),
                                    jnp.sum(jnp.where(lo, 0.0, prod), axis=1, keepdims=True)]
                do_st = jnp.concatenate(do_parts, axis=0)
                delta = jnp.concatenate(delta_parts, axis=0)
                p = p_s[b * 2 + g]
                dp = _mm_nt(do_st, v2_s[g, band, :])
                ds = p * (dp - delta)
                sink_t = ps_s[b * 2 + g] * delta
                for k in range(4):
                    acc_sink[4 * g + k:4 * g + k + 1, :] += -jnp.sum(sink_t[k * BLOCK:(k + 1) * BLOCK], axis=0, keepdims=True)
                ds_a = ds.astype(act)
                dq2 = _mm(ds_a, k2_s[g, band, :]) * SCALE
                for pr in range(2):
                    cols = slice((2 * g + pr) * BLOCK, (2 * g + pr + 1) * BLOCK)
                    dq = jnp.where(lo, dq2[(2 * pr) * BLOCK:(2 * pr + 1) * BLOCK], dq2[(2 * pr + 1) * BLOCK:(2 * pr + 2) * BLOCK])
                    dproj_ref[rows, cols] = dq.astype(act)
                    acc_bin[:, cols] += _rows8(dq)
                lo_band = lax.broadcasted_iota(jnp.int32, (2 * BLOCK, BLOCK), 1) < HEAD_DIM
                mine = lo_band if g == 0 else jnp.logical_not(lo_band)
                for acc, lhs, rhs in ((dkf_s, ds_a, stacked_q(b, g)), (dvf_s, p.astype(act), do_st)):
                    d2 = _mm_tn(lhs, rhs)
                    full = d2 + pltpu.roll(d2, HEAD_DIM, 1)
                    acc[band, :] += jnp.where(mine, full, 0.0)
        for acc, carry, off in ((dkf_s, carryk_s, OFF_K), (dvf_s, carryv_s, OFF_V)):
            acc[t:t + BLOCK, :] += carry[...]
            carry[...] = acc[0:BLOCK, :]
            d = acc[BLOCK:, :]
            dproj_ref[:, off:off + BLOCK] = d.astype(act)
            acc_bin[:, off:off + BLOCK] += _rows8(d)

        dh_s[...] = _mm(dproj_ref[...], wint_v[...])
        r1p_s[...] = r1_s[...]
        dprojt_ref[...] = dproj_ref[...].T
        mixedt_ref[...] = mixed_ref[...].T

        @pl.when(i > 0)
        def _():
            dx_out(tile + 1).start()

        @pl.when(i == nt - 1)
        def _():
            @pl.when(i > 0)
            def _():
                dx_out(tile + 1).wait()

            norm_bwd(x_ref, r1_s)
            dx_out(tile).start()
            dx_out(tile).wait()
            tril = lax.broadcasted_iota(jnp.int32, (BLOCK, BLOCK), 0) >= lane128
            for hh in range(N_SGU_HEADS):
                rws = slice(hh * BLOCK, (hh + 1) * BLOCK)
                sw_ref[rws, :] = jnp.where(tril, acc_dws[rws, :], 0.0)
            vec_ref[...] = jnp.zeros((VEC_ROWS, 128), F32)

            def put(row0, acc):
                s = jnp.sum(acc[...], axis=0, keepdims=True)
                for k in range(acc.shape[1] // 128):
                    vec_ref[row0 + k:row0 + k + 1, :] = s[:, k * 128:(k + 1) * 128]

            put(R_G1, acc_g1)
            put(R_BIN, acc_bin)
            put(R_LNG, acc_lng)
            put(R_LNB, acc_lnb)
            put(R_BOUT, acc_bout)
            put(R_G3, acc_g3)
            vec_ref[R_SINK:R_SINK + 1, :] = jnp.sum(
                jnp.where(lax.broadcasted_iota(jnp.int32, (8, 128), 0) == lax.broadcasted_iota(jnp.int32, (8, 128), 1),
                          acc_sink[...], 0.0), axis=0, keepdims=True)
            vec_ref[R_LOSS:R_LOSS + 1, :] = jnp.zeros((1, 128), F32) + jnp.sum(acc_loss[...])
            dbs_t = acc_dbs[...].T
            vec_ref[R_SGUB:R_SGUB + 8, :] = jnp.sum(dbs_t.reshape(N_SGU_HEADS, SGU_W // N_SGU_HEADS, BLOCK), axis=1)

    full = lambda shape: pl.BlockSpec(shape, lambda i: (0,) * len(shape))
    tok = lambda w: pl.BlockSpec((t, w), lambda i: (nt - 1 - i, 0))
    in_specs = [
        pl.BlockSpec(memory_space=pltpu.SMEM),
        tok(D_MODEL),
        pl.BlockSpec((t, D_MODEL), lambda i: (nt - 1 - jnp.maximum(i - 1, 0), 0)),
        pl.BlockSpec((BLOCK, D_MODEL), lambda i: (jnp.maximum((nt - 1 - i) * nb - 1, 0), 0)),
        tok(D_MODEL),
        full((1, D_MODEL)), full((1, IN_W)), full((1, SGU_W)), full((1, SGU_W)),
        full((N_SGU_HEADS, BLOCK, BLOCK)), full((BLOCK, SGU_W)), full((1, D_MODEL)), full((1, D_MODEL)),
        pl.BlockSpec(memory_space=pl.ANY), pl.BlockSpec(memory_space=pl.ANY),
    ]
    out_shape = [
        jax.ShapeDtypeStruct((seq, D_MODEL), F32),
        jax.ShapeDtypeStruct((seq, D_MODEL), act),
        jax.ShapeDtypeStruct((IN_W, seq), act),
        jax.ShapeDtypeStruct((D_MODEL, seq), act),
        jax.ShapeDtypeStruct((seq, D_MODEL), act),
        jax.ShapeDtypeStruct((SW_ROWS, 128), F32),
        jax.ShapeDtypeStruct((VEC_ROWS, 128), F32),
    ]
    tok_t = lambda w: pl.BlockSpec((w, t), lambda i: (0, nt - 1 - i))
    out_specs = [pl.BlockSpec(memory_space=pl.ANY), tok(D_MODEL), tok_t(IN_W), tok_t(D_MODEL), tok(D_MODEL),
                 full((SW_ROWS, 128)), full((VEC_ROWS, 128))]
    vm = pltpu.VMEM
    scratch = [
        vm((t, IN_W), act), vm((t, D_MODEL), act),
        vm((IN_W, D_MODEL), act), vm((D_MODEL, D_MODEL), act),
        vm((4, BLOCK, 2 * BLOCK), act), vm((4, BLOCK, 2 * BLOCK), act),
        vm((t, ATTN_W), act),
        vm((t + BLOCK, BLOCK), F32), vm((t + BLOCK, BLOCK), F32),
        vm((2, t + BLOCK, BLOCK), act), vm((2, t + BLOCK, BLOCK), act),
        vm((4, t, 512), F32),
        vm((2 * nb, 4 * BLOCK, 2 * BLOCK), F32), vm((2 * nb, 4 * BLOCK, 1), F32),
        vm((t, ATTN_W), F32), vm((t, SGU_W), F32), vm((t, SGU_W), F32), vm((t, SGU_W), F32), vm((t, 1), F32),
        vm((t, SGU_W), F32), vm((t, SGU_W), F32), vm((t, 1), F32),
        vm((t, D_MODEL), F32), vm((t, D_MODEL), F32),
        vm((t + BLOCK, BLOCK), F32), vm((t + BLOCK, BLOCK), F32), vm((BLOCK, BLOCK), F32), vm((BLOCK, BLOCK), F32),
        vm((8, IN_W), F32), vm((8, D_MODEL), F32), vm((8, D_MODEL), F32), vm((8, D_MODEL), F32),
        vm((8, SGU_W), F32), vm((8, SGU_W), F32), vm((N_SGU_HEADS * BLOCK, BLOCK), F32), vm((BLOCK, SGU_W), F32),
        vm((8, 128), F32), vm((8, 128), F32),
        vm((t, D_MODEL), F32), vm((t, 1), F32), vm((t, D_MODEL), F32),
        pltpu.SemaphoreType.DMA((3,)),
    ]
    return pl.pallas_call(
        body, name="fused", grid=(nt,), in_specs=in_specs, out_specs=out_specs, out_shape=out_shape,
        scratch_shapes=scratch,
        compiler_params=pltpu.CompilerParams(dimension_semantics=("arbitrary",), vmem_limit_bytes=VMEM_LIMIT),
    )(sinks, x, x, x, tgt, g1, b_in, ln_g, ln_b, sgu_w, bexp, b_out, g3, win_t, wout)


def _place():
    x, y, c = lax.axis_index("x"), lax.axis_index("y"), lax.axis_index("c")
    chips = [(1 - x, y), (x, 1 - y), (1 - x, 1 - y)]
    return x, y, c, chips


def _gather_call(a_loc, b_loc):
    def body(a_ref, b_ref, ga_ref, gb_ref, send_sems, recv_sems, loc_sems):
        x, y, c, chips = _place()
        me, sibling = (x, y, c), (x, y, 1 - c)
        j = 2 * x + y

        def half(which, cj, hf):
            ref, shard = (ga_ref, W_IN_SHARD) if which == 0 else (gb_ref, W_OUT_SHARD)
            n = shard // 2
            return ref.at[pl.ds(pl.multiple_of(cj * shard + hf * n, 16), n), :]

        def copy(k, src, dst, to):
            return pltpu.make_async_remote_copy(src_ref=src, dst_ref=dst, send_sem=send_sems.at[k], recv_sem=recv_sems.at[k],
                                                device_id=to, device_id_type=MESH)

        own = [pltpu.make_async_copy(a_ref, ga_ref.at[pl.ds(pl.multiple_of(j * W_IN_SHARD, 16), W_IN_SHARD), :], loc_sems.at[0]),
               pltpu.make_async_copy(b_ref, gb_ref.at[pl.ds(pl.multiple_of(j * W_OUT_SHARD, 16), W_OUT_SHARD), :], loc_sems.at[1])]
        for cp in own:
            cp.start()
        srcs = [a_ref.at[pl.ds(pl.multiple_of(c * HALF_A, 16), HALF_A), :], b_ref.at[pl.ds(pl.multiple_of(c * HALF_B, 16), HALF_B), :]]
        first = [copy(2 * r + w, srcs[w], half(w, j, c), (*chip, c)) for r, chip in enumerate(chips) for w in range(2)]
        for cp in first:
            cp.start()
        passed = []
        for r, (cx, cy) in enumerate(chips):
            cj = 2 * cx + cy
            for w in range(2):
                copy(2 * r + w, half(w, cj, c), half(w, cj, c), me).wait_recv()
                fwd = copy(6 + 2 * r + w, half(w, cj, c), half(w, cj, c), sibling)
                fwd.start()
                passed.append(fwd)
        for r, (cx, cy) in enumerate(chips):
            cj = 2 * cx + cy
            for w in range(2):
                copy(6 + 2 * r + w, half(w, cj, 1 - c), half(w, cj, 1 - c), me).wait_recv()
        for cp in first + passed:
            cp.wait_send()
        for cp in own:
            cp.wait()

    hbm = pl.BlockSpec(memory_space=pl.ANY)
    return pl.pallas_call(
        body, name="gather", in_specs=[hbm, hbm], out_specs=[hbm, hbm],
        out_shape=[jax.ShapeDtypeStruct((IN_W, D_MODEL), a_loc.dtype), jax.ShapeDtypeStruct((D_MODEL, D_MODEL), b_loc.dtype)],
        scratch_shapes=[pltpu.SemaphoreType.DMA((12,)), pltpu.SemaphoreType.DMA((12,)), pltpu.SemaphoreType.DMA((2,))],
    )(a_loc, b_loc)


def _wgrad_reduce_call(dproj_t, h, mixed_t, dout, sw, vec):
    wire = jnp.bfloat16
    half_sw = SW_ROWS // 2
    seq = h.shape[0]
    tk = min(1024, seq)
    nk = seq // tk
    n_steps = 2 * N_CHIPS
    rel_of = lambda s: s % 3 if s < 6 else 3
    half_of = lambda s: s // 3 if s < 6 else s - 6
    x, y = lax.axis_index("x"), lax.axis_index("y")
    chip_of = [2 * (1 - x) + (1 - y), 2 * (1 - x) + y, 2 * x + (1 - y), 2 * x + y]
    order = jnp.stack([2 * chip_of[rel_of(s)] + half_of(s) for s in range(n_steps)]).astype(jnp.int32)

    def body(order_ref, dpt_ref, h_hbm, mxt_ref, dout_hbm, sw_ref, small_ref, oa_ref, ob_ref, osw_ref, osmall_ref,
             h_v, dout_v, acc_a, acc_b, sib_a, sib_b, snd_a, snd_b, in_a, in_b, own_a, own_b, fin_a, fin_b,
             all_small, sw_sib, sw_chips, sw_fin, send_sems, recv_sems, loc_sems):
        x, y, c = lax.axis_index("x"), lax.axis_index("y"), lax.axis_index("c")
        me, sibling = (x, y, c), (x, y, 1 - c)
        steps = [(1 - x, 1 - y), (1 - x, y), (x, 1 - y)]
        j = 2 * x + y
        dev = 4 * x + 2 * y + c
        b, k = pl.program_id(0), pl.program_id(1)

        def copy(n, src, dst, to):
            return pltpu.make_async_remote_copy(src_ref=src, dst_ref=dst, send_sem=send_sems.at[n], recv_sem=recv_sems.at[n],
                                                device_id=to, device_id_type=MESH)

        def sw_rows(ref, hf):
            return ref.at[pl.ds(pl.multiple_of(hf * half_sw, 8), half_sw), :]

        sm_first = [copy(16, small_ref, all_small.at[dev], sibling)]
        sm_first += [copy(17 + r, small_ref, all_small.at[dev], (*chip, c)) for r, chip in enumerate(steps)]
        sw_to_sib = copy(23, sw_rows(sw_ref, 1 - c), sw_sib, sibling)

        @pl.when((b == 0) & (k == 0))
        def _():
            all_small[dev] = small_ref[...]
            for cp in sm_first + [sw_to_sib]:
                cp.start()

        def load(kk):
            rows = pl.ds(kk * tk, tk)
            return [pltpu.make_async_copy(h_hbm.at[rows, :], h_v.at[rows, :], loc_sems.at[kk]),
                    pltpu.make_async_copy(dout_hbm.at[rows, :], dout_v.at[rows, :], loc_sems.at[nk + kk])]

        for kk in range(nk):
            @pl.when((b == 0) & (k == 0))
            def _():
                for cp in load(kk):
                    cp.start()

        for kk in range(nk):
            @pl.when((b == 0) & (k == kk))
            def _():
                for cp in load(kk):
                    cp.wait()

        tok = pl.ds(pl.multiple_of(k * tk, tk), tk)
        pa = _mm(dpt_ref[...], h_v[tok, :])
        pb = _mm(mxt_ref[...], dout_v[tok, :])
        slot = b % 2

        @pl.when(k == 0)
        def _():
            acc_a[slot] = pa
            acc_b[slot] = pb

        @pl.when(k != 0)
        def _():
            acc_a[slot] += pa
            acc_b[slot] += pb

        def to_sibling(s):
            r = rel_of(s)
            return [copy(r, acc_a.at[s % 2], sib_a.at[r], sibling), copy(4 + r, acc_b.at[s % 2], sib_b.at[r], sibling)]

        def chip_partial(s):
            r = rel_of(s)
            copy(r, sib_a.at[r], sib_a.at[r], me).wait_recv()
            copy(4 + r, sib_b.at[r], sib_b.at[r], me).wait_recv()
            return acc_a[s % 2] + sib_a[r], acc_b[s % 2] + sib_b[r]

        def to_owner(r):
            return [copy(8 + r, snd_a.at[r], in_a.at[r], (*steps[r], c)), copy(11 + r, snd_b.at[r], in_b.at[r], (*steps[r], c))]

        for s in range(n_steps):
            if s >= 1:
                sp = s - 1

                @pl.when((b == s) & (k == 0) & (c == half_of(sp)))
                def _():
                    ta, tb = chip_partial(sp)
                    r = rel_of(sp)
                    if r < 3:
                        snd_a[r] = ta.astype(wire)
                        snd_b[r] = tb.astype(wire)
                        for cp in to_owner(r):
                            cp.start()
                    else:
                        own_a[...] = ta
                        own_b[...] = tb

                @pl.when((b == s) & (k == nk - 1) & (c != half_of(sp)))
                def _():
                    for cp in to_sibling(sp):
                        cp.wait_send()

            @pl.when((b == s) & (k == nk - 1) & (c != half_of(s)))
            def _():
                for cp in to_sibling(s):
                    cp.start()

        @pl.when((b == n_steps - 1) & (k == nk - 1))
        def _():
            sm_passed = []
            for r, (cx, cy) in enumerate(steps):
                d = 4 * cx + 2 * cy + c
                copy(17 + r, all_small.at[d], all_small.at[d], me).wait_recv()
                fwd = copy(20 + r, all_small.at[d], all_small.at[d], sibling)
                fwd.start()
                sm_passed.append(fwd)
            copy(23, sw_sib, sw_sib, me).wait_recv()
            sw_chips[j] = (sw_rows(sw_ref, c)[...] + sw_sib[...]).astype(wire)
            sw_ici = [copy(24 + r, sw_chips.at[j], sw_chips.at[j], (*chip, c)) for r, chip in enumerate(steps)]
            for cp in sw_ici:
                cp.start()

            last = n_steps - 1

            @pl.when(c == half_of(last))
            def _():
                own_a[...], own_b[...] = chip_partial(last)

            @pl.when(c != half_of(last))
            def _():
                for cp in to_sibling(last):
                    cp.wait_send()

            tot_a = own_a[...]
            tot_b = own_b[...]
            for s in range(3):
                copy(8 + s, in_a.at[s], in_a.at[s], me).wait_recv()
                copy(11 + s, in_b.at[s], in_b.at[s], me).wait_recv()
                tot_a = tot_a + in_a[s].astype(F32)
                tot_b = tot_b + in_b[s].astype(F32)
            mine_a = oa_ref.at[pl.ds(pl.multiple_of(c * HALF_A, 8), HALF_A), :]
            mine_b = ob_ref.at[pl.ds(pl.multiple_of(c * HALF_B, 8), HALF_B), :]
            mine_a[...] = tot_a
            mine_b[...] = tot_b
            back = [copy(14, mine_a, fin_a, sibling), copy(15, mine_b, fin_b, sibling)]
            for cp in back:
                cp.start()

            copy(16, small_ref, all_small.at[dev ^ 1], me).wait_recv()
            for r, (cx, cy) in enumerate(steps):
                d = 4 * cx + 2 * cy + (1 - c)
                copy(20 + r, all_small.at[d], all_small.at[d], me).wait_recv()
            tot = all_small[0]
            for d in range(1, N_DEV):
                tot = tot + all_small[d]
            osmall_ref[...] = tot

            for r, (cx, cy) in enumerate(steps):
                cj = 2 * cx + cy
                copy(24 + r, sw_chips.at[cj], sw_chips.at[cj], me).wait_recv()
            tot_sw = sw_chips[0].astype(F32)
            for q in range(1, N_CHIPS):
                tot_sw = tot_sw + sw_chips[q].astype(F32)
            sw_rows(osw_ref, c)[...] = tot_sw
            sw_back = copy(27, sw_rows(osw_ref, c), sw_fin, sibling)
            sw_back.start()

            copy(14, fin_a, fin_a, me).wait_recv()
            copy(15, fin_b, fin_b, me).wait_recv()
            copy(27, sw_fin, sw_fin, me).wait_recv()
            oa_ref[pl.ds(pl.multiple_of((1 - c) * HALF_A, 8), HALF_A), :] = fin_a[...]
            ob_ref[pl.ds(pl.multiple_of((1 - c) * HALF_B, 8), HALF_B), :] = fin_b[...]
            sw_rows(osw_ref, 1 - c)[...] = sw_fin[...]
            sends = sm_first + sm_passed + sw_ici + back + [sw_to_sib, sw_back]
            for s in range(3):
                sends += to_owner(s)
            for cp in sends:
                cp.wait_send()

    vmem = pl.BlockSpec(memory_space=pltpu.VMEM)
    vm = pltpu.VMEM
    grid_spec = pltpu.PrefetchScalarGridSpec(
        num_scalar_prefetch=1, grid=(n_steps, nk),
        in_specs=[pl.BlockSpec((HALF_A, tk), lambda b, k, o: (o[b], k)), pl.BlockSpec(memory_space=pl.ANY),
                  pl.BlockSpec((HALF_B, tk), lambda b, k, o: (o[b], k)), pl.BlockSpec(memory_space=pl.ANY),
                  vmem, vmem],
        out_specs=[vmem, vmem, vmem, vmem],
        scratch_shapes=[vm((seq, D_MODEL), h.dtype), vm((seq, D_MODEL), dout.dtype),
                        vm((2, HALF_A, D_MODEL), F32), vm((2, HALF_B, D_MODEL), F32),
                        vm((N_CHIPS, HALF_A, D_MODEL), F32), vm((N_CHIPS, HALF_B, D_MODEL), F32),
                        vm((3, HALF_A, D_MODEL), wire), vm((3, HALF_B, D_MODEL), wire),
                        vm((3, HALF_A, D_MODEL), wire), vm((3, HALF_B, D_MODEL), wire),
                        vm((HALF_A, D_MODEL), F32), vm((HALF_B, D_MODEL), F32),
                        vm((HALF_A, D_MODEL), F32), vm((HALF_B, D_MODEL), F32),
                        vm((N_DEV, VEC_ROWS, 128), F32), vm((half_sw, 128), F32), vm((N_CHIPS, half_sw, 128), wire),
                        vm((half_sw, 128), F32),
                        pltpu.SemaphoreType.DMA((28,)), pltpu.SemaphoreType.DMA((28,)), pltpu.SemaphoreType.DMA((2 * nk,))])
    return pl.pallas_call(
        body, name="wgrad_reduce", grid_spec=grid_spec,
        out_shape=[jax.ShapeDtypeStruct((W_IN_SHARD, D_MODEL), F32), jax.ShapeDtypeStruct((W_OUT_SHARD, D_MODEL), F32),
                   jax.ShapeDtypeStruct((SW_ROWS, 128), F32), jax.ShapeDtypeStruct((VEC_ROWS, 128), F32)],
        compiler_params=pltpu.CompilerParams(dimension_semantics=("arbitrary", "arbitrary"), vmem_limit_bytes=VMEM_LIMIT),
    )(order, dproj_t, h, mixed_t, dout, sw, vec)


def _adamw(w, g, m, v):
    nm = ADAM_B1 * m + (1.0 - ADAM_B1) * g
    nv = ADAM_B2 * v + (1.0 - ADAM_B2) * (g * g)
    m_hat = nm / (1.0 - ADAM_B1 ** ADAM_STEP)
    v_hat = nv / (1.0 - ADAM_B2 ** ADAM_STEP)
    return -ADAM_LR * (m_hat / (jnp.sqrt(v_hat) + ADAM_EPS) + ADAM_WD * w), nm, nv


def _adamw_call(w, g, m, v, block_rows, name):
    rows, cols = w.shape

    def body(w_ref, g_ref, m_ref, v_ref, go_ref, d_ref, nm_ref, nv_ref):
        gg = g_ref[...]
        go_ref[...] = gg
        d_ref[...], nm_ref[...], nv_ref[...] = _adamw(w_ref[...], gg, m_ref[...], v_ref[...])

    spec = pl.BlockSpec((block_rows, cols), lambda i: (i, 0))
    return pl.pallas_call(
        body, name=name, grid=(rows // block_rows,), in_specs=[spec] * 4, out_specs=[spec] * 4,
        out_shape=[jax.ShapeDtypeStruct((rows, cols), F32)] * 4,
        compiler_params=pltpu.CompilerParams(dimension_semantics=("arbitrary",)),
    )(w, g, m, v)


_SMALL = (("norm_g", (1, D_MODEL), R_G1), ("b_in", (1, IN_W), R_BIN), ("attn_sinks", (1, 8), R_SINK),
          ("sgu_ln_g", (1, SGU_W), R_LNG), ("sgu_ln_b", (1, SGU_W), R_LNB), ("sgu_b", (N_SGU_HEADS, BLOCK), R_SGUB),
          ("b_out", (1, D_MODEL), R_BOUT), ("final_norm_g", (1, D_MODEL), R_G3))


def _adamw_small_call(sw_g, vec_g, sgu_w3, ws, ms, vs):
    n = len(_SMALL)

    def body(*refs):
        sw_ref, vec_ref = refs[0], refs[1]
        w3 = refs[2:5]
        w_refs, m_refs, v_refs = refs[5:5 + n], refs[5 + n:5 + 2 * n], refs[5 + 2 * n:5 + 3 * n]
        outs = refs[5 + 3 * n:]
        outs[0][...] = vec_ref[R_LOSS:R_LOSS + 1, 0:1]
        g = sw_ref[...]
        outs[1][...] = g
        outs[2][...], outs[3][...], outs[4][...] = _adamw(w3[0][...], g, w3[1][...], w3[2][...])
        for k, (_, shape, row) in enumerate(_SMALL):
            if shape[0] == 1 and shape[1] >= 128:
                g = jnp.concatenate([vec_ref[row + q:row + q + 1, :] for q in range(shape[1] // 128)], axis=1)
            else:
                g = vec_ref[row:row + shape[0], 0:shape[1]]
            o = outs[5 + 4 * k:9 + 4 * k]
            o[0][...] = g
            o[1][...], o[2][...], o[3][...] = _adamw(w_refs[k][...], g, m_refs[k][...], v_refs[k][...])

    vmem = pl.BlockSpec(memory_space=pltpu.VMEM)
    out_shape = [jax.ShapeDtypeStruct((1, 1), F32)] + [jax.ShapeDtypeStruct((SW_ROWS, 128), F32)] * 4
    for _, shape, _ in _SMALL:
        out_shape += [jax.ShapeDtypeStruct(shape, F32)] * 4
    args = [sw_g, vec_g, *sgu_w3, *ws, *ms, *vs]
    return pl.pallas_call(
        body, name="adamw_small", in_specs=[vmem] * len(args), out_specs=[vmem] * len(out_shape), out_shape=out_shape,
    )(*args)


def kernel(x, norm_g, w_in, b_in, attn_sinks, sgu_ln_g, sgu_ln_b, sgu_w, sgu_b, w_out, b_out, final_norm_g, loss_target, m_norm_g, m_w_in, m_b_in, m_attn_sinks, m_sgu_ln_g, m_sgu_ln_b, m_sgu_w, m_sgu_b, m_w_out, m_b_out, m_final_norm_g, v_norm_g, v_w_in, v_b_in, v_attn_sinks, v_sgu_ln_g, v_sgu_ln_b, v_sgu_w, v_sgu_b, v_w_out, v_b_out, v_final_norm_g):
    seq = x.shape[1]
    win_t, wout = _gather_call(w_in[0].T.astype(MXU_DTYPE), w_out[0].astype(MXU_DTYPE))
    bexp = jnp.repeat(sgu_b[0].T, SGU_W // N_SGU_HEADS, axis=1)
    dx, h, dproj_t, mixed_t, dout, sw, vec = _fused_call(
        x[0], loss_target[0], attn_sinks[0], norm_g, b_in, sgu_ln_g, sgu_ln_b, sgu_w[0], bexp, b_out,
        final_norm_g.reshape(1, D_MODEL), win_t, wout)
    ga_t, g_w_out, sw, vec = _wgrad_reduce_call(dproj_t, h, mixed_t, dout, sw, vec)

    names = ["norm_g", "w_in", "b_in", "attn_sinks", "sgu_ln_g", "sgu_ln_b", "sgu_w", "sgu_b", "w_out", "b_out", "final_norm_g"]
    res = {}
    res["w_in"] = [a.T[None] for a in _adamw_call(w_in[0].T, ga_t, m_w_in[0].T, v_w_in[0].T, W_IN_SHARD // 4, "adamw_w_in")]
    res["w_out"] = [a[None] for a in _adamw_call(w_out[0], g_w_out, m_w_out[0], v_w_out[0], HALF_B, "adamw_w_out")]
    given = dict(norm_g=(norm_g, m_norm_g, v_norm_g), b_in=(b_in, m_b_in, v_b_in), attn_sinks=(attn_sinks, m_attn_sinks, v_attn_sinks),
                 sgu_ln_g=(sgu_ln_g, m_sgu_ln_g, v_sgu_ln_g), sgu_ln_b=(sgu_ln_b, m_sgu_ln_b, v_sgu_ln_b),
                 sgu_b=(sgu_b, m_sgu_b, v_sgu_b), b_out=(b_out, m_b_out, v_b_out),
                 final_norm_g=(final_norm_g, m_final_norm_g, v_final_norm_g))
    wmv = [[given[n][k].reshape(shape) for n, shape, _ in _SMALL] for k in range(3)]
    outs = _adamw_small_call(sw, vec, [a.reshape(SW_ROWS, BLOCK) for a in (sgu_w, m_sgu_w, v_sgu_w)], *wmv)
    loss = outs[0].reshape(())
    res["sgu_w"] = [a.reshape(sgu_w.shape) for a in outs[1:5]]
    for k, (n, _, _) in enumerate(_SMALL):
        res[n] = [a.reshape(given[n][0].shape) for a in outs[5 + 4 * k:9 + 4 * k]]

    return (loss, dx[None], *[res[n][0] for n in names], *[res[n][1] for n in names], *[res[n][2] for n in names],
            *[res[n][3] for n in names])
```

```python
import functools
import math

import jax
import jax.numpy as jnp
from jax import lax
from jax.experimental import pallas as pl
from jax.experimental.pallas import tpu as pltpu

F32 = jnp.float32
MXU_DTYPE = jnp.bfloat16

D_MODEL = 1024
HEAD_DIM = 64
ATTN_W = 512
SGU_W = 512
N_SGU_HEADS = 8
BLOCK = 128
IN_W = 2816
OFF_K, OFF_V, OFF_ZA, OFF_US, OFF_VS, OFF_ZS = 512, 640, 768, 1280, 1792, 2304
NORM_EPS = 1e-5
NEG_INF = -1e30
SCALE = HEAD_DIM ** -0.5
SQRT_HALF = math.sqrt(0.5)
INV_SQRT_2PI = 1.0 / math.sqrt(2.0 * math.pi)

N_CHIPS = 4
N_DEV = 8
W_IN_SHARD = IN_W // N_CHIPS
W_OUT_SHARD = D_MODEL // N_CHIPS
HALF_A = W_IN_SHARD // 2
HALF_B = W_OUT_SHARD // 2

TILE = 256
VMEM_LIMIT = 56 * 1024 * 1024

ADAM_LR, ADAM_B1, ADAM_B2, ADAM_EPS, ADAM_WD, ADAM_STEP = 0.001, 0.9, 0.999, 1e-08, 0.01, 10

SW_ROWS = N_SGU_HEADS * BLOCK
R_G1, R_BIN, R_SINK, R_LOSS, R_LNG, R_LNB, R_SGUB, R_BOUT, R_G3 = 0, 8, 32, 40, 48, 56, 64, 72, 80
VEC_ROWS = 88

MESH = pl.DeviceIdType.MESH


def _mm(a, b):
    return jnp.dot(a, b, preferred_element_type=F32)


def _mm_nt(a, b):
    return lax.dot_general(a, b, (((1,), (1,)), ((), ())), preferred_element_type=F32)


def _mm_tn(a, b):
    return lax.dot_general(a, b, (((0,), (0,)), ((), ())), preferred_element_type=F32)


def _sigmoid(z):
    return 1.0 / (1.0 + jnp.exp(-z))


def _norm_cdf(z):
    return 0.5 * (1.0 + lax.erf(z * SQRT_HALF))


def _norm_pdf(z):
    return jnp.exp(-0.5 * z * z) * INV_SQRT_2PI


def _rows8(v):
    r, n = v.shape
    return jnp.sum(v.reshape(r // 8, 8, n), axis=0)


def _mean_rows(v):
    return jnp.sum(v, axis=1, keepdims=True) * (1.0 / v.shape[1])


def _fused_call(x, tgt, sinks, g1, b_in, ln_g, ln_b, sgu_w, bexp, b_out, g3, win_t, wout):
    seq = x.shape[0]
    t = TILE
    nt = seq // t
    nb = t // BLOCK
    act = MXU_DTYPE

    def body(sinks_ref, x_ref, xh_ref, tgt_ref, g1_ref, bin_ref, lng_ref, lnb_ref, sguw_ref, bexp_ref, bout_ref, g3_ref,
             wint_hbm, wout_hbm,
             dx_ref, h_ref, dprojt_ref, mixedt_ref, dout_ref, sw_ref, vec_ref,
             dproj_ref, mixed_ref, wint_v, wout_v, wf_v, wb_v, q_s, kf_s, vf_s, k2_s, v2_s, gate_s, p_s, ps_s, o_s, u_s, mix_s, vhat_s, r2_s,
             cdfu_s, cdfv_s, r1_s, doutf_s, dmix_s, dkf_s, dvf_s, carryk_s, carryv_s,
             acc_bin, acc_g1, acc_bout, acc_g3, acc_lng, acc_lnb, acc_dws, acc_dbs, acc_sink, acc_loss,
             h_s, rhs_s, vlnp_s, dvln_s, sems):
        i = pl.program_id(0)
        tile = nt - 1 - i
        lane128 = lax.broadcasted_iota(jnp.int32, (BLOCK, BLOCK), 1)
        lo = lane128 < HEAD_DIM

        @pl.when(i == 0)
        def _():
            cp_a = pltpu.make_async_copy(wint_hbm, wint_v, sems.at[0])
            cp_b = pltpu.make_async_copy(wout_hbm, wout_v, sems.at[1])
            cp_a.start()
            cp_b.start()
            for acc in (acc_bin, acc_g1, acc_bout, acc_g3, acc_lng, acc_lnb, acc_dws, acc_dbs, acc_sink, acc_loss,
                        carryk_s, carryv_s):
                acc[...] = jnp.zeros(acc.shape, F32)
            tril = lax.broadcasted_iota(jnp.int32, (BLOCK, BLOCK), 0) >= lane128
            for h in range(N_SGU_HEADS):
                w = jnp.where(tril, sguw_ref[h], 0.0)
                wf_v[h // 2, :, (h % 2) * BLOCK:(h % 2 + 1) * BLOCK] = w.astype(act)
                wb_v[h // 2, :, (h % 2) * BLOCK:(h % 2 + 1) * BLOCK] = w.T.astype(act)
            cp_a.wait()
            cp_b.wait()

        g1 = g1_ref[...]

        def rms(v):
            r = lax.rsqrt(_mean_rows(v * v) + NORM_EPS)
            return r, v * r

        _, xnh = rms(xh_ref[...])
        h_s[0:BLOCK, :] = (xnh * g1).astype(act)
        for c in range(nb):
            rows = slice(c * BLOCK, (c + 1) * BLOCK)
            r, xn = rms(x_ref[rows, :])
            r1_s[rows, :] = r
            h_s[BLOCK + c * BLOCK:BLOCK + (c + 1) * BLOCK, :] = (xn * g1).astype(act)

        h = h_s[BLOCK:, :]
        h_ref[...] = h
        q = _mm_nt(h, wint_v[0:OFF_K, :]) + bin_ref[:, 0:OFF_K]
        q_s[...] = (q * SCALE).astype(act)
        kv = _mm_nt(h_s[...], wint_v[OFF_K:OFF_ZA, :]) + bin_ref[:, OFF_K:OFF_ZA]
        kf_s[...] = kv[:, :BLOCK]
        vf_s[...] = kv[:, BLOCK:]
        for r in range(4):
            cols = slice(OFF_ZA + r * 512, OFF_ZA + (r + 1) * 512)
            gate_s[r] = _mm_nt(h, wint_v[cols, :]) + bin_ref[:, cols]

        lo_kv = lax.broadcasted_iota(jnp.int32, (t + BLOCK, BLOCK), 1) < HEAD_DIM
        for src, dst in ((kf_s, k2_s), (vf_s, v2_s)):
            v = src[...]
            vr = pltpu.roll(v, HEAD_DIM, 1)
            dst[0] = jnp.where(lo_kv, v, vr).astype(act)
            dst[1] = jnp.where(lo_kv, vr, v).astype(act)

        rowi = lax.broadcasted_iota(jnp.int32, (BLOCK, 2 * BLOCK), 0)
        colj = lax.broadcasted_iota(jnp.int32, (BLOCK, 2 * BLOCK), 1)
        in_band = (colj > rowi) & (colj <= rowi + BLOCK)
        row512 = lax.broadcasted_iota(jnp.int32, (4 * BLOCK, 1), 0)

        def stacked_q(b, g):
            parts = []
            for p in range(2):
                slab = q_s[b * BLOCK:(b + 1) * BLOCK, (2 * g + p) * BLOCK:(2 * g + p + 1) * BLOCK]
                parts += [jnp.where(lo, slab, jnp.zeros_like(slab)), jnp.where(lo, jnp.zeros_like(slab), slab)]
            return jnp.concatenate(parts, axis=0)

        def sink_col(g):
            s = [sinks_ref[4 * g + k] for k in range(4)]
            return jnp.where(row512 < BLOCK, s[0], jnp.where(row512 < 2 * BLOCK, s[1], jnp.where(row512 < 3 * BLOCK, s[2], s[3])))

        for b in range(nb):
            band = slice(b * BLOCK, (b + 2) * BLOCK)
            first_key = jnp.where(tile * nb + b > 0, 0, BLOCK)
            valid = in_band & (colj >= first_key)
            valid4 = jnp.concatenate([valid] * 4, axis=0)
            for g in range(2):
                s = _mm_nt(stacked_q(b, g), k2_s[g, band, :])
                s = jnp.where(valid4, s, NEG_INF)
                sk = sink_col(g)
                m = jnp.maximum(jnp.max(s, axis=1, keepdims=True), sk)
                p = jnp.exp(s - m)
                psk = jnp.exp(sk - m)
                inv = 1.0 / (jnp.sum(p, axis=1, keepdims=True) + psk)
                p = p * inv
                p_s[b * 2 + g] = p
                ps_s[b * 2 + g] = psk * inv
                o2 = _mm(p.astype(act), v2_s[g, band, :])
                for pr in range(2):
                    o_s[b * BLOCK:(b + 1) * BLOCK, (2 * g + pr) * BLOCK:(2 * g + pr + 1) * BLOCK] = jnp.where(
                        lo, o2[(2 * pr) * BLOCK:(2 * pr + 1) * BLOCK], o2[(2 * pr + 1) * BLOCK:(2 * pr + 2) * BLOCK])

        lng = lng_ref[...]
        lnb = lnb_ref[...]

        def split_pairs(val, c):
            for p in range(4):
                slab = val[:, p * BLOCK:(p + 1) * BLOCK]
                rhs_s[p, 0:BLOCK, c * BLOCK:(c + 1) * BLOCK] = jnp.where(lo, slab, 0.0).astype(act)
                rhs_s[p, BLOCK:, c * BLOCK:(c + 1) * BLOCK] = jnp.where(lo, 0.0, slab).astype(act)

        for c in range(nb):
            rows = slice(c * BLOCK, (c + 1) * BLOCK)
            za = gate_s[0, rows, :]
            mixed_ref[rows, 0:ATTN_W] = (o_s[rows, :] * (za * _sigmoid(za))).astype(act)
            us = gate_s[1, rows, :]
            vs = gate_s[2, rows, :]
            cu = _norm_cdf(us)
            cv = _norm_cdf(vs)
            cdfu_s[rows, :] = cu
            cdfv_s[rows, :] = cv
            u = us * cu
            vg = vs * cv
            vc = vg - _mean_rows(vg)
            r2 = lax.rsqrt(_mean_rows(vc * vc) + NORM_EPS)
            vhat = vc * r2
            r2_s[rows, :] = r2
            vhat_s[rows, :] = vhat
            u_s[rows, :] = u
            split_pairs(vhat * lng + lnb, c)
        for p in range(4):
            cols = slice(p * BLOCK, (p + 1) * BLOCK)
            mix = _mm(wf_v[p], rhs_s[p])
            for c in range(nb):
                mix_s[c * BLOCK:(c + 1) * BLOCK, cols] = mix[:, c * BLOCK:(c + 1) * BLOCK] + bexp_ref[:, cols]
        for c in range(nb):
            rows = slice(c * BLOCK, (c + 1) * BLOCK)
            zs = gate_s[3, rows, :]
            mixed_ref[rows, ATTN_W:] = (u_s[rows, :] * mix_s[rows, :] * (zs * _sigmoid(zs))).astype(act)

        g3 = g3_ref[...]
        proj_o = _mm(mixed_ref[...], wout_v[...])
        for c in range(nb):
            rows = slice(c * BLOCK, (c + 1) * BLOCK)
            out = x_ref[rows, :] + proj_o[rows, :] + bout_ref[...]
            r3, on = rms(out)
            e = on * g3 - tgt_ref[rows, :]
            e2 = _rows8(e * e)
            acc_loss[...] += sum(e2[:, k * 128:(k + 1) * 128] for k in range(D_MODEL // 128)) * (0.5 / D_MODEL)
            dy = e * (1.0 / D_MODEL)
            acc_g3[...] += _rows8(dy * on)
            don = dy * g3
            dout = r3 * (don - on * _mean_rows(don * on))
            doutf_s[rows, :] = dout
            dout_ref[rows, :] = dout.astype(act)
            acc_bout[...] += _rows8(dout)
        dmix_s[...] = _mm_nt(dout_ref[...], wout_v[...])

        for c in range(nb):
            rows = slice(c * BLOCK, (c + 1) * BLOCK)
            dso = dmix_s[rows, ATTN_W:]
            u = u_s[rows, :]
            mix = mix_s[rows, :]
            zs = gate_s[3, rows, :]
            sg = _sigmoid(zs)
            sgs = zs * sg
            du = dso * mix * sgs
            dmx = dso * u * sgs
            dzs = dso * u * mix * (sg * (1.0 + zs * (1.0 - sg)))
            us = gate_s[1, rows, :]
            dus = du * (cdfu_s[rows, :] + us * _norm_pdf(us))
            vln = (vhat_s[rows, :] * lng + lnb).astype(act)
            for p in range(4):
                vlnp_s[p, :, c * BLOCK:(c + 1) * BLOCK] = vln[:, p * BLOCK:(p + 1) * BLOCK]
            split_pairs(dmx, c)
            acc_dbs[...] += dmx
            for off, val in ((OFF_US, dus), (OFF_ZS, dzs)):
                dproj_ref[rows, off:off + 512] = val.astype(act)
                acc_bin[:, off:off + 512] += _rows8(val)
        for p in range(4):
            dvln = _mm(wb_v[p], rhs_s[p])
            for c in range(nb):
                dvln_s[c * BLOCK:(c + 1) * BLOCK, p * BLOCK:(p + 1) * BLOCK] = dvln[:, c * BLOCK:(c + 1) * BLOCK]
            acc_dws[(2 * p) * BLOCK:(2 * p + 1) * BLOCK, :] += _mm_nt(rhs_s[p, 0:BLOCK, :], vlnp_s[p])
            acc_dws[(2 * p + 1) * BLOCK:(2 * p + 2) * BLOCK, :] += _mm_nt(rhs_s[p, BLOCK:, :], vlnp_s[p])
        for c in range(nb):
            rows = slice(c * BLOCK, (c + 1) * BLOCK)
            dvln = dvln_s[rows, :]
            vhat = vhat_s[rows, :]
            acc_lng[...] += _rows8(dvln * vhat)
            acc_lnb[...] += _rows8(dvln)
            dvhat = dvln * lng
            dvg = r2_s[rows, :] * (dvhat - _mean_rows(dvhat) - vhat * _mean_rows(dvhat * vhat))
            vs = gate_s[2, rows, :]
            dvs = dvg * (cdfv_s[rows, :] + vs * _norm_pdf(vs))
            dproj_ref[rows, OFF_VS:OFF_VS + 512] = dvs.astype(act)
            acc_bin[:, OFF_VS:OFF_VS + 512] += _rows8(dvs)

        dkf_s[...] = jnp.zeros(dkf_s.shape, F32)
        dvf_s[...] = jnp.zeros(dvf_s.shape, F32)
        for b in range(nb):
            rows = slice(b * BLOCK, (b + 1) * BLOCK)
            band = slice(b * BLOCK, (b + 2) * BLOCK)
            za = gate_s[0, rows, :]
            sg = _sigmoid(za)
            dao = dmix_s[rows, 0:ATTN_W]
            o = o_s[rows, :]
            do = dao * (za * sg)
            dza = dao * o * (sg * (1.0 + za * (1.0 - sg)))
            dproj_ref[rows, OFF_ZA:OFF_US] = dza.astype(act)
            acc_bin[:, OFF_ZA:OFF_US] += _rows8(dza)
            for g in range(2):
                do_parts, delta_parts = [], []
                for pr in range(2):
                    cols = slice((2 * g + pr) * BLOCK, (2 * g + pr + 1) * BLOCK)
                    d_pair = do[:, cols]
                    prod = d_pair * o[:, cols]
                    do_parts += [jnp.where(lo, d_pair, 0.0).astype(act), jnp.where(lo, 0.0, d_pair).astype(act)]
                    delta_parts += [jnp.sum(jnp.where(lo, prod, 0.0), axis=1, keepdims=True),
                                    jnp.sum(jnp.where(lo, 0.0, prod), axis=1, keepdims=True)]
                do_st = jnp.concatenate(do_parts, axis=0)
                delta = jnp.concatenate(delta_parts, axis=0)
                p = p_s[b * 2 + g]
                dp = _mm_nt(do_st, v2_s[g, band, :])
                ds = p * (dp - delta)
                sink_t = ps_s[b * 2 + g] * delta
                for k in range(4):
                    acc_sink[4 * g + k:4 * g + k + 1, :] += -jnp.sum(sink_t[k * BLOCK:(k + 1) * BLOCK], axis=0, keepdims=True)
                ds_a = ds.astype(act)
                dq2 = _mm(ds_a, k2_s[g, band, :]) * SCALE
                for pr in range(2):
                    cols = slice((2 * g + pr) * BLOCK, (2 * g + pr + 1) * BLOCK)
                    dq = jnp.where(lo, dq2[(2 * pr) * BLOCK:(2 * pr + 1) * BLOCK], dq2[(2 * pr + 1) * BLOCK:(2 * pr + 2) * BLOCK])
                    dproj_ref[rows, cols] = dq.astype(act)
                    acc_bin[:, cols] += _rows8(dq)
                lo_band = lax.broadcasted_iota(jnp.int32, (2 * BLOCK, BLOCK), 1) < HEAD_DIM
                mine = lo_band if g == 0 else jnp.logical_not(lo_band)
                for acc, lhs, rhs in ((dkf_s, ds_a, stacked_q(b, g)), (dvf_s, p.astype(act), do_st)):
                    d2 = _mm_tn(lhs, rhs)
                    full = d2 + pltpu.roll(d2, HEAD_DIM, 1)
                    acc[band, :] += jnp.where(mine, full, 0.0)
        for acc, carry, off in ((dkf_s, carryk_s, OFF_K), (dvf_s, carryv_s, OFF_V)):
            acc[t:t + BLOCK, :] += carry[...]
            carry[...] = acc[0:BLOCK, :]
            d = acc[BLOCK:, :]
            dproj_ref[:, off:off + BLOCK] = d.astype(act)
            acc_bin[:, off:off + BLOCK] += _rows8(d)

        dh = _mm(dproj_ref[...], wint_v[...])
        dprojt_ref[...] = dproj_ref[...].T
        mixedt_ref[...] = mixed_ref[...].T
        for c in range(nb):
            rows = slice(c * BLOCK, (c + 1) * BLOCK)
            r1 = r1_s[rows, :]
            xn = x_ref[rows, :] * r1
            dhc = dh[rows, :]
            acc_g1[...] += _rows8(dhc * xn)
            dxn = dhc * g1
            dx_ref[rows, :] = doutf_s[rows, :] + r1 * (dxn - xn * _mean_rows(dxn * xn))

        @pl.when(i == nt - 1)
        def _():
            tril = lax.broadcasted_iota(jnp.int32, (BLOCK, BLOCK), 0) >= lane128
            for hh in range(N_SGU_HEADS):
                rws = slice(hh * BLOCK, (hh + 1) * BLOCK)
                sw_ref[rws, :] = jnp.where(tril, acc_dws[rws, :], 0.0)
            vec_ref[...] = jnp.zeros((VEC_ROWS, 128), F32)

            def put(row0, acc):
                s = jnp.sum(acc[...], axis=0, keepdims=True)
                for k in range(acc.shape[1] // 128):
                    vec_ref[row0 + k:row0 + k + 1, :] = s[:, k * 128:(k + 1) * 128]

            put(R_G1, acc_g1)
            put(R_BIN, acc_bin)
            put(R_LNG, acc_lng)
            put(R_LNB, acc_lnb)
            put(R_BOUT, acc_bout)
            put(R_G3, acc_g3)
            vec_ref[R_SINK:R_SINK + 1, :] = jnp.sum(
                jnp.where(lax.broadcasted_iota(jnp.int32, (8, 128), 0) == lax.broadcasted_iota(jnp.int32, (8, 128), 1),
                          acc_sink[...], 0.0), axis=0, keepdims=True)
            vec_ref[R_LOSS:R_LOSS + 1, :] = jnp.zeros((1, 128), F32) + jnp.sum(acc_loss[...])
            dbs_t = acc_dbs[...].T
            vec_ref[R_SGUB:R_SGUB + 8, :] = jnp.sum(dbs_t.reshape(N_SGU_HEADS, SGU_W // N_SGU_HEADS, BLOCK), axis=1)

    full = lambda shape: pl.BlockSpec(shape, lambda i: (0,) * len(shape))
    tok = lambda w: pl.BlockSpec((t, w), lambda i: (nt - 1 - i, 0))
    in_specs = [
        pl.BlockSpec(memory_space=pltpu.SMEM),
        tok(D_MODEL),
        pl.BlockSpec((BLOCK, D_MODEL), lambda i: (jnp.maximum((nt - 1 - i) * nb - 1, 0), 0)),
        tok(D_MODEL),
        full((1, D_MODEL)), full((1, IN_W)), full((1, SGU_W)), full((1, SGU_W)),
        full((N_SGU_HEADS, BLOCK, BLOCK)), full((BLOCK, SGU_W)), full((1, D_MODEL)), full((1, D_MODEL)),
        pl.BlockSpec(memory_space=pl.ANY), pl.BlockSpec(memory_space=pl.ANY),
    ]
    out_shape = [
        jax.ShapeDtypeStruct((seq, D_MODEL), F32),
        jax.ShapeDtypeStruct((seq, D_MODEL), act),
        jax.ShapeDtypeStruct((IN_W, seq), act),
        jax.ShapeDtypeStruct((D_MODEL, seq), act),
        jax.ShapeDtypeStruct((seq, D_MODEL), act),
        jax.ShapeDtypeStruct((SW_ROWS, 128), F32),
        jax.ShapeDtypeStruct((VEC_ROWS, 128), F32),
    ]
    tok_t = lambda w: pl.BlockSpec((w, t), lambda i: (0, nt - 1 - i))
    out_specs = [tok(D_MODEL), tok(D_MODEL), tok_t(IN_W), tok_t(D_MODEL), tok(D_MODEL), full((SW_ROWS, 128)), full((VEC_ROWS, 128))]
    vm = pltpu.VMEM
    scratch = [
        vm((t, IN_W), act), vm((t, D_MODEL), act),
        vm((IN_W, D_MODEL), act), vm((D_MODEL, D_MODEL), act),
        vm((4, BLOCK, 2 * BLOCK), act), vm((4, BLOCK, 2 * BLOCK), act),
        vm((t, ATTN_W), act),
        vm((t + BLOCK, BLOCK), F32), vm((t + BLOCK, BLOCK), F32),
        vm((2, t + BLOCK, BLOCK), act), vm((2, t + BLOCK, BLOCK), act),
        vm((4, t, 512), F32),
        vm((2 * nb, 4 * BLOCK, 2 * BLOCK), F32), vm((2 * nb, 4 * BLOCK, 1), F32),
        vm((t, ATTN_W), F32), vm((t, SGU_W), F32), vm((t, SGU_W), F32), vm((t, SGU_W), F32), vm((t, 1), F32),
        vm((t, SGU_W), F32), vm((t, SGU_W), F32), vm((t, 1), F32),
        vm((t, D_MODEL), F32), vm((t, D_MODEL), F32),
        vm((t + BLOCK, BLOCK), F32), vm((t + BLOCK, BLOCK), F32), vm((BLOCK, BLOCK), F32), vm((BLOCK, BLOCK), F32),
        vm((8, IN_W), F32), vm((8, D_MODEL), F32), vm((8, D_MODEL), F32), vm((8, D_MODEL), F32),
        vm((8, SGU_W), F32), vm((8, SGU_W), F32), vm((N_SGU_HEADS * BLOCK, BLOCK), F32), vm((BLOCK, SGU_W), F32),
        vm((8, 128), F32), vm((8, 128), F32),
        vm((t + BLOCK, D_MODEL), act), vm((4, 2 * BLOCK, t), act), vm((4, BLOCK, t), act), vm((t, SGU_W), F32),
        pltpu.SemaphoreType.DMA((2,)),
    ]
    return pl.pallas_call(
        body, name="fused", grid=(nt,), in_specs=in_specs, out_specs=out_specs, out_shape=out_shape,
        scratch_shapes=scratch,
        compiler_params=pltpu.CompilerParams(dimension_semantics=("arbitrary",), vmem_limit_bytes=VMEM_LIMIT),
    )(sinks, x, x, tgt, g1, b_in, ln_g, ln_b, sgu_w, bexp, b_out, g3, win_t, wout)


def _place():
    x, y, c = lax.axis_index("x"), lax.axis_index("y"), lax.axis_index("c")
    chips = [(1 - x, y), (x, 1 - y), (1 - x, 1 - y)]
    return x, y, c, chips


def _gather_call(a_loc, b_loc):
    def body(a_ref, b_ref, ga_ref, gb_ref, send_sems, recv_sems, loc_sems):
        x, y, c, chips = _place()
        me, sibling = (x, y, c), (x, y, 1 - c)
        j = 2 * x + y

        def half(which, cj, hf):
            ref, shard = (ga_ref, W_IN_SHARD) if which == 0 else (gb_ref, W_OUT_SHARD)
            n = shard // 2
            return ref.at[pl.ds(pl.multiple_of(cj * shard + hf * n, 16), n), :]

        def copy(k, src, dst, to):
            return pltpu.make_async_remote_copy(src_ref=src, dst_ref=dst, send_sem=send_sems.at[k], recv_sem=recv_sems.at[k],
                                                device_id=to, device_id_type=MESH)

        own = [pltpu.make_async_copy(a_ref, ga_ref.at[pl.ds(pl.multiple_of(j * W_IN_SHARD, 16), W_IN_SHARD), :], loc_sems.at[0]),
               pltpu.make_async_copy(b_ref, gb_ref.at[pl.ds(pl.multiple_of(j * W_OUT_SHARD, 16), W_OUT_SHARD), :], loc_sems.at[1])]
        for cp in own:
            cp.start()
        srcs = [a_ref.at[pl.ds(pl.multiple_of(c * HALF_A, 16), HALF_A), :], b_ref.at[pl.ds(pl.multiple_of(c * HALF_B, 16), HALF_B), :]]
        first = [copy(2 * r + w, srcs[w], half(w, j, c), (*chip, c)) for r, chip in enumerate(chips) for w in range(2)]
        for cp in first:
            cp.start()
        passed = []
        for r, (cx, cy) in enumerate(chips):
            cj = 2 * cx + cy
            for w in range(2):
                copy(2 * r + w, half(w, cj, c), half(w, cj, c), me).wait_recv()
                fwd = copy(6 + 2 * r + w, half(w, cj, c), half(w, cj, c), sibling)
                fwd.start()
                passed.append(fwd)
        for r, (cx, cy) in enumerate(chips):
            cj = 2 * cx + cy
            for w in range(2):
                copy(6 + 2 * r + w, half(w, cj, 1 - c), half(w, cj, 1 - c), me).wait_recv()
        for cp in first + passed:
            cp.wait_send()
        for cp in own:
            cp.wait()

    hbm = pl.BlockSpec(memory_space=pl.ANY)
    return pl.pallas_call(
        body, name="gather", in_specs=[hbm, hbm], out_specs=[hbm, hbm],
        out_shape=[jax.ShapeDtypeStruct((IN_W, D_MODEL), a_loc.dtype), jax.ShapeDtypeStruct((D_MODEL, D_MODEL), b_loc.dtype)],
        scratch_shapes=[pltpu.SemaphoreType.DMA((12,)), pltpu.SemaphoreType.DMA((12,)), pltpu.SemaphoreType.DMA((2,))],
    )(a_loc, b_loc)


def _wgrad_reduce_call(dproj_t, h, mixed_t, dout, sw, vec):
    wire = jnp.bfloat16
    half_sw = SW_ROWS // 2
    seq = h.shape[0]
    tk = min(1024, seq)
    nk = seq // tk
    n_steps = 2 * N_CHIPS
    rel_of = lambda s: s % 3 if s < 6 else 3
    half_of = lambda s: s // 3 if s < 6 else s - 6
    x, y = lax.axis_index("x"), lax.axis_index("y")
    chip_of = [2 * (1 - x) + (1 - y), 2 * (1 - x) + y, 2 * x + (1 - y), 2 * x + y]
    order = jnp.stack([2 * chip_of[rel_of(s)] + half_of(s) for s in range(n_steps)]).astype(jnp.int32)

    def body(order_ref, dpt_ref, h_hbm, mxt_ref, dout_hbm, sw_ref, small_ref, oa_ref, ob_ref, osw_ref, osmall_ref,
             h_v, dout_v, acc_a, acc_b, sib_a, sib_b, snd_a, snd_b, in_a, in_b, own_a, own_b, fin_a, fin_b,
             all_small, sw_sib, sw_chips, sw_fin, send_sems, recv_sems, loc_sems):
        x, y, c = lax.axis_index("x"), lax.axis_index("y"), lax.axis_index("c")
        me, sibling = (x, y, c), (x, y, 1 - c)
        steps = [(1 - x, 1 - y), (1 - x, y), (x, 1 - y)]
        j = 2 * x + y
        dev = 4 * x + 2 * y + c
        b, k = pl.program_id(0), pl.program_id(1)

        def copy(n, src, dst, to):
            return pltpu.make_async_remote_copy(src_ref=src, dst_ref=dst, send_sem=send_sems.at[n], recv_sem=recv_sems.at[n],
                                                device_id=to, device_id_type=MESH)

        def sw_rows(ref, hf):
            return ref.at[pl.ds(pl.multiple_of(hf * half_sw, 8), half_sw), :]

        sm_first = [copy(16, small_ref, all_small.at[dev], sibling)]
        sm_first += [copy(17 + r, small_ref, all_small.at[dev], (*chip, c)) for r, chip in enumerate(steps)]
        sw_to_sib = copy(23, sw_rows(sw_ref, 1 - c), sw_sib, sibling)

        @pl.when((b == 0) & (k == 0))
        def _():
            all_small[dev] = small_ref[...]
            for cp in sm_first + [sw_to_sib]:
                cp.start()

        def load(kk):
            rows = pl.ds(kk * tk, tk)
            return [pltpu.make_async_copy(h_hbm.at[rows, :], h_v.at[rows, :], loc_sems.at[kk]),
                    pltpu.make_async_copy(dout_hbm.at[rows, :], dout_v.at[rows, :], loc_sems.at[nk + kk])]

        for kk in range(nk):
            @pl.when((b == 0) & (k == 0))
            def _():
                for cp in load(kk):
                    cp.start()

        for kk in range(nk):
            @pl.when((b == 0) & (k == kk))
            def _():
                for cp in load(kk):
                    cp.wait()

        tok = pl.ds(pl.multiple_of(k * tk, tk), tk)
        pa = _mm(dpt_ref[...], h_v[tok, :])
        pb = _mm(mxt_ref[...], dout_v[tok, :])
        slot = b % 2

        @pl.when(k == 0)
        def _():
            acc_a[slot] = pa
            acc_b[slot] = pb

        @pl.when(k != 0)
        def _():
            acc_a[slot] += pa
            acc_b[slot] += pb

        def to_sibling(s):
            r = rel_of(s)
            return [copy(r, acc_a.at[s % 2], sib_a.at[r], sibling), copy(4 + r, acc_b.at[s % 2], sib_b.at[r], sibling)]

        def chip_partial(s):
            r = rel_of(s)
            copy(r, sib_a.at[r], sib_a.at[r], me).wait_recv()
            copy(4 + r, sib_b.at[r], sib_b.at[r], me).wait_recv()
            return acc_a[s % 2] + sib_a[r], acc_b[s % 2] + sib_b[r]

        def to_owner(r):
            return [copy(8 + r, snd_a.at[r], in_a.at[r], (*steps[r], c)), copy(11 + r, snd_b.at[r], in_b.at[r], (*steps[r], c))]

        for s in range(n_steps):
            if s >= 1:
                sp = s - 1

                @pl.when((b == s) & (k == 0) & (c == half_of(sp)))
                def _():
                    ta, tb = chip_partial(sp)
                    r = rel_of(sp)
                    if r < 3:
                        snd_a[r] = ta.astype(wire)
                        snd_b[r] = tb.astype(wire)
                        for cp in to_owner(r):
                            cp.start()
                    else:
                        own_a[...] = ta
                        own_b[...] = tb

                @pl.when((b == s) & (k == nk - 1) & (c != half_of(sp)))
                def _():
                    for cp in to_sibling(sp):
                        cp.wait_send()

            @pl.when((b == s) & (k == nk - 1) & (c != half_of(s)))
            def _():
                for cp in to_sibling(s):
                    cp.start()

        @pl.when((b == n_steps - 1) & (k == nk - 1))
        def _():
            sm_passed = []
            for r, (cx, cy) in enumerate(steps):
                d = 4 * cx + 2 * cy + c
                copy(17 + r, all_small.at[d], all_small.at[d], me).wait_recv()
                fwd = copy(20 + r, all_small.at[d], all_small.at[d], sibling)
                fwd.start()
                sm_passed.append(fwd)
            copy(23, sw_sib, sw_sib, me).wait_recv()
            sw_chips[j] = (sw_rows(sw_ref, c)[...] + sw_sib[...]).astype(wire)
            sw_ici = [copy(24 + r, sw_chips.at[j], sw_chips.at[j], (*chip, c)) for r, chip in enumerate(steps)]
            for cp in sw_ici:
                cp.start()

            last = n_steps - 1

            @pl.when(c == half_of(last))
            def _():
                own_a[...], own_b[...] = chip_partial(last)

            @pl.when(c != half_of(last))
            def _():
                for cp in to_sibling(last):
                    cp.wait_send()

            tot_a = own_a[...]
            tot_b = own_b[...]
            for s in range(3):
                copy(8 + s, in_a.at[s], in_a.at[s], me).wait_recv()
                copy(11 + s, in_b.at[s], in_b.at[s], me).wait_recv()
                tot_a = tot_a + in_a[s].astype(F32)
                tot_b = tot_b + in_b[s].astype(F32)
            mine_a = oa_ref.at[pl.ds(pl.multiple_of(c * HALF_A, 8), HALF_A), :]
            mine_b = ob_ref.at[pl.ds(pl.multiple_of(c * HALF_B, 8), HALF_B), :]
            mine_a[...] = tot_a
            mine_b[...] = tot_b
            back = [copy(14, mine_a, fin_a, sibling), copy(15, mine_b, fin_b, sibling)]
            for cp in back:
                cp.start()

            copy(16, small_ref, all_small.at[dev ^ 1], me).wait_recv()
            for r, (cx, cy) in enumerate(steps):
                d = 4 * cx + 2 * cy + (1 - c)
                copy(20 + r, all_small.at[d], all_small.at[d], me).wait_recv()
            tot = all_small[0]
            for d in range(1, N_DEV):
                tot = tot + all_small[d]
            osmall_ref[...] = tot

            for r, (cx, cy) in enumerate(steps):
                cj = 2 * cx + cy
                copy(24 + r, sw_chips.at[cj], sw_chips.at[cj], me).wait_recv()
            tot_sw = sw_chips[0].astype(F32)
            for q in range(1, N_CHIPS):
                tot_sw = tot_sw + sw_chips[q].astype(F32)
            sw_rows(osw_ref, c)[...] = tot_sw
            sw_back = copy(27, sw_rows(osw_ref, c), sw_fin, sibling)
            sw_back.start()

            copy(14, fin_a, fin_a, me).wait_recv()
            copy(15, fin_b, fin_b, me).wait_recv()
            copy(27, sw_fin, sw_fin, me).wait_recv()
            oa_ref[pl.ds(pl.multiple_of((1 - c) * HALF_A, 8), HALF_A), :] = fin_a[...]
            ob_ref[pl.ds(pl.multiple_of((1 - c) * HALF_B, 8), HALF_B), :] = fin_b[...]
            sw_rows(osw_ref, 1 - c)[...] = sw_fin[...]
            sends = sm_first + sm_passed + sw_ici + back + [sw_to_sib, sw_back]
            for s in range(3):
                sends += to_owner(s)
            for cp in sends:
                cp.wait_send()

    vmem = pl.BlockSpec(memory_space=pltpu.VMEM)
    vm = pltpu.VMEM
    grid_spec = pltpu.PrefetchScalarGridSpec(
        num_scalar_prefetch=1, grid=(n_steps, nk),
        in_specs=[pl.BlockSpec((HALF_A, tk), lambda b, k, o: (o[b], k)), pl.BlockSpec(memory_space=pl.ANY),
                  pl.BlockSpec((HALF_B, tk), lambda b, k, o: (o[b], k)), pl.BlockSpec(memory_space=pl.ANY),
                  vmem, vmem],
        out_specs=[vmem, vmem, vmem, vmem],
        scratch_shapes=[vm((seq, D_MODEL), h.dtype), vm((seq, D_MODEL), dout.dtype),
                        vm((2, HALF_A, D_MODEL), F32), vm((2, HALF_B, D_MODEL), F32),
                        vm((N_CHIPS, HALF_A, D_MODEL), F32), vm((N_CHIPS, HALF_B, D_MODEL), F32),
                        vm((3, HALF_A, D_MODEL), wire), vm((3, HALF_B, D_MODEL), wire),
                        vm((3, HALF_A, D_MODEL), wire), vm((3, HALF_B, D_MODEL), wire),
                        vm((HALF_A, D_MODEL), F32), vm((HALF_B, D_MODEL), F32),
                        vm((HALF_A, D_MODEL), F32), vm((HALF_B, D_MODEL), F32),
                        vm((N_DEV, VEC_ROWS, 128), F32), vm((half_sw, 128), F32), vm((N_CHIPS, half_sw, 128), wire),
                        vm((half_sw, 128), F32),
                        pltpu.SemaphoreType.DMA((28,)), pltpu.SemaphoreType.DMA((28,)), pltpu.SemaphoreType.DMA((2 * nk,))])
    return pl.pallas_call(
        body, name="wgrad_reduce", grid_spec=grid_spec,
        out_shape=[jax.ShapeDtypeStruct((W_IN_SHARD, D_MODEL), F32), jax.ShapeDtypeStruct((W_OUT_SHARD, D_MODEL), F32),
                   jax.ShapeDtypeStruct((SW_ROWS, 128), F32), jax.ShapeDtypeStruct((VEC_ROWS, 128), F32)],
        compiler_params=pltpu.CompilerParams(dimension_semantics=("arbitrary", "arbitrary"), vmem_limit_bytes=VMEM_LIMIT),
    )(order, dproj_t, h, mixed_t, dout, sw, vec)


def _adamw(w, g, m, v):
    nm = ADAM_B1 * m + (1.0 - ADAM_B1) * g
    nv = ADAM_B2 * v + (1.0 - ADAM_B2) * (g * g)
    m_hat = nm / (1.0 - ADAM_B1 ** ADAM_STEP)
    v_hat = nv / (1.0 - ADAM_B2 ** ADAM_STEP)
    return -ADAM_LR * (m_hat / (jnp.sqrt(v_hat) + ADAM_EPS) + ADAM_WD * w), nm, nv


def _adamw_call(w, g, m, v, block_rows, name):
    rows, cols = w.shape

    def body(w_ref, g_ref, m_ref, v_ref, go_ref, d_ref, nm_ref, nv_ref):
        gg = g_ref[...]
        go_ref[...] = gg
        d_ref[...], nm_ref[...], nv_ref[...] = _adamw(w_ref[...], gg, m_ref[...], v_ref[...])

    spec = pl.BlockSpec((block_rows, cols), lambda i: (i, 0))
    return pl.pallas_call(
        body, name=name, grid=(rows // block_rows,), in_specs=[spec] * 4, out_specs=[spec] * 4,
        out_shape=[jax.ShapeDtypeStruct((rows, cols), F32)] * 4,
        compiler_params=pltpu.CompilerParams(dimension_semantics=("arbitrary",)),
    )(w, g, m, v)


_SMALL = (("norm_g", (1, D_MODEL), R_G1), ("b_in", (1, IN_W), R_BIN), ("attn_sinks", (1, 8), R_SINK),
          ("sgu_ln_g", (1, SGU_W), R_LNG), ("sgu_ln_b", (1, SGU_W), R_LNB), ("sgu_b", (N_SGU_HEADS, BLOCK), R_SGUB),
          ("b_out", (1, D_MODEL), R_BOUT), ("final_norm_g", (1, D_MODEL), R_G3))


def _adamw_small_call(sw_g, vec_g, sgu_w3, ws, ms, vs):
    n = len(_SMALL)

    def body(*refs):
        sw_ref, vec_ref = refs[0], refs[1]
        w3 = refs[2:5]
        w_refs, m_refs, v_refs = refs[5:5 + n], refs[5 + n:5 + 2 * n], refs[5 + 2 * n:5 + 3 * n]
        outs = refs[5 + 3 * n:]
        outs[0][...] = vec_ref[R_LOSS:R_LOSS + 1, 0:1]
        g = sw_ref[...]
        outs[1][...] = g
        outs[2][...], outs[3][...], outs[4][...] = _adamw(w3[0][...], g, w3[1][...], w3[2][...])
        for k, (_, shape, row) in enumerate(_SMALL):
            if shape[0] == 1 and shape[1] >= 128:
                g = jnp.concatenate([vec_ref[row + q:row + q + 1, :] for q in range(shape[1] // 128)], axis=1)
            else:
                g = vec_ref[row:row + shape[0], 0:shape[1]]
            o = outs[5 + 4 * k:9 + 4 * k]
            o[0][...] = g
            o[1][...], o[2][...], o[3][...] = _adamw(w_refs[k][...], g, m_refs[k][...], v_refs[k][...])

    vmem = pl.BlockSpec(memory_space=pltpu.VMEM)
    out_shape = [jax.ShapeDtypeStruct((1, 1), F32)] + [jax.ShapeDtypeStruct((SW_ROWS, 128), F32)] * 4
    for _, shape, _ in _SMALL:
        out_shape += [jax.ShapeDtypeStruct(shape, F32)] * 4
    args = [sw_g, vec_g, *sgu_w3, *ws, *ms, *vs]
    return pl.pallas_call(
        body, name="adamw_small", in_specs=[vmem] * len(args), out_specs=[vmem] * len(out_shape), out_shape=out_shape,
    )(*args)


def kernel(x, norm_g, w_in, b_in, attn_sinks, sgu_ln_g, sgu_ln_b, sgu_w, sgu_b, w_out, b_out, final_norm_g, loss_target, m_norm_g, m_w_in, m_b_in, m_attn_sinks, m_sgu_ln_g, m_sgu_ln_b, m_sgu_w, m_sgu_b, m_w_out, m_b_out, m_final_norm_g, v_norm_g, v_w_in, v_b_in, v_attn_sinks, v_sgu_ln_g, v_sgu_ln_b, v_sgu_w, v_sgu_b, v_w_out, v_b_out, v_final_norm_g):
    seq = x.shape[1]
    win_t, wout = _gather_call(w_in[0].T.astype(MXU_DTYPE), w_out[0].astype(MXU_DTYPE))
    bexp = jnp.repeat(sgu_b[0].T, SGU_W // N_SGU_HEADS, axis=1)
    dx, h, dproj_t, mixed_t, dout, sw, vec = _fused_call(
        x[0], loss_target[0], attn_sinks[0], norm_g, b_in, sgu_ln_g, sgu_ln_b, sgu_w[0], bexp, b_out,
        final_norm_g.reshape(1, D_MODEL), win_t, wout)
    ga_t, g_w_out, sw, vec = _wgrad_reduce_call(dproj_t, h, mixed_t, dout, sw, vec)

    names = ["norm_g", "w_in", "b_in", "attn_sinks", "sgu_ln_g", "sgu_ln_b", "sgu_w", "sgu_b", "w_out", "b_out", "final_norm_g"]
    res = {}
    res["w_in"] = [a.T[None] for a in _adamw_call(w_in[0].T, ga_t, m_w_in[0].T, v_w_in[0].T, W_IN_SHARD // 4, "adamw_w_in")]
    res["w_out"] = [a[None] for a in _adamw_call(w_out[0], g_w_out, m_w_out[0], v_w_out[0], HALF_B, "adamw_w_out")]
    given = dict(norm_g=(norm_g, m_norm_g, v_norm_g), b_in=(b_in, m_b_in, v_b_in), attn_sinks=(attn_sinks, m_attn_sinks, v_attn_sinks),
                 sgu_ln_g=(sgu_ln_g, m_sgu_ln_g, v_sgu_ln_g), sgu_ln_b=(sgu_ln_b, m_sgu_ln_b, v_sgu_ln_b),
                 sgu_b=(sgu_b, m_sgu_b, v_sgu_b), b_out=(b_out, m_b_out, v_b_out),
                 final_norm_g=(final_norm_g, m_final_norm_g, v_final_norm_g))
    wmv = [[given[n][k].reshape(shape) for n, shape, _ in _SMALL] for k in range(3)]
    outs = _adamw_small_call(sw, vec, [a.reshape(SW_ROWS, BLOCK) for a in (sgu_w, m_sgu_w, v_sgu_w)], *wmv)
    loss = outs[0].reshape(())
    res["sgu_w"] = [a.reshape(sgu_w.shape) for a in outs[1:5]]
    for k, (n, _, _) in enumerate(_SMALL):
        res[n] = [a.reshape(given[n][0].shape) for a in outs[5 + 4 * k:9 + 4 * k]]

    return (loss, dx[None], *[res[n][0] for n in names], *[res[n][1] for n in names], *[res[n][2] for n in names],
            *[res[n][3] for n in names])
```

```python
import functools
import math

import jax
import jax.numpy as jnp
from jax import lax
from jax.experimental import pallas as pl
from jax.experimental.pallas import tpu as pltpu

F32 = jnp.float32
MXU_DTYPE = jnp.bfloat16

D_MODEL = 1024
HEAD_DIM = 64
ATTN_W = 512
SGU_W = 512
N_SGU_HEADS = 8
BLOCK = 128
IN_W = 2816
OFF_K, OFF_V, OFF_ZA, OFF_US, OFF_VS, OFF_ZS = 512, 640, 768, 1280, 1792, 2304
NORM_EPS = 1e-5
NEG_INF = -1e30
SCALE = HEAD_DIM ** -0.5
SQRT_HALF = math.sqrt(0.5)
INV_SQRT_2PI = 1.0 / math.sqrt(2.0 * math.pi)

N_CHIPS = 4
N_DEV = 8
W_IN_SHARD = IN_W // N_CHIPS
W_OUT_SHARD = D_MODEL // N_CHIPS
HALF_A = W_IN_SHARD // 2
HALF_B = W_OUT_SHARD // 2

TILE = 256
VMEM_LIMIT = 56 * 1024 * 1024

ADAM_LR, ADAM_B1, ADAM_B2, ADAM_EPS, ADAM_WD, ADAM_STEP = 0.001, 0.9, 0.999, 1e-08, 0.01, 10

SW_ROWS = N_SGU_HEADS * BLOCK
R_G1, R_BIN, R_SINK, R_LOSS, R_LNG, R_LNB, R_SGUB, R_BOUT, R_G3 = 0, 8, 32, 40, 48, 56, 64, 72, 80
VEC_ROWS = 88

MESH = pl.DeviceIdType.MESH


def _mm(a, b):
    return jnp.dot(a, b, preferred_element_type=F32)


def _mm_nt(a, b):
    return lax.dot_general(a, b, (((1,), (1,)), ((), ())), preferred_element_type=F32)


def _mm_tn(a, b):
    return lax.dot_general(a, b, (((0,), (0,)), ((), ())), preferred_element_type=F32)


def _sigmoid(z):
    return 1.0 / (1.0 + jnp.exp(-z))


def _norm_cdf(z):
    return 0.5 * (1.0 + lax.erf(z * SQRT_HALF))


def _norm_pdf(z):
    return jnp.exp(-0.5 * z * z) * INV_SQRT_2PI


def _rows8(v):
    r, n = v.shape
    return jnp.sum(v.reshape(r // 8, 8, n), axis=0)


def _mean_rows(v):
    return jnp.sum(v, axis=1, keepdims=True) * (1.0 / v.shape[1])


def _fused_call(x, tgt, sinks, g1, b_in, ln_g, ln_b, sgu_w, bexp, b_out, g3, win_t, wout):
    seq = x.shape[0]
    t = TILE
    nt = seq // t
    nb = t // BLOCK
    act = MXU_DTYPE

    def body(sinks_ref, x_ref, xh_ref, tgt_ref, g1_ref, bin_ref, lng_ref, lnb_ref, sguw_ref, bexp_ref, bout_ref, g3_ref,
             wint_hbm, wout_hbm,
             dx_ref, h_ref, dprojt_ref, mixedt_ref, dout_ref, sw_ref, vec_ref,
             dproj_ref, mixed_ref, wint_v, wout_v, wf_v, wb_v, q_s, kf_s, vf_s, k2_s, v2_s, gate_s, p_s, ps_s, o_s, u_s, mix_s, vhat_s, r2_s,
             cdfu_s, cdfv_s, r1_s, doutf_s, dmix_s, dkf_s, dvf_s, carryk_s, carryv_s,
             acc_bin, acc_g1, acc_bout, acc_g3, acc_lng, acc_lnb, acc_dws, acc_dbs, acc_sink, acc_loss,
             h_s, rhs_s, vlnp_s, dvln_s, sems):
        i = pl.program_id(0)
        tile = nt - 1 - i
        lane128 = lax.broadcasted_iota(jnp.int32, (BLOCK, BLOCK), 1)
        lo = lane128 < HEAD_DIM

        @pl.when(i == 0)
        def _():
            cp_a = pltpu.make_async_copy(wint_hbm, wint_v, sems.at[0])
            cp_b = pltpu.make_async_copy(wout_hbm, wout_v, sems.at[1])
            cp_a.start()
            cp_b.start()
            for acc in (acc_bin, acc_g1, acc_bout, acc_g3, acc_lng, acc_lnb, acc_dws, acc_dbs, acc_sink, acc_loss,
                        carryk_s, carryv_s):
                acc[...] = jnp.zeros(acc.shape, F32)
            tril = lax.broadcasted_iota(jnp.int32, (BLOCK, BLOCK), 0) >= lane128
            for h in range(N_SGU_HEADS):
                w = jnp.where(tril, sguw_ref[h], 0.0)
                wf_v[h // 2, :, (h % 2) * BLOCK:(h % 2 + 1) * BLOCK] = w.astype(act)
                wb_v[h // 2, :, (h % 2) * BLOCK:(h % 2 + 1) * BLOCK] = w.T.astype(act)
            cp_a.wait()
            cp_b.wait()

        g1 = g1_ref[...]

        def rms(v):
            r = lax.rsqrt(_mean_rows(v * v) + NORM_EPS)
            return r, v * r

        _, xnh = rms(xh_ref[...])
        h_s[0:BLOCK, :] = (xnh * g1).astype(act)
        for c in range(nb):
            rows = slice(c * BLOCK, (c + 1) * BLOCK)
            r, xn = rms(x_ref[rows, :])
            r1_s[rows, :] = r
            h_s[BLOCK + c * BLOCK:BLOCK + (c + 1) * BLOCK, :] = (xn * g1).astype(act)

        h = h_s[BLOCK:, :]
        h_ref[...] = h
        q = _mm_nt(h, wint_v[0:OFF_K, :]) + bin_ref[:, 0:OFF_K]
        q_s[...] = (q * SCALE).astype(act)
        kv = _mm_nt(h_s[...], wint_v[OFF_K:OFF_ZA, :]) + bin_ref[:, OFF_K:OFF_ZA]
        kf_s[...] = kv[:, :BLOCK]
        vf_s[...] = kv[:, BLOCK:]
        for r in range(4):
            cols = slice(OFF_ZA + r * 512, OFF_ZA + (r + 1) * 512)
            gate_s[r] = _mm_nt(h, wint_v[cols, :]) + bin_ref[:, cols]

        lo_kv = lax.broadcasted_iota(jnp.int32, (t + BLOCK, BLOCK), 1) < HEAD_DIM
        for src, dst in ((kf_s, k2_s), (vf_s, v2_s)):
            v = src[...]
            vr = pltpu.roll(v, HEAD_DIM, 1)
            dst[0] = jnp.where(lo_kv, v, vr).astype(act)
            dst[1] = jnp.where(lo_kv, vr, v).astype(act)

        rowi = lax.broadcasted_iota(jnp.int32, (BLOCK, 2 * BLOCK), 0)
        colj = lax.broadcasted_iota(jnp.int32, (BLOCK, 2 * BLOCK), 1)
        in_band = (colj > rowi) & (colj <= rowi + BLOCK)
        row512 = lax.broadcasted_iota(jnp.int32, (4 * BLOCK, 1), 0)

        def stacked_q(b, g):
            parts = []
            for p in range(2):
                slab = q_s[b * BLOCK:(b + 1) * BLOCK, (2 * g + p) * BLOCK:(2 * g + p + 1) * BLOCK]
                parts += [jnp.where(lo, slab, jnp.zeros_like(slab)), jnp.where(lo, jnp.zeros_like(slab), slab)]
            return jnp.concatenate(parts, axis=0)

        def sink_col(g):
            s = [sinks_ref[4 * g + k] for k in range(4)]
            return jnp.where(row512 < BLOCK, s[0], jnp.where(row512 < 2 * BLOCK, s[1], jnp.where(row512 < 3 * BLOCK, s[2], s[3])))

        for b in range(nb):
            band = slice(b * BLOCK, (b + 2) * BLOCK)
            first_key = jnp.where(tile * nb + b > 0, 0, BLOCK)
            valid = in_band & (colj >= first_key)
            valid4 = jnp.concatenate([valid] * 4, axis=0)
            for g in range(2):
                s = _mm_nt(stacked_q(b, g), k2_s[g, band, :])
                s = jnp.where(valid4, s, NEG_INF)
                sk = sink_col(g)
                m = jnp.maximum(jnp.max(s, axis=1, keepdims=True), sk)
                p = jnp.exp(s - m)
                psk = jnp.exp(sk - m)
                inv = 1.0 / (jnp.sum(p, axis=1, keepdims=True) + psk)
                p = p * inv
                p_s[b * 2 + g] = p
                ps_s[b * 2 + g] = psk * inv
                o2 = _mm(p.astype(act), v2_s[g, band, :])
                for pr in range(2):
                    o_s[b * BLOCK:(b + 1) * BLOCK, (2 * g + pr) * BLOCK:(2 * g + pr + 1) * BLOCK] = jnp.where(
                        lo, o2[(2 * pr) * BLOCK:(2 * pr + 1) * BLOCK], o2[(2 * pr + 1) * BLOCK:(2 * pr + 2) * BLOCK])

        lng = lng_ref[...]
        lnb = lnb_ref[...]

        def split_pairs(val, c):
            for p in range(4):
                slab = val[:, p * BLOCK:(p + 1) * BLOCK]
                rhs_s[p, 0:BLOCK, c * BLOCK:(c + 1) * BLOCK] = jnp.where(lo, slab, 0.0).astype(act)
                rhs_s[p, BLOCK:, c * BLOCK:(c + 1) * BLOCK] = jnp.where(lo, 0.0, slab).astype(act)

        for c in range(nb):
            rows = slice(c * BLOCK, (c + 1) * BLOCK)
            za = gate_s[0, rows, :]
            mixed_ref[rows, 0:ATTN_W] = (o_s[rows, :] * (za * _sigmoid(za))).astype(act)
            us = gate_s[1, rows, :]
            vs = gate_s[2, rows, :]
            cu = _norm_cdf(us)
            cv = _norm_cdf(vs)
            cdfu_s[rows, :] = cu
            cdfv_s[rows, :] = cv
            u = us * cu
            vg = vs * cv
            vc = vg - _mean_rows(vg)
            r2 = lax.rsqrt(_mean_rows(vc * vc) + NORM_EPS)
            vhat = vc * r2
            r2_s[rows, :] = r2
            vhat_s[rows, :] = vhat
            u_s[rows, :] = u
            split_pairs(vhat * lng + lnb, c)
        for p in range(4):
            cols = slice(p * BLOCK, (p + 1) * BLOCK)
            mix = _mm(wf_v[p], rhs_s[p])
            for c in range(nb):
                mix_s[c * BLOCK:(c + 1) * BLOCK, cols] = mix[:, c * BLOCK:(c + 1) * BLOCK] + bexp_ref[:, cols]
        for c in range(nb):
            rows = slice(c * BLOCK, (c + 1) * BLOCK)
            zs = gate_s[3, rows, :]
            mixed_ref[rows, ATTN_W:] = (u_s[rows, :] * mix_s[rows, :] * (zs * _sigmoid(zs))).astype(act)

        g3 = g3_ref[...]
        proj_o = _mm(mixed_ref[...], wout_v[...])
        for c in range(nb):
            rows = slice(c * BLOCK, (c + 1) * BLOCK)
            out = x_ref[rows, :] + proj_o[rows, :] + bout_ref[...]
            r3, on = rms(out)
            e = on * g3 - tgt_ref[rows, :]
            e2 = _rows8(e * e)
            acc_loss[...] += sum(e2[:, k * 128:(k + 1) * 128] for k in range(D_MODEL // 128)) * (0.5 / D_MODEL)
            dy = e * (1.0 / D_MODEL)
            acc_g3[...] += _rows8(dy * on)
            don = dy * g3
            dout = r3 * (don - on * _mean_rows(don * on))
            doutf_s[rows, :] = dout
            dout_ref[rows, :] = dout.astype(act)
            acc_bout[...] += _rows8(dout)
        dmix_s[...] = _mm_nt(dout_ref[...], wout_v[...])

        for c in range(nb):
            rows = slice(c * BLOCK, (c + 1) * BLOCK)
            dso = dmix_s[rows, ATTN_W:]
            u = u_s[rows, :]
            mix = mix_s[rows, :]
            zs = gate_s[3, rows, :]
            sg = _sigmoid(zs)
            sgs = zs * sg
            du = dso * mix * sgs
            dmx = dso * u * sgs
            dzs = dso * u * mix * (sg * (1.0 + zs * (1.0 - sg)))
            us = gate_s[1, rows, :]
            dus = du * (cdfu_s[rows, :] + us * _norm_pdf(us))
            vln = (vhat_s[rows, :] * lng + lnb).astype(act)
            for p in range(4):
                vlnp_s[p, :, c * BLOCK:(c + 1) * BLOCK] = vln[:, p * BLOCK:(p + 1) * BLOCK]
            split_pairs(dmx, c)
            acc_dbs[...] += dmx
            for off, val in ((OFF_US, dus), (OFF_ZS, dzs)):
                dproj_ref[rows, off:off + 512] = val.astype(act)
                acc_bin[:, off:off + 512] += _rows8(val)
        for p in range(4):
            dvln = _mm(wb_v[p], rhs_s[p])
            for c in range(nb):
                dvln_s[c * BLOCK:(c + 1) * BLOCK, p * BLOCK:(p + 1) * BLOCK] = dvln[:, c * BLOCK:(c + 1) * BLOCK]
            acc_dws[(2 * p) * BLOCK:(2 * p + 1) * BLOCK, :] += _mm_nt(rhs_s[p, 0:BLOCK, :], vlnp_s[p])
            acc_dws[(2 * p + 1) * BLOCK:(2 * p + 2) * BLOCK, :] += _mm_nt(rhs_s[p, BLOCK:, :], vlnp_s[p])
        for c in range(nb):
            rows = slice(c * BLOCK, (c + 1) * BLOCK)
            dvln = dvln_s[rows, :]
            vhat = vhat_s[rows, :]
            acc_lng[...] += _rows8(dvln * vhat)
            acc_lnb[...] += _rows8(dvln)
            dvhat = dvln * lng
            dvg = r2_s[rows, :] * (dvhat - _mean_rows(dvhat) - vhat * _mean_rows(dvhat * vhat))
            vs = gate_s[2, rows, :]
            dvs = dvg * (cdfv_s[rows, :] + vs * _norm_pdf(vs))
            dproj_ref[rows, OFF_VS:OFF_VS + 512] = dvs.astype(act)
            acc_bin[:, OFF_VS:OFF_VS + 512] += _rows8(dvs)

        dkf_s[...] = jnp.zeros(dkf_s.shape, F32)
        dvf_s[...] = jnp.zeros(dvf_s.shape, F32)
        for b in range(nb):
            rows = slice(b * BLOCK, (b + 1) * BLOCK)
            band = slice(b * BLOCK, (b + 2) * BLOCK)
            za = gate_s[0, rows, :]
            sg = _sigmoid(za)
            dao = dmix_s[rows, 0:ATTN_W]
            o = o_s[rows, :]
            do = dao * (za * sg)
            dza = dao * o * (sg * (1.0 + za * (1.0 - sg)))
            dproj_ref[rows, OFF_ZA:OFF_US] = dza.astype(act)
            acc_bin[:, OFF_ZA:OFF_US] += _rows8(dza)
            for g in range(2):
                do_parts, delta_parts = [], []
                for pr in range(2):
                    cols = slice((2 * g + pr) * BLOCK, (2 * g + pr + 1) * BLOCK)
                    d_pair = do[:, cols]
                    prod = d_pair * o[:, cols]
                    do_parts += [jnp.where(lo, d_pair, 0.0).astype(act), jnp.where(lo, 0.0, d_pair).astype(act)]
                    delta_parts += [jnp.sum(jnp.where(lo, prod, 0.0), axis=1, keepdims=True),
                                    jnp.sum(jnp.where(lo, 0.0, prod), axis=1, keepdims=True)]
                do_st = jnp.concatenate(do_parts, axis=0)
                delta = jnp.concatenate(delta_parts, axis=0)
                p = p_s[b * 2 + g]
                dp = _mm_nt(do_st, v2_s[g, band, :])
                ds = p * (dp - delta)
                sink_t = ps_s[b * 2 + g] * delta
                for k in range(4):
                    acc_sink[4 * g + k:4 * g + k + 1, :] += -jnp.sum(sink_t[k * BLOCK:(k + 1) * BLOCK], axis=0, keepdims=True)
                ds_a = ds.astype(act)
                dq2 = _mm(ds_a, k2_s[g, band, :]) * SCALE
                for pr in range(2):
                    cols = slice((2 * g + pr) * BLOCK, (2 * g + pr + 1) * BLOCK)
                    dq = jnp.where(lo, dq2[(2 * pr) * BLOCK:(2 * pr + 1) * BLOCK], dq2[(2 * pr + 1) * BLOCK:(2 * pr + 2) * BLOCK])
                    dproj_ref[rows, cols] = dq.astype(act)
                    acc_bin[:, cols] += _rows8(dq)
                lo_band = lax.broadcasted_iota(jnp.int32, (2 * BLOCK, BLOCK), 1) < HEAD_DIM
                mine = lo_band if g == 0 else jnp.logical_not(lo_band)
                for acc, lhs, rhs in ((dkf_s, ds_a, stacked_q(b, g)), (dvf_s, p.astype(act), do_st)):
                    d2 = _mm_tn(lhs, rhs)
                    full = d2 + pltpu.roll(d2, HEAD_DIM, 1)
                    acc[band, :] += jnp.where(mine, full, 0.0)
        for acc, carry, off in ((dkf_s, carryk_s, OFF_K), (dvf_s, carryv_s, OFF_V)):
            acc[t:t + BLOCK, :] += carry[...]
            carry[...] = acc[0:BLOCK, :]
            d = acc[BLOCK:, :]
            dproj_ref[:, off:off + BLOCK] = d.astype(act)
            acc_bin[:, off:off + BLOCK] += _rows8(d)

        dh = _mm(dproj_ref[...], wint_v[...])
        dprojt_ref[...] = dproj_ref[...].T
        mixedt_ref[...] = mixed_ref[...].T
        for c in range(nb):
            rows = slice(c * BLOCK, (c + 1) * BLOCK)
            r1 = r1_s[rows, :]
            xn = x_ref[rows, :] * r1
            dhc = dh[rows, :]
            acc_g1[...] += _rows8(dhc * xn)
            dxn = dhc * g1
            dx_ref[rows, :] = doutf_s[rows, :] + r1 * (dxn - xn * _mean_rows(dxn * xn))

        @pl.when(i == nt - 1)
        def _():
            tril = lax.broadcasted_iota(jnp.int32, (BLOCK, BLOCK), 0) >= lane128
            for hh in range(N_SGU_HEADS):
                rws = slice(hh * BLOCK, (hh + 1) * BLOCK)
                sw_ref[rws, :] = jnp.where(tril, acc_dws[rws, :], 0.0)
            vec_ref[...] = jnp.zeros((VEC_ROWS, 128), F32)

            def put(row0, acc):
                s = jnp.sum(acc[...], axis=0, keepdims=True)
                for k in range(acc.shape[1] // 128):
                    vec_ref[row0 + k:row0 + k + 1, :] = s[:, k * 128:(k + 1) * 128]

            put(R_G1, acc_g1)
            put(R_BIN, acc_bin)
            put(R_LNG, acc_lng)
            put(R_LNB, acc_lnb)
            put(R_BOUT, acc_bout)
            put(R_G3, acc_g3)
            vec_ref[R_SINK:R_SINK + 1, :] = jnp.sum(
                jnp.where(lax.broadcasted_iota(jnp.int32, (8, 128), 0) == lax.broadcasted_iota(jnp.int32, (8, 128), 1),
                          acc_sink[...], 0.0), axis=0, keepdims=True)
            vec_ref[R_LOSS:R_LOSS + 1, :] = jnp.zeros((1, 128), F32) + jnp.sum(acc_loss[...])
            dbs_t = acc_dbs[...].T
            vec_ref[R_SGUB:R_SGUB + 8, :] = jnp.sum(dbs_t.reshape(N_SGU_HEADS, SGU_W // N_SGU_HEADS, BLOCK), axis=1)

    full = lambda shape: pl.BlockSpec(shape, lambda i: (0,) * len(shape))
    tok = lambda w: pl.BlockSpec((t, w), lambda i: (nt - 1 - i, 0))
    in_specs = [
        pl.BlockSpec(memory_space=pltpu.SMEM),
        tok(D_MODEL),
        pl.BlockSpec((BLOCK, D_MODEL), lambda i: (jnp.maximum((nt - 1 - i) * nb - 1, 0), 0)),
        tok(D_MODEL),
        full((1, D_MODEL)), full((1, IN_W)), full((1, SGU_W)), full((1, SGU_W)),
        full((N_SGU_HEADS, BLOCK, BLOCK)), full((BLOCK, SGU_W)), full((1, D_MODEL)), full((1, D_MODEL)),
        pl.BlockSpec(memory_space=pl.ANY), pl.BlockSpec(memory_space=pl.ANY),
    ]
    out_shape = [
        jax.ShapeDtypeStruct((seq, D_MODEL), F32),
        jax.ShapeDtypeStruct((seq, D_MODEL), act),
        jax.ShapeDtypeStruct((IN_W, seq), act),
        jax.ShapeDtypeStruct((D_MODEL, seq), act),
        jax.ShapeDtypeStruct((seq, D_MODEL), act),
        jax.ShapeDtypeStruct((SW_ROWS, 128), F32),
        jax.ShapeDtypeStruct((VEC_ROWS, 128), F32),
    ]
    tok_t = lambda w: pl.BlockSpec((w, t), lambda i: (0, nt - 1 - i))
    out_specs = [tok(D_MODEL), tok(D_MODEL), tok_t(IN_W), tok_t(D_MODEL), tok(D_MODEL), full((SW_ROWS, 128)), full((VEC_ROWS, 128))]
    vm = pltpu.VMEM
    scratch = [
        vm((t, IN_W), act), vm((t, D_MODEL), act),
        vm((IN_W, D_MODEL), act), vm((D_MODEL, D_MODEL), act),
        vm((4, BLOCK, 2 * BLOCK), act), vm((4, BLOCK, 2 * BLOCK), act),
        vm((t, ATTN_W), act),
        vm((t + BLOCK, BLOCK), F32), vm((t + BLOCK, BLOCK), F32),
        vm((2, t + BLOCK, BLOCK), act), vm((2, t + BLOCK, BLOCK), act),
        vm((4, t, 512), F32),
        vm((2 * nb, 4 * BLOCK, 2 * BLOCK), F32), vm((2 * nb, 4 * BLOCK, 1), F32),
        vm((t, ATTN_W), F32), vm((t, SGU_W), F32), vm((t, SGU_W), F32), vm((t, SGU_W), F32), vm((t, 1), F32),
        vm((t, SGU_W), F32), vm((t, SGU_W), F32), vm((t, 1), F32),
        vm((t, D_MODEL), F32), vm((t, D_MODEL), F32),
        vm((t + BLOCK, BLOCK), F32), vm((t + BLOCK, BLOCK), F32), vm((BLOCK, BLOCK), F32), vm((BLOCK, BLOCK), F32),
        vm((8, IN_W), F32), vm((8, D_MODEL), F32), vm((8, D_MODEL), F32), vm((8, D_MODEL), F32),
        vm((8, SGU_W), F32), vm((8, SGU_W), F32), vm((N_SGU_HEADS * BLOCK, BLOCK), F32), vm((BLOCK, SGU_W), F32),
        vm((8, 128), F32), vm((8, 128), F32),
        vm((t + BLOCK, D_MODEL), act), vm((4, 2 * BLOCK, t), act), vm((4, BLOCK, t), act), vm((t, SGU_W), F32),
        pltpu.SemaphoreType.DMA((2,)),
    ]
    return pl.pallas_call(
        body, name="fused", grid=(nt,), in_specs=in_specs, out_specs=out_specs, out_shape=out_shape,
        scratch_shapes=scratch,
        compiler_params=pltpu.CompilerParams(dimension_semantics=("arbitrary",), vmem_limit_bytes=VMEM_LIMIT),
    )(sinks, x, x, tgt, g1, b_in, ln_g, ln_b, sgu_w, bexp, b_out, g3, win_t, wout)


def _place():
    x, y, c = lax.axis_index("x"), lax.axis_index("y"), lax.axis_index("c")
    chips = [(1 - x, y), (x, 1 - y), (1 - x, 1 - y)]
    return x, y, c, chips


def _gather_call(a32, b32):
    act = MXU_DTYPE

    def body(a32_hbm, b32_hbm, ga_ref, gb_ref, a32_v, b32_v, a_ref, b_ref, send_sems, recv_sems, loc_sems):
        x, y, c, chips = _place()
        me, sibling = (x, y, c), (x, y, 1 - c)
        j = 2 * x + y
        loads = [pltpu.make_async_copy(a32_hbm, a32_v, loc_sems.at[0]), pltpu.make_async_copy(b32_hbm, b32_v, loc_sems.at[1])]
        for cp in loads:
            cp.start()
        for cp in loads:
            cp.wait()
        a_ref[...] = a32_v[...].astype(act)
        b_ref[...] = b32_v[...].astype(act)

        def half(which, cj, hf):
            ref, shard = (ga_ref, W_IN_SHARD) if which == 0 else (gb_ref, W_OUT_SHARD)
            n = shard // 2
            return ref.at[pl.ds(pl.multiple_of(cj * shard + hf * n, 16), n), :]

        def copy(k, src, dst, to):
            return pltpu.make_async_remote_copy(src_ref=src, dst_ref=dst, send_sem=send_sems.at[k], recv_sem=recv_sems.at[k],
                                                device_id=to, device_id_type=MESH)

        own = [pltpu.make_async_copy(a_ref, ga_ref.at[pl.ds(pl.multiple_of(j * W_IN_SHARD, 16), W_IN_SHARD), :], loc_sems.at[0]),
               pltpu.make_async_copy(b_ref, gb_ref.at[pl.ds(pl.multiple_of(j * W_OUT_SHARD, 16), W_OUT_SHARD), :], loc_sems.at[1])]
        for cp in own:
            cp.start()
        srcs = [a_ref.at[pl.ds(pl.multiple_of(c * HALF_A, 16), HALF_A), :], b_ref.at[pl.ds(pl.multiple_of(c * HALF_B, 16), HALF_B), :]]
        first = [copy(2 * r + w, srcs[w], half(w, j, c), (*chip, c)) for r, chip in enumerate(chips) for w in range(2)]
        for cp in first:
            cp.start()
        passed = []
        for r, (cx, cy) in enumerate(chips):
            cj = 2 * cx + cy
            for w in range(2):
                copy(2 * r + w, half(w, cj, c), half(w, cj, c), me).wait_recv()
                fwd = copy(6 + 2 * r + w, half(w, cj, c), half(w, cj, c), sibling)
                fwd.start()
                passed.append(fwd)
        for r, (cx, cy) in enumerate(chips):
            cj = 2 * cx + cy
            for w in range(2):
                copy(6 + 2 * r + w, half(w, cj, 1 - c), half(w, cj, 1 - c), me).wait_recv()
        for cp in first + passed:
            cp.wait_send()
        for cp in own:
            cp.wait()

    hbm = pl.BlockSpec(memory_space=pl.ANY)
    vm = pltpu.VMEM
    return pl.pallas_call(
        body, name="gather", in_specs=[hbm, hbm], out_specs=[hbm, hbm],
        out_shape=[jax.ShapeDtypeStruct((IN_W, D_MODEL), act), jax.ShapeDtypeStruct((D_MODEL, D_MODEL), act)],
        scratch_shapes=[vm(a32.shape, F32), vm(b32.shape, F32), vm(a32.shape, act), vm(b32.shape, act),
                        pltpu.SemaphoreType.DMA((12,)), pltpu.SemaphoreType.DMA((12,)), pltpu.SemaphoreType.DMA((2,))],
    )(a32, b32)


def _wgrad_reduce_call(dproj_t, h, mixed_t, dout, sw, vec):
    wire = jnp.bfloat16
    half_sw = SW_ROWS // 2
    seq = h.shape[0]
    tk = min(1024, seq)
    nk = seq // tk
    n_steps = 2 * N_CHIPS
    rel_of = lambda s: s % 3 if s < 6 else 3
    half_of = lambda s: s // 3 if s < 6 else s - 6
    x, y = lax.axis_index("x"), lax.axis_index("y")
    chip_of = [2 * (1 - x) + (1 - y), 2 * (1 - x) + y, 2 * x + (1 - y), 2 * x + y]
    order = jnp.stack([2 * chip_of[rel_of(s)] + half_of(s) for s in range(n_steps)]).astype(jnp.int32)

    def body(order_ref, dpt_ref, h_hbm, mxt_ref, dout_hbm, sw_ref, small_ref, oa_ref, ob_ref, osw_ref, osmall_ref,
             h_v, dout_v, acc_a, acc_b, sib_a, sib_b, snd_a, snd_b, in_a, in_b, own_a, own_b, fin_a, fin_b,
             all_small, sw_sib, sw_chips, sw_fin, send_sems, recv_sems, loc_sems):
        x, y, c = lax.axis_index("x"), lax.axis_index("y"), lax.axis_index("c")
        me, sibling = (x, y, c), (x, y, 1 - c)
        steps = [(1 - x, 1 - y), (1 - x, y), (x, 1 - y)]
        j = 2 * x + y
        dev = 4 * x + 2 * y + c
        b, k = pl.program_id(0), pl.program_id(1)

        def copy(n, src, dst, to):
            return pltpu.make_async_remote_copy(src_ref=src, dst_ref=dst, send_sem=send_sems.at[n], recv_sem=recv_sems.at[n],
                                                device_id=to, device_id_type=MESH)

        def sw_rows(ref, hf):
            return ref.at[pl.ds(pl.multiple_of(hf * half_sw, 8), half_sw), :]

        sm_first = [copy(16, small_ref, all_small.at[dev], sibling)]
        sm_first += [copy(17 + r, small_ref, all_small.at[dev], (*chip, c)) for r, chip in enumerate(steps)]
        sw_to_sib = copy(23, sw_rows(sw_ref, 1 - c), sw_sib, sibling)

        @pl.when((b == 0) & (k == 0))
        def _():
            all_small[dev] = small_ref[...]
            for cp in sm_first + [sw_to_sib]:
                cp.start()

        def load(kk):
            rows = pl.ds(kk * tk, tk)
            return [pltpu.make_async_copy(h_hbm.at[rows, :], h_v.at[rows, :], loc_sems.at[kk]),
                    pltpu.make_async_copy(dout_hbm.at[rows, :], dout_v.at[rows, :], loc_sems.at[nk + kk])]

        for kk in range(nk):
            @pl.when((b == 0) & (k == 0))
            def _():
                for cp in load(kk):
                    cp.start()

        for kk in range(nk):
            @pl.when((b == 0) & (k == kk))
            def _():
                for cp in load(kk):
                    cp.wait()

        tok = pl.ds(pl.multiple_of(k * tk, tk), tk)
        pa = _mm(dpt_ref[...], h_v[tok, :])
        pb = _mm(mxt_ref[...], dout_v[tok, :])
        slot = b % 2

        @pl.when(k == 0)
        def _():
            acc_a[slot] = pa
            acc_b[slot] = pb

        @pl.when(k != 0)
        def _():
            acc_a[slot] += pa
            acc_b[slot] += pb

        def to_sibling(s):
            r = rel_of(s)
            return [copy(r, acc_a.at[s % 2], sib_a.at[r], sibling), copy(4 + r, acc_b.at[s % 2], sib_b.at[r], sibling)]

        def chip_partial(s):
            r = rel_of(s)
            copy(r, sib_a.at[r], sib_a.at[r], me).wait_recv()
            copy(4 + r, sib_b.at[r], sib_b.at[r], me).wait_recv()
            return acc_a[s % 2] + sib_a[r], acc_b[s % 2] + sib_b[r]

        def to_owner(r):
            return [copy(8 + r, snd_a.at[r], in_a.at[r], (*steps[r], c)), copy(11 + r, snd_b.at[r], in_b.at[r], (*steps[r], c))]

        for s in range(n_steps):
            if s >= 1:
                sp = s - 1

                @pl.when((b == s) & (k == 0) & (c == half_of(sp)))
                def _():
                    ta, tb = chip_partial(sp)
                    r = rel_of(sp)
                    if r < 3:
                        snd_a[r] = ta.astype(wire)
                        snd_b[r] = tb.astype(wire)
                        for cp in to_owner(r):
                            cp.start()
                    else:
                        own_a[...] = ta
                        own_b[...] = tb

                @pl.when((b == s) & (k == nk - 1) & (c != half_of(sp)))
                def _():
                    for cp in to_sibling(sp):
                        cp.wait_send()

            @pl.when((b == s) & (k == nk - 1) & (c != half_of(s)))
            def _():
                for cp in to_sibling(s):
                    cp.start()

        @pl.when((b == n_steps - 1) & (k == nk - 1))
        def _():
            sm_passed = []
            for r, (cx, cy) in enumerate(steps):
                d = 4 * cx + 2 * cy + c
                copy(17 + r, all_small.at[d], all_small.at[d], me).wait_recv()
                fwd = copy(20 + r, all_small.at[d], all_small.at[d], sibling)
                fwd.start()
                sm_passed.append(fwd)
            copy(23, sw_sib, sw_sib, me).wait_recv()
            sw_chips[j] = (sw_rows(sw_ref, c)[...] + sw_sib[...]).astype(wire)
            sw_ici = [copy(24 + r, sw_chips.at[j], sw_chips.at[j], (*chip, c)) for r, chip in enumerate(steps)]
            for cp in sw_ici:
                cp.start()

            last = n_steps - 1

            @pl.when(c == half_of(last))
            def _():
                own_a[...], own_b[...] = chip_partial(last)

            @pl.when(c != half_of(last))
            def _():
                for cp in to_sibling(last):
                    cp.wait_send()

            tot_a = own_a[...]
            tot_b = own_b[...]
            for s in range(3):
                copy(8 + s, in_a.at[s], in_a.at[s], me).wait_recv()
                copy(11 + s, in_b.at[s], in_b.at[s], me).wait_recv()
                tot_a = tot_a + in_a[s].astype(F32)
                tot_b = tot_b + in_b[s].astype(F32)
            mine_a = oa_ref.at[pl.ds(pl.multiple_of(c * HALF_A, 8), HALF_A), :]
            mine_b = ob_ref.at[pl.ds(pl.multiple_of(c * HALF_B, 8), HALF_B), :]
            mine_a[...] = tot_a
            mine_b[...] = tot_b
            back = [copy(14, mine_a, fin_a, sibling), copy(15, mine_b, fin_b, sibling)]
            for cp in back:
                cp.start()

            copy(16, small_ref, all_small.at[dev ^ 1], me).wait_recv()
            for r, (cx, cy) in enumerate(steps):
                d = 4 * cx + 2 * cy + (1 - c)
                copy(20 + r, all_small.at[d], all_small.at[d], me).wait_recv()
            tot = all_small[0]
            for d in range(1, N_DEV):
                tot = tot + all_small[d]
            osmall_ref[...] = tot

            for r, (cx, cy) in enumerate(steps):
                cj = 2 * cx + cy
                copy(24 + r, sw_chips.at[cj], sw_chips.at[cj], me).wait_recv()
            tot_sw = sw_chips[0].astype(F32)
            for q in range(1, N_CHIPS):
                tot_sw = tot_sw + sw_chips[q].astype(F32)
            sw_rows(osw_ref, c)[...] = tot_sw
            sw_back = copy(27, sw_rows(osw_ref, c), sw_fin, sibling)
            sw_back.start()

            copy(14, fin_a, fin_a, me).wait_recv()
            copy(15, fin_b, fin_b, me).wait_recv()
            copy(27, sw_fin, sw_fin, me).wait_recv()
            oa_ref[pl.ds(pl.multiple_of((1 - c) * HALF_A, 8), HALF_A), :] = fin_a[...]
            ob_ref[pl.ds(pl.multiple_of((1 - c) * HALF_B, 8), HALF_B), :] = fin_b[...]
            sw_rows(osw_ref, 1 - c)[...] = sw_fin[...]
            sends = sm_first + sm_passed + sw_ici + back + [sw_to_sib, sw_back]
            for s in range(3):
                sends += to_owner(s)
            for cp in sends:
                cp.wait_send()

    vmem = pl.BlockSpec(memory_space=pltpu.VMEM)
    vm = pltpu.VMEM
    grid_spec = pltpu.PrefetchScalarGridSpec(
        num_scalar_prefetch=1, grid=(n_steps, nk),
        in_specs=[pl.BlockSpec((HALF_A, tk), lambda b, k, o: (o[b], k)), pl.BlockSpec(memory_space=pl.ANY),
                  pl.BlockSpec((HALF_B, tk), lambda b, k, o: (o[b], k)), pl.BlockSpec(memory_space=pl.ANY),
                  vmem, vmem],
        out_specs=[vmem, vmem, vmem, vmem],
        scratch_shapes=[vm((seq, D_MODEL), h.dtype), vm((seq, D_MODEL), dout.dtype),
                        vm((2, HALF_A, D_MODEL), F32), vm((2, HALF_B, D_MODEL), F32),
                        vm((N_CHIPS, HALF_A, D_MODEL), F32), vm((N_CHIPS, HALF_B, D_MODEL), F32),
                        vm((3, HALF_A, D_MODEL), wire), vm((3, HALF_B, D_MODEL), wire),
                        vm((3, HALF_A, D_MODEL), wire), vm((3, HALF_B, D_MODEL), wire),
                        vm((HALF_A, D_MODEL), F32), vm((HALF_B, D_MODEL), F32),
                        vm((HALF_A, D_MODEL), F32), vm((HALF_B, D_MODEL), F32),
                        vm((N_DEV, VEC_ROWS, 128), F32), vm((half_sw, 128), F32), vm((N_CHIPS, half_sw, 128), wire),
                        vm((half_sw, 128), F32),
                        pltpu.SemaphoreType.DMA((28,)), pltpu.SemaphoreType.DMA((28,)), pltpu.SemaphoreType.DMA((2 * nk,))])
    return pl.pallas_call(
        body, name="wgrad_reduce", grid_spec=grid_spec,
        out_shape=[jax.ShapeDtypeStruct((W_IN_SHARD, D_MODEL), F32), jax.ShapeDtypeStruct((W_OUT_SHARD, D_MODEL), F32),
                   jax.ShapeDtypeStruct((SW_ROWS, 128), F32), jax.ShapeDtypeStruct((VEC_ROWS, 128), F32)],
        compiler_params=pltpu.CompilerParams(dimension_semantics=("arbitrary", "arbitrary"), vmem_limit_bytes=VMEM_LIMIT),
    )(order, dproj_t, h, mixed_t, dout, sw, vec)


def _adamw(w, g, m, v):
    nm = ADAM_B1 * m + (1.0 - ADAM_B1) * g
    nv = ADAM_B2 * v + (1.0 - ADAM_B2) * (g * g)
    m_hat = nm / (1.0 - ADAM_B1 ** ADAM_STEP)
    v_hat = nv / (1.0 - ADAM_B2 ** ADAM_STEP)
    return -ADAM_LR * (m_hat / (jnp.sqrt(v_hat) + ADAM_EPS) + ADAM_WD * w), nm, nv


def _adamw_shards_call(a, b, steps=4):
    def body(*refs):
        for k in range(2):
            w_ref, g_ref, m_ref, v_ref = refs[4 * k:4 * k + 4]
            go_ref, d_ref, nm_ref, nv_ref = refs[8 + 4 * k:12 + 4 * k]
            gg = g_ref[...]
            go_ref[...] = gg
            d_ref[...], nm_ref[...], nv_ref[...] = _adamw(w_ref[...], gg, m_ref[...], v_ref[...])

    specs, shapes = [], []
    for w in (a[0], b[0]):
        rows, cols = w.shape
        specs += [pl.BlockSpec((rows // steps, cols), lambda i: (i, 0))] * 4
        shapes += [jax.ShapeDtypeStruct((rows, cols), F32)] * 4
    return pl.pallas_call(
        body, name="adamw_shards", grid=(steps,), in_specs=specs, out_specs=specs, out_shape=shapes,
        compiler_params=pltpu.CompilerParams(dimension_semantics=("arbitrary",)),
    )(*a, *b)


_SMALL = (("norm_g", (1, D_MODEL), R_G1), ("b_in", (1, IN_W), R_BIN), ("attn_sinks", (1, 8), R_SINK),
          ("sgu_ln_g", (1, SGU_W), R_LNG), ("sgu_ln_b", (1, SGU_W), R_LNB), ("sgu_b", (N_SGU_HEADS, BLOCK), R_SGUB),
          ("b_out", (1, D_MODEL), R_BOUT), ("final_norm_g", (1, D_MODEL), R_G3))


def _adamw_small_call(sw_g, vec_g, sgu_w3, ws, ms, vs):
    n = len(_SMALL)

    def body(*refs):
        sw_ref, vec_ref = refs[0], refs[1]
        w3 = refs[2:5]
        w_refs, m_refs, v_refs = refs[5:5 + n], refs[5 + n:5 + 2 * n], refs[5 + 2 * n:5 + 3 * n]
        outs = refs[5 + 3 * n:]
        outs[0][...] = vec_ref[R_LOSS:R_LOSS + 1, 0:1]
        g = sw_ref[...]
        outs[1][...] = g
        outs[2][...], outs[3][...], outs[4][...] = _adamw(w3[0][...], g, w3[1][...], w3[2][...])
        for k, (_, shape, row) in enumerate(_SMALL):
            if shape[0] == 1 and shape[1] >= 128:
                g = jnp.concatenate([vec_ref[row + q:row + q + 1, :] for q in range(shape[1] // 128)], axis=1)
            else:
                g = vec_ref[row:row + shape[0], 0:shape[1]]
            o = outs[5 + 4 * k:9 + 4 * k]
            o[0][...] = g
            o[1][...], o[2][...], o[3][...] = _adamw(w_refs[k][...], g, m_refs[k][...], v_refs[k][...])

    vmem = pl.BlockSpec(memory_space=pltpu.VMEM)
    out_shape = [jax.ShapeDtypeStruct((1, 1), F32)] + [jax.ShapeDtypeStruct((SW_ROWS, 128), F32)] * 4
    for _, shape, _ in _SMALL:
        out_shape += [jax.ShapeDtypeStruct(shape, F32)] * 4
    args = [sw_g, vec_g, *sgu_w3, *ws, *ms, *vs]
    return pl.pallas_call(
        body, name="adamw_small", in_specs=[vmem] * len(args), out_specs=[vmem] * len(out_shape), out_shape=out_shape,
    )(*args)


def kernel(x, norm_g, w_in, b_in, attn_sinks, sgu_ln_g, sgu_ln_b, sgu_w, sgu_b, w_out, b_out, final_norm_g, loss_target, m_norm_g, m_w_in, m_b_in, m_attn_sinks, m_sgu_ln_g, m_sgu_ln_b, m_sgu_w, m_sgu_b, m_w_out, m_b_out, m_final_norm_g, v_norm_g, v_w_in, v_b_in, v_attn_sinks, v_sgu_ln_g, v_sgu_ln_b, v_sgu_w, v_sgu_b, v_w_out, v_b_out, v_final_norm_g):
    seq = x.shape[1]
    win_t, wout = _gather_call(w_in[0].T, w_out[0])
    bexp = jnp.repeat(sgu_b[0].T, SGU_W // N_SGU_HEADS, axis=1)
    dx, h, dproj_t, mixed_t, dout, sw, vec = _fused_call(
        x[0], loss_target[0], attn_sinks[0], norm_g, b_in, sgu_ln_g, sgu_ln_b, sgu_w[0], bexp, b_out,
        final_norm_g.reshape(1, D_MODEL), win_t, wout)
    ga_t, g_w_out, sw, vec = _wgrad_reduce_call(dproj_t, h, mixed_t, dout, sw, vec)

    names = ["norm_g", "w_in", "b_in", "attn_sinks", "sgu_ln_g", "sgu_ln_b", "sgu_w", "sgu_b", "w_out", "b_out", "final_norm_g"]
    res = {}
    shards = _adamw_shards_call((w_in[0].T, ga_t, m_w_in[0].T, v_w_in[0].T), (w_out[0], g_w_out, m_w_out[0], v_w_out[0]))
    res["w_in"] = [a.T[None] for a in shards[:4]]
    res["w_out"] = [a[None] for a in shards[4:]]
    given = dict(norm_g=(norm_g, m_norm_g, v_norm_g), b_in=(b_in, m_b_in, v_b_in), attn_sinks=(attn_sinks, m_attn_sinks, v_attn_sinks),
                 sgu_ln_g=(sgu_ln_g, m_sgu_ln_g, v_sgu_ln_g), sgu_ln_b=(sgu_ln_b, m_sgu_ln_b, v_sgu_ln_b),
                 sgu_b=(sgu_b, m_sgu_b, v_sgu_b), b_out=(b_out, m_b_out, v_b_out),
                 final_norm_g=(final_norm_g, m_final_norm_g, v_final_norm_g))
    wmv = [[given[n][k].reshape(shape) for n, shape, _ in _SMALL] for k in range(3)]
    outs = _adamw_small_call(sw, vec, [a.reshape(SW_ROWS, BLOCK) for a in (sgu_w, m_sgu_w, v_sgu_w)], *wmv)
    loss = outs[0].reshape(())
    res["sgu_w"] = [a.reshape(sgu_w.shape) for a in outs[1:5]]
    for k, (n, _, _) in enumerate(_SMALL):
        res[n] = [a.reshape(given[n][0].shape) for a in outs[5 + 4 * k:9 + 4 * k]]

    return (loss, dx[None], *[res[n][0] for n in names], *[res[n][1] for n in names], *[res[n][2] for n in names],
            *[res[n][3] for n in names])
```

```python
import functools
import math

import jax
import jax.numpy as jnp
from jax import lax
from jax.experimental import pallas as pl
from jax.experimental.pallas import tpu as pltpu

F32 = jnp.float32
MXU_DTYPE = jnp.bfloat16

D_MODEL = 1024
HEAD_DIM = 64
ATTN_W = 512
SGU_W = 512
N_SGU_HEADS = 8
BLOCK = 128
IN_W = 2816
OFF_K, OFF_V, OFF_ZA, OFF_US, OFF_VS, OFF_ZS = 512, 640, 768, 1280, 1792, 2304
NORM_EPS = 1e-5
NEG_INF = -1e30
SCALE = HEAD_DIM ** -0.5
SQRT_HALF = math.sqrt(0.5)
INV_SQRT_2PI = 1.0 / math.sqrt(2.0 * math.pi)

N_CHIPS = 4
N_DEV = 8
W_IN_SHARD = IN_W // N_CHIPS
W_OUT_SHARD = D_MODEL // N_CHIPS
HALF_A = W_IN_SHARD // 2
HALF_B = W_OUT_SHARD // 2

TILE = 256
VMEM_LIMIT = 56 * 1024 * 1024

ADAM_LR, ADAM_B1, ADAM_B2, ADAM_EPS, ADAM_WD, ADAM_STEP = 0.001, 0.9, 0.999, 1e-08, 0.01, 10

SW_ROWS = N_SGU_HEADS * BLOCK
R_G1, R_BIN, R_SINK, R_LOSS, R_LNG, R_LNB, R_SGUB, R_BOUT, R_G3 = 0, 8, 32, 40, 48, 56, 64, 72, 80
VEC_ROWS = 88

MESH = pl.DeviceIdType.MESH


def _mm(a, b):
    return jnp.dot(a, b, preferred_element_type=F32)


def _mm_nt(a, b):
    return lax.dot_general(a, b, (((1,), (1,)), ((), ())), preferred_element_type=F32)


def _mm_tn(a, b):
    return lax.dot_general(a, b, (((0,), (0,)), ((), ())), preferred_element_type=F32)


def _sigmoid(z):
    return 1.0 / (1.0 + jnp.exp(-z))


def _norm_cdf(z):
    return 0.5 * (1.0 + lax.erf(z * SQRT_HALF))


def _norm_pdf(z):
    return jnp.exp(-0.5 * z * z) * INV_SQRT_2PI


def _rows8(v):
    r, n = v.shape
    return jnp.sum(v.reshape(r // 8, 8, n), axis=0)


def _mean_rows(v):
    return jnp.sum(v, axis=1, keepdims=True) * (1.0 / v.shape[1])


def _fused_call(x, tgt, sinks, g1, b_in, ln_g, ln_b, sgu_w, bexp, b_out, g3, win_t, wout):
    seq = x.shape[0]
    t = TILE
    nt = seq // t
    nb = t // BLOCK
    act = MXU_DTYPE

    def body(sinks_ref, x_ref, xh_ref, tgt_ref, g1_ref, bin_ref, lng_ref, lnb_ref, sguw_ref, bexp_ref, bout_ref, g3_ref,
             wint_hbm, wout_hbm,
             dx_ref, h_ref, dprojt_ref, mixedt_ref, dout_ref, sw_ref, vec_ref,
             dproj_ref, mixed_ref, wint_v, wout_v, wf_v, wb_v, q_s, kf_s, vf_s, k2_s, v2_s, gate_s, p_s, ps_s, o_s, u_s, mix_s, vhat_s, r2_s,
             cdfu_s, cdfv_s, r1_s, doutf_s, dmix_s, dkf_s, dvf_s, carryk_s, carryv_s,
             acc_bin, acc_g1, acc_bout, acc_g3, acc_lng, acc_lnb, acc_dws, acc_dbs, acc_sink, acc_loss,
             h_s, rhs_s, vlnp_s, dvln_s, sems):
        i = pl.program_id(0)
        tile = nt - 1 - i
        lane128 = lax.broadcasted_iota(jnp.int32, (BLOCK, BLOCK), 1)
        lo = lane128 < HEAD_DIM

        @pl.when(i == 0)
        def _():
            cp_a = pltpu.make_async_copy(wint_hbm, wint_v, sems.at[0])
            cp_b = pltpu.make_async_copy(wout_hbm, wout_v, sems.at[1])
            cp_a.start()
            cp_b.start()
            for acc in (acc_bin, acc_g1, acc_bout, acc_g3, acc_lng, acc_lnb, acc_dws, acc_dbs, acc_sink, acc_loss,
                        carryk_s, carryv_s):
                acc[...] = jnp.zeros(acc.shape, F32)
            tril = lax.broadcasted_iota(jnp.int32, (BLOCK, BLOCK), 0) >= lane128
            for h in range(N_SGU_HEADS):
                w = jnp.where(tril, sguw_ref[h], 0.0)
                wf_v[h // 2, :, (h % 2) * BLOCK:(h % 2 + 1) * BLOCK] = w.astype(act)
                wb_v[h // 2, :, (h % 2) * BLOCK:(h % 2 + 1) * BLOCK] = w.T.astype(act)
            cp_a.wait()
            cp_b.wait()

        g1 = g1_ref[...]

        def rms(v):
            r = lax.rsqrt(_mean_rows(v * v) + NORM_EPS)
            return r, v * r

        _, xnh = rms(xh_ref[...])
        h_s[0:BLOCK, :] = (xnh * g1).astype(act)
        for c in range(nb):
            rows = slice(c * BLOCK, (c + 1) * BLOCK)
            r, xn = rms(x_ref[rows, :])
            r1_s[rows, :] = r
            h_s[BLOCK + c * BLOCK:BLOCK + (c + 1) * BLOCK, :] = (xn * g1).astype(act)

        h = h_s[BLOCK:, :]
        h_ref[...] = h
        q = _mm_nt(h, wint_v[0:OFF_K, :]) + bin_ref[:, 0:OFF_K]
        q_s[...] = (q * SCALE).astype(act)
        kv = _mm_nt(h_s[...], wint_v[OFF_K:OFF_ZA, :]) + bin_ref[:, OFF_K:OFF_ZA]
        kf_s[...] = kv[:, :BLOCK]
        vf_s[...] = kv[:, BLOCK:]
        for r in range(4):
            cols = slice(OFF_ZA + r * 512, OFF_ZA + (r + 1) * 512)
            gate_s[r] = _mm_nt(h, wint_v[cols, :]) + bin_ref[:, cols]

        lo_kv = lax.broadcasted_iota(jnp.int32, (t + BLOCK, BLOCK), 1) < HEAD_DIM
        for src, dst in ((kf_s, k2_s), (vf_s, v2_s)):
            v = src[...]
            vr = pltpu.roll(v, HEAD_DIM, 1)
            dst[0] = jnp.where(lo_kv, v, vr).astype(act)
            dst[1] = jnp.where(lo_kv, vr, v).astype(act)

        rowi = lax.broadcasted_iota(jnp.int32, (BLOCK, 2 * BLOCK), 0)
        colj = lax.broadcasted_iota(jnp.int32, (BLOCK, 2 * BLOCK), 1)
        in_band = (colj > rowi) & (colj <= rowi + BLOCK)
        row512 = lax.broadcasted_iota(jnp.int32, (4 * BLOCK, 1), 0)

        def stacked_q(b, g):
            parts = []
            for p in range(2):
                slab = q_s[b * BLOCK:(b + 1) * BLOCK, (2 * g + p) * BLOCK:(2 * g + p + 1) * BLOCK]
                parts += [jnp.where(lo, slab, jnp.zeros_like(slab)), jnp.where(lo, jnp.zeros_like(slab), slab)]
            return jnp.concatenate(parts, axis=0)

        def sink_col(g):
            s = [sinks_ref[4 * g + k] for k in range(4)]
            return jnp.where(row512 < BLOCK, s[0], jnp.where(row512 < 2 * BLOCK, s[1], jnp.where(row512 < 3 * BLOCK, s[2], s[3])))

        for b in range(nb):
            band = slice(b * BLOCK, (b + 2) * BLOCK)
            first_key = jnp.where(tile * nb + b > 0, 0, BLOCK)
            valid = in_band & (colj >= first_key)
            valid4 = jnp.concatenate([valid] * 4, axis=0)
            for g in range(2):
                s = _mm_nt(stacked_q(b, g), k2_s[g, band, :])
                s = jnp.where(valid4, s, NEG_INF)
                sk = sink_col(g)
                m = jnp.maximum(jnp.max(s, axis=1, keepdims=True), sk)
                p = jnp.exp(s - m)
                psk = jnp.exp(sk - m)
                inv = 1.0 / (jnp.sum(p, axis=1, keepdims=True) + psk)
                p = p * inv
                p_s[b * 2 + g] = p
                ps_s[b * 2 + g] = psk * inv
                o2 = _mm(p.astype(act), v2_s[g, band, :])
                for pr in range(2):
                    o_s[b * BLOCK:(b + 1) * BLOCK, (2 * g + pr) * BLOCK:(2 * g + pr + 1) * BLOCK] = jnp.where(
                        lo, o2[(2 * pr) * BLOCK:(2 * pr + 1) * BLOCK], o2[(2 * pr + 1) * BLOCK:(2 * pr + 2) * BLOCK])

        lng = lng_ref[...]
        lnb = lnb_ref[...]

        def split_pairs(val, c):
            for p in range(4):
                slab = val[:, p * BLOCK:(p + 1) * BLOCK]
                rhs_s[p, 0:BLOCK, c * BLOCK:(c + 1) * BLOCK] = jnp.where(lo, slab, 0.0).astype(act)
                rhs_s[p, BLOCK:, c * BLOCK:(c + 1) * BLOCK] = jnp.where(lo, 0.0, slab).astype(act)

        for c in range(nb):
            rows = slice(c * BLOCK, (c + 1) * BLOCK)
            za = gate_s[0, rows, :]
            mixed_ref[rows, 0:ATTN_W] = (o_s[rows, :] * (za * _sigmoid(za))).astype(act)
            us = gate_s[1, rows, :]
            vs = gate_s[2, rows, :]
            cu = _norm_cdf(us)
            cv = _norm_cdf(vs)
            cdfu_s[rows, :] = cu
            cdfv_s[rows, :] = cv
            u = us * cu
            vg = vs * cv
            vc = vg - _mean_rows(vg)
            r2 = lax.rsqrt(_mean_rows(vc * vc) + NORM_EPS)
            vhat = vc * r2
            r2_s[rows, :] = r2
            vhat_s[rows, :] = vhat
            u_s[rows, :] = u
            split_pairs(vhat * lng + lnb, c)
        for p in range(4):
            cols = slice(p * BLOCK, (p + 1) * BLOCK)
            mix = _mm(wf_v[p], rhs_s[p])
            for c in range(nb):
                mix_s[c * BLOCK:(c + 1) * BLOCK, cols] = mix[:, c * BLOCK:(c + 1) * BLOCK] + bexp_ref[:, cols]
        for c in range(nb):
            rows = slice(c * BLOCK, (c + 1) * BLOCK)
            zs = gate_s[3, rows, :]
            mixed_ref[rows, ATTN_W:] = (u_s[rows, :] * mix_s[rows, :] * (zs * _sigmoid(zs))).astype(act)

        g3 = g3_ref[...]
        proj_o = _mm(mixed_ref[...], wout_v[...])
        for c in range(nb):
            rows = slice(c * BLOCK, (c + 1) * BLOCK)
            out = x_ref[rows, :] + proj_o[rows, :] + bout_ref[...]
            r3, on = rms(out)
            e = on * g3 - tgt_ref[rows, :]
            e2 = _rows8(e * e)
            acc_loss[...] += sum(e2[:, k * 128:(k + 1) * 128] for k in range(D_MODEL // 128)) * (0.5 / D_MODEL)
            dy = e * (1.0 / D_MODEL)
            acc_g3[...] += _rows8(dy * on)
            don = dy * g3
            dout = r3 * (don - on * _mean_rows(don * on))
            doutf_s[rows, :] = dout
            dout_ref[rows, :] = dout.astype(act)
            acc_bout[...] += _rows8(dout)
        dmix_s[...] = _mm_nt(dout_ref[...], wout_v[...])

        for c in range(nb):
            rows = slice(c * BLOCK, (c + 1) * BLOCK)
            dso = dmix_s[rows, ATTN_W:]
            u = u_s[rows, :]
            mix = mix_s[rows, :]
            zs = gate_s[3, rows, :]
            sg = _sigmoid(zs)
            sgs = zs * sg
            du = dso * mix * sgs
            dmx = dso * u * sgs
            dzs = dso * u * mix * (sg * (1.0 + zs * (1.0 - sg)))
            us = gate_s[1, rows, :]
            dus = du * (cdfu_s[rows, :] + us * _norm_pdf(us))
            vln = (vhat_s[rows, :] * lng + lnb).astype(act)
            for p in range(4):
                vlnp_s[p, :, c * BLOCK:(c + 1) * BLOCK] = vln[:, p * BLOCK:(p + 1) * BLOCK]
            split_pairs(dmx, c)
            acc_dbs[...] += dmx
            for off, val in ((OFF_US, dus), (OFF_ZS, dzs)):
                dproj_ref[rows, off:off + 512] = val.astype(act)
                acc_bin[:, off:off + 512] += _rows8(val)
        for p in range(4):
            dvln = _mm(wb_v[p], rhs_s[p])
            for c in range(nb):
                dvln_s[c * BLOCK:(c + 1) * BLOCK, p * BLOCK:(p + 1) * BLOCK] = dvln[:, c * BLOCK:(c + 1) * BLOCK]
            acc_dws[(2 * p) * BLOCK:(2 * p + 1) * BLOCK, :] += _mm_nt(rhs_s[p, 0:BLOCK, :], vlnp_s[p])
            acc_dws[(2 * p + 1) * BLOCK:(2 * p + 2) * BLOCK, :] += _mm_nt(rhs_s[p, BLOCK:, :], vlnp_s[p])
        for c in range(nb):
            rows = slice(c * BLOCK, (c + 1) * BLOCK)
            dvln = dvln_s[rows, :]
            vhat = vhat_s[rows, :]
            acc_lng[...] += _rows8(dvln * vhat)
            acc_lnb[...] += _rows8(dvln)
            dvhat = dvln * lng
            dvg = r2_s[rows, :] * (dvhat - _mean_rows(dvhat) - vhat * _mean_rows(dvhat * vhat))
            vs = gate_s[2, rows, :]
            dvs = dvg * (cdfv_s[rows, :] + vs * _norm_pdf(vs))
            dproj_ref[rows, OFF_VS:OFF_VS + 512] = dvs.astype(act)
            acc_bin[:, OFF_VS:OFF_VS + 512] += _rows8(dvs)

        dkf_s[...] = jnp.zeros(dkf_s.shape, F32)
        dvf_s[...] = jnp.zeros(dvf_s.shape, F32)
        for b in range(nb):
            rows = slice(b * BLOCK, (b + 1) * BLOCK)
            band = slice(b * BLOCK, (b + 2) * BLOCK)
            za = gate_s[0, rows, :]
            sg = _sigmoid(za)
            dao = dmix_s[rows, 0:ATTN_W]
            o = o_s[rows, :]
            do = dao * (za * sg)
            dza = dao * o * (sg * (1.0 + za * (1.0 - sg)))
            dproj_ref[rows, OFF_ZA:OFF_US] = dza.astype(act)
            acc_bin[:, OFF_ZA:OFF_US] += _rows8(dza)
            for g in range(2):
                do_parts, delta_parts = [], []
                for pr in range(2):
                    cols = slice((2 * g + pr) * BLOCK, (2 * g + pr + 1) * BLOCK)
                    d_pair = do[:, cols]
                    prod = d_pair * o[:, cols]
                    do_parts += [jnp.where(lo, d_pair, 0.0).astype(act), jnp.where(lo, 0.0, d_pair).astype(act)]
                    delta_parts += [jnp.sum(jnp.where(lo, prod, 0.0), axis=1, keepdims=True),
                                    jnp.sum(jnp.where(lo, 0.0, prod), axis=1, keepdims=True)]
                do_st = jnp.concatenate(do_parts, axis=0)
                delta = jnp.concatenate(delta_parts, axis=0)
                p = p_s[b * 2 + g]
                dp = _mm_nt(do_st, v2_s[g, band, :])
                ds = p * (dp - delta)
                sink_t = ps_s[b * 2 + g] * delta
                for k in range(4):
                    acc_sink[4 * g + k:4 * g + k + 1, :] += -jnp.sum(sink_t[k * BLOCK:(k + 1) * BLOCK], axis=0, keepdims=True)
                ds_a = ds.astype(act)
                dq2 = _mm(ds_a, k2_s[g, band, :]) * SCALE
                for pr in range(2):
                    cols = slice((2 * g + pr) * BLOCK, (2 * g + pr + 1) * BLOCK)
                    dq = jnp.where(lo, dq2[(2 * pr) * BLOCK:(2 * pr + 1) * BLOCK], dq2[(2 * pr + 1) * BLOCK:(2 * pr + 2) * BLOCK])
                    dproj_ref[rows, cols] = dq.astype(act)
                    acc_bin[:, cols] += _rows8(dq)
                lo_band = lax.broadcasted_iota(jnp.int32, (2 * BLOCK, BLOCK), 1) < HEAD_DIM
                mine = lo_band if g == 0 else jnp.logical_not(lo_band)
                for acc, lhs, rhs in ((dkf_s, ds_a, stacked_q(b, g)), (dvf_s, p.astype(act), do_st)):
                    d2 = _mm_tn(lhs, rhs)
                    full = d2 + pltpu.roll(d2, HEAD_DIM, 1)
                    acc[band, :] += jnp.where(mine, full, 0.0)
        for acc, carry, off in ((dkf_s, carryk_s, OFF_K), (dvf_s, carryv_s, OFF_V)):
            acc[t:t + BLOCK, :] += carry[...]
            carry[...] = acc[0:BLOCK, :]
            d = acc[BLOCK:, :]
            dproj_ref[:, off:off + BLOCK] = d.astype(act)
            acc_bin[:, off:off + BLOCK] += _rows8(d)

        dh = _mm(dproj_ref[...], wint_v[...])
        dprojt_ref[...] = dproj_ref[...].T
        mixedt_ref[...] = mixed_ref[...].T
        for c in range(nb):
            rows = slice(c * BLOCK, (c + 1) * BLOCK)
            r1 = r1_s[rows, :]
            xn = x_ref[rows, :] * r1
            dhc = dh[rows, :]
            acc_g1[...] += _rows8(dhc * xn)
            dxn = dhc * g1
            dx_ref[rows, :] = doutf_s[rows, :] + r1 * (dxn - xn * _mean_rows(dxn * xn))

        @pl.when(i == nt - 1)
        def _():
            tril = lax.broadcasted_iota(jnp.int32, (BLOCK, BLOCK), 0) >= lane128
            for hh in range(N_SGU_HEADS):
                rws = slice(hh * BLOCK, (hh + 1) * BLOCK)
                sw_ref[rws, :] = jnp.where(tril, acc_dws[rws, :], 0.0)
            vec_ref[...] = jnp.zeros((VEC_ROWS, 128), F32)

            def put(row0, acc):
                s = jnp.sum(acc[...], axis=0, keepdims=True)
                for k in range(acc.shape[1] // 128):
                    vec_ref[row0 + k:row0 + k + 1, :] = s[:, k * 128:(k + 1) * 128]

            put(R_G1, acc_g1)
            put(R_BIN, acc_bin)
            put(R_LNG, acc_lng)
            put(R_LNB, acc_lnb)
            put(R_BOUT, acc_bout)
            put(R_G3, acc_g3)
            vec_ref[R_SINK:R_SINK + 1, :] = jnp.sum(
                jnp.where(lax.broadcasted_iota(jnp.int32, (8, 128), 0) == lax.broadcasted_iota(jnp.int32, (8, 128), 1),
                          acc_sink[...], 0.0), axis=0, keepdims=True)
            vec_ref[R_LOSS:R_LOSS + 1, :] = jnp.zeros((1, 128), F32) + jnp.sum(acc_loss[...])
            dbs_t = acc_dbs[...].T
            vec_ref[R_SGUB:R_SGUB + 8, :] = jnp.sum(dbs_t.reshape(N_SGU_HEADS, SGU_W // N_SGU_HEADS, BLOCK), axis=1)

    full = lambda shape: pl.BlockSpec(shape, lambda i: (0,) * len(shape))
    tok = lambda w: pl.BlockSpec((t, w), lambda i: (nt - 1 - i, 0))
    in_specs = [
        pl.BlockSpec(memory_space=pltpu.SMEM),
        tok(D_MODEL),
        pl.BlockSpec((BLOCK, D_MODEL), lambda i: (jnp.maximum((nt - 1 - i) * nb - 1, 0), 0)),
        tok(D_MODEL),
        full((1, D_MODEL)), full((1, IN_W)), full((1, SGU_W)), full((1, SGU_W)),
        full((N_SGU_HEADS, BLOCK, BLOCK)), full((BLOCK, SGU_W)), full((1, D_MODEL)), full((1, D_MODEL)),
        pl.BlockSpec(memory_space=pl.ANY), pl.BlockSpec(memory_space=pl.ANY),
    ]
    out_shape = [
        jax.ShapeDtypeStruct((seq, D_MODEL), F32),
        jax.ShapeDtypeStruct((seq, D_MODEL), act),
        jax.ShapeDtypeStruct((IN_W, seq), act),
        jax.ShapeDtypeStruct((D_MODEL, seq), act),
        jax.ShapeDtypeStruct((seq, D_MODEL), act),
        jax.ShapeDtypeStruct((SW_ROWS, 128), F32),
        jax.ShapeDtypeStruct((VEC_ROWS, 128), F32),
    ]
    tok_t = lambda w: pl.BlockSpec((w, t), lambda i: (0, nt - 1 - i))
    out_specs = [tok(D_MODEL), tok(D_MODEL), tok_t(IN_W), tok_t(D_MODEL), tok(D_MODEL), full((SW_ROWS, 128)), full((VEC_ROWS, 128))]
    vm = pltpu.VMEM
    scratch = [
        vm((t, IN_W), act), vm((t, D_MODEL), act),
        vm((IN_W, D_MODEL), act), vm((D_MODEL, D_MODEL), act),
        vm((4, BLOCK, 2 * BLOCK), act), vm((4, BLOCK, 2 * BLOCK), act),
        vm((t, ATTN_W), act),
        vm((t + BLOCK, BLOCK), F32), vm((t + BLOCK, BLOCK), F32),
        vm((2, t + BLOCK, BLOCK), act), vm((2, t + BLOCK, BLOCK), act),
        vm((4, t, 512), F32),
        vm((2 * nb, 4 * BLOCK, 2 * BLOCK), F32), vm((2 * nb, 4 * BLOCK, 1), F32),
        vm((t, ATTN_W), F32), vm((t, SGU_W), F32), vm((t, SGU_W), F32), vm((t, SGU_W), F32), vm((t, 1), F32),
        vm((t, SGU_W), F32), vm((t, SGU_W), F32), vm((t, 1), F32),
        vm((t, D_MODEL), F32), vm((t, D_MODEL), F32),
        vm((t + BLOCK, BLOCK), F32), vm((t + BLOCK, BLOCK), F32), vm((BLOCK, BLOCK), F32), vm((BLOCK, BLOCK), F32),
        vm((8, IN_W), F32), vm((8, D_MODEL), F32), vm((8, D_MODEL), F32), vm((8, D_MODEL), F32),
        vm((8, SGU_W), F32), vm((8, SGU_W), F32), vm((N_SGU_HEADS * BLOCK, BLOCK), F32), vm((BLOCK, SGU_W), F32),
        vm((8, 128), F32), vm((8, 128), F32),
        vm((t + BLOCK, D_MODEL), act), vm((4, 2 * BLOCK, t), act), vm((4, BLOCK, t), act), vm((t, SGU_W), F32),
        pltpu.SemaphoreType.DMA((2,)),
    ]
    return pl.pallas_call(
        body, name="fused", grid=(nt,), in_specs=in_specs, out_specs=out_specs, out_shape=out_shape,
        scratch_shapes=scratch,
        compiler_params=pltpu.CompilerParams(dimension_semantics=("arbitrary",), vmem_limit_bytes=VMEM_LIMIT),
    )(sinks, x, x, tgt, g1, b_in, ln_g, ln_b, sgu_w, bexp, b_out, g3, win_t, wout)


def _place():
    x, y, c = lax.axis_index("x"), lax.axis_index("y"), lax.axis_index("c")
    chips = [(1 - x, y), (x, 1 - y), (1 - x, 1 - y)]
    return x, y, c, chips


def _gather_call(a32, b32):
    act = MXU_DTYPE

    def body(a32_hbm, b32_hbm, ga_ref, gb_ref, a32_v, b32_v, a_ref, b_ref, send_sems, recv_sems, loc_sems):
        x, y, c, chips = _place()
        me, sibling = (x, y, c), (x, y, 1 - c)
        j = 2 * x + y
        loads = [pltpu.make_async_copy(a32_hbm, a32_v, loc_sems.at[0]), pltpu.make_async_copy(b32_hbm, b32_v, loc_sems.at[1])]
        for cp in loads:
            cp.start()
        for cp in loads:
            cp.wait()
        a_ref[...] = a32_v[...].astype(act)
        b_ref[...] = b32_v[...].astype(act)

        def half(which, cj, hf):
            ref, shard = (ga_ref, W_IN_SHARD) if which == 0 else (gb_ref, W_OUT_SHARD)
            n = shard // 2
            return ref.at[pl.ds(pl.multiple_of(cj * shard + hf * n, 16), n), :]

        def copy(k, src, dst, to):
            return pltpu.make_async_remote_copy(src_ref=src, dst_ref=dst, send_sem=send_sems.at[k], recv_sem=recv_sems.at[k],
                                                device_id=to, device_id_type=MESH)

        own = [pltpu.make_async_copy(a_ref, ga_ref.at[pl.ds(pl.multiple_of(j * W_IN_SHARD, 16), W_IN_SHARD), :], loc_sems.at[0]),
               pltpu.make_async_copy(b_ref, gb_ref.at[pl.ds(pl.multiple_of(j * W_OUT_SHARD, 16), W_OUT_SHARD), :], loc_sems.at[1])]
        for cp in own:
            cp.start()
        srcs = [a_ref.at[pl.ds(pl.multiple_of(c * HALF_A, 16), HALF_A), :], b_ref.at[pl.ds(pl.multiple_of(c * HALF_B, 16), HALF_B), :]]
        first = [copy(2 * r + w, srcs[w], half(w, j, c), (*chip, c)) for r, chip in enumerate(chips) for w in range(2)]
        for cp in first:
            cp.start()
        passed = []
        for r, (cx, cy) in enumerate(chips):
            cj = 2 * cx + cy
            for w in range(2):
                copy(2 * r + w, half(w, cj, c), half(w, cj, c), me).wait_recv()
                fwd = copy(6 + 2 * r + w, half(w, cj, c), half(w, cj, c), sibling)
                fwd.start()
                passed.append(fwd)
        for r, (cx, cy) in enumerate(chips):
            cj = 2 * cx + cy
            for w in range(2):
                copy(6 + 2 * r + w, half(w, cj, 1 - c), half(w, cj, 1 - c), me).wait_recv()
        for cp in first + passed:
            cp.wait_send()
        for cp in own:
            cp.wait()

    hbm = pl.BlockSpec(memory_space=pl.ANY)
    vm = pltpu.VMEM
    in_vmem = pl.BlockSpec(memory_space=vm)
    return pl.pallas_call(
        body, name="gather", in_specs=[hbm, hbm], out_specs=[in_vmem, in_vmem],
        out_shape=[jax.ShapeDtypeStruct((IN_W, D_MODEL), act), jax.ShapeDtypeStruct((D_MODEL, D_MODEL), act)],
        scratch_shapes=[vm(a32.shape, F32), vm(b32.shape, F32), vm(a32.shape, act), vm(b32.shape, act),
                        pltpu.SemaphoreType.DMA((12,)), pltpu.SemaphoreType.DMA((12,)), pltpu.SemaphoreType.DMA((2,))],
        compiler_params=pltpu.CompilerParams(vmem_limit_bytes=VMEM_LIMIT),
    )(a32, b32)


def _wgrad_reduce_call(dproj_t, h, mixed_t, dout, sw, vec):
    wire = jnp.bfloat16
    half_sw = SW_ROWS // 2
    seq = h.shape[0]
    tk = min(1024, seq)
    nk = seq // tk
    n_steps = 2 * N_CHIPS
    rel_of = lambda s: s % 3 if s < 6 else 3
    half_of = lambda s: s // 3 if s < 6 else s - 6
    x, y = lax.axis_index("x"), lax.axis_index("y")
    chip_of = [2 * (1 - x) + (1 - y), 2 * (1 - x) + y, 2 * x + (1 - y), 2 * x + y]
    order = jnp.stack([2 * chip_of[rel_of(s)] + half_of(s) for s in range(n_steps)]).astype(jnp.int32)

    def body(order_ref, dpt_ref, h_hbm, mxt_ref, dout_hbm, sw_ref, small_ref, oa_ref, ob_ref, osw_ref, osmall_ref,
             h_v, dout_v, acc_a, acc_b, sib_a, sib_b, snd_a, snd_b, in_a, in_b, own_a, own_b, fin_a, fin_b,
             all_small, sw_sib, sw_chips, sw_fin, send_sems, recv_sems, loc_sems):
        x, y, c = lax.axis_index("x"), lax.axis_index("y"), lax.axis_index("c")
        me, sibling = (x, y, c), (x, y, 1 - c)
        steps = [(1 - x, 1 - y), (1 - x, y), (x, 1 - y)]
        j = 2 * x + y
        dev = 4 * x + 2 * y + c
        b, k = pl.program_id(0), pl.program_id(1)

        def copy(n, src, dst, to):
            return pltpu.make_async_remote_copy(src_ref=src, dst_ref=dst, send_sem=send_sems.at[n], recv_sem=recv_sems.at[n],
                                                device_id=to, device_id_type=MESH)

        def sw_rows(ref, hf):
            return ref.at[pl.ds(pl.multiple_of(hf * half_sw, 8), half_sw), :]

        sm_first = [copy(16, small_ref, all_small.at[dev], sibling)]
        sm_first += [copy(17 + r, small_ref, all_small.at[dev], (*chip, c)) for r, chip in enumerate(steps)]
        sw_to_sib = copy(23, sw_rows(sw_ref, 1 - c), sw_sib, sibling)

        @pl.when((b == 0) & (k == 0))
        def _():
            all_small[dev] = small_ref[...]
            for cp in sm_first + [sw_to_sib]:
                cp.start()

        def load(kk):
            rows = pl.ds(kk * tk, tk)
            return [pltpu.make_async_copy(h_hbm.at[rows, :], h_v.at[rows, :], loc_sems.at[kk]),
                    pltpu.make_async_copy(dout_hbm.at[rows, :], dout_v.at[rows, :], loc_sems.at[nk + kk])]

        for kk in range(nk):
            @pl.when((b == 0) & (k == 0))
            def _():
                for cp in load(kk):
                    cp.start()

        for kk in range(nk):
            @pl.when((b == 0) & (k == kk))
            def _():
                for cp in load(kk):
                    cp.wait()

        tok = pl.ds(pl.multiple_of(k * tk, tk), tk)
        pa = _mm(dpt_ref[...], h_v[tok, :])
        pb = _mm(mxt_ref[...], dout_v[tok, :])
        slot = b % 2

        @pl.when(k == 0)
        def _():
            acc_a[slot] = pa
            acc_b[slot] = pb

        @pl.when(k != 0)
        def _():
            acc_a[slot] += pa
            acc_b[slot] += pb

        def to_sibling(s):
            r = rel_of(s)
            return [copy(r, acc_a.at[s % 2], sib_a.at[r], sibling), copy(4 + r, acc_b.at[s % 2], sib_b.at[r], sibling)]

        def chip_partial(s):
            r = rel_of(s)
            copy(r, sib_a.at[r], sib_a.at[r], me).wait_recv()
            copy(4 + r, sib_b.at[r], sib_b.at[r], me).wait_recv()
            return acc_a[s % 2] + sib_a[r], acc_b[s % 2] + sib_b[r]

        def to_owner(r):
            return [copy(8 + r, snd_a.at[r], in_a.at[r], (*steps[r], c)), copy(11 + r, snd_b.at[r], in_b.at[r], (*steps[r], c))]

        for s in range(n_steps):
            if s >= 1:
                sp = s - 1

                @pl.when((b == s) & (k == 0) & (c == half_of(sp)))
                def _():
                    ta, tb = chip_partial(sp)
                    r = rel_of(sp)
                    if r < 3:
                        snd_a[r] = ta.astype(wire)
                        snd_b[r] = tb.astype(wire)
                        for cp in to_owner(r):
                            cp.start()
                    else:
                        own_a[...] = ta
                        own_b[...] = tb

                @pl.when((b == s) & (k == nk - 1) & (c != half_of(sp)))
                def _():
                    for cp in to_sibling(sp):
                        cp.wait_send()

            @pl.when((b == s) & (k == nk - 1) & (c != half_of(s)))
            def _():
                for cp in to_sibling(s):
                    cp.start()

        @pl.when((b == n_steps - 1) & (k == nk - 1))
        def _():
            sm_passed = []
            for r, (cx, cy) in enumerate(steps):
                d = 4 * cx + 2 * cy + c
                copy(17 + r, all_small.at[d], all_small.at[d], me).wait_recv()
                fwd = copy(20 + r, all_small.at[d], all_small.at[d], sibling)
                fwd.start()
                sm_passed.append(fwd)
            copy(23, sw_sib, sw_sib, me).wait_recv()
            sw_chips[j] = (sw_rows(sw_ref, c)[...] + sw_sib[...]).astype(wire)
            sw_ici = [copy(24 + r, sw_chips.at[j], sw_chips.at[j], (*chip, c)) for r, chip in enumerate(steps)]
            for cp in sw_ici:
                cp.start()

            last = n_steps - 1

            @pl.when(c == half_of(last))
            def _():
                own_a[...], own_b[...] = chip_partial(last)

            @pl.when(c != half_of(last))
            def _():
                for cp in to_sibling(last):
                    cp.wait_send()

            tot_a = own_a[...]
            tot_b = own_b[...]
            for s in range(3):
                copy(8 + s, in_a.at[s], in_a.at[s], me).wait_recv()
                copy(11 + s, in_b.at[s], in_b.at[s], me).wait_recv()
                tot_a = tot_a + in_a[s].astype(F32)
                tot_b = tot_b + in_b[s].astype(F32)
            mine_a = oa_ref.at[pl.ds(pl.multiple_of(c * HALF_A, 8), HALF_A), :]
            mine_b = ob_ref.at[pl.ds(pl.multiple_of(c * HALF_B, 8), HALF_B), :]
            mine_a[...] = tot_a
            mine_b[...] = tot_b
            back = [copy(14, mine_a, fin_a, sibling), copy(15, mine_b, fin_b, sibling)]
            for cp in back:
                cp.start()

            copy(16, small_ref, all_small.at[dev ^ 1], me).wait_recv()
            for r, (cx, cy) in enumerate(steps):
                d = 4 * cx + 2 * cy + (1 - c)
                copy(20 + r, all_small.at[d], all_small.at[d], me).wait_recv()
            tot = all_small[0]
            for d in range(1, N_DEV):
                tot = tot + all_small[d]
            osmall_ref[...] = tot

            for r, (cx, cy) in enumerate(steps):
                cj = 2 * cx + cy
                copy(24 + r, sw_chips.at[cj], sw_chips.at[cj], me).wait_recv()
            tot_sw = sw_chips[0].astype(F32)
            for q in range(1, N_CHIPS):
                tot_sw = tot_sw + sw_chips[q].astype(F32)
            sw_rows(osw_ref, c)[...] = tot_sw
            sw_back = copy(27, sw_rows(osw_ref, c), sw_fin, sibling)
            sw_back.start()

            copy(14, fin_a, fin_a, me).wait_recv()
            copy(15, fin_b, fin_b, me).wait_recv()
            copy(27, sw_fin, sw_fin, me).wait_recv()
            oa_ref[pl.ds(pl.multiple_of((1 - c) * HALF_A, 8), HALF_A), :] = fin_a[...]
            ob_ref[pl.ds(pl.multiple_of((1 - c) * HALF_B, 8), HALF_B), :] = fin_b[...]
            sw_rows(osw_ref, 1 - c)[...] = sw_fin[...]
            sends = sm_first + sm_passed + sw_ici + back + [sw_to_sib, sw_back]
            for s in range(3):
                sends += to_owner(s)
            for cp in sends:
                cp.wait_send()

    vmem = pl.BlockSpec(memory_space=pltpu.VMEM)
    vm = pltpu.VMEM
    grid_spec = pltpu.PrefetchScalarGridSpec(
        num_scalar_prefetch=1, grid=(n_steps, nk),
        in_specs=[pl.BlockSpec((HALF_A, tk), lambda b, k, o: (o[b], k)), pl.BlockSpec(memory_space=pl.ANY),
                  pl.BlockSpec((HALF_B, tk), lambda b, k, o: (o[b], k)), pl.BlockSpec(memory_space=pl.ANY),
                  vmem, vmem],
        out_specs=[vmem, vmem, vmem, vmem],
        scratch_shapes=[vm((seq, D_MODEL), h.dtype), vm((seq, D_MODEL), dout.dtype),
                        vm((2, HALF_A, D_MODEL), F32), vm((2, HALF_B, D_MODEL), F32),
                        vm((N_CHIPS, HALF_A, D_MODEL), F32), vm((N_CHIPS, HALF_B, D_MODEL), F32),
                        vm((3, HALF_A, D_MODEL), wire), vm((3, HALF_B, D_MODEL), wire),
                        vm((3, HALF_A, D_MODEL), wire), vm((3, HALF_B, D_MODEL), wire),
                        vm((HALF_A, D_MODEL), F32), vm((HALF_B, D_MODEL), F32),
                        vm((HALF_A, D_MODEL), F32), vm((HALF_B, D_MODEL), F32),
                        vm((N_DEV, VEC_ROWS, 128), F32), vm((half_sw, 128), F32), vm((N_CHIPS, half_sw, 128), wire),
                        vm((half_sw, 128), F32),
                        pltpu.SemaphoreType.DMA((28,)), pltpu.SemaphoreType.DMA((28,)), pltpu.SemaphoreType.DMA((2 * nk,))])
    return pl.pallas_call(
        body, name="wgrad_reduce", grid_spec=grid_spec,
        out_shape=[jax.ShapeDtypeStruct((W_IN_SHARD, D_MODEL), F32), jax.ShapeDtypeStruct((W_OUT_SHARD, D_MODEL), F32),
                   jax.ShapeDtypeStruct((SW_ROWS, 128), F32), jax.ShapeDtypeStruct((VEC_ROWS, 128), F32)],
        compiler_params=pltpu.CompilerParams(dimension_semantics=("arbitrary", "arbitrary"), vmem_limit_bytes=VMEM_LIMIT),
    )(order, dproj_t, h, mixed_t, dout, sw, vec)


def _adamw(w, g, m, v):
    nm = ADAM_B1 * m + (1.0 - ADAM_B1) * g
    nv = ADAM_B2 * v + (1.0 - ADAM_B2) * (g * g)
    m_hat = nm / (1.0 - ADAM_B1 ** ADAM_STEP)
    v_hat = nv / (1.0 - ADAM_B2 ** ADAM_STEP)
    return -ADAM_LR * (m_hat / (jnp.sqrt(v_hat) + ADAM_EPS) + ADAM_WD * w), nm, nv


def _adamw_shards_call(a, b, steps=4):
    def body(*refs):
        for k in range(2):
            w_ref, g_ref, m_ref, v_ref = refs[4 * k:4 * k + 4]
            go_ref, d_ref, nm_ref, nv_ref = refs[8 + 4 * k:12 + 4 * k]
            gg = g_ref[...]
            go_ref[...] = gg
            d_ref[...], nm_ref[...], nv_ref[...] = _adamw(w_ref[...], gg, m_ref[...], v_ref[...])

    specs, shapes = [], []
    for w in (a[0], b[0]):
        rows, cols = w.shape
        specs += [pl.BlockSpec((rows // steps, cols), lambda i: (i, 0))] * 4
        shapes += [jax.ShapeDtypeStruct((rows, cols), F32)] * 4
    return pl.pallas_call(
        body, name="adamw_shards", grid=(steps,), in_specs=specs, out_specs=specs, out_shape=shapes,
        compiler_params=pltpu.CompilerParams(dimension_semantics=("arbitrary",)),
    )(*a, *b)


_SMALL = (("norm_g", (1, D_MODEL), R_G1), ("b_in", (1, IN_W), R_BIN), ("attn_sinks", (1, 8), R_SINK),
          ("sgu_ln_g", (1, SGU_W), R_LNG), ("sgu_ln_b", (1, SGU_W), R_LNB), ("sgu_b", (N_SGU_HEADS, BLOCK), R_SGUB),
          ("b_out", (1, D_MODEL), R_BOUT), ("final_norm_g", (1, D_MODEL), R_G3))


def _adamw_small_call(sw_g, vec_g, sgu_w3, ws, ms, vs):
    n = len(_SMALL)

    def body(*refs):
        sw_ref, vec_ref = refs[0], refs[1]
        w3 = refs[2:5]
        w_refs, m_refs, v_refs = refs[5:5 + n], refs[5 + n:5 + 2 * n], refs[5 + 2 * n:5 + 3 * n]
        outs = refs[5 + 3 * n:]
        outs[0][...] = vec_ref[R_LOSS:R_LOSS + 1, 0:1]
        g = sw_ref[...]
        outs[1][...] = g
        outs[2][...], outs[3][...], outs[4][...] = _adamw(w3[0][...], g, w3[1][...], w3[2][...])
        for k, (_, shape, row) in enumerate(_SMALL):
            if shape[0] == 1 and shape[1] >= 128:
                g = jnp.concatenate([vec_ref[row + q:row + q + 1, :] for q in range(shape[1] // 128)], axis=1)
            else:
                g = vec_ref[row:row + shape[0], 0:shape[1]]
            o = outs[5 + 4 * k:9 + 4 * k]
            o[0][...] = g
            o[1][...], o[2][...], o[3][...] = _adamw(w_refs[k][...], g, m_refs[k][...], v_refs[k][...])

    vmem = pl.BlockSpec(memory_space=pltpu.VMEM)
    out_shape = [jax.ShapeDtypeStruct((1, 1), F32)] + [jax.ShapeDtypeStruct((SW_ROWS, 128), F32)] * 4
    for _, shape, _ in _SMALL:
        out_shape += [jax.ShapeDtypeStruct(shape, F32)] * 4
    args = [sw_g, vec_g, *sgu_w3, *ws, *ms, *vs]
    return pl.pallas_call(
        body, name="adamw_small", in_specs=[vmem] * len(args), out_specs=[vmem] * len(out_shape), out_shape=out_shape,
    )(*args)


def kernel(x, norm_g, w_in, b_in, attn_sinks, sgu_ln_g, sgu_ln_b, sgu_w, sgu_b, w_out, b_out, final_norm_g, loss_target, m_norm_g, m_w_in, m_b_in, m_attn_sinks, m_sgu_ln_g, m_sgu_ln_b, m_sgu_w, m_sgu_b, m_w_out, m_b_out, m_final_norm_g, v_norm_g, v_w_in, v_b_in, v_attn_sinks, v_sgu_ln_g, v_sgu_ln_b, v_sgu_w, v_sgu_b, v_w_out, v_b_out, v_final_norm_g):
    seq = x.shape[1]
    win_t, wout = _gather_call(w_in[0].T, w_out[0])
    bexp = jnp.repeat(sgu_b[0].T, SGU_W // N_SGU_HEADS, axis=1)
    dx, h, dproj_t, mixed_t, dout, sw, vec = _fused_call(
        x[0], loss_target[0], attn_sinks[0], norm_g, b_in, sgu_ln_g, sgu_ln_b, sgu_w[0], bexp, b_out,
        final_norm_g.reshape(1, D_MODEL), win_t, wout)
    ga_t, g_w_out, sw, vec = _wgrad_reduce_call(dproj_t, h, mixed_t, dout, sw, vec)

    names = ["norm_g", "w_in", "b_in", "attn_sinks", "sgu_ln_g", "sgu_ln_b", "sgu_w", "sgu_b", "w_out", "b_out", "final_norm_g"]
    res = {}
    shards = _adamw_shards_call((w_in[0].T, ga_t, m_w_in[0].T, v_w_in[0].T), (w_out[0], g_w_out, m_w_out[0], v_w_out[0]))
    res["w_in"] = [a.T[None] for a in shards[:4]]
    res["w_out"] = [a[None] for a in shards[4:]]
    given = dict(norm_g=(norm_g, m_norm_g, v_norm_g), b_in=(b_in, m_b_in, v_b_in), attn_sinks=(attn_sinks, m_attn_sinks, v_attn_sinks),
                 sgu_ln_g=(sgu_ln_g, m_sgu_ln_g, v_sgu_ln_g), sgu_ln_b=(sgu_ln_b, m_sgu_ln_b, v_sgu_ln_b),
                 sgu_b=(sgu_b, m_sgu_b, v_sgu_b), b_out=(b_out, m_b_out, v_b_out),
                 final_norm_g=(final_norm_g, m_final_norm_g, v_final_norm_g))
    wmv = [[given[n][k].reshape(shape) for n, shape, _ in _SMALL] for k in range(3)]
    outs = _adamw_small_call(sw, vec, [a.reshape(SW_ROWS, BLOCK) for a in (sgu_w, m_sgu_w, v_sgu_w)], *wmv)
    loss = outs[0].reshape(())
    res["sgu_w"] = [a.reshape(sgu_w.shape) for a in outs[1:5]]
    for k, (n, _, _) in enumerate(_SMALL):
        res[n] = [a.reshape(given[n][0].shape) for a in outs[5 + 4 * k:9 + 4 * k]]

    return (loss, dx[None], *[res[n][0] for n in names], *[res[n][1] for n in names], *[res[n][2] for n in names],
            *[res[n][3] for n in names])
```

```python
import functools
import math

import jax
import jax.numpy as jnp
from jax import lax
from jax.experimental import pallas as pl
from jax.experimental.pallas import tpu as pltpu

F32 = jnp.float32
MXU_DTYPE = jnp.bfloat16

D_MODEL = 1024
HEAD_DIM = 64
ATTN_W = 512
SGU_W = 512
N_SGU_HEADS = 8
BLOCK = 128
IN_W = 2816
OFF_K, OFF_V, OFF_ZA, OFF_US, OFF_VS, OFF_ZS = 512, 640, 768, 1280, 1792, 2304
NORM_EPS = 1e-5
NEG_INF = -1e30
SCALE = HEAD_DIM ** -0.5
SQRT_HALF = math.sqrt(0.5)
INV_SQRT_2PI = 1.0 / math.sqrt(2.0 * math.pi)

N_CHIPS = 4
N_DEV = 8
W_IN_SHARD = IN_W // N_CHIPS
W_OUT_SHARD = D_MODEL // N_CHIPS
HALF_A = W_IN_SHARD // 2
HALF_B = W_OUT_SHARD // 2

TILE = 256
VMEM_LIMIT = 56 * 1024 * 1024

ADAM_LR, ADAM_B1, ADAM_B2, ADAM_EPS, ADAM_WD, ADAM_STEP = 0.001, 0.9, 0.999, 1e-08, 0.01, 10

SW_ROWS = N_SGU_HEADS * BLOCK
R_G1, R_BIN, R_SINK, R_LOSS, R_LNG, R_LNB, R_SGUB, R_BOUT, R_G3 = 0, 8, 32, 40, 48, 56, 64, 72, 80
VEC_ROWS = 88

MESH = pl.DeviceIdType.MESH


def _mm(a, b):
    return jnp.dot(a, b, preferred_element_type=F32)


def _mm_nt(a, b):
    return lax.dot_general(a, b, (((1,), (1,)), ((), ())), preferred_element_type=F32)


def _mm_tn(a, b):
    return lax.dot_general(a, b, (((0,), (0,)), ((), ())), preferred_element_type=F32)


def _sigmoid(z):
    return 1.0 / (1.0 + jnp.exp(-z))


def _norm_cdf(z):
    return 0.5 * (1.0 + lax.erf(z * SQRT_HALF))


def _norm_pdf(z):
    return jnp.exp(-0.5 * z * z) * INV_SQRT_2PI


def _rows8(v):
    r, n = v.shape
    return jnp.sum(v.reshape(r // 8, 8, n), axis=0)


def _mean_rows(v):
    return jnp.sum(v, axis=1, keepdims=True) * (1.0 / v.shape[1])


def _fused_call(x, tgt, sinks, g1, b_in, ln_g, ln_b, sgu_w, bexp, b_out, g3, win_t, wout):
    seq = x.shape[0]
    t = TILE
    nt = seq // t
    nb = t // BLOCK
    act = MXU_DTYPE

    def body(sinks_ref, x_ref, xh_ref, tgt_ref, g1_ref, bin_ref, lng_ref, lnb_ref, sguw_ref, bexp_ref, bout_ref, g3_ref,
             wint_hbm, wout_hbm,
             dx_ref, h_ref, dprojt_ref, mixedt_ref, dout_ref, sw_ref, vec_ref,
             dproj_ref, mixed_ref, wint_v, wout_v, wf_v, wb_v, q_s, kf_s, vf_s, k2_s, v2_s, gate_s, p_s, ps_s, o_s, u_s, mix_s, vhat_s, r2_s,
             cdfu_s, cdfv_s, r1_s, doutf_s, dmix_s, dkf_s, dvf_s, carryk_s, carryv_s,
             acc_bin, acc_g1, acc_bout, acc_g3, acc_lng, acc_lnb, acc_dws, acc_dbs, acc_sink, acc_loss,
             h_s, rhs_s, vlnp_s, dvln_s, sems):
        i = pl.program_id(0)
        tile = nt - 1 - i
        lane128 = lax.broadcasted_iota(jnp.int32, (BLOCK, BLOCK), 1)
        lo = lane128 < HEAD_DIM

        @pl.when(i == 0)
        def _():
            cp_a = pltpu.make_async_copy(wint_hbm, wint_v, sems.at[0])
            cp_b = pltpu.make_async_copy(wout_hbm, wout_v, sems.at[1])
            cp_a.start()
            cp_b.start()
            for acc in (acc_bin, acc_g1, acc_bout, acc_g3, acc_lng, acc_lnb, acc_dws, acc_dbs, acc_sink, acc_loss,
                        carryk_s, carryv_s):
                acc[...] = jnp.zeros(acc.shape, F32)
            tril = lax.broadcasted_iota(jnp.int32, (BLOCK, BLOCK), 0) >= lane128
            for h in range(N_SGU_HEADS):
                w = jnp.where(tril, sguw_ref[h], 0.0)
                wf_v[h // 2, :, (h % 2) * BLOCK:(h % 2 + 1) * BLOCK] = w.astype(act)
                wb_v[h // 2, :, (h % 2) * BLOCK:(h % 2 + 1) * BLOCK] = w.T.astype(act)
            cp_a.wait()
            cp_b.wait()

        g1 = g1_ref[...]

        def rms(v):
            r = lax.rsqrt(_mean_rows(v * v) + NORM_EPS)
            return r, v * r

        _, xnh = rms(xh_ref[...])
        h_s[0:BLOCK, :] = (xnh * g1).astype(act)
        for c in range(nb):
            rows = slice(c * BLOCK, (c + 1) * BLOCK)
            r, xn = rms(x_ref[rows, :])
            r1_s[rows, :] = r
            h_s[BLOCK + c * BLOCK:BLOCK + (c + 1) * BLOCK, :] = (xn * g1).astype(act)

        h = h_s[BLOCK:, :]
        h_ref[...] = h
        q = _mm_nt(h, wint_v[0:OFF_K, :]) + bin_ref[:, 0:OFF_K]
        q_s[...] = (q * SCALE).astype(act)
        kv = _mm_nt(h_s[...], wint_v[OFF_K:OFF_ZA, :]) + bin_ref[:, OFF_K:OFF_ZA]
        kf_s[...] = kv[:, :BLOCK]
        vf_s[...] = kv[:, BLOCK:]
        for r in range(4):
            cols = slice(OFF_ZA + r * 512, OFF_ZA + (r + 1) * 512)
            gate_s[r] = _mm_nt(h, wint_v[cols, :]) + bin_ref[:, cols]

        lo_kv = lax.broadcasted_iota(jnp.int32, (t + BLOCK, BLOCK), 1) < HEAD_DIM
        for src, dst in ((kf_s, k2_s), (vf_s, v2_s)):
            v = src[...]
            vr = pltpu.roll(v, HEAD_DIM, 1)
            dst[0] = jnp.where(lo_kv, v, vr).astype(act)
            dst[1] = jnp.where(lo_kv, vr, v).astype(act)

        rowi = lax.broadcasted_iota(jnp.int32, (BLOCK, 2 * BLOCK), 0)
        colj = lax.broadcasted_iota(jnp.int32, (BLOCK, 2 * BLOCK), 1)
        in_band = (colj > rowi) & (colj <= rowi + BLOCK)
        row512 = lax.broadcasted_iota(jnp.int32, (4 * BLOCK, 1), 0)

        def stacked_q(b, g):
            parts = []
            for p in range(2):
                slab = q_s[b * BLOCK:(b + 1) * BLOCK, (2 * g + p) * BLOCK:(2 * g + p + 1) * BLOCK]
                parts += [jnp.where(lo, slab, jnp.zeros_like(slab)), jnp.where(lo, jnp.zeros_like(slab), slab)]
            return jnp.concatenate(parts, axis=0)

        def sink_col(g):
            s = [sinks_ref[4 * g + k] for k in range(4)]
            return jnp.where(row512 < BLOCK, s[0], jnp.where(row512 < 2 * BLOCK, s[1], jnp.where(row512 < 3 * BLOCK, s[2], s[3])))

        for b in range(nb):
            band = slice(b * BLOCK, (b + 2) * BLOCK)
            first_key = jnp.where(tile * nb + b > 0, 0, BLOCK)
            valid = in_band & (colj >= first_key)
            valid4 = jnp.concatenate([valid] * 4, axis=0)
            for g in range(2):
                s = _mm_nt(stacked_q(b, g), k2_s[g, band, :])
                s = jnp.where(valid4, s, NEG_INF)
                sk = sink_col(g)
                m = jnp.maximum(jnp.max(s, axis=1, keepdims=True), sk)
                p = jnp.exp(s - m)
                psk = jnp.exp(sk - m)
                inv = 1.0 / (jnp.sum(p, axis=1, keepdims=True) + psk)
                p = p * inv
                p_s[b * 2 + g] = p
                ps_s[b * 2 + g] = psk * inv
                o2 = _mm(p.astype(act), v2_s[g, band, :])
                for pr in range(2):
                    o_s[b * BLOCK:(b + 1) * BLOCK, (2 * g + pr) * BLOCK:(2 * g + pr + 1) * BLOCK] = jnp.where(
                        lo, o2[(2 * pr) * BLOCK:(2 * pr + 1) * BLOCK], o2[(2 * pr + 1) * BLOCK:(2 * pr + 2) * BLOCK])

        lng = lng_ref[...]
        lnb = lnb_ref[...]

        def split_pairs(val, c):
            for p in range(4):
                slab = val[:, p * BLOCK:(p + 1) * BLOCK]
                rhs_s[p, 0:BLOCK, c * BLOCK:(c + 1) * BLOCK] = jnp.where(lo, slab, 0.0).astype(act)
                rhs_s[p, BLOCK:, c * BLOCK:(c + 1) * BLOCK] = jnp.where(lo, 0.0, slab).astype(act)

        for c in range(nb):
            rows = slice(c * BLOCK, (c + 1) * BLOCK)
            za = gate_s[0, rows, :]
            mixed_ref[rows, 0:ATTN_W] = (o_s[rows, :] * (za * _sigmoid(za))).astype(act)
            us = gate_s[1, rows, :]
            vs = gate_s[2, rows, :]
            cu = _norm_cdf(us)
            cv = _norm_cdf(vs)
            cdfu_s[rows, :] = cu
            cdfv_s[rows, :] = cv
            u = us * cu
            vg = vs * cv
            vc = vg - _mean_rows(vg)
            r2 = lax.rsqrt(_mean_rows(vc * vc) + NORM_EPS)
            vhat = vc * r2
            r2_s[rows, :] = r2
            vhat_s[rows, :] = vhat
            u_s[rows, :] = u
            split_pairs(vhat * lng + lnb, c)
        for p in range(4):
            cols = slice(p * BLOCK, (p + 1) * BLOCK)
            mix = _mm(wf_v[p], rhs_s[p])
            for c in range(nb):
                mix_s[c * BLOCK:(c + 1) * BLOCK, cols] = mix[:, c * BLOCK:(c + 1) * BLOCK] + bexp_ref[:, cols]
        for c in range(nb):
            rows = slice(c * BLOCK, (c + 1) * BLOCK)
            zs = gate_s[3, rows, :]
            mixed_ref[rows, ATTN_W:] = (u_s[rows, :] * mix_s[rows, :] * (zs * _sigmoid(zs))).astype(act)

        g3 = g3_ref[...]
        proj_o = _mm(mixed_ref[...], wout_v[...])
        for c in range(nb):
            rows = slice(c * BLOCK, (c + 1) * BLOCK)
            out = x_ref[rows, :] + proj_o[rows, :] + bout_ref[...]
            r3, on = rms(out)
            e = on * g3 - tgt_ref[rows, :]
            e2 = _rows8(e * e)
            acc_loss[...] += sum(e2[:, k * 128:(k + 1) * 128] for k in range(D_MODEL // 128)) * (0.5 / D_MODEL)
            dy = e * (1.0 / D_MODEL)
            acc_g3[...] += _rows8(dy * on)
            don = dy * g3
            dout = r3 * (don - on * _mean_rows(don * on))
            doutf_s[rows, :] = dout
            dout_ref[rows, :] = dout.astype(act)
            acc_bout[...] += _rows8(dout)
        dmix_s[...] = _mm_nt(dout_ref[...], wout_v[...])

        for c in range(nb):
            rows = slice(c * BLOCK, (c + 1) * BLOCK)
            dso = dmix_s[rows, ATTN_W:]
            u = u_s[rows, :]
            mix = mix_s[rows, :]
            zs = gate_s[3, rows, :]
            sg = _sigmoid(zs)
            sgs = zs * sg
            du = dso * mix * sgs
            dmx = dso * u * sgs
            dzs = dso * u * mix * (sg * (1.0 + zs * (1.0 - sg)))
            us = gate_s[1, rows, :]
            dus = du * (cdfu_s[rows, :] + us * _norm_pdf(us))
            vln = (vhat_s[rows, :] * lng + lnb).astype(act)
            for p in range(4):
                vlnp_s[p, :, c * BLOCK:(c + 1) * BLOCK] = vln[:, p * BLOCK:(p + 1) * BLOCK]
            split_pairs(dmx, c)
            acc_dbs[...] += dmx
            for off, val in ((OFF_US, dus), (OFF_ZS, dzs)):
                dproj_ref[rows, off:off + 512] = val.astype(act)
                acc_bin[:, off:off + 512] += _rows8(val)
        for p in range(4):
            dvln = _mm(wb_v[p], rhs_s[p])
            for c in range(nb):
                dvln_s[c * BLOCK:(c + 1) * BLOCK, p * BLOCK:(p + 1) * BLOCK] = dvln[:, c * BLOCK:(c + 1) * BLOCK]
            acc_dws[(2 * p) * BLOCK:(2 * p + 1) * BLOCK, :] += _mm_nt(rhs_s[p, 0:BLOCK, :], vlnp_s[p])
            acc_dws[(2 * p + 1) * BLOCK:(2 * p + 2) * BLOCK, :] += _mm_nt(rhs_s[p, BLOCK:, :], vlnp_s[p])
        for c in range(nb):
            rows = slice(c * BLOCK, (c + 1) * BLOCK)
            dvln = dvln_s[rows, :]
            vhat = vhat_s[rows, :]
            acc_lng[...] += _rows8(dvln * vhat)
            acc_lnb[...] += _rows8(dvln)
            dvhat = dvln * lng
            dvg = r2_s[rows, :] * (dvhat - _mean_rows(dvhat) - vhat * _mean_rows(dvhat * vhat))
            vs = gate_s[2, rows, :]
            dvs = dvg * (cdfv_s[rows, :] + vs * _norm_pdf(vs))
            dproj_ref[rows, OFF_VS:OFF_VS + 512] = dvs.astype(act)
            acc_bin[:, OFF_VS:OFF_VS + 512] += _rows8(dvs)

        dkf_s[...] = jnp.zeros(dkf_s.shape, F32)
        dvf_s[...] = jnp.zeros(dvf_s.shape, F32)
        for b in range(nb):
            rows = slice(b * BLOCK, (b + 1) * BLOCK)
            band = slice(b * BLOCK, (b + 2) * BLOCK)
            za = gate_s[0, rows, :]
            sg = _sigmoid(za)
            dao = dmix_s[rows, 0:ATTN_W]
            o = o_s[rows, :]
            do = dao * (za * sg)
            dza = dao * o * (sg * (1.0 + za * (1.0 - sg)))
            dproj_ref[rows, OFF_ZA:OFF_US] = dza.astype(act)
            acc_bin[:, OFF_ZA:OFF_US] += _rows8(dza)
            for g in range(2):
                do_parts, delta_parts = [], []
                for pr in range(2):
                    cols = slice((2 * g + pr) * BLOCK, (2 * g + pr + 1) * BLOCK)
                    d_pair = do[:, cols]
                    prod = d_pair * o[:, cols]
                    do_parts += [jnp.where(lo, d_pair, 0.0).astype(act), jnp.where(lo, 0.0, d_pair).astype(act)]
                    delta_parts += [jnp.sum(jnp.where(lo, prod, 0.0), axis=1, keepdims=True),
                                    jnp.sum(jnp.where(lo, 0.0, prod), axis=1, keepdims=True)]
                do_st = jnp.concatenate(do_parts, axis=0)
                delta = jnp.concatenate(delta_parts, axis=0)
                p = p_s[b * 2 + g]
                dp = _mm_nt(do_st, v2_s[g, band, :])
                ds = p * (dp - delta)
                sink_t = ps_s[b * 2 + g] * delta
                for k in range(4):
                    acc_sink[4 * g + k:4 * g + k + 1, :] += -jnp.sum(sink_t[k * BLOCK:(k + 1) * BLOCK], axis=0, keepdims=True)
                ds_a = ds.astype(act)
                dq2 = _mm(ds_a, k2_s[g, band, :]) * SCALE
                for pr in range(2):
                    cols = slice((2 * g + pr) * BLOCK, (2 * g + pr + 1) * BLOCK)
                    dq = jnp.where(lo, dq2[(2 * pr) * BLOCK:(2 * pr + 1) * BLOCK], dq2[(2 * pr + 1) * BLOCK:(2 * pr + 2) * BLOCK])
                    dproj_ref[rows, cols] = dq.astype(act)
                    acc_bin[:, cols] += _rows8(dq)
                lo_band = lax.broadcasted_iota(jnp.int32, (2 * BLOCK, BLOCK), 1) < HEAD_DIM
                mine = lo_band if g == 0 else jnp.logical_not(lo_band)
                for acc, lhs, rhs in ((dkf_s, ds_a, stacked_q(b, g)), (dvf_s, p.astype(act), do_st)):
                    d2 = _mm_tn(lhs, rhs)
                    full = d2 + pltpu.roll(d2, HEAD_DIM, 1)
                    acc[band, :] += jnp.where(mine, full, 0.0)
        for acc, carry, off in ((dkf_s, carryk_s, OFF_K), (dvf_s, carryv_s, OFF_V)):
            acc[t:t + BLOCK, :] += carry[...]
            carry[...] = acc[0:BLOCK, :]
            d = acc[BLOCK:, :]
            dproj_ref[:, off:off + BLOCK] = d.astype(act)
            acc_bin[:, off:off + BLOCK] += _rows8(d)

        dh = _mm(dproj_ref[...], wint_v[...])
        dprojt_ref[...] = dproj_ref[...].T
        mixedt_ref[...] = mixed_ref[...].T
        for c in range(nb):
            rows = slice(c * BLOCK, (c + 1) * BLOCK)
            r1 = r1_s[rows, :]
            xn = x_ref[rows, :] * r1
            dhc = dh[rows, :]
            acc_g1[...] += _rows8(dhc * xn)
            dxn = dhc * g1
            dx_ref[rows, :] = doutf_s[rows, :] + r1 * (dxn - xn * _mean_rows(dxn * xn))

        @pl.when(i == nt - 1)
        def _():
            tril = lax.broadcasted_iota(jnp.int32, (BLOCK, BLOCK), 0) >= lane128
            for hh in range(N_SGU_HEADS):
                rws = slice(hh * BLOCK, (hh + 1) * BLOCK)
                sw_ref[rws, :] = jnp.where(tril, acc_dws[rws, :], 0.0)
            vec_ref[...] = jnp.zeros((VEC_ROWS, 128), F32)

            def put(row0, acc):
                s = jnp.sum(acc[...], axis=0, keepdims=True)
                for k in range(acc.shape[1] // 128):
                    vec_ref[row0 + k:row0 + k + 1, :] = s[:, k * 128:(k + 1) * 128]

            put(R_G1, acc_g1)
            put(R_BIN, acc_bin)
            put(R_LNG, acc_lng)
            put(R_LNB, acc_lnb)
            put(R_BOUT, acc_bout)
            put(R_G3, acc_g3)
            vec_ref[R_SINK:R_SINK + 1, :] = jnp.sum(
                jnp.where(lax.broadcasted_iota(jnp.int32, (8, 128), 0) == lax.broadcasted_iota(jnp.int32, (8, 128), 1),
                          acc_sink[...], 0.0), axis=0, keepdims=True)
            vec_ref[R_LOSS:R_LOSS + 1, :] = jnp.zeros((1, 128), F32) + jnp.sum(acc_loss[...])
            dbs_t = acc_dbs[...].T
            vec_ref[R_SGUB:R_SGUB + 8, :] = jnp.sum(dbs_t.reshape(N_SGU_HEADS, SGU_W // N_SGU_HEADS, BLOCK), axis=1)

    full = lambda shape: pl.BlockSpec(shape, lambda i: (0,) * len(shape))
    tok = lambda w: pl.BlockSpec((t, w), lambda i: (nt - 1 - i, 0))
    in_specs = [
        pl.BlockSpec(memory_space=pltpu.SMEM),
        tok(D_MODEL),
        pl.BlockSpec((BLOCK, D_MODEL), lambda i: (jnp.maximum((nt - 1 - i) * nb - 1, 0), 0)),
        tok(D_MODEL),
        full((1, D_MODEL)), full((1, IN_W)), full((1, SGU_W)), full((1, SGU_W)),
        full((N_SGU_HEADS, BLOCK, BLOCK)), full((BLOCK, SGU_W)), full((1, D_MODEL)), full((1, D_MODEL)),
        pl.BlockSpec(memory_space=pl.ANY), pl.BlockSpec(memory_space=pl.ANY),
    ]
    out_shape = [
        jax.ShapeDtypeStruct((seq, D_MODEL), F32),
        jax.ShapeDtypeStruct((seq, D_MODEL), act),
        jax.ShapeDtypeStruct((IN_W, seq), act),
        jax.ShapeDtypeStruct((D_MODEL, seq), act),
        jax.ShapeDtypeStruct((seq, D_MODEL), act),
        jax.ShapeDtypeStruct((SW_ROWS, 128), F32),
        jax.ShapeDtypeStruct((VEC_ROWS, 128), F32),
    ]
    tok_t = lambda w: pl.BlockSpec((w, t), lambda i: (0, nt - 1 - i))
    out_specs = [tok(D_MODEL), tok(D_MODEL), tok_t(IN_W), tok_t(D_MODEL), tok(D_MODEL), full((SW_ROWS, 128)), full((VEC_ROWS, 128))]
    vm = pltpu.VMEM
    scratch = [
        vm((t, IN_W), act), vm((t, D_MODEL), act),
        vm((IN_W, D_MODEL), act), vm((D_MODEL, D_MODEL), act),
        vm((4, BLOCK, 2 * BLOCK), act), vm((4, BLOCK, 2 * BLOCK), act),
        vm((t, ATTN_W), act),
        vm((t + BLOCK, BLOCK), F32), vm((t + BLOCK, BLOCK), F32),
        vm((2, t + BLOCK, BLOCK), act), vm((2, t + BLOCK, BLOCK), act),
        vm((4, t, 512), F32),
        vm((2 * nb, 4 * BLOCK, 2 * BLOCK), F32), vm((2 * nb, 4 * BLOCK, 1), F32),
        vm((t, ATTN_W), F32), vm((t, SGU_W), F32), vm((t, SGU_W), F32), vm((t, SGU_W), F32), vm((t, 1), F32),
        vm((t, SGU_W), F32), vm((t, SGU_W), F32), vm((t, 1), F32),
        vm((t, D_MODEL), F32), vm((t, D_MODEL), F32),
        vm((t + BLOCK, BLOCK), F32), vm((t + BLOCK, BLOCK), F32), vm((BLOCK, BLOCK), F32), vm((BLOCK, BLOCK), F32),
        vm((8, IN_W), F32), vm((8, D_MODEL), F32), vm((8, D_MODEL), F32), vm((8, D_MODEL), F32),
        vm((8, SGU_W), F32), vm((8, SGU_W), F32), vm((N_SGU_HEADS * BLOCK, BLOCK), F32), vm((BLOCK, SGU_W), F32),
        vm((8, 128), F32), vm((8, 128), F32),
        vm((t + BLOCK, D_MODEL), act), vm((4, 2 * BLOCK, t), act), vm((4, BLOCK, t), act), vm((t, SGU_W), F32),
        pltpu.SemaphoreType.DMA((2,)),
    ]
    return pl.pallas_call(
        body, name="fused", grid=(nt,), in_specs=in_specs, out_specs=out_specs, out_shape=out_shape,
        scratch_shapes=scratch,
        compiler_params=pltpu.CompilerParams(dimension_semantics=("arbitrary",), vmem_limit_bytes=VMEM_LIMIT),
    )(sinks, x, x, tgt, g1, b_in, ln_g, ln_b, sgu_w, bexp, b_out, g3, win_t, wout)


def _place():
    x, y, c = lax.axis_index("x"), lax.axis_index("y"), lax.axis_index("c")
    chips = [(1 - x, y), (x, 1 - y), (1 - x, 1 - y)]
    return x, y, c, chips


def _gather_call(a32, b32):
    act = MXU_DTYPE

    def body(a32_hbm, b32_hbm, ga_ref, gb_ref, a32_v, b32_v, a_ref, b_ref, send_sems, recv_sems, loc_sems):
        x, y, c, chips = _place()
        me, sibling = (x, y, c), (x, y, 1 - c)
        j = 2 * x + y
        loads = [pltpu.make_async_copy(a32_hbm, a32_v, loc_sems.at[0]), pltpu.make_async_copy(b32_hbm, b32_v, loc_sems.at[1])]
        for cp in loads:
            cp.start()
        for cp in loads:
            cp.wait()
        a_ref[...] = a32_v[...].astype(act)
        b_ref[...] = b32_v[...].astype(act)

        def half(which, cj, hf):
            ref, shard = (ga_ref, W_IN_SHARD) if which == 0 else (gb_ref, W_OUT_SHARD)
            n = shard // 2
            return ref.at[pl.ds(pl.multiple_of(cj * shard + hf * n, 16), n), :]

        def copy(k, src, dst, to):
            return pltpu.make_async_remote_copy(src_ref=src, dst_ref=dst, send_sem=send_sems.at[k], recv_sem=recv_sems.at[k],
                                                device_id=to, device_id_type=MESH)

        own = [pltpu.make_async_copy(a_ref, ga_ref.at[pl.ds(pl.multiple_of(j * W_IN_SHARD, 16), W_IN_SHARD), :], loc_sems.at[0]),
               pltpu.make_async_copy(b_ref, gb_ref.at[pl.ds(pl.multiple_of(j * W_OUT_SHARD, 16), W_OUT_SHARD), :], loc_sems.at[1])]
        for cp in own:
            cp.start()
        srcs = [a_ref.at[pl.ds(pl.multiple_of(c * HALF_A, 16), HALF_A), :], b_ref.at[pl.ds(pl.multiple_of(c * HALF_B, 16), HALF_B), :]]
        first = [copy(2 * r + w, srcs[w], half(w, j, c), (*chip, c)) for r, chip in enumerate(chips) for w in range(2)]
        for cp in first:
            cp.start()
        passed = []
        for r, (cx, cy) in enumerate(chips):
            cj = 2 * cx + cy
            for w in range(2):
                copy(2 * r + w, half(w, cj, c), half(w, cj, c), me).wait_recv()
                fwd = copy(6 + 2 * r + w, half(w, cj, c), half(w, cj, c), sibling)
                fwd.start()
                passed.append(fwd)
        for r, (cx, cy) in enumerate(chips):
            cj = 2 * cx + cy
            for w in range(2):
                copy(6 + 2 * r + w, half(w, cj, 1 - c), half(w, cj, 1 - c), me).wait_recv()
        for cp in first + passed:
            cp.wait_send()
        for cp in own:
            cp.wait()

    hbm = pl.BlockSpec(memory_space=pl.ANY)
    vm = pltpu.VMEM
    return pl.pallas_call(
        body, name="gather", in_specs=[hbm, hbm], out_specs=[hbm, hbm],
        out_shape=[jax.ShapeDtypeStruct((IN_W, D_MODEL), act), jax.ShapeDtypeStruct((D_MODEL, D_MODEL), act)],
        scratch_shapes=[vm(a32.shape, F32), vm(b32.shape, F32), vm(a32.shape, act), vm(b32.shape, act),
                        pltpu.SemaphoreType.DMA((12,)), pltpu.SemaphoreType.DMA((12,)), pltpu.SemaphoreType.DMA((2,))],
        compiler_params=pltpu.CompilerParams(vmem_limit_bytes=VMEM_LIMIT),
    )(a32, b32)


def _wgrad_reduce_call(dproj_t, h, mixed_t, dout, sw, vec):
    wire = jnp.bfloat16
    half_sw = SW_ROWS // 2
    seq = h.shape[0]
    tk = min(1024, seq)
    nk = seq // tk
    n_steps = 2 * N_CHIPS
    rel_of = lambda s: s % 3 if s < 6 else 3
    half_of = lambda s: s // 3 if s < 6 else s - 6
    x, y = lax.axis_index("x"), lax.axis_index("y")
    chip_of = [2 * (1 - x) + (1 - y), 2 * (1 - x) + y, 2 * x + (1 - y), 2 * x + y]
    order = jnp.stack([2 * chip_of[rel_of(s)] + half_of(s) for s in range(n_steps)]).astype(jnp.int32)

    def body(order_ref, dpt_ref, h_hbm, mxt_ref, dout_hbm, sw_ref, small_ref, oa_ref, ob_ref, osw_ref, osmall_ref,
             h_v, dout_v, acc_a, acc_b, sib_a, sib_b, snd_a, snd_b, in_a, in_b, own_a, own_b, fin_a, fin_b,
             all_small, sw_sib, sw_chips, sw_fin, send_sems, recv_sems, loc_sems):
        x, y, c = lax.axis_index("x"), lax.axis_index("y"), lax.axis_index("c")
        me, sibling = (x, y, c), (x, y, 1 - c)
        steps = [(1 - x, 1 - y), (1 - x, y), (x, 1 - y)]
        j = 2 * x + y
        dev = 4 * x + 2 * y + c
        b, k = pl.program_id(0), pl.program_id(1)

        def copy(n, src, dst, to):
            return pltpu.make_async_remote_copy(src_ref=src, dst_ref=dst, send_sem=send_sems.at[n], recv_sem=recv_sems.at[n],
                                                device_id=to, device_id_type=MESH)

        def sw_rows(ref, hf):
            return ref.at[pl.ds(pl.multiple_of(hf * half_sw, 8), half_sw), :]

        sm_first = [copy(16, small_ref, all_small.at[dev], sibling)]
        sm_first += [copy(17 + r, small_ref, all_small.at[dev], (*chip, c)) for r, chip in enumerate(steps)]
        sw_to_sib = copy(23, sw_rows(sw_ref, 1 - c), sw_sib, sibling)

        @pl.when((b == 0) & (k == 0))
        def _():
            all_small[dev] = small_ref[...]
            for cp in sm_first + [sw_to_sib]:
                cp.start()

        def load(kk):
            rows = pl.ds(kk * tk, tk)
            return [pltpu.make_async_copy(h_hbm.at[rows, :], h_v.at[rows, :], loc_sems.at[kk]),
                    pltpu.make_async_copy(dout_hbm.at[rows, :], dout_v.at[rows, :], loc_sems.at[nk + kk])]

        for kk in range(nk):
            @pl.when((b == 0) & (k == 0))
            def _():
                for cp in load(kk):
                    cp.start()

        for kk in range(nk):
            @pl.when((b == 0) & (k == kk))
            def _():
                for cp in load(kk):
                    cp.wait()

        tok = pl.ds(pl.multiple_of(k * tk, tk), tk)
        pa = _mm(dpt_ref[...], h_v[tok, :])
        pb = _mm(mxt_ref[...], dout_v[tok, :])
        slot = b % 2

        @pl.when(k == 0)
        def _():
            acc_a[slot] = pa
            acc_b[slot] = pb

        @pl.when(k != 0)
        def _():
            acc_a[slot] += pa
            acc_b[slot] += pb

        def to_sibling(s):
            r = rel_of(s)
            return [copy(r, acc_a.at[s % 2], sib_a.at[r], sibling), copy(4 + r, acc_b.at[s % 2], sib_b.at[r], sibling)]

        def chip_partial(s):
            r = rel_of(s)
            copy(r, sib_a.at[r], sib_a.at[r], me).wait_recv()
            copy(4 + r, sib_b.at[r], sib_b.at[r], me).wait_recv()
            return acc_a[s % 2] + sib_a[r], acc_b[s % 2] + sib_b[r]

        def to_owner(r):
            return [copy(8 + r, snd_a.at[r], in_a.at[r], (*steps[r], c)), copy(11 + r, snd_b.at[r], in_b.at[r], (*steps[r], c))]

        def vec_forwards():
            return [copy(20 + r, all_small.at[4 * cx + 2 * cy + c], all_small.at[4 * cx + 2 * cy + c], sibling)
                    for r, (cx, cy) in enumerate(steps)]

        def sw_to_chips():
            return [copy(24 + r, sw_chips.at[j], sw_chips.at[j], (*chip, c)) for r, chip in enumerate(steps)]

        def sw_to_sibling():
            return copy(27, sw_rows(osw_ref, c), sw_fin, sibling)

        @pl.when((b == 1) & (k == 0))
        def _():
            for r, (cx, cy) in enumerate(steps):
                d = 4 * cx + 2 * cy + c
                copy(17 + r, all_small.at[d], all_small.at[d], me).wait_recv()
            for cp in vec_forwards():
                cp.start()
            copy(23, sw_sib, sw_sib, me).wait_recv()
            sw_chips[j] = (sw_rows(sw_ref, c)[...] + sw_sib[...]).astype(wire)
            for cp in sw_to_chips():
                cp.start()

        @pl.when((b == 3) & (k == 0))
        def _():
            for r, (cx, cy) in enumerate(steps):
                cj = 2 * cx + cy
                copy(24 + r, sw_chips.at[cj], sw_chips.at[cj], me).wait_recv()
            tot_sw = sw_chips[0].astype(F32)
            for q in range(1, N_CHIPS):
                tot_sw = tot_sw + sw_chips[q].astype(F32)
            sw_rows(osw_ref, c)[...] = tot_sw
            sw_to_sibling().start()

        late = min(1, nk - 1)
        for s in range(n_steps):
            if s >= 1:
                sp = s - 1

                @pl.when((b == s) & (k == late) & (c == half_of(sp)))
                def _():
                    ta, tb = chip_partial(sp)
                    r = rel_of(sp)
                    if r < 3:
                        snd_a[r] = ta.astype(wire)
                        snd_b[r] = tb.astype(wire)
                        for cp in to_owner(r):
                            cp.start()
                    else:
                        own_a[...] = ta
                        own_b[...] = tb

                @pl.when((b == s) & (k == nk - 1) & (c != half_of(sp)))
                def _():
                    for cp in to_sibling(sp):
                        cp.wait_send()

            @pl.when((b == s) & (k == nk - 1) & (c != half_of(s)))
            def _():
                for cp in to_sibling(s):
                    cp.start()

        @pl.when((b == n_steps - 1) & (k == nk - 1))
        def _():
            last = n_steps - 1

            @pl.when(c == half_of(last))
            def _():
                own_a[...], own_b[...] = chip_partial(last)

            @pl.when(c != half_of(last))
            def _():
                for cp in to_sibling(last):
                    cp.wait_send()

            tot_a = own_a[...]
            tot_b = own_b[...]
            for s in range(3):
                copy(8 + s, in_a.at[s], in_a.at[s], me).wait_recv()
                copy(11 + s, in_b.at[s], in_b.at[s], me).wait_recv()
                tot_a = tot_a + in_a[s].astype(F32)
                tot_b = tot_b + in_b[s].astype(F32)
            mine_a = oa_ref.at[pl.ds(pl.multiple_of(c * HALF_A, 8), HALF_A), :]
            mine_b = ob_ref.at[pl.ds(pl.multiple_of(c * HALF_B, 8), HALF_B), :]
            mine_a[...] = tot_a
            mine_b[...] = tot_b
            back = [copy(14, mine_a, fin_a, sibling), copy(15, mine_b, fin_b, sibling)]
            for cp in back:
                cp.start()

            copy(16, small_ref, all_small.at[dev ^ 1], me).wait_recv()
            for r, (cx, cy) in enumerate(steps):
                d = 4 * cx + 2 * cy + (1 - c)
                copy(20 + r, all_small.at[d], all_small.at[d], me).wait_recv()
            tot = all_small[0]
            for d in range(1, N_DEV):
                tot = tot + all_small[d]
            osmall_ref[...] = tot

            copy(14, fin_a, fin_a, me).wait_recv()
            copy(15, fin_b, fin_b, me).wait_recv()
            copy(27, sw_fin, sw_fin, me).wait_recv()
            oa_ref[pl.ds(pl.multiple_of((1 - c) * HALF_A, 8), HALF_A), :] = fin_a[...]
            ob_ref[pl.ds(pl.multiple_of((1 - c) * HALF_B, 8), HALF_B), :] = fin_b[...]
            sw_rows(osw_ref, 1 - c)[...] = sw_fin[...]
            sends = sm_first + vec_forwards() + sw_to_chips() + back + [sw_to_sib, sw_to_sibling()]
            for s in range(3):
                sends += to_owner(s)
            for cp in sends:
                cp.wait_send()

    vmem = pl.BlockSpec(memory_space=pltpu.VMEM)
    vm = pltpu.VMEM
    grid_spec = pltpu.PrefetchScalarGridSpec(
        num_scalar_prefetch=1, grid=(n_steps, nk),
        in_specs=[pl.BlockSpec((HALF_A, tk), lambda b, k, o: (o[b], k)), pl.BlockSpec(memory_space=pl.ANY),
                  pl.BlockSpec((HALF_B, tk), lambda b, k, o: (o[b], k)), pl.BlockSpec(memory_space=pl.ANY),
                  vmem, vmem],
        out_specs=[vmem, vmem, vmem, vmem],
        scratch_shapes=[vm((seq, D_MODEL), h.dtype), vm((seq, D_MODEL), dout.dtype),
                        vm((2, HALF_A, D_MODEL), F32), vm((2, HALF_B, D_MODEL), F32),
                        vm((N_CHIPS, HALF_A, D_MODEL), F32), vm((N_CHIPS, HALF_B, D_MODEL), F32),
                        vm((3, HALF_A, D_MODEL), wire), vm((3, HALF_B, D_MODEL), wire),
                        vm((3, HALF_A, D_MODEL), wire), vm((3, HALF_B, D_MODEL), wire),
                        vm((HALF_A, D_MODEL), F32), vm((HALF_B, D_MODEL), F32),
                        vm((HALF_A, D_MODEL), F32), vm((HALF_B, D_MODEL), F32),
                        vm((N_DEV, VEC_ROWS, 128), F32), vm((half_sw, 128), F32), vm((N_CHIPS, half_sw, 128), wire),
                        vm((half_sw, 128), F32),
                        pltpu.SemaphoreType.DMA((28,)), pltpu.SemaphoreType.DMA((28,)), pltpu.SemaphoreType.DMA((2 * nk,))])
    return pl.pallas_call(
        body, name="wgrad_reduce", grid_spec=grid_spec,
        out_shape=[jax.ShapeDtypeStruct((W_IN_SHARD, D_MODEL), F32), jax.ShapeDtypeStruct((W_OUT_SHARD, D_MODEL), F32),
                   jax.ShapeDtypeStruct((SW_ROWS, 128), F32), jax.ShapeDtypeStruct((VEC_ROWS, 128), F32)],
        compiler_params=pltpu.CompilerParams(dimension_semantics=("arbitrary", "arbitrary"), vmem_limit_bytes=VMEM_LIMIT),
    )(order, dproj_t, h, mixed_t, dout, sw, vec)


def _adamw(w, g, m, v):
    nm = ADAM_B1 * m + (1.0 - ADAM_B1) * g
    nv = ADAM_B2 * v + (1.0 - ADAM_B2) * (g * g)
    m_hat = nm / (1.0 - ADAM_B1 ** ADAM_STEP)
    v_hat = nv / (1.0 - ADAM_B2 ** ADAM_STEP)
    return -ADAM_LR * (m_hat / (jnp.sqrt(v_hat) + ADAM_EPS) + ADAM_WD * w), nm, nv


def _adamw_shards_call(a, b, steps=4):
    def body(*refs):
        for k in range(2):
            w_ref, g_ref, m_ref, v_ref = refs[4 * k:4 * k + 4]
            go_ref, d_ref, nm_ref, nv_ref = refs[8 + 4 * k:12 + 4 * k]
            gg = g_ref[...]
            go_ref[...] = gg
            d_ref[...], nm_ref[...], nv_ref[...] = _adamw(w_ref[...], gg, m_ref[...], v_ref[...])

    specs, shapes = [], []
    for w in (a[0], b[0]):
        rows, cols = w.shape
        specs += [pl.BlockSpec((rows // steps, cols), lambda i: (i, 0))] * 4
        shapes += [jax.ShapeDtypeStruct((rows, cols), F32)] * 4
    return pl.pallas_call(
        body, name="adamw_shards", grid=(steps,), in_specs=specs, out_specs=specs, out_shape=shapes,
        compiler_params=pltpu.CompilerParams(dimension_semantics=("arbitrary",)),
    )(*a, *b)


_SMALL = (("norm_g", (1, D_MODEL), R_G1), ("b_in", (1, IN_W), R_BIN), ("attn_sinks", (1, 8), R_SINK),
          ("sgu_ln_g", (1, SGU_W), R_LNG), ("sgu_ln_b", (1, SGU_W), R_LNB), ("sgu_b", (N_SGU_HEADS, BLOCK), R_SGUB),
          ("b_out", (1, D_MODEL), R_BOUT), ("final_norm_g", (1, D_MODEL), R_G3))


def _adamw_small_call(sw_g, vec_g, sgu_w3, ws, ms, vs):
    n = len(_SMALL)

    def body(*refs):
        sw_ref, vec_ref = refs[0], refs[1]
        w3 = refs[2:5]
        w_refs, m_refs, v_refs = refs[5:5 + n], refs[5 + n:5 + 2 * n], refs[5 + 2 * n:5 + 3 * n]
        outs = refs[5 + 3 * n:]
        outs[0][...] = vec_ref[R_LOSS:R_LOSS + 1, 0:1]
        g = sw_ref[...]
        outs[1][...] = g
        outs[2][...], outs[3][...], outs[4][...] = _adamw(w3[0][...], g, w3[1][...], w3[2][...])
        for k, (_, shape, row) in enumerate(_SMALL):
            if shape[0] == 1 and shape[1] >= 128:
                g = jnp.concatenate([vec_ref[row + q:row + q + 1, :] for q in range(shape[1] // 128)], axis=1)
            else:
                g = vec_ref[row:row + shape[0], 0:shape[1]]
            o = outs[5 + 4 * k:9 + 4 * k]
            o[0][...] = g
            o[1][...], o[2][...], o[3][...] = _adamw(w_refs[k][...], g, m_refs[k][...], v_refs[k][...])

    vmem = pl.BlockSpec(memory_space=pltpu.VMEM)
    out_shape = [jax.ShapeDtypeStruct((1, 1), F32)] + [jax.ShapeDtypeStruct((SW_ROWS, 128), F32)] * 4
    for _, shape, _ in _SMALL:
        out_shape += [jax.ShapeDtypeStruct(shape, F32)] * 4
    args = [sw_g, vec_g, *sgu_w3, *ws, *ms, *vs]
    return pl.pallas_call(
        body, name="adamw_small", in_specs=[vmem] * len(args), out_specs=[vmem] * len(out_shape), out_shape=out_shape,
    )(*args)


def kernel(x, norm_g, w_in, b_in, attn_sinks, sgu_ln_g, sgu_ln_b, sgu_w, sgu_b, w_out, b_out, final_norm_g, loss_target, m_norm_g, m_w_in, m_b_in, m_attn_sinks, m_sgu_ln_g, m_sgu_ln_b, m_sgu_w, m_sgu_b, m_w_out, m_b_out, m_final_norm_g, v_norm_g, v_w_in, v_b_in, v_attn_sinks, v_sgu_ln_g, v_sgu_ln_b, v_sgu_w, v_sgu_b, v_w_out, v_b_out, v_final_norm_g):
    seq = x.shape[1]
    win_t, wout = _gather_call(w_in[0].T, w_out[0])
    bexp = jnp.repeat(sgu_b[0].T, SGU_W // N_SGU_HEADS, axis=1)
    dx, h, dproj_t, mixed_t, dout, sw, vec = _fused_call(
        x[0], loss_target[0], attn_sinks[0], norm_g, b_in, sgu_ln_g, sgu_ln_b, sgu_w[0], bexp, b_out,
        final_norm_g.reshape(1, D_MODEL), win_t, wout)
    ga_t, g_w_out, sw, vec = _wgrad_reduce_call(dproj_t, h, mixed_t, dout, sw, vec)

    names = ["norm_g", "w_in", "b_in", "attn_sinks", "sgu_ln_g", "sgu_ln_b", "sgu_w", "sgu_b", "w_out", "b_out", "final_norm_g"]
    res = {}
    shards = _adamw_shards_call((w_in[0].T, ga_t, m_w_in[0].T, v_w_in[0].T), (w_out[0], g_w_out, m_w_out[0], v_w_out[0]))
    res["w_in"] = [a.T[None] for a in shards[:4]]
    res["w_out"] = [a[None] for a in shards[4:]]
    given = dict(norm_g=(norm_g, m_norm_g, v_norm_g), b_in=(b_in, m_b_in, v_b_in), attn_sinks=(attn_sinks, m_attn_sinks, v_attn_sinks),
                 sgu_ln_g=(sgu_ln_g, m_sgu_ln_g, v_sgu_ln_g), sgu_ln_b=(sgu_ln_b, m_sgu_ln_b, v_sgu_ln_b),
                 sgu_b=(sgu_b, m_sgu_b, v_sgu_b), b_out=(b_out, m_b_out, v_b_out),
                 final_norm_g=(final_norm_g, m_final_norm_g, v_final_norm_g))
    wmv = [[given[n][k].reshape(shape) for n, shape, _ in _SMALL] for k in range(3)]
    outs = _adamw_small_call(sw, vec, [a.reshape(SW_ROWS, BLOCK) for a in (sgu_w, m_sgu_w, v_sgu_w)], *wmv)
    loss = outs[0].reshape(())
    res["sgu_w"] = [a.reshape(sgu_w.shape) for a in outs[1:5]]
    for k, (n, _, _) in enumerate(_SMALL):
        res[n] = [a.reshape(given[n][0].shape) for a in outs[5 + 4 * k:9 + 4 * k]]

    return (loss, dx[None], *[res[n][0] for n in names], *[res[n][1] for n in names], *[res[n][2] for n in names],
            *[res[n][3] for n in names])
```

```python
import functools
import math

import jax
import jax.numpy as jnp
from jax import lax
from jax.experimental import pallas as pl
from jax.experimental.pallas import tpu as pltpu

F32 = jnp.float32
MXU_DTYPE = jnp.bfloat16

D_MODEL = 1024
HEAD_DIM = 64
ATTN_W = 512
SGU_W = 512
N_SGU_HEADS = 8
BLOCK = 128
IN_W = 2816
OFF_K, OFF_V, OFF_ZA, OFF_US, OFF_VS, OFF_ZS = 512, 640, 768, 1280, 1792, 2304
NORM_EPS = 1e-5
NEG_INF = -1e30
SCALE = HEAD_DIM ** -0.5
SQRT_HALF = math.sqrt(0.5)
INV_SQRT_2PI = 1.0 / math.sqrt(2.0 * math.pi)

N_CHIPS = 4
N_DEV = 8
W_IN_SHARD = IN_W // N_CHIPS
W_OUT_SHARD = D_MODEL // N_CHIPS
HALF_A = W_IN_SHARD // 2
HALF_B = W_OUT_SHARD // 2

TILE = 256
VMEM_LIMIT = 56 * 1024 * 1024

ADAM_LR, ADAM_B1, ADAM_B2, ADAM_EPS, ADAM_WD, ADAM_STEP = 0.001, 0.9, 0.999, 1e-08, 0.01, 10

SW_ROWS = N_SGU_HEADS * BLOCK
R_G1, R_BIN, R_SINK, R_LOSS, R_LNG, R_LNB, R_SGUB, R_BOUT, R_G3 = 0, 8, 32, 40, 48, 56, 64, 72, 80
VEC_ROWS = 88

MESH = pl.DeviceIdType.MESH


def _mm(a, b):
    return jnp.dot(a, b, preferred_element_type=F32)


def _mm_nt(a, b):
    return lax.dot_general(a, b, (((1,), (1,)), ((), ())), preferred_element_type=F32)


def _mm_tn(a, b):
    return lax.dot_general(a, b, (((0,), (0,)), ((), ())), preferred_element_type=F32)


def _sigmoid(z):
    return 1.0 / (1.0 + jnp.exp(-z))


def _norm_cdf(z):
    return 0.5 * (1.0 + lax.erf(z * SQRT_HALF))


def _norm_pdf(z):
    return jnp.exp(-0.5 * z * z) * INV_SQRT_2PI


def _rows8(v):
    r, n = v.shape
    return jnp.sum(v.reshape(r // 8, 8, n), axis=0)


def _mean_rows(v):
    return jnp.sum(v, axis=1, keepdims=True) * (1.0 / v.shape[1])


def _fused_call(x, tgt, sinks, g1, b_in, ln_g, ln_b, sgu_w, bexp, b_out, g3, win_t, wout):
    seq = x.shape[0]
    t = TILE
    nt = seq // t
    nb = t // BLOCK
    act = MXU_DTYPE

    def body(sinks_ref, x_ref, xh_ref, tgt_ref, g1_ref, bin_ref, lng_ref, lnb_ref, sguw_ref, bexp_ref, bout_ref, g3_ref,
             wint_hbm, wout_hbm,
             dx_ref, h_ref, dprojt_ref, mixedt_ref, dout_ref, sw_ref, vec_ref,
             dproj_ref, mixed_ref, wint_v, wout_v, wf_v, wb_v, q_s, kf_s, vf_s, k2_s, v2_s, gate_s, p_s, ps_s, o_s, u_s, mix_s, vhat_s, r2_s,
             cdfu_s, cdfv_s, r1_s, doutf_s, dmix_s, dkf_s, dvf_s, carryk_s, carryv_s,
             acc_bin, acc_g1, acc_bout, acc_g3, acc_lng, acc_lnb, acc_dws, acc_dbs, acc_sink, acc_loss,
             h_s, rhs_s, vlnp_s, dvln_s, sems):
        i = pl.program_id(0)
        tile = nt - 1 - i
        lane128 = lax.broadcasted_iota(jnp.int32, (BLOCK, BLOCK), 1)
        lo = lane128 < HEAD_DIM

        @pl.when(i == 0)
        def _():
            cp_a = pltpu.make_async_copy(wint_hbm, wint_v, sems.at[0])
            cp_b = pltpu.make_async_copy(wout_hbm, wout_v, sems.at[1])
            cp_a.start()
            cp_b.start()
            for acc in (acc_bin, acc_g1, acc_bout, acc_g3, acc_lng, acc_lnb, acc_dws, acc_dbs, acc_sink, acc_loss,
                        carryk_s, carryv_s):
                acc[...] = jnp.zeros(acc.shape, F32)
            tril = lax.broadcasted_iota(jnp.int32, (BLOCK, BLOCK), 0) >= lane128
            for h in range(N_SGU_HEADS):
                w = jnp.where(tril, sguw_ref[h], 0.0)
                wf_v[h // 2, :, (h % 2) * BLOCK:(h % 2 + 1) * BLOCK] = w.astype(act)
                wb_v[h // 2, :, (h % 2) * BLOCK:(h % 2 + 1) * BLOCK] = w.T.astype(act)
            cp_a.wait()
            cp_b.wait()

        g1 = g1_ref[...]

        def rms(v):
            r = lax.rsqrt(_mean_rows(v * v) + NORM_EPS)
            return r, v * r

        _, xnh = rms(xh_ref[...])
        h_s[0:BLOCK, :] = (xnh * g1).astype(act)
        for c in range(nb):
            rows = slice(c * BLOCK, (c + 1) * BLOCK)
            r, xn = rms(x_ref[rows, :])
            r1_s[rows, :] = r
            h_s[BLOCK + c * BLOCK:BLOCK + (c + 1) * BLOCK, :] = (xn * g1).astype(act)

        h = h_s[BLOCK:, :]
        h_ref[...] = h
        q = _mm_nt(h, wint_v[0:OFF_K, :]) + bin_ref[:, 0:OFF_K]
        q_s[...] = (q * SCALE).astype(act)
        kv = _mm_nt(h_s[...], wint_v[OFF_K:OFF_ZA, :]) + bin_ref[:, OFF_K:OFF_ZA]
        kf_s[...] = kv[:, :BLOCK]
        vf_s[...] = kv[:, BLOCK:]
        for r in range(4):
            cols = slice(OFF_ZA + r * 512, OFF_ZA + (r + 1) * 512)
            gate_s[r] = _mm_nt(h, wint_v[cols, :]) + bin_ref[:, cols]

        lo_kv = lax.broadcasted_iota(jnp.int32, (t + BLOCK, BLOCK), 1) < HEAD_DIM
        for src, dst in ((kf_s, k2_s), (vf_s, v2_s)):
            v = src[...]
            vr = pltpu.roll(v, HEAD_DIM, 1)
            dst[0] = jnp.where(lo_kv, v, vr).astype(act)
            dst[1] = jnp.where(lo_kv, vr, v).astype(act)

        rowi = lax.broadcasted_iota(jnp.int32, (BLOCK, 2 * BLOCK), 0)
        colj = lax.broadcasted_iota(jnp.int32, (BLOCK, 2 * BLOCK), 1)
        in_band = (colj > rowi) & (colj <= rowi + BLOCK)
        row512 = lax.broadcasted_iota(jnp.int32, (4 * BLOCK, 1), 0)

        def stacked_q(b, g):
            parts = []
            for p in range(2):
                slab = q_s[b * BLOCK:(b + 1) * BLOCK, (2 * g + p) * BLOCK:(2 * g + p + 1) * BLOCK]
                parts += [jnp.where(lo, slab, jnp.zeros_like(slab)), jnp.where(lo, jnp.zeros_like(slab), slab)]
            return jnp.concatenate(parts, axis=0)

        def sink_col(g):
            s = [sinks_ref[4 * g + k] for k in range(4)]
            return jnp.where(row512 < BLOCK, s[0], jnp.where(row512 < 2 * BLOCK, s[1], jnp.where(row512 < 3 * BLOCK, s[2], s[3])))

        for b in range(nb):
            band = slice(b * BLOCK, (b + 2) * BLOCK)
            first_key = jnp.where(tile * nb + b > 0, 0, BLOCK)
            valid = in_band & (colj >= first_key)
            valid4 = jnp.concatenate([valid] * 4, axis=0)
            for g in range(2):
                s = _mm_nt(stacked_q(b, g), k2_s[g, band, :])
                s = jnp.where(valid4, s, NEG_INF)
                sk = sink_col(g)
                m = jnp.maximum(jnp.max(s, axis=1, keepdims=True), sk)
                p = jnp.exp(s - m)
                psk = jnp.exp(sk - m)
                inv = 1.0 / (jnp.sum(p, axis=1, keepdims=True) + psk)
                p = p * inv
                p_s[b * 2 + g] = p
                ps_s[b * 2 + g] = psk * inv
                o2 = _mm(p.astype(act), v2_s[g, band, :])
                for pr in range(2):
                    o_s[b * BLOCK:(b + 1) * BLOCK, (2 * g + pr) * BLOCK:(2 * g + pr + 1) * BLOCK] = jnp.where(
                        lo, o2[(2 * pr) * BLOCK:(2 * pr + 1) * BLOCK], o2[(2 * pr + 1) * BLOCK:(2 * pr + 2) * BLOCK])

        lng = lng_ref[...]
        lnb = lnb_ref[...]

        def split_pairs(val, c):
            for p in range(4):
                slab = val[:, p * BLOCK:(p + 1) * BLOCK]
                rhs_s[p, 0:BLOCK, c * BLOCK:(c + 1) * BLOCK] = jnp.where(lo, slab, 0.0).astype(act)
                rhs_s[p, BLOCK:, c * BLOCK:(c + 1) * BLOCK] = jnp.where(lo, 0.0, slab).astype(act)

        for c in range(nb):
            rows = slice(c * BLOCK, (c + 1) * BLOCK)
            za = gate_s[0, rows, :]
            mixed_ref[rows, 0:ATTN_W] = (o_s[rows, :] * (za * _sigmoid(za))).astype(act)
            us = gate_s[1, rows, :]
            vs = gate_s[2, rows, :]
            cu = _norm_cdf(us)
            cv = _norm_cdf(vs)
            cdfu_s[rows, :] = cu
            cdfv_s[rows, :] = cv
            u = us * cu
            vg = vs * cv
            vc = vg - _mean_rows(vg)
            r2 = lax.rsqrt(_mean_rows(vc * vc) + NORM_EPS)
            vhat = vc * r2
            r2_s[rows, :] = r2
            vhat_s[rows, :] = vhat
            u_s[rows, :] = u
            split_pairs(vhat * lng + lnb, c)
        for p in range(4):
            cols = slice(p * BLOCK, (p + 1) * BLOCK)
            mix = _mm(wf_v[p], rhs_s[p])
            for c in range(nb):
                mix_s[c * BLOCK:(c + 1) * BLOCK, cols] = mix[:, c * BLOCK:(c + 1) * BLOCK] + bexp_ref[:, cols]
        for c in range(nb):
            rows = slice(c * BLOCK, (c + 1) * BLOCK)
            zs = gate_s[3, rows, :]
            mixed_ref[rows, ATTN_W:] = (u_s[rows, :] * mix_s[rows, :] * (zs * _sigmoid(zs))).astype(act)

        g3 = g3_ref[...]
        proj_o = _mm(mixed_ref[...], wout_v[...])
        for c in range(nb):
            rows = slice(c * BLOCK, (c + 1) * BLOCK)
            out = x_ref[rows, :] + proj_o[rows, :] + bout_ref[...]
            r3, on = rms(out)
            e = on * g3 - tgt_ref[rows, :]
            e2 = _rows8(e * e)
            acc_loss[...] += sum(e2[:, k * 128:(k + 1) * 128] for k in range(D_MODEL // 128)) * (0.5 / D_MODEL)
            dy = e * (1.0 / D_MODEL)
            acc_g3[...] += _rows8(dy * on)
            don = dy * g3
            dout = r3 * (don - on * _mean_rows(don * on))
            doutf_s[rows, :] = dout
            dout_ref[rows, :] = dout.astype(act)
            acc_bout[...] += _rows8(dout)
        dmix_s[...] = _mm_nt(dout_ref[...], wout_v[...])

        for c in range(nb):
            rows = slice(c * BLOCK, (c + 1) * BLOCK)
            dso = dmix_s[rows, ATTN_W:]
            u = u_s[rows, :]
            mix = mix_s[rows, :]
            zs = gate_s[3, rows, :]
            sg = _sigmoid(zs)
            sgs = zs * sg
            du = dso * mix * sgs
            dmx = dso * u * sgs
            dzs = dso * u * mix * (sg * (1.0 + zs * (1.0 - sg)))
            us = gate_s[1, rows, :]
            dus = du * (cdfu_s[rows, :] + us * _norm_pdf(us))
            vln = (vhat_s[rows, :] * lng + lnb).astype(act)
            for p in range(4):
                vlnp_s[p, :, c * BLOCK:(c + 1) * BLOCK] = vln[:, p * BLOCK:(p + 1) * BLOCK]
            split_pairs(dmx, c)
            acc_dbs[...] += dmx
            for off, val in ((OFF_US, dus), (OFF_ZS, dzs)):
                dproj_ref[rows, off:off + 512] = val.astype(act)
                acc_bin[:, off:off + 512] += _rows8(val)
        for p in range(4):
            dvln = _mm(wb_v[p], rhs_s[p])
            for c in range(nb):
                dvln_s[c * BLOCK:(c + 1) * BLOCK, p * BLOCK:(p + 1) * BLOCK] = dvln[:, c * BLOCK:(c + 1) * BLOCK]
            acc_dws[(2 * p) * BLOCK:(2 * p + 1) * BLOCK, :] += _mm_nt(rhs_s[p, 0:BLOCK, :], vlnp_s[p])
            acc_dws[(2 * p + 1) * BLOCK:(2 * p + 2) * BLOCK, :] += _mm_nt(rhs_s[p, BLOCK:, :], vlnp_s[p])
        for c in range(nb):
            rows = slice(c * BLOCK, (c + 1) * BLOCK)
            dvln = dvln_s[rows, :]
            vhat = vhat_s[rows, :]
            acc_lng[...] += _rows8(dvln * vhat)
            acc_lnb[...] += _rows8(dvln)
            dvhat = dvln * lng
            dvg = r2_s[rows, :] * (dvhat - _mean_rows(dvhat) - vhat * _mean_rows(dvhat * vhat))
            vs = gate_s[2, rows, :]
            dvs = dvg * (cdfv_s[rows, :] + vs * _norm_pdf(vs))
            dproj_ref[rows, OFF_VS:OFF_VS + 512] = dvs.astype(act)
            acc_bin[:, OFF_VS:OFF_VS + 512] += _rows8(dvs)

        dkf_s[...] = jnp.zeros(dkf_s.shape, F32)
        dvf_s[...] = jnp.zeros(dvf_s.shape, F32)
        for b in range(nb):
            rows = slice(b * BLOCK, (b + 1) * BLOCK)
            band = slice(b * BLOCK, (b + 2) * BLOCK)
            za = gate_s[0, rows, :]
            sg = _sigmoid(za)
            dao = dmix_s[rows, 0:ATTN_W]
            o = o_s[rows, :]
            do = dao * (za * sg)
            dza = dao * o * (sg * (1.0 + za * (1.0 - sg)))
            dproj_ref[rows, OFF_ZA:OFF_US] = dza.astype(act)
            acc_bin[:, OFF_ZA:OFF_US] += _rows8(dza)
            for g in range(2):
                do_parts, delta_parts = [], []
                for pr in range(2):
                    cols = slice((2 * g + pr) * BLOCK, (2 * g + pr + 1) * BLOCK)
                    d_pair = do[:, cols]
                    prod = d_pair * o[:, cols]
                    do_parts += [jnp.where(lo, d_pair, 0.0).astype(act), jnp.where(lo, 0.0, d_pair).astype(act)]
                    delta_parts += [jnp.sum(jnp.where(lo, prod, 0.0), axis=1, keepdims=True),
                                    jnp.sum(jnp.where(lo, 0.0, prod), axis=1, keepdims=True)]
                do_st = jnp.concatenate(do_parts, axis=0)
                delta = jnp.concatenate(delta_parts, axis=0)
                p = p_s[b * 2 + g]
                dp = _mm_nt(do_st, v2_s[g, band, :])
                ds = p * (dp - delta)
                sink_t = ps_s[b * 2 + g] * delta
                for k in range(4):
                    acc_sink[4 * g + k:4 * g + k + 1, :] += -jnp.sum(sink_t[k * BLOCK:(k + 1) * BLOCK], axis=0, keepdims=True)
                ds_a = ds.astype(act)
                dq2 = _mm(ds_a, k2_s[g, band, :]) * SCALE
                for pr in range(2):
                    cols = slice((2 * g + pr) * BLOCK, (2 * g + pr + 1) * BLOCK)
                    dq = jnp.where(lo, dq2[(2 * pr) * BLOCK:(2 * pr + 1) * BLOCK], dq2[(2 * pr + 1) * BLOCK:(2 * pr + 2) * BLOCK])
                    dproj_ref[rows, cols] = dq.astype(act)
                    acc_bin[:, cols] += _rows8(dq)
                lo_band = lax.broadcasted_iota(jnp.int32, (2 * BLOCK, BLOCK), 1) < HEAD_DIM
                mine = lo_band if g == 0 else jnp.logical_not(lo_band)
                for acc, lhs, rhs in ((dkf_s, ds_a, stacked_q(b, g)), (dvf_s, p.astype(act), do_st)):
                    d2 = _mm_tn(lhs, rhs)
                    full = d2 + pltpu.roll(d2, HEAD_DIM, 1)
                    acc[band, :] += jnp.where(mine, full, 0.0)
        for acc, carry, off in ((dkf_s, carryk_s, OFF_K), (dvf_s, carryv_s, OFF_V)):
            acc[t:t + BLOCK, :] += carry[...]
            carry[...] = acc[0:BLOCK, :]
            d = acc[BLOCK:, :]
            dproj_ref[:, off:off + BLOCK] = d.astype(act)
            acc_bin[:, off:off + BLOCK] += _rows8(d)

        dh = _mm(dproj_ref[...], wint_v[...])
        dprojt_ref[...] = dproj_ref[...].T
        mixedt_ref[...] = mixed_ref[...].T
        for c in range(nb):
            rows = slice(c * BLOCK, (c + 1) * BLOCK)
            r1 = r1_s[rows, :]
            xn = x_ref[rows, :] * r1
            dhc = dh[rows, :]
            acc_g1[...] += _rows8(dhc * xn)
            dxn = dhc * g1
            dx_ref[rows, :] = doutf_s[rows, :] + r1 * (dxn - xn * _mean_rows(dxn * xn))

        @pl.when(i == nt - 1)
        def _():
            tril = lax.broadcasted_iota(jnp.int32, (BLOCK, BLOCK), 0) >= lane128
            for hh in range(N_SGU_HEADS):
                rws = slice(hh * BLOCK, (hh + 1) * BLOCK)
                sw_ref[rws, :] = jnp.where(tril, acc_dws[rws, :], 0.0)
            vec_ref[...] = jnp.zeros((VEC_ROWS, 128), F32)

            def put(row0, acc):
                s = jnp.sum(acc[...], axis=0, keepdims=True)
                for k in range(acc.shape[1] // 128):
                    vec_ref[row0 + k:row0 + k + 1, :] = s[:, k * 128:(k + 1) * 128]

            put(R_G1, acc_g1)
            put(R_BIN, acc_bin)
            put(R_LNG, acc_lng)
            put(R_LNB, acc_lnb)
            put(R_BOUT, acc_bout)
            put(R_G3, acc_g3)
            vec_ref[R_SINK:R_SINK + 1, :] = jnp.sum(
                jnp.where(lax.broadcasted_iota(jnp.int32, (8, 128), 0) == lax.broadcasted_iota(jnp.int32, (8, 128), 1),
                          acc_sink[...], 0.0), axis=0, keepdims=True)
            vec_ref[R_LOSS:R_LOSS + 1, :] = jnp.zeros((1, 128), F32) + jnp.sum(acc_loss[...])
            dbs_t = acc_dbs[...].T
            vec_ref[R_SGUB:R_SGUB + 8, :] = jnp.sum(dbs_t.reshape(N_SGU_HEADS, SGU_W // N_SGU_HEADS, BLOCK), axis=1)

    full = lambda shape: pl.BlockSpec(shape, lambda i: (0,) * len(shape))
    tok = lambda w: pl.BlockSpec((t, w), lambda i: (nt - 1 - i, 0))
    in_specs = [
        pl.BlockSpec(memory_space=pltpu.SMEM),
        tok(D_MODEL),
        pl.BlockSpec((BLOCK, D_MODEL), lambda i: (jnp.maximum((nt - 1 - i) * nb - 1, 0), 0)),
        tok(D_MODEL),
        full((1, D_MODEL)), full((1, IN_W)), full((1, SGU_W)), full((1, SGU_W)),
        full((N_SGU_HEADS, BLOCK, BLOCK)), full((BLOCK, SGU_W)), full((1, D_MODEL)), full((1, D_MODEL)),
        pl.BlockSpec(memory_space=pl.ANY), pl.BlockSpec(memory_space=pl.ANY),
    ]
    out_shape = [
        jax.ShapeDtypeStruct((seq, D_MODEL), F32),
        jax.ShapeDtypeStruct((seq, D_MODEL), act),
        jax.ShapeDtypeStruct((IN_W, seq), act),
        jax.ShapeDtypeStruct((D_MODEL, seq), act),
        jax.ShapeDtypeStruct((seq, D_MODEL), act),
        jax.ShapeDtypeStruct((SW_ROWS, 128), F32),
        jax.ShapeDtypeStruct((VEC_ROWS, 128), F32),
    ]
    tok_t = lambda w: pl.BlockSpec((w, t), lambda i: (0, nt - 1 - i))
    out_specs = [tok(D_MODEL), tok(D_MODEL), tok_t(IN_W), tok_t(D_MODEL), tok(D_MODEL), full((SW_ROWS, 128)), full((VEC_ROWS, 128))]
    vm = pltpu.VMEM
    scratch = [
        vm((t, IN_W), act), vm((t, D_MODEL), act),
        vm((IN_W, D_MODEL), act), vm((D_MODEL, D_MODEL), act),
        vm((4, BLOCK, 2 * BLOCK), act), vm((4, BLOCK, 2 * BLOCK), act),
        vm((t, ATTN_W), act),
        vm((t + BLOCK, BLOCK), F32), vm((t + BLOCK, BLOCK), F32),
        vm((2, t + BLOCK, BLOCK), act), vm((2, t + BLOCK, BLOCK), act),
        vm((4, t, 512), F32),
        vm((2 * nb, 4 * BLOCK, 2 * BLOCK), F32), vm((2 * nb, 4 * BLOCK, 1), F32),
        vm((t, ATTN_W), F32), vm((t, SGU_W), F32), vm((t, SGU_W), F32), vm((t, SGU_W), F32), vm((t, 1), F32),
        vm((t, SGU_W), F32), vm((t, SGU_W), F32), vm((t, 1), F32),
        vm((t, D_MODEL), F32), vm((t, D_MODEL), F32),
        vm((t + BLOCK, BLOCK), F32), vm((t + BLOCK, BLOCK), F32), vm((BLOCK, BLOCK), F32), vm((BLOCK, BLOCK), F32),
        vm((8, IN_W), F32), vm((8, D_MODEL), F32), vm((8, D_MODEL), F32), vm((8, D_MODEL), F32),
        vm((8, SGU_W), F32), vm((8, SGU_W), F32), vm((N_SGU_HEADS * BLOCK, BLOCK), F32), vm((BLOCK, SGU_W), F32),
        vm((8, 128), F32), vm((8, 128), F32),
        vm((t + BLOCK, D_MODEL), act), vm((4, 2 * BLOCK, t), act), vm((4, BLOCK, t), act), vm((t, SGU_W), F32),
        pltpu.SemaphoreType.DMA((2,)),
    ]
    return pl.pallas_call(
        body, name="fused", grid=(nt,), in_specs=in_specs, out_specs=out_specs, out_shape=out_shape,
        scratch_shapes=scratch,
        compiler_params=pltpu.CompilerParams(dimension_semantics=("arbitrary",), vmem_limit_bytes=VMEM_LIMIT),
    )(sinks, x, x, tgt, g1, b_in, ln_g, ln_b, sgu_w, bexp, b_out, g3, win_t, wout)


def _gather_call(a32, b32):
    act = MXU_DTYPE

    def body(a32_hbm, b32_hbm, ga_hbm, gb_hbm, a32_v, b32_v, ga_v, gb_v, send_sems, recv_sems, loc_sems):
        x, y, c = lax.axis_index("x"), lax.axis_index("y"), lax.axis_index("c")
        me, sibling, x_nbr, y_nbr = (x, y, c), (x, y, 1 - c), (1 - x, y, c), (x, 1 - y, c)
        j, j_x, j_y, j_d = 2 * x + y, 2 * (1 - x) + y, 2 * x + (1 - y), 2 * (1 - x) + (1 - y)
        arrays = ((0, ga_v, ga_hbm, W_IN_SHARD), (1, gb_v, gb_hbm, W_OUT_SHARD))

        loads = [pltpu.make_async_copy(a32_hbm, a32_v, loc_sems.at[0]), pltpu.make_async_copy(b32_hbm, b32_v, loc_sems.at[1])]
        for cp in loads:
            cp.start()
        for cp in loads:
            cp.wait()
        ga_v[pl.ds(pl.multiple_of(j * W_IN_SHARD, 16), W_IN_SHARD), :] = a32_v[...].astype(act)
        gb_v[pl.ds(pl.multiple_of(j * W_OUT_SHARD, 16), W_OUT_SHARD), :] = b32_v[...].astype(act)

        def rows(shard, chip, hf=None, q=None):
            if hf is None:
                return pl.ds(pl.multiple_of(chip * shard, 16), shard)
            return pl.ds(pl.multiple_of(chip * shard + hf * (shard // 2) + q * (shard // 4), 16), shard // 4)

        def copy(k, ref, at, to):
            return pltpu.make_async_remote_copy(src_ref=ref.at[at, :], dst_ref=ref.at[at, :], send_sem=send_sems.at[k],
                                                recv_sem=recv_sems.at[k], device_id=to, device_id_type=MESH)

        stores = []

        def store(vmem, hbm, at):
            stores.append(pltpu.make_async_copy(vmem.at[at, :], hbm.at[at, :], loc_sems.at[len(stores)]))
            stores[-1].start()

        sent = []
        for w, vmem, hbm, shard in arrays:
            for k, (q, to) in enumerate(((0, x_nbr), (1, x_nbr), (1, y_nbr), (0, y_nbr))):
                sent.append(copy(12 * w + k, vmem, rows(shard, j, c, q), to))
                sent[-1].start()
        for w, vmem, hbm, shard in arrays:
            store(vmem, hbm, rows(shard, j))

        def landed(w, vmem, hbm, shard, k, chip, q, onward=None):
            at = rows(shard, chip, c, q)
            copy(12 * w + k, vmem, at, me).wait_recv()
            if onward is not None:
                sent.append(copy(12 * w + onward[0], vmem, at, onward[1]))
                sent[-1].start()
            sent.append(copy(12 * w + 6 + k, vmem, at, sibling))
            sent[-1].start()
            store(vmem, hbm, at)

        for arr in arrays:
            landed(*arr, 0, j_x, 0, onward=(4, y_nbr))
            landed(*arr, 2, j_y, 1, onward=(5, x_nbr))
        for arr in arrays:
            landed(*arr, 1, j_x, 1)
            landed(*arr, 3, j_y, 0)
            landed(*arr, 4, j_d, 0)
            landed(*arr, 5, j_d, 1)
        for w, vmem, hbm, shard in arrays:
            for k, (chip, q) in enumerate(((j_x, 0), (j_x, 1), (j_y, 1), (j_y, 0), (j_d, 0), (j_d, 1))):
                at = rows(shard, chip, 1 - c, q)
                copy(12 * w + 6 + k, vmem, at, me).wait_recv()
                store(vmem, hbm, at)
        for cp in sent:
            cp.wait_send()
        for cp in stores:
            cp.wait()

    hbm = pl.BlockSpec(memory_space=pl.ANY)
    vm = pltpu.VMEM
    return pl.pallas_call(
        body, name="gather", in_specs=[hbm, hbm], out_specs=[hbm, hbm],
        out_shape=[jax.ShapeDtypeStruct((IN_W, D_MODEL), act), jax.ShapeDtypeStruct((D_MODEL, D_MODEL), act)],
        scratch_shapes=[vm(a32.shape, F32), vm(b32.shape, F32), vm((IN_W, D_MODEL), act), vm((D_MODEL, D_MODEL), act),
                        pltpu.SemaphoreType.DMA((24,)), pltpu.SemaphoreType.DMA((24,)), pltpu.SemaphoreType.DMA((26,))],
        compiler_params=pltpu.CompilerParams(vmem_limit_bytes=VMEM_LIMIT),
    )(a32, b32)


def _wgrad_reduce_call(dproj_t, h, mixed_t, dout, sw, vec):
    wire = jnp.bfloat16
    half_sw = SW_ROWS // 2
    seq = h.shape[0]
    tk = min(1024, seq)
    nk = seq // tk
    n_steps = 2 * N_CHIPS
    rel_of = lambda s: s % 3 if s < 6 else 3
    half_of = lambda s: s // 3 if s < 6 else s - 6
    x, y = lax.axis_index("x"), lax.axis_index("y")
    chip_of = [2 * (1 - x) + (1 - y), 2 * (1 - x) + y, 2 * x + (1 - y), 2 * x + y]
    order = jnp.stack([2 * chip_of[rel_of(s)] + half_of(s) for s in range(n_steps)]).astype(jnp.int32)

    def body(order_ref, dpt_ref, h_hbm, mxt_ref, dout_hbm, sw_ref, small_ref, oa_ref, ob_ref, osw_ref, osmall_ref,
             h_v, dout_v, acc_a, acc_b, sib_a, sib_b, snd_a, snd_b, in_a, in_b, own_a, own_b, fin_a, fin_b,
             all_small, sw_sib, sw_chips, sw_fin, send_sems, recv_sems, loc_sems):
        x, y, c = lax.axis_index("x"), lax.axis_index("y"), lax.axis_index("c")
        me, sibling = (x, y, c), (x, y, 1 - c)
        steps = [(1 - x, 1 - y), (1 - x, y), (x, 1 - y)]
        j = 2 * x + y
        dev = 4 * x + 2 * y + c
        b, k = pl.program_id(0), pl.program_id(1)

        def copy(n, src, dst, to):
            return pltpu.make_async_remote_copy(src_ref=src, dst_ref=dst, send_sem=send_sems.at[n], recv_sem=recv_sems.at[n],
                                                device_id=to, device_id_type=MESH)

        def sw_rows(ref, hf):
            return ref.at[pl.ds(pl.multiple_of(hf * half_sw, 8), half_sw), :]

        sm_first = [copy(16, small_ref, all_small.at[dev], sibling)]
        sm_first += [copy(17 + r, small_ref, all_small.at[dev], (*chip, c)) for r, chip in enumerate(steps)]
        sw_to_sib = copy(23, sw_rows(sw_ref, 1 - c), sw_sib, sibling)

        @pl.when((b == 0) & (k == 0))
        def _():
            all_small[dev] = small_ref[...]
            for cp in sm_first + [sw_to_sib]:
                cp.start()

        def load(kk):
            rows = pl.ds(kk * tk, tk)
            return [pltpu.make_async_copy(h_hbm.at[rows, :], h_v.at[rows, :], loc_sems.at[kk]),
                    pltpu.make_async_copy(dout_hbm.at[rows, :], dout_v.at[rows, :], loc_sems.at[nk + kk])]

        for kk in range(nk):
            @pl.when((b == 0) & (k == 0))
            def _():
                for cp in load(kk):
                    cp.start()

        for kk in range(nk):
            @pl.when((b == 0) & (k == kk))
            def _():
                for cp in load(kk):
                    cp.wait()

        tok = pl.ds(pl.multiple_of(k * tk, tk), tk)
        pa = _mm(dpt_ref[...], h_v[tok, :])
        pb = _mm(mxt_ref[...], dout_v[tok, :])
        slot = b % 2

        @pl.when(k == 0)
        def _():
            acc_a[slot] = pa
            acc_b[slot] = pb

        @pl.when(k != 0)
        def _():
            acc_a[slot] += pa
            acc_b[slot] += pb

        def to_sibling(s):
            r = rel_of(s)
            return [copy(r, acc_a.at[s % 2], sib_a.at[r], sibling), copy(4 + r, acc_b.at[s % 2], sib_b.at[r], sibling)]

        def chip_partial(s):
            r = rel_of(s)
            copy(r, sib_a.at[r], sib_a.at[r], me).wait_recv()
            copy(4 + r, sib_b.at[r], sib_b.at[r], me).wait_recv()
            return acc_a[s % 2] + sib_a[r], acc_b[s % 2] + sib_b[r]

        def to_owner(r):
            return [copy(8 + r, snd_a.at[r], in_a.at[r], (*steps[r], c)), copy(11 + r, snd_b.at[r], in_b.at[r], (*steps[r], c))]

        def vec_forwards():
            return [copy(20 + r, all_small.at[4 * cx + 2 * cy + c], all_small.at[4 * cx + 2 * cy + c], sibling)
                    for r, (cx, cy) in enumerate(steps)]

        def sw_to_chips():
            return [copy(24 + r, sw_chips.at[j], sw_chips.at[j], (*chip, c)) for r, chip in enumerate(steps)]

        def sw_to_sibling():
            return copy(27, sw_rows(osw_ref, c), sw_fin, sibling)

        @pl.when((b == 1) & (k == 0))
        def _():
            for r, (cx, cy) in enumerate(steps):
                d = 4 * cx + 2 * cy + c
                copy(17 + r, all_small.at[d], all_small.at[d], me).wait_recv()
            for cp in vec_forwards():
                cp.start()
            copy(23, sw_sib, sw_sib, me).wait_recv()
            sw_chips[j] = (sw_rows(sw_ref, c)[...] + sw_sib[...]).astype(wire)
            for cp in sw_to_chips():
                cp.start()

        @pl.when((b == 3) & (k == 0))
        def _():
            for r, (cx, cy) in enumerate(steps):
                cj = 2 * cx + cy
                copy(24 + r, sw_chips.at[cj], sw_chips.at[cj], me).wait_recv()
            tot_sw = sw_chips[0].astype(F32)
            for q in range(1, N_CHIPS):
                tot_sw = tot_sw + sw_chips[q].astype(F32)
            sw_rows(osw_ref, c)[...] = tot_sw
            sw_to_sibling().start()

        late = min(1, nk - 1)
        for s in range(n_steps):
            if s >= 1:
                sp = s - 1

                @pl.when((b == s) & (k == late) & (c == half_of(sp)))
                def _():
                    ta, tb = chip_partial(sp)
                    r = rel_of(sp)
                    if r < 3:
                        snd_a[r] = ta.astype(wire)
                        snd_b[r] = tb.astype(wire)
                        for cp in to_owner(r):
                            cp.start()
                    else:
                        own_a[...] = ta
                        own_b[...] = tb

                @pl.when((b == s) & (k == nk - 1) & (c != half_of(sp)))
                def _():
                    for cp in to_sibling(sp):
                        cp.wait_send()

            @pl.when((b == s) & (k == nk - 1) & (c != half_of(s)))
            def _():
                for cp in to_sibling(s):
                    cp.start()

        @pl.when((b == n_steps - 1) & (k == nk - 1))
        def _():
            last = n_steps - 1

            @pl.when(c == half_of(last))
            def _():
                own_a[...], own_b[...] = chip_partial(last)

            @pl.when(c != half_of(last))
            def _():
                for cp in to_sibling(last):
                    cp.wait_send()

            tot_a = own_a[...]
            tot_b = own_b[...]
            for s in range(3):
                copy(8 + s, in_a.at[s], in_a.at[s], me).wait_recv()
                copy(11 + s, in_b.at[s], in_b.at[s], me).wait_recv()
                tot_a = tot_a + in_a[s].astype(F32)
                tot_b = tot_b + in_b[s].astype(F32)
            mine_a = oa_ref.at[pl.ds(pl.multiple_of(c * HALF_A, 8), HALF_A), :]
            mine_b = ob_ref.at[pl.ds(pl.multiple_of(c * HALF_B, 8), HALF_B), :]
            mine_a[...] = tot_a
            mine_b[...] = tot_b
            back = [copy(14, mine_a, fin_a, sibling), copy(15, mine_b, fin_b, sibling)]
            for cp in back:
                cp.start()

            copy(16, small_ref, all_small.at[dev ^ 1], me).wait_recv()
            for r, (cx, cy) in enumerate(steps):
                d = 4 * cx + 2 * cy + (1 - c)
                copy(20 + r, all_small.at[d], all_small.at[d], me).wait_recv()
            tot = all_small[0]
            for d in range(1, N_DEV):
                tot = tot + all_small[d]
            osmall_ref[...] = tot

            copy(14, fin_a, fin_a, me).wait_recv()
            copy(15, fin_b, fin_b, me).wait_recv()
            copy(27, sw_fin, sw_fin, me).wait_recv()
            oa_ref[pl.ds(pl.multiple_of((1 - c) * HALF_A, 8), HALF_A), :] = fin_a[...]
            ob_ref[pl.ds(pl.multiple_of((1 - c) * HALF_B, 8), HALF_B), :] = fin_b[...]
            sw_rows(osw_ref, 1 - c)[...] = sw_fin[...]
            sends = sm_first + vec_forwards() + sw_to_chips() + back + [sw_to_sib, sw_to_sibling()]
            for s in range(3):
                sends += to_owner(s)
            for cp in sends:
                cp.wait_send()

    vmem = pl.BlockSpec(memory_space=pltpu.VMEM)
    vm = pltpu.VMEM
    grid_spec = pltpu.PrefetchScalarGridSpec(
        num_scalar_prefetch=1, grid=(n_steps, nk),
        in_specs=[pl.BlockSpec((HALF_A, tk), lambda b, k, o: (o[b], k)), pl.BlockSpec(memory_space=pl.ANY),
                  pl.BlockSpec((HALF_B, tk), lambda b, k, o: (o[b], k)), pl.BlockSpec(memory_space=pl.ANY),
                  vmem, vmem],
        out_specs=[vmem, vmem, vmem, vmem],
        scratch_shapes=[vm((seq, D_MODEL), h.dtype), vm((seq, D_MODEL), dout.dtype),
                        vm((2, HALF_A, D_MODEL), F32), vm((2, HALF_B, D_MODEL), F32),
                        vm((N_CHIPS, HALF_A, D_MODEL), F32), vm((N_CHIPS, HALF_B, D_MODEL), F32),
                        vm((3, HALF_A, D_MODEL), wire), vm((3, HALF_B, D_MODEL), wire),
                        vm((3, HALF_A, D_MODEL), wire), vm((3, HALF_B, D_MODEL), wire),
                        vm((HALF_A, D_MODEL), F32), vm((HALF_B, D_MODEL), F32),
                        vm((HALF_A, D_MODEL), F32), vm((HALF_B, D_MODEL), F32),
                        vm((N_DEV, VEC_ROWS, 128), F32), vm((half_sw, 128), F32), vm((N_CHIPS, half_sw, 128), wire),
                        vm((half_sw, 128), F32),
                        pltpu.SemaphoreType.DMA((28,)), pltpu.SemaphoreType.DMA((28,)), pltpu.SemaphoreType.DMA((2 * nk,))])
    return pl.pallas_call(
        body, name="wgrad_reduce", grid_spec=grid_spec,
        out_shape=[jax.ShapeDtypeStruct((W_IN_SHARD, D_MODEL), F32), jax.ShapeDtypeStruct((W_OUT_SHARD, D_MODEL), F32),
                   jax.ShapeDtypeStruct((SW_ROWS, 128), F32), jax.ShapeDtypeStruct((VEC_ROWS, 128), F32)],
        compiler_params=pltpu.CompilerParams(dimension_semantics=("arbitrary", "arbitrary"), vmem_limit_bytes=VMEM_LIMIT),
    )(order, dproj_t, h, mixed_t, dout, sw, vec)


def _adamw(w, g, m, v):
    nm = ADAM_B1 * m + (1.0 - ADAM_B1) * g
    nv = ADAM_B2 * v + (1.0 - ADAM_B2) * (g * g)
    m_hat = nm / (1.0 - ADAM_B1 ** ADAM_STEP)
    v_hat = nv / (1.0 - ADAM_B2 ** ADAM_STEP)
    return -ADAM_LR * (m_hat / (jnp.sqrt(v_hat) + ADAM_EPS) + ADAM_WD * w), nm, nv


def _adamw_shards_call(a, b, steps=4):
    def body(*refs):
        for k in range(2):
            w_ref, g_ref, m_ref, v_ref = refs[4 * k:4 * k + 4]
            go_ref, d_ref, nm_ref, nv_ref = refs[8 + 4 * k:12 + 4 * k]
            gg = g_ref[...]
            go_ref[...] = gg
            d_ref[...], nm_ref[...], nv_ref[...] = _adamw(w_ref[...], gg, m_ref[...], v_ref[...])

    specs, shapes = [], []
    for w in (a[0], b[0]):
        rows, cols = w.shape
        specs += [pl.BlockSpec((rows // steps, cols), lambda i: (i, 0))] * 4
        shapes += [jax.ShapeDtypeStruct((rows, cols), F32)] * 4
    return pl.pallas_call(
        body, name="adamw_shards", grid=(steps,), in_specs=specs, out_specs=specs, out_shape=shapes,
        compiler_params=pltpu.CompilerParams(dimension_semantics=("arbitrary",)),
    )(*a, *b)


_SMALL = (("norm_g", (1, D_MODEL), R_G1), ("b_in", (1, IN_W), R_BIN), ("attn_sinks", (1, 8), R_SINK),
          ("sgu_ln_g", (1, SGU_W), R_LNG), ("sgu_ln_b", (1, SGU_W), R_LNB), ("sgu_b", (N_SGU_HEADS, BLOCK), R_SGUB),
          ("b_out", (1, D_MODEL), R_BOUT), ("final_norm_g", (1, D_MODEL), R_G3))


def _adamw_small_call(sw_g, vec_g, sgu_w3, ws, ms, vs):
    n = len(_SMALL)

    def body(*refs):
        sw_ref, vec_ref = refs[0], refs[1]
        w3 = refs[2:5]
        w_refs, m_refs, v_refs = refs[5:5 + n], refs[5 + n:5 + 2 * n], refs[5 + 2 * n:5 + 3 * n]
        outs = refs[5 + 3 * n:]
        outs[0][...] = vec_ref[R_LOSS:R_LOSS + 1, 0:1]
        g = sw_ref[...]
        outs[1][...] = g
        outs[2][...], outs[3][...], outs[4][...] = _adamw(w3[0][...], g, w3[1][...], w3[2][...])
        for k, (_, shape, row) in enumerate(_SMALL):
            if shape[0] == 1 and shape[1] >= 128:
                g = jnp.concatenate([vec_ref[row + q:row + q + 1, :] for q in range(shape[1] // 128)], axis=1)
            else:
                g = vec_ref[row:row + shape[0], 0:shape[1]]
            o = outs[5 + 4 * k:9 + 4 * k]
            o[0][...] = g
            o[1][...], o[2][...], o[3][...] = _adamw(w_refs[k][...], g, m_refs[k][...], v_refs[k][...])

    vmem = pl.BlockSpec(memory_space=pltpu.VMEM)
    out_shape = [jax.ShapeDtypeStruct((1, 1), F32)] + [jax.ShapeDtypeStruct((SW_ROWS, 128), F32)] * 4
    for _, shape, _ in _SMALL:
        out_shape += [jax.ShapeDtypeStruct(shape, F32)] * 4
    args = [sw_g, vec_g, *sgu_w3, *ws, *ms, *vs]
    return pl.pallas_call(
        body, name="adamw_small", in_specs=[vmem] * len(args), out_specs=[vmem] * len(out_shape), out_shape=out_shape,
    )(*args)


def kernel(x, norm_g, w_in, b_in, attn_sinks, sgu_ln_g, sgu_ln_b, sgu_w, sgu_b, w_out, b_out, final_norm_g, loss_target, m_norm_g, m_w_in, m_b_in, m_attn_sinks, m_sgu_ln_g, m_sgu_ln_b, m_sgu_w, m_sgu_b, m_w_out, m_b_out, m_final_norm_g, v_norm_g, v_w_in, v_b_in, v_attn_sinks, v_sgu_ln_g, v_sgu_ln_b, v_sgu_w, v_sgu_b, v_w_out, v_b_out, v_final_norm_g):
    seq = x.shape[1]
    win_t, wout = _gather_call(w_in[0].T, w_out[0])
    bexp = jnp.repeat(sgu_b[0].T, SGU_W // N_SGU_HEADS, axis=1)
    dx, h, dproj_t, mixed_t, dout, sw, vec = _fused_call(
        x[0], loss_target[0], attn_sinks[0], norm_g, b_in, sgu_ln_g, sgu_ln_b, sgu_w[0], bexp, b_out,
        final_norm_g.reshape(1, D_MODEL), win_t, wout)
    ga_t, g_w_out, sw, vec = _wgrad_reduce_call(dproj_t, h, mixed_t, dout, sw, vec)

    names = ["norm_g", "w_in", "b_in", "attn_sinks", "sgu_ln_g", "sgu_ln_b", "sgu_w", "sgu_b", "w_out", "b_out", "final_norm_g"]
    res = {}
    shards = _adamw_shards_call((w_in[0].T, ga_t, m_w_in[0].T, v_w_in[0].T), (w_out[0], g_w_out, m_w_out[0], v_w_out[0]))
    res["w_in"] = [a.T[None] for a in shards[:4]]
    res["w_out"] = [a[None] for a in shards[4:]]
    given = dict(norm_g=(norm_g, m_norm_g, v_norm_g), b_in=(b_in, m_b_in, v_b_in), attn_sinks=(attn_sinks, m_attn_sinks, v_attn_sinks),
                 sgu_ln_g=(sgu_ln_g, m_sgu_ln_g, v_sgu_ln_g), sgu_ln_b=(sgu_ln_b, m_sgu_ln_b, v_sgu_ln_b),
                 sgu_b=(sgu_b, m_sgu_b, v_sgu_b), b_out=(b_out, m_b_out, v_b_out),
                 final_norm_g=(final_norm_g, m_final_norm_g, v_final_norm_g))
    wmv = [[given[n][k].reshape(shape) for n, shape, _ in _SMALL] for k in range(3)]
    outs = _adamw_small_call(sw, vec, [a.reshape(SW_ROWS, BLOCK) for a in (sgu_w, m_sgu_w, v_sgu_w)], *wmv)
    loss = outs[0].reshape(())
    res["sgu_w"] = [a.reshape(sgu_w.shape) for a in outs[1:5]]
    for k, (n, _, _) in enumerate(_SMALL):
        res[n] = [a.reshape(given[n][0].shape) for a in outs[5 + 4 * k:9 + 4 * k]]

    return (loss, dx[None], *[res[n][0] for n in names], *[res[n][1] for n in names], *[res[n][2] for n in names],
            *[res[n][3] for n in names])
```

```python
import functools
import math

import jax
import jax.numpy as jnp
from jax import lax
from jax.experimental import pallas as pl
from jax.experimental.pallas import tpu as pltpu

F32 = jnp.float32
MXU_DTYPE = jnp.bfloat16

D_MODEL = 1024
HEAD_DIM = 64
ATTN_W = 512
SGU_W = 512
N_SGU_HEADS = 8
BLOCK = 128
IN_W = 2816
OFF_K, OFF_V, OFF_ZA, OFF_US, OFF_VS, OFF_ZS = 512, 640, 768, 1280, 1792, 2304
NORM_EPS = 1e-5
NEG_INF = -1e30
SCALE = HEAD_DIM ** -0.5
SQRT_HALF = math.sqrt(0.5)
INV_SQRT_2PI = 1.0 / math.sqrt(2.0 * math.pi)

N_CHIPS = 4
N_DEV = 8
W_IN_SHARD = IN_W // N_CHIPS
W_OUT_SHARD = D_MODEL // N_CHIPS
HALF_A = W_IN_SHARD // 2
HALF_B = W_OUT_SHARD // 2

TILE = 256
VMEM_LIMIT = 56 * 1024 * 1024

ADAM_LR, ADAM_B1, ADAM_B2, ADAM_EPS, ADAM_WD, ADAM_STEP = 0.001, 0.9, 0.999, 1e-08, 0.01, 10

SW_ROWS = N_SGU_HEADS * BLOCK
R_G1, R_BIN, R_SINK, R_LOSS, R_LNG, R_LNB, R_SGUB, R_BOUT, R_G3 = 0, 8, 32, 40, 48, 56, 64, 72, 80
VEC_ROWS = 88

MESH = pl.DeviceIdType.MESH


def _mm(a, b):
    return jnp.dot(a, b, preferred_element_type=F32)


def _mm_nt(a, b):
    return lax.dot_general(a, b, (((1,), (1,)), ((), ())), preferred_element_type=F32)


def _mm_tn(a, b):
    return lax.dot_general(a, b, (((0,), (0,)), ((), ())), preferred_element_type=F32)


def _sigmoid(z):
    return 1.0 / (1.0 + jnp.exp(-z))


def _norm_cdf(z):
    return 0.5 * (1.0 + lax.erf(z * SQRT_HALF))


def _norm_pdf(z):
    return jnp.exp(-0.5 * z * z) * INV_SQRT_2PI


def _rows8(v):
    r, n = v.shape
    return jnp.sum(v.reshape(r // 8, 8, n), axis=0)


def _mean_rows(v):
    return jnp.sum(v, axis=1, keepdims=True) * (1.0 / v.shape[1])


def _fused_call(x, tgt, sinks, g1, b_in, ln_g, ln_b, sgu_w, bexp, b_out, g3, win_t, wout):
    seq = x.shape[0]
    t = TILE
    nt = seq // t
    nb = t // BLOCK
    act = MXU_DTYPE

    def body(sinks_ref, x_ref, xh_ref, tgt_ref, g1_ref, bin_ref, lng_ref, lnb_ref, sguw_ref, bexp_ref, bout_ref, g3_ref,
             wint_hbm, wout_hbm,
             dx_ref, h_ref, dprojt_ref, mixedt_ref, dout_ref, sw_ref, vec_ref,
             dproj_ref, mixed_ref, wint_v, wout_v, wf_v, wb_v, q_s, kf_s, vf_s, k2_s, v2_s, gate_s, p_s, ps_s, o_s, u_s, mix_s, vhat_s, r2_s,
             cdfu_s, cdfv_s, r1_s, doutf_s, dmix_s, dkf_s, dvf_s, carryk_s, carryv_s,
             acc_bin, acc_g1, acc_bout, acc_g3, acc_lng, acc_lnb, acc_dws, acc_dbs, acc_sink, acc_loss,
             h_s, rhs_s, vlnp_s, dvln_s, sems):
        i = pl.program_id(0)
        tile = nt - 1 - i
        lane128 = lax.broadcasted_iota(jnp.int32, (BLOCK, BLOCK), 1)
        lo = lane128 < HEAD_DIM

        @pl.when(i == 0)
        def _():
            cp_a = pltpu.make_async_copy(wint_hbm, wint_v, sems.at[0])
            cp_b = pltpu.make_async_copy(wout_hbm, wout_v, sems.at[1])
            cp_a.start()
            cp_b.start()
            for acc in (acc_bin, acc_g1, acc_bout, acc_g3, acc_lng, acc_lnb, acc_dws, acc_dbs, acc_sink, acc_loss,
                        carryk_s, carryv_s):
                acc[...] = jnp.zeros(acc.shape, F32)
            tril = lax.broadcasted_iota(jnp.int32, (BLOCK, BLOCK), 0) >= lane128
            for h in range(N_SGU_HEADS):
                w = jnp.where(tril, sguw_ref[h], 0.0)
                wf_v[h // 2, :, (h % 2) * BLOCK:(h % 2 + 1) * BLOCK] = w.astype(act)
                wb_v[h // 2, :, (h % 2) * BLOCK:(h % 2 + 1) * BLOCK] = w.T.astype(act)
            cp_a.wait()
            cp_b.wait()

        g1 = g1_ref[...]

        def rms(v):
            r = lax.rsqrt(_mean_rows(v * v) + NORM_EPS)
            return r, v * r

        _, xnh = rms(xh_ref[...])
        h_s[0:BLOCK, :] = (xnh * g1).astype(act)
        for c in range(nb):
            rows = slice(c * BLOCK, (c + 1) * BLOCK)
            r, xn = rms(x_ref[rows, :])
            r1_s[rows, :] = r
            h_s[BLOCK + c * BLOCK:BLOCK + (c + 1) * BLOCK, :] = (xn * g1).astype(act)

        h = h_s[BLOCK:, :]
        h_ref[...] = h
        q = _mm_nt(h, wint_v[0:OFF_K, :]) + bin_ref[:, 0:OFF_K]
        q_s[...] = (q * SCALE).astype(act)
        kv = _mm_nt(h_s[...], wint_v[OFF_K:OFF_ZA, :]) + bin_ref[:, OFF_K:OFF_ZA]
        kf_s[...] = kv[:, :BLOCK]
        vf_s[...] = kv[:, BLOCK:]
        for r in range(4):
            cols = slice(OFF_ZA + r * 512, OFF_ZA + (r + 1) * 512)
            gate_s[r] = _mm_nt(h, wint_v[cols, :]) + bin_ref[:, cols]

        lo_kv = lax.broadcasted_iota(jnp.int32, (t + BLOCK, BLOCK), 1) < HEAD_DIM
        for src, dst in ((kf_s, k2_s), (vf_s, v2_s)):
            v = src[...]
            vr = pltpu.roll(v, HEAD_DIM, 1)
            dst[0] = jnp.where(lo_kv, v, vr).astype(act)
            dst[1] = jnp.where(lo_kv, vr, v).astype(act)

        rowi = lax.broadcasted_iota(jnp.int32, (BLOCK, 2 * BLOCK), 0)
        colj = lax.broadcasted_iota(jnp.int32, (BLOCK, 2 * BLOCK), 1)
        in_band = (colj > rowi) & (colj <= rowi + BLOCK)
        row512 = lax.broadcasted_iota(jnp.int32, (4 * BLOCK, 1), 0)

        def stacked_q(b, g):
            parts = []
            for p in range(2):
                slab = q_s[b * BLOCK:(b + 1) * BLOCK, (2 * g + p) * BLOCK:(2 * g + p + 1) * BLOCK]
                parts += [jnp.where(lo, slab, jnp.zeros_like(slab)), jnp.where(lo, jnp.zeros_like(slab), slab)]
            return jnp.concatenate(parts, axis=0)

        def sink_col(g):
            s = [sinks_ref[4 * g + k] for k in range(4)]
            return jnp.where(row512 < BLOCK, s[0], jnp.where(row512 < 2 * BLOCK, s[1], jnp.where(row512 < 3 * BLOCK, s[2], s[3])))

        for b in range(nb):
            band = slice(b * BLOCK, (b + 2) * BLOCK)
            first_key = jnp.where(tile * nb + b > 0, 0, BLOCK)
            valid = in_band & (colj >= first_key)
            valid4 = jnp.concatenate([valid] * 4, axis=0)
            for g in range(2):
                s = _mm_nt(stacked_q(b, g), k2_s[g, band, :])
                s = jnp.where(valid4, s, NEG_INF)
                sk = sink_col(g)
                m = jnp.maximum(jnp.max(s, axis=1, keepdims=True), sk)
                p = jnp.exp(s - m)
                psk = jnp.exp(sk - m)
                inv = 1.0 / (jnp.sum(p, axis=1, keepdims=True) + psk)
                p = p * inv
                p_s[b * 2 + g] = p
                ps_s[b * 2 + g] = psk * inv
                o2 = _mm(p.astype(act), v2_s[g, band, :])
                for pr in range(2):
                    o_s[b * BLOCK:(b + 1) * BLOCK, (2 * g + pr) * BLOCK:(2 * g + pr + 1) * BLOCK] = jnp.where(
                        lo, o2[(2 * pr) * BLOCK:(2 * pr + 1) * BLOCK], o2[(2 * pr + 1) * BLOCK:(2 * pr + 2) * BLOCK])

        lng = lng_ref[...]
        lnb = lnb_ref[...]

        def split_pairs(val, c):
            for p in range(4):
                slab = val[:, p * BLOCK:(p + 1) * BLOCK]
                rhs_s[p, 0:BLOCK, c * BLOCK:(c + 1) * BLOCK] = jnp.where(lo, slab, 0.0).astype(act)
                rhs_s[p, BLOCK:, c * BLOCK:(c + 1) * BLOCK] = jnp.where(lo, 0.0, slab).astype(act)

        for c in range(nb):
            rows = slice(c * BLOCK, (c + 1) * BLOCK)
            za = gate_s[0, rows, :]
            mixed_ref[rows, 0:ATTN_W] = (o_s[rows, :] * (za * _sigmoid(za))).astype(act)
            us = gate_s[1, rows, :]
            vs = gate_s[2, rows, :]
            cu = _norm_cdf(us)
            cv = _norm_cdf(vs)
            cdfu_s[rows, :] = cu
            cdfv_s[rows, :] = cv
            u = us * cu
            vg = vs * cv
            vc = vg - _mean_rows(vg)
            r2 = lax.rsqrt(_mean_rows(vc * vc) + NORM_EPS)
            vhat = vc * r2
            r2_s[rows, :] = r2
            vhat_s[rows, :] = vhat
            u_s[rows, :] = u
            split_pairs(vhat * lng + lnb, c)
        for p in range(4):
            cols = slice(p * BLOCK, (p + 1) * BLOCK)
            mix = _mm(wf_v[p], rhs_s[p])
            for c in range(nb):
                mix_s[c * BLOCK:(c + 1) * BLOCK, cols] = mix[:, c * BLOCK:(c + 1) * BLOCK] + bexp_ref[:, cols]
        for c in range(nb):
            rows = slice(c * BLOCK, (c + 1) * BLOCK)
            zs = gate_s[3, rows, :]
            mixed_ref[rows, ATTN_W:] = (u_s[rows, :] * mix_s[rows, :] * (zs * _sigmoid(zs))).astype(act)

        g3 = g3_ref[...]
        proj_o = _mm(mixed_ref[...], wout_v[...])
        for c in range(nb):
            rows = slice(c * BLOCK, (c + 1) * BLOCK)
            out = x_ref[rows, :] + proj_o[rows, :] + bout_ref[...]
            r3, on = rms(out)
            e = on * g3 - tgt_ref[rows, :]
            e2 = _rows8(e * e)
            acc_loss[...] += sum(e2[:, k * 128:(k + 1) * 128] for k in range(D_MODEL // 128)) * (0.5 / D_MODEL)
            dy = e * (1.0 / D_MODEL)
            acc_g3[...] += _rows8(dy * on)
            don = dy * g3
            dout = r3 * (don - on * _mean_rows(don * on))
            doutf_s[rows, :] = dout
            dout_ref[rows, :] = dout.astype(act)
            acc_bout[...] += _rows8(dout)
        dmix_s[...] = _mm_nt(dout_ref[...], wout_v[...])

        for c in range(nb):
            rows = slice(c * BLOCK, (c + 1) * BLOCK)
            dso = dmix_s[rows, ATTN_W:]
            u = u_s[rows, :]
            mix = mix_s[rows, :]
            zs = gate_s[3, rows, :]
            sg = _sigmoid(zs)
            sgs = zs * sg
            du = dso * mix * sgs
            dmx = dso * u * sgs
            dzs = dso * u * mix * (sg * (1.0 + zs * (1.0 - sg)))
            us = gate_s[1, rows, :]
            dus = du * (cdfu_s[rows, :] + us * _norm_pdf(us))
            vln = (vhat_s[rows, :] * lng + lnb).astype(act)
            for p in range(4):
                vlnp_s[p, :, c * BLOCK:(c + 1) * BLOCK] = vln[:, p * BLOCK:(p + 1) * BLOCK]
            split_pairs(dmx, c)
            acc_dbs[...] += dmx
            for off, val in ((OFF_US, dus), (OFF_ZS, dzs)):
                dproj_ref[rows, off:off + 512] = val.astype(act)
                acc_bin[:, off:off + 512] += _rows8(val)
        for p in range(4):
            dvln = _mm(wb_v[p], rhs_s[p])
            for c in range(nb):
                dvln_s[c * BLOCK:(c + 1) * BLOCK, p * BLOCK:(p + 1) * BLOCK] = dvln[:, c * BLOCK:(c + 1) * BLOCK]
            acc_dws[(2 * p) * BLOCK:(2 * p + 1) * BLOCK, :] += _mm_nt(rhs_s[p, 0:BLOCK, :], vlnp_s[p])
            acc_dws[(2 * p + 1) * BLOCK:(2 * p + 2) * BLOCK, :] += _mm_nt(rhs_s[p, BLOCK:, :], vlnp_s[p])
        for c in range(nb):
            rows = slice(c * BLOCK, (c + 1) * BLOCK)
            dvln = dvln_s[rows, :]
            vhat = vhat_s[rows, :]
            acc_lng[...] += _rows8(dvln * vhat)
            acc_lnb[...] += _rows8(dvln)
            dvhat = dvln * lng
            dvg = r2_s[rows, :] * (dvhat - _mean_rows(dvhat) - vhat * _mean_rows(dvhat * vhat))
            vs = gate_s[2, rows, :]
            dvs = dvg * (cdfv_s[rows, :] + vs * _norm_pdf(vs))
            dproj_ref[rows, OFF_VS:OFF_VS + 512] = dvs.astype(act)
            acc_bin[:, OFF_VS:OFF_VS + 512] += _rows8(dvs)

        dkf_s[...] = jnp.zeros(dkf_s.shape, F32)
        dvf_s[...] = jnp.zeros(dvf_s.shape, F32)
        for b in range(nb):
            rows = slice(b * BLOCK, (b + 1) * BLOCK)
            band = slice(b * BLOCK, (b + 2) * BLOCK)
            za = gate_s[0, rows, :]
            sg = _sigmoid(za)
            dao = dmix_s[rows, 0:ATTN_W]
            o = o_s[rows, :]
            do = dao * (za * sg)
            dza = dao * o * (sg * (1.0 + za * (1.0 - sg)))
            dproj_ref[rows, OFF_ZA:OFF_US] = dza.astype(act)
            acc_bin[:, OFF_ZA:OFF_US] += _rows8(dza)
            for g in range(2):
                do_parts, delta_parts = [], []
                for pr in range(2):
                    cols = slice((2 * g + pr) * BLOCK, (2 * g + pr + 1) * BLOCK)
                    d_pair = do[:, cols]
                    prod = d_pair * o[:, cols]
                    do_parts += [jnp.where(lo, d_pair, 0.0).astype(act), jnp.where(lo, 0.0, d_pair).astype(act)]
                    delta_parts += [jnp.sum(jnp.where(lo, prod, 0.0), axis=1, keepdims=True),
                                    jnp.sum(jnp.where(lo, 0.0, prod), axis=1, keepdims=True)]
                do_st = jnp.concatenate(do_parts, axis=0)
                delta = jnp.concatenate(delta_parts, axis=0)
                p = p_s[b * 2 + g]
                dp = _mm_nt(do_st, v2_s[g, band, :])
                ds = p * (dp - delta)
                sink_t = ps_s[b * 2 + g] * delta
                for k in range(4):
                    acc_sink[4 * g + k:4 * g + k + 1, :] += -jnp.sum(sink_t[k * BLOCK:(k + 1) * BLOCK], axis=0, keepdims=True)
                ds_a = ds.astype(act)
                dq2 = _mm(ds_a, k2_s[g, band, :]) * SCALE
                for pr in range(2):
                    cols = slice((2 * g + pr) * BLOCK, (2 * g + pr + 1) * BLOCK)
                    dq = jnp.where(lo, dq2[(2 * pr) * BLOCK:(2 * pr + 1) * BLOCK], dq2[(2 * pr + 1) * BLOCK:(2 * pr + 2) * BLOCK])
                    dproj_ref[rows, cols] = dq.astype(act)
                    acc_bin[:, cols] += _rows8(dq)
                lo_band = lax.broadcasted_iota(jnp.int32, (2 * BLOCK, BLOCK), 1) < HEAD_DIM
                mine = lo_band if g == 0 else jnp.logical_not(lo_band)
                for acc, lhs, rhs in ((dkf_s, ds_a, stacked_q(b, g)), (dvf_s, p.astype(act), do_st)):
                    d2 = _mm_tn(lhs, rhs)
                    full = d2 + pltpu.roll(d2, HEAD_DIM, 1)
                    acc[band, :] += jnp.where(mine, full, 0.0)
        for acc, carry, off in ((dkf_s, carryk_s, OFF_K), (dvf_s, carryv_s, OFF_V)):
            acc[t:t + BLOCK, :] += carry[...]
            carry[...] = acc[0:BLOCK, :]
            d = acc[BLOCK:, :]
            dproj_ref[:, off:off + BLOCK] = d.astype(act)
            acc_bin[:, off:off + BLOCK] += _rows8(d)

        dh = _mm(dproj_ref[...], wint_v[...])
        dprojt_ref[...] = dproj_ref[...].T
        mixedt_ref[...] = mixed_ref[...].T
        for c in range(nb):
            rows = slice(c * BLOCK, (c + 1) * BLOCK)
            r1 = r1_s[rows, :]
            xn = x_ref[rows, :] * r1
            dhc = dh[rows, :]
            acc_g1[...] += _rows8(dhc * xn)
            dxn = dhc * g1
            dx_ref[rows, :] = doutf_s[rows, :] + r1 * (dxn - xn * _mean_rows(dxn * xn))

        @pl.when(i == nt - 1)
        def _():
            tril = lax.broadcasted_iota(jnp.int32, (BLOCK, BLOCK), 0) >= lane128
            for hh in range(N_SGU_HEADS):
                rws = slice(hh * BLOCK, (hh + 1) * BLOCK)
                sw_ref[rws, :] = jnp.where(tril, acc_dws[rws, :], 0.0)
            vec_ref[...] = jnp.zeros((VEC_ROWS, 128), F32)

            def put(row0, acc):
                s = jnp.sum(acc[...], axis=0, keepdims=True)
                for k in range(acc.shape[1] // 128):
                    vec_ref[row0 + k:row0 + k + 1, :] = s[:, k * 128:(k + 1) * 128]

            put(R_G1, acc_g1)
            put(R_BIN, acc_bin)
            put(R_LNG, acc_lng)
            put(R_LNB, acc_lnb)
            put(R_BOUT, acc_bout)
            put(R_G3, acc_g3)
            vec_ref[R_SINK:R_SINK + 1, :] = jnp.sum(
                jnp.where(lax.broadcasted_iota(jnp.int32, (8, 128), 0) == lax.broadcasted_iota(jnp.int32, (8, 128), 1),
                          acc_sink[...], 0.0), axis=0, keepdims=True)
            vec_ref[R_LOSS:R_LOSS + 1, :] = jnp.zeros((1, 128), F32) + jnp.sum(acc_loss[...])
            dbs_t = acc_dbs[...].T
            vec_ref[R_SGUB:R_SGUB + 8, :] = jnp.sum(dbs_t.reshape(N_SGU_HEADS, SGU_W // N_SGU_HEADS, BLOCK), axis=1)

    full = lambda shape: pl.BlockSpec(shape, lambda i: (0,) * len(shape))
    tok = lambda w: pl.BlockSpec((t, w), lambda i: (nt - 1 - i, 0))
    in_specs = [
        pl.BlockSpec(memory_space=pltpu.SMEM),
        tok(D_MODEL),
        pl.BlockSpec((BLOCK, D_MODEL), lambda i: (jnp.maximum((nt - 1 - i) * nb - 1, 0), 0)),
        tok(D_MODEL),
        full((1, D_MODEL)), full((1, IN_W)), full((1, SGU_W)), full((1, SGU_W)),
        full((N_SGU_HEADS, BLOCK, BLOCK)), full((BLOCK, SGU_W)), full((1, D_MODEL)), full((1, D_MODEL)),
        pl.BlockSpec(memory_space=pl.ANY), pl.BlockSpec(memory_space=pl.ANY),
    ]
    out_shape = [
        jax.ShapeDtypeStruct((seq, D_MODEL), F32),
        jax.ShapeDtypeStruct((seq, D_MODEL), act),
        jax.ShapeDtypeStruct((IN_W, seq), act),
        jax.ShapeDtypeStruct((D_MODEL, seq), act),
        jax.ShapeDtypeStruct((seq, D_MODEL), act),
        jax.ShapeDtypeStruct((SW_ROWS, 128), F32),
        jax.ShapeDtypeStruct((VEC_ROWS, 128), F32),
    ]
    tok_t = lambda w: pl.BlockSpec((w, t), lambda i: (0, nt - 1 - i))
    out_specs = [tok(D_MODEL), tok(D_MODEL), tok_t(IN_W), tok_t(D_MODEL), tok(D_MODEL), full((SW_ROWS, 128)), full((VEC_ROWS, 128))]
    vm = pltpu.VMEM
    scratch = [
        vm((t, IN_W), act), vm((t, D_MODEL), act),
        vm((IN_W, D_MODEL), act), vm((D_MODEL, D_MODEL), act),
        vm((4, BLOCK, 2 * BLOCK), act), vm((4, BLOCK, 2 * BLOCK), act),
        vm((t, ATTN_W), act),
        vm((t + BLOCK, BLOCK), F32), vm((t + BLOCK, BLOCK), F32),
        vm((2, t + BLOCK, BLOCK), act), vm((2, t + BLOCK, BLOCK), act),
        vm((4, t, 512), F32),
        vm((2 * nb, 4 * BLOCK, 2 * BLOCK), F32), vm((2 * nb, 4 * BLOCK, 1), F32),
        vm((t, ATTN_W), F32), vm((t, SGU_W), F32), vm((t, SGU_W), F32), vm((t, SGU_W), F32), vm((t, 1), F32),
        vm((t, SGU_W), F32), vm((t, SGU_W), F32), vm((t, 1), F32),
        vm((t, D_MODEL), F32), vm((t, D_MODEL), F32),
        vm((t + BLOCK, BLOCK), F32), vm((t + BLOCK, BLOCK), F32), vm((BLOCK, BLOCK), F32), vm((BLOCK, BLOCK), F32),
        vm((8, IN_W), F32), vm((8, D_MODEL), F32), vm((8, D_MODEL), F32), vm((8, D_MODEL), F32),
        vm((8, SGU_W), F32), vm((8, SGU_W), F32), vm((N_SGU_HEADS * BLOCK, BLOCK), F32), vm((BLOCK, SGU_W), F32),
        vm((8, 128), F32), vm((8, 128), F32),
        vm((t + BLOCK, D_MODEL), act), vm((4, 2 * BLOCK, t), act), vm((4, BLOCK, t), act), vm((t, SGU_W), F32),
        pltpu.SemaphoreType.DMA((2,)),
    ]
    return pl.pallas_call(
        body, name="fused", grid=(nt,), in_specs=in_specs, out_specs=out_specs, out_shape=out_shape,
        scratch_shapes=scratch,
        compiler_params=pltpu.CompilerParams(dimension_semantics=("arbitrary",), vmem_limit_bytes=VMEM_LIMIT),
    )(sinks, x, x, tgt, g1, b_in, ln_g, ln_b, sgu_w, bexp, b_out, g3, win_t, wout)


def _gather_call(a32, b32):
    act = MXU_DTYPE

    def body(a32_hbm, b32_hbm, ga_hbm, gb_hbm, a32_v, b32_v, ga_v, gb_v, send_sems, recv_sems, loc_sems):
        x, y, c = lax.axis_index("x"), lax.axis_index("y"), lax.axis_index("c")
        me, sibling, x_nbr, y_nbr = (x, y, c), (x, y, 1 - c), (1 - x, y, c), (x, 1 - y, c)
        j, j_x, j_y, j_d = 2 * x + y, 2 * (1 - x) + y, 2 * x + (1 - y), 2 * (1 - x) + (1 - y)
        arrays = ((0, ga_v, ga_hbm, W_IN_SHARD), (1, gb_v, gb_hbm, W_OUT_SHARD))

        loads = [pltpu.make_async_copy(a32_hbm, a32_v, loc_sems.at[0]), pltpu.make_async_copy(b32_hbm, b32_v, loc_sems.at[1])]
        for cp in loads:
            cp.start()
        for cp in loads:
            cp.wait()
        ga_v[pl.ds(pl.multiple_of(j * W_IN_SHARD, 16), W_IN_SHARD), :] = a32_v[...].astype(act)
        gb_v[pl.ds(pl.multiple_of(j * W_OUT_SHARD, 16), W_OUT_SHARD), :] = b32_v[...].astype(act)

        def rows(shard, chip, hf=None, q=None):
            if hf is None:
                return pl.ds(pl.multiple_of(chip * shard, 16), shard)
            return pl.ds(pl.multiple_of(chip * shard + hf * (shard // 2) + q * (shard // 4), 16), shard // 4)

        def copy(k, ref, at, to):
            return pltpu.make_async_remote_copy(src_ref=ref.at[at, :], dst_ref=ref.at[at, :], send_sem=send_sems.at[k],
                                                recv_sem=recv_sems.at[k], device_id=to, device_id_type=MESH)

        stores = []

        def store(vmem, hbm, at):
            stores.append(pltpu.make_async_copy(vmem.at[at, :], hbm.at[at, :], loc_sems.at[len(stores)]))
            stores[-1].start()

        sent = []
        for w, vmem, hbm, shard in arrays:
            for k, (q, to) in enumerate(((0, x_nbr), (1, x_nbr), (1, y_nbr), (0, y_nbr))):
                sent.append(copy(12 * w + k, vmem, rows(shard, j, c, q), to))
                sent[-1].start()
        for w, vmem, hbm, shard in arrays:
            store(vmem, hbm, rows(shard, j))

        def landed(w, vmem, hbm, shard, k, chip, q, onward=None):
            at = rows(shard, chip, c, q)
            copy(12 * w + k, vmem, at, me).wait_recv()
            if onward is not None:
                sent.append(copy(12 * w + onward[0], vmem, at, onward[1]))
                sent[-1].start()
            sent.append(copy(12 * w + 6 + k, vmem, at, sibling))
            sent[-1].start()
            store(vmem, hbm, at)

        for arr in arrays:
            landed(*arr, 0, j_x, 0, onward=(4, y_nbr))
            landed(*arr, 2, j_y, 1, onward=(5, x_nbr))
        for arr in arrays:
            landed(*arr, 1, j_x, 1)
            landed(*arr, 3, j_y, 0)
            landed(*arr, 4, j_d, 0)
            landed(*arr, 5, j_d, 1)
        for w, vmem, hbm, shard in arrays:
            for k, (chip, q) in enumerate(((j_x, 0), (j_x, 1), (j_y, 1), (j_y, 0), (j_d, 0), (j_d, 1))):
                at = rows(shard, chip, 1 - c, q)
                copy(12 * w + 6 + k, vmem, at, me).wait_recv()
                store(vmem, hbm, at)
        for cp in sent:
            cp.wait_send()
        for cp in stores:
            cp.wait()

    hbm = pl.BlockSpec(memory_space=pl.ANY)
    vm = pltpu.VMEM
    return pl.pallas_call(
        body, name="gather", in_specs=[hbm, hbm], out_specs=[hbm, hbm],
        out_shape=[jax.ShapeDtypeStruct((IN_W, D_MODEL), act), jax.ShapeDtypeStruct((D_MODEL, D_MODEL), act)],
        scratch_shapes=[vm(a32.shape, F32), vm(b32.shape, F32), vm((IN_W, D_MODEL), act), vm((D_MODEL, D_MODEL), act),
                        pltpu.SemaphoreType.DMA((24,)), pltpu.SemaphoreType.DMA((24,)), pltpu.SemaphoreType.DMA((26,))],
        compiler_params=pltpu.CompilerParams(vmem_limit_bytes=VMEM_LIMIT),
    )(a32, b32)


def _wgrad_reduce_call(dproj_t, h, mixed_t, dout, sw, vec):
    wire = jnp.bfloat16
    half_sw = SW_ROWS // 2
    seq = h.shape[0]
    tk = min(2048, seq)
    nk = seq // tk
    n_steps = 2 * N_CHIPS
    rel_of = lambda s: s % 3 if s < 6 else 3
    half_of = lambda s: s // 3 if s < 6 else s - 6
    x, y = lax.axis_index("x"), lax.axis_index("y")
    chip_of = [2 * (1 - x) + (1 - y), 2 * (1 - x) + y, 2 * x + (1 - y), 2 * x + y]
    order = jnp.stack([2 * chip_of[rel_of(s)] + half_of(s) for s in range(n_steps)]).astype(jnp.int32)

    def body(order_ref, dpt_ref, h_hbm, mxt_ref, dout_hbm, sw_ref, small_ref, oa_ref, ob_ref, osw_ref, osmall_ref,
             h_v, dout_v, acc_a, acc_b, sib_a, sib_b, snd_a, snd_b, in_a, in_b, own_a, own_b, fin_a, fin_b,
             all_small, sw_sib, sw_chips, sw_fin, send_sems, recv_sems, loc_sems):
        x, y, c = lax.axis_index("x"), lax.axis_index("y"), lax.axis_index("c")
        me, sibling = (x, y, c), (x, y, 1 - c)
        steps = [(1 - x, 1 - y), (1 - x, y), (x, 1 - y)]
        j = 2 * x + y
        dev = 4 * x + 2 * y + c
        b, k = pl.program_id(0), pl.program_id(1)

        def copy(n, src, dst, to):
            return pltpu.make_async_remote_copy(src_ref=src, dst_ref=dst, send_sem=send_sems.at[n], recv_sem=recv_sems.at[n],
                                                device_id=to, device_id_type=MESH)

        def sw_rows(ref, hf):
            return ref.at[pl.ds(pl.multiple_of(hf * half_sw, 8), half_sw), :]

        sm_first = [copy(16, small_ref, all_small.at[dev], sibling)]
        sm_first += [copy(17 + r, small_ref, all_small.at[dev], (*chip, c)) for r, chip in enumerate(steps)]
        sw_to_sib = copy(23, sw_rows(sw_ref, 1 - c), sw_sib, sibling)

        @pl.when((b == 0) & (k == 0))
        def _():
            all_small[dev] = small_ref[...]
            for cp in sm_first + [sw_to_sib]:
                cp.start()

        def load(kk):
            rows = pl.ds(kk * tk, tk)
            return [pltpu.make_async_copy(h_hbm.at[rows, :], h_v.at[rows, :], loc_sems.at[kk]),
                    pltpu.make_async_copy(dout_hbm.at[rows, :], dout_v.at[rows, :], loc_sems.at[nk + kk])]

        for kk in range(nk):
            @pl.when((b == 0) & (k == 0))
            def _():
                for cp in load(kk):
                    cp.start()

        for kk in range(nk):
            @pl.when((b == 0) & (k == kk))
            def _():
                for cp in load(kk):
                    cp.wait()

        tok = pl.ds(pl.multiple_of(k * tk, tk), tk)
        pa = _mm(dpt_ref[...], h_v[tok, :])
        pb = _mm(mxt_ref[...], dout_v[tok, :])
        slot = b % 2

        @pl.when(k == 0)
        def _():
            acc_a[slot] = pa
            acc_b[slot] = pb

        @pl.when(k != 0)
        def _():
            acc_a[slot] += pa
            acc_b[slot] += pb

        def to_sibling(s):
            r = rel_of(s)
            return [copy(r, acc_a.at[s % 2], sib_a.at[r], sibling), copy(4 + r, acc_b.at[s % 2], sib_b.at[r], sibling)]

        def chip_partial(s):
            r = rel_of(s)
            copy(r, sib_a.at[r], sib_a.at[r], me).wait_recv()
            copy(4 + r, sib_b.at[r], sib_b.at[r], me).wait_recv()
            return acc_a[s % 2] + sib_a[r], acc_b[s % 2] + sib_b[r]

        def to_owner(r):
            return [copy(8 + r, snd_a.at[r], in_a.at[r], (*steps[r], c)), copy(11 + r, snd_b.at[r], in_b.at[r], (*steps[r], c))]

        def vec_forwards():
            return [copy(20 + r, all_small.at[4 * cx + 2 * cy + c], all_small.at[4 * cx + 2 * cy + c], sibling)
                    for r, (cx, cy) in enumerate(steps)]

        def sw_to_chips():
            return [copy(24 + r, sw_chips.at[j], sw_chips.at[j], (*chip, c)) for r, chip in enumerate(steps)]

        def sw_to_sibling():
            return copy(27, sw_rows(osw_ref, c), sw_fin, sibling)

        @pl.when((b == 1) & (k == 0))
        def _():
            for r, (cx, cy) in enumerate(steps):
                d = 4 * cx + 2 * cy + c
                copy(17 + r, all_small.at[d], all_small.at[d], me).wait_recv()
            for cp in vec_forwards():
                cp.start()
            copy(23, sw_sib, sw_sib, me).wait_recv()
            sw_chips[j] = (sw_rows(sw_ref, c)[...] + sw_sib[...]).astype(wire)
            for cp in sw_to_chips():
                cp.start()

        @pl.when((b == 3) & (k == 0))
        def _():
            for r, (cx, cy) in enumerate(steps):
                cj = 2 * cx + cy
                copy(24 + r, sw_chips.at[cj], sw_chips.at[cj], me).wait_recv()
            tot_sw = sw_chips[0].astype(F32)
            for q in range(1, N_CHIPS):
                tot_sw = tot_sw + sw_chips[q].astype(F32)
            sw_rows(osw_ref, c)[...] = tot_sw
            sw_to_sibling().start()

        late = min(1, nk - 1)
        for s in range(n_steps):
            if s >= 1:
                sp = s - 1

                @pl.when((b == s) & (k == late) & (c == half_of(sp)))
                def _():
                    ta, tb = chip_partial(sp)
                    r = rel_of(sp)
                    if r < 3:
                        snd_a[r] = ta.astype(wire)
                        snd_b[r] = tb.astype(wire)
                        for cp in to_owner(r):
                            cp.start()
                    else:
                        own_a[...] = ta
                        own_b[...] = tb

                @pl.when((b == s) & (k == nk - 1) & (c != half_of(sp)))
                def _():
                    for cp in to_sibling(sp):
                        cp.wait_send()

            @pl.when((b == s) & (k == nk - 1) & (c != half_of(s)))
            def _():
                for cp in to_sibling(s):
                    cp.start()

        @pl.when((b == n_steps - 1) & (k == nk - 1))
        def _():
            last = n_steps - 1

            @pl.when(c == half_of(last))
            def _():
                own_a[...], own_b[...] = chip_partial(last)

            @pl.when(c != half_of(last))
            def _():
                for cp in to_sibling(last):
                    cp.wait_send()

            tot_a = own_a[...]
            tot_b = own_b[...]
            for s in range(3):
                copy(8 + s, in_a.at[s], in_a.at[s], me).wait_recv()
                copy(11 + s, in_b.at[s], in_b.at[s], me).wait_recv()
                tot_a = tot_a + in_a[s].astype(F32)
                tot_b = tot_b + in_b[s].astype(F32)
            mine_a = oa_ref.at[pl.ds(pl.multiple_of(c * HALF_A, 8), HALF_A), :]
            mine_b = ob_ref.at[pl.ds(pl.multiple_of(c * HALF_B, 8), HALF_B), :]
            mine_a[...] = tot_a
            mine_b[...] = tot_b
            back = [copy(14, mine_a, fin_a, sibling), copy(15, mine_b, fin_b, sibling)]
            for cp in back:
                cp.start()

            copy(16, small_ref, all_small.at[dev ^ 1], me).wait_recv()
            for r, (cx, cy) in enumerate(steps):
                d = 4 * cx + 2 * cy + (1 - c)
                copy(20 + r, all_small.at[d], all_small.at[d], me).wait_recv()
            tot = all_small[0]
            for d in range(1, N_DEV):
                tot = tot + all_small[d]
            osmall_ref[...] = tot

            copy(14, fin_a, fin_a, me).wait_recv()
            copy(15, fin_b, fin_b, me).wait_recv()
            copy(27, sw_fin, sw_fin, me).wait_recv()
            oa_ref[pl.ds(pl.multiple_of((1 - c) * HALF_A, 8), HALF_A), :] = fin_a[...]
            ob_ref[pl.ds(pl.multiple_of((1 - c) * HALF_B, 8), HALF_B), :] = fin_b[...]
            sw_rows(osw_ref, 1 - c)[...] = sw_fin[...]
            sends = sm_first + vec_forwards() + sw_to_chips() + back + [sw_to_sib, sw_to_sibling()]
            for s in range(3):
                sends += to_owner(s)
            for cp in sends:
                cp.wait_send()

    vmem = pl.BlockSpec(memory_space=pltpu.VMEM)
    vm = pltpu.VMEM
    grid_spec = pltpu.PrefetchScalarGridSpec(
        num_scalar_prefetch=1, grid=(n_steps, nk),
        in_specs=[pl.BlockSpec((HALF_A, tk), lambda b, k, o: (o[b], k)), pl.BlockSpec(memory_space=pl.ANY),
                  pl.BlockSpec((HALF_B, tk), lambda b, k, o: (o[b], k)), pl.BlockSpec(memory_space=pl.ANY),
                  vmem, vmem],
        out_specs=[vmem, vmem, vmem, vmem],
        scratch_shapes=[vm((seq, D_MODEL), h.dtype), vm((seq, D_MODEL), dout.dtype),
                        vm((2, HALF_A, D_MODEL), F32), vm((2, HALF_B, D_MODEL), F32),
                        vm((N_CHIPS, HALF_A, D_MODEL), F32), vm((N_CHIPS, HALF_B, D_MODEL), F32),
                        vm((3, HALF_A, D_MODEL), wire), vm((3, HALF_B, D_MODEL), wire),
                        vm((3, HALF_A, D_MODEL), wire), vm((3, HALF_B, D_MODEL), wire),
                        vm((HALF_A, D_MODEL), F32), vm((HALF_B, D_MODEL), F32),
                        vm((HALF_A, D_MODEL), F32), vm((HALF_B, D_MODEL), F32),
                        vm((N_DEV, VEC_ROWS, 128), F32), vm((half_sw, 128), F32), vm((N_CHIPS, half_sw, 128), wire),
                        vm((half_sw, 128), F32),
                        pltpu.SemaphoreType.DMA((28,)), pltpu.SemaphoreType.DMA((28,)), pltpu.SemaphoreType.DMA((2 * nk,))])
    return pl.pallas_call(
        body, name="wgrad_reduce", grid_spec=grid_spec,
        out_shape=[jax.ShapeDtypeStruct((W_IN_SHARD, D_MODEL), F32), jax.ShapeDtypeStruct((W_OUT_SHARD, D_MODEL), F32),
                   jax.ShapeDtypeStruct((SW_ROWS, 128), F32), jax.ShapeDtypeStruct((VEC_ROWS, 128), F32)],
        compiler_params=pltpu.CompilerParams(dimension_semantics=("arbitrary", "arbitrary"), vmem_limit_bytes=VMEM_LIMIT),
    )(order, dproj_t, h, mixed_t, dout, sw, vec)


def _adamw(w, g, m, v):
    nm = ADAM_B1 * m + (1.0 - ADAM_B1) * g
    nv = ADAM_B2 * v + (1.0 - ADAM_B2) * (g * g)
    m_hat = nm / (1.0 - ADAM_B1 ** ADAM_STEP)
    v_hat = nv / (1.0 - ADAM_B2 ** ADAM_STEP)
    return -ADAM_LR * (m_hat / (jnp.sqrt(v_hat) + ADAM_EPS) + ADAM_WD * w), nm, nv


def _adamw_shards_call(a, b, steps=4):
    def body(*refs):
        for k in range(2):
            w_ref, g_ref, m_ref, v_ref = refs[4 * k:4 * k + 4]
            go_ref, d_ref, nm_ref, nv_ref = refs[8 + 4 * k:12 + 4 * k]
            gg = g_ref[...]
            go_ref[...] = gg
            d_ref[...], nm_ref[...], nv_ref[...] = _adamw(w_ref[...], gg, m_ref[...], v_ref[...])

    specs, shapes = [], []
    for w in (a[0], b[0]):
        rows, cols = w.shape
        specs += [pl.BlockSpec((rows // steps, cols), lambda i: (i, 0))] * 4
        shapes += [jax.ShapeDtypeStruct((rows, cols), F32)] * 4
    return pl.pallas_call(
        body, name="adamw_shards", grid=(steps,), in_specs=specs, out_specs=specs, out_shape=shapes,
        compiler_params=pltpu.CompilerParams(dimension_semantics=("arbitrary",)),
    )(*a, *b)


_SMALL = (("norm_g", (1, D_MODEL), R_G1), ("b_in", (1, IN_W), R_BIN), ("attn_sinks", (1, 8), R_SINK),
          ("sgu_ln_g", (1, SGU_W), R_LNG), ("sgu_ln_b", (1, SGU_W), R_LNB), ("sgu_b", (N_SGU_HEADS, BLOCK), R_SGUB),
          ("b_out", (1, D_MODEL), R_BOUT), ("final_norm_g", (1, D_MODEL), R_G3))


def _adamw_small_call(sw_g, vec_g, sgu_w3, ws, ms, vs):
    n = len(_SMALL)

    def body(*refs):
        sw_ref, vec_ref = refs[0], refs[1]
        w3 = refs[2:5]
        w_refs, m_refs, v_refs = refs[5:5 + n], refs[5 + n:5 + 2 * n], refs[5 + 2 * n:5 + 3 * n]
        outs = refs[5 + 3 * n:]
        outs[0][...] = vec_ref[R_LOSS:R_LOSS + 1, 0:1]
        g = sw_ref[...]
        outs[1][...] = g
        outs[2][...], outs[3][...], outs[4][...] = _adamw(w3[0][...], g, w3[1][...], w3[2][...])
        for k, (_, shape, row) in enumerate(_SMALL):
            if shape[0] == 1 and shape[1] >= 128:
                g = jnp.concatenate([vec_ref[row + q:row + q + 1, :] for q in range(shape[1] // 128)], axis=1)
            else:
                g = vec_ref[row:row + shape[0], 0:shape[1]]
            o = outs[5 + 4 * k:9 + 4 * k]
            o[0][...] = g
            o[1][...], o[2][...], o[3][...] = _adamw(w_refs[k][...], g, m_refs[k][...], v_refs[k][...])

    vmem = pl.BlockSpec(memory_space=pltpu.VMEM)
    out_shape = [jax.ShapeDtypeStruct((1, 1), F32)] + [jax.ShapeDtypeStruct((SW_ROWS, 128), F32)] * 4
    for _, shape, _ in _SMALL:
        out_shape += [jax.ShapeDtypeStruct(shape, F32)] * 4
    args = [sw_g, vec_g, *sgu_w3, *ws, *ms, *vs]
    return pl.pallas_call(
        body, name="adamw_small", in_specs=[vmem] * len(args), out_specs=[vmem] * len(out_shape), out_shape=out_shape,
    )(*args)


def kernel(x, norm_g, w_in, b_in, attn_sinks, sgu_ln_g, sgu_ln_b, sgu_w, sgu_b, w_out, b_out, final_norm_g, loss_target, m_norm_g, m_w_in, m_b_in, m_attn_sinks, m_sgu_ln_g, m_sgu_ln_b, m_sgu_w, m_sgu_b, m_w_out, m_b_out, m_final_norm_g, v_norm_g, v_w_in, v_b_in, v_attn_sinks, v_sgu_ln_g, v_sgu_ln_b, v_sgu_w, v_sgu_b, v_w_out, v_b_out, v_final_norm_g):
    seq = x.shape[1]
    win_t, wout = _gather_call(w_in[0].T, w_out[0])
    bexp = jnp.repeat(sgu_b[0].T, SGU_W // N_SGU_HEADS, axis=1)
    dx, h, dproj_t, mixed_t, dout, sw, vec = _fused_call(
        x[0], loss_target[0], attn_sinks[0], norm_g, b_in, sgu_ln_g, sgu_ln_b, sgu_w[0], bexp, b_out,
        final_norm_g.reshape(1, D_MODEL), win_t, wout)
    ga_t, g_w_out, sw, vec = _wgrad_reduce_call(dproj_t, h, mixed_t, dout, sw, vec)

    names = ["norm_g", "w_in", "b_in", "attn_sinks", "sgu_ln_g", "sgu_ln_b", "sgu_w", "sgu_b", "w_out", "b_out", "final_norm_g"]
    res = {}
    shards = _adamw_shards_call((w_in[0].T, ga_t, m_w_in[0].T, v_w_in[0].T), (w_out[0], g_w_out, m_w_out[0], v_w_out[0]))
    res["w_in"] = [a.T[None] for a in shards[:4]]
    res["w_out"] = [a[None] for a in shards[4:]]
    given = dict(norm_g=(norm_g, m_norm_g, v_norm_g), b_in=(b_in, m_b_in, v_b_in), attn_sinks=(attn_sinks, m_attn_sinks, v_attn_sinks),
                 sgu_ln_g=(sgu_ln_g, m_sgu_ln_g, v_sgu_ln_g), sgu_ln_b=(sgu_ln_b, m_sgu_ln_b, v_sgu_ln_b),
                 sgu_b=(sgu_b, m_sgu_b, v_sgu_b), b_out=(b_out, m_b_out, v_b_out),
                 final_norm_g=(final_norm_g, m_final_norm_g, v_final_norm_g))
    wmv = [[given[n][k].reshape(shape) for n, shape, _ in _SMALL] for k in range(3)]
    outs = _adamw_small_call(sw, vec, [a.reshape(SW_ROWS, BLOCK) for a in (sgu_w, m_sgu_w, v_sgu_w)], *wmv)
    loss = outs[0].reshape(())
    res["sgu_w"] = [a.reshape(sgu_w.shape) for a in outs[1:5]]
    for k, (n, _, _) in enumerate(_SMALL):
        res[n] = [a.reshape(given[n][0].shape) for a in outs[5 + 4 * k:9 + 4 * k]]

    return (loss, dx[None], *[res[n][0] for n in names], *[res[n][1] for n in names], *[res[n][2] for n in names],
            *[res[n][3] for n in names])
```

```python
import functools
import math

import jax
import jax.numpy as jnp
from jax import lax
from jax.experimental import pallas as pl
from jax.experimental.pallas import tpu as pltpu

F32 = jnp.float32
MXU_DTYPE = jnp.bfloat16

D_MODEL = 1024
HEAD_DIM = 64
ATTN_W = 512
SGU_W = 512
N_SGU_HEADS = 8
BLOCK = 128
IN_W = 2816
OFF_K, OFF_V, OFF_ZA, OFF_US, OFF_VS, OFF_ZS = 512, 640, 768, 1280, 1792, 2304
NORM_EPS = 1e-5
NEG_INF = -1e30
SCALE = HEAD_DIM ** -0.5
SQRT_HALF = math.sqrt(0.5)
INV_SQRT_2PI = 1.0 / math.sqrt(2.0 * math.pi)

N_CHIPS = 4
N_DEV = 8
W_IN_SHARD = IN_W // N_CHIPS
W_OUT_SHARD = D_MODEL // N_CHIPS
HALF_A = W_IN_SHARD // 2
HALF_B = W_OUT_SHARD // 2

TILE = 256
VMEM_LIMIT = 56 * 1024 * 1024

ADAM_LR, ADAM_B1, ADAM_B2, ADAM_EPS, ADAM_WD, ADAM_STEP = 0.001, 0.9, 0.999, 1e-08, 0.01, 10

SW_ROWS = N_SGU_HEADS * BLOCK
R_G1, R_BIN, R_SINK, R_LOSS, R_LNG, R_LNB, R_SGUB, R_BOUT, R_G3 = 0, 8, 32, 40, 48, 56, 64, 72, 80
VEC_ROWS = 88

MESH = pl.DeviceIdType.MESH


def _mm(a, b):
    return jnp.dot(a, b, preferred_element_type=F32)


def _mm_nt(a, b):
    return lax.dot_general(a, b, (((1,), (1,)), ((), ())), preferred_element_type=F32)


def _mm_tn(a, b):
    return lax.dot_general(a, b, (((0,), (0,)), ((), ())), preferred_element_type=F32)


def _sigmoid(z):
    return 1.0 / (1.0 + jnp.exp(-z))


def _norm_cdf(z):
    return 0.5 * (1.0 + lax.erf(z * SQRT_HALF))


def _norm_pdf(z):
    return jnp.exp(-0.5 * z * z) * INV_SQRT_2PI


def _rows8(v):
    r, n = v.shape
    return jnp.sum(v.reshape(r // 8, 8, n), axis=0)


def _mean_rows(v):
    return jnp.sum(v, axis=1, keepdims=True) * (1.0 / v.shape[1])


def _fused_call(x, tgt, sinks, g1, b_in, ln_g, ln_b, sgu_w, bexp, b_out, g3, win_t, wout):
    seq = x.shape[0]
    t = TILE
    nt = seq // t
    nb = t // BLOCK
    act = MXU_DTYPE

    def body(sinks_ref, x_ref, xh_ref, tgt_ref, g1_ref, bin_ref, lng_ref, lnb_ref, sguw_ref, bexp_ref, bout_ref, g3_ref,
             wint_hbm, wout_hbm,
             dx_ref, h_ref, dprojt_ref, mixedt_ref, dout_ref, sw_ref, vec_ref,
             dproj_ref, mixed_ref, wint_v, wout_v, wf_v, wb_v, q_s, kf_s, vf_s, k2_s, v2_s, gate_s, p_s, ps_s, o_s, u_s, mix_s, vhat_s, r2_s,
             cdfu_s, cdfv_s, r1_s, doutf_s, dmix_s, dkf_s, dvf_s, carryk_s, carryv_s,
             acc_bin, acc_g1, acc_bout, acc_g3, acc_lng, acc_lnb, acc_dws, acc_dbs, acc_sink, acc_loss,
             h_s, rhs_s, vlnp_s, dvln_s, sems):
        i = pl.program_id(0)
        tile = nt - 1 - i
        lane128 = lax.broadcasted_iota(jnp.int32, (BLOCK, BLOCK), 1)
        lo = lane128 < HEAD_DIM

        @pl.when(i == 0)
        def _():
            cp_a = pltpu.make_async_copy(wint_hbm, wint_v, sems.at[0])
            cp_b = pltpu.make_async_copy(wout_hbm, wout_v, sems.at[1])
            cp_a.start()
            cp_b.start()
            for acc in (acc_bin, acc_g1, acc_bout, acc_g3, acc_lng, acc_lnb, acc_dws, acc_dbs, acc_sink, acc_loss,
                        carryk_s, carryv_s):
                acc[...] = jnp.zeros(acc.shape, F32)
            tril = lax.broadcasted_iota(jnp.int32, (BLOCK, BLOCK), 0) >= lane128
            for h in range(N_SGU_HEADS):
                w = jnp.where(tril, sguw_ref[h], 0.0)
                wf_v[h // 2, :, (h % 2) * BLOCK:(h % 2 + 1) * BLOCK] = w.astype(act)
                wb_v[h // 2, :, (h % 2) * BLOCK:(h % 2 + 1) * BLOCK] = w.T.astype(act)
            cp_a.wait()
            cp_b.wait()

        g1 = g1_ref[...]

        def rms(v):
            r = lax.rsqrt(_mean_rows(v * v) + NORM_EPS)
            return r, v * r

        _, xnh = rms(xh_ref[...])
        h_s[0:BLOCK, :] = (xnh * g1).astype(act)
        for c in range(nb):
            rows = slice(c * BLOCK, (c + 1) * BLOCK)
            r, xn = rms(x_ref[rows, :])
            r1_s[rows, :] = r
            h_s[BLOCK + c * BLOCK:BLOCK + (c + 1) * BLOCK, :] = (xn * g1).astype(act)

        h = h_s[BLOCK:, :]
        h_ref[...] = h
        q = _mm_nt(h, wint_v[0:OFF_K, :]) + bin_ref[:, 0:OFF_K]
        q_s[...] = (q * SCALE).astype(act)
        kv = _mm_nt(h_s[...], wint_v[OFF_K:OFF_ZA, :]) + bin_ref[:, OFF_K:OFF_ZA]
        kf_s[...] = kv[:, :BLOCK]
        vf_s[...] = kv[:, BLOCK:]
        for r in range(4):
            cols = slice(OFF_ZA + r * 512, OFF_ZA + (r + 1) * 512)
            gate_s[r] = _mm_nt(h, wint_v[cols, :]) + bin_ref[:, cols]

        lo_kv = lax.broadcasted_iota(jnp.int32, (t + BLOCK, BLOCK), 1) < HEAD_DIM
        for src, dst in ((kf_s, k2_s), (vf_s, v2_s)):
            v = src[...]
            vr = pltpu.roll(v, HEAD_DIM, 1)
            dst[0] = jnp.where(lo_kv, v, vr).astype(act)
            dst[1] = jnp.where(lo_kv, vr, v).astype(act)

        rowi = lax.broadcasted_iota(jnp.int32, (BLOCK, 2 * BLOCK), 0)
        colj = lax.broadcasted_iota(jnp.int32, (BLOCK, 2 * BLOCK), 1)
        in_band = (colj > rowi) & (colj <= rowi + BLOCK)
        row512 = lax.broadcasted_iota(jnp.int32, (4 * BLOCK, 1), 0)

        def stacked_q(b, g):
            parts = []
            for p in range(2):
                slab = q_s[b * BLOCK:(b + 1) * BLOCK, (2 * g + p) * BLOCK:(2 * g + p + 1) * BLOCK]
                parts += [jnp.where(lo, slab, jnp.zeros_like(slab)), jnp.where(lo, jnp.zeros_like(slab), slab)]
            return jnp.concatenate(parts, axis=0)

        def sink_col(g):
            s = [sinks_ref[4 * g + k] for k in range(4)]
            return jnp.where(row512 < BLOCK, s[0], jnp.where(row512 < 2 * BLOCK, s[1], jnp.where(row512 < 3 * BLOCK, s[2], s[3])))

        for b in range(nb):
            band = slice(b * BLOCK, (b + 2) * BLOCK)
            first_key = jnp.where(tile * nb + b > 0, 0, BLOCK)
            valid = in_band & (colj >= first_key)
            valid4 = jnp.concatenate([valid] * 4, axis=0)
            for g in range(2):
                s = _mm_nt(stacked_q(b, g), k2_s[g, band, :])
                s = jnp.where(valid4, s, NEG_INF)
                sk = sink_col(g)
                m = jnp.maximum(jnp.max(s, axis=1, keepdims=True), sk)
                p = jnp.exp(s - m)
                psk = jnp.exp(sk - m)
                inv = 1.0 / (jnp.sum(p, axis=1, keepdims=True) + psk)
                p = p * inv
                p_s[b * 2 + g] = p
                ps_s[b * 2 + g] = psk * inv
                o2 = _mm(p.astype(act), v2_s[g, band, :])
                for pr in range(2):
                    o_s[b * BLOCK:(b + 1) * BLOCK, (2 * g + pr) * BLOCK:(2 * g + pr + 1) * BLOCK] = jnp.where(
                        lo, o2[(2 * pr) * BLOCK:(2 * pr + 1) * BLOCK], o2[(2 * pr + 1) * BLOCK:(2 * pr + 2) * BLOCK])

        lng = lng_ref[...]
        lnb = lnb_ref[...]

        def split_pairs(val, c):
            for p in range(4):
                slab = val[:, p * BLOCK:(p + 1) * BLOCK]
                rhs_s[p, 0:BLOCK, c * BLOCK:(c + 1) * BLOCK] = jnp.where(lo, slab, 0.0).astype(act)
                rhs_s[p, BLOCK:, c * BLOCK:(c + 1) * BLOCK] = jnp.where(lo, 0.0, slab).astype(act)

        for c in range(nb):
            rows = slice(c * BLOCK, (c + 1) * BLOCK)
            za = gate_s[0, rows, :]
            mixed_ref[rows, 0:ATTN_W] = (o_s[rows, :] * (za * _sigmoid(za))).astype(act)
            us = gate_s[1, rows, :]
            vs = gate_s[2, rows, :]
            cu = _norm_cdf(us)
            cv = _norm_cdf(vs)
            cdfu_s[rows, :] = cu
            cdfv_s[rows, :] = cv
            u = us * cu
            vg = vs * cv
            vc = vg - _mean_rows(vg)
            r2 = lax.rsqrt(_mean_rows(vc * vc) + NORM_EPS)
            vhat = vc * r2
            r2_s[rows, :] = r2
            vhat_s[rows, :] = vhat
            u_s[rows, :] = u
            split_pairs(vhat * lng + lnb, c)
        for p in range(4):
            cols = slice(p * BLOCK, (p + 1) * BLOCK)
            mix = _mm(wf_v[p], rhs_s[p])
            for c in range(nb):
                mix_s[c * BLOCK:(c + 1) * BLOCK, cols] = mix[:, c * BLOCK:(c + 1) * BLOCK] + bexp_ref[:, cols]
        for c in range(nb):
            rows = slice(c * BLOCK, (c + 1) * BLOCK)
            zs = gate_s[3, rows, :]
            mixed_ref[rows, ATTN_W:] = (u_s[rows, :] * mix_s[rows, :] * (zs * _sigmoid(zs))).astype(act)

        g3 = g3_ref[...]
        proj_o = _mm(mixed_ref[...], wout_v[...])
        for c in range(nb):
            rows = slice(c * BLOCK, (c + 1) * BLOCK)
            out = x_ref[rows, :] + proj_o[rows, :] + bout_ref[...]
            r3, on = rms(out)
            e = on * g3 - tgt_ref[rows, :]
            e2 = _rows8(e * e)
            acc_loss[...] += sum(e2[:, k * 128:(k + 1) * 128] for k in range(D_MODEL // 128)) * (0.5 / D_MODEL)
            dy = e * (1.0 / D_MODEL)
            acc_g3[...] += _rows8(dy * on)
            don = dy * g3
            dout = r3 * (don - on * _mean_rows(don * on))
            doutf_s[rows, :] = dout
            dout_ref[rows, :] = dout.astype(act)
            acc_bout[...] += _rows8(dout)
        dmix_s[...] = _mm_nt(dout_ref[...], wout_v[...])

        for c in range(nb):
            rows = slice(c * BLOCK, (c + 1) * BLOCK)
            dso = dmix_s[rows, ATTN_W:]
            u = u_s[rows, :]
            mix = mix_s[rows, :]
            zs = gate_s[3, rows, :]
            sg = _sigmoid(zs)
            sgs = zs * sg
            du = dso * mix * sgs
            dmx = dso * u * sgs
            dzs = dso * u * mix * (sg * (1.0 + zs * (1.0 - sg)))
            us = gate_s[1, rows, :]
            dus = du * (cdfu_s[rows, :] + us * _norm_pdf(us))
            vln = (vhat_s[rows, :] * lng + lnb).astype(act)
            for p in range(4):
                vlnp_s[p, :, c * BLOCK:(c + 1) * BLOCK] = vln[:, p * BLOCK:(p + 1) * BLOCK]
            split_pairs(dmx, c)
            acc_dbs[...] += dmx
            for off, val in ((OFF_US, dus), (OFF_ZS, dzs)):
                dproj_ref[rows, off:off + 512] = val.astype(act)
                acc_bin[:, off:off + 512] += _rows8(val)
        for p in range(4):
            dvln = _mm(wb_v[p], rhs_s[p])
            for c in range(nb):
                dvln_s[c * BLOCK:(c + 1) * BLOCK, p * BLOCK:(p + 1) * BLOCK] = dvln[:, c * BLOCK:(c + 1) * BLOCK]
            acc_dws[(2 * p) * BLOCK:(2 * p + 1) * BLOCK, :] += _mm_nt(rhs_s[p, 0:BLOCK, :], vlnp_s[p])
            acc_dws[(2 * p + 1) * BLOCK:(2 * p + 2) * BLOCK, :] += _mm_nt(rhs_s[p, BLOCK:, :], vlnp_s[p])
        for c in range(nb):
            rows = slice(c * BLOCK, (c + 1) * BLOCK)
            dvln = dvln_s[rows, :]
            vhat = vhat_s[rows, :]
            acc_lng[...] += _rows8(dvln * vhat)
            acc_lnb[...] += _rows8(dvln)
            dvhat = dvln * lng
            dvg = r2_s[rows, :] * (dvhat - _mean_rows(dvhat) - vhat * _mean_rows(dvhat * vhat))
            vs = gate_s[2, rows, :]
            dvs = dvg * (cdfv_s[rows, :] + vs * _norm_pdf(vs))
            dproj_ref[rows, OFF_VS:OFF_VS + 512] = dvs.astype(act)
            acc_bin[:, OFF_VS:OFF_VS + 512] += _rows8(dvs)

        dkf_s[...] = jnp.zeros(dkf_s.shape, F32)
        dvf_s[...] = jnp.zeros(dvf_s.shape, F32)
        for b in range(nb):
            rows = slice(b * BLOCK, (b + 1) * BLOCK)
            band = slice(b * BLOCK, (b + 2) * BLOCK)
            za = gate_s[0, rows, :]
            sg = _sigmoid(za)
            dao = dmix_s[rows, 0:ATTN_W]
            o = o_s[rows, :]
            do = dao * (za * sg)
            dza = dao * o * (sg * (1.0 + za * (1.0 - sg)))
            dproj_ref[rows, OFF_ZA:OFF_US] = dza.astype(act)
            acc_bin[:, OFF_ZA:OFF_US] += _rows8(dza)
            for g in range(2):
                do_parts, delta_parts = [], []
                for pr in range(2):
                    cols = slice((2 * g + pr) * BLOCK, (2 * g + pr + 1) * BLOCK)
                    d_pair = do[:, cols]
                    prod = d_pair * o[:, cols]
                    do_parts += [jnp.where(lo, d_pair, 0.0).astype(act), jnp.where(lo, 0.0, d_pair).astype(act)]
                    delta_parts += [jnp.sum(jnp.where(lo, prod, 0.0), axis=1, keepdims=True),
                                    jnp.sum(jnp.where(lo, 0.0, prod), axis=1, keepdims=True)]
                do_st = jnp.concatenate(do_parts, axis=0)
                delta = jnp.concatenate(delta_parts, axis=0)
                p = p_s[b * 2 + g]
                dp = _mm_nt(do_st, v2_s[g, band, :])
                ds = p * (dp - delta)
                sink_t = ps_s[b * 2 + g] * delta
                for k in range(4):
                    acc_sink[4 * g + k:4 * g + k + 1, :] += -jnp.sum(sink_t[k * BLOCK:(k + 1) * BLOCK], axis=0, keepdims=True)
                ds_a = ds.astype(act)
                dq2 = _mm(ds_a, k2_s[g, band, :]) * SCALE
                for pr in range(2):
                    cols = slice((2 * g + pr) * BLOCK, (2 * g + pr + 1) * BLOCK)
                    dq = jnp.where(lo, dq2[(2 * pr) * BLOCK:(2 * pr + 1) * BLOCK], dq2[(2 * pr + 1) * BLOCK:(2 * pr + 2) * BLOCK])
                    dproj_ref[rows, cols] = dq.astype(act)
                    acc_bin[:, cols] += _rows8(dq)
                for acc, lhs, rhs in ((dkf_s, stacked_q(b, g), ds_a), (dvf_s, do_st, p.astype(act))):
                    d2t = _mm(lhs.T, rhs)
                    acc[g * HEAD_DIM:(g + 1) * HEAD_DIM, band] += d2t[0:HEAD_DIM, :] + d2t[HEAD_DIM:, :]
        for acc, carry, off in ((dkf_s, carryk_s, OFF_K), (dvf_s, carryv_s, OFF_V)):
            acc[:, t:t + BLOCK] += carry[...]
            carry[...] = acc[:, 0:BLOCK]
            d = acc[:, BLOCK:].T
            dproj_ref[:, off:off + BLOCK] = d.astype(act)
            acc_bin[:, off:off + BLOCK] += _rows8(d)

        dh = _mm(dproj_ref[...], wint_v[...])
        dprojt_ref[...] = dproj_ref[...].T
        mixedt_ref[...] = mixed_ref[...].T
        for c in range(nb):
            rows = slice(c * BLOCK, (c + 1) * BLOCK)
            r1 = r1_s[rows, :]
            xn = x_ref[rows, :] * r1
            dhc = dh[rows, :]
            acc_g1[...] += _rows8(dhc * xn)
            dxn = dhc * g1
            dx_ref[rows, :] = doutf_s[rows, :] + r1 * (dxn - xn * _mean_rows(dxn * xn))

        @pl.when(i == nt - 1)
        def _():
            tril = lax.broadcasted_iota(jnp.int32, (BLOCK, BLOCK), 0) >= lane128
            for hh in range(N_SGU_HEADS):
                rws = slice(hh * BLOCK, (hh + 1) * BLOCK)
                sw_ref[rws, :] = jnp.where(tril, acc_dws[rws, :], 0.0)
            vec_ref[...] = jnp.zeros((VEC_ROWS, 128), F32)

            def put(row0, acc):
                s = jnp.sum(acc[...], axis=0, keepdims=True)
                for k in range(acc.shape[1] // 128):
                    vec_ref[row0 + k:row0 + k + 1, :] = s[:, k * 128:(k + 1) * 128]

            put(R_G1, acc_g1)
            put(R_BIN, acc_bin)
            put(R_LNG, acc_lng)
            put(R_LNB, acc_lnb)
            put(R_BOUT, acc_bout)
            put(R_G3, acc_g3)
            vec_ref[R_SINK:R_SINK + 1, :] = jnp.sum(
                jnp.where(lax.broadcasted_iota(jnp.int32, (8, 128), 0) == lax.broadcasted_iota(jnp.int32, (8, 128), 1),
                          acc_sink[...], 0.0), axis=0, keepdims=True)
            vec_ref[R_LOSS:R_LOSS + 1, :] = jnp.zeros((1, 128), F32) + jnp.sum(acc_loss[...])
            dbs_t = acc_dbs[...].T
            vec_ref[R_SGUB:R_SGUB + 8, :] = jnp.sum(dbs_t.reshape(N_SGU_HEADS, SGU_W // N_SGU_HEADS, BLOCK), axis=1)

    full = lambda shape: pl.BlockSpec(shape, lambda i: (0,) * len(shape))
    tok = lambda w: pl.BlockSpec((t, w), lambda i: (nt - 1 - i, 0))
    in_specs = [
        pl.BlockSpec(memory_space=pltpu.SMEM),
        tok(D_MODEL),
        pl.BlockSpec((BLOCK, D_MODEL), lambda i: (jnp.maximum((nt - 1 - i) * nb - 1, 0), 0)),
        tok(D_MODEL),
        full((1, D_MODEL)), full((1, IN_W)), full((1, SGU_W)), full((1, SGU_W)),
        full((N_SGU_HEADS, BLOCK, BLOCK)), full((BLOCK, SGU_W)), full((1, D_MODEL)), full((1, D_MODEL)),
        pl.BlockSpec(memory_space=pl.ANY), pl.BlockSpec(memory_space=pl.ANY),
    ]
    out_shape = [
        jax.ShapeDtypeStruct((seq, D_MODEL), F32),
        jax.ShapeDtypeStruct((seq, D_MODEL), act),
        jax.ShapeDtypeStruct((IN_W, seq), act),
        jax.ShapeDtypeStruct((D_MODEL, seq), act),
        jax.ShapeDtypeStruct((seq, D_MODEL), act),
        jax.ShapeDtypeStruct((SW_ROWS, 128), F32),
        jax.ShapeDtypeStruct((VEC_ROWS, 128), F32),
    ]
    tok_t = lambda w: pl.BlockSpec((w, t), lambda i: (0, nt - 1 - i))
    out_specs = [tok(D_MODEL), tok(D_MODEL), tok_t(IN_W), tok_t(D_MODEL), tok(D_MODEL), full((SW_ROWS, 128)), full((VEC_ROWS, 128))]
    vm = pltpu.VMEM
    scratch = [
        vm((t, IN_W), act), vm((t, D_MODEL), act),
        vm((IN_W, D_MODEL), act), vm((D_MODEL, D_MODEL), act),
        vm((4, BLOCK, 2 * BLOCK), act), vm((4, BLOCK, 2 * BLOCK), act),
        vm((t, ATTN_W), act),
        vm((t + BLOCK, BLOCK), F32), vm((t + BLOCK, BLOCK), F32),
        vm((2, t + BLOCK, BLOCK), act), vm((2, t + BLOCK, BLOCK), act),
        vm((4, t, 512), F32),
        vm((2 * nb, 4 * BLOCK, 2 * BLOCK), F32), vm((2 * nb, 4 * BLOCK, 1), F32),
        vm((t, ATTN_W), F32), vm((t, SGU_W), F32), vm((t, SGU_W), F32), vm((t, SGU_W), F32), vm((t, 1), F32),
        vm((t, SGU_W), F32), vm((t, SGU_W), F32), vm((t, 1), F32),
        vm((t, D_MODEL), F32), vm((t, D_MODEL), F32),
        vm((BLOCK, t + BLOCK), F32), vm((BLOCK, t + BLOCK), F32), vm((BLOCK, BLOCK), F32), vm((BLOCK, BLOCK), F32),
        vm((8, IN_W), F32), vm((8, D_MODEL), F32), vm((8, D_MODEL), F32), vm((8, D_MODEL), F32),
        vm((8, SGU_W), F32), vm((8, SGU_W), F32), vm((N_SGU_HEADS * BLOCK, BLOCK), F32), vm((BLOCK, SGU_W), F32),
        vm((8, 128), F32), vm((8, 128), F32),
        vm((t + BLOCK, D_MODEL), act), vm((4, 2 * BLOCK, t), act), vm((4, BLOCK, t), act), vm((t, SGU_W), F32),
        pltpu.SemaphoreType.DMA((2,)),
    ]
    return pl.pallas_call(
        body, name="fused", grid=(nt,), in_specs=in_specs, out_specs=out_specs, out_shape=out_shape,
        scratch_shapes=scratch,
        compiler_params=pltpu.CompilerParams(dimension_semantics=("arbitrary",), vmem_limit_bytes=VMEM_LIMIT),
    )(sinks, x, x, tgt, g1, b_in, ln_g, ln_b, sgu_w, bexp, b_out, g3, win_t, wout)


def _gather_call(a32, b32):
    act = MXU_DTYPE

    def body(a32_hbm, b32_hbm, ga_hbm, gb_hbm, a32_v, b32_v, ga_v, gb_v, send_sems, recv_sems, loc_sems):
        x, y, c = lax.axis_index("x"), lax.axis_index("y"), lax.axis_index("c")
        me, sibling, x_nbr, y_nbr = (x, y, c), (x, y, 1 - c), (1 - x, y, c), (x, 1 - y, c)
        j, j_x, j_y, j_d = 2 * x + y, 2 * (1 - x) + y, 2 * x + (1 - y), 2 * (1 - x) + (1 - y)
        arrays = ((0, ga_v, ga_hbm, W_IN_SHARD), (1, gb_v, gb_hbm, W_OUT_SHARD))

        loads = [pltpu.make_async_copy(a32_hbm, a32_v, loc_sems.at[0]), pltpu.make_async_copy(b32_hbm, b32_v, loc_sems.at[1])]
        for cp in loads:
            cp.start()
        for cp in loads:
            cp.wait()
        ga_v[pl.ds(pl.multiple_of(j * W_IN_SHARD, 16), W_IN_SHARD), :] = a32_v[...].astype(act)
        gb_v[pl.ds(pl.multiple_of(j * W_OUT_SHARD, 16), W_OUT_SHARD), :] = b32_v[...].astype(act)

        def rows(shard, chip, hf=None, q=None):
            if hf is None:
                return pl.ds(pl.multiple_of(chip * shard, 16), shard)
            return pl.ds(pl.multiple_of(chip * shard + hf * (shard // 2) + q * (shard // 4), 16), shard // 4)

        def copy(k, ref, at, to):
            return pltpu.make_async_remote_copy(src_ref=ref.at[at, :], dst_ref=ref.at[at, :], send_sem=send_sems.at[k],
                                                recv_sem=recv_sems.at[k], device_id=to, device_id_type=MESH)

        stores = []

        def store(vmem, hbm, at):
            stores.append(pltpu.make_async_copy(vmem.at[at, :], hbm.at[at, :], loc_sems.at[len(stores)]))
            stores[-1].start()

        sent = []
        for w, vmem, hbm, shard in arrays:
            for k, (q, to) in enumerate(((0, x_nbr), (1, x_nbr), (1, y_nbr), (0, y_nbr))):
                sent.append(copy(12 * w + k, vmem, rows(shard, j, c, q), to))
                sent[-1].start()
        for w, vmem, hbm, shard in arrays:
            store(vmem, hbm, rows(shard, j))

        def landed(w, vmem, hbm, shard, k, chip, q, onward=None):
            at = rows(shard, chip, c, q)
            copy(12 * w + k, vmem, at, me).wait_recv()
            if onward is not None:
                sent.append(copy(12 * w + onward[0], vmem, at, onward[1]))
                sent[-1].start()
            sent.append(copy(12 * w + 6 + k, vmem, at, sibling))
            sent[-1].start()
            store(vmem, hbm, at)

        for arr in arrays:
            landed(*arr, 0, j_x, 0, onward=(4, y_nbr))
            landed(*arr, 2, j_y, 1, onward=(5, x_nbr))
        for arr in arrays:
            landed(*arr, 1, j_x, 1)
            landed(*arr, 3, j_y, 0)
            landed(*arr, 4, j_d, 0)
            landed(*arr, 5, j_d, 1)
        for w, vmem, hbm, shard in arrays:
            for k, (chip, q) in enumerate(((j_x, 0), (j_x, 1), (j_y, 1), (j_y, 0), (j_d, 0), (j_d, 1))):
                at = rows(shard, chip, 1 - c, q)
                copy(12 * w + 6 + k, vmem, at, me).wait_recv()
                store(vmem, hbm, at)
        for cp in sent:
            cp.wait_send()
        for cp in stores:
            cp.wait()

    hbm = pl.BlockSpec(memory_space=pl.ANY)
    vm = pltpu.VMEM
    return pl.pallas_call(
        body, name="gather", in_specs=[hbm, hbm], out_specs=[hbm, hbm],
        out_shape=[jax.ShapeDtypeStruct((IN_W, D_MODEL), act), jax.ShapeDtypeStruct((D_MODEL, D_MODEL), act)],
        scratch_shapes=[vm(a32.shape, F32), vm(b32.shape, F32), vm((IN_W, D_MODEL), act), vm((D_MODEL, D_MODEL), act),
                        pltpu.SemaphoreType.DMA((24,)), pltpu.SemaphoreType.DMA((24,)), pltpu.SemaphoreType.DMA((26,))],
        compiler_params=pltpu.CompilerParams(vmem_limit_bytes=VMEM_LIMIT),
    )(a32, b32)


def _wgrad_reduce_call(dproj_t, h, mixed_t, dout, sw, vec):
    wire = jnp.bfloat16
    half_sw = SW_ROWS // 2
    seq = h.shape[0]
    tk = min(2048, seq)
    nk = seq // tk
    n_steps = 2 * N_CHIPS
    rel_of = lambda s: s % 3 if s < 6 else 3
    half_of = lambda s: s // 3 if s < 6 else s - 6
    x, y = lax.axis_index("x"), lax.axis_index("y")
    chip_of = [2 * (1 - x) + (1 - y), 2 * (1 - x) + y, 2 * x + (1 - y), 2 * x + y]
    order = jnp.stack([2 * chip_of[rel_of(s)] + half_of(s) for s in range(n_steps)]).astype(jnp.int32)

    def body(order_ref, dpt_ref, h_hbm, mxt_ref, dout_hbm, sw_ref, small_ref, oa_ref, ob_ref, osw_ref, osmall_ref,
             h_v, dout_v, acc_a, acc_b, sib_a, sib_b, snd_a, snd_b, in_a, in_b, own_a, own_b, fin_a, fin_b,
             all_small, sw_sib, sw_chips, sw_fin, send_sems, recv_sems, loc_sems):
        x, y, c = lax.axis_index("x"), lax.axis_index("y"), lax.axis_index("c")
        me, sibling = (x, y, c), (x, y, 1 - c)
        steps = [(1 - x, 1 - y), (1 - x, y), (x, 1 - y)]
        j = 2 * x + y
        dev = 4 * x + 2 * y + c
        b, k = pl.program_id(0), pl.program_id(1)

        def copy(n, src, dst, to):
            return pltpu.make_async_remote_copy(src_ref=src, dst_ref=dst, send_sem=send_sems.at[n], recv_sem=recv_sems.at[n],
                                                device_id=to, device_id_type=MESH)

        def sw_rows(ref, hf):
            return ref.at[pl.ds(pl.multiple_of(hf * half_sw, 8), half_sw), :]

        sm_first = [copy(16, small_ref, all_small.at[dev], sibling)]
        sm_first += [copy(17 + r, small_ref, all_small.at[dev], (*chip, c)) for r, chip in enumerate(steps)]
        sw_to_sib = copy(23, sw_rows(sw_ref, 1 - c), sw_sib, sibling)

        @pl.when((b == 0) & (k == 0))
        def _():
            all_small[dev] = small_ref[...]
            for cp in sm_first + [sw_to_sib]:
                cp.start()

        def load(kk):
            rows = pl.ds(kk * tk, tk)
            return [pltpu.make_async_copy(h_hbm.at[rows, :], h_v.at[rows, :], loc_sems.at[kk]),
                    pltpu.make_async_copy(dout_hbm.at[rows, :], dout_v.at[rows, :], loc_sems.at[nk + kk])]

        for kk in range(nk):
            @pl.when((b == 0) & (k == 0))
            def _():
                for cp in load(kk):
                    cp.start()

        for kk in range(nk):
            @pl.when((b == 0) & (k == kk))
            def _():
                for cp in load(kk):
                    cp.wait()

        tok = pl.ds(pl.multiple_of(k * tk, tk), tk)
        pa = _mm(dpt_ref[...], h_v[tok, :])
        pb = _mm(mxt_ref[...], dout_v[tok, :])
        slot = b % 2

        @pl.when(k == 0)
        def _():
            acc_a[slot] = pa
            acc_b[slot] = pb

        @pl.when(k != 0)
        def _():
            acc_a[slot] += pa
            acc_b[slot] += pb

        def to_sibling(s):
            r = rel_of(s)
            return [copy(r, acc_a.at[s % 2], sib_a.at[r], sibling), copy(4 + r, acc_b.at[s % 2], sib_b.at[r], sibling)]

        def chip_partial(s):
            r = rel_of(s)
            copy(r, sib_a.at[r], sib_a.at[r], me).wait_recv()
            copy(4 + r, sib_b.at[r], sib_b.at[r], me).wait_recv()
            return acc_a[s % 2] + sib_a[r], acc_b[s % 2] + sib_b[r]

        def to_owner(r):
            return [copy(8 + r, snd_a.at[r], in_a.at[r], (*steps[r], c)), copy(11 + r, snd_b.at[r], in_b.at[r], (*steps[r], c))]

        def vec_forwards():
            return [copy(20 + r, all_small.at[4 * cx + 2 * cy + c], all_small.at[4 * cx + 2 * cy + c], sibling)
                    for r, (cx, cy) in enumerate(steps)]

        def sw_to_chips():
            return [copy(24 + r, sw_chips.at[j], sw_chips.at[j], (*chip, c)) for r, chip in enumerate(steps)]

        def sw_to_sibling():
            return copy(27, sw_rows(osw_ref, c), sw_fin, sibling)

        @pl.when((b == 1) & (k == 0))
        def _():
            for r, (cx, cy) in enumerate(steps):
                d = 4 * cx + 2 * cy + c
                copy(17 + r, all_small.at[d], all_small.at[d], me).wait_recv()
            for cp in vec_forwards():
                cp.start()
            copy(23, sw_sib, sw_sib, me).wait_recv()
            sw_chips[j] = (sw_rows(sw_ref, c)[...] + sw_sib[...]).astype(wire)
            for cp in sw_to_chips():
                cp.start()

        @pl.when((b == 3) & (k == 0))
        def _():
            for r, (cx, cy) in enumerate(steps):
                cj = 2 * cx + cy
                copy(24 + r, sw_chips.at[cj], sw_chips.at[cj], me).wait_recv()
            tot_sw = sw_chips[0].astype(F32)
            for q in range(1, N_CHIPS):
                tot_sw = tot_sw + sw_chips[q].astype(F32)
            sw_rows(osw_ref, c)[...] = tot_sw
            sw_to_sibling().start()

        late = min(1, nk - 1)
        for s in range(n_steps):
            if s >= 1:
                sp = s - 1

                @pl.when((b == s) & (k == late) & (c == half_of(sp)))
                def _():
                    ta, tb = chip_partial(sp)
                    r = rel_of(sp)
                    if r < 3:
                        snd_a[r] = ta.astype(wire)
                        snd_b[r] = tb.astype(wire)
                        for cp in to_owner(r):
                            cp.start()
                    else:
                        own_a[...] = ta
                        own_b[...] = tb

                @pl.when((b == s) & (k == nk - 1) & (c != half_of(sp)))
                def _():
                    for cp in to_sibling(sp):
                        cp.wait_send()

            @pl.when((b == s) & (k == nk - 1) & (c != half_of(s)))
            def _():
                for cp in to_sibling(s):
                    cp.start()

        @pl.when((b == n_steps - 1) & (k == nk - 1))
        def _():
            last = n_steps - 1

            @pl.when(c == half_of(last))
            def _():
                own_a[...], own_b[...] = chip_partial(last)

            @pl.when(c != half_of(last))
            def _():
                for cp in to_sibling(last):
                    cp.wait_send()

            tot_a = own_a[...]
            tot_b = own_b[...]
            for s in range(3):
                copy(8 + s, in_a.at[s], in_a.at[s], me).wait_recv()
                copy(11 + s, in_b.at[s], in_b.at[s], me).wait_recv()
                tot_a = tot_a + in_a[s].astype(F32)
                tot_b = tot_b + in_b[s].astype(F32)
            mine_a = oa_ref.at[pl.ds(pl.multiple_of(c * HALF_A, 8), HALF_A), :]
            mine_b = ob_ref.at[pl.ds(pl.multiple_of(c * HALF_B, 8), HALF_B), :]
            mine_a[...] = tot_a
            mine_b[...] = tot_b
            back = [copy(14, mine_a, fin_a, sibling), copy(15, mine_b, fin_b, sibling)]
            for cp in back:
                cp.start()

            copy(16, small_ref, all_small.at[dev ^ 1], me).wait_recv()
            for r, (cx, cy) in enumerate(steps):
                d = 4 * cx + 2 * cy + (1 - c)
                copy(20 + r, all_small.at[d], all_small.at[d], me).wait_recv()
            tot = all_small[0]
            for d in range(1, N_DEV):
                tot = tot + all_small[d]
            osmall_ref[...] = tot

            copy(14, fin_a, fin_a, me).wait_recv()
            copy(15, fin_b, fin_b, me).wait_recv()
            copy(27, sw_fin, sw_fin, me).wait_recv()
            oa_ref[pl.ds(pl.multiple_of((1 - c) * HALF_A, 8), HALF_A), :] = fin_a[...]
            ob_ref[pl.ds(pl.multiple_of((1 - c) * HALF_B, 8), HALF_B), :] = fin_b[...]
            sw_rows(osw_ref, 1 - c)[...] = sw_fin[...]
            sends = sm_first + vec_forwards() + sw_to_chips() + back + [sw_to_sib, sw_to_sibling()]
            for s in range(3):
                sends += to_owner(s)
            for cp in sends:
                cp.wait_send()

    vmem = pl.BlockSpec(memory_space=pltpu.VMEM)
    vm = pltpu.VMEM
    grid_spec = pltpu.PrefetchScalarGridSpec(
        num_scalar_prefetch=1, grid=(n_steps, nk),
        in_specs=[pl.BlockSpec((HALF_A, tk), lambda b, k, o: (o[b], k)), pl.BlockSpec(memory_space=pl.ANY),
                  pl.BlockSpec((HALF_B, tk), lambda b, k, o: (o[b], k)), pl.BlockSpec(memory_space=pl.ANY),
                  vmem, vmem],
        out_specs=[vmem, vmem, vmem, vmem],
        scratch_shapes=[vm((seq, D_MODEL), h.dtype), vm((seq, D_MODEL), dout.dtype),
                        vm((2, HALF_A, D_MODEL), F32), vm((2, HALF_B, D_MODEL), F32),
                        vm((N_CHIPS, HALF_A, D_MODEL), F32), vm((N_CHIPS, HALF_B, D_MODEL), F32),
                        vm((3, HALF_A, D_MODEL), wire), vm((3, HALF_B, D_MODEL), wire),
                        vm((3, HALF_A, D_MODEL), wire), vm((3, HALF_B, D_MODEL), wire),
                        vm((HALF_A, D_MODEL), F32), vm((HALF_B, D_MODEL), F32),
                        vm((HALF_A, D_MODEL), F32), vm((HALF_B, D_MODEL), F32),
                        vm((N_DEV, VEC_ROWS, 128), F32), vm((half_sw, 128), F32), vm((N_CHIPS, half_sw, 128), wire),
                        vm((half_sw, 128), F32),
                        pltpu.SemaphoreType.DMA((28,)), pltpu.SemaphoreType.DMA((28,)), pltpu.SemaphoreType.DMA((2 * nk,))])
    return pl.pallas_call(
        body, name="wgrad_reduce", grid_spec=grid_spec,
        out_shape=[jax.ShapeDtypeStruct((W_IN_SHARD, D_MODEL), F32), jax.ShapeDtypeStruct((W_OUT_SHARD, D_MODEL), F32),
                   jax.ShapeDtypeStruct((SW_ROWS, 128), F32), jax.ShapeDtypeStruct((VEC_ROWS, 128), F32)],
        compiler_params=pltpu.CompilerParams(dimension_semantics=("arbitrary", "arbitrary"), vmem_limit_bytes=VMEM_LIMIT),
    )(order, dproj_t, h, mixed_t, dout, sw, vec)


def _adamw(w, g, m, v):
    nm = ADAM_B1 * m + (1.0 - ADAM_B1) * g
    nv = ADAM_B2 * v + (1.0 - ADAM_B2) * (g * g)
    m_hat = nm / (1.0 - ADAM_B1 ** ADAM_STEP)
    v_hat = nv / (1.0 - ADAM_B2 ** ADAM_STEP)
    return -ADAM_LR * (m_hat / (jnp.sqrt(v_hat) + ADAM_EPS) + ADAM_WD * w), nm, nv


def _adamw_shards_call(a, b, steps=4):
    def body(*refs):
        for k in range(2):
            w_ref, g_ref, m_ref, v_ref = refs[4 * k:4 * k + 4]
            go_ref, d_ref, nm_ref, nv_ref = refs[8 + 4 * k:12 + 4 * k]
            gg = g_ref[...]
            go_ref[...] = gg
            d_ref[...], nm_ref[...], nv_ref[...] = _adamw(w_ref[...], gg, m_ref[...], v_ref[...])

    specs, shapes = [], []
    for w in (a[0], b[0]):
        rows, cols = w.shape
        specs += [pl.BlockSpec((rows // steps, cols), lambda i: (i, 0))] * 4
        shapes += [jax.ShapeDtypeStruct((rows, cols), F32)] * 4
    return pl.pallas_call(
        body, name="adamw_shards", grid=(steps,), in_specs=specs, out_specs=specs, out_shape=shapes,
        compiler_params=pltpu.CompilerParams(dimension_semantics=("arbitrary",)),
    )(*a, *b)


_SMALL = (("norm_g", (1, D_MODEL), R_G1), ("b_in", (1, IN_W), R_BIN), ("attn_sinks", (1, 8), R_SINK),
          ("sgu_ln_g", (1, SGU_W), R_LNG), ("sgu_ln_b", (1, SGU_W), R_LNB), ("sgu_b", (N_SGU_HEADS, BLOCK), R_SGUB),
          ("b_out", (1, D_MODEL), R_BOUT), ("final_norm_g", (1, D_MODEL), R_G3))


def _adamw_small_call(sw_g, vec_g, sgu_w3, ws, ms, vs):
    n = len(_SMALL)

    def body(*refs):
        sw_ref, vec_ref = refs[0], refs[1]
        w3 = refs[2:5]
        w_refs, m_refs, v_refs = refs[5:5 + n], refs[5 + n:5 + 2 * n], refs[5 + 2 * n:5 + 3 * n]
        outs = refs[5 + 3 * n:]
        outs[0][...] = vec_ref[R_LOSS:R_LOSS + 1, 0:1]
        g = sw_ref[...]
        outs[1][...] = g
        outs[2][...], outs[3][...], outs[4][...] = _adamw(w3[0][...], g, w3[1][...], w3[2][...])
        for k, (_, shape, row) in enumerate(_SMALL):
            if shape[0] == 1 and shape[1] >= 128:
                g = jnp.concatenate([vec_ref[row + q:row + q + 1, :] for q in range(shape[1] // 128)], axis=1)
            else:
                g = vec_ref[row:row + shape[0], 0:shape[1]]
            o = outs[5 + 4 * k:9 + 4 * k]
            o[0][...] = g
            o[1][...], o[2][...], o[3][...] = _adamw(w_refs[k][...], g, m_refs[k][...], v_refs[k][...])

    vmem = pl.BlockSpec(memory_space=pltpu.VMEM)
    out_shape = [jax.ShapeDtypeStruct((1, 1), F32)] + [jax.ShapeDtypeStruct((SW_ROWS, 128), F32)] * 4
    for _, shape, _ in _SMALL:
        out_shape += [jax.ShapeDtypeStruct(shape, F32)] * 4
    args = [sw_g, vec_g, *sgu_w3, *ws, *ms, *vs]
    return pl.pallas_call(
        body, name="adamw_small", in_specs=[vmem] * len(args), out_specs=[vmem] * len(out_shape), out_shape=out_shape,
    )(*args)


def kernel(x, norm_g, w_in, b_in, attn_sinks, sgu_ln_g, sgu_ln_b, sgu_w, sgu_b, w_out, b_out, final_norm_g, loss_target, m_norm_g, m_w_in, m_b_in, m_attn_sinks, m_sgu_ln_g, m_sgu_ln_b, m_sgu_w, m_sgu_b, m_w_out, m_b_out, m_final_norm_g, v_norm_g, v_w_in, v_b_in, v_attn_sinks, v_sgu_ln_g, v_sgu_ln_b, v_sgu_w, v_sgu_b, v_w_out, v_b_out, v_final_norm_g):
    seq = x.shape[1]
    win_t, wout = _gather_call(w_in[0].T, w_out[0])
    bexp = jnp.repeat(sgu_b[0].T, SGU_W // N_SGU_HEADS, axis=1)
    dx, h, dproj_t, mixed_t, dout, sw, vec = _fused_call(
        x[0], loss_target[0], attn_sinks[0], norm_g, b_in, sgu_ln_g, sgu_ln_b, sgu_w[0], bexp, b_out,
        final_norm_g.reshape(1, D_MODEL), win_t, wout)
    ga_t, g_w_out, sw, vec = _wgrad_reduce_call(dproj_t, h, mixed_t, dout, sw, vec)

    names = ["norm_g", "w_in", "b_in", "attn_sinks", "sgu_ln_g", "sgu_ln_b", "sgu_w", "sgu_b", "w_out", "b_out", "final_norm_g"]
    res = {}
    shards = _adamw_shards_call((w_in[0].T, ga_t, m_w_in[0].T, v_w_in[0].T), (w_out[0], g_w_out, m_w_out[0], v_w_out[0]))
    res["w_in"] = [a.T[None] for a in shards[:4]]
    res["w_out"] = [a[None] for a in shards[4:]]
    given = dict(norm_g=(norm_g, m_norm_g, v_norm_g), b_in=(b_in, m_b_in, v_b_in), attn_sinks=(attn_sinks, m_attn_sinks, v_attn_sinks),
                 sgu_ln_g=(sgu_ln_g, m_sgu_ln_g, v_sgu_ln_g), sgu_ln_b=(sgu_ln_b, m_sgu_ln_b, v_sgu_ln_b),
                 sgu_b=(sgu_b, m_sgu_b, v_sgu_b), b_out=(b_out, m_b_out, v_b_out),
                 final_norm_g=(final_norm_g, m_final_norm_g, v_final_norm_g))
    wmv = [[given[n][k].reshape(shape) for n, shape, _ in _SMALL] for k in range(3)]
    outs = _adamw_small_call(sw, vec, [a.reshape(SW_ROWS, BLOCK) for a in (sgu_w, m_sgu_w, v_sgu_w)], *wmv)
    loss = outs[0].reshape(())
    res["sgu_w"] = [a.reshape(sgu_w.shape) for a in outs[1:5]]
    for k, (n, _, _) in enumerate(_SMALL):
        res[n] = [a.reshape(given[n][0].shape) for a in outs[5 + 4 * k:9 + 4 * k]]

    return (loss, dx[None], *[res[n][0] for n in names], *[res[n][1] for n in names], *[res[n][2] for n in names],
            *[res[n][3] for n in names])
```

```python
import functools
import math

import jax
import jax.numpy as jnp
from jax import lax
from jax.experimental import pallas as pl
from jax.experimental.pallas import tpu as pltpu

F32 = jnp.float32
MXU_DTYPE = jnp.bfloat16

D_MODEL = 1024
HEAD_DIM = 64
ATTN_W = 512
SGU_W = 512
N_SGU_HEADS = 8
BLOCK = 128
IN_W = 2816
OFF_K, OFF_V, OFF_ZA, OFF_US, OFF_VS, OFF_ZS = 512, 640, 768, 1280, 1792, 2304
NORM_EPS = 1e-5
NEG_INF = -1e30
SCALE = HEAD_DIM ** -0.5
SQRT_HALF = math.sqrt(0.5)
INV_SQRT_2PI = 1.0 / math.sqrt(2.0 * math.pi)

N_CHIPS = 4
N_DEV = 8
W_IN_SHARD = IN_W // N_CHIPS
W_OUT_SHARD = D_MODEL // N_CHIPS
HALF_A = W_IN_SHARD // 2
HALF_B = W_OUT_SHARD // 2

TILE = 256
VMEM_LIMIT = 56 * 1024 * 1024

ADAM_LR, ADAM_B1, ADAM_B2, ADAM_EPS, ADAM_WD, ADAM_STEP = 0.001, 0.9, 0.999, 1e-08, 0.01, 10

SW_ROWS = N_SGU_HEADS * BLOCK
R_G1, R_BIN, R_SINK, R_LOSS, R_LNG, R_LNB, R_SGUB, R_BOUT, R_G3 = 0, 8, 32, 40, 48, 56, 64, 72, 80
VEC_ROWS = 88

MESH = pl.DeviceIdType.MESH


def _mm(a, b):
    return jnp.dot(a, b, preferred_element_type=F32)


def _mm_nt(a, b):
    return lax.dot_general(a, b, (((1,), (1,)), ((), ())), preferred_element_type=F32)


def _sigmoid(z):
    return 1.0 / (1.0 + jnp.exp(-z))


def _norm_cdf(z):
    return 0.5 * (1.0 + lax.erf(z * SQRT_HALF))


def _norm_pdf(z):
    return jnp.exp(-0.5 * z * z) * INV_SQRT_2PI


def _rows8(v):
    r, n = v.shape
    return jnp.sum(v.reshape(r // 8, 8, n), axis=0)


def _mean_rows(v):
    return jnp.sum(v, axis=1, keepdims=True) * (1.0 / v.shape[1])


def _fused_call(x, h, r1, tgt, sinks, g1, b_in, ln_g, ln_b, sgu_w, bexp, b_out, g3, win_t, wout):
    seq = x.shape[0]
    t = TILE
    nt = seq // t
    nb = t // BLOCK
    act = MXU_DTYPE

    def body(sinks_ref, x_ref, h_ref, hh_ref, r1_ref, tgt_ref, g1_ref, bin_ref, lng_ref, lnb_ref, sguw_ref, bexp_ref, bout_ref, g3_ref,
             wint_hbm, wout_hbm,
             dx_ref, dprojt_ref, mixedt_ref, dout_ref, sw_ref, vec_ref,
             dproj_ref, mixed_ref, wint_v, wout_v, wf_v, wb_v, q_s, kf_s, vf_s, k2_s, v2_s, gate_s, p_s, ps_s, o_s, u_s, mix_s, vhat_s, r2_s,
             cdfu_s, cdfv_s, doutf_s, dmix_s, dkf_s, dvf_s, carryk_s, carryv_s,
             acc_bin, acc_g1, acc_bout, acc_g3, acc_lng, acc_lnb, acc_dws, acc_dbs, acc_sink, acc_loss,
             h_s, rhs_s, vlnp_s, dvln_s, sems):
        i = pl.program_id(0)
        tile = nt - 1 - i
        lane128 = lax.broadcasted_iota(jnp.int32, (BLOCK, BLOCK), 1)
        lo = lane128 < HEAD_DIM

        @pl.when(i == 0)
        def _():
            cp_a = pltpu.make_async_copy(wint_hbm, wint_v, sems.at[0])
            cp_b = pltpu.make_async_copy(wout_hbm, wout_v, sems.at[1])
            cp_a.start()
            cp_b.start()
            for acc in (acc_bin, acc_g1, acc_bout, acc_g3, acc_lng, acc_lnb, acc_dws, acc_dbs, acc_sink, acc_loss,
                        carryk_s, carryv_s):
                acc[...] = jnp.zeros(acc.shape, F32)
            tril = lax.broadcasted_iota(jnp.int32, (BLOCK, BLOCK), 0) >= lane128
            for h in range(N_SGU_HEADS):
                w = jnp.where(tril, sguw_ref[h], 0.0)
                wf_v[h // 2, :, (h % 2) * BLOCK:(h % 2 + 1) * BLOCK] = w.astype(act)
                wb_v[h // 2, :, (h % 2) * BLOCK:(h % 2 + 1) * BLOCK] = w.T.astype(act)
            cp_a.wait()
            cp_b.wait()

        g1 = g1_ref[...]

        def rms(v):
            r = lax.rsqrt(_mean_rows(v * v) + NORM_EPS)
            return r, v * r

        h_s[0:BLOCK, :] = hh_ref[...]
        h_s[BLOCK:, :] = h_ref[...]
        r1_s = r1_ref

        h = h_ref[...]
        q = _mm_nt(h, wint_v[0:OFF_K, :]) + bin_ref[:, 0:OFF_K]
        q_s[...] = (q * SCALE).astype(act)
        kv = _mm_nt(h_s[...], wint_v[OFF_K:OFF_ZA, :]) + bin_ref[:, OFF_K:OFF_ZA]
        kf_s[...] = kv[:, :BLOCK]
        vf_s[...] = kv[:, BLOCK:]
        for r in range(4):
            cols = slice(OFF_ZA + r * 512, OFF_ZA + (r + 1) * 512)
            gate_s[r] = _mm_nt(h, wint_v[cols, :]) + bin_ref[:, cols]

        lo_kv = lax.broadcasted_iota(jnp.int32, (t + BLOCK, BLOCK), 1) < HEAD_DIM
        for src, dst in ((kf_s, k2_s), (vf_s, v2_s)):
            v = src[...]
            vr = pltpu.roll(v, HEAD_DIM, 1)
            dst[0] = jnp.where(lo_kv, v, vr).astype(act)
            dst[1] = jnp.where(lo_kv, vr, v).astype(act)

        rowi = lax.broadcasted_iota(jnp.int32, (BLOCK, 2 * BLOCK), 0)
        colj = lax.broadcasted_iota(jnp.int32, (BLOCK, 2 * BLOCK), 1)
        in_band = (colj > rowi) & (colj <= rowi + BLOCK)
        row512 = lax.broadcasted_iota(jnp.int32, (4 * BLOCK, 1), 0)

        def stacked_q(b, g):
            parts = []
            for p in range(2):
                slab = q_s[b * BLOCK:(b + 1) * BLOCK, (2 * g + p) * BLOCK:(2 * g + p + 1) * BLOCK]
                parts += [jnp.where(lo, slab, jnp.zeros_like(slab)), jnp.where(lo, jnp.zeros_like(slab), slab)]
            return jnp.concatenate(parts, axis=0)

        def sink_col(g):
            s = [sinks_ref[4 * g + k] for k in range(4)]
            return jnp.where(row512 < BLOCK, s[0], jnp.where(row512 < 2 * BLOCK, s[1], jnp.where(row512 < 3 * BLOCK, s[2], s[3])))

        for b in range(nb):
            band = slice(b * BLOCK, (b + 2) * BLOCK)
            first_key = jnp.where(tile * nb + b > 0, 0, BLOCK)
            valid = in_band & (colj >= first_key)
            valid4 = jnp.concatenate([valid] * 4, axis=0)
            for g in range(2):
                s = _mm_nt(stacked_q(b, g), k2_s[g, band, :])
                s = jnp.where(valid4, s, NEG_INF)
                sk = sink_col(g)
                m = jnp.maximum(jnp.max(s, axis=1, keepdims=True), sk)
                p = jnp.exp(s - m)
                psk = jnp.exp(sk - m)
                inv = 1.0 / (jnp.sum(p, axis=1, keepdims=True) + psk)
                p = p * inv
                p_s[b * 2 + g] = p
                ps_s[b * 2 + g] = psk * inv
                o2 = _mm(p.astype(act), v2_s[g, band, :])
                for pr in range(2):
                    o_s[b * BLOCK:(b + 1) * BLOCK, (2 * g + pr) * BLOCK:(2 * g + pr + 1) * BLOCK] = jnp.where(
                        lo, o2[(2 * pr) * BLOCK:(2 * pr + 1) * BLOCK], o2[(2 * pr + 1) * BLOCK:(2 * pr + 2) * BLOCK])

        lng = lng_ref[...]
        lnb = lnb_ref[...]

        def split_pairs(val, c):
            for p in range(4):
                slab = val[:, p * BLOCK:(p + 1) * BLOCK]
                rhs_s[p, 0:BLOCK, c * BLOCK:(c + 1) * BLOCK] = jnp.where(lo, slab, 0.0).astype(act)
                rhs_s[p, BLOCK:, c * BLOCK:(c + 1) * BLOCK] = jnp.where(lo, 0.0, slab).astype(act)

        for c in range(nb):
            rows = slice(c * BLOCK, (c + 1) * BLOCK)
            za = gate_s[0, rows, :]
            mixed_ref[rows, 0:ATTN_W] = (o_s[rows, :] * (za * _sigmoid(za))).astype(act)
            us = gate_s[1, rows, :]
            vs = gate_s[2, rows, :]
            cu = _norm_cdf(us)
            cv = _norm_cdf(vs)
            cdfu_s[rows, :] = cu
            cdfv_s[rows, :] = cv
            u = us * cu
            vg = vs * cv
            vc = vg - _mean_rows(vg)
            r2 = lax.rsqrt(_mean_rows(vc * vc) + NORM_EPS)
            vhat = vc * r2
            r2_s[rows, :] = r2
            vhat_s[rows, :] = vhat
            u_s[rows, :] = u
            split_pairs(vhat * lng + lnb, c)
        for p in range(4):
            cols = slice(p * BLOCK, (p + 1) * BLOCK)
            mix = _mm(wf_v[p], rhs_s[p])
            for c in range(nb):
                mix_s[c * BLOCK:(c + 1) * BLOCK, cols] = mix[:, c * BLOCK:(c + 1) * BLOCK] + bexp_ref[:, cols]
        for c in range(nb):
            rows = slice(c * BLOCK, (c + 1) * BLOCK)
            zs = gate_s[3, rows, :]
            mixed_ref[rows, ATTN_W:] = (u_s[rows, :] * mix_s[rows, :] * (zs * _sigmoid(zs))).astype(act)

        g3 = g3_ref[...]
        proj_o = _mm(mixed_ref[...], wout_v[...])
        for c in range(nb):
            rows = slice(c * BLOCK, (c + 1) * BLOCK)
            out = x_ref[rows, :] + proj_o[rows, :] + bout_ref[...]
            r3, on = rms(out)
            e = on * g3 - tgt_ref[rows, :]
            e2 = _rows8(e * e)
            acc_loss[...] += sum(e2[:, k * 128:(k + 1) * 128] for k in range(D_MODEL // 128)) * (0.5 / D_MODEL)
            acc_g3[...] += _rows8(e * on)
            don = e * g3
            dout = (r3 * (1.0 / D_MODEL)) * (don - on * _mean_rows(don * on))
            doutf_s[rows, :] = dout
            dout_ref[rows, :] = dout.astype(act)
            acc_bout[...] += _rows8(dout)
        dmix_s[...] = _mm_nt(dout_ref[...], wout_v[...])

        for c in range(nb):
            rows = slice(c * BLOCK, (c + 1) * BLOCK)
            dso = dmix_s[rows, ATTN_W:]
            u = u_s[rows, :]
            mix = mix_s[rows, :]
            zs = gate_s[3, rows, :]
            sg = _sigmoid(zs)
            sgs = zs * sg
            du = dso * mix * sgs
            dmx = dso * u * sgs
            dzs = dso * u * mix * (sg * (1.0 + zs * (1.0 - sg)))
            us = gate_s[1, rows, :]
            dus = du * (cdfu_s[rows, :] + us * _norm_pdf(us))
            vln = (vhat_s[rows, :] * lng + lnb).astype(act)
            for p in range(4):
                vlnp_s[p, :, c * BLOCK:(c + 1) * BLOCK] = vln[:, p * BLOCK:(p + 1) * BLOCK]
            split_pairs(dmx, c)
            acc_dbs[...] += dmx
            for off, val in ((OFF_US, dus), (OFF_ZS, dzs)):
                dproj_ref[rows, off:off + 512] = val.astype(act)
                acc_bin[:, off:off + 512] += _rows8(val)
        for p in range(4):
            dvln = _mm(wb_v[p], rhs_s[p])
            for c in range(nb):
                dvln_s[c * BLOCK:(c + 1) * BLOCK, p * BLOCK:(p + 1) * BLOCK] = dvln[:, c * BLOCK:(c + 1) * BLOCK]
            acc_dws[(2 * p) * BLOCK:(2 * p + 1) * BLOCK, :] += _mm_nt(rhs_s[p, 0:BLOCK, :], vlnp_s[p])
            acc_dws[(2 * p + 1) * BLOCK:(2 * p + 2) * BLOCK, :] += _mm_nt(rhs_s[p, BLOCK:, :], vlnp_s[p])
        for c in range(nb):
            rows = slice(c * BLOCK, (c + 1) * BLOCK)
            dvln = dvln_s[rows, :]
            vhat = vhat_s[rows, :]
            acc_lng[...] += _rows8(dvln * vhat)
            acc_lnb[...] += _rows8(dvln)
            dvhat = dvln * lng
            dvg = r2_s[rows, :] * (dvhat - _mean_rows(dvhat) - vhat * _mean_rows(dvhat * vhat))
            vs = gate_s[2, rows, :]
            dvs = dvg * (cdfv_s[rows, :] + vs * _norm_pdf(vs))
            dproj_ref[rows, OFF_VS:OFF_VS + 512] = dvs.astype(act)
            acc_bin[:, OFF_VS:OFF_VS + 512] += _rows8(dvs)

        dkf_s[...] = jnp.zeros(dkf_s.shape, F32)
        dvf_s[...] = jnp.zeros(dvf_s.shape, F32)
        for b in range(nb):
            rows = slice(b * BLOCK, (b + 1) * BLOCK)
            band = slice(b * BLOCK, (b + 2) * BLOCK)
            za = gate_s[0, rows, :]
            sg = _sigmoid(za)
            dao = dmix_s[rows, 0:ATTN_W]
            o = o_s[rows, :]
            do = dao * (za * sg)
            dza = dao * o * (sg * (1.0 + za * (1.0 - sg)))
            dproj_ref[rows, OFF_ZA:OFF_US] = dza.astype(act)
            acc_bin[:, OFF_ZA:OFF_US] += _rows8(dza)
            for g in range(2):
                do_parts, delta_parts = [], []
                for pr in range(2):
                    cols = slice((2 * g + pr) * BLOCK, (2 * g + pr + 1) * BLOCK)
                    d_pair = do[:, cols]
                    prod = d_pair * o[:, cols]
                    do_parts += [jnp.where(lo, d_pair, 0.0).astype(act), jnp.where(lo, 0.0, d_pair).astype(act)]
                    delta_parts += [jnp.sum(jnp.where(lo, prod, 0.0), axis=1, keepdims=True),
                                    jnp.sum(jnp.where(lo, 0.0, prod), axis=1, keepdims=True)]
                do_st = jnp.concatenate(do_parts, axis=0)
                delta = jnp.concatenate(delta_parts, axis=0)
                p = p_s[b * 2 + g]
                dp = _mm_nt(do_st, v2_s[g, band, :])
                ds = p * (dp - delta)
                sink_t = ps_s[b * 2 + g] * delta
                for k in range(4):
                    acc_sink[4 * g + k:4 * g + k + 1, :] += -jnp.sum(sink_t[k * BLOCK:(k + 1) * BLOCK], axis=0, keepdims=True)
                ds_a = ds.astype(act)
                dq2 = _mm(ds_a, k2_s[g, band, :]) * SCALE
                for pr in range(2):
                    cols = slice((2 * g + pr) * BLOCK, (2 * g + pr + 1) * BLOCK)
                    dq = jnp.where(lo, dq2[(2 * pr) * BLOCK:(2 * pr + 1) * BLOCK], dq2[(2 * pr + 1) * BLOCK:(2 * pr + 2) * BLOCK])
                    dproj_ref[rows, cols] = dq.astype(act)
                    acc_bin[:, cols] += _rows8(dq)
                for acc, lhs, rhs in ((dkf_s, stacked_q(b, g), ds_a), (dvf_s, do_st, p.astype(act))):
                    d2t = _mm(lhs.T, rhs)
                    acc[g * HEAD_DIM:(g + 1) * HEAD_DIM, band] += d2t[0:HEAD_DIM, :] + d2t[HEAD_DIM:, :]
        for acc, carry, off in ((dkf_s, carryk_s, OFF_K), (dvf_s, carryv_s, OFF_V)):
            acc[:, t:t + BLOCK] += carry[...]
            carry[...] = acc[:, 0:BLOCK]
            d = acc[:, BLOCK:].T
            dproj_ref[:, off:off + BLOCK] = d.astype(act)
            acc_bin[:, off:off + BLOCK] += _rows8(d)

        dh = _mm(dproj_ref[...], wint_v[...])
        dprojt_ref[...] = dproj_ref[...].T
        mixedt_ref[...] = mixed_ref[...].T
        for c in range(nb):
            rows = slice(c * BLOCK, (c + 1) * BLOCK)
            r1 = r1_s[rows, :]
            xn = x_ref[rows, :] * r1
            dhc = dh[rows, :]
            acc_g1[...] += _rows8(dhc * xn)
            dxn = dhc * g1
            dx_ref[rows, :] = doutf_s[rows, :] + r1 * (dxn - xn * _mean_rows(dxn * xn))

        @pl.when(i == nt - 1)
        def _():
            tril = lax.broadcasted_iota(jnp.int32, (BLOCK, BLOCK), 0) >= lane128
            for hh in range(N_SGU_HEADS):
                rws = slice(hh * BLOCK, (hh + 1) * BLOCK)
                sw_ref[rws, :] = jnp.where(tril, acc_dws[rws, :], 0.0)
            vec_ref[...] = jnp.zeros((VEC_ROWS, 128), F32)

            def put(row0, acc, scale=1.0):
                s = jnp.sum(acc[...], axis=0, keepdims=True) * scale
                for k in range(acc.shape[1] // 128):
                    vec_ref[row0 + k:row0 + k + 1, :] = s[:, k * 128:(k + 1) * 128]

            put(R_G1, acc_g1)
            put(R_BIN, acc_bin)
            put(R_LNG, acc_lng)
            put(R_LNB, acc_lnb)
            put(R_BOUT, acc_bout)
            put(R_G3, acc_g3, 1.0 / D_MODEL)
            vec_ref[R_SINK:R_SINK + 1, :] = jnp.sum(
                jnp.where(lax.broadcasted_iota(jnp.int32, (8, 128), 0) == lax.broadcasted_iota(jnp.int32, (8, 128), 1),
                          acc_sink[...], 0.0), axis=0, keepdims=True)
            vec_ref[R_LOSS:R_LOSS + 1, :] = jnp.zeros((1, 128), F32) + jnp.sum(acc_loss[...])
            dbs_t = acc_dbs[...].T
            vec_ref[R_SGUB:R_SGUB + 8, :] = jnp.sum(dbs_t.reshape(N_SGU_HEADS, SGU_W // N_SGU_HEADS, BLOCK), axis=1)

    full = lambda shape: pl.BlockSpec(shape, lambda i: (0,) * len(shape))
    tok = lambda w: pl.BlockSpec((t, w), lambda i: (nt - 1 - i, 0))
    in_specs = [
        pl.BlockSpec(memory_space=pltpu.SMEM),
        tok(D_MODEL), tok(D_MODEL),
        pl.BlockSpec((BLOCK, D_MODEL), lambda i: (jnp.maximum((nt - 1 - i) * nb - 1, 0), 0)),
        tok(1), tok(D_MODEL),
        full((1, D_MODEL)), full((1, IN_W)), full((1, SGU_W)), full((1, SGU_W)),
        full((N_SGU_HEADS, BLOCK, BLOCK)), full((BLOCK, SGU_W)), full((1, D_MODEL)), full((1, D_MODEL)),
        pl.BlockSpec(memory_space=pl.ANY), pl.BlockSpec(memory_space=pl.ANY),
    ]
    out_shape = [
        jax.ShapeDtypeStruct((seq, D_MODEL), F32),
        jax.ShapeDtypeStruct((IN_W, seq), act),
        jax.ShapeDtypeStruct((D_MODEL, seq), act),
        jax.ShapeDtypeStruct((seq, D_MODEL), act),
        jax.ShapeDtypeStruct((SW_ROWS, 128), F32),
        jax.ShapeDtypeStruct((VEC_ROWS, 128), F32),
    ]
    tok_t = lambda w: pl.BlockSpec((w, t), lambda i: (0, nt - 1 - i))
    out_specs = [tok(D_MODEL), tok_t(IN_W), tok_t(D_MODEL), tok(D_MODEL), full((SW_ROWS, 128)), full((VEC_ROWS, 128))]
    vm = pltpu.VMEM
    scratch = [
        vm((t, IN_W), act), vm((t, D_MODEL), act),
        vm((IN_W, D_MODEL), act), vm((D_MODEL, D_MODEL), act),
        vm((4, BLOCK, 2 * BLOCK), act), vm((4, BLOCK, 2 * BLOCK), act),
        vm((t, ATTN_W), act),
        vm((t + BLOCK, BLOCK), F32), vm((t + BLOCK, BLOCK), F32),
        vm((2, t + BLOCK, BLOCK), act), vm((2, t + BLOCK, BLOCK), act),
        vm((4, t, 512), F32),
        vm((2 * nb, 4 * BLOCK, 2 * BLOCK), F32), vm((2 * nb, 4 * BLOCK, 1), F32),
        vm((t, ATTN_W), F32), vm((t, SGU_W), F32), vm((t, SGU_W), F32), vm((t, SGU_W), F32), vm((t, 1), F32),
        vm((t, SGU_W), F32), vm((t, SGU_W), F32),
        vm((t, D_MODEL), F32), vm((t, D_MODEL), F32),
        vm((BLOCK, t + BLOCK), F32), vm((BLOCK, t + BLOCK), F32), vm((BLOCK, BLOCK), F32), vm((BLOCK, BLOCK), F32),
        vm((8, IN_W), F32), vm((8, D_MODEL), F32), vm((8, D_MODEL), F32), vm((8, D_MODEL), F32),
        vm((8, SGU_W), F32), vm((8, SGU_W), F32), vm((N_SGU_HEADS * BLOCK, BLOCK), F32), vm((BLOCK, SGU_W), F32),
        vm((8, 128), F32), vm((8, 128), F32),
        vm((t + BLOCK, D_MODEL), act), vm((4, 2 * BLOCK, t), act), vm((4, BLOCK, t), act), vm((t, SGU_W), F32),
        pltpu.SemaphoreType.DMA((2,)),
    ]
    return pl.pallas_call(
        body, name="fused", grid=(nt,), in_specs=in_specs, out_specs=out_specs, out_shape=out_shape,
        scratch_shapes=scratch,
        compiler_params=pltpu.CompilerParams(dimension_semantics=("arbitrary",), vmem_limit_bytes=VMEM_LIMIT),
    )(sinks, x, h, h, r1, tgt, g1, b_in, ln_g, ln_b, sgu_w, bexp, b_out, g3, win_t, wout)


def _gather_call(a32, b32, x, g1):
    act = MXU_DTYPE
    seq = x.shape[0]
    xc = min(256, seq)
    n_xc = seq // xc

    def body(a32_hbm, b32_hbm, x_hbm, g1_ref, ga_hbm, gb_hbm, h_hbm, r1_hbm, a32_v, b32_v, ga_v, gb_v, x_buf, h_buf, r1_v,
             send_sems, recv_sems, loc_sems, norm_sems):
        x, y, c = lax.axis_index("x"), lax.axis_index("y"), lax.axis_index("c")
        me, sibling, x_nbr, y_nbr = (x, y, c), (x, y, 1 - c), (1 - x, y, c), (x, 1 - y, c)
        j, j_x, j_y, j_d = 2 * x + y, 2 * (1 - x) + y, 2 * x + (1 - y), 2 * (1 - x) + (1 - y)
        arrays = ((0, ga_v, ga_hbm, W_IN_SHARD), (1, gb_v, gb_hbm, W_OUT_SHARD))

        loads = [pltpu.make_async_copy(a32_hbm, a32_v, loc_sems.at[0]), pltpu.make_async_copy(b32_hbm, b32_v, loc_sems.at[1])]
        for cp in loads:
            cp.start()
        for cp in loads:
            cp.wait()
        ga_v[pl.ds(pl.multiple_of(j * W_IN_SHARD, 16), W_IN_SHARD), :] = a32_v[...].astype(act)
        gb_v[pl.ds(pl.multiple_of(j * W_OUT_SHARD, 16), W_OUT_SHARD), :] = b32_v[...].astype(act)

        def rows(shard, chip, hf=None, q=None):
            if hf is None:
                return pl.ds(pl.multiple_of(chip * shard, 16), shard)
            return pl.ds(pl.multiple_of(chip * shard + hf * (shard // 2) + q * (shard // 4), 16), shard // 4)

        def copy(k, ref, at, to):
            return pltpu.make_async_remote_copy(src_ref=ref.at[at, :], dst_ref=ref.at[at, :], send_sem=send_sems.at[k],
                                                recv_sem=recv_sems.at[k], device_id=to, device_id_type=MESH)

        stores = []

        def store(vmem, hbm, at):
            stores.append(pltpu.make_async_copy(vmem.at[at, :], hbm.at[at, :], loc_sems.at[len(stores)]))
            stores[-1].start()

        sent = []
        for w, vmem, hbm, shard in arrays:
            for k, (q, to) in enumerate(((0, x_nbr), (1, x_nbr), (1, y_nbr), (0, y_nbr))):
                sent.append(copy(12 * w + k, vmem, rows(shard, j, c, q), to))
                sent[-1].start()
        for w, vmem, hbm, shard in arrays:
            store(vmem, hbm, rows(shard, j))

        def x_load(n):
            return pltpu.make_async_copy(x_hbm.at[pl.ds(n * xc, xc), :], x_buf.at[n % 2], norm_sems.at[n % 2])

        def h_store(n):
            return pltpu.make_async_copy(h_buf.at[n % 2], h_hbm.at[pl.ds(n * xc, xc), :], norm_sems.at[2 + n % 2])

        g1 = g1_ref[...]
        x_load(0).start()
        for n in range(n_xc):
            if n + 1 < n_xc:
                x_load(n + 1).start()
            x_load(n).wait()
            if n >= 2:
                h_store(n - 2).wait()
            xv = x_buf[n % 2]
            r = lax.rsqrt(_mean_rows(xv * xv) + NORM_EPS)
            r1_v[n * xc:(n + 1) * xc, :] = r
            h_buf[n % 2] = (xv * r * g1).astype(act)
            h_store(n).start()
        for n in range(max(n_xc - 2, 0), n_xc):
            h_store(n).wait()
        r1_out = pltpu.make_async_copy(r1_v, r1_hbm, norm_sems.at[4])
        r1_out.start()

        def landed(w, vmem, hbm, shard, k, chip, q, onward=None):
            at = rows(shard, chip, c, q)
            copy(12 * w + k, vmem, at, me).wait_recv()
            if onward is not None:
                sent.append(copy(12 * w + onward[0], vmem, at, onward[1]))
                sent[-1].start()
            sent.append(copy(12 * w + 6 + k, vmem, at, sibling))
            sent[-1].start()
            store(vmem, hbm, at)

        for arr in arrays:
            landed(*arr, 0, j_x, 0, onward=(4, y_nbr))
            landed(*arr, 2, j_y, 1, onward=(5, x_nbr))
        for arr in arrays:
            landed(*arr, 1, j_x, 1)
            landed(*arr, 3, j_y, 0)
            landed(*arr, 4, j_d, 0)
            landed(*arr, 5, j_d, 1)
        for w, vmem, hbm, shard in arrays:
            for k, (chip, q) in enumerate(((j_x, 0), (j_x, 1), (j_y, 1), (j_y, 0), (j_d, 0), (j_d, 1))):
                at = rows(shard, chip, 1 - c, q)
                copy(12 * w + 6 + k, vmem, at, me).wait_recv()
                store(vmem, hbm, at)
        for cp in sent:
            cp.wait_send()
        for cp in stores:
            cp.wait()
        r1_out.wait()

    hbm = pl.BlockSpec(memory_space=pl.ANY)
    vm = pltpu.VMEM
    return pl.pallas_call(
        body, name="gather", in_specs=[hbm, hbm, hbm, pl.BlockSpec(memory_space=vm)], out_specs=[hbm, hbm, hbm, hbm],
        out_shape=[jax.ShapeDtypeStruct((IN_W, D_MODEL), act), jax.ShapeDtypeStruct((D_MODEL, D_MODEL), act),
                   jax.ShapeDtypeStruct((seq, D_MODEL), act), jax.ShapeDtypeStruct((seq, 1), F32)],
        scratch_shapes=[vm(a32.shape, F32), vm(b32.shape, F32), vm((IN_W, D_MODEL), act), vm((D_MODEL, D_MODEL), act),
                        vm((2, xc, D_MODEL), F32), vm((2, xc, D_MODEL), act), vm((seq, 1), F32),
                        pltpu.SemaphoreType.DMA((24,)), pltpu.SemaphoreType.DMA((24,)), pltpu.SemaphoreType.DMA((26,)),
                        pltpu.SemaphoreType.DMA((5,))],
        compiler_params=pltpu.CompilerParams(vmem_limit_bytes=VMEM_LIMIT),
    )(a32, b32, x, g1)


def _wgrad_reduce_call(dproj_t, h, mixed_t, dout, sw, vec):
    wire = jnp.bfloat16
    half_sw = SW_ROWS // 2
    seq = h.shape[0]
    tk = min(2048, seq)
    nk = seq // tk
    n_steps = 2 * N_CHIPS
    rel_of = lambda s: s % 3 if s < 6 else 3
    half_of = lambda s: s // 3 if s < 6 else s - 6
    x, y = lax.axis_index("x"), lax.axis_index("y")
    chip_of = [2 * (1 - x) + (1 - y), 2 * (1 - x) + y, 2 * x + (1 - y), 2 * x + y]
    order = jnp.stack([2 * chip_of[rel_of(s)] + half_of(s) for s in range(n_steps)]).astype(jnp.int32)

    def body(order_ref, dpt_ref, h_hbm, mxt_ref, dout_hbm, sw_ref, small_ref, oa_ref, ob_ref, osw_ref, osmall_ref,
             h_v, dout_v, acc_a, acc_b, sib_a, sib_b, snd_a, snd_b, in_a, in_b, own_a, own_b, fin_a, fin_b,
             all_small, sw_sib, sw_chips, sw_fin, send_sems, recv_sems, loc_sems):
        x, y, c = lax.axis_index("x"), lax.axis_index("y"), lax.axis_index("c")
        me, sibling = (x, y, c), (x, y, 1 - c)
        steps = [(1 - x, 1 - y), (1 - x, y), (x, 1 - y)]
        j = 2 * x + y
        dev = 4 * x + 2 * y + c
        b, k = pl.program_id(0), pl.program_id(1)

        def copy(n, src, dst, to):
            return pltpu.make_async_remote_copy(src_ref=src, dst_ref=dst, send_sem=send_sems.at[n], recv_sem=recv_sems.at[n],
                                                device_id=to, device_id_type=MESH)

        def sw_rows(ref, hf):
            return ref.at[pl.ds(pl.multiple_of(hf * half_sw, 8), half_sw), :]

        sm_first = [copy(16, small_ref, all_small.at[dev], sibling)]
        sm_first += [copy(17 + r, small_ref, all_small.at[dev], (*chip, c)) for r, chip in enumerate(steps)]
        sw_to_sib = copy(23, sw_rows(sw_ref, 1 - c), sw_sib, sibling)

        @pl.when((b == 0) & (k == 0))
        def _():
            all_small[dev] = small_ref[...]
            for cp in sm_first + [sw_to_sib]:
                cp.start()

        def load(kk):
            rows = pl.ds(kk * tk, tk)
            return [pltpu.make_async_copy(h_hbm.at[rows, :], h_v.at[rows, :], loc_sems.at[kk]),
                    pltpu.make_async_copy(dout_hbm.at[rows, :], dout_v.at[rows, :], loc_sems.at[nk + kk])]

        for kk in range(nk):
            @pl.when((b == 0) & (k == 0))
            def _():
                for cp in load(kk):
                    cp.start()

        for kk in range(nk):
            @pl.when((b == 0) & (k == kk))
            def _():
                for cp in load(kk):
                    cp.wait()

        tok = pl.ds(pl.multiple_of(k * tk, tk), tk)
        pa = _mm(dpt_ref[...], h_v[tok, :])
        pb = _mm(mxt_ref[...], dout_v[tok, :])
        slot = b % 2

        @pl.when(k == 0)
        def _():
            acc_a[slot] = pa
            acc_b[slot] = pb

        @pl.when(k != 0)
        def _():
            acc_a[slot] += pa
            acc_b[slot] += pb

        def to_sibling(s):
            r = rel_of(s)
            return [copy(r, acc_a.at[s % 2], sib_a.at[r], sibling), copy(4 + r, acc_b.at[s % 2], sib_b.at[r], sibling)]

        def chip_partial(s):
            r = rel_of(s)
            copy(r, sib_a.at[r], sib_a.at[r], me).wait_recv()
            copy(4 + r, sib_b.at[r], sib_b.at[r], me).wait_recv()
            return acc_a[s % 2] + sib_a[r], acc_b[s % 2] + sib_b[r]

        def to_owner(r):
            return [copy(8 + r, snd_a.at[r], in_a.at[r], (*steps[r], c)), copy(11 + r, snd_b.at[r], in_b.at[r], (*steps[r], c))]

        def vec_forwards():
            return [copy(20 + r, all_small.at[4 * cx + 2 * cy + c], all_small.at[4 * cx + 2 * cy + c], sibling)
                    for r, (cx, cy) in enumerate(steps)]

        def sw_to_chips():
            return [copy(24 + r, sw_chips.at[j], sw_chips.at[j], (*chip, c)) for r, chip in enumerate(steps)]

        def sw_to_sibling():
            return copy(27, sw_rows(osw_ref, c), sw_fin, sibling)

        @pl.when((b == 1) & (k == 0))
        def _():
            for r, (cx, cy) in enumerate(steps):
                d = 4 * cx + 2 * cy + c
                copy(17 + r, all_small.at[d], all_small.at[d], me).wait_recv()
            for cp in vec_forwards():
                cp.start()
            copy(23, sw_sib, sw_sib, me).wait_recv()
            sw_chips[j] = (sw_rows(sw_ref, c)[...] + sw_sib[...]).astype(wire)
            for cp in sw_to_chips():
                cp.start()

        @pl.when((b == 3) & (k == 0))
        def _():
            for r, (cx, cy) in enumerate(steps):
                cj = 2 * cx + cy
                copy(24 + r, sw_chips.at[cj], sw_chips.at[cj], me).wait_recv()
            tot_sw = sw_chips[0].astype(F32)
            for q in range(1, N_CHIPS):
                tot_sw = tot_sw + sw_chips[q].astype(F32)
            sw_rows(osw_ref, c)[...] = tot_sw
            sw_to_sibling().start()

        late = min(1, nk - 1)
        for s in range(n_steps):
            if s >= 1:
                sp = s - 1

                @pl.when((b == s) & (k == late) & (c == half_of(sp)))
                def _():
                    ta, tb = chip_partial(sp)
                    r = rel_of(sp)
                    if r < 3:
                        snd_a[r] = ta.astype(wire)
                        snd_b[r] = tb.astype(wire)
                        for cp in to_owner(r):
                            cp.start()
                    else:
                        own_a[...] = ta
                        own_b[...] = tb

                @pl.when((b == s) & (k == nk - 1) & (c != half_of(sp)))
                def _():
                    for cp in to_sibling(sp):
                        cp.wait_send()

            @pl.when((b == s) & (k == nk - 1) & (c != half_of(s)))
            def _():
                for cp in to_sibling(s):
                    cp.start()

        @pl.when((b == n_steps - 1) & (k == nk - 1))
        def _():
            last = n_steps - 1

            @pl.when(c == half_of(last))
            def _():
                own_a[...], own_b[...] = chip_partial(last)

            @pl.when(c != half_of(last))
            def _():
                for cp in to_sibling(last):
                    cp.wait_send()

            tot_a = own_a[...]
            tot_b = own_b[...]
            for s in range(3):
                copy(8 + s, in_a.at[s], in_a.at[s], me).wait_recv()
                copy(11 + s, in_b.at[s], in_b.at[s], me).wait_recv()
                tot_a = tot_a + in_a[s].astype(F32)
                tot_b = tot_b + in_b[s].astype(F32)
            mine_a = oa_ref.at[pl.ds(pl.multiple_of(c * HALF_A, 8), HALF_A), :]
            mine_b = ob_ref.at[pl.ds(pl.multiple_of(c * HALF_B, 8), HALF_B), :]
            mine_a[...] = tot_a
            mine_b[...] = tot_b
            back = [copy(14, mine_a, fin_a, sibling), copy(15, mine_b, fin_b, sibling)]
            for cp in back:
                cp.start()

            copy(16, small_ref, all_small.at[dev ^ 1], me).wait_recv()
            for r, (cx, cy) in enumerate(steps):
                d = 4 * cx + 2 * cy + (1 - c)
                copy(20 + r, all_small.at[d], all_small.at[d], me).wait_recv()
            tot = all_small[0]
            for d in range(1, N_DEV):
                tot = tot + all_small[d]
            osmall_ref[...] = tot

            copy(14, fin_a, fin_a, me).wait_recv()
            copy(15, fin_b, fin_b, me).wait_recv()
            copy(27, sw_fin, sw_fin, me).wait_recv()
            oa_ref[pl.ds(pl.multiple_of((1 - c) * HALF_A, 8), HALF_A), :] = fin_a[...]
            ob_ref[pl.ds(pl.multiple_of((1 - c) * HALF_B, 8), HALF_B), :] = fin_b[...]
            sw_rows(osw_ref, 1 - c)[...] = sw_fin[...]
            sends = sm_first + vec_forwards() + sw_to_chips() + back + [sw_to_sib, sw_to_sibling()]
            for s in range(3):
                sends += to_owner(s)
            for cp in sends:
                cp.wait_send()

    vmem = pl.BlockSpec(memory_space=pltpu.VMEM)
    vm = pltpu.VMEM
    grid_spec = pltpu.PrefetchScalarGridSpec(
        num_scalar_prefetch=1, grid=(n_steps, nk),
        in_specs=[pl.BlockSpec((HALF_A, tk), lambda b, k, o: (o[b], k)), pl.BlockSpec(memory_space=pl.ANY),
                  pl.BlockSpec((HALF_B, tk), lambda b, k, o: (o[b], k)), pl.BlockSpec(memory_space=pl.ANY),
                  vmem, vmem],
        out_specs=[vmem, vmem, vmem, vmem],
        scratch_shapes=[vm((seq, D_MODEL), h.dtype), vm((seq, D_MODEL), dout.dtype),
                        vm((2, HALF_A, D_MODEL), F32), vm((2, HALF_B, D_MODEL), F32),
                        vm((N_CHIPS, HALF_A, D_MODEL), F32), vm((N_CHIPS, HALF_B, D_MODEL), F32),
                        vm((3, HALF_A, D_MODEL), wire), vm((3, HALF_B, D_MODEL), wire),
                        vm((3, HALF_A, D_MODEL), wire), vm((3, HALF_B, D_MODEL), wire),
                        vm((HALF_A, D_MODEL), F32), vm((HALF_B, D_MODEL), F32),
                        vm((HALF_A, D_MODEL), F32), vm((HALF_B, D_MODEL), F32),
                        vm((N_DEV, VEC_ROWS, 128), F32), vm((half_sw, 128), F32), vm((N_CHIPS, half_sw, 128), wire),
                        vm((half_sw, 128), F32),
                        pltpu.SemaphoreType.DMA((28,)), pltpu.SemaphoreType.DMA((28,)), pltpu.SemaphoreType.DMA((2 * nk,))])
    return pl.pallas_call(
        body, name="wgrad_reduce", grid_spec=grid_spec,
        out_shape=[jax.ShapeDtypeStruct((W_IN_SHARD, D_MODEL), F32), jax.ShapeDtypeStruct((W_OUT_SHARD, D_MODEL), F32),
                   jax.ShapeDtypeStruct((SW_ROWS, 128), F32), jax.ShapeDtypeStruct((VEC_ROWS, 128), F32)],
        compiler_params=pltpu.CompilerParams(dimension_semantics=("arbitrary", "arbitrary"), vmem_limit_bytes=VMEM_LIMIT),
    )(order, dproj_t, h, mixed_t, dout, sw, vec)


def _adamw(w, g, m, v):
    nm = ADAM_B1 * m + (1.0 - ADAM_B1) * g
    nv = ADAM_B2 * v + (1.0 - ADAM_B2) * (g * g)
    m_hat = nm / (1.0 - ADAM_B1 ** ADAM_STEP)
    v_hat = nv / (1.0 - ADAM_B2 ** ADAM_STEP)
    return -ADAM_LR * (m_hat / (jnp.sqrt(v_hat) + ADAM_EPS) + ADAM_WD * w), nm, nv


def _adamw_shards_call(a, b, steps=4):
    def body(*refs):
        for k in range(2):
            w_ref, g_ref, m_ref, v_ref = refs[4 * k:4 * k + 4]
            go_ref, d_ref, nm_ref, nv_ref = refs[8 + 4 * k:12 + 4 * k]
            gg = g_ref[...]
            go_ref[...] = gg
            d_ref[...], nm_ref[...], nv_ref[...] = _adamw(w_ref[...], gg, m_ref[...], v_ref[...])

    specs, shapes = [], []
    for w in (a[0], b[0]):
        rows, cols = w.shape
        specs += [pl.BlockSpec((rows // steps, cols), lambda i: (i, 0))] * 4
        shapes += [jax.ShapeDtypeStruct((rows, cols), F32)] * 4
    return pl.pallas_call(
        body, name="adamw_shards", grid=(steps,), in_specs=specs, out_specs=specs, out_shape=shapes,
        compiler_params=pltpu.CompilerParams(dimension_semantics=("arbitrary",)),
    )(*a, *b)


_SMALL = (("norm_g", (1, D_MODEL), R_G1), ("b_in", (1, IN_W), R_BIN), ("attn_sinks", (1, 8), R_SINK),
          ("sgu_ln_g", (1, SGU_W), R_LNG), ("sgu_ln_b", (1, SGU_W), R_LNB), ("sgu_b", (N_SGU_HEADS, BLOCK), R_SGUB),
          ("b_out", (1, D_MODEL), R_BOUT), ("final_norm_g", (1, D_MODEL), R_G3))


def _adamw_small_call(sw_g, vec_g, sgu_w3, ws, ms, vs):
    n = len(_SMALL)

    def body(*refs):
        sw_ref, vec_ref = refs[0], refs[1]
        w3 = refs[2:5]
        w_refs, m_refs, v_refs = refs[5:5 + n], refs[5 + n:5 + 2 * n], refs[5 + 2 * n:5 + 3 * n]
        outs = refs[5 + 3 * n:]
        outs[0][...] = vec_ref[R_LOSS:R_LOSS + 1, 0:1]
        g = sw_ref[...]
        outs[1][...] = g
        outs[2][...], outs[3][...], outs[4][...] = _adamw(w3[0][...], g, w3[1][...], w3[2][...])
        for k, (_, shape, row) in enumerate(_SMALL):
            if shape[0] == 1 and shape[1] >= 128:
                g = jnp.concatenate([vec_ref[row + q:row + q + 1, :] for q in range(shape[1] // 128)], axis=1)
            else:
                g = vec_ref[row:row + shape[0], 0:shape[1]]
            o = outs[5 + 4 * k:9 + 4 * k]
            o[0][...] = g
            o[1][...], o[2][...], o[3][...] = _adamw(w_refs[k][...], g, m_refs[k][...], v_refs[k][...])

    vmem = pl.BlockSpec(memory_space=pltpu.VMEM)
    out_shape = [jax.ShapeDtypeStruct((1, 1), F32)] + [jax.ShapeDtypeStruct((SW_ROWS, 128), F32)] * 4
    for _, shape, _ in _SMALL:
        out_shape += [jax.ShapeDtypeStruct(shape, F32)] * 4
    args = [sw_g, vec_g, *sgu_w3, *ws, *ms, *vs]
    return pl.pallas_call(
        body, name="adamw_small", in_specs=[vmem] * len(args), out_specs=[vmem] * len(out_shape), out_shape=out_shape,
    )(*args)


def kernel(x, norm_g, w_in, b_in, attn_sinks, sgu_ln_g, sgu_ln_b, sgu_w, sgu_b, w_out, b_out, final_norm_g, loss_target, m_norm_g, m_w_in, m_b_in, m_attn_sinks, m_sgu_ln_g, m_sgu_ln_b, m_sgu_w, m_sgu_b, m_w_out, m_b_out, m_final_norm_g, v_norm_g, v_w_in, v_b_in, v_attn_sinks, v_sgu_ln_g, v_sgu_ln_b, v_sgu_w, v_sgu_b, v_w_out, v_b_out, v_final_norm_g):
    seq = x.shape[1]
    win_t, wout, h, r1 = _gather_call(w_in[0].T, w_out[0], x[0], norm_g)
    bexp = jnp.repeat(sgu_b[0].T, SGU_W // N_SGU_HEADS, axis=1)
    dx, dproj_t, mixed_t, dout, sw, vec = _fused_call(
        x[0], h, r1, loss_target[0], attn_sinks[0], norm_g, b_in, sgu_ln_g, sgu_ln_b, sgu_w[0], bexp, b_out,
        final_norm_g.reshape(1, D_MODEL), win_t, wout)
    ga_t, g_w_out, sw, vec = _wgrad_reduce_call(dproj_t, h, mixed_t, dout, sw, vec)

    names = ["norm_g", "w_in", "b_in", "attn_sinks", "sgu_ln_g", "sgu_ln_b", "sgu_w", "sgu_b", "w_out", "b_out", "final_norm_g"]
    res = {}
    shards = _adamw_shards_call((w_in[0].T, ga_t, m_w_in[0].T, v_w_in[0].T), (w_out[0], g_w_out, m_w_out[0], v_w_out[0]))
    res["w_in"] = [a.T[None] for a in shards[:4]]
    res["w_out"] = [a[None] for a in shards[4:]]
    given = dict(norm_g=(norm_g, m_norm_g, v_norm_g), b_in=(b_in, m_b_in, v_b_in), attn_sinks=(attn_sinks, m_attn_sinks, v_attn_sinks),
                 sgu_ln_g=(sgu_ln_g, m_sgu_ln_g, v_sgu_ln_g), sgu_ln_b=(sgu_ln_b, m_sgu_ln_b, v_sgu_ln_b),
                 sgu_b=(sgu_b, m_sgu_b, v_sgu_b), b_out=(b_out, m_b_out, v_b_out),
                 final_norm_g=(final_norm_g, m_final_norm_g, v_final_norm_g))
    wmv = [[given[n][k].reshape(shape) for n, shape, _ in _SMALL] for k in range(3)]
    outs = _adamw_small_call(sw, vec, [a.reshape(SW_ROWS, BLOCK) for a in (sgu_w, m_sgu_w, v_sgu_w)], *wmv)
    loss = outs[0].reshape(())
    res["sgu_w"] = [a.reshape(sgu_w.shape) for a in outs[1:5]]
    for k, (n, _, _) in enumerate(_SMALL):
        res[n] = [a.reshape(given[n][0].shape) for a in outs[5 + 4 * k:9 + 4 * k]]

    return (loss, dx[None], *[res[n][0] for n in names], *[res[n][1] for n in names], *[res[n][2] for n in names],
            *[res[n][3] for n in names])
```

```python
import functools
import math

import jax
import jax.numpy as jnp
from jax import lax
from jax.experimental import pallas as pl
from jax.experimental.pallas import tpu as pltpu

F32 = jnp.float32
MXU_DTYPE = jnp.bfloat16

D_MODEL = 1024
HEAD_DIM = 64
ATTN_W = 512
SGU_W = 512
N_SGU_HEADS = 8
BLOCK = 128
IN_W = 2816
OFF_K, OFF_V, OFF_ZA, OFF_US, OFF_VS, OFF_ZS = 512, 640, 768, 1280, 1792, 2304
NORM_EPS = 1e-5
NEG_INF = -1e30
SCALE = HEAD_DIM ** -0.5
SQRT_HALF = math.sqrt(0.5)
INV_SQRT_2PI = 1.0 / math.sqrt(2.0 * math.pi)

N_CHIPS = 4
N_DEV = 8
W_IN_SHARD = IN_W // N_CHIPS
W_OUT_SHARD = D_MODEL // N_CHIPS
HALF_A = W_IN_SHARD // 2
HALF_B = W_OUT_SHARD // 2

TILE = 256
VMEM_LIMIT = 56 * 1024 * 1024

ADAM_LR, ADAM_B1, ADAM_B2, ADAM_EPS, ADAM_WD, ADAM_STEP = 0.001, 0.9, 0.999, 1e-08, 0.01, 10

SW_ROWS = N_SGU_HEADS * BLOCK
R_G1, R_BIN, R_SINK, R_LOSS, R_LNG, R_LNB, R_SGUB, R_BOUT, R_G3 = 0, 8, 32, 40, 48, 56, 64, 72, 80
VEC_ROWS = 88

MESH = pl.DeviceIdType.MESH


def _mm(a, b):
    return jnp.dot(a, b, preferred_element_type=F32)


def _mm_nt(a, b):
    return lax.dot_general(a, b, (((1,), (1,)), ((), ())), preferred_element_type=F32)


def _sigmoid(z):
    return 1.0 / (1.0 + jnp.exp(-z))


def _norm_cdf(z):
    return 0.5 * (1.0 + lax.erf(z * SQRT_HALF))


def _norm_pdf(z):
    return jnp.exp(-0.5 * z * z) * INV_SQRT_2PI


def _rows8(v):
    r, n = v.shape
    return jnp.sum(v.reshape(r // 8, 8, n), axis=0)


def _mean_rows(v):
    return jnp.sum(v, axis=1, keepdims=True) * (1.0 / v.shape[1])


def _fused_call(x, h, r1, tgt, sinks, g1, b_in, ln_g, ln_b, sgu_w, bexp, b_out, g3, win_t, wout):
    seq = x.shape[0]
    t = TILE
    nt = seq // t
    nb = t // BLOCK
    act = MXU_DTYPE

    def body(sinks_ref, x_ref, h_ref, hh_ref, r1_ref, tgt_ref, g1_ref, bin_ref, lng_ref, lnb_ref, sguw_ref, bexp_ref, bout_ref, g3_ref,
             wint_hbm, wout_hbm,
             dx_ref, dprojt_ref, mixedt_ref, dout_ref, sw_ref, vec_ref,
             dproj_ref, mixed_ref, wint_v, wout_v, wf_v, wb_v, q_s, kf_s, vf_s, k2_s, v2_s, gate_s, p_s, ps_s, o_s, u_s, mix_s, vhat_s, r2_s,
             cdfu_s, cdfv_s, doutf_s, dmix_s, dkf_s, dvf_s, carryk_s, carryv_s,
             acc_bin, acc_g1, acc_bout, acc_g3, acc_lng, acc_lnb, acc_dws, acc_dbs, acc_sink, acc_loss,
             h_s, rhs_s, vlnp_s, dvln_s, sems):
        i = pl.program_id(0)
        tile = nt - 1 - i
        lane128 = lax.broadcasted_iota(jnp.int32, (BLOCK, BLOCK), 1)
        lo = lane128 < HEAD_DIM

        @pl.when(i == 0)
        def _():
            cp_a = pltpu.make_async_copy(wint_hbm, wint_v, sems.at[0])
            cp_b = pltpu.make_async_copy(wout_hbm, wout_v, sems.at[1])
            cp_a.start()
            cp_b.start()
            for acc in (acc_bin, acc_g1, acc_bout, acc_g3, acc_lng, acc_lnb, acc_dws, acc_dbs, acc_sink, acc_loss,
                        carryk_s, carryv_s):
                acc[...] = jnp.zeros(acc.shape, F32)
            tril = lax.broadcasted_iota(jnp.int32, (BLOCK, BLOCK), 0) >= lane128
            for h in range(N_SGU_HEADS):
                w = jnp.where(tril, sguw_ref[h], 0.0)
                wf_v[h // 2, :, (h % 2) * BLOCK:(h % 2 + 1) * BLOCK] = w.astype(act)
                wb_v[h // 2, :, (h % 2) * BLOCK:(h % 2 + 1) * BLOCK] = w.T.astype(act)
            cp_a.wait()
            cp_b.wait()

        g1 = g1_ref[...]

        def rms(v):
            r = lax.rsqrt(_mean_rows(v * v) + NORM_EPS)
            return r, v * r

        h_s[0:BLOCK, :] = hh_ref[...]
        h_s[BLOCK:, :] = h_ref[...]
        r1_s = r1_ref

        h = h_ref[...]
        q = _mm_nt(h, wint_v[0:OFF_K, :]) + bin_ref[:, 0:OFF_K]
        q_s[...] = (q * SCALE).astype(act)
        kv = _mm_nt(h_s[...], wint_v[OFF_K:OFF_ZA, :]) + bin_ref[:, OFF_K:OFF_ZA]
        kf_s[...] = kv[:, :BLOCK]
        vf_s[...] = kv[:, BLOCK:]
        for r in range(4):
            cols = slice(OFF_ZA + r * 512, OFF_ZA + (r + 1) * 512)
            gate_s[r] = _mm_nt(h, wint_v[cols, :]) + bin_ref[:, cols]

        lo_kv = lax.broadcasted_iota(jnp.int32, (t + BLOCK, BLOCK), 1) < HEAD_DIM
        for src, dst in ((kf_s, k2_s), (vf_s, v2_s)):
            v = src[...]
            vr = pltpu.roll(v, HEAD_DIM, 1)
            dst[0] = jnp.where(lo_kv, v, vr).astype(act)
            dst[1] = jnp.where(lo_kv, vr, v).astype(act)

        rowi = lax.broadcasted_iota(jnp.int32, (BLOCK, 2 * BLOCK), 0)
        colj = lax.broadcasted_iota(jnp.int32, (BLOCK, 2 * BLOCK), 1)
        in_band = (colj > rowi) & (colj <= rowi + BLOCK)
        row512 = lax.broadcasted_iota(jnp.int32, (4 * BLOCK, 1), 0)

        def stacked_q(b, g):
            parts = []
            for p in range(2):
                slab = q_s[b * BLOCK:(b + 1) * BLOCK, (2 * g + p) * BLOCK:(2 * g + p + 1) * BLOCK]
                parts += [jnp.where(lo, slab, jnp.zeros_like(slab)), jnp.where(lo, jnp.zeros_like(slab), slab)]
            return jnp.concatenate(parts, axis=0)

        def sink_col(g):
            s = [sinks_ref[4 * g + k] for k in range(4)]
            return jnp.where(row512 < BLOCK, s[0], jnp.where(row512 < 2 * BLOCK, s[1], jnp.where(row512 < 3 * BLOCK, s[2], s[3])))

        for b in range(nb):
            band = slice(b * BLOCK, (b + 2) * BLOCK)
            first_key = jnp.where(tile * nb + b > 0, 0, BLOCK)
            valid = in_band & (colj >= first_key)
            valid4 = jnp.concatenate([valid] * 4, axis=0)
            for g in range(2):
                s = _mm_nt(stacked_q(b, g), k2_s[g, band, :])
                s = jnp.where(valid4, s, NEG_INF)
                sk = sink_col(g)
                m = jnp.maximum(jnp.max(s, axis=1, keepdims=True), sk)
                p = jnp.exp(s - m)
                psk = jnp.exp(sk - m)
                inv = 1.0 / (jnp.sum(p, axis=1, keepdims=True) + psk)
                p = p * inv
                p_s[b * 2 + g] = p
                ps_s[b * 2 + g] = psk * inv
                o2 = _mm(p.astype(act), v2_s[g, band, :])
                for pr in range(2):
                    o_s[b * BLOCK:(b + 1) * BLOCK, (2 * g + pr) * BLOCK:(2 * g + pr + 1) * BLOCK] = jnp.where(
                        lo, o2[(2 * pr) * BLOCK:(2 * pr + 1) * BLOCK], o2[(2 * pr + 1) * BLOCK:(2 * pr + 2) * BLOCK])

        lng = lng_ref[...]
        lnb = lnb_ref[...]

        def split_pairs(val, c):
            for p in range(4):
                slab = val[:, p * BLOCK:(p + 1) * BLOCK]
                rhs_s[p, 0:BLOCK, c * BLOCK:(c + 1) * BLOCK] = jnp.where(lo, slab, 0.0).astype(act)
                rhs_s[p, BLOCK:, c * BLOCK:(c + 1) * BLOCK] = jnp.where(lo, 0.0, slab).astype(act)

        for c in range(nb):
            rows = slice(c * BLOCK, (c + 1) * BLOCK)
            za = gate_s[0, rows, :]
            mixed_ref[rows, 0:ATTN_W] = (o_s[rows, :] * (za * _sigmoid(za))).astype(act)
            us = gate_s[1, rows, :]
            vs = gate_s[2, rows, :]
            cu = _norm_cdf(us)
            cv = _norm_cdf(vs)
            cdfu_s[rows, :] = cu
            cdfv_s[rows, :] = cv
            u = us * cu
            vg = vs * cv
            vc = vg - _mean_rows(vg)
            r2 = lax.rsqrt(_mean_rows(vc * vc) + NORM_EPS)
            vhat = vc * r2
            r2_s[rows, :] = r2
            vhat_s[rows, :] = vhat
            u_s[rows, :] = u
            split_pairs(vhat * lng + lnb, c)
        for p in range(4):
            cols = slice(p * BLOCK, (p + 1) * BLOCK)
            mix = _mm(wf_v[p], rhs_s[p])
            for c in range(nb):
                mix_s[c * BLOCK:(c + 1) * BLOCK, cols] = mix[:, c * BLOCK:(c + 1) * BLOCK] + bexp_ref[:, cols]
        for c in range(nb):
            rows = slice(c * BLOCK, (c + 1) * BLOCK)
            zs = gate_s[3, rows, :]
            mixed_ref[rows, ATTN_W:] = (u_s[rows, :] * mix_s[rows, :] * (zs * _sigmoid(zs))).astype(act)

        g3 = g3_ref[...]
        proj_o = _mm(mixed_ref[...], wout_v[...])
        for c in range(nb):
            rows = slice(c * BLOCK, (c + 1) * BLOCK)
            out = x_ref[rows, :] + proj_o[rows, :] + bout_ref[...]
            r3, on = rms(out)
            e = on * g3 - tgt_ref[rows, :]
            e2 = _rows8(e * e)
            acc_loss[...] += sum(e2[:, k * 128:(k + 1) * 128] for k in range(D_MODEL // 128)) * (0.5 / D_MODEL)
            acc_g3[...] += _rows8(e * on)
            don = e * g3
            dout = (r3 * (1.0 / D_MODEL)) * (don - on * _mean_rows(don * on))
            doutf_s[rows, :] = dout
            dout_ref[rows, :] = dout.astype(act)
            acc_bout[...] += _rows8(dout)
        dmix_s[...] = _mm_nt(dout_ref[...], wout_v[...])

        for c in range(nb):
            rows = slice(c * BLOCK, (c + 1) * BLOCK)
            dso = dmix_s[rows, ATTN_W:]
            u = u_s[rows, :]
            mix = mix_s[rows, :]
            zs = gate_s[3, rows, :]
            sg = _sigmoid(zs)
            sgs = zs * sg
            du = dso * mix * sgs
            dmx = dso * u * sgs
            dzs = dso * u * mix * (sg * (1.0 + zs * (1.0 - sg)))
            us = gate_s[1, rows, :]
            dus = du * (cdfu_s[rows, :] + us * _norm_pdf(us))
            vln = (vhat_s[rows, :] * lng + lnb).astype(act)
            for p in range(4):
                vlnp_s[p, :, c * BLOCK:(c + 1) * BLOCK] = vln[:, p * BLOCK:(p + 1) * BLOCK]
            split_pairs(dmx, c)
            acc_dbs[...] += dmx
            for off, val in ((OFF_US, dus), (OFF_ZS, dzs)):
                dproj_ref[rows, off:off + 512] = val.astype(act)
                acc_bin[:, off:off + 512] += _rows8(val)
        for p in range(4):
            dvln = _mm(wb_v[p], rhs_s[p])
            for c in range(nb):
                dvln_s[c * BLOCK:(c + 1) * BLOCK, p * BLOCK:(p + 1) * BLOCK] = dvln[:, c * BLOCK:(c + 1) * BLOCK]
            acc_dws[(2 * p) * BLOCK:(2 * p + 1) * BLOCK, :] += _mm_nt(rhs_s[p, 0:BLOCK, :], vlnp_s[p])
            acc_dws[(2 * p + 1) * BLOCK:(2 * p + 2) * BLOCK, :] += _mm_nt(rhs_s[p, BLOCK:, :], vlnp_s[p])
        for c in range(nb):
            rows = slice(c * BLOCK, (c + 1) * BLOCK)
            dvln = dvln_s[rows, :]
            vhat = vhat_s[rows, :]
            acc_lng[...] += _rows8(dvln * vhat)
            acc_lnb[...] += _rows8(dvln)
            dvhat = dvln * lng
            dvg = r2_s[rows, :] * (dvhat - _mean_rows(dvhat) - vhat * _mean_rows(dvhat * vhat))
            vs = gate_s[2, rows, :]
            dvs = dvg * (cdfv_s[rows, :] + vs * _norm_pdf(vs))
            dproj_ref[rows, OFF_VS:OFF_VS + 512] = dvs.astype(act)
            acc_bin[:, OFF_VS:OFF_VS + 512] += _rows8(dvs)

        dkf_s[...] = jnp.zeros(dkf_s.shape, F32)
        dvf_s[...] = jnp.zeros(dvf_s.shape, F32)
        for b in range(nb):
            rows = slice(b * BLOCK, (b + 1) * BLOCK)
            band = slice(b * BLOCK, (b + 2) * BLOCK)
            za = gate_s[0, rows, :]
            sg = _sigmoid(za)
            dao = dmix_s[rows, 0:ATTN_W]
            o = o_s[rows, :]
            do = dao * (za * sg)
            dza = dao * o * (sg * (1.0 + za * (1.0 - sg)))
            dproj_ref[rows, OFF_ZA:OFF_US] = dza.astype(act)
            acc_bin[:, OFF_ZA:OFF_US] += _rows8(dza)
            for g in range(2):
                do_parts, delta_parts = [], []
                for pr in range(2):
                    cols = slice((2 * g + pr) * BLOCK, (2 * g + pr + 1) * BLOCK)
                    d_pair = do[:, cols]
                    prod = d_pair * o[:, cols]
                    do_parts += [jnp.where(lo, d_pair, 0.0).astype(act), jnp.where(lo, 0.0, d_pair).astype(act)]
                    delta_parts += [jnp.sum(jnp.where(lo, prod, 0.0), axis=1, keepdims=True),
                                    jnp.sum(jnp.where(lo, 0.0, prod), axis=1, keepdims=True)]
                do_st = jnp.concatenate(do_parts, axis=0)
                delta = jnp.concatenate(delta_parts, axis=0)
                p = p_s[b * 2 + g]
                dp = _mm_nt(do_st, v2_s[g, band, :])
                ds = p * (dp - delta)
                sink_t = ps_s[b * 2 + g] * delta
                for k in range(4):
                    acc_sink[4 * g + k:4 * g + k + 1, :] += -jnp.sum(sink_t[k * BLOCK:(k + 1) * BLOCK], axis=0, keepdims=True)
                ds_a = ds.astype(act)
                dq2 = _mm(ds_a, k2_s[g, band, :]) * SCALE
                for pr in range(2):
                    cols = slice((2 * g + pr) * BLOCK, (2 * g + pr + 1) * BLOCK)
                    dq = jnp.where(lo, dq2[(2 * pr) * BLOCK:(2 * pr + 1) * BLOCK], dq2[(2 * pr + 1) * BLOCK:(2 * pr + 2) * BLOCK])
                    dproj_ref[rows, cols] = dq.astype(act)
                    acc_bin[:, cols] += _rows8(dq)
                for acc, lhs, rhs in ((dkf_s, stacked_q(b, g), ds_a), (dvf_s, do_st, p.astype(act))):
                    d2t = _mm(lhs.T, rhs)
                    acc[g * HEAD_DIM:(g + 1) * HEAD_DIM, band] += d2t[0:HEAD_DIM, :] + d2t[HEAD_DIM:, :]
        for acc, carry, off in ((dkf_s, carryk_s, OFF_K), (dvf_s, carryv_s, OFF_V)):
            acc[:, t:t + BLOCK] += carry[...]
            carry[...] = acc[:, 0:BLOCK]
            d = acc[:, BLOCK:].T
            dproj_ref[:, off:off + BLOCK] = d.astype(act)
            acc_bin[:, off:off + BLOCK] += _rows8(d)

        dh = _mm(dproj_ref[...], wint_v[...])
        dprojt_ref[...] = dproj_ref[...].T
        mixedt_ref[...] = mixed_ref[...].T
        for c in range(nb):
            rows = slice(c * BLOCK, (c + 1) * BLOCK)
            r1 = r1_s[rows, :]
            xn = x_ref[rows, :] * r1
            dhc = dh[rows, :]
            acc_g1[...] += _rows8(dhc * xn)
            dxn = dhc * g1
            dx_ref[rows, :] = doutf_s[rows, :] + r1 * (dxn - xn * _mean_rows(dxn * xn))

        @pl.when(i == nt - 1)
        def _():
            tril = lax.broadcasted_iota(jnp.int32, (BLOCK, BLOCK), 0) >= lane128
            for hh in range(N_SGU_HEADS):
                rws = slice(hh * BLOCK, (hh + 1) * BLOCK)
                sw_ref[rws, :] = jnp.where(tril, acc_dws[rws, :], 0.0)
            vec_ref[...] = jnp.zeros((VEC_ROWS, 128), F32)

            def put(row0, acc, scale=1.0):
                s = jnp.sum(acc[...], axis=0, keepdims=True) * scale
                for k in range(acc.shape[1] // 128):
                    vec_ref[row0 + k:row0 + k + 1, :] = s[:, k * 128:(k + 1) * 128]

            put(R_G1, acc_g1)
            put(R_BIN, acc_bin)
            put(R_LNG, acc_lng)
            put(R_LNB, acc_lnb)
            put(R_BOUT, acc_bout)
            put(R_G3, acc_g3, 1.0 / D_MODEL)
            vec_ref[R_SINK:R_SINK + 1, :] = jnp.sum(
                jnp.where(lax.broadcasted_iota(jnp.int32, (8, 128), 0) == lax.broadcasted_iota(jnp.int32, (8, 128), 1),
                          acc_sink[...], 0.0), axis=0, keepdims=True)
            vec_ref[R_LOSS:R_LOSS + 1, :] = jnp.zeros((1, 128), F32) + jnp.sum(acc_loss[...])
            dbs_t = acc_dbs[...].T
            vec_ref[R_SGUB:R_SGUB + 8, :] = jnp.sum(dbs_t.reshape(N_SGU_HEADS, SGU_W // N_SGU_HEADS, BLOCK), axis=1)

    full = lambda shape: pl.BlockSpec(shape, lambda i: (0,) * len(shape))
    tok = lambda w: pl.BlockSpec((t, w), lambda i: (nt - 1 - i, 0))
    in_specs = [
        pl.BlockSpec(memory_space=pltpu.SMEM),
        tok(D_MODEL), tok(D_MODEL),
        pl.BlockSpec((BLOCK, D_MODEL), lambda i: (jnp.maximum((nt - 1 - i) * nb - 1, 0), 0)),
        tok(1), tok(D_MODEL),
        full((1, D_MODEL)), full((1, IN_W)), full((1, SGU_W)), full((1, SGU_W)),
        full((N_SGU_HEADS, BLOCK, BLOCK)), full((BLOCK, SGU_W)), full((1, D_MODEL)), full((1, D_MODEL)),
        pl.BlockSpec(memory_space=pl.ANY), pl.BlockSpec(memory_space=pl.ANY),
    ]
    out_shape = [
        jax.ShapeDtypeStruct((seq, D_MODEL), F32),
        jax.ShapeDtypeStruct((IN_W, seq), act),
        jax.ShapeDtypeStruct((D_MODEL, seq), act),
        jax.ShapeDtypeStruct((seq, D_MODEL), act),
        jax.ShapeDtypeStruct((SW_ROWS, 128), F32),
        jax.ShapeDtypeStruct((VEC_ROWS, 128), F32),
    ]
    tok_t = lambda w: pl.BlockSpec((w, t), lambda i: (0, nt - 1 - i))
    out_specs = [tok(D_MODEL), tok_t(IN_W), tok_t(D_MODEL), tok(D_MODEL), full((SW_ROWS, 128)), full((VEC_ROWS, 128))]
    vm = pltpu.VMEM
    scratch = [
        vm((t, IN_W), act), vm((t, D_MODEL), act),
        vm((IN_W, D_MODEL), act), vm((D_MODEL, D_MODEL), act),
        vm((4, BLOCK, 2 * BLOCK), act), vm((4, BLOCK, 2 * BLOCK), act),
        vm((t, ATTN_W), act),
        vm((t + BLOCK, BLOCK), F32), vm((t + BLOCK, BLOCK), F32),
        vm((2, t + BLOCK, BLOCK), act), vm((2, t + BLOCK, BLOCK), act),
        vm((4, t, 512), F32),
        vm((2 * nb, 4 * BLOCK, 2 * BLOCK), F32), vm((2 * nb, 4 * BLOCK, 1), F32),
        vm((t, ATTN_W), F32), vm((t, SGU_W), F32), vm((t, SGU_W), F32), vm((t, SGU_W), F32), vm((t, 1), F32),
        vm((t, SGU_W), F32), vm((t, SGU_W), F32),
        vm((t, D_MODEL), F32), vm((t, D_MODEL), F32),
        vm((BLOCK, t + BLOCK), F32), vm((BLOCK, t + BLOCK), F32), vm((BLOCK, BLOCK), F32), vm((BLOCK, BLOCK), F32),
        vm((8, IN_W), F32), vm((8, D_MODEL), F32), vm((8, D_MODEL), F32), vm((8, D_MODEL), F32),
        vm((8, SGU_W), F32), vm((8, SGU_W), F32), vm((N_SGU_HEADS * BLOCK, BLOCK), F32), vm((BLOCK, SGU_W), F32),
        vm((8, 128), F32), vm((8, 128), F32),
        vm((t + BLOCK, D_MODEL), act), vm((4, 2 * BLOCK, t), act), vm((4, BLOCK, t), act), vm((t, SGU_W), F32),
        pltpu.SemaphoreType.DMA((2,)),
    ]
    return pl.pallas_call(
        body, name="fused", grid=(nt,), in_specs=in_specs, out_specs=out_specs, out_shape=out_shape,
        scratch_shapes=scratch,
        compiler_params=pltpu.CompilerParams(dimension_semantics=("arbitrary",), vmem_limit_bytes=VMEM_LIMIT),
    )(sinks, x, h, h, r1, tgt, g1, b_in, ln_g, ln_b, sgu_w, bexp, b_out, g3, win_t, wout)


def _gather_call(a32, b32, x, g1):
    act = MXU_DTYPE
    seq = x.shape[0]
    xc = min(256, seq)
    n_xc = seq // xc

    def body(a32_hbm, b32_hbm, x_hbm, g1_ref, ga_hbm, gb_hbm, h_hbm, r1_hbm, a32_v, b32_v, ga_v, gb_v, x_buf, h_buf, r1_v,
             send_sems, recv_sems, loc_sems, norm_sems):
        x, y, c = lax.axis_index("x"), lax.axis_index("y"), lax.axis_index("c")
        me, sibling, x_nbr, y_nbr = (x, y, c), (x, y, 1 - c), (1 - x, y, c), (x, 1 - y, c)
        j, j_x, j_y, j_d = 2 * x + y, 2 * (1 - x) + y, 2 * x + (1 - y), 2 * (1 - x) + (1 - y)
        arrays = ((0, ga_v, ga_hbm, W_IN_SHARD), (1, gb_v, gb_hbm, W_OUT_SHARD))

        loads = [pltpu.make_async_copy(a32_hbm, a32_v, loc_sems.at[0]), pltpu.make_async_copy(b32_hbm, b32_v, loc_sems.at[1])]
        for cp in loads:
            cp.start()
        for cp in loads:
            cp.wait()
        ga_v[pl.ds(pl.multiple_of(j * W_IN_SHARD, 16), W_IN_SHARD), :] = a32_v[...].astype(act)
        gb_v[pl.ds(pl.multiple_of(j * W_OUT_SHARD, 16), W_OUT_SHARD), :] = b32_v[...].astype(act)

        def rows(shard, chip, hf=None, q=None):
            if hf is None:
                return pl.ds(pl.multiple_of(chip * shard, 16), shard)
            return pl.ds(pl.multiple_of(chip * shard + hf * (shard // 2) + q * (shard // 4), 16), shard // 4)

        def copy(k, ref, at, to):
            return pltpu.make_async_remote_copy(src_ref=ref.at[at, :], dst_ref=ref.at[at, :], send_sem=send_sems.at[k],
                                                recv_sem=recv_sems.at[k], device_id=to, device_id_type=MESH)

        stores = []

        def store(vmem, hbm, at):
            stores.append(pltpu.make_async_copy(vmem.at[at, :], hbm.at[at, :], loc_sems.at[len(stores)]))
            stores[-1].start()

        sent = []
        for w, vmem, hbm, shard in arrays:
            for k, (q, to) in enumerate(((0, x_nbr), (1, x_nbr), (1, y_nbr), (0, y_nbr))):
                sent.append(copy(12 * w + k, vmem, rows(shard, j, c, q), to))
                sent[-1].start()
        for w, vmem, hbm, shard in arrays:
            store(vmem, hbm, rows(shard, j))

        def x_load(n):
            return pltpu.make_async_copy(x_hbm.at[pl.ds(n * xc, xc), :], x_buf.at[n % 2], norm_sems.at[n % 2])

        def h_store(n):
            return pltpu.make_async_copy(h_buf.at[n % 2], h_hbm.at[pl.ds(n * xc, xc), :], norm_sems.at[2 + n % 2])

        g1 = g1_ref[...]
        x_load(0).start()
        for n in range(n_xc):
            if n + 1 < n_xc:
                x_load(n + 1).start()
            x_load(n).wait()
            if n >= 2:
                h_store(n - 2).wait()
            xv = x_buf[n % 2]
            r = lax.rsqrt(_mean_rows(xv * xv) + NORM_EPS)
            r1_v[n * xc:(n + 1) * xc, :] = r
            h_buf[n % 2] = (xv * r * g1).astype(act)
            h_store(n).start()
        for n in range(max(n_xc - 2, 0), n_xc):
            h_store(n).wait()
        r1_out = pltpu.make_async_copy(r1_v, r1_hbm, norm_sems.at[4])
        r1_out.start()

        def landed(w, vmem, hbm, shard, k, chip, q, onward=None):
            at = rows(shard, chip, c, q)
            copy(12 * w + k, vmem, at, me).wait_recv()
            if onward is not None:
                sent.append(copy(12 * w + onward[0], vmem, at, onward[1]))
                sent[-1].start()
            sent.append(copy(12 * w + 6 + k, vmem, at, sibling))
            sent[-1].start()
            store(vmem, hbm, at)

        for arr in arrays:
            landed(*arr, 0, j_x, 0, onward=(4, y_nbr))
            landed(*arr, 2, j_y, 1, onward=(5, x_nbr))
        for arr in arrays:
            landed(*arr, 1, j_x, 1)
            landed(*arr, 3, j_y, 0)
            landed(*arr, 4, j_d, 0)
            landed(*arr, 5, j_d, 1)
        for w, vmem, hbm, shard in arrays:
            for k, (chip, q) in enumerate(((j_x, 0), (j_x, 1), (j_y, 1), (j_y, 0), (j_d, 0), (j_d, 1))):
                at = rows(shard, chip, 1 - c, q)
                copy(12 * w + 6 + k, vmem, at, me).wait_recv()
                store(vmem, hbm, at)
        for cp in sent:
            cp.wait_send()
        for cp in stores:
            cp.wait()
        r1_out.wait()

    hbm = pl.BlockSpec(memory_space=pl.ANY)
    vm = pltpu.VMEM
    return pl.pallas_call(
        body, name="gather", in_specs=[hbm, hbm, hbm, pl.BlockSpec(memory_space=vm)], out_specs=[hbm, hbm, hbm, hbm],
        out_shape=[jax.ShapeDtypeStruct((IN_W, D_MODEL), act), jax.ShapeDtypeStruct((D_MODEL, D_MODEL), act),
                   jax.ShapeDtypeStruct((seq, D_MODEL), act), jax.ShapeDtypeStruct((seq, 1), F32)],
        scratch_shapes=[vm(a32.shape, F32), vm(b32.shape, F32), vm((IN_W, D_MODEL), act), vm((D_MODEL, D_MODEL), act),
                        vm((2, xc, D_MODEL), F32), vm((2, xc, D_MODEL), act), vm((seq, 1), F32),
                        pltpu.SemaphoreType.DMA((24,)), pltpu.SemaphoreType.DMA((24,)), pltpu.SemaphoreType.DMA((26,)),
                        pltpu.SemaphoreType.DMA((5,))],
        compiler_params=pltpu.CompilerParams(vmem_limit_bytes=VMEM_LIMIT),
    )(a32, b32, x, g1)


def _wgrad_reduce_call(dproj_t, h, mixed_t, dout, sw, vec):
    wire = jnp.bfloat16
    half_sw = SW_ROWS // 2
    seq = h.shape[0]
    tk = min(2048, seq)
    nk = seq // tk
    n_steps = 2 * N_CHIPS
    rel_of = lambda s: s % 3 if s < 6 else 3
    half_of = lambda s: s // 3 if s < 6 else s - 6
    x, y = lax.axis_index("x"), lax.axis_index("y")
    chip_of = [2 * (1 - x) + (1 - y), 2 * (1 - x) + y, 2 * x + (1 - y), 2 * x + y]
    order = jnp.stack([2 * chip_of[rel_of(s)] + half_of(s) for s in range(n_steps)]).astype(jnp.int32)

    def body(order_ref, dpt_ref, h_hbm, mxt_ref, dout_hbm, sw_ref, small_ref, oa_ref, ob_ref, osw_ref, osmall_ref,
             h_v, dout_v, acc_a, acc_b, sib_a, sib_b, snd_a, snd_b, in_a, in_b, own_a, own_b, fin_a, fin_b,
             all_small, sw_sib, sw_chips, sw_fin, mrg_xa, mrg_xb, mrg_ya, mrg_yb, send_sems, recv_sems, loc_sems):
        x, y, c = lax.axis_index("x"), lax.axis_index("y"), lax.axis_index("c")
        me, sibling = (x, y, c), (x, y, 1 - c)
        steps = [(1 - x, 1 - y), (1 - x, y), (x, 1 - y)]
        j = 2 * x + y
        dev = 4 * x + 2 * y + c
        b, k = pl.program_id(0), pl.program_id(1)

        def copy(n, src, dst, to):
            return pltpu.make_async_remote_copy(src_ref=src, dst_ref=dst, send_sem=send_sems.at[n], recv_sem=recv_sems.at[n],
                                                device_id=to, device_id_type=MESH)

        def sw_rows(ref, hf):
            return ref.at[pl.ds(pl.multiple_of(hf * half_sw, 8), half_sw), :]

        sm_first = [copy(16, small_ref, all_small.at[dev], sibling)]
        sm_first += [copy(17 + r, small_ref, all_small.at[dev], (*chip, c)) for r, chip in enumerate(steps)]
        sw_to_sib = copy(23, sw_rows(sw_ref, 1 - c), sw_sib, sibling)

        @pl.when((b == 0) & (k == 0))
        def _():
            all_small[dev] = small_ref[...]
            for cp in sm_first + [sw_to_sib]:
                cp.start()

        def load(kk):
            rows = pl.ds(kk * tk, tk)
            return [pltpu.make_async_copy(h_hbm.at[rows, :], h_v.at[rows, :], loc_sems.at[kk]),
                    pltpu.make_async_copy(dout_hbm.at[rows, :], dout_v.at[rows, :], loc_sems.at[nk + kk])]

        for kk in range(nk):
            @pl.when((b == 0) & (k == 0))
            def _():
                for cp in load(kk):
                    cp.start()

        for kk in range(nk):
            @pl.when((b == 0) & (k == kk))
            def _():
                for cp in load(kk):
                    cp.wait()

        tok = pl.ds(pl.multiple_of(k * tk, tk), tk)
        pa = _mm(dpt_ref[...], h_v[tok, :])
        pb = _mm(mxt_ref[...], dout_v[tok, :])
        slot = b % 2

        @pl.when(k == 0)
        def _():
            acc_a[slot] = pa
            acc_b[slot] = pb

        @pl.when(k != 0)
        def _():
            acc_a[slot] += pa
            acc_b[slot] += pb

        def to_sibling(s):
            r = rel_of(s)
            return [copy(r, acc_a.at[s % 2], sib_a.at[r], sibling), copy(4 + r, acc_b.at[s % 2], sib_b.at[r], sibling)]

        def chip_partial(s):
            r = rel_of(s)
            copy(r, sib_a.at[r], sib_a.at[r], me).wait_recv()
            copy(4 + r, sib_b.at[r], sib_b.at[r], me).wait_recv()
            return acc_a[s % 2] + sib_a[r], acc_b[s % 2] + sib_b[r]

        def to_owner(r):
            return [copy(8 + r, snd_a.at[r], in_a.at[r], (*steps[r], c)), copy(11 + r, snd_b.at[r], in_b.at[r], (*steps[r], c))]

        qa, qb = HALF_A // 2, HALF_B // 2

        def to_neighbours():
            return [copy(8, snd_a.at[0, 0:qa, :], mrg_xa, (1 - x, y, c)), copy(11, snd_b.at[0, 0:qb, :], mrg_xb, (1 - x, y, c)),
                    copy(28, snd_a.at[0, qa:, :], mrg_ya, (x, 1 - y, c)), copy(29, snd_b.at[0, qb:, :], mrg_yb, (x, 1 - y, c))]

        def merged(r, ta, tb):
            if r == 1:
                copy(28, mrg_ya, mrg_ya, me).wait_recv()
                copy(29, mrg_yb, mrg_yb, me).wait_recv()
                return (jnp.concatenate([ta[0:qa], ta[qa:] + mrg_ya[...].astype(F32)], axis=0),
                        jnp.concatenate([tb[0:qb], tb[qb:] + mrg_yb[...].astype(F32)], axis=0))
            copy(8, mrg_xa, mrg_xa, me).wait_recv()
            copy(11, mrg_xb, mrg_xb, me).wait_recv()
            return (jnp.concatenate([ta[0:qa] + mrg_xa[...].astype(F32), ta[qa:]], axis=0),
                    jnp.concatenate([tb[0:qb] + mrg_xb[...].astype(F32), tb[qb:]], axis=0))

        def vec_forwards():
            return [copy(20 + r, all_small.at[4 * cx + 2 * cy + c], all_small.at[4 * cx + 2 * cy + c], sibling)
                    for r, (cx, cy) in enumerate(steps)]

        def sw_to_chips():
            return [copy(24 + r, sw_chips.at[j], sw_chips.at[j], (*chip, c)) for r, chip in enumerate(steps)]

        def sw_to_sibling():
            return copy(27, sw_rows(osw_ref, c), sw_fin, sibling)

        @pl.when((b == 1) & (k == 0))
        def _():
            for r, (cx, cy) in enumerate(steps):
                d = 4 * cx + 2 * cy + c
                copy(17 + r, all_small.at[d], all_small.at[d], me).wait_recv()
            for cp in vec_forwards():
                cp.start()
            copy(23, sw_sib, sw_sib, me).wait_recv()
            sw_chips[j] = (sw_rows(sw_ref, c)[...] + sw_sib[...]).astype(wire)
            for cp in sw_to_chips():
                cp.start()

        @pl.when((b == 3) & (k == 0))
        def _():
            for r, (cx, cy) in enumerate(steps):
                cj = 2 * cx + cy
                copy(24 + r, sw_chips.at[cj], sw_chips.at[cj], me).wait_recv()
            tot_sw = sw_chips[0].astype(F32)
            for q in range(1, N_CHIPS):
                tot_sw = tot_sw + sw_chips[q].astype(F32)
            sw_rows(osw_ref, c)[...] = tot_sw
            sw_to_sibling().start()

        late = min(1, nk - 1)
        for s in range(n_steps):
            if s >= 1:
                sp = s - 1

                @pl.when((b == s) & (k == late) & (c == half_of(sp)))
                def _():
                    ta, tb = chip_partial(sp)
                    r = rel_of(sp)
                    if r == 3:
                        own_a[...] = ta
                        own_b[...] = tb
                    else:
                        if r > 0:
                            ta, tb = merged(r, ta, tb)
                        snd_a[r] = ta.astype(wire)
                        snd_b[r] = tb.astype(wire)
                        for cp in (to_owner(r) if r > 0 else to_neighbours()):
                            cp.start()

                @pl.when((b == s) & (k == nk - 1) & (c != half_of(sp)))
                def _():
                    for cp in to_sibling(sp):
                        cp.wait_send()

            @pl.when((b == s) & (k == nk - 1) & (c != half_of(s)))
            def _():
                for cp in to_sibling(s):
                    cp.start()

        @pl.when((b == n_steps - 1) & (k == nk - 1))
        def _():
            last = n_steps - 1

            @pl.when(c == half_of(last))
            def _():
                own_a[...], own_b[...] = chip_partial(last)

            @pl.when(c != half_of(last))
            def _():
                for cp in to_sibling(last):
                    cp.wait_send()

            tot_a = own_a[...]
            tot_b = own_b[...]
            for s in (1, 2):
                copy(8 + s, in_a.at[s], in_a.at[s], me).wait_recv()
                copy(11 + s, in_b.at[s], in_b.at[s], me).wait_recv()
                tot_a = tot_a + in_a[s].astype(F32)
                tot_b = tot_b + in_b[s].astype(F32)
            mine_a = oa_ref.at[pl.ds(pl.multiple_of(c * HALF_A, 8), HALF_A), :]
            mine_b = ob_ref.at[pl.ds(pl.multiple_of(c * HALF_B, 8), HALF_B), :]
            mine_a[...] = tot_a
            mine_b[...] = tot_b
            back = [copy(14, mine_a, fin_a, sibling), copy(15, mine_b, fin_b, sibling)]
            for cp in back:
                cp.start()

            copy(16, small_ref, all_small.at[dev ^ 1], me).wait_recv()
            for r, (cx, cy) in enumerate(steps):
                d = 4 * cx + 2 * cy + (1 - c)
                copy(20 + r, all_small.at[d], all_small.at[d], me).wait_recv()
            tot = all_small[0]
            for d in range(1, N_DEV):
                tot = tot + all_small[d]
            osmall_ref[...] = tot

            copy(14, fin_a, fin_a, me).wait_recv()
            copy(15, fin_b, fin_b, me).wait_recv()
            copy(27, sw_fin, sw_fin, me).wait_recv()
            oa_ref[pl.ds(pl.multiple_of((1 - c) * HALF_A, 8), HALF_A), :] = fin_a[...]
            ob_ref[pl.ds(pl.multiple_of((1 - c) * HALF_B, 8), HALF_B), :] = fin_b[...]
            sw_rows(osw_ref, 1 - c)[...] = sw_fin[...]
            sends = sm_first + vec_forwards() + sw_to_chips() + back + [sw_to_sib, sw_to_sibling()]
            sends += to_neighbours() + to_owner(1) + to_owner(2)
            for cp in sends:
                cp.wait_send()

    vmem = pl.BlockSpec(memory_space=pltpu.VMEM)
    vm = pltpu.VMEM
    grid_spec = pltpu.PrefetchScalarGridSpec(
        num_scalar_prefetch=1, grid=(n_steps, nk),
        in_specs=[pl.BlockSpec((HALF_A, tk), lambda b, k, o: (o[b], k)), pl.BlockSpec(memory_space=pl.ANY),
                  pl.BlockSpec((HALF_B, tk), lambda b, k, o: (o[b], k)), pl.BlockSpec(memory_space=pl.ANY),
                  vmem, vmem],
        out_specs=[vmem, vmem, vmem, vmem],
        scratch_shapes=[vm((seq, D_MODEL), h.dtype), vm((seq, D_MODEL), dout.dtype),
                        vm((2, HALF_A, D_MODEL), F32), vm((2, HALF_B, D_MODEL), F32),
                        vm((N_CHIPS, HALF_A, D_MODEL), F32), vm((N_CHIPS, HALF_B, D_MODEL), F32),
                        vm((3, HALF_A, D_MODEL), wire), vm((3, HALF_B, D_MODEL), wire),
                        vm((3, HALF_A, D_MODEL), wire), vm((3, HALF_B, D_MODEL), wire),
                        vm((HALF_A, D_MODEL), F32), vm((HALF_B, D_MODEL), F32),
                        vm((HALF_A, D_MODEL), F32), vm((HALF_B, D_MODEL), F32),
                        vm((N_DEV, VEC_ROWS, 128), F32), vm((half_sw, 128), F32), vm((N_CHIPS, half_sw, 128), wire),
                        vm((half_sw, 128), F32),
                        vm((HALF_A // 2, D_MODEL), wire), vm((HALF_B // 2, D_MODEL), wire),
                        vm((HALF_A // 2, D_MODEL), wire), vm((HALF_B // 2, D_MODEL), wire),
                        pltpu.SemaphoreType.DMA((30,)), pltpu.SemaphoreType.DMA((30,)), pltpu.SemaphoreType.DMA((2 * nk,))])
    return pl.pallas_call(
        body, name="wgrad_reduce", grid_spec=grid_spec,
        out_shape=[jax.ShapeDtypeStruct((W_IN_SHARD, D_MODEL), F32), jax.ShapeDtypeStruct((W_OUT_SHARD, D_MODEL), F32),
                   jax.ShapeDtypeStruct((SW_ROWS, 128), F32), jax.ShapeDtypeStruct((VEC_ROWS, 128), F32)],
        compiler_params=pltpu.CompilerParams(dimension_semantics=("arbitrary", "arbitrary"), vmem_limit_bytes=VMEM_LIMIT),
    )(order, dproj_t, h, mixed_t, dout, sw, vec)


def _adamw(w, g, m, v):
    nm = ADAM_B1 * m + (1.0 - ADAM_B1) * g
    nv = ADAM_B2 * v + (1.0 - ADAM_B2) * (g * g)
    m_hat = nm / (1.0 - ADAM_B1 ** ADAM_STEP)
    v_hat = nv / (1.0 - ADAM_B2 ** ADAM_STEP)
    return -ADAM_LR * (m_hat / (jnp.sqrt(v_hat) + ADAM_EPS) + ADAM_WD * w), nm, nv


def _adamw_shards_call(a, b, steps=4):
    def body(*refs):
        for k in range(2):
            w_ref, g_ref, m_ref, v_ref = refs[4 * k:4 * k + 4]
            go_ref, d_ref, nm_ref, nv_ref = refs[8 + 4 * k:12 + 4 * k]
            gg = g_ref[...]
            go_ref[...] = gg
            d_ref[...], nm_ref[...], nv_ref[...] = _adamw(w_ref[...], gg, m_ref[...], v_ref[...])

    specs, shapes = [], []
    for w in (a[0], b[0]):
        rows, cols = w.shape
        specs += [pl.BlockSpec((rows // steps, cols), lambda i: (i, 0))] * 4
        shapes += [jax.ShapeDtypeStruct((rows, cols), F32)] * 4
    return pl.pallas_call(
        body, name="adamw_shards", grid=(steps,), in_specs=specs, out_specs=specs, out_shape=shapes,
        compiler_params=pltpu.CompilerParams(dimension_semantics=("arbitrary",)),
    )(*a, *b)


_SMALL = (("norm_g", (1, D_MODEL), R_G1), ("b_in", (1, IN_W), R_BIN), ("attn_sinks", (1, 8), R_SINK),
          ("sgu_ln_g", (1, SGU_W), R_LNG), ("sgu_ln_b", (1, SGU_W), R_LNB), ("sgu_b", (N_SGU_HEADS, BLOCK), R_SGUB),
          ("b_out", (1, D_MODEL), R_BOUT), ("final_norm_g", (1, D_MODEL), R_G3))


def _adamw_small_call(sw_g, vec_g, sgu_w3, ws, ms, vs):
    n = len(_SMALL)

    def body(*refs):
        sw_ref, vec_ref = refs[0], refs[1]
        w3 = refs[2:5]
        w_refs, m_refs, v_refs = refs[5:5 + n], refs[5 + n:5 + 2 * n], refs[5 + 2 * n:5 + 3 * n]
        outs = refs[5 + 3 * n:]
        outs[0][...] = vec_ref[R_LOSS:R_LOSS + 1, 0:1]
        g = sw_ref[...]
        outs[1][...] = g
        outs[2][...], outs[3][...], outs[4][...] = _adamw(w3[0][...], g, w3[1][...], w3[2][...])
        for k, (_, shape, row) in enumerate(_SMALL):
            if shape[0] == 1 and shape[1] >= 128:
                g = jnp.concatenate([vec_ref[row + q:row + q + 1, :] for q in range(shape[1] // 128)], axis=1)
            else:
                g = vec_ref[row:row + shape[0], 0:shape[1]]
            o = outs[5 + 4 * k:9 + 4 * k]
            o[0][...] = g
            o[1][...], o[2][...], o[3][...] = _adamw(w_refs[k][...], g, m_refs[k][...], v_refs[k][...])

    vmem = pl.BlockSpec(memory_space=pltpu.VMEM)
    out_shape = [jax.ShapeDtypeStruct((1, 1), F32)] + [jax.ShapeDtypeStruct((SW_ROWS, 128), F32)] * 4
    for _, shape, _ in _SMALL:
        out_shape += [jax.ShapeDtypeStruct(shape, F32)] * 4
    args = [sw_g, vec_g, *sgu_w3, *ws, *ms, *vs]
    return pl.pallas_call(
        body, name="adamw_small", in_specs=[vmem] * len(args), out_specs=[vmem] * len(out_shape), out_shape=out_shape,
    )(*args)


def kernel(x, norm_g, w_in, b_in, attn_sinks, sgu_ln_g, sgu_ln_b, sgu_w, sgu_b, w_out, b_out, final_norm_g, loss_target, m_norm_g, m_w_in, m_b_in, m_attn_sinks, m_sgu_ln_g, m_sgu_ln_b, m_sgu_w, m_sgu_b, m_w_out, m_b_out, m_final_norm_g, v_norm_g, v_w_in, v_b_in, v_attn_sinks, v_sgu_ln_g, v_sgu_ln_b, v_sgu_w, v_sgu_b, v_w_out, v_b_out, v_final_norm_g):
    seq = x.shape[1]
    win_t, wout, h, r1 = _gather_call(w_in[0].T, w_out[0], x[0], norm_g)
    bexp = jnp.repeat(sgu_b[0].T, SGU_W // N_SGU_HEADS, axis=1)
    dx, dproj_t, mixed_t, dout, sw, vec = _fused_call(
        x[0], h, r1, loss_target[0], attn_sinks[0], norm_g, b_in, sgu_ln_g, sgu_ln_b, sgu_w[0], bexp, b_out,
        final_norm_g.reshape(1, D_MODEL), win_t, wout)
    ga_t, g_w_out, sw, vec = _wgrad_reduce_call(dproj_t, h, mixed_t, dout, sw, vec)

    names = ["norm_g", "w_in", "b_in", "attn_sinks", "sgu_ln_g", "sgu_ln_b", "sgu_w", "sgu_b", "w_out", "b_out", "final_norm_g"]
    res = {}
    shards = _adamw_shards_call((w_in[0].T, ga_t, m_w_in[0].T, v_w_in[0].T), (w_out[0], g_w_out, m_w_out[0], v_w_out[0]))
    res["w_in"] = [a.T[None] for a in shards[:4]]
    res["w_out"] = [a[None] for a in shards[4:]]
    given = dict(norm_g=(norm_g, m_norm_g, v_norm_g), b_in=(b_in, m_b_in, v_b_in), attn_sinks=(attn_sinks, m_attn_sinks, v_attn_sinks),
                 sgu_ln_g=(sgu_ln_g, m_sgu_ln_g, v_sgu_ln_g), sgu_ln_b=(sgu_ln_b, m_sgu_ln_b, v_sgu_ln_b),
                 sgu_b=(sgu_b, m_sgu_b, v_sgu_b), b_out=(b_out, m_b_out, v_b_out),
                 final_norm_g=(final_norm_g, m_final_norm_g, v_final_norm_g))
    wmv = [[given[n][k].reshape(shape) for n, shape, _ in _SMALL] for k in range(3)]
    outs = _adamw_small_call(sw, vec, [a.reshape(SW_ROWS, BLOCK) for a in (sgu_w, m_sgu_w, v_sgu_w)], *wmv)
    loss = outs[0].reshape(())
    res["sgu_w"] = [a.reshape(sgu_w.shape) for a in outs[1:5]]
    for k, (n, _, _) in enumerate(_SMALL):
        res[n] = [a.reshape(given[n][0].shape) for a in outs[5 + 4 * k:9 + 4 * k]]

    return (loss, dx[None], *[res[n][0] for n in names], *[res[n][1] for n in names], *[res[n][2] for n in names],
            *[res[n][3] for n in names])
```

```python
import functools
import math

import jax
import jax.numpy as jnp
from jax import lax
from jax.experimental import pallas as pl
from jax.experimental.pallas import tpu as pltpu

F32 = jnp.float32
MXU_DTYPE = jnp.bfloat16

D_MODEL = 1024
HEAD_DIM = 64
ATTN_W = 512
SGU_W = 512
N_SGU_HEADS = 8
BLOCK = 128
IN_W = 2816
OFF_K, OFF_V, OFF_ZA, OFF_US, OFF_VS, OFF_ZS = 512, 640, 768, 1280, 1792, 2304
NORM_EPS = 1e-5
NEG_INF = -1e30
SCALE = HEAD_DIM ** -0.5
SQRT_HALF = math.sqrt(0.5)
INV_SQRT_2PI = 1.0 / math.sqrt(2.0 * math.pi)

N_CHIPS = 4
N_DEV = 8
W_IN_SHARD = IN_W // N_CHIPS
W_OUT_SHARD = D_MODEL // N_CHIPS
HALF_A = W_IN_SHARD // 2
HALF_B = W_OUT_SHARD // 2

TILE = 256
VMEM_LIMIT = 56 * 1024 * 1024

ADAM_LR, ADAM_B1, ADAM_B2, ADAM_EPS, ADAM_WD, ADAM_STEP = 0.001, 0.9, 0.999, 1e-08, 0.01, 10

SW_ROWS = N_SGU_HEADS * BLOCK
R_G1, R_BIN, R_SINK, R_LOSS, R_LNG, R_LNB, R_SGUB, R_BOUT, R_G3 = 0, 8, 32, 40, 48, 56, 64, 72, 80
VEC_ROWS = 88

MESH = pl.DeviceIdType.MESH


def _mm(a, b):
    return jnp.dot(a, b, preferred_element_type=F32)


def _mm_nt(a, b):
    return lax.dot_general(a, b, (((1,), (1,)), ((), ())), preferred_element_type=F32)


def _sigmoid(z):
    return 1.0 / (1.0 + jnp.exp(-z))


def _norm_cdf(z):
    return 0.5 * (1.0 + lax.erf(z * SQRT_HALF))


def _norm_pdf(z):
    return jnp.exp(-0.5 * z * z) * INV_SQRT_2PI


def _rows8(v):
    r, n = v.shape
    return jnp.sum(v.reshape(r // 8, 8, n), axis=0)


def _mean_rows(v):
    return jnp.sum(v, axis=1, keepdims=True) * (1.0 / v.shape[1])


def _fused_call(x, h, r1, tgt, sinks, g1, b_in, ln_g, ln_b, sgu_w, bexp, b_out, g3, win_t, wout):
    seq = x.shape[0]
    t = TILE
    nt = seq // t
    nb = t // BLOCK
    act = MXU_DTYPE

    def body(sinks_ref, x_ref, h_ref, hh_ref, r1_ref, tgt_ref, g1_ref, bin_ref, lng_ref, lnb_ref, sguw_ref, bexp_ref, bout_ref, g3_ref,
             wint_hbm, wout_hbm,
             dx_ref, dprojt_ref, mixedt_ref, dout_ref, sw_ref, vec_ref,
             dproj_ref, mixed_ref, wint_v, wout_v, wf_v, wb_v, q_s, kf_s, vf_s, k2_s, v2_s, gate_s, p_s, ps_s, o_s, u_s, mix_s, vhat_s, r2_s,
             cdfu_s, cdfv_s, doutf_s, dmix_s, dkf_s, dvf_s, carryk_s, carryv_s,
             acc_bin, acc_g1, acc_bout, acc_g3, acc_lng, acc_lnb, acc_dws, acc_dbs, acc_sink, acc_loss,
             h_s, rhs_s, vlnp_s, dvln_s, sems):
        i = pl.program_id(0)
        tile = nt - 1 - i
        lane128 = lax.broadcasted_iota(jnp.int32, (BLOCK, BLOCK), 1)
        lo = lane128 < HEAD_DIM

        @pl.when(i == 0)
        def _():
            cp_a = pltpu.make_async_copy(wint_hbm, wint_v, sems.at[0])
            cp_b = pltpu.make_async_copy(wout_hbm, wout_v, sems.at[1])
            cp_a.start()
            cp_b.start()
            for acc in (acc_bin, acc_g1, acc_bout, acc_g3, acc_lng, acc_lnb, acc_dws, acc_dbs, acc_sink, acc_loss,
                        carryk_s, carryv_s):
                acc[...] = jnp.zeros(acc.shape, F32)
            tril = lax.broadcasted_iota(jnp.int32, (BLOCK, BLOCK), 0) >= lane128
            for h in range(N_SGU_HEADS):
                w = jnp.where(tril, sguw_ref[h], 0.0)
                wf_v[h // 2, :, (h % 2) * BLOCK:(h % 2 + 1) * BLOCK] = w.astype(act)
                wb_v[h // 2, :, (h % 2) * BLOCK:(h % 2 + 1) * BLOCK] = w.T.astype(act)
            cp_a.wait()
            cp_b.wait()

        g1 = g1_ref[...]

        def rms(v):
            r = lax.rsqrt(_mean_rows(v * v) + NORM_EPS)
            return r, v * r

        h_s[0:BLOCK, :] = hh_ref[...]
        h_s[BLOCK:, :] = h_ref[...]
        r1_s = r1_ref

        h = h_ref[...]
        q = _mm_nt(h, wint_v[0:OFF_K, :]) + bin_ref[:, 0:OFF_K]
        q_s[...] = (q * SCALE).astype(act)
        kv = _mm_nt(h_s[...], wint_v[OFF_K:OFF_ZA, :]) + bin_ref[:, OFF_K:OFF_ZA]
        kf_s[...] = kv[:, :BLOCK]
        vf_s[...] = kv[:, BLOCK:]
        for r in range(4):
            cols = slice(OFF_ZA + r * 512, OFF_ZA + (r + 1) * 512)
            gate_s[r] = _mm_nt(h, wint_v[cols, :]) + bin_ref[:, cols]

        lo_kv = lax.broadcasted_iota(jnp.int32, (t + BLOCK, BLOCK), 1) < HEAD_DIM
        for src, dst in ((kf_s, k2_s), (vf_s, v2_s)):
            v = src[...]
            vr = pltpu.roll(v, HEAD_DIM, 1)
            dst[0] = jnp.where(lo_kv, v, vr).astype(act)
            dst[1] = jnp.where(lo_kv, vr, v).astype(act)

        rowi = lax.broadcasted_iota(jnp.int32, (BLOCK, 2 * BLOCK), 0)
        colj = lax.broadcasted_iota(jnp.int32, (BLOCK, 2 * BLOCK), 1)
        in_band = (colj > rowi) & (colj <= rowi + BLOCK)
        row512 = lax.broadcasted_iota(jnp.int32, (4 * BLOCK, 1), 0)

        def stacked_q(b, g):
            parts = []
            for p in range(2):
                slab = q_s[b * BLOCK:(b + 1) * BLOCK, (2 * g + p) * BLOCK:(2 * g + p + 1) * BLOCK]
                parts += [jnp.where(lo, slab, jnp.zeros_like(slab)), jnp.where(lo, jnp.zeros_like(slab), slab)]
            return jnp.concatenate(parts, axis=0)

        def sink_col(g):
            s = [sinks_ref[4 * g + k] for k in range(4)]
            return jnp.where(row512 < BLOCK, s[0], jnp.where(row512 < 2 * BLOCK, s[1], jnp.where(row512 < 3 * BLOCK, s[2], s[3])))

        for b in range(nb):
            band = slice(b * BLOCK, (b + 2) * BLOCK)
            first_key = jnp.where(tile * nb + b > 0, 0, BLOCK)
            valid = in_band & (colj >= first_key)
            valid4 = jnp.concatenate([valid] * 4, axis=0)
            for g in range(2):
                s = _mm_nt(stacked_q(b, g), k2_s[g, band, :])
                s = jnp.where(valid4, s, NEG_INF)
                sk = sink_col(g)
                m = jnp.maximum(jnp.max(s, axis=1, keepdims=True), sk)
                p = jnp.exp(s - m)
                psk = jnp.exp(sk - m)
                inv = 1.0 / (jnp.sum(p, axis=1, keepdims=True) + psk)
                p = p * inv
                p_s[b * 2 + g] = p
                ps_s[b * 2 + g] = psk * inv
                o2 = _mm(p.astype(act), v2_s[g, band, :])
                for pr in range(2):
                    o_s[b * BLOCK:(b + 1) * BLOCK, (2 * g + pr) * BLOCK:(2 * g + pr + 1) * BLOCK] = jnp.where(
                        lo, o2[(2 * pr) * BLOCK:(2 * pr + 1) * BLOCK], o2[(2 * pr + 1) * BLOCK:(2 * pr + 2) * BLOCK])

        lng = lng_ref[...]
        lnb = lnb_ref[...]

        def split_pairs(val, c):
            for p in range(4):
                slab = val[:, p * BLOCK:(p + 1) * BLOCK]
                rhs_s[p, 0:BLOCK, c * BLOCK:(c + 1) * BLOCK] = jnp.where(lo, slab, 0.0).astype(act)
                rhs_s[p, BLOCK:, c * BLOCK:(c + 1) * BLOCK] = jnp.where(lo, 0.0, slab).astype(act)

        for c in range(nb):
            rows = slice(c * BLOCK, (c + 1) * BLOCK)
            za = gate_s[0, rows, :]
            mixed_ref[rows, 0:ATTN_W] = (o_s[rows, :] * (za * _sigmoid(za))).astype(act)
            us = gate_s[1, rows, :]
            vs = gate_s[2, rows, :]
            cu = _norm_cdf(us)
            cv = _norm_cdf(vs)
            cdfu_s[rows, :] = cu
            cdfv_s[rows, :] = cv
            u = us * cu
            vg = vs * cv
            vc = vg - _mean_rows(vg)
            r2 = lax.rsqrt(_mean_rows(vc * vc) + NORM_EPS)
            vhat = vc * r2
            r2_s[rows, :] = r2
            vhat_s[rows, :] = vhat
            u_s[rows, :] = u
            split_pairs(vhat * lng + lnb, c)
        for p in range(4):
            cols = slice(p * BLOCK, (p + 1) * BLOCK)
            mix = _mm(wf_v[p], rhs_s[p])
            for c in range(nb):
                mix_s[c * BLOCK:(c + 1) * BLOCK, cols] = mix[:, c * BLOCK:(c + 1) * BLOCK] + bexp_ref[:, cols]
        for c in range(nb):
            rows = slice(c * BLOCK, (c + 1) * BLOCK)
            zs = gate_s[3, rows, :]
            mixed_ref[rows, ATTN_W:] = (u_s[rows, :] * mix_s[rows, :] * (zs * _sigmoid(zs))).astype(act)

        g3 = g3_ref[...]
        proj_o = _mm(mixed_ref[...], wout_v[...])
        for c in range(nb):
            rows = slice(c * BLOCK, (c + 1) * BLOCK)
            out = x_ref[rows, :] + proj_o[rows, :] + bout_ref[...]
            r3, on = rms(out)
            e = on * g3 - tgt_ref[rows, :]
            e2 = _rows8(e * e)
            acc_loss[...] += sum(e2[:, k * 128:(k + 1) * 128] for k in range(D_MODEL // 128)) * (0.5 / D_MODEL)
            acc_g3[...] += _rows8(e * on)
            don = e * g3
            dout = (r3 * (1.0 / D_MODEL)) * (don - on * _mean_rows(don * on))
            doutf_s[rows, :] = dout
            dout_ref[rows, :] = dout.astype(act)
            acc_bout[...] += _rows8(dout)
        dmix_s[...] = _mm_nt(dout_ref[...], wout_v[...])

        dkf_s[...] = jnp.zeros(dkf_s.shape, F32)
        dvf_s[...] = jnp.zeros(dvf_s.shape, F32)
        for b in range(nb):
            rows = slice(b * BLOCK, (b + 1) * BLOCK)
            band = slice(b * BLOCK, (b + 2) * BLOCK)
            za = gate_s[0, rows, :]
            sg = _sigmoid(za)
            dao = dmix_s[rows, 0:ATTN_W]
            o = o_s[rows, :]
            do = dao * (za * sg)
            dza = dao * o * (sg * (1.0 + za * (1.0 - sg)))
            dproj_ref[rows, OFF_ZA:OFF_US] = dza.astype(act)
            acc_bin[:, OFF_ZA:OFF_US] += _rows8(dza)
            for g in range(2):
                do_parts, delta_parts = [], []
                for pr in range(2):
                    cols = slice((2 * g + pr) * BLOCK, (2 * g + pr + 1) * BLOCK)
                    d_pair = do[:, cols]
                    prod = d_pair * o[:, cols]
                    do_parts += [jnp.where(lo, d_pair, 0.0).astype(act), jnp.where(lo, 0.0, d_pair).astype(act)]
                    delta_parts += [jnp.sum(jnp.where(lo, prod, 0.0), axis=1, keepdims=True),
                                    jnp.sum(jnp.where(lo, 0.0, prod), axis=1, keepdims=True)]
                do_st = jnp.concatenate(do_parts, axis=0)
                delta = jnp.concatenate(delta_parts, axis=0)
                p = p_s[b * 2 + g]
                dp = _mm_nt(do_st, v2_s[g, band, :])
                ds = p * (dp - delta)
                sink_t = ps_s[b * 2 + g] * delta
                for k in range(4):
                    acc_sink[4 * g + k:4 * g + k + 1, :] += -jnp.sum(sink_t[k * BLOCK:(k + 1) * BLOCK], axis=0, keepdims=True)
                ds_a = ds.astype(act)
                dq2 = _mm(ds_a, k2_s[g, band, :]) * SCALE
                for pr in range(2):
                    cols = slice((2 * g + pr) * BLOCK, (2 * g + pr + 1) * BLOCK)
                    dq = jnp.where(lo, dq2[(2 * pr) * BLOCK:(2 * pr + 1) * BLOCK], dq2[(2 * pr + 1) * BLOCK:(2 * pr + 2) * BLOCK])
                    dproj_ref[rows, cols] = dq.astype(act)
                    acc_bin[:, cols] += _rows8(dq)
                for acc, lhs, rhs in ((dkf_s, stacked_q(b, g), ds_a), (dvf_s, do_st, p.astype(act))):
                    d2t = _mm(lhs.T, rhs)
                    acc[g * HEAD_DIM:(g + 1) * HEAD_DIM, band] += d2t[0:HEAD_DIM, :] + d2t[HEAD_DIM:, :]
        for acc, carry, off in ((dkf_s, carryk_s, OFF_K), (dvf_s, carryv_s, OFF_V)):
            acc[:, t:t + BLOCK] += carry[...]
            carry[...] = acc[:, 0:BLOCK]
            d = acc[:, BLOCK:].T
            dproj_ref[:, off:off + BLOCK] = d.astype(act)
            acc_bin[:, off:off + BLOCK] += _rows8(d)

        for c in range(nb):
            rows = slice(c * BLOCK, (c + 1) * BLOCK)
            dso = dmix_s[rows, ATTN_W:]
            u = u_s[rows, :]
            mix = mix_s[rows, :]
            zs = gate_s[3, rows, :]
            sg = _sigmoid(zs)
            sgs = zs * sg
            du = dso * mix * sgs
            dmx = dso * u * sgs
            dzs = dso * u * mix * (sg * (1.0 + zs * (1.0 - sg)))
            us = gate_s[1, rows, :]
            dus = du * (cdfu_s[rows, :] + us * _norm_pdf(us))
            vln = (vhat_s[rows, :] * lng + lnb).astype(act)
            for p in range(4):
                vlnp_s[p, :, c * BLOCK:(c + 1) * BLOCK] = vln[:, p * BLOCK:(p + 1) * BLOCK]
            split_pairs(dmx, c)
            acc_dbs[...] += dmx
            for off, val in ((OFF_US, dus), (OFF_ZS, dzs)):
                dproj_ref[rows, off:off + 512] = val.astype(act)
                acc_bin[:, off:off + 512] += _rows8(val)
        for p in range(4):
            dvln = _mm(wb_v[p], rhs_s[p])
            for c in range(nb):
                dvln_s[c * BLOCK:(c + 1) * BLOCK, p * BLOCK:(p + 1) * BLOCK] = dvln[:, c * BLOCK:(c + 1) * BLOCK]
            acc_dws[(2 * p) * BLOCK:(2 * p + 1) * BLOCK, :] += _mm_nt(rhs_s[p, 0:BLOCK, :], vlnp_s[p])
            acc_dws[(2 * p + 1) * BLOCK:(2 * p + 2) * BLOCK, :] += _mm_nt(rhs_s[p, BLOCK:, :], vlnp_s[p])
        for c in range(nb):
            rows = slice(c * BLOCK, (c + 1) * BLOCK)
            dvln = dvln_s[rows, :]
            vhat = vhat_s[rows, :]
            acc_lng[...] += _rows8(dvln * vhat)
            acc_lnb[...] += _rows8(dvln)
            dvhat = dvln * lng
            dvg = r2_s[rows, :] * (dvhat - _mean_rows(dvhat) - vhat * _mean_rows(dvhat * vhat))
            vs = gate_s[2, rows, :]
            dvs = dvg * (cdfv_s[rows, :] + vs * _norm_pdf(vs))
            dproj_ref[rows, OFF_VS:OFF_VS + 512] = dvs.astype(act)
            acc_bin[:, OFF_VS:OFF_VS + 512] += _rows8(dvs)

        dh = _mm(dproj_ref[...], wint_v[...])
        dprojt_ref[...] = dproj_ref[...].T
        mixedt_ref[...] = mixed_ref[...].T
        for c in range(nb):
            rows = slice(c * BLOCK, (c + 1) * BLOCK)
            r1 = r1_s[rows, :]
            xn = x_ref[rows, :] * r1
            dhc = dh[rows, :]
            acc_g1[...] += _rows8(dhc * xn)
            dxn = dhc * g1
            dx_ref[rows, :] = doutf_s[rows, :] + r1 * (dxn - xn * _mean_rows(dxn * xn))

        @pl.when(i == nt - 1)
        def _():
            tril = lax.broadcasted_iota(jnp.int32, (BLOCK, BLOCK), 0) >= lane128
            for hh in range(N_SGU_HEADS):
                rws = slice(hh * BLOCK, (hh + 1) * BLOCK)
                sw_ref[rws, :] = jnp.where(tril, acc_dws[rws, :], 0.0)
            vec_ref[...] = jnp.zeros((VEC_ROWS, 128), F32)

            def put(row0, acc, scale=1.0):
                s = jnp.sum(acc[...], axis=0, keepdims=True) * scale
                for k in range(acc.shape[1] // 128):
                    vec_ref[row0 + k:row0 + k + 1, :] = s[:, k * 128:(k + 1) * 128]

            put(R_G1, acc_g1)
            put(R_BIN, acc_bin)
            put(R_LNG, acc_lng)
            put(R_LNB, acc_lnb)
            put(R_BOUT, acc_bout)
            put(R_G3, acc_g3, 1.0 / D_MODEL)
            vec_ref[R_SINK:R_SINK + 1, :] = jnp.sum(
                jnp.where(lax.broadcasted_iota(jnp.int32, (8, 128), 0) == lax.broadcasted_iota(jnp.int32, (8, 128), 1),
                          acc_sink[...], 0.0), axis=0, keepdims=True)
            vec_ref[R_LOSS:R_LOSS + 1, :] = jnp.zeros((1, 128), F32) + jnp.sum(acc_loss[...])
            dbs_t = acc_dbs[...].T
            vec_ref[R_SGUB:R_SGUB + 8, :] = jnp.sum(dbs_t.reshape(N_SGU_HEADS, SGU_W // N_SGU_HEADS, BLOCK), axis=1)

    full = lambda shape: pl.BlockSpec(shape, lambda i: (0,) * len(shape))
    tok = lambda w: pl.BlockSpec((t, w), lambda i: (nt - 1 - i, 0))
    in_specs = [
        pl.BlockSpec(memory_space=pltpu.SMEM),
        tok(D_MODEL), tok(D_MODEL),
        pl.BlockSpec((BLOCK, D_MODEL), lambda i: (jnp.maximum((nt - 1 - i) * nb - 1, 0), 0)),
        tok(1), tok(D_MODEL),
        full((1, D_MODEL)), full((1, IN_W)), full((1, SGU_W)), full((1, SGU_W)),
        full((N_SGU_HEADS, BLOCK, BLOCK)), full((BLOCK, SGU_W)), full((1, D_MODEL)), full((1, D_MODEL)),
        pl.BlockSpec(memory_space=pl.ANY), pl.BlockSpec(memory_space=pl.ANY),
    ]
    out_shape = [
        jax.ShapeDtypeStruct((seq, D_MODEL), F32),
        jax.ShapeDtypeStruct((IN_W, seq), act),
        jax.ShapeDtypeStruct((D_MODEL, seq), act),
        jax.ShapeDtypeStruct((seq, D_MODEL), act),
        jax.ShapeDtypeStruct((SW_ROWS, 128), F32),
        jax.ShapeDtypeStruct((VEC_ROWS, 128), F32),
    ]
    tok_t = lambda w: pl.BlockSpec((w, t), lambda i: (0, nt - 1 - i))
    out_specs = [tok(D_MODEL), tok_t(IN_W), tok_t(D_MODEL), tok(D_MODEL), full((SW_ROWS, 128)), full((VEC_ROWS, 128))]
    vm = pltpu.VMEM
    scratch = [
        vm((t, IN_W), act), vm((t, D_MODEL), act),
        vm((IN_W, D_MODEL), act), vm((D_MODEL, D_MODEL), act),
        vm((4, BLOCK, 2 * BLOCK), act), vm((4, BLOCK, 2 * BLOCK), act),
        vm((t, ATTN_W), act),
        vm((t + BLOCK, BLOCK), F32), vm((t + BLOCK, BLOCK), F32),
        vm((2, t + BLOCK, BLOCK), act), vm((2, t + BLOCK, BLOCK), act),
        vm((4, t, 512), F32),
        vm((2 * nb, 4 * BLOCK, 2 * BLOCK), F32), vm((2 * nb, 4 * BLOCK, 1), F32),
        vm((t, ATTN_W), F32), vm((t, SGU_W), F32), vm((t, SGU_W), F32), vm((t, SGU_W), F32), vm((t, 1), F32),
        vm((t, SGU_W), F32), vm((t, SGU_W), F32),
        vm((t, D_MODEL), F32), vm((t, D_MODEL), F32),
        vm((BLOCK, t + BLOCK), F32), vm((BLOCK, t + BLOCK), F32), vm((BLOCK, BLOCK), F32), vm((BLOCK, BLOCK), F32),
        vm((8, IN_W), F32), vm((8, D_MODEL), F32), vm((8, D_MODEL), F32), vm((8, D_MODEL), F32),
        vm((8, SGU_W), F32), vm((8, SGU_W), F32), vm((N_SGU_HEADS * BLOCK, BLOCK), F32), vm((BLOCK, SGU_W), F32),
        vm((8, 128), F32), vm((8, 128), F32),
        vm((t + BLOCK, D_MODEL), act), vm((4, 2 * BLOCK, t), act), vm((4, BLOCK, t), act), vm((t, SGU_W), F32),
        pltpu.SemaphoreType.DMA((2,)),
    ]
    return pl.pallas_call(
        body, name="fused", grid=(nt,), in_specs=in_specs, out_specs=out_specs, out_shape=out_shape,
        scratch_shapes=scratch,
        compiler_params=pltpu.CompilerParams(dimension_semantics=("arbitrary",), vmem_limit_bytes=VMEM_LIMIT),
    )(sinks, x, h, h, r1, tgt, g1, b_in, ln_g, ln_b, sgu_w, bexp, b_out, g3, win_t, wout)


def _gather_call(a32, b32, x, g1):
    act = MXU_DTYPE
    seq = x.shape[0]
    xc = min(256, seq)
    n_xc = seq // xc

    def body(a32_hbm, b32_hbm, x_hbm, g1_ref, ga_hbm, gb_hbm, h_hbm, r1_hbm, a32_v, b32_v, ga_v, gb_v, x_buf, h_buf, r1_v,
             send_sems, recv_sems, loc_sems, norm_sems):
        x, y, c = lax.axis_index("x"), lax.axis_index("y"), lax.axis_index("c")
        me, sibling, x_nbr, y_nbr = (x, y, c), (x, y, 1 - c), (1 - x, y, c), (x, 1 - y, c)
        j, j_x, j_y, j_d = 2 * x + y, 2 * (1 - x) + y, 2 * x + (1 - y), 2 * (1 - x) + (1 - y)
        arrays = ((0, ga_v, ga_hbm, W_IN_SHARD), (1, gb_v, gb_hbm, W_OUT_SHARD))

        loads = [pltpu.make_async_copy(a32_hbm, a32_v, loc_sems.at[0]), pltpu.make_async_copy(b32_hbm, b32_v, loc_sems.at[1])]
        for cp in loads:
            cp.start()
        for cp in loads:
            cp.wait()
        ga_v[pl.ds(pl.multiple_of(j * W_IN_SHARD, 16), W_IN_SHARD), :] = a32_v[...].astype(act)
        gb_v[pl.ds(pl.multiple_of(j * W_OUT_SHARD, 16), W_OUT_SHARD), :] = b32_v[...].astype(act)

        def rows(shard, chip, hf=None, q=None):
            if hf is None:
                return pl.ds(pl.multiple_of(chip * shard, 16), shard)
            return pl.ds(pl.multiple_of(chip * shard + hf * (shard // 2) + q * (shard // 4), 16), shard // 4)

        def copy(k, ref, at, to):
            return pltpu.make_async_remote_copy(src_ref=ref.at[at, :], dst_ref=ref.at[at, :], send_sem=send_sems.at[k],
                                                recv_sem=recv_sems.at[k], device_id=to, device_id_type=MESH)

        stores = []

        def store(vmem, hbm, at):
            stores.append(pltpu.make_async_copy(vmem.at[at, :], hbm.at[at, :], loc_sems.at[len(stores)]))
            stores[-1].start()

        sent = []
        for w, vmem, hbm, shard in arrays:
            for k, (q, to) in enumerate(((0, x_nbr), (1, x_nbr), (1, y_nbr), (0, y_nbr))):
                sent.append(copy(12 * w + k, vmem, rows(shard, j, c, q), to))
                sent[-1].start()
        for w, vmem, hbm, shard in arrays:
            store(vmem, hbm, rows(shard, j))

        def x_load(n):
            return pltpu.make_async_copy(x_hbm.at[pl.ds(n * xc, xc), :], x_buf.at[n % 2], norm_sems.at[n % 2])

        def h_store(n):
            return pltpu.make_async_copy(h_buf.at[n % 2], h_hbm.at[pl.ds(n * xc, xc), :], norm_sems.at[2 + n % 2])

        g1 = g1_ref[...]
        x_load(0).start()
        for n in range(n_xc):
            if n + 1 < n_xc:
                x_load(n + 1).start()
            x_load(n).wait()
            if n >= 2:
                h_store(n - 2).wait()
            xv = x_buf[n % 2]
            r = lax.rsqrt(_mean_rows(xv * xv) + NORM_EPS)
            r1_v[n * xc:(n + 1) * xc, :] = r
            h_buf[n % 2] = (xv * r * g1).astype(act)
            h_store(n).start()
        for n in range(max(n_xc - 2, 0), n_xc):
            h_store(n).wait()
        r1_out = pltpu.make_async_copy(r1_v, r1_hbm, norm_sems.at[4])
        r1_out.start()

        def landed(w, vmem, hbm, shard, k, chip, q, onward=None):
            at = rows(shard, chip, c, q)
            copy(12 * w + k, vmem, at, me).wait_recv()
            if onward is not None:
                sent.append(copy(12 * w + onward[0], vmem, at, onward[1]))
                sent[-1].start()
            sent.append(copy(12 * w + 6 + k, vmem, at, sibling))
            sent[-1].start()
            store(vmem, hbm, at)

        for arr in arrays:
            landed(*arr, 0, j_x, 0, onward=(4, y_nbr))
            landed(*arr, 2, j_y, 1, onward=(5, x_nbr))
        for arr in arrays:
            landed(*arr, 1, j_x, 1)
            landed(*arr, 3, j_y, 0)
            landed(*arr, 4, j_d, 0)
            landed(*arr, 5, j_d, 1)
        for w, vmem, hbm, shard in arrays:
            for k, (chip, q) in enumerate(((j_x, 0), (j_x, 1), (j_y, 1), (j_y, 0), (j_d, 0), (j_d, 1))):
                at = rows(shard, chip, 1 - c, q)
                copy(12 * w + 6 + k, vmem, at, me).wait_recv()
                store(vmem, hbm, at)
        for cp in sent:
            cp.wait_send()
        for cp in stores:
            cp.wait()
        r1_out.wait()

    hbm = pl.BlockSpec(memory_space=pl.ANY)
    vm = pltpu.VMEM
    return pl.pallas_call(
        body, name="gather", in_specs=[hbm, hbm, hbm, pl.BlockSpec(memory_space=vm)], out_specs=[hbm, hbm, hbm, hbm],
        out_shape=[jax.ShapeDtypeStruct((IN_W, D_MODEL), act), jax.ShapeDtypeStruct((D_MODEL, D_MODEL), act),
                   jax.ShapeDtypeStruct((seq, D_MODEL), act), jax.ShapeDtypeStruct((seq, 1), F32)],
        scratch_shapes=[vm(a32.shape, F32), vm(b32.shape, F32), vm((IN_W, D_MODEL), act), vm((D_MODEL, D_MODEL), act),
                        vm((2, xc, D_MODEL), F32), vm((2, xc, D_MODEL), act), vm((seq, 1), F32),
                        pltpu.SemaphoreType.DMA((24,)), pltpu.SemaphoreType.DMA((24,)), pltpu.SemaphoreType.DMA((26,)),
                        pltpu.SemaphoreType.DMA((5,))],
        compiler_params=pltpu.CompilerParams(vmem_limit_bytes=VMEM_LIMIT),
    )(a32, b32, x, g1)


def _wgrad_reduce_call(dproj_t, h, mixed_t, dout, sw, vec):
    wire = jnp.bfloat16
    half_sw = SW_ROWS // 2
    seq = h.shape[0]
    tk = min(2048, seq)
    nk = seq // tk
    n_steps = 2 * N_CHIPS
    rel_of = lambda s: s % 3 if s < 6 else 3
    half_of = lambda s: s // 3 if s < 6 else s - 6
    x, y = lax.axis_index("x"), lax.axis_index("y")
    chip_of = [2 * (1 - x) + (1 - y), 2 * (1 - x) + y, 2 * x + (1 - y), 2 * x + y]
    order = jnp.stack([2 * chip_of[rel_of(s)] + half_of(s) for s in range(n_steps)]).astype(jnp.int32)

    def body(order_ref, dpt_ref, h_hbm, mxt_ref, dout_hbm, sw_ref, small_ref, oa_ref, ob_ref, osw_ref, osmall_ref,
             h_v, dout_v, acc_a, acc_b, sib_a, sib_b, snd_a, snd_b, in_a, in_b, own_a, own_b, fin_a, fin_b,
             all_small, sw_sib, sw_chips, sw_fin, send_sems, recv_sems, loc_sems):
        x, y, c = lax.axis_index("x"), lax.axis_index("y"), lax.axis_index("c")
        me, sibling = (x, y, c), (x, y, 1 - c)
        steps = [(1 - x, 1 - y), (1 - x, y), (x, 1 - y)]
        j = 2 * x + y
        dev = 4 * x + 2 * y + c
        b, k = pl.program_id(0), pl.program_id(1)

        def copy(n, src, dst, to):
            return pltpu.make_async_remote_copy(src_ref=src, dst_ref=dst, send_sem=send_sems.at[n], recv_sem=recv_sems.at[n],
                                                device_id=to, device_id_type=MESH)

        def sw_rows(ref, hf):
            return ref.at[pl.ds(pl.multiple_of(hf * half_sw, 8), half_sw), :]

        sm_first = [copy(16, small_ref, all_small.at[dev], sibling)]
        sm_first += [copy(17 + r, small_ref, all_small.at[dev], (*chip, c)) for r, chip in enumerate(steps)]
        sw_to_sib = copy(23, sw_rows(sw_ref, 1 - c), sw_sib, sibling)

        @pl.when((b == 0) & (k == 0))
        def _():
            all_small[dev] = small_ref[...]
            for cp in sm_first + [sw_to_sib]:
                cp.start()

        def load(kk):
            rows = pl.ds(kk * tk, tk)
            return [pltpu.make_async_copy(h_hbm.at[rows, :], h_v.at[rows, :], loc_sems.at[kk]),
                    pltpu.make_async_copy(dout_hbm.at[rows, :], dout_v.at[rows, :], loc_sems.at[nk + kk])]

        for kk in range(nk):
            @pl.when((b == 0) & (k == 0))
            def _():
                for cp in load(kk):
                    cp.start()

        for kk in range(nk):
            @pl.when((b == 0) & (k == kk))
            def _():
                for cp in load(kk):
                    cp.wait()

        tok = pl.ds(pl.multiple_of(k * tk, tk), tk)
        pa = _mm(dpt_ref[...], h_v[tok, :])
        pb = _mm(mxt_ref[...], dout_v[tok, :])
        slot = b % 2

        @pl.when(k == 0)
        def _():
            acc_a[slot] = pa
            acc_b[slot] = pb

        @pl.when(k != 0)
        def _():
            acc_a[slot] += pa
            acc_b[slot] += pb

        def to_sibling(s):
            r = rel_of(s)
            return [copy(r, acc_a.at[s % 2], sib_a.at[r], sibling), copy(4 + r, acc_b.at[s % 2], sib_b.at[r], sibling)]

        def chip_partial(s):
            r = rel_of(s)
            copy(r, sib_a.at[r], sib_a.at[r], me).wait_recv()
            copy(4 + r, sib_b.at[r], sib_b.at[r], me).wait_recv()
            return acc_a[s % 2] + sib_a[r], acc_b[s % 2] + sib_b[r]

        def to_owner(r):
            return [copy(8 + r, snd_a.at[r], in_a.at[r], (*steps[r], c)), copy(11 + r, snd_b.at[r], in_b.at[r], (*steps[r], c))]

        def vec_forwards():
            return [copy(20 + r, all_small.at[4 * cx + 2 * cy + c], all_small.at[4 * cx + 2 * cy + c], sibling)
                    for r, (cx, cy) in enumerate(steps)]

        def sw_to_chips():
            return [copy(24 + r, sw_chips.at[j], sw_chips.at[j], (*chip, c)) for r, chip in enumerate(steps)]

        def sw_to_sibling():
            return copy(27, sw_rows(osw_ref, c), sw_fin, sibling)

        @pl.when((b == 1) & (k == 0))
        def _():
            for r, (cx, cy) in enumerate(steps):
                d = 4 * cx + 2 * cy + c
                copy(17 + r, all_small.at[d], all_small.at[d], me).wait_recv()
            for cp in vec_forwards():
                cp.start()
            copy(23, sw_sib, sw_sib, me).wait_recv()
            sw_chips[j] = (sw_rows(sw_ref, c)[...] + sw_sib[...]).astype(wire)
            for cp in sw_to_chips():
                cp.start()

        @pl.when((b == 3) & (k == 0))
        def _():
            for r, (cx, cy) in enumerate(steps):
                cj = 2 * cx + cy
                copy(24 + r, sw_chips.at[cj], sw_chips.at[cj], me).wait_recv()
            tot_sw = sw_chips[0].astype(F32)
            for q in range(1, N_CHIPS):
                tot_sw = tot_sw + sw_chips[q].astype(F32)
            sw_rows(osw_ref, c)[...] = tot_sw
            sw_to_sibling().start()

        late = min(1, nk - 1)
        for s in range(n_steps):
            if s >= 1:
                sp = s - 1

                @pl.when((b == s) & (k == late) & (c == half_of(sp)))
                def _():
                    ta, tb = chip_partial(sp)
                    r = rel_of(sp)
                    if r < 3:
                        snd_a[r] = ta.astype(wire)
                        snd_b[r] = tb.astype(wire)
                        for cp in to_owner(r):
                            cp.start()
                    else:
                        own_a[...] = ta
                        own_b[...] = tb

                @pl.when((b == s) & (k == nk - 1) & (c != half_of(sp)))
                def _():
                    for cp in to_sibling(sp):
                        cp.wait_send()

            @pl.when((b == s) & (k == nk - 1) & (c != half_of(s)))
            def _():
                for cp in to_sibling(s):
                    cp.start()

        @pl.when((b == n_steps - 1) & (k == nk - 1))
        def _():
            last = n_steps - 1

            @pl.when(c == half_of(last))
            def _():
                own_a[...], own_b[...] = chip_partial(last)

            @pl.when(c != half_of(last))
            def _():
                for cp in to_sibling(last):
                    cp.wait_send()

            tot_a = own_a[...]
            tot_b = own_b[...]
            for s in range(3):
                copy(8 + s, in_a.at[s], in_a.at[s], me).wait_recv()
                copy(11 + s, in_b.at[s], in_b.at[s], me).wait_recv()
                tot_a = tot_a + in_a[s].astype(F32)
                tot_b = tot_b + in_b[s].astype(F32)
            mine_a = oa_ref.at[pl.ds(pl.multiple_of(c * HALF_A, 8), HALF_A), :]
            mine_b = ob_ref.at[pl.ds(pl.multiple_of(c * HALF_B, 8), HALF_B), :]
            mine_a[...] = tot_a
            mine_b[...] = tot_b
            back = [copy(14, mine_a, fin_a, sibling), copy(15, mine_b, fin_b, sibling)]
            for cp in back:
                cp.start()

            copy(16, small_ref, all_small.at[dev ^ 1], me).wait_recv()
            for r, (cx, cy) in enumerate(steps):
                d = 4 * cx + 2 * cy + (1 - c)
                copy(20 + r, all_small.at[d], all_small.at[d], me).wait_recv()
            tot = all_small[0]
            for d in range(1, N_DEV):
                tot = tot + all_small[d]
            osmall_ref[...] = tot

            copy(14, fin_a, fin_a, me).wait_recv()
            copy(15, fin_b, fin_b, me).wait_recv()
            copy(27, sw_fin, sw_fin, me).wait_recv()
            oa_ref[pl.ds(pl.multiple_of((1 - c) * HALF_A, 8), HALF_A), :] = fin_a[...]
            ob_ref[pl.ds(pl.multiple_of((1 - c) * HALF_B, 8), HALF_B), :] = fin_b[...]
            sw_rows(osw_ref, 1 - c)[...] = sw_fin[...]
            sends = sm_first + vec_forwards() + sw_to_chips() + back + [sw_to_sib, sw_to_sibling()]
            for s in range(3):
                sends += to_owner(s)
            for cp in sends:
                cp.wait_send()

    vmem = pl.BlockSpec(memory_space=pltpu.VMEM)
    vm = pltpu.VMEM
    grid_spec = pltpu.PrefetchScalarGridSpec(
        num_scalar_prefetch=1, grid=(n_steps, nk),
        in_specs=[pl.BlockSpec((HALF_A, tk), lambda b, k, o: (o[b], k)), pl.BlockSpec(memory_space=pl.ANY),
                  pl.BlockSpec((HALF_B, tk), lambda b, k, o: (o[b], k)), pl.BlockSpec(memory_space=pl.ANY),
                  vmem, vmem],
        out_specs=[vmem, vmem, vmem, vmem],
        scratch_shapes=[vm((seq, D_MODEL), h.dtype), vm((seq, D_MODEL), dout.dtype),
                        vm((2, HALF_A, D_MODEL), F32), vm((2, HALF_B, D_MODEL), F32),
                        vm((N_CHIPS, HALF_A, D_MODEL), F32), vm((N_CHIPS, HALF_B, D_MODEL), F32),
                        vm((3, HALF_A, D_MODEL), wire), vm((3, HALF_B, D_MODEL), wire),
                        vm((3, HALF_A, D_MODEL), wire), vm((3, HALF_B, D_MODEL), wire),
                        vm((HALF_A, D_MODEL), F32), vm((HALF_B, D_MODEL), F32),
                        vm((HALF_A, D_MODEL), F32), vm((HALF_B, D_MODEL), F32),
                        vm((N_DEV, VEC_ROWS, 128), F32), vm((half_sw, 128), F32), vm((N_CHIPS, half_sw, 128), wire),
                        vm((half_sw, 128), F32),
                        pltpu.SemaphoreType.DMA((28,)), pltpu.SemaphoreType.DMA((28,)), pltpu.SemaphoreType.DMA((2 * nk,))])
    return pl.pallas_call(
        body, name="wgrad_reduce", grid_spec=grid_spec,
        out_shape=[jax.ShapeDtypeStruct((W_IN_SHARD, D_MODEL), F32), jax.ShapeDtypeStruct((W_OUT_SHARD, D_MODEL), F32),
                   jax.ShapeDtypeStruct((SW_ROWS, 128), F32), jax.ShapeDtypeStruct((VEC_ROWS, 128), F32)],
        compiler_params=pltpu.CompilerParams(dimension_semantics=("arbitrary", "arbitrary"), vmem_limit_bytes=VMEM_LIMIT),
    )(order, dproj_t, h, mixed_t, dout, sw, vec)


def _adamw(w, g, m, v):
    nm = ADAM_B1 * m + (1.0 - ADAM_B1) * g
    nv = ADAM_B2 * v + (1.0 - ADAM_B2) * (g * g)
    m_hat = nm / (1.0 - ADAM_B1 ** ADAM_STEP)
    v_hat = nv / (1.0 - ADAM_B2 ** ADAM_STEP)
    return -ADAM_LR * (m_hat / (jnp.sqrt(v_hat) + ADAM_EPS) + ADAM_WD * w), nm, nv


def _adamw_shards_call(a, b, steps=4):
    def body(*refs):
        for k in range(2):
            w_ref, g_ref, m_ref, v_ref = refs[4 * k:4 * k + 4]
            go_ref, d_ref, nm_ref, nv_ref = refs[8 + 4 * k:12 + 4 * k]
            gg = g_ref[...]
            go_ref[...] = gg
            d_ref[...], nm_ref[...], nv_ref[...] = _adamw(w_ref[...], gg, m_ref[...], v_ref[...])

    specs, shapes = [], []
    for w in (a[0], b[0]):
        rows, cols = w.shape
        specs += [pl.BlockSpec((rows // steps, cols), lambda i: (i, 0))] * 4
        shapes += [jax.ShapeDtypeStruct((rows, cols), F32)] * 4
    return pl.pallas_call(
        body, name="adamw_shards", grid=(steps,), in_specs=specs, out_specs=specs, out_shape=shapes,
        compiler_params=pltpu.CompilerParams(dimension_semantics=("arbitrary",)),
    )(*a, *b)


_SMALL = (("norm_g", (1, D_MODEL), R_G1), ("b_in", (1, IN_W), R_BIN), ("attn_sinks", (1, 8), R_SINK),
          ("sgu_ln_g", (1, SGU_W), R_LNG), ("sgu_ln_b", (1, SGU_W), R_LNB), ("sgu_b", (N_SGU_HEADS, BLOCK), R_SGUB),
          ("b_out", (1, D_MODEL), R_BOUT), ("final_norm_g", (1, D_MODEL), R_G3))


def _adamw_small_call(sw_g, vec_g, sgu_w3, ws, ms, vs):
    n = len(_SMALL)

    def body(*refs):
        sw_ref, vec_ref = refs[0], refs[1]
        w3 = refs[2:5]
        w_refs, m_refs, v_refs = refs[5:5 + n], refs[5 + n:5 + 2 * n], refs[5 + 2 * n:5 + 3 * n]
        outs = refs[5 + 3 * n:]
        outs[0][...] = vec_ref[R_LOSS:R_LOSS + 1, 0:1]
        g = sw_ref[...]
        outs[1][...] = g
        outs[2][...], outs[3][...], outs[4][...] = _adamw(w3[0][...], g, w3[1][...], w3[2][...])
        for k, (_, shape, row) in enumerate(_SMALL):
            if shape[0] == 1 and shape[1] >= 128:
                g = jnp.concatenate([vec_ref[row + q:row + q + 1, :] for q in range(shape[1] // 128)], axis=1)
            else:
                g = vec_ref[row:row + shape[0], 0:shape[1]]
            o = outs[5 + 4 * k:9 + 4 * k]
            o[0][...] = g
            o[1][...], o[2][...], o[3][...] = _adamw(w_refs[k][...], g, m_refs[k][...], v_refs[k][...])

    vmem = pl.BlockSpec(memory_space=pltpu.VMEM)
    out_shape = [jax.ShapeDtypeStruct((1, 1), F32)] + [jax.ShapeDtypeStruct((SW_ROWS, 128), F32)] * 4
    for _, shape, _ in _SMALL:
        out_shape += [jax.ShapeDtypeStruct(shape, F32)] * 4
    args = [sw_g, vec_g, *sgu_w3, *ws, *ms, *vs]
    return pl.pallas_call(
        body, name="adamw_small", in_specs=[vmem] * len(args), out_specs=[vmem] * len(out_shape), out_shape=out_shape,
    )(*args)


def kernel(x, norm_g, w_in, b_in, attn_sinks, sgu_ln_g, sgu_ln_b, sgu_w, sgu_b, w_out, b_out, final_norm_g, loss_target, m_norm_g, m_w_in, m_b_in, m_attn_sinks, m_sgu_ln_g, m_sgu_ln_b, m_sgu_w, m_sgu_b, m_w_out, m_b_out, m_final_norm_g, v_norm_g, v_w_in, v_b_in, v_attn_sinks, v_sgu_ln_g, v_sgu_ln_b, v_sgu_w, v_sgu_b, v_w_out, v_b_out, v_final_norm_g):
    seq = x.shape[1]
    win_t, wout, h, r1 = _gather_call(w_in[0].T, w_out[0], x[0], norm_g)
    bexp = jnp.repeat(sgu_b[0].T, SGU_W // N_SGU_HEADS, axis=1)
    dx, dproj_t, mixed_t, dout, sw, vec = _fused_call(
        x[0], h, r1, loss_target[0], attn_sinks[0], norm_g, b_in, sgu_ln_g, sgu_ln_b, sgu_w[0], bexp, b_out,
        final_norm_g.reshape(1, D_MODEL), win_t, wout)
    ga_t, g_w_out, sw, vec = _wgrad_reduce_call(dproj_t, h, mixed_t, dout, sw, vec)

    names = ["norm_g", "w_in", "b_in", "attn_sinks", "sgu_ln_g", "sgu_ln_b", "sgu_w", "sgu_b", "w_out", "b_out", "final_norm_g"]
    res = {}
    shards = _adamw_shards_call((w_in[0].T, ga_t, m_w_in[0].T, v_w_in[0].T), (w_out[0], g_w_out, m_w_out[0], v_w_out[0]))
    res["w_in"] = [a.T[None] for a in shards[:4]]
    res["w_out"] = [a[None] for a in shards[4:]]
    given = dict(norm_g=(norm_g, m_norm_g, v_norm_g), b_in=(b_in, m_b_in, v_b_in), attn_sinks=(attn_sinks, m_attn_sinks, v_attn_sinks),
                 sgu_ln_g=(sgu_ln_g, m_sgu_ln_g, v_sgu_ln_g), sgu_ln_b=(sgu_ln_b, m_sgu_ln_b, v_sgu_ln_b),
                 sgu_b=(sgu_b, m_sgu_b, v_sgu_b), b_out=(b_out, m_b_out, v_b_out),
                 final_norm_g=(final_norm_g, m_final_norm_g, v_final_norm_g))
    wmv = [[given[n][k].reshape(shape) for n, shape, _ in _SMALL] for k in range(3)]
    outs = _adamw_small_call(sw, vec, [a.reshape(SW_ROWS, BLOCK) for a in (sgu_w, m_sgu_w, v_sgu_w)], *wmv)
    loss = outs[0].reshape(())
    res["sgu_w"] = [a.reshape(sgu_w.shape) for a in outs[1:5]]
    for k, (n, _, _) in enumerate(_SMALL):
        res[n] = [a.reshape(given[n][0].shape) for a in outs[5 + 4 * k:9 + 4 * k]]

    return (loss, dx[None], *[res[n][0] for n in names], *[res[n][1] for n in names], *[res[n][2] for n in names],
            *[res[n][3] for n in names])
```

```python
import functools
import math

import jax
import jax.numpy as jnp
from jax import lax
from jax.experimental import pallas as pl
from jax.experimental.pallas import tpu as pltpu

F32 = jnp.float32
MXU_DTYPE = jnp.bfloat16

D_MODEL = 1024
HEAD_DIM = 64
ATTN_W = 512
SGU_W = 512
N_SGU_HEADS = 8
BLOCK = 128
IN_W = 2816
OFF_K, OFF_V, OFF_ZA, OFF_US, OFF_VS, OFF_ZS = 512, 640, 768, 1280, 1792, 2304
NORM_EPS = 1e-5
NEG_INF = -1e30
SCALE = HEAD_DIM ** -0.5
SQRT_HALF = math.sqrt(0.5)
INV_SQRT_2PI = 1.0 / math.sqrt(2.0 * math.pi)

N_CHIPS = 4
N_DEV = 8
W_IN_SHARD = IN_W // N_CHIPS
W_OUT_SHARD = D_MODEL // N_CHIPS
HALF_A = W_IN_SHARD // 2
HALF_B = W_OUT_SHARD // 2

TILE = 256
VMEM_LIMIT = 56 * 1024 * 1024

ADAM_LR, ADAM_B1, ADAM_B2, ADAM_EPS, ADAM_WD, ADAM_STEP = 0.001, 0.9, 0.999, 1e-08, 0.01, 10

SW_ROWS = N_SGU_HEADS * BLOCK
R_G1, R_BIN, R_SINK, R_LOSS, R_LNG, R_LNB, R_SGUB, R_BOUT, R_G3 = 0, 8, 32, 40, 48, 56, 64, 72, 80
VEC_ROWS = 88

MESH = pl.DeviceIdType.MESH


def _mm(a, b):
    return jnp.dot(a, b, preferred_element_type=F32)


def _mm_nt(a, b):
    return lax.dot_general(a, b, (((1,), (1,)), ((), ())), preferred_element_type=F32)


def _sigmoid(z):
    return 1.0 / (1.0 + jnp.exp(-z))


def _norm_cdf(z):
    return 0.5 * (1.0 + lax.erf(z * SQRT_HALF))


def _norm_pdf(z):
    return jnp.exp(-0.5 * z * z) * INV_SQRT_2PI


def _rows8(v):
    r, n = v.shape
    return jnp.sum(v.reshape(r // 8, 8, n), axis=0)


def _mean_rows(v):
    return jnp.sum(v, axis=1, keepdims=True) * (1.0 / v.shape[1])


def _fused_call(x, h, r1, tgt, sinks, g1, b_in, ln_g, ln_b, sgu_w, bexp, b_out, g3, win_t, wout):
    seq = x.shape[0]
    t = TILE
    nt = seq // t
    nb = t // BLOCK
    act = MXU_DTYPE

    def body(sinks_ref, x_ref, h_ref, hh_ref, r1_ref, tgt_ref, g1_ref, bin_ref, lng_ref, lnb_ref, sguw_ref, bexp_ref, bout_ref, g3_ref,
             wint_hbm, wout_hbm,
             dx_ref, dprojt_ref, mixedt_ref, dout_ref, sw_ref, vec_ref,
             dproj_ref, mixed_ref, wint_v, wout_v, wf_v, wb_v, q_s, kf_s, vf_s, k2_s, v2_s, gate_s, p_s, ps_s, o_s, u_s, mix_s, vhat_s, r2_s,
             cdfu_s, cdfv_s, doutf_s, dmix_s, dkf_s, dvf_s, carryk_s, carryv_s,
             acc_bin, acc_g1, acc_bout, acc_g3, acc_lng, acc_lnb, acc_dws, acc_dbs, acc_sink, acc_loss,
             h_s, rhs_s, vlnp_s, dvln_s, sems):
        i = pl.program_id(0)
        tile = nt - 1 - i
        lane128 = lax.broadcasted_iota(jnp.int32, (BLOCK, BLOCK), 1)
        lo = lane128 < HEAD_DIM

        @pl.when(i == 0)
        def _():
            cp_a = pltpu.make_async_copy(wint_hbm, wint_v, sems.at[0])
            cp_b = pltpu.make_async_copy(wout_hbm, wout_v, sems.at[1])
            cp_a.start()
            cp_b.start()
            for acc in (acc_bin, acc_g1, acc_bout, acc_g3, acc_lng, acc_lnb, acc_dws, acc_dbs, acc_sink, acc_loss,
                        carryk_s, carryv_s):
                acc[...] = jnp.zeros(acc.shape, F32)
            tril = lax.broadcasted_iota(jnp.int32, (BLOCK, BLOCK), 0) >= lane128
            for h in range(N_SGU_HEADS):
                w = jnp.where(tril, sguw_ref[h], 0.0)
                wf_v[h // 2, :, (h % 2) * BLOCK:(h % 2 + 1) * BLOCK] = w.astype(act)
                wb_v[h // 2, :, (h % 2) * BLOCK:(h % 2 + 1) * BLOCK] = w.T.astype(act)
            cp_a.wait()
            cp_b.wait()

        g1 = g1_ref[...]

        def rms(v):
            r = lax.rsqrt(_mean_rows(v * v) + NORM_EPS)
            return r, v * r

        h_s[0:BLOCK, :] = hh_ref[...]
        h_s[BLOCK:, :] = h_ref[...]
        r1_s = r1_ref

        h = h_ref[...]
        q = _mm_nt(h, wint_v[0:OFF_K, :]) + bin_ref[:, 0:OFF_K]
        q_s[...] = (q * SCALE).astype(act)
        kv = _mm_nt(h_s[...], wint_v[OFF_K:OFF_ZA, :]) + bin_ref[:, OFF_K:OFF_ZA]
        kf_s[...] = kv[:, :BLOCK]
        vf_s[...] = kv[:, BLOCK:]
        for r in range(4):
            cols = slice(OFF_ZA + r * 512, OFF_ZA + (r + 1) * 512)
            gate_s[r] = _mm_nt(h, wint_v[cols, :]) + bin_ref[:, cols]

        lo_kv = lax.broadcasted_iota(jnp.int32, (t + BLOCK, BLOCK), 1) < HEAD_DIM
        for src, dst in ((kf_s, k2_s), (vf_s, v2_s)):
            v = src[...]
            vr = pltpu.roll(v, HEAD_DIM, 1)
            dst[0] = jnp.where(lo_kv, v, vr).astype(act)
            dst[1] = jnp.where(lo_kv, vr, v).astype(act)

        rowi = lax.broadcasted_iota(jnp.int32, (BLOCK, 2 * BLOCK), 0)
        colj = lax.broadcasted_iota(jnp.int32, (BLOCK, 2 * BLOCK), 1)
        in_band = (colj > rowi) & (colj <= rowi + BLOCK)
        row512 = lax.broadcasted_iota(jnp.int32, (4 * BLOCK, 1), 0)

        def stacked_q(b, g):
            parts = []
            for p in range(2):
                slab = q_s[b * BLOCK:(b + 1) * BLOCK, (2 * g + p) * BLOCK:(2 * g + p + 1) * BLOCK]
                parts += [jnp.where(lo, slab, jnp.zeros_like(slab)), jnp.where(lo, jnp.zeros_like(slab), slab)]
            return jnp.concatenate(parts, axis=0)

        def sink_col(g):
            s = [sinks_ref[4 * g + k] for k in range(4)]
            return jnp.where(row512 < BLOCK, s[0], jnp.where(row512 < 2 * BLOCK, s[1], jnp.where(row512 < 3 * BLOCK, s[2], s[3])))

        for b in range(nb):
            band = slice(b * BLOCK, (b + 2) * BLOCK)
            first_key = jnp.where(tile * nb + b > 0, 0, BLOCK)
            valid = in_band & (colj >= first_key)
            valid4 = jnp.concatenate([valid] * 4, axis=0)
            for g in range(2):
                s = _mm_nt(stacked_q(b, g), k2_s[g, band, :])
                s = jnp.where(valid4, s, NEG_INF)
                sk = sink_col(g)
                m = jnp.maximum(jnp.max(s, axis=1, keepdims=True), sk)
                p = jnp.exp(s - m)
                psk = jnp.exp(sk - m)
                inv = 1.0 / (jnp.sum(p, axis=1, keepdims=True) + psk)
                p = p * inv
                p_s[b * 2 + g] = p
                ps_s[b * 2 + g] = psk * inv
                o2 = _mm(p.astype(act), v2_s[g, band, :])
                for pr in range(2):
                    o_s[b * BLOCK:(b + 1) * BLOCK, (2 * g + pr) * BLOCK:(2 * g + pr + 1) * BLOCK] = jnp.where(
                        lo, o2[(2 * pr) * BLOCK:(2 * pr + 1) * BLOCK], o2[(2 * pr + 1) * BLOCK:(2 * pr + 2) * BLOCK])

        lng = lng_ref[...]
        lnb = lnb_ref[...]

        def split_pairs(val, c):
            for p in range(4):
                slab = val[:, p * BLOCK:(p + 1) * BLOCK]
                rhs_s[p, 0:BLOCK, c * BLOCK:(c + 1) * BLOCK] = jnp.where(lo, slab, 0.0).astype(act)
                rhs_s[p, BLOCK:, c * BLOCK:(c + 1) * BLOCK] = jnp.where(lo, 0.0, slab).astype(act)

        for c in range(nb):
            rows = slice(c * BLOCK, (c + 1) * BLOCK)
            za = gate_s[0, rows, :]
            mixed_ref[rows, 0:ATTN_W] = (o_s[rows, :] * (za * _sigmoid(za))).astype(act)
            us = gate_s[1, rows, :]
            vs = gate_s[2, rows, :]
            cu = _norm_cdf(us)
            cv = _norm_cdf(vs)
            cdfu_s[rows, :] = cu
            cdfv_s[rows, :] = cv
            u = us * cu
            vg = vs * cv
            vc = vg - _mean_rows(vg)
            r2 = lax.rsqrt(_mean_rows(vc * vc) + NORM_EPS)
            vhat = vc * r2
            r2_s[rows, :] = r2
            vhat_s[rows, :] = vhat
            u_s[rows, :] = u
            split_pairs(vhat * lng + lnb, c)
        for p in range(4):
            cols = slice(p * BLOCK, (p + 1) * BLOCK)
            mix = _mm(wf_v[p], rhs_s[p])
            for c in range(nb):
                mix_s[c * BLOCK:(c + 1) * BLOCK, cols] = mix[:, c * BLOCK:(c + 1) * BLOCK] + bexp_ref[:, cols]
        for c in range(nb):
            rows = slice(c * BLOCK, (c + 1) * BLOCK)
            zs = gate_s[3, rows, :]
            mixed_ref[rows, ATTN_W:] = (u_s[rows, :] * mix_s[rows, :] * (zs * _sigmoid(zs))).astype(act)

        g3 = g3_ref[...]
        proj_o = _mm(mixed_ref[...], wout_v[...])
        mixedt_ref[...] = mixed_ref[...].T
        for c in range(nb):
            rows = slice(c * BLOCK, (c + 1) * BLOCK)
            out = x_ref[rows, :] + proj_o[rows, :] + bout_ref[...]
            r3, on = rms(out)
            e = on * g3 - tgt_ref[rows, :]
            e2 = _rows8(e * e)
            acc_loss[...] += sum(e2[:, k * 128:(k + 1) * 128] for k in range(D_MODEL // 128)) * (0.5 / D_MODEL)
            acc_g3[...] += _rows8(e * on)
            don = e * g3
            dout = (r3 * (1.0 / D_MODEL)) * (don - on * _mean_rows(don * on))
            doutf_s[rows, :] = dout
            dout_ref[rows, :] = dout.astype(act)
            acc_bout[...] += _rows8(dout)
        dmix_s[...] = _mm_nt(dout_ref[...], wout_v[...])

        dkf_s[...] = jnp.zeros(dkf_s.shape, F32)
        dvf_s[...] = jnp.zeros(dvf_s.shape, F32)
        for b in range(nb):
            rows = slice(b * BLOCK, (b + 1) * BLOCK)
            band = slice(b * BLOCK, (b + 2) * BLOCK)
            za = gate_s[0, rows, :]
            sg = _sigmoid(za)
            dao = dmix_s[rows, 0:ATTN_W]
            o = o_s[rows, :]
            do = dao * (za * sg)
            dza = dao * o * (sg * (1.0 + za * (1.0 - sg)))
            dproj_ref[rows, OFF_ZA:OFF_US] = dza.astype(act)
            acc_bin[:, OFF_ZA:OFF_US] += _rows8(dza)
            for g in range(2):
                do_parts, delta_parts = [], []
                for pr in range(2):
                    cols = slice((2 * g + pr) * BLOCK, (2 * g + pr + 1) * BLOCK)
                    d_pair = do[:, cols]
                    prod = d_pair * o[:, cols]
                    do_parts += [jnp.where(lo, d_pair, 0.0).astype(act), jnp.where(lo, 0.0, d_pair).astype(act)]
                    delta_parts += [jnp.sum(jnp.where(lo, prod, 0.0), axis=1, keepdims=True),
                                    jnp.sum(jnp.where(lo, 0.0, prod), axis=1, keepdims=True)]
                do_st = jnp.concatenate(do_parts, axis=0)
                delta = jnp.concatenate(delta_parts, axis=0)
                p = p_s[b * 2 + g]

                def add_transposed(acc, stacked, by_key):
                    d2t = _mm(stacked.T, by_key)
                    acc[g * HEAD_DIM:(g + 1) * HEAD_DIM, band] += d2t[0:HEAD_DIM, :] + d2t[HEAD_DIM:, :]

                add_transposed(dvf_s, do_st, p.astype(act))
                dp = _mm_nt(do_st, v2_s[g, band, :])
                ds = p * (dp - delta)
                sink_t = ps_s[b * 2 + g] * delta
                for k in range(4):
                    acc_sink[4 * g + k:4 * g + k + 1, :] += -jnp.sum(sink_t[k * BLOCK:(k + 1) * BLOCK], axis=0, keepdims=True)
                ds_a = ds.astype(act)
                dq2 = _mm(ds_a, k2_s[g, band, :]) * SCALE
                for pr in range(2):
                    cols = slice((2 * g + pr) * BLOCK, (2 * g + pr + 1) * BLOCK)
                    dq = jnp.where(lo, dq2[(2 * pr) * BLOCK:(2 * pr + 1) * BLOCK], dq2[(2 * pr + 1) * BLOCK:(2 * pr + 2) * BLOCK])
                    dproj_ref[rows, cols] = dq.astype(act)
                    acc_bin[:, cols] += _rows8(dq)
                add_transposed(dkf_s, stacked_q(b, g), ds_a)
        for acc, carry, off in ((dkf_s, carryk_s, OFF_K), (dvf_s, carryv_s, OFF_V)):
            acc[:, t:t + BLOCK] += carry[...]
            carry[...] = acc[:, 0:BLOCK]
            d = acc[:, BLOCK:].T
            dproj_ref[:, off:off + BLOCK] = d.astype(act)
            acc_bin[:, off:off + BLOCK] += _rows8(d)

        for c in range(nb):
            rows = slice(c * BLOCK, (c + 1) * BLOCK)
            dso = dmix_s[rows, ATTN_W:]
            u = u_s[rows, :]
            mix = mix_s[rows, :]
            zs = gate_s[3, rows, :]
            sg = _sigmoid(zs)
            sgs = zs * sg
            du = dso * mix * sgs
            dmx = dso * u * sgs
            dzs = dso * u * mix * (sg * (1.0 + zs * (1.0 - sg)))
            us = gate_s[1, rows, :]
            dus = du * (cdfu_s[rows, :] + us * _norm_pdf(us))
            vln = (vhat_s[rows, :] * lng + lnb).astype(act)
            for p in range(4):
                vlnp_s[p, :, c * BLOCK:(c + 1) * BLOCK] = vln[:, p * BLOCK:(p + 1) * BLOCK]
            split_pairs(dmx, c)
            acc_dbs[...] += dmx
            for off, val in ((OFF_US, dus), (OFF_ZS, dzs)):
                dproj_ref[rows, off:off + 512] = val.astype(act)
                acc_bin[:, off:off + 512] += _rows8(val)
        for p in range(4):
            dvln = _mm(wb_v[p], rhs_s[p])
            for c in range(nb):
                dvln_s[c * BLOCK:(c + 1) * BLOCK, p * BLOCK:(p + 1) * BLOCK] = dvln[:, c * BLOCK:(c + 1) * BLOCK]
            acc_dws[(2 * p) * BLOCK:(2 * p + 1) * BLOCK, :] += _mm_nt(rhs_s[p, 0:BLOCK, :], vlnp_s[p])
            acc_dws[(2 * p + 1) * BLOCK:(2 * p + 2) * BLOCK, :] += _mm_nt(rhs_s[p, BLOCK:, :], vlnp_s[p])
        for c in range(nb):
            rows = slice(c * BLOCK, (c + 1) * BLOCK)
            dvln = dvln_s[rows, :]
            vhat = vhat_s[rows, :]
            acc_lng[...] += _rows8(dvln * vhat)
            acc_lnb[...] += _rows8(dvln)
            dvhat = dvln * lng
            dvg = r2_s[rows, :] * (dvhat - _mean_rows(dvhat) - vhat * _mean_rows(dvhat * vhat))
            vs = gate_s[2, rows, :]
            dvs = dvg * (cdfv_s[rows, :] + vs * _norm_pdf(vs))
            dproj_ref[rows, OFF_VS:OFF_VS + 512] = dvs.astype(act)
            acc_bin[:, OFF_VS:OFF_VS + 512] += _rows8(dvs)

        dh = _mm(dproj_ref[...], wint_v[...])
        dprojt_ref[...] = dproj_ref[...].T
        for c in range(nb):
            rows = slice(c * BLOCK, (c + 1) * BLOCK)
            r1 = r1_s[rows, :]
            xn = x_ref[rows, :] * r1
            dhc = dh[rows, :]
            acc_g1[...] += _rows8(dhc * xn)
            dxn = dhc * g1
            dx_ref[rows, :] = doutf_s[rows, :] + r1 * (dxn - xn * _mean_rows(dxn * xn))

        @pl.when(i == nt - 1)
        def _():
            tril = lax.broadcasted_iota(jnp.int32, (BLOCK, BLOCK), 0) >= lane128
            for hh in range(N_SGU_HEADS):
                rws = slice(hh * BLOCK, (hh + 1) * BLOCK)
                sw_ref[rws, :] = jnp.where(tril, acc_dws[rws, :], 0.0)
            vec_ref[...] = jnp.zeros((VEC_ROWS, 128), F32)

            def put(row0, acc, scale=1.0):
                s = jnp.sum(acc[...], axis=0, keepdims=True) * scale
                for k in range(acc.shape[1] // 128):
                    vec_ref[row0 + k:row0 + k + 1, :] = s[:, k * 128:(k + 1) * 128]

            put(R_G1, acc_g1)
            put(R_BIN, acc_bin)
            put(R_LNG, acc_lng)
            put(R_LNB, acc_lnb)
            put(R_BOUT, acc_bout)
            put(R_G3, acc_g3, 1.0 / D_MODEL)
            vec_ref[R_SINK:R_SINK + 1, :] = jnp.sum(
                jnp.where(lax.broadcasted_iota(jnp.int32, (8, 128), 0) == lax.broadcasted_iota(jnp.int32, (8, 128), 1),
                          acc_sink[...], 0.0), axis=0, keepdims=True)
            vec_ref[R_LOSS:R_LOSS + 1, :] = jnp.zeros((1, 128), F32) + jnp.sum(acc_loss[...])
            dbs_t = acc_dbs[...].T
            vec_ref[R_SGUB:R_SGUB + 8, :] = jnp.sum(dbs_t.reshape(N_SGU_HEADS, SGU_W // N_SGU_HEADS, BLOCK), axis=1)

    full = lambda shape: pl.BlockSpec(shape, lambda i: (0,) * len(shape))
    tok = lambda w: pl.BlockSpec((t, w), lambda i: (nt - 1 - i, 0))
    in_specs = [
        pl.BlockSpec(memory_space=pltpu.SMEM),
        tok(D_MODEL), tok(D_MODEL),
        pl.BlockSpec((BLOCK, D_MODEL), lambda i: (jnp.maximum((nt - 1 - i) * nb - 1, 0), 0)),
        tok(1), tok(D_MODEL),
        full((1, D_MODEL)), full((1, IN_W)), full((1, SGU_W)), full((1, SGU_W)),
        full((N_SGU_HEADS, BLOCK, BLOCK)), full((BLOCK, SGU_W)), full((1, D_MODEL)), full((1, D_MODEL)),
        pl.BlockSpec(memory_space=pl.ANY), pl.BlockSpec(memory_space=pl.ANY),
    ]
    out_shape = [
        jax.ShapeDtypeStruct((seq, D_MODEL), F32),
        jax.ShapeDtypeStruct((IN_W, seq), act),
        jax.ShapeDtypeStruct((D_MODEL, seq), act),
        jax.ShapeDtypeStruct((seq, D_MODEL), act),
        jax.ShapeDtypeStruct((SW_ROWS, 128), F32),
        jax.ShapeDtypeStruct((VEC_ROWS, 128), F32),
    ]
    tok_t = lambda w: pl.BlockSpec((w, t), lambda i: (0, nt - 1 - i))
    out_specs = [tok(D_MODEL), tok_t(IN_W), tok_t(D_MODEL), tok(D_MODEL), full((SW_ROWS, 128)), full((VEC_ROWS, 128))]
    vm = pltpu.VMEM
    scratch = [
        vm((t, IN_W), act), vm((t, D_MODEL), act),
        vm((IN_W, D_MODEL), act), vm((D_MODEL, D_MODEL), act),
        vm((4, BLOCK, 2 * BLOCK), act), vm((4, BLOCK, 2 * BLOCK), act),
        vm((t, ATTN_W), act),
        vm((t + BLOCK, BLOCK), F32), vm((t + BLOCK, BLOCK), F32),
        vm((2, t + BLOCK, BLOCK), act), vm((2, t + BLOCK, BLOCK), act),
        vm((4, t, 512), F32),
        vm((2 * nb, 4 * BLOCK, 2 * BLOCK), F32), vm((2 * nb, 4 * BLOCK, 1), F32),
        vm((t, ATTN_W), F32), vm((t, SGU_W), F32), vm((t, SGU_W), F32), vm((t, SGU_W), F32), vm((t, 1), F32),
        vm((t, SGU_W), F32), vm((t, SGU_W), F32),
        vm((t, D_MODEL), F32), vm((t, D_MODEL), F32),
        vm((BLOCK, t + BLOCK), F32), vm((BLOCK, t + BLOCK), F32), vm((BLOCK, BLOCK), F32), vm((BLOCK, BLOCK), F32),
        vm((8, IN_W), F32), vm((8, D_MODEL), F32), vm((8, D_MODEL), F32), vm((8, D_MODEL), F32),
        vm((8, SGU_W), F32), vm((8, SGU_W), F32), vm((N_SGU_HEADS * BLOCK, BLOCK), F32), vm((BLOCK, SGU_W), F32),
        vm((8, 128), F32), vm((8, 128), F32),
        vm((t + BLOCK, D_MODEL), act), vm((4, 2 * BLOCK, t), act), vm((4, BLOCK, t), act), vm((t, SGU_W), F32),
        pltpu.SemaphoreType.DMA((2,)),
    ]
    return pl.pallas_call(
        body, name="fused", grid=(nt,), in_specs=in_specs, out_specs=out_specs, out_shape=out_shape,
        scratch_shapes=scratch,
        compiler_params=pltpu.CompilerParams(dimension_semantics=("arbitrary",), vmem_limit_bytes=VMEM_LIMIT),
    )(sinks, x, h, h, r1, tgt, g1, b_in, ln_g, ln_b, sgu_w, bexp, b_out, g3, win_t, wout)


def _gather_call(a32, b32, x, g1):
    act = MXU_DTYPE
    seq = x.shape[0]
    xc = min(256, seq)
    n_xc = seq // xc

    def body(a32_hbm, b32_hbm, x_hbm, g1_ref, ga_hbm, gb_hbm, h_hbm, r1_hbm, a32_v, b32_v, ga_v, gb_v, x_buf, h_buf, r1_v,
             send_sems, recv_sems, loc_sems, norm_sems):
        x, y, c = lax.axis_index("x"), lax.axis_index("y"), lax.axis_index("c")
        me, sibling, x_nbr, y_nbr = (x, y, c), (x, y, 1 - c), (1 - x, y, c), (x, 1 - y, c)
        j, j_x, j_y, j_d = 2 * x + y, 2 * (1 - x) + y, 2 * x + (1 - y), 2 * (1 - x) + (1 - y)
        arrays = ((0, ga_v, ga_hbm, W_IN_SHARD), (1, gb_v, gb_hbm, W_OUT_SHARD))

        loads = [pltpu.make_async_copy(a32_hbm, a32_v, loc_sems.at[0]), pltpu.make_async_copy(b32_hbm, b32_v, loc_sems.at[1])]
        for cp in loads:
            cp.start()
        for cp in loads:
            cp.wait()
        ga_v[pl.ds(pl.multiple_of(j * W_IN_SHARD, 16), W_IN_SHARD), :] = a32_v[...].astype(act)
        gb_v[pl.ds(pl.multiple_of(j * W_OUT_SHARD, 16), W_OUT_SHARD), :] = b32_v[...].astype(act)

        def rows(shard, chip, hf=None, q=None):
            if hf is None:
                return pl.ds(pl.multiple_of(chip * shard, 16), shard)
            return pl.ds(pl.multiple_of(chip * shard + hf * (shard // 2) + q * (shard // 4), 16), shard // 4)

        def copy(k, ref, at, to):
            return pltpu.make_async_remote_copy(src_ref=ref.at[at, :], dst_ref=ref.at[at, :], send_sem=send_sems.at[k],
                                                recv_sem=recv_sems.at[k], device_id=to, device_id_type=MESH)

        stores = []

        def store(vmem, hbm, at):
            stores.append(pltpu.make_async_copy(vmem.at[at, :], hbm.at[at, :], loc_sems.at[len(stores)]))
            stores[-1].start()

        sent = []
        for w, vmem, hbm, shard in arrays:
            for k, (q, to) in enumerate(((0, x_nbr), (1, x_nbr), (1, y_nbr), (0, y_nbr))):
                sent.append(copy(12 * w + k, vmem, rows(shard, j, c, q), to))
                sent[-1].start()
        for w, vmem, hbm, shard in arrays:
            store(vmem, hbm, rows(shard, j))

        def x_load(n):
            return pltpu.make_async_copy(x_hbm.at[pl.ds(n * xc, xc), :], x_buf.at[n % 2], norm_sems.at[n % 2])

        def h_store(n):
            return pltpu.make_async_copy(h_buf.at[n % 2], h_hbm.at[pl.ds(n * xc, xc), :], norm_sems.at[2 + n % 2])

        g1 = g1_ref[...]
        x_load(0).start()
        for n in range(n_xc):
            if n + 1 < n_xc:
                x_load(n + 1).start()
            x_load(n).wait()
            if n >= 2:
                h_store(n - 2).wait()
            xv = x_buf[n % 2]
            r = lax.rsqrt(_mean_rows(xv * xv) + NORM_EPS)
            r1_v[n * xc:(n + 1) * xc, :] = r
            h_buf[n % 2] = (xv * r * g1).astype(act)
            h_store(n).start()
        for n in range(max(n_xc - 2, 0), n_xc):
            h_store(n).wait()
        r1_out = pltpu.make_async_copy(r1_v, r1_hbm, norm_sems.at[4])
        r1_out.start()

        def landed(w, vmem, hbm, shard, k, chip, q, onward=None):
            at = rows(shard, chip, c, q)
            copy(12 * w + k, vmem, at, me).wait_recv()
            if onward is not None:
                sent.append(copy(12 * w + onward[0], vmem, at, onward[1]))
                sent[-1].start()
            sent.append(copy(12 * w + 6 + k, vmem, at, sibling))
            sent[-1].start()
            store(vmem, hbm, at)

        for arr in arrays:
            landed(*arr, 0, j_x, 0, onward=(4, y_nbr))
            landed(*arr, 2, j_y, 1, onward=(5, x_nbr))
        for arr in arrays:
            landed(*arr, 1, j_x, 1)
            landed(*arr, 3, j_y, 0)
            landed(*arr, 4, j_d, 0)
            landed(*arr, 5, j_d, 1)
        for w, vmem, hbm, shard in arrays:
            for k, (chip, q) in enumerate(((j_x, 0), (j_x, 1), (j_y, 1), (j_y, 0), (j_d, 0), (j_d, 1))):
                at = rows(shard, chip, 1 - c, q)
                copy(12 * w + 6 + k, vmem, at, me).wait_recv()
                store(vmem, hbm, at)
        for cp in sent:
            cp.wait_send()
        for cp in stores:
            cp.wait()
        r1_out.wait()

    hbm = pl.BlockSpec(memory_space=pl.ANY)
    vm = pltpu.VMEM
    return pl.pallas_call(
        body, name="gather", in_specs=[hbm, hbm, hbm, pl.BlockSpec(memory_space=vm)], out_specs=[hbm, hbm, hbm, hbm],
        out_shape=[jax.ShapeDtypeStruct((IN_W, D_MODEL), act), jax.ShapeDtypeStruct((D_MODEL, D_MODEL), act),
                   jax.ShapeDtypeStruct((seq, D_MODEL), act), jax.ShapeDtypeStruct((seq, 1), F32)],
        scratch_shapes=[vm(a32.shape, F32), vm(b32.shape, F32), vm((IN_W, D_MODEL), act), vm((D_MODEL, D_MODEL), act),
                        vm((2, xc, D_MODEL), F32), vm((2, xc, D_MODEL), act), vm((seq, 1), F32),
                        pltpu.SemaphoreType.DMA((24,)), pltpu.SemaphoreType.DMA((24,)), pltpu.SemaphoreType.DMA((26,)),
                        pltpu.SemaphoreType.DMA((5,))],
        compiler_params=pltpu.CompilerParams(vmem_limit_bytes=VMEM_LIMIT),
    )(a32, b32, x, g1)


def _wgrad_reduce_call(dproj_t, h, mixed_t, dout, sw, vec):
    wire = jnp.bfloat16
    half_sw = SW_ROWS // 2
    seq = h.shape[0]
    tk = min(2048, seq)
    nk = seq // tk
    n_steps = 2 * N_CHIPS
    rel_of = lambda s: s % 3 if s < 6 else 3
    half_of = lambda s: s // 3 if s < 6 else s - 6
    x, y = lax.axis_index("x"), lax.axis_index("y")
    chip_of = [2 * (1 - x) + (1 - y), 2 * (1 - x) + y, 2 * x + (1 - y), 2 * x + y]
    order = jnp.stack([2 * chip_of[rel_of(s)] + half_of(s) for s in range(n_steps)]).astype(jnp.int32)

    def body(order_ref, dpt_ref, h_hbm, mxt_ref, dout_hbm, sw_ref, small_ref, oa_ref, ob_ref, osw_ref, osmall_ref,
             h_v, dout_v, acc_a, acc_b, sib_a, sib_b, snd_a, snd_b, in_a, in_b, own_a, own_b, fin_a, fin_b,
             all_small, sw_sib, sw_chips, sw_fin, send_sems, recv_sems, loc_sems):
        x, y, c = lax.axis_index("x"), lax.axis_index("y"), lax.axis_index("c")
        me, sibling = (x, y, c), (x, y, 1 - c)
        steps = [(1 - x, 1 - y), (1 - x, y), (x, 1 - y)]
        j = 2 * x + y
        dev = 4 * x + 2 * y + c
        b, k = pl.program_id(0), pl.program_id(1)

        def copy(n, src, dst, to):
            return pltpu.make_async_remote_copy(src_ref=src, dst_ref=dst, send_sem=send_sems.at[n], recv_sem=recv_sems.at[n],
                                                device_id=to, device_id_type=MESH)

        def sw_rows(ref, hf):
            return ref.at[pl.ds(pl.multiple_of(hf * half_sw, 8), half_sw), :]

        sm_first = [copy(16, small_ref, all_small.at[dev], sibling)]
        sm_first += [copy(17 + r, small_ref, all_small.at[dev], (*chip, c)) for r, chip in enumerate(steps)]
        sw_to_sib = copy(23, sw_rows(sw_ref, 1 - c), sw_sib, sibling)

        @pl.when((b == 0) & (k == 0))
        def _():
            all_small[dev] = small_ref[...]
            for cp in sm_first + [sw_to_sib]:
                cp.start()

        def load(kk):
            rows = pl.ds(kk * tk, tk)
            return [pltpu.make_async_copy(h_hbm.at[rows, :], h_v.at[rows, :], loc_sems.at[kk]),
                    pltpu.make_async_copy(dout_hbm.at[rows, :], dout_v.at[rows, :], loc_sems.at[nk + kk])]

        for kk in range(nk):
            @pl.when((b == 0) & (k == 0))
            def _():
                for cp in load(kk):
                    cp.start()

        for kk in range(nk):
            @pl.when((b == 0) & (k == kk))
            def _():
                for cp in load(kk):
                    cp.wait()

        tok = pl.ds(pl.multiple_of(k * tk, tk), tk)
        pa = _mm(dpt_ref[...], h_v[tok, :])
        pb = _mm(mxt_ref[...], dout_v[tok, :])
        slot = b % 2

        @pl.when(k == 0)
        def _():
            acc_a[slot] = pa
            acc_b[slot] = pb

        @pl.when(k != 0)
        def _():
            acc_a[slot] += pa
            acc_b[slot] += pb

        def to_sibling(s):
            r = rel_of(s)
            return [copy(r, acc_a.at[s % 2], sib_a.at[r], sibling), copy(4 + r, acc_b.at[s % 2], sib_b.at[r], sibling)]

        def chip_partial(s):
            r = rel_of(s)
            copy(r, sib_a.at[r], sib_a.at[r], me).wait_recv()
            copy(4 + r, sib_b.at[r], sib_b.at[r], me).wait_recv()
            return acc_a[s % 2] + sib_a[r], acc_b[s % 2] + sib_b[r]

        def to_owner(r):
            return [copy(8 + r, snd_a.at[r], in_a.at[r], (*steps[r], c)), copy(11 + r, snd_b.at[r], in_b.at[r], (*steps[r], c))]

        def vec_forwards():
            return [copy(20 + r, all_small.at[4 * cx + 2 * cy + c], all_small.at[4 * cx + 2 * cy + c], sibling)
                    for r, (cx, cy) in enumerate(steps)]

        def sw_to_chips():
            return [copy(24 + r, sw_chips.at[j], sw_chips.at[j], (*chip, c)) for r, chip in enumerate(steps)]

        def sw_to_sibling():
            return copy(27, sw_rows(osw_ref, c), sw_fin, sibling)

        @pl.when((b == 1) & (k == 0))
        def _():
            for r, (cx, cy) in enumerate(steps):
                d = 4 * cx + 2 * cy + c
                copy(17 + r, all_small.at[d], all_small.at[d], me).wait_recv()
            for cp in vec_forwards():
                cp.start()
            copy(23, sw_sib, sw_sib, me).wait_recv()
            sw_chips[j] = (sw_rows(sw_ref, c)[...] + sw_sib[...]).astype(wire)
            for cp in sw_to_chips():
                cp.start()

        @pl.when((b == 3) & (k == 0))
        def _():
            for r, (cx, cy) in enumerate(steps):
                cj = 2 * cx + cy
                copy(24 + r, sw_chips.at[cj], sw_chips.at[cj], me).wait_recv()
            tot_sw = sw_chips[0].astype(F32)
            for q in range(1, N_CHIPS):
                tot_sw = tot_sw + sw_chips[q].astype(F32)
            sw_rows(osw_ref, c)[...] = tot_sw
            sw_to_sibling().start()

        late = min(1, nk - 1)
        for s in range(n_steps):
            if s >= 1:
                sp = s - 1

                @pl.when((b == s) & (k == late) & (c == half_of(sp)))
                def _():
                    ta, tb = chip_partial(sp)
                    r = rel_of(sp)
                    if r < 3:
                        snd_a[r] = ta.astype(wire)
                        snd_b[r] = tb.astype(wire)
                        for cp in to_owner(r):
                            cp.start()
                    else:
                        own_a[...] = ta
                        own_b[...] = tb

                @pl.when((b == s) & (k == nk - 1) & (c != half_of(sp)))
                def _():
                    for cp in to_sibling(sp):
                        cp.wait_send()

            @pl.when((b == s) & (k == nk - 1) & (c != half_of(s)))
            def _():
                for cp in to_sibling(s):
                    cp.start()

        @pl.when((b == n_steps - 1) & (k == nk - 1))
        def _():
            last = n_steps - 1

            @pl.when(c == half_of(last))
            def _():
                own_a[...], own_b[...] = chip_partial(last)

            @pl.when(c != half_of(last))
            def _():
                for cp in to_sibling(last):
                    cp.wait_send()

            tot_a = own_a[...]
            tot_b = own_b[...]
            for s in range(3):
                copy(8 + s, in_a.at[s], in_a.at[s], me).wait_recv()
                copy(11 + s, in_b.at[s], in_b.at[s], me).wait_recv()
                tot_a = tot_a + in_a[s].astype(F32)
                tot_b = tot_b + in_b[s].astype(F32)
            mine_a = oa_ref.at[pl.ds(pl.multiple_of(c * HALF_A, 8), HALF_A), :]
            mine_b = ob_ref.at[pl.ds(pl.multiple_of(c * HALF_B, 8), HALF_B), :]
            mine_a[...] = tot_a
            mine_b[...] = tot_b
            back = [copy(14, mine_a, fin_a, sibling), copy(15, mine_b, fin_b, sibling)]
            for cp in back:
                cp.start()

            copy(16, small_ref, all_small.at[dev ^ 1], me).wait_recv()
            for r, (cx, cy) in enumerate(steps):
                d = 4 * cx + 2 * cy + (1 - c)
                copy(20 + r, all_small.at[d], all_small.at[d], me).wait_recv()
            tot = all_small[0]
            for d in range(1, N_DEV):
                tot = tot + all_small[d]
            osmall_ref[...] = tot

            copy(14, fin_a, fin_a, me).wait_recv()
            copy(15, fin_b, fin_b, me).wait_recv()
            copy(27, sw_fin, sw_fin, me).wait_recv()
            oa_ref[pl.ds(pl.multiple_of((1 - c) * HALF_A, 8), HALF_A), :] = fin_a[...]
            ob_ref[pl.ds(pl.multiple_of((1 - c) * HALF_B, 8), HALF_B), :] = fin_b[...]
            sw_rows(osw_ref, 1 - c)[...] = sw_fin[...]
            sends = sm_first + vec_forwards() + sw_to_chips() + back + [sw_to_sib, sw_to_sibling()]
            for s in range(3):
                sends += to_owner(s)
            for cp in sends:
                cp.wait_send()

    vmem = pl.BlockSpec(memory_space=pltpu.VMEM)
    vm = pltpu.VMEM
    grid_spec = pltpu.PrefetchScalarGridSpec(
        num_scalar_prefetch=1, grid=(n_steps, nk),
        in_specs=[pl.BlockSpec((HALF_A, tk), lambda b, k, o: (o[b], k)), pl.BlockSpec(memory_space=pl.ANY),
                  pl.BlockSpec((HALF_B, tk), lambda b, k, o: (o[b], k)), pl.BlockSpec(memory_space=pl.ANY),
                  vmem, vmem],
        out_specs=[vmem, vmem, vmem, vmem],
        scratch_shapes=[vm((seq, D_MODEL), h.dtype), vm((seq, D_MODEL), dout.dtype),
                        vm((2, HALF_A, D_MODEL), F32), vm((2, HALF_B, D_MODEL), F32),
                        vm((N_CHIPS, HALF_A, D_MODEL), F32), vm((N_CHIPS, HALF_B, D_MODEL), F32),
                        vm((3, HALF_A, D_MODEL), wire), vm((3, HALF_B, D_MODEL), wire),
                        vm((3, HALF_A, D_MODEL), wire), vm((3, HALF_B, D_MODEL), wire),
                        vm((HALF_A, D_MODEL), F32), vm((HALF_B, D_MODEL), F32),
                        vm((HALF_A, D_MODEL), F32), vm((HALF_B, D_MODEL), F32),
                        vm((N_DEV, VEC_ROWS, 128), F32), vm((half_sw, 128), F32), vm((N_CHIPS, half_sw, 128), wire),
                        vm((half_sw, 128), F32),
                        pltpu.SemaphoreType.DMA((28,)), pltpu.SemaphoreType.DMA((28,)), pltpu.SemaphoreType.DMA((2 * nk,))])
    return pl.pallas_call(
        body, name="wgrad_reduce", grid_spec=grid_spec,
        out_shape=[jax.ShapeDtypeStruct((W_IN_SHARD, D_MODEL), F32), jax.ShapeDtypeStruct((W_OUT_SHARD, D_MODEL), F32),
                   jax.ShapeDtypeStruct((SW_ROWS, 128), F32), jax.ShapeDtypeStruct((VEC_ROWS, 128), F32)],
        compiler_params=pltpu.CompilerParams(dimension_semantics=("arbitrary", "arbitrary"), vmem_limit_bytes=VMEM_LIMIT),
    )(order, dproj_t, h, mixed_t, dout, sw, vec)


def _adamw(w, g, m, v):
    nm = ADAM_B1 * m + (1.0 - ADAM_B1) * g
    nv = ADAM_B2 * v + (1.0 - ADAM_B2) * (g * g)
    m_hat = nm / (1.0 - ADAM_B1 ** ADAM_STEP)
    v_hat = nv / (1.0 - ADAM_B2 ** ADAM_STEP)
    return -ADAM_LR * (m_hat / (jnp.sqrt(v_hat) + ADAM_EPS) + ADAM_WD * w), nm, nv


def _adamw_shards_call(a, b, steps=4):
    def body(*refs):
        for k in range(2):
            w_ref, g_ref, m_ref, v_ref = refs[4 * k:4 * k + 4]
            go_ref, d_ref, nm_ref, nv_ref = refs[8 + 4 * k:12 + 4 * k]
            gg = g_ref[...]
            go_ref[...] = gg
            d_ref[...], nm_ref[...], nv_ref[...] = _adamw(w_ref[...], gg, m_ref[...], v_ref[...])

    specs, shapes = [], []
    for w in (a[0], b[0]):
        rows, cols = w.shape
        specs += [pl.BlockSpec((rows // steps, cols), lambda i: (i, 0))] * 4
        shapes += [jax.ShapeDtypeStruct((rows, cols), F32)] * 4
    return pl.pallas_call(
        body, name="adamw_shards", grid=(steps,), in_specs=specs, out_specs=specs, out_shape=shapes,
        compiler_params=pltpu.CompilerParams(dimension_semantics=("arbitrary",)),
    )(*a, *b)


_SMALL = (("norm_g", (1, D_MODEL), R_G1), ("b_in", (1, IN_W), R_BIN), ("attn_sinks", (1, 8), R_SINK),
          ("sgu_ln_g", (1, SGU_W), R_LNG), ("sgu_ln_b", (1, SGU_W), R_LNB), ("sgu_b", (N_SGU_HEADS, BLOCK), R_SGUB),
          ("b_out", (1, D_MODEL), R_BOUT), ("final_norm_g", (1, D_MODEL), R_G3))


def _adamw_small_call(sw_g, vec_g, sgu_w3, ws, ms, vs):
    n = len(_SMALL)

    def body(*refs):
        sw_ref, vec_ref = refs[0], refs[1]
        w3 = refs[2:5]
        w_refs, m_refs, v_refs = refs[5:5 + n], refs[5 + n:5 + 2 * n], refs[5 + 2 * n:5 + 3 * n]
        outs = refs[5 + 3 * n:]
        outs[0][...] = vec_ref[R_LOSS:R_LOSS + 1, 0:1]
        g = sw_ref[...]
        outs[1][...] = g
        outs[2][...], outs[3][...], outs[4][...] = _adamw(w3[0][...], g, w3[1][...], w3[2][...])
        for k, (_, shape, row) in enumerate(_SMALL):
            if shape[0] == 1 and shape[1] >= 128:
                g = jnp.concatenate([vec_ref[row + q:row + q + 1, :] for q in range(shape[1] // 128)], axis=1)
            else:
                g = vec_ref[row:row + shape[0], 0:shape[1]]
            o = outs[5 + 4 * k:9 + 4 * k]
            o[0][...] = g
            o[1][...], o[2][...], o[3][...] = _adamw(w_refs[k][...], g, m_refs[k][...], v_refs[k][...])

    vmem = pl.BlockSpec(memory_space=pltpu.VMEM)
    out_shape = [jax.ShapeDtypeStruct((1, 1), F32)] + [jax.ShapeDtypeStruct((SW_ROWS, 128), F32)] * 4
    for _, shape, _ in _SMALL:
        out_shape += [jax.ShapeDtypeStruct(shape, F32)] * 4
    args = [sw_g, vec_g, *sgu_w3, *ws, *ms, *vs]
    return pl.pallas_call(
        body, name="adamw_small", in_specs=[vmem] * len(args), out_specs=[vmem] * len(out_shape), out_shape=out_shape,
    )(*args)


def kernel(x, norm_g, w_in, b_in, attn_sinks, sgu_ln_g, sgu_ln_b, sgu_w, sgu_b, w_out, b_out, final_norm_g, loss_target, m_norm_g, m_w_in, m_b_in, m_attn_sinks, m_sgu_ln_g, m_sgu_ln_b, m_sgu_w, m_sgu_b, m_w_out, m_b_out, m_final_norm_g, v_norm_g, v_w_in, v_b_in, v_attn_sinks, v_sgu_ln_g, v_sgu_ln_b, v_sgu_w, v_sgu_b, v_w_out, v_b_out, v_final_norm_g):
    seq = x.shape[1]
    win_t, wout, h, r1 = _gather_call(w_in[0].T, w_out[0], x[0], norm_g)
    bexp = jnp.repeat(sgu_b[0].T, SGU_W // N_SGU_HEADS, axis=1)
    dx, dproj_t, mixed_t, dout, sw, vec = _fused_call(
        x[0], h, r1, loss_target[0], attn_sinks[0], norm_g, b_in, sgu_ln_g, sgu_ln_b, sgu_w[0], bexp, b_out,
        final_norm_g.reshape(1, D_MODEL), win_t, wout)
    ga_t, g_w_out, sw, vec = _wgrad_reduce_call(dproj_t, h, mixed_t, dout, sw, vec)

    names = ["norm_g", "w_in", "b_in", "attn_sinks", "sgu_ln_g", "sgu_ln_b", "sgu_w", "sgu_b", "w_out", "b_out", "final_norm_g"]
    res = {}
    shards = _adamw_shards_call((w_in[0].T, ga_t, m_w_in[0].T, v_w_in[0].T), (w_out[0], g_w_out, m_w_out[0], v_w_out[0]))
    res["w_in"] = [a.T[None] for a in shards[:4]]
    res["w_out"] = [a[None] for a in shards[4:]]
    given = dict(norm_g=(norm_g, m_norm_g, v_norm_g), b_in=(b_in, m_b_in, v_b_in), attn_sinks=(attn_sinks, m_attn_sinks, v_attn_sinks),
                 sgu_ln_g=(sgu_ln_g, m_sgu_ln_g, v_sgu_ln_g), sgu_ln_b=(sgu_ln_b, m_sgu_ln_b, v_sgu_ln_b),
                 sgu_b=(sgu_b, m_sgu_b, v_sgu_b), b_out=(b_out, m_b_out, v_b_out),
                 final_norm_g=(final_norm_g, m_final_norm_g, v_final_norm_g))
    wmv = [[given[n][k].reshape(shape) for n, shape, _ in _SMALL] for k in range(3)]
    outs = _adamw_small_call(sw, vec, [a.reshape(SW_ROWS, BLOCK) for a in (sgu_w, m_sgu_w, v_sgu_w)], *wmv)
    loss = outs[0].reshape(())
    res["sgu_w"] = [a.reshape(sgu_w.shape) for a in outs[1:5]]
    for k, (n, _, _) in enumerate(_SMALL):
        res[n] = [a.reshape(given[n][0].shape) for a in outs[5 + 4 * k:9 + 4 * k]]

    return (loss, dx[None], *[res[n][0] for n in names], *[res[n][1] for n in names], *[res[n][2] for n in names],
            *[res[n][3] for n in names])
```

```python
import functools
import math

import jax
import jax.numpy as jnp
from jax import lax
from jax.experimental import pallas as pl
from jax.experimental.pallas import tpu as pltpu

F32 = jnp.float32
MXU_DTYPE = jnp.bfloat16

D_MODEL = 1024
HEAD_DIM = 64
ATTN_W = 512
SGU_W = 512
N_SGU_HEADS = 8
BLOCK = 128
IN_W = 2816
OFF_K, OFF_V, OFF_ZA, OFF_US, OFF_VS, OFF_ZS = 512, 640, 768, 1280, 1792, 2304
NORM_EPS = 1e-5
NEG_INF = -1e30
SCALE = HEAD_DIM ** -0.5
SQRT_HALF = math.sqrt(0.5)
INV_SQRT_2PI = 1.0 / math.sqrt(2.0 * math.pi)

N_CHIPS = 4
N_DEV = 8
W_IN_SHARD = IN_W // N_CHIPS
W_OUT_SHARD = D_MODEL // N_CHIPS
HALF_A = W_IN_SHARD // 2
HALF_B = W_OUT_SHARD // 2

TILE = 256
VMEM_LIMIT = 56 * 1024 * 1024

ADAM_LR, ADAM_B1, ADAM_B2, ADAM_EPS, ADAM_WD, ADAM_STEP = 0.001, 0.9, 0.999, 1e-08, 0.01, 10

SW_ROWS = N_SGU_HEADS * BLOCK
R_G1, R_BIN, R_SINK, R_LOSS, R_LNG, R_LNB, R_SGUB, R_BOUT, R_G3 = 0, 8, 32, 40, 48, 56, 64, 72, 80
VEC_ROWS = 88

MESH = pl.DeviceIdType.MESH


def _mm(a, b):
    return jnp.dot(a, b, preferred_element_type=F32)


def _mm_nt(a, b):
    return lax.dot_general(a, b, (((1,), (1,)), ((), ())), preferred_element_type=F32)


def _sigmoid(z):
    return 1.0 / (1.0 + jnp.exp(-z))


def _norm_cdf(z):
    return 0.5 * (1.0 + lax.erf(z * SQRT_HALF))


def _norm_pdf(z):
    return jnp.exp(-0.5 * z * z) * INV_SQRT_2PI


def _rows8(v):
    r, n = v.shape
    return jnp.sum(v.reshape(r // 8, 8, n), axis=0)


def _mean_rows(v):
    return jnp.sum(v, axis=1, keepdims=True) * (1.0 / v.shape[1])


def _fused_call(x, h, r1, tgt, sinks, g1, b_in, ln_g, ln_b, sgu_w, bexp, b_out, g3, win_t, wout):
    seq = x.shape[0]
    t = TILE
    nt = seq // t
    nb = t // BLOCK
    act = MXU_DTYPE

    def body(sinks_ref, x_ref, h_ref, hh_ref, r1_ref, tgt_ref, g1_ref, bin_ref, lng_ref, lnb_ref, sguw_ref, bexp_ref, bout_ref, g3_ref,
             wint_hbm, wout_hbm,
             dx_ref, dprojt_ref, mixedt_ref, dout_ref, sw_ref, vec_ref,
             dproj_ref, mixed_ref, wint_v, wout_v, wf_v, wb_v, q_s, kf_s, vf_s, k2_s, v2_s, gate_s, p_s, ps_s, o_s, u_s, mix_s, vhat_s, r2_s,
             cdfu_s, cdfv_s, doutf_s, dmix_s, dkf_s, dvf_s, carryk_s, carryv_s,
             acc_bin, acc_g1, acc_bout, acc_g3, acc_lng, acc_lnb, acc_dws, acc_dbs, acc_sink, acc_loss,
             h_s, rhs_s, vlnp_s, dvln_s, sems):
        i = pl.program_id(0)
        tile = nt - 1 - i
        lane128 = lax.broadcasted_iota(jnp.int32, (BLOCK, BLOCK), 1)
        lo = lane128 < HEAD_DIM

        @pl.when(i == 0)
        def _():
            cp_a = pltpu.make_async_copy(wint_hbm, wint_v, sems.at[0])
            cp_b = pltpu.make_async_copy(wout_hbm, wout_v, sems.at[1])
            cp_a.start()
            cp_b.start()
            for acc in (acc_bin, acc_g1, acc_bout, acc_g3, acc_lng, acc_lnb, acc_dws, acc_dbs, acc_sink, acc_loss,
                        carryk_s, carryv_s):
                acc[...] = jnp.zeros(acc.shape, F32)
            tril = lax.broadcasted_iota(jnp.int32, (BLOCK, BLOCK), 0) >= lane128
            for h in range(N_SGU_HEADS):
                w = jnp.where(tril, sguw_ref[h], 0.0)
                wf_v[h // 2, :, (h % 2) * BLOCK:(h % 2 + 1) * BLOCK] = w.astype(act)
                wb_v[h // 2, :, (h % 2) * BLOCK:(h % 2 + 1) * BLOCK] = w.T.astype(act)
            cp_a.wait()
            cp_b.wait()

        g1 = g1_ref[...]

        def rms(v):
            r = lax.rsqrt(_mean_rows(v * v) + NORM_EPS)
            return r, v * r

        h_s[0:BLOCK, :] = hh_ref[...]
        h_s[BLOCK:, :] = h_ref[...]
        r1_s = r1_ref

        h = h_ref[...]
        q = _mm_nt(h, wint_v[0:OFF_K, :]) + bin_ref[:, 0:OFF_K]
        q_s[...] = (q * SCALE).astype(act)
        kv = _mm_nt(h_s[...], wint_v[OFF_K:OFF_ZA, :]) + bin_ref[:, OFF_K:OFF_ZA]
        kf_s[...] = kv[:, :BLOCK]
        vf_s[...] = kv[:, BLOCK:]
        for r in range(4):
            cols = slice(OFF_ZA + r * 512, OFF_ZA + (r + 1) * 512)
            gate_s[r] = _mm_nt(h, wint_v[cols, :]) + bin_ref[:, cols]

        lo_kv = lax.broadcasted_iota(jnp.int32, (t + BLOCK, BLOCK), 1) < HEAD_DIM
        for src, dst in ((kf_s, k2_s), (vf_s, v2_s)):
            v = src[...]
            vr = pltpu.roll(v, HEAD_DIM, 1)
            dst[0] = jnp.where(lo_kv, v, vr).astype(act)
            dst[1] = jnp.where(lo_kv, vr, v).astype(act)

        rowi = lax.broadcasted_iota(jnp.int32, (BLOCK, 2 * BLOCK), 0)
        colj = lax.broadcasted_iota(jnp.int32, (BLOCK, 2 * BLOCK), 1)
        in_band = (colj > rowi) & (colj <= rowi + BLOCK)
        row512 = lax.broadcasted_iota(jnp.int32, (4 * BLOCK, 1), 0)

        def stacked_q(b, g):
            parts = []
            for p in range(2):
                slab = q_s[b * BLOCK:(b + 1) * BLOCK, (2 * g + p) * BLOCK:(2 * g + p + 1) * BLOCK]
                parts += [jnp.where(lo, slab, jnp.zeros_like(slab)), jnp.where(lo, jnp.zeros_like(slab), slab)]
            return jnp.concatenate(parts, axis=0)

        def sink_col(g):
            s = [sinks_ref[4 * g + k] for k in range(4)]
            return jnp.where(row512 < BLOCK, s[0], jnp.where(row512 < 2 * BLOCK, s[1], jnp.where(row512 < 3 * BLOCK, s[2], s[3])))

        for b in range(nb):
            band = slice(b * BLOCK, (b + 2) * BLOCK)
            first_key = jnp.where(tile * nb + b > 0, 0, BLOCK)
            valid = in_band & (colj >= first_key)
            valid4 = jnp.concatenate([valid] * 4, axis=0)
            for g in range(2):
                s = _mm_nt(stacked_q(b, g), k2_s[g, band, :])
                s = jnp.where(valid4, s, NEG_INF)
                sk = sink_col(g)
                m = jnp.maximum(jnp.max(s, axis=1, keepdims=True), sk)
                p = jnp.exp(s - m)
                psk = jnp.exp(sk - m)
                inv = 1.0 / (jnp.sum(p, axis=1, keepdims=True) + psk)
                p = p * inv
                p_s[b * 2 + g] = p
                ps_s[b * 2 + g] = psk * inv
                o2 = _mm(p.astype(act), v2_s[g, band, :])
                for pr in range(2):
                    o_s[b * BLOCK:(b + 1) * BLOCK, (2 * g + pr) * BLOCK:(2 * g + pr + 1) * BLOCK] = jnp.where(
                        lo, o2[(2 * pr) * BLOCK:(2 * pr + 1) * BLOCK], o2[(2 * pr + 1) * BLOCK:(2 * pr + 2) * BLOCK])

        lng = lng_ref[...]
        lnb = lnb_ref[...]

        def split_pairs(val, c):
            for p in range(4):
                slab = val[:, p * BLOCK:(p + 1) * BLOCK]
                rhs_s[p, 0:BLOCK, c * BLOCK:(c + 1) * BLOCK] = jnp.where(lo, slab, 0.0).astype(act)
                rhs_s[p, BLOCK:, c * BLOCK:(c + 1) * BLOCK] = jnp.where(lo, 0.0, slab).astype(act)

        for c in range(nb):
            rows = slice(c * BLOCK, (c + 1) * BLOCK)
            za = gate_s[0, rows, :]
            mixed_ref[rows, 0:ATTN_W] = (o_s[rows, :] * (za * _sigmoid(za))).astype(act)
            us = gate_s[1, rows, :]
            vs = gate_s[2, rows, :]
            cu = _norm_cdf(us)
            cv = _norm_cdf(vs)
            cdfu_s[rows, :] = cu
            cdfv_s[rows, :] = cv
            u = us * cu
            vg = vs * cv
            vc = vg - _mean_rows(vg)
            r2 = lax.rsqrt(_mean_rows(vc * vc) + NORM_EPS)
            vhat = vc * r2
            r2_s[rows, :] = r2
            vhat_s[rows, :] = vhat
            u_s[rows, :] = u
            split_pairs(vhat * lng + lnb, c)
        for p in range(4):
            cols = slice(p * BLOCK, (p + 1) * BLOCK)
            mix = _mm(wf_v[p], rhs_s[p])
            for c in range(nb):
                mix_s[c * BLOCK:(c + 1) * BLOCK, cols] = mix[:, c * BLOCK:(c + 1) * BLOCK] + bexp_ref[:, cols]
        for c in range(nb):
            rows = slice(c * BLOCK, (c + 1) * BLOCK)
            zs = gate_s[3, rows, :]
            mixed_ref[rows, ATTN_W:] = (u_s[rows, :] * mix_s[rows, :] * (zs * _sigmoid(zs))).astype(act)

        g3 = g3_ref[...]
        proj_o = _mm(mixed_ref[...], wout_v[...])
        for c in range(nb):
            rows = slice(c * BLOCK, (c + 1) * BLOCK)
            out = x_ref[rows, :] + proj_o[rows, :] + bout_ref[...]
            r3, on = rms(out)
            e = on * g3 - tgt_ref[rows, :]
            e2 = _rows8(e * e)
            acc_loss[...] += sum(e2[:, k * 128:(k + 1) * 128] for k in range(D_MODEL // 128)) * (0.5 / D_MODEL)
            acc_g3[...] += _rows8(e * on)
            don = e * g3
            dout = (r3 * (1.0 / D_MODEL)) * (don - on * _mean_rows(don * on))
            doutf_s[rows, :] = dout
            dout_ref[rows, :] = dout.astype(act)
            acc_bout[...] += _rows8(dout)
        dmix_s[...] = _mm_nt(dout_ref[...], wout_v[...])

        dkf_s[...] = jnp.zeros(dkf_s.shape, F32)
        dvf_s[...] = jnp.zeros(dvf_s.shape, F32)
        for b in range(nb):
            rows = slice(b * BLOCK, (b + 1) * BLOCK)
            band = slice(b * BLOCK, (b + 2) * BLOCK)
            za = gate_s[0, rows, :]
            sg = _sigmoid(za)
            dao = dmix_s[rows, 0:ATTN_W]
            o = o_s[rows, :]
            do = dao * (za * sg)
            dza = dao * o * (sg * (1.0 + za * (1.0 - sg)))
            dproj_ref[rows, OFF_ZA:OFF_US] = dza.astype(act)
            acc_bin[:, OFF_ZA:OFF_US] += _rows8(dza)
            for g in range(2):
                do_parts, delta_parts = [], []
                for pr in range(2):
                    cols = slice((2 * g + pr) * BLOCK, (2 * g + pr + 1) * BLOCK)
                    d_pair = do[:, cols]
                    prod = d_pair * o[:, cols]
                    do_parts += [jnp.where(lo, d_pair, 0.0).astype(act), jnp.where(lo, 0.0, d_pair).astype(act)]
                    delta_parts += [jnp.sum(jnp.where(lo, prod, 0.0), axis=1, keepdims=True),
                                    jnp.sum(jnp.where(lo, 0.0, prod), axis=1, keepdims=True)]
                do_st = jnp.concatenate(do_parts, axis=0)
                delta = jnp.concatenate(delta_parts, axis=0)
                p = p_s[b * 2 + g]
                dp = _mm_nt(do_st, v2_s[g, band, :])
                ds = p * (dp - delta)
                sink_t = ps_s[b * 2 + g] * delta
                for k in range(4):
                    acc_sink[4 * g + k:4 * g + k + 1, :] += -jnp.sum(sink_t[k * BLOCK:(k + 1) * BLOCK], axis=0, keepdims=True)
                ds_a = ds.astype(act)
                dq2 = _mm(ds_a, k2_s[g, band, :]) * SCALE
                for pr in range(2):
                    cols = slice((2 * g + pr) * BLOCK, (2 * g + pr + 1) * BLOCK)
                    dq = jnp.where(lo, dq2[(2 * pr) * BLOCK:(2 * pr + 1) * BLOCK], dq2[(2 * pr + 1) * BLOCK:(2 * pr + 2) * BLOCK])
                    dproj_ref[rows, cols] = dq.astype(act)
                    acc_bin[:, cols] += _rows8(dq)
                for acc, lhs, rhs in ((dkf_s, stacked_q(b, g), ds_a), (dvf_s, do_st, p.astype(act))):
                    d2t = _mm(lhs.T, rhs)
                    acc[g * HEAD_DIM:(g + 1) * HEAD_DIM, band] += d2t[0:HEAD_DIM, :] + d2t[HEAD_DIM:, :]
        for acc, carry, off in ((dkf_s, carryk_s, OFF_K), (dvf_s, carryv_s, OFF_V)):
            acc[:, t:t + BLOCK] += carry[...]
            carry[...] = acc[:, 0:BLOCK]
            d = acc[:, BLOCK:].T
            dproj_ref[:, off:off + BLOCK] = d.astype(act)
            acc_bin[:, off:off + BLOCK] += _rows8(d)

        for c in range(nb):
            rows = slice(c * BLOCK, (c + 1) * BLOCK)
            dso = dmix_s[rows, ATTN_W:]
            u = u_s[rows, :]
            mix = mix_s[rows, :]
            zs = gate_s[3, rows, :]
            sg = _sigmoid(zs)
            sgs = zs * sg
            du = dso * mix * sgs
            dmx = dso * u * sgs
            dzs = dso * u * mix * (sg * (1.0 + zs * (1.0 - sg)))
            us = gate_s[1, rows, :]
            dus = du * (cdfu_s[rows, :] + us * _norm_pdf(us))
            vln = (vhat_s[rows, :] * lng + lnb).astype(act)
            for p in range(4):
                vlnp_s[p, :, c * BLOCK:(c + 1) * BLOCK] = vln[:, p * BLOCK:(p + 1) * BLOCK]
            split_pairs(dmx, c)
            acc_dbs[...] += dmx
            for off, val in ((OFF_US, dus), (OFF_ZS, dzs)):
                dproj_ref[rows, off:off + 512] = val.astype(act)
                acc_bin[:, off:off + 512] += _rows8(val)
        for p in range(4):
            dvln = _mm(wb_v[p], rhs_s[p])
            for c in range(nb):
                dvln_s[c * BLOCK:(c + 1) * BLOCK, p * BLOCK:(p + 1) * BLOCK] = dvln[:, c * BLOCK:(c + 1) * BLOCK]
            acc_dws[(2 * p) * BLOCK:(2 * p + 1) * BLOCK, :] += _mm_nt(rhs_s[p, 0:BLOCK, :], vlnp_s[p])
            acc_dws[(2 * p + 1) * BLOCK:(2 * p + 2) * BLOCK, :] += _mm_nt(rhs_s[p, BLOCK:, :], vlnp_s[p])
        for c in range(nb):
            rows = slice(c * BLOCK, (c + 1) * BLOCK)
            dvln = dvln_s[rows, :]
            vhat = vhat_s[rows, :]
            acc_lng[...] += _rows8(dvln * vhat)
            acc_lnb[...] += _rows8(dvln)
            dvhat = dvln * lng
            dvg = r2_s[rows, :] * (dvhat - _mean_rows(dvhat) - vhat * _mean_rows(dvhat * vhat))
            vs = gate_s[2, rows, :]
            dvs = dvg * (cdfv_s[rows, :] + vs * _norm_pdf(vs))
            dproj_ref[rows, OFF_VS:OFF_VS + 512] = dvs.astype(act)
            acc_bin[:, OFF_VS:OFF_VS + 512] += _rows8(dvs)

        dh = _mm(dproj_ref[...], wint_v[...])
        dprojt_ref[...] = dproj_ref[...].T
        mixedt_ref[...] = mixed_ref[...].T
        for c in range(nb):
            rows = slice(c * BLOCK, (c + 1) * BLOCK)
            r1 = r1_s[rows, :]
            xn = x_ref[rows, :] * r1
            dhc = dh[rows, :]
            acc_g1[...] += _rows8(dhc * xn)
            dxn = dhc * g1
            dx_ref[rows, :] = doutf_s[rows, :] + r1 * (dxn - xn * _mean_rows(dxn * xn))

        @pl.when(i == nt - 1)
        def _():
            tril = lax.broadcasted_iota(jnp.int32, (BLOCK, BLOCK), 0) >= lane128
            for hh in range(N_SGU_HEADS):
                rws = slice(hh * BLOCK, (hh + 1) * BLOCK)
                sw_ref[rws, :] = jnp.where(tril, acc_dws[rws, :], 0.0)
            vec_ref[...] = jnp.zeros((VEC_ROWS, 128), F32)

            def put(row0, acc, scale=1.0):
                s = jnp.sum(acc[...], axis=0, keepdims=True) * scale
                for k in range(acc.shape[1] // 128):
                    vec_ref[row0 + k:row0 + k + 1, :] = s[:, k * 128:(k + 1) * 128]

            put(R_G1, acc_g1)
            put(R_BIN, acc_bin)
            put(R_LNG, acc_lng)
            put(R_LNB, acc_lnb)
            put(R_BOUT, acc_bout)
            put(R_G3, acc_g3, 1.0 / D_MODEL)
            vec_ref[R_SINK:R_SINK + 1, :] = jnp.sum(
                jnp.where(lax.broadcasted_iota(jnp.int32, (8, 128), 0) == lax.broadcasted_iota(jnp.int32, (8, 128), 1),
                          acc_sink[...], 0.0), axis=0, keepdims=True)
            vec_ref[R_LOSS:R_LOSS + 1, :] = jnp.zeros((1, 128), F32) + jnp.sum(acc_loss[...])
            dbs_t = acc_dbs[...].T
            vec_ref[R_SGUB:R_SGUB + 8, :] = jnp.sum(dbs_t.reshape(N_SGU_HEADS, SGU_W // N_SGU_HEADS, BLOCK), axis=1)

    full = lambda shape: pl.BlockSpec(shape, lambda i: (0,) * len(shape))
    tok = lambda w: pl.BlockSpec((t, w), lambda i: (nt - 1 - i, 0))
    in_specs = [
        pl.BlockSpec(memory_space=pltpu.SMEM),
        tok(D_MODEL), tok(D_MODEL),
        pl.BlockSpec((BLOCK, D_MODEL), lambda i: (jnp.maximum((nt - 1 - i) * nb - 1, 0), 0)),
        tok(1), tok(D_MODEL),
        full((1, D_MODEL)), full((1, IN_W)), full((1, SGU_W)), full((1, SGU_W)),
        full((N_SGU_HEADS, BLOCK, BLOCK)), full((BLOCK, SGU_W)), full((1, D_MODEL)), full((1, D_MODEL)),
        pl.BlockSpec(memory_space=pl.ANY), pl.BlockSpec(memory_space=pl.ANY),
    ]
    out_shape = [
        jax.ShapeDtypeStruct((seq, D_MODEL), F32),
        jax.ShapeDtypeStruct((IN_W, seq), act),
        jax.ShapeDtypeStruct((D_MODEL, seq), act),
        jax.ShapeDtypeStruct((seq, D_MODEL), act),
        jax.ShapeDtypeStruct((SW_ROWS, 128), F32),
        jax.ShapeDtypeStruct((VEC_ROWS, 128), F32),
    ]
    tok_t = lambda w: pl.BlockSpec((w, t), lambda i: (0, nt - 1 - i))
    out_specs = [tok(D_MODEL), tok_t(IN_W), tok_t(D_MODEL), tok(D_MODEL), full((SW_ROWS, 128)), full((VEC_ROWS, 128))]
    vm = pltpu.VMEM
    scratch = [
        vm((t, IN_W), act), vm((t, D_MODEL), act),
        vm((IN_W, D_MODEL), act), vm((D_MODEL, D_MODEL), act),
        vm((4, BLOCK, 2 * BLOCK), act), vm((4, BLOCK, 2 * BLOCK), act),
        vm((t, ATTN_W), act),
        vm((t + BLOCK, BLOCK), F32), vm((t + BLOCK, BLOCK), F32),
        vm((2, t + BLOCK, BLOCK), act), vm((2, t + BLOCK, BLOCK), act),
        vm((4, t, 512), F32),
        vm((2 * nb, 4 * BLOCK, 2 * BLOCK), F32), vm((2 * nb, 4 * BLOCK, 1), F32),
        vm((t, ATTN_W), F32), vm((t, SGU_W), F32), vm((t, SGU_W), F32), vm((t, SGU_W), F32), vm((t, 1), F32),
        vm((t, SGU_W), F32), vm((t, SGU_W), F32),
        vm((t, D_MODEL), F32), vm((t, D_MODEL), F32),
        vm((BLOCK, t + BLOCK), F32), vm((BLOCK, t + BLOCK), F32), vm((BLOCK, BLOCK), F32), vm((BLOCK, BLOCK), F32),
        vm((8, IN_W), F32), vm((8, D_MODEL), F32), vm((8, D_MODEL), F32), vm((8, D_MODEL), F32),
        vm((8, SGU_W), F32), vm((8, SGU_W), F32), vm((N_SGU_HEADS * BLOCK, BLOCK), F32), vm((BLOCK, SGU_W), F32),
        vm((8, 128), F32), vm((8, 128), F32),
        vm((t + BLOCK, D_MODEL), act), vm((4, 2 * BLOCK, t), act), vm((4, BLOCK, t), act), vm((t, SGU_W), F32),
        pltpu.SemaphoreType.DMA((2,)),
    ]
    return pl.pallas_call(
        body, name="fused", grid=(nt,), in_specs=in_specs, out_specs=out_specs, out_shape=out_shape,
        scratch_shapes=scratch,
        compiler_params=pltpu.CompilerParams(dimension_semantics=("arbitrary",), vmem_limit_bytes=VMEM_LIMIT),
    )(sinks, x, h, h, r1, tgt, g1, b_in, ln_g, ln_b, sgu_w, bexp, b_out, g3, win_t, wout)


def _gather_call(a32, b32, x, g1):
    act = MXU_DTYPE
    seq = x.shape[0]
    xc = min(256, seq)
    n_xc = seq // xc

    def body(a32_hbm, b32_hbm, x_hbm, g1_ref, ga_hbm, gb_hbm, h_hbm, r1_hbm, a32_v, b32_v, ga_v, gb_v, x_buf, h_buf, r1_v,
             send_sems, recv_sems, loc_sems, norm_sems):
        x, y, c = lax.axis_index("x"), lax.axis_index("y"), lax.axis_index("c")
        me, sibling, x_nbr, y_nbr = (x, y, c), (x, y, 1 - c), (1 - x, y, c), (x, 1 - y, c)
        j, j_x, j_y, j_d = 2 * x + y, 2 * (1 - x) + y, 2 * x + (1 - y), 2 * (1 - x) + (1 - y)
        arrays = ((0, ga_v, ga_hbm, W_IN_SHARD), (1, gb_v, gb_hbm, W_OUT_SHARD))

        loads = [pltpu.make_async_copy(a32_hbm, a32_v, loc_sems.at[0]), pltpu.make_async_copy(b32_hbm, b32_v, loc_sems.at[1])]
        for cp in loads:
            cp.start()
        for cp in loads:
            cp.wait()
        ga_v[pl.ds(pl.multiple_of(j * W_IN_SHARD, 16), W_IN_SHARD), :] = a32_v[...].astype(act)
        gb_v[pl.ds(pl.multiple_of(j * W_OUT_SHARD, 16), W_OUT_SHARD), :] = b32_v[...].astype(act)

        def rows(shard, chip, hf=None, q=None):
            if hf is None:
                return pl.ds(pl.multiple_of(chip * shard, 16), shard)
            return pl.ds(pl.multiple_of(chip * shard + hf * (shard // 2) + q * (shard // 4), 16), shard // 4)

        def copy(k, ref, at, to):
            return pltpu.make_async_remote_copy(src_ref=ref.at[at, :], dst_ref=ref.at[at, :], send_sem=send_sems.at[k],
                                                recv_sem=recv_sems.at[k], device_id=to, device_id_type=MESH)

        stores = []

        def store(vmem, hbm, at):
            stores.append(pltpu.make_async_copy(vmem.at[at, :], hbm.at[at, :], loc_sems.at[len(stores)]))
            stores[-1].start()

        sent = []
        for w, vmem, hbm, shard in arrays:
            for k, (q, to) in enumerate(((0, x_nbr), (1, x_nbr), (1, y_nbr), (0, y_nbr))):
                sent.append(copy(12 * w + k, vmem, rows(shard, j, c, q), to))
                sent[-1].start()
        for w, vmem, hbm, shard in arrays:
            store(vmem, hbm, rows(shard, j))

        def x_load(n):
            return pltpu.make_async_copy(x_hbm.at[pl.ds(n * xc, xc), :], x_buf.at[n % 2], norm_sems.at[n % 2])

        def h_store(n):
            return pltpu.make_async_copy(h_buf.at[n % 2], h_hbm.at[pl.ds(n * xc, xc), :], norm_sems.at[2 + n % 2])

        g1 = g1_ref[...]
        x_load(0).start()
        for n in range(n_xc):
            if n + 1 < n_xc:
                x_load(n + 1).start()
            x_load(n).wait()
            if n >= 2:
                h_store(n - 2).wait()
            xv = x_buf[n % 2]
            r = lax.rsqrt(_mean_rows(xv * xv) + NORM_EPS)
            r1_v[n * xc:(n + 1) * xc, :] = r
            h_buf[n % 2] = (xv * r * g1).astype(act)
            h_store(n).start()
        for n in range(max(n_xc - 2, 0), n_xc):
            h_store(n).wait()
        r1_out = pltpu.make_async_copy(r1_v, r1_hbm, norm_sems.at[4])
        r1_out.start()

        def landed(w, vmem, hbm, shard, k, chip, q, onward=None):
            at = rows(shard, chip, c, q)
            copy(12 * w + k, vmem, at, me).wait_recv()
            if onward is not None:
                sent.append(copy(12 * w + onward[0], vmem, at, onward[1]))
                sent[-1].start()
            sent.append(copy(12 * w + 6 + k, vmem, at, sibling))
            sent[-1].start()
            store(vmem, hbm, at)

        for arr in arrays:
            landed(*arr, 0, j_x, 0, onward=(4, y_nbr))
            landed(*arr, 2, j_y, 1, onward=(5, x_nbr))
        for arr in arrays:
            landed(*arr, 1, j_x, 1)
            landed(*arr, 3, j_y, 0)
            landed(*arr, 4, j_d, 0)
            landed(*arr, 5, j_d, 1)
        for w, vmem, hbm, shard in arrays:
            for k, (chip, q) in enumerate(((j_x, 0), (j_x, 1), (j_y, 1), (j_y, 0), (j_d, 0), (j_d, 1))):
                at = rows(shard, chip, 1 - c, q)
                copy(12 * w + 6 + k, vmem, at, me).wait_recv()
                store(vmem, hbm, at)
        for cp in sent:
            cp.wait_send()
        for cp in stores:
            cp.wait()
        r1_out.wait()

    hbm = pl.BlockSpec(memory_space=pl.ANY)
    vm = pltpu.VMEM
    return pl.pallas_call(
        body, name="gather", in_specs=[hbm, hbm, hbm, pl.BlockSpec(memory_space=vm)], out_specs=[hbm, hbm, hbm, hbm],
        out_shape=[jax.ShapeDtypeStruct((IN_W, D_MODEL), act), jax.ShapeDtypeStruct((D_MODEL, D_MODEL), act),
                   jax.ShapeDtypeStruct((seq, D_MODEL), act), jax.ShapeDtypeStruct((seq, 1), F32)],
        scratch_shapes=[vm(a32.shape, F32), vm(b32.shape, F32), vm((IN_W, D_MODEL), act), vm((D_MODEL, D_MODEL), act),
                        vm((2, xc, D_MODEL), F32), vm((2, xc, D_MODEL), act), vm((seq, 1), F32),
                        pltpu.SemaphoreType.DMA((24,)), pltpu.SemaphoreType.DMA((24,)), pltpu.SemaphoreType.DMA((26,)),
                        pltpu.SemaphoreType.DMA((5,))],
        compiler_params=pltpu.CompilerParams(vmem_limit_bytes=VMEM_LIMIT),
    )(a32, b32, x, g1)


def _wgrad_reduce_call(dproj_t, h, mixed_t, dout, sw, vec):
    wire = jnp.bfloat16
    half_sw = SW_ROWS // 2
    seq = h.shape[0]
    tk = min(2048, seq)
    nk = seq // tk
    n_steps = 2 * N_CHIPS
    rel_of = lambda s: s % 3 if s < 6 else 3
    half_of = lambda s: s // 3 if s < 6 else s - 6
    x, y = lax.axis_index("x"), lax.axis_index("y")
    chip_of = [2 * (1 - x) + (1 - y), 2 * (1 - x) + y, 2 * x + (1 - y), 2 * x + y]
    order = jnp.stack([2 * chip_of[rel_of(s)] + half_of(s) for s in range(n_steps)]).astype(jnp.int32)

    def body(order_ref, dpt_ref, h_hbm, mxt_ref, dout_hbm, sw_ref, small_ref, oa_ref, ob_ref, osw_ref, osmall_ref,
             h_v, dout_v, acc_a, acc_b, sib_a, sib_b, snd_a, snd_b, in_a, in_b, own_a, own_b, fin_a, fin_b,
             all_small, sw_sib, sw_chips, sw_fin, send_sems, recv_sems, loc_sems):
        x, y, c = lax.axis_index("x"), lax.axis_index("y"), lax.axis_index("c")
        me, sibling = (x, y, c), (x, y, 1 - c)
        steps = [(1 - x, 1 - y), (1 - x, y), (x, 1 - y)]
        j = 2 * x + y
        dev = 4 * x + 2 * y + c
        b, k = pl.program_id(0), pl.program_id(1)

        def copy(n, src, dst, to):
            return pltpu.make_async_remote_copy(src_ref=src, dst_ref=dst, send_sem=send_sems.at[n], recv_sem=recv_sems.at[n],
                                                device_id=to, device_id_type=MESH)

        def sw_rows(ref, hf):
            return ref.at[pl.ds(pl.multiple_of(hf * half_sw, 8), half_sw), :]

        sm_first = [copy(16, small_ref, all_small.at[dev], sibling)]
        sm_first += [copy(17 + r, small_ref, all_small.at[dev], (*chip, c)) for r, chip in enumerate(steps)]
        sw_to_sib = copy(23, sw_rows(sw_ref, 1 - c), sw_sib, sibling)

        @pl.when((b == 0) & (k == 0))
        def _():
            all_small[dev] = small_ref[...]
            for cp in sm_first + [sw_to_sib]:
                cp.start()

        def load(kk):
            rows = pl.ds(kk * tk, tk)
            return [pltpu.make_async_copy(h_hbm.at[rows, :], h_v.at[rows, :], loc_sems.at[kk]),
                    pltpu.make_async_copy(dout_hbm.at[rows, :], dout_v.at[rows, :], loc_sems.at[nk + kk])]

        for kk in range(nk):
            @pl.when((b == 0) & (k == 0))
            def _():
                for cp in load(kk):
                    cp.start()

        for kk in range(nk):
            @pl.when((b == 0) & (k == kk))
            def _():
                for cp in load(kk):
                    cp.wait()

        tok = pl.ds(pl.multiple_of(k * tk, tk), tk)
        pa = _mm(dpt_ref[...], h_v[tok, :])
        pb = _mm(mxt_ref[...], dout_v[tok, :])
        slot = b % 2

        @pl.when(k == 0)
        def _():
            acc_a[slot] = pa
            acc_b[slot] = pb

        @pl.when(k != 0)
        def _():
            acc_a[slot] += pa
            acc_b[slot] += pb

        def to_sibling(s):
            r = rel_of(s)
            return [copy(r, acc_a.at[s % 2], sib_a.at[r], sibling), copy(4 + r, acc_b.at[s % 2], sib_b.at[r], sibling)]

        def chip_partial(s):
            r = rel_of(s)
            copy(r, sib_a.at[r], sib_a.at[r], me).wait_recv()
            copy(4 + r, sib_b.at[r], sib_b.at[r], me).wait_recv()
            return acc_a[s % 2] + sib_a[r], acc_b[s % 2] + sib_b[r]

        def to_owner(r):
            return [copy(8 + r, snd_a.at[r], in_a.at[r], (*steps[r], c)), copy(11 + r, snd_b.at[r], in_b.at[r], (*steps[r], c))]

        def vec_forwards():
            return [copy(20 + r, all_small.at[4 * cx + 2 * cy + c], all_small.at[4 * cx + 2 * cy + c], sibling)
                    for r, (cx, cy) in enumerate(steps)]

        def sw_to_chips():
            return [copy(24 + r, sw_chips.at[j], sw_chips.at[j], (*chip, c)) for r, chip in enumerate(steps)]

        def sw_to_sibling():
            return copy(27, sw_rows(osw_ref, c), sw_fin, sibling)

        @pl.when((b == 1) & (k == 0))
        def _():
            for r, (cx, cy) in enumerate(steps):
                d = 4 * cx + 2 * cy + c
                copy(17 + r, all_small.at[d], all_small.at[d], me).wait_recv()
            for cp in vec_forwards():
                cp.start()
            copy(23, sw_sib, sw_sib, me).wait_recv()
            sw_chips[j] = (sw_rows(sw_ref, c)[...] + sw_sib[...]).astype(wire)
            for cp in sw_to_chips():
                cp.start()

        @pl.when((b == 3) & (k == 0))
        def _():
            for r, (cx, cy) in enumerate(steps):
                cj = 2 * cx + cy
                copy(24 + r, sw_chips.at[cj], sw_chips.at[cj], me).wait_recv()
            tot_sw = sw_chips[0].astype(F32)
            for q in range(1, N_CHIPS):
                tot_sw = tot_sw + sw_chips[q].astype(F32)
            sw_rows(osw_ref, c)[...] = tot_sw
            sw_to_sibling().start()

        late = min(1, nk - 1)
        for s in range(n_steps):
            if s >= 1:
                sp = s - 1

                @pl.when((b == s) & (k == late) & (c == half_of(sp)))
                def _():
                    ta, tb = chip_partial(sp)
                    r = rel_of(sp)
                    if r < 3:
                        snd_a[r] = ta.astype(wire)
                        snd_b[r] = tb.astype(wire)
                        for cp in to_owner(r):
                            cp.start()
                    else:
                        own_a[...] = ta
                        own_b[...] = tb

                @pl.when((b == s) & (k == nk - 1) & (c != half_of(sp)))
                def _():
                    for cp in to_sibling(sp):
                        cp.wait_send()

            @pl.when((b == s) & (k == nk - 1) & (c != half_of(s)))
            def _():
                for cp in to_sibling(s):
                    cp.start()

        @pl.when((b == n_steps - 1) & (k == nk - 1))
        def _():
            last = n_steps - 1

            @pl.when(c == half_of(last))
            def _():
                own_a[...], own_b[...] = chip_partial(last)

            @pl.when(c != half_of(last))
            def _():
                for cp in to_sibling(last):
                    cp.wait_send()

            tot_a = own_a[...]
            tot_b = own_b[...]
            for s in range(3):
                copy(8 + s, in_a.at[s], in_a.at[s], me).wait_recv()
                copy(11 + s, in_b.at[s], in_b.at[s], me).wait_recv()
                tot_a = tot_a + in_a[s].astype(F32)
                tot_b = tot_b + in_b[s].astype(F32)
            mine_a = oa_ref.at[pl.ds(pl.multiple_of(c * HALF_A, 8), HALF_A), :]
            mine_b = ob_ref.at[pl.ds(pl.multiple_of(c * HALF_B, 8), HALF_B), :]
            mine_a[...] = tot_a
            mine_b[...] = tot_b
            back = [copy(14, mine_a, fin_a, sibling), copy(15, mine_b, fin_b, sibling)]
            for cp in back:
                cp.start()

            copy(16, small_ref, all_small.at[dev ^ 1], me).wait_recv()
            for r, (cx, cy) in enumerate(steps):
                d = 4 * cx + 2 * cy + (1 - c)
                copy(20 + r, all_small.at[d], all_small.at[d], me).wait_recv()
            tot = all_small[0]
            for d in range(1, N_DEV):
                tot = tot + all_small[d]
            osmall_ref[...] = tot

            copy(14, fin_a, fin_a, me).wait_recv()
            copy(15, fin_b, fin_b, me).wait_recv()
            copy(27, sw_fin, sw_fin, me).wait_recv()
            oa_ref[pl.ds(pl.multiple_of((1 - c) * HALF_A, 8), HALF_A), :] = fin_a[...]
            ob_ref[pl.ds(pl.multiple_of((1 - c) * HALF_B, 8), HALF_B), :] = fin_b[...]
            sw_rows(osw_ref, 1 - c)[...] = sw_fin[...]
            sends = sm_first + vec_forwards() + sw_to_chips() + back + [sw_to_sib, sw_to_sibling()]
            for s in range(3):
                sends += to_owner(s)
            for cp in sends:
                cp.wait_send()

    vmem = pl.BlockSpec(memory_space=pltpu.VMEM)
    vm = pltpu.VMEM
    grid_spec = pltpu.PrefetchScalarGridSpec(
        num_scalar_prefetch=1, grid=(n_steps, nk),
        in_specs=[pl.BlockSpec((HALF_A, tk), lambda b, k, o: (o[b], k)), pl.BlockSpec(memory_space=pl.ANY),
                  pl.BlockSpec((HALF_B, tk), lambda b, k, o: (o[b], k)), pl.BlockSpec(memory_space=pl.ANY),
                  vmem, vmem],
        out_specs=[vmem, vmem, vmem, vmem],
        scratch_shapes=[vm((seq, D_MODEL), h.dtype), vm((seq, D_MODEL), dout.dtype),
                        vm((2, HALF_A, D_MODEL), F32), vm((2, HALF_B, D_MODEL), F32),
                        vm((N_CHIPS, HALF_A, D_MODEL), F32), vm((N_CHIPS, HALF_B, D_MODEL), F32),
                        vm((3, HALF_A, D_MODEL), wire), vm((3, HALF_B, D_MODEL), wire),
                        vm((3, HALF_A, D_MODEL), wire), vm((3, HALF_B, D_MODEL), wire),
                        vm((HALF_A, D_MODEL), F32), vm((HALF_B, D_MODEL), F32),
                        vm((HALF_A, D_MODEL), F32), vm((HALF_B, D_MODEL), F32),
                        vm((N_DEV, VEC_ROWS, 128), F32), vm((half_sw, 128), F32), vm((N_CHIPS, half_sw, 128), wire),
                        vm((half_sw, 128), F32),
                        pltpu.SemaphoreType.DMA((28,)), pltpu.SemaphoreType.DMA((28,)), pltpu.SemaphoreType.DMA((2 * nk,))])
    return pl.pallas_call(
        body, name="wgrad_reduce", grid_spec=grid_spec,
        out_shape=[jax.ShapeDtypeStruct((W_IN_SHARD, D_MODEL), F32), jax.ShapeDtypeStruct((W_OUT_SHARD, D_MODEL), F32),
                   jax.ShapeDtypeStruct((SW_ROWS, 128), F32), jax.ShapeDtypeStruct((VEC_ROWS, 128), F32)],
        compiler_params=pltpu.CompilerParams(dimension_semantics=("arbitrary", "arbitrary"), vmem_limit_bytes=VMEM_LIMIT),
    )(order, dproj_t, h, mixed_t, dout, sw, vec)


def _adamw(w, g, m, v):
    nm = ADAM_B1 * m + (1.0 - ADAM_B1) * g
    nv = ADAM_B2 * v + (1.0 - ADAM_B2) * (g * g)
    m_hat = nm / (1.0 - ADAM_B1 ** ADAM_STEP)
    v_hat = nv / (1.0 - ADAM_B2 ** ADAM_STEP)
    return -ADAM_LR * (m_hat / (jnp.sqrt(v_hat) + ADAM_EPS) + ADAM_WD * w), nm, nv


_SMALL = (("norm_g", (1, D_MODEL), R_G1), ("b_in", (1, IN_W), R_BIN), ("attn_sinks", (1, 8), R_SINK),
          ("sgu_ln_g", (1, SGU_W), R_LNG), ("sgu_ln_b", (1, SGU_W), R_LNB), ("sgu_b", (N_SGU_HEADS, BLOCK), R_SGUB),
          ("b_out", (1, D_MODEL), R_BOUT), ("final_norm_g", (1, D_MODEL), R_G3))


def _adamw_call(big, vec_g, ws, ms, vs, steps=4):
    n = len(_SMALL)
    n_big = 4 * len(big)

    def body(*refs):
        ins, outs = refs[:n_big + 1 + 3 * n], refs[n_big + 1 + 3 * n:]
        for k in range(len(big)):
            w_ref, g_ref, m_ref, v_ref = ins[4 * k:4 * k + 4]
            go_ref, d_ref, nm_ref, nv_ref = outs[4 * k:4 * k + 4]
            gg = g_ref[...]
            go_ref[...] = gg
            d_ref[...], nm_ref[...], nv_ref[...] = _adamw(w_ref[...], gg, m_ref[...], v_ref[...])

        @pl.when(pl.program_id(0) == 0)
        def _():
            vec_ref = ins[n_big]
            w_refs, m_refs, v_refs = (ins[n_big + 1 + j * n:n_big + 1 + (j + 1) * n] for j in range(3))
            small = outs[n_big:]
            small[0][...] = vec_ref[R_LOSS:R_LOSS + 1, 0:1]
            for k, (_, shape, row) in enumerate(_SMALL):
                if shape[0] == 1 and shape[1] >= 128:
                    g = jnp.concatenate([vec_ref[row + q:row + q + 1, :] for q in range(shape[1] // 128)], axis=1)
                else:
                    g = vec_ref[row:row + shape[0], 0:shape[1]]
                o = small[1 + 4 * k:5 + 4 * k]
                o[0][...] = g
                o[1][...], o[2][...], o[3][...] = _adamw(w_refs[k][...], g, m_refs[k][...], v_refs[k][...])

    def whole(shape):
        return pl.BlockSpec(shape, lambda i: (0, 0))

    big_specs, big_shapes = [], []
    for arrs in big:
        rows, cols = arrs[0].shape
        big_specs += [pl.BlockSpec((rows // steps, cols), lambda i: (i, 0))] * 4
        big_shapes += [jax.ShapeDtypeStruct((rows, cols), F32)] * 4
    small_in = [whole(vec_g.shape)] + [whole(shape) for _, shape, _ in _SMALL] * 3
    small_out, small_shapes = [whole((1, 1))], [jax.ShapeDtypeStruct((1, 1), F32)]
    for _, shape, _ in _SMALL:
        small_out += [whole(shape)] * 4
        small_shapes += [jax.ShapeDtypeStruct(shape, F32)] * 4
    return pl.pallas_call(
        body, name="adamw", grid=(steps,), in_specs=big_specs + small_in, out_specs=big_specs + small_out,
        out_shape=big_shapes + small_shapes,
        compiler_params=pltpu.CompilerParams(dimension_semantics=("arbitrary",)),
    )(*[a for arrs in big for a in arrs], vec_g, *ws, *ms, *vs)


def kernel(x, norm_g, w_in, b_in, attn_sinks, sgu_ln_g, sgu_ln_b, sgu_w, sgu_b, w_out, b_out, final_norm_g, loss_target, m_norm_g, m_w_in, m_b_in, m_attn_sinks, m_sgu_ln_g, m_sgu_ln_b, m_sgu_w, m_sgu_b, m_w_out, m_b_out, m_final_norm_g, v_norm_g, v_w_in, v_b_in, v_attn_sinks, v_sgu_ln_g, v_sgu_ln_b, v_sgu_w, v_sgu_b, v_w_out, v_b_out, v_final_norm_g):
    seq = x.shape[1]
    win_t, wout, h, r1 = _gather_call(w_in[0].T, w_out[0], x[0], norm_g)
    bexp = jnp.repeat(sgu_b[0].T, SGU_W // N_SGU_HEADS, axis=1)
    dx, dproj_t, mixed_t, dout, sw, vec = _fused_call(
        x[0], h, r1, loss_target[0], attn_sinks[0], norm_g, b_in, sgu_ln_g, sgu_ln_b, sgu_w[0], bexp, b_out,
        final_norm_g.reshape(1, D_MODEL), win_t, wout)
    ga_t, g_w_out, sw, vec = _wgrad_reduce_call(dproj_t, h, mixed_t, dout, sw, vec)

    names = ["norm_g", "w_in", "b_in", "attn_sinks", "sgu_ln_g", "sgu_ln_b", "sgu_w", "sgu_b", "w_out", "b_out", "final_norm_g"]
    res = {}
    given = dict(norm_g=(norm_g, m_norm_g, v_norm_g), b_in=(b_in, m_b_in, v_b_in), attn_sinks=(attn_sinks, m_attn_sinks, v_attn_sinks),
                 sgu_ln_g=(sgu_ln_g, m_sgu_ln_g, v_sgu_ln_g), sgu_ln_b=(sgu_ln_b, m_sgu_ln_b, v_sgu_ln_b),
                 sgu_b=(sgu_b, m_sgu_b, v_sgu_b), b_out=(b_out, m_b_out, v_b_out),
                 final_norm_g=(final_norm_g, m_final_norm_g, v_final_norm_g))
    wmv = [[given[n][k].reshape(shape) for n, shape, _ in _SMALL] for k in range(3)]
    sw_w, sw_m, sw_v = (a.reshape(SW_ROWS, BLOCK) for a in (sgu_w, m_sgu_w, v_sgu_w))
    outs = _adamw_call([(w_in[0].T, ga_t, m_w_in[0].T, v_w_in[0].T), (w_out[0], g_w_out, m_w_out[0], v_w_out[0]),
                        (sw_w, sw, sw_m, sw_v)], vec, *wmv)
    res["w_in"] = [a.T[None] for a in outs[0:4]]
    res["w_out"] = [a[None] for a in outs[4:8]]
    res["sgu_w"] = [a.reshape(sgu_w.shape) for a in outs[8:12]]
    loss = outs[12].reshape(())
    for k, (n, _, _) in enumerate(_SMALL):
        res[n] = [a.reshape(given[n][0].shape) for a in outs[13 + 4 * k:17 + 4 * k]]

    return (loss, dx[None], *[res[n][0] for n in names], *[res[n][1] for n in names], *[res[n][2] for n in names],
            *[res[n][3] for n in names])
```

```python
import functools
import math

import jax
import jax.numpy as jnp
from jax import lax
from jax.experimental import pallas as pl
from jax.experimental.pallas import tpu as pltpu

F32 = jnp.float32
MXU_DTYPE = jnp.bfloat16

D_MODEL = 1024
HEAD_DIM = 64
ATTN_W = 512
SGU_W = 512
N_SGU_HEADS = 8
BLOCK = 128
IN_W = 2816
OFF_K, OFF_V, OFF_ZA, OFF_US, OFF_VS, OFF_ZS = 512, 640, 768, 1280, 1792, 2304
NORM_EPS = 1e-5
NEG_INF = -1e30
SCALE = HEAD_DIM ** -0.5
SQRT_HALF = math.sqrt(0.5)
INV_SQRT_2PI = 1.0 / math.sqrt(2.0 * math.pi)

N_CHIPS = 4
N_DEV = 8
W_IN_SHARD = IN_W // N_CHIPS
W_OUT_SHARD = D_MODEL // N_CHIPS
HALF_A = W_IN_SHARD // 2
HALF_B = W_OUT_SHARD // 2

TILE = 256
VMEM_LIMIT = 56 * 1024 * 1024

ADAM_LR, ADAM_B1, ADAM_B2, ADAM_EPS, ADAM_WD, ADAM_STEP = 0.001, 0.9, 0.999, 1e-08, 0.01, 10

SW_ROWS = N_SGU_HEADS * BLOCK
R_G1, R_BIN, R_SINK, R_LOSS, R_LNG, R_LNB, R_SGUB, R_BOUT, R_G3 = 0, 8, 32, 40, 48, 56, 64, 72, 80
VEC_ROWS = 88

MESH = pl.DeviceIdType.MESH


def _mm(a, b):
    return jnp.dot(a, b, preferred_element_type=F32)


def _mm_nt(a, b):
    return lax.dot_general(a, b, (((1,), (1,)), ((), ())), preferred_element_type=F32)


def _sigmoid(z):
    return 1.0 / (1.0 + jnp.exp(-z))


def _norm_cdf(z):
    return 0.5 * (1.0 + lax.erf(z * SQRT_HALF))


def _norm_pdf(z):
    return jnp.exp(-0.5 * z * z) * INV_SQRT_2PI


def _rows8(v):
    r, n = v.shape
    return jnp.sum(v.reshape(r // 8, 8, n), axis=0)


def _mean_rows(v):
    return jnp.sum(v, axis=1, keepdims=True) * (1.0 / v.shape[1])


def _fused_call(x, h, r1, tgt, sinks, g1, b_in, ln_g, ln_b, sgu_w, sgu_b, b_out, g3, win_t, wout):
    seq = x.shape[0]
    t = TILE
    nt = seq // t
    nb = t // BLOCK
    act = MXU_DTYPE

    def body(sinks_ref, x_ref, h_ref, hh_ref, r1_ref, tgt_ref, g1_ref, bin_ref, lng_ref, lnb_ref, sguw_ref, sgub_ref, bout_ref, g3_ref,
             wint_hbm, wout_hbm,
             dx_ref, dprojt_ref, mixedt_ref, dout_ref, sw_ref, vec_ref,
             dproj_ref, mixed_ref, wint_v, wout_v, wf_v, wb_v, q_s, kf_s, vf_s, k2_s, v2_s, gate_s, p_s, ps_s, o_s, u_s, mix_s, vhat_s, r2_s,
             cdfu_s, cdfv_s, doutf_s, dmix_s, dkf_s, dvf_s, carryk_s, carryv_s,
             acc_bin, acc_g1, acc_bout, acc_g3, acc_lng, acc_lnb, acc_dws, acc_dbs, acc_sink, acc_loss,
             h_s, rhs_s, vlnp_s, dvln_s, bexp_s, sems):
        i = pl.program_id(0)
        tile = nt - 1 - i
        lane128 = lax.broadcasted_iota(jnp.int32, (BLOCK, BLOCK), 1)
        lo = lane128 < HEAD_DIM

        @pl.when(i == 0)
        def _():
            cp_a = pltpu.make_async_copy(wint_hbm, wint_v, sems.at[0])
            cp_b = pltpu.make_async_copy(wout_hbm, wout_v, sems.at[1])
            cp_a.start()
            cp_b.start()
            for acc in (acc_bin, acc_g1, acc_bout, acc_g3, acc_lng, acc_lnb, acc_dws, acc_dbs, acc_sink, acc_loss,
                        carryk_s, carryv_s):
                acc[...] = jnp.zeros(acc.shape, F32)
            tril = lax.broadcasted_iota(jnp.int32, (BLOCK, BLOCK), 0) >= lane128
            for h in range(N_SGU_HEADS):
                w = jnp.where(tril, sguw_ref[h], 0.0)
                wf_v[h // 2, :, (h % 2) * BLOCK:(h % 2 + 1) * BLOCK] = w.astype(act)
                wb_v[h // 2, :, (h % 2) * BLOCK:(h % 2 + 1) * BLOCK] = w.T.astype(act)
            eye = lax.broadcasted_iota(jnp.int32, (BLOCK, BLOCK), 0) == lane128
            for p in range(4):
                col = [jnp.sum(jnp.where(eye, sgub_ref[2 * p + q:2 * p + q + 1, :], 0.0), axis=1, keepdims=True) for q in range(2)]
                bexp_s[:, p * BLOCK:(p + 1) * BLOCK] = jnp.where(lo, col[0], col[1])
            cp_a.wait()
            cp_b.wait()

        g1 = g1_ref[...]

        def rms(v):
            r = lax.rsqrt(_mean_rows(v * v) + NORM_EPS)
            return r, v * r

        h_s[0:BLOCK, :] = hh_ref[...]
        h_s[BLOCK:, :] = h_ref[...]
        r1_s = r1_ref

        h = h_ref[...]
        q = _mm_nt(h, wint_v[0:OFF_K, :]) + bin_ref[:, 0:OFF_K]
        q_s[...] = (q * SCALE).astype(act)
        kv = _mm_nt(h_s[...], wint_v[OFF_K:OFF_ZA, :]) + bin_ref[:, OFF_K:OFF_ZA]
        kf_s[...] = kv[:, :BLOCK]
        vf_s[...] = kv[:, BLOCK:]
        for r in range(4):
            cols = slice(OFF_ZA + r * 512, OFF_ZA + (r + 1) * 512)
            gate_s[r] = _mm_nt(h, wint_v[cols, :]) + bin_ref[:, cols]

        lo_kv = lax.broadcasted_iota(jnp.int32, (t + BLOCK, BLOCK), 1) < HEAD_DIM
        for src, dst in ((kf_s, k2_s), (vf_s, v2_s)):
            v = src[...]
            vr = pltpu.roll(v, HEAD_DIM, 1)
            dst[0] = jnp.where(lo_kv, v, vr).astype(act)
            dst[1] = jnp.where(lo_kv, vr, v).astype(act)

        rowi = lax.broadcasted_iota(jnp.int32, (BLOCK, 2 * BLOCK), 0)
        colj = lax.broadcasted_iota(jnp.int32, (BLOCK, 2 * BLOCK), 1)
        in_band = (colj > rowi) & (colj <= rowi + BLOCK)
        row512 = lax.broadcasted_iota(jnp.int32, (4 * BLOCK, 1), 0)

        def stacked_q(b, g):
            parts = []
            for p in range(2):
                slab = q_s[b * BLOCK:(b + 1) * BLOCK, (2 * g + p) * BLOCK:(2 * g + p + 1) * BLOCK]
                parts += [jnp.where(lo, slab, jnp.zeros_like(slab)), jnp.where(lo, jnp.zeros_like(slab), slab)]
            return jnp.concatenate(parts, axis=0)

        def sink_col(g):
            s = [sinks_ref[4 * g + k] for k in range(4)]
            return jnp.where(row512 < BLOCK, s[0], jnp.where(row512 < 2 * BLOCK, s[1], jnp.where(row512 < 3 * BLOCK, s[2], s[3])))

        for b in range(nb):
            band = slice(b * BLOCK, (b + 2) * BLOCK)
            first_key = jnp.where(tile * nb + b > 0, 0, BLOCK)
            valid = in_band & (colj >= first_key)
            valid4 = jnp.concatenate([valid] * 4, axis=0)
            for g in range(2):
                s = _mm_nt(stacked_q(b, g), k2_s[g, band, :])
                s = jnp.where(valid4, s, NEG_INF)
                sk = sink_col(g)
                m = jnp.maximum(jnp.max(s, axis=1, keepdims=True), sk)
                p = jnp.exp(s - m)
                psk = jnp.exp(sk - m)
                inv = 1.0 / (jnp.sum(p, axis=1, keepdims=True) + psk)
                p = p * inv
                p_s[b * 2 + g] = p
                ps_s[b * 2 + g] = psk * inv
                o2 = _mm(p.astype(act), v2_s[g, band, :])
                for pr in range(2):
                    o_s[b * BLOCK:(b + 1) * BLOCK, (2 * g + pr) * BLOCK:(2 * g + pr + 1) * BLOCK] = jnp.where(
                        lo, o2[(2 * pr) * BLOCK:(2 * pr + 1) * BLOCK], o2[(2 * pr + 1) * BLOCK:(2 * pr + 2) * BLOCK])

        lng = lng_ref[...]
        lnb = lnb_ref[...]

        def split_pairs(val, c):
            for p in range(4):
                slab = val[:, p * BLOCK:(p + 1) * BLOCK]
                rhs_s[p, 0:BLOCK, c * BLOCK:(c + 1) * BLOCK] = jnp.where(lo, slab, 0.0).astype(act)
                rhs_s[p, BLOCK:, c * BLOCK:(c + 1) * BLOCK] = jnp.where(lo, 0.0, slab).astype(act)

        for c in range(nb):
            rows = slice(c * BLOCK, (c + 1) * BLOCK)
            za = gate_s[0, rows, :]
            mixed_ref[rows, 0:ATTN_W] = (o_s[rows, :] * (za * _sigmoid(za))).astype(act)
            us = gate_s[1, rows, :]
            vs = gate_s[2, rows, :]
            cu = _norm_cdf(us)
            cv = _norm_cdf(vs)
            cdfu_s[rows, :] = cu
            cdfv_s[rows, :] = cv
            u = us * cu
            vg = vs * cv
            vc = vg - _mean_rows(vg)
            r2 = lax.rsqrt(_mean_rows(vc * vc) + NORM_EPS)
            vhat = vc * r2
            r2_s[rows, :] = r2
            vhat_s[rows, :] = vhat
            u_s[rows, :] = u
            split_pairs(vhat * lng + lnb, c)
        for p in range(4):
            cols = slice(p * BLOCK, (p + 1) * BLOCK)
            mix = _mm(wf_v[p], rhs_s[p])
            for c in range(nb):
                mix_s[c * BLOCK:(c + 1) * BLOCK, cols] = mix[:, c * BLOCK:(c + 1) * BLOCK] + bexp_s[:, cols]
        for c in range(nb):
            rows = slice(c * BLOCK, (c + 1) * BLOCK)
            zs = gate_s[3, rows, :]
            mixed_ref[rows, ATTN_W:] = (u_s[rows, :] * mix_s[rows, :] * (zs * _sigmoid(zs))).astype(act)

        g3 = g3_ref[...]
        proj_o = _mm(mixed_ref[...], wout_v[...])
        for c in range(nb):
            rows = slice(c * BLOCK, (c + 1) * BLOCK)
            out = x_ref[rows, :] + proj_o[rows, :] + bout_ref[...]
            r3, on = rms(out)
            e = on * g3 - tgt_ref[rows, :]
            e2 = _rows8(e * e)
            acc_loss[...] += sum(e2[:, k * 128:(k + 1) * 128] for k in range(D_MODEL // 128)) * (0.5 / D_MODEL)
            acc_g3[...] += _rows8(e * on)
            don = e * g3
            dout = (r3 * (1.0 / D_MODEL)) * (don - on * _mean_rows(don * on))
            doutf_s[rows, :] = dout
            dout_ref[rows, :] = dout.astype(act)
            acc_bout[...] += _rows8(dout)
        dmix_s[...] = _mm_nt(dout_ref[...], wout_v[...])

        dkf_s[...] = jnp.zeros(dkf_s.shape, F32)
        dvf_s[...] = jnp.zeros(dvf_s.shape, F32)
        for b in range(nb):
            rows = slice(b * BLOCK, (b + 1) * BLOCK)
            band = slice(b * BLOCK, (b + 2) * BLOCK)
            za = gate_s[0, rows, :]
            sg = _sigmoid(za)
            dao = dmix_s[rows, 0:ATTN_W]
            o = o_s[rows, :]
            do = dao * (za * sg)
            dza = dao * o * (sg * (1.0 + za * (1.0 - sg)))
            dproj_ref[rows, OFF_ZA:OFF_US] = dza.astype(act)
            acc_bin[:, OFF_ZA:OFF_US] += _rows8(dza)
            for g in range(2):
                do_parts, delta_parts = [], []
                for pr in range(2):
                    cols = slice((2 * g + pr) * BLOCK, (2 * g + pr + 1) * BLOCK)
                    d_pair = do[:, cols]
                    prod = d_pair * o[:, cols]
                    do_parts += [jnp.where(lo, d_pair, 0.0).astype(act), jnp.where(lo, 0.0, d_pair).astype(act)]
                    delta_parts += [jnp.sum(jnp.where(lo, prod, 0.0), axis=1, keepdims=True),
                                    jnp.sum(jnp.where(lo, 0.0, prod), axis=1, keepdims=True)]
                do_st = jnp.concatenate(do_parts, axis=0)
                delta = jnp.concatenate(delta_parts, axis=0)
                p = p_s[b * 2 + g]
                dp = _mm_nt(do_st, v2_s[g, band, :])
                ds = p * (dp - delta)
                sink_t = ps_s[b * 2 + g] * delta
                for k in range(4):
                    acc_sink[4 * g + k:4 * g + k + 1, :] += -jnp.sum(sink_t[k * BLOCK:(k + 1) * BLOCK], axis=0, keepdims=True)
                ds_a = ds.astype(act)
                dq2 = _mm(ds_a, k2_s[g, band, :]) * SCALE
                for pr in range(2):
                    cols = slice((2 * g + pr) * BLOCK, (2 * g + pr + 1) * BLOCK)
                    dq = jnp.where(lo, dq2[(2 * pr) * BLOCK:(2 * pr + 1) * BLOCK], dq2[(2 * pr + 1) * BLOCK:(2 * pr + 2) * BLOCK])
                    dproj_ref[rows, cols] = dq.astype(act)
                    acc_bin[:, cols] += _rows8(dq)
                for acc, lhs, rhs in ((dkf_s, stacked_q(b, g), ds_a), (dvf_s, do_st, p.astype(act))):
                    d2t = _mm(lhs.T, rhs)
                    acc[g * HEAD_DIM:(g + 1) * HEAD_DIM, band] += d2t[0:HEAD_DIM, :] + d2t[HEAD_DIM:, :]
        for acc, carry, off in ((dkf_s, carryk_s, OFF_K), (dvf_s, carryv_s, OFF_V)):
            acc[:, t:t + BLOCK] += carry[...]
            carry[...] = acc[:, 0:BLOCK]
            d = acc[:, BLOCK:].T
            dproj_ref[:, off:off + BLOCK] = d.astype(act)
            acc_bin[:, off:off + BLOCK] += _rows8(d)

        for c in range(nb):
            rows = slice(c * BLOCK, (c + 1) * BLOCK)
            dso = dmix_s[rows, ATTN_W:]
            u = u_s[rows, :]
            mix = mix_s[rows, :]
            zs = gate_s[3, rows, :]
            sg = _sigmoid(zs)
            sgs = zs * sg
            du = dso * mix * sgs
            dmx = dso * u * sgs
            dzs = dso * u * mix * (sg * (1.0 + zs * (1.0 - sg)))
            us = gate_s[1, rows, :]
            dus = du * (cdfu_s[rows, :] + us * _norm_pdf(us))
            vln = (vhat_s[rows, :] * lng + lnb).astype(act)
            for p in range(4):
                vlnp_s[p, :, c * BLOCK:(c + 1) * BLOCK] = vln[:, p * BLOCK:(p + 1) * BLOCK]
            split_pairs(dmx, c)
            acc_dbs[...] += dmx
            for off, val in ((OFF_US, dus), (OFF_ZS, dzs)):
                dproj_ref[rows, off:off + 512] = val.astype(act)
                acc_bin[:, off:off + 512] += _rows8(val)
        for p in range(4):
            dvln = _mm(wb_v[p], rhs_s[p])
            for c in range(nb):
                dvln_s[c * BLOCK:(c + 1) * BLOCK, p * BLOCK:(p + 1) * BLOCK] = dvln[:, c * BLOCK:(c + 1) * BLOCK]
            acc_dws[(2 * p) * BLOCK:(2 * p + 1) * BLOCK, :] += _mm_nt(rhs_s[p, 0:BLOCK, :], vlnp_s[p])
            acc_dws[(2 * p + 1) * BLOCK:(2 * p + 2) * BLOCK, :] += _mm_nt(rhs_s[p, BLOCK:, :], vlnp_s[p])
        for c in range(nb):
            rows = slice(c * BLOCK, (c + 1) * BLOCK)
            dvln = dvln_s[rows, :]
            vhat = vhat_s[rows, :]
            acc_lng[...] += _rows8(dvln * vhat)
            acc_lnb[...] += _rows8(dvln)
            dvhat = dvln * lng
            dvg = r2_s[rows, :] * (dvhat - _mean_rows(dvhat) - vhat * _mean_rows(dvhat * vhat))
            vs = gate_s[2, rows, :]
            dvs = dvg * (cdfv_s[rows, :] + vs * _norm_pdf(vs))
            dproj_ref[rows, OFF_VS:OFF_VS + 512] = dvs.astype(act)
            acc_bin[:, OFF_VS:OFF_VS + 512] += _rows8(dvs)

        dh = _mm(dproj_ref[...], wint_v[...])
        dprojt_ref[...] = dproj_ref[...].T
        mixedt_ref[...] = mixed_ref[...].T
        for c in range(nb):
            rows = slice(c * BLOCK, (c + 1) * BLOCK)
            r1 = r1_s[rows, :]
            xn = x_ref[rows, :] * r1
            dhc = dh[rows, :]
            acc_g1[...] += _rows8(dhc * xn)
            dxn = dhc * g1
            dx_ref[rows, :] = doutf_s[rows, :] + r1 * (dxn - xn * _mean_rows(dxn * xn))

        @pl.when(i == nt - 1)
        def _():
            tril = lax.broadcasted_iota(jnp.int32, (BLOCK, BLOCK), 0) >= lane128
            for hh in range(N_SGU_HEADS):
                rws = slice(hh * BLOCK, (hh + 1) * BLOCK)
                sw_ref[rws, :] = jnp.where(tril, acc_dws[rws, :], 0.0)
            vec_ref[...] = jnp.zeros((VEC_ROWS, 128), F32)

            def put(row0, acc, scale=1.0):
                s = jnp.sum(acc[...], axis=0, keepdims=True) * scale
                for k in range(acc.shape[1] // 128):
                    vec_ref[row0 + k:row0 + k + 1, :] = s[:, k * 128:(k + 1) * 128]

            put(R_G1, acc_g1)
            put(R_BIN, acc_bin)
            put(R_LNG, acc_lng)
            put(R_LNB, acc_lnb)
            put(R_BOUT, acc_bout)
            put(R_G3, acc_g3, 1.0 / D_MODEL)
            vec_ref[R_SINK:R_SINK + 1, :] = jnp.sum(
                jnp.where(lax.broadcasted_iota(jnp.int32, (8, 128), 0) == lax.broadcasted_iota(jnp.int32, (8, 128), 1),
                          acc_sink[...], 0.0), axis=0, keepdims=True)
            vec_ref[R_LOSS:R_LOSS + 1, :] = jnp.zeros((1, 128), F32) + jnp.sum(acc_loss[...])
            dbs_t = acc_dbs[...].T
            vec_ref[R_SGUB:R_SGUB + 8, :] = jnp.sum(dbs_t.reshape(N_SGU_HEADS, SGU_W // N_SGU_HEADS, BLOCK), axis=1)

    full = lambda shape: pl.BlockSpec(shape, lambda i: (0,) * len(shape))
    tok = lambda w: pl.BlockSpec((t, w), lambda i: (nt - 1 - i, 0))
    in_specs = [
        pl.BlockSpec(memory_space=pltpu.SMEM),
        tok(D_MODEL), tok(D_MODEL),
        pl.BlockSpec((BLOCK, D_MODEL), lambda i: (jnp.maximum((nt - 1 - i) * nb - 1, 0), 0)),
        tok(1), tok(D_MODEL),
        full((1, D_MODEL)), full((1, IN_W)), full((1, SGU_W)), full((1, SGU_W)),
        full((N_SGU_HEADS, BLOCK, BLOCK)), full((N_SGU_HEADS, BLOCK)), full((1, D_MODEL)), full((1, D_MODEL)),
        pl.BlockSpec(memory_space=pl.ANY), pl.BlockSpec(memory_space=pl.ANY),
    ]
    out_shape = [
        jax.ShapeDtypeStruct((seq, D_MODEL), F32),
        jax.ShapeDtypeStruct((IN_W, seq), act),
        jax.ShapeDtypeStruct((D_MODEL, seq), act),
        jax.ShapeDtypeStruct((seq, D_MODEL), act),
        jax.ShapeDtypeStruct((SW_ROWS, 128), F32),
        jax.ShapeDtypeStruct((VEC_ROWS, 128), F32),
    ]
    tok_t = lambda w: pl.BlockSpec((w, t), lambda i: (0, nt - 1 - i))
    out_specs = [tok(D_MODEL), tok_t(IN_W), tok_t(D_MODEL), tok(D_MODEL), full((SW_ROWS, 128)), full((VEC_ROWS, 128))]
    vm = pltpu.VMEM
    scratch = [
        vm((t, IN_W), act), vm((t, D_MODEL), act),
        vm((IN_W, D_MODEL), act), vm((D_MODEL, D_MODEL), act),
        vm((4, BLOCK, 2 * BLOCK), act), vm((4, BLOCK, 2 * BLOCK), act),
        vm((t, ATTN_W), act),
        vm((t + BLOCK, BLOCK), F32), vm((t + BLOCK, BLOCK), F32),
        vm((2, t + BLOCK, BLOCK), act), vm((2, t + BLOCK, BLOCK), act),
        vm((4, t, 512), F32),
        vm((2 * nb, 4 * BLOCK, 2 * BLOCK), F32), vm((2 * nb, 4 * BLOCK, 1), F32),
        vm((t, ATTN_W), F32), vm((t, SGU_W), F32), vm((t, SGU_W), F32), vm((t, SGU_W), F32), vm((t, 1), F32),
        vm((t, SGU_W), F32), vm((t, SGU_W), F32),
        vm((t, D_MODEL), F32), vm((t, D_MODEL), F32),
        vm((BLOCK, t + BLOCK), F32), vm((BLOCK, t + BLOCK), F32), vm((BLOCK, BLOCK), F32), vm((BLOCK, BLOCK), F32),
        vm((8, IN_W), F32), vm((8, D_MODEL), F32), vm((8, D_MODEL), F32), vm((8, D_MODEL), F32),
        vm((8, SGU_W), F32), vm((8, SGU_W), F32), vm((N_SGU_HEADS * BLOCK, BLOCK), F32), vm((BLOCK, SGU_W), F32),
        vm((8, 128), F32), vm((8, 128), F32),
        vm((t + BLOCK, D_MODEL), act), vm((4, 2 * BLOCK, t), act), vm((4, BLOCK, t), act), vm((t, SGU_W), F32),
        vm((BLOCK, SGU_W), F32),
        pltpu.SemaphoreType.DMA((2,)),
    ]
    return pl.pallas_call(
        body, name="fused", grid=(nt,), in_specs=in_specs, out_specs=out_specs, out_shape=out_shape,
        scratch_shapes=scratch,
        compiler_params=pltpu.CompilerParams(dimension_semantics=("arbitrary",), vmem_limit_bytes=VMEM_LIMIT),
    )(sinks, x, h, h, r1, tgt, g1, b_in, ln_g, ln_b, sgu_w, sgu_b, b_out, g3, win_t, wout)


def _gather_call(a32, b32, x, g1):
    act = MXU_DTYPE
    seq = x.shape[0]
    xc = min(256, seq)
    n_xc = seq // xc

    def body(a32_hbm, b32_hbm, x_hbm, g1_ref, ga_hbm, gb_hbm, h_hbm, r1_hbm, a32_v, b32_v, ga_v, gb_v, x_buf, h_buf, r1_v,
             send_sems, recv_sems, loc_sems, norm_sems):
        x, y, c = lax.axis_index("x"), lax.axis_index("y"), lax.axis_index("c")
        me, sibling, x_nbr, y_nbr = (x, y, c), (x, y, 1 - c), (1 - x, y, c), (x, 1 - y, c)
        j, j_x, j_y, j_d = 2 * x + y, 2 * (1 - x) + y, 2 * x + (1 - y), 2 * (1 - x) + (1 - y)
        arrays = ((0, ga_v, ga_hbm, W_IN_SHARD), (1, gb_v, gb_hbm, W_OUT_SHARD))

        loads = [pltpu.make_async_copy(a32_hbm, a32_v, loc_sems.at[0]), pltpu.make_async_copy(b32_hbm, b32_v, loc_sems.at[1])]
        for cp in loads:
            cp.start()
        for cp in loads:
            cp.wait()
        ga_v[pl.ds(pl.multiple_of(j * W_IN_SHARD, 16), W_IN_SHARD), :] = a32_v[...].astype(act)
        gb_v[pl.ds(pl.multiple_of(j * W_OUT_SHARD, 16), W_OUT_SHARD), :] = b32_v[...].astype(act)

        def rows(shard, chip, hf=None, q=None):
            if hf is None:
                return pl.ds(pl.multiple_of(chip * shard, 16), shard)
            return pl.ds(pl.multiple_of(chip * shard + hf * (shard // 2) + q * (shard // 4), 16), shard // 4)

        def copy(k, ref, at, to):
            return pltpu.make_async_remote_copy(src_ref=ref.at[at, :], dst_ref=ref.at[at, :], send_sem=send_sems.at[k],
                                                recv_sem=recv_sems.at[k], device_id=to, device_id_type=MESH)

        stores = []

        def store(vmem, hbm, at):
            stores.append(pltpu.make_async_copy(vmem.at[at, :], hbm.at[at, :], loc_sems.at[len(stores)]))
            stores[-1].start()

        sent = []
        for w, vmem, hbm, shard in arrays:
            for k, (q, to) in enumerate(((0, x_nbr), (1, x_nbr), (1, y_nbr), (0, y_nbr))):
                sent.append(copy(12 * w + k, vmem, rows(shard, j, c, q), to))
                sent[-1].start()
        for w, vmem, hbm, shard in arrays:
            store(vmem, hbm, rows(shard, j))

        def x_load(n):
            return pltpu.make_async_copy(x_hbm.at[pl.ds(n * xc, xc), :], x_buf.at[n % 2], norm_sems.at[n % 2])

        def h_store(n):
            return pltpu.make_async_copy(h_buf.at[n % 2], h_hbm.at[pl.ds(n * xc, xc), :], norm_sems.at[2 + n % 2])

        g1 = g1_ref[...]
        x_load(0).start()
        for n in range(n_xc):
            if n + 1 < n_xc:
                x_load(n + 1).start()
            x_load(n).wait()
            if n >= 2:
                h_store(n - 2).wait()
            xv = x_buf[n % 2]
            r = lax.rsqrt(_mean_rows(xv * xv) + NORM_EPS)
            r1_v[n * xc:(n + 1) * xc, :] = r
            h_buf[n % 2] = (xv * r * g1).astype(act)
            h_store(n).start()
        for n in range(max(n_xc - 2, 0), n_xc):
            h_store(n).wait()
        r1_out = pltpu.make_async_copy(r1_v, r1_hbm, norm_sems.at[4])
        r1_out.start()

        def landed(w, vmem, hbm, shard, k, chip, q, onward=None):
            at = rows(shard, chip, c, q)
            copy(12 * w + k, vmem, at, me).wait_recv()
            if onward is not None:
                sent.append(copy(12 * w + onward[0], vmem, at, onward[1]))
                sent[-1].start()
            sent.append(copy(12 * w + 6 + k, vmem, at, sibling))
            sent[-1].start()
            store(vmem, hbm, at)

        for arr in arrays:
            landed(*arr, 0, j_x, 0, onward=(4, y_nbr))
            landed(*arr, 2, j_y, 1, onward=(5, x_nbr))
        for arr in arrays:
            landed(*arr, 1, j_x, 1)
            landed(*arr, 3, j_y, 0)
            landed(*arr, 4, j_d, 0)
            landed(*arr, 5, j_d, 1)
        for w, vmem, hbm, shard in arrays:
            for k, (chip, q) in enumerate(((j_x, 0), (j_x, 1), (j_y, 1), (j_y, 0), (j_d, 0), (j_d, 1))):
                at = rows(shard, chip, 1 - c, q)
                copy(12 * w + 6 + k, vmem, at, me).wait_recv()
                store(vmem, hbm, at)
        for cp in sent:
            cp.wait_send()
        for cp in stores:
            cp.wait()
        r1_out.wait()

    hbm = pl.BlockSpec(memory_space=pl.ANY)
    vm = pltpu.VMEM
    return pl.pallas_call(
        body, name="gather", in_specs=[hbm, hbm, hbm, pl.BlockSpec(memory_space=vm)], out_specs=[hbm, hbm, hbm, hbm],
        out_shape=[jax.ShapeDtypeStruct((IN_W, D_MODEL), act), jax.ShapeDtypeStruct((D_MODEL, D_MODEL), act),
                   jax.ShapeDtypeStruct((seq, D_MODEL), act), jax.ShapeDtypeStruct((seq, 1), F32)],
        scratch_shapes=[vm(a32.shape, F32), vm(b32.shape, F32), vm((IN_W, D_MODEL), act), vm((D_MODEL, D_MODEL), act),
                        vm((2, xc, D_MODEL), F32), vm((2, xc, D_MODEL), act), vm((seq, 1), F32),
                        pltpu.SemaphoreType.DMA((24,)), pltpu.SemaphoreType.DMA((24,)), pltpu.SemaphoreType.DMA((26,)),
                        pltpu.SemaphoreType.DMA((5,))],
        compiler_params=pltpu.CompilerParams(vmem_limit_bytes=VMEM_LIMIT),
    )(a32, b32, x, g1)


def _wgrad_reduce_call(dproj_t, h, mixed_t, dout, sw, vec):
    wire = jnp.bfloat16
    half_sw = SW_ROWS // 2
    seq = h.shape[0]
    tk = min(2048, seq)
    nk = seq // tk
    n_steps = 2 * N_CHIPS
    rel_of = lambda s: s % 3 if s < 6 else 3
    half_of = lambda s: s // 3 if s < 6 else s - 6
    x, y = lax.axis_index("x"), lax.axis_index("y")
    chip_of = [2 * (1 - x) + (1 - y), 2 * (1 - x) + y, 2 * x + (1 - y), 2 * x + y]
    order = jnp.stack([2 * chip_of[rel_of(s)] + half_of(s) for s in range(n_steps)]).astype(jnp.int32)

    def body(order_ref, dpt_ref, h_hbm, mxt_ref, dout_hbm, sw_ref, small_ref, oa_ref, ob_ref, osw_ref, osmall_ref,
             h_v, dout_v, acc_a, acc_b, sib_a, sib_b, snd_a, snd_b, in_a, in_b, own_a, own_b, fin_a, fin_b,
             all_small, sw_sib, sw_chips, sw_fin, send_sems, recv_sems, loc_sems):
        x, y, c = lax.axis_index("x"), lax.axis_index("y"), lax.axis_index("c")
        me, sibling = (x, y, c), (x, y, 1 - c)
        steps = [(1 - x, 1 - y), (1 - x, y), (x, 1 - y)]
        j = 2 * x + y
        dev = 4 * x + 2 * y + c
        b, k = pl.program_id(0), pl.program_id(1)

        def copy(n, src, dst, to):
            return pltpu.make_async_remote_copy(src_ref=src, dst_ref=dst, send_sem=send_sems.at[n], recv_sem=recv_sems.at[n],
                                                device_id=to, device_id_type=MESH)

        def sw_rows(ref, hf):
            return ref.at[pl.ds(pl.multiple_of(hf * half_sw, 8), half_sw), :]

        sm_first = [copy(16, small_ref, all_small.at[dev], sibling)]
        sm_first += [copy(17 + r, small_ref, all_small.at[dev], (*chip, c)) for r, chip in enumerate(steps)]
        sw_to_sib = copy(23, sw_rows(sw_ref, 1 - c), sw_sib, sibling)

        @pl.when((b == 0) & (k == 0))
        def _():
            all_small[dev] = small_ref[...]
            for cp in sm_first + [sw_to_sib]:
                cp.start()

        def load(kk):
            rows = pl.ds(kk * tk, tk)
            return [pltpu.make_async_copy(h_hbm.at[rows, :], h_v.at[rows, :], loc_sems.at[kk]),
                    pltpu.make_async_copy(dout_hbm.at[rows, :], dout_v.at[rows, :], loc_sems.at[nk + kk])]

        for kk in range(nk):
            @pl.when((b == 0) & (k == 0))
            def _():
                for cp in load(kk):
                    cp.start()

        for kk in range(nk):
            @pl.when((b == 0) & (k == kk))
            def _():
                for cp in load(kk):
                    cp.wait()

        tok = pl.ds(pl.multiple_of(k * tk, tk), tk)
        pa = _mm(dpt_ref[...], h_v[tok, :])
        pb = _mm(mxt_ref[...], dout_v[tok, :])
        slot = b % 2

        @pl.when(k == 0)
        def _():
            acc_a[slot] = pa
            acc_b[slot] = pb

        @pl.when(k != 0)
        def _():
            acc_a[slot] += pa
            acc_b[slot] += pb

        def to_sibling(s):
            r = rel_of(s)
            return [copy(r, acc_a.at[s % 2], sib_a.at[r], sibling), copy(4 + r, acc_b.at[s % 2], sib_b.at[r], sibling)]

        def chip_partial(s):
            r = rel_of(s)
            copy(r, sib_a.at[r], sib_a.at[r], me).wait_recv()
            copy(4 + r, sib_b.at[r], sib_b.at[r], me).wait_recv()
            return acc_a[s % 2] + sib_a[r], acc_b[s % 2] + sib_b[r]

        def to_owner(r):
            return [copy(8 + r, snd_a.at[r], in_a.at[r], (*steps[r], c)), copy(11 + r, snd_b.at[r], in_b.at[r], (*steps[r], c))]

        def vec_forwards():
            return [copy(20 + r, all_small.at[4 * cx + 2 * cy + c], all_small.at[4 * cx + 2 * cy + c], sibling)
                    for r, (cx, cy) in enumerate(steps)]

        def sw_to_chips():
            return [copy(24 + r, sw_chips.at[j], sw_chips.at[j], (*chip, c)) for r, chip in enumerate(steps)]

        def sw_to_sibling():
            return copy(27, sw_rows(osw_ref, c), sw_fin, sibling)

        @pl.when((b == 1) & (k == 0))
        def _():
            for r, (cx, cy) in enumerate(steps):
                d = 4 * cx + 2 * cy + c
                copy(17 + r, all_small.at[d], all_small.at[d], me).wait_recv()
            for cp in vec_forwards():
                cp.start()
            copy(23, sw_sib, sw_sib, me).wait_recv()
            sw_chips[j] = (sw_rows(sw_ref, c)[...] + sw_sib[...]).astype(wire)
            for cp in sw_to_chips():
                cp.start()

        @pl.when((b == 3) & (k == 0))
        def _():
            for r, (cx, cy) in enumerate(steps):
                cj = 2 * cx + cy
                copy(24 + r, sw_chips.at[cj], sw_chips.at[cj], me).wait_recv()
            tot_sw = sw_chips[0].astype(F32)
            for q in range(1, N_CHIPS):
                tot_sw = tot_sw + sw_chips[q].astype(F32)
            sw_rows(osw_ref, c)[...] = tot_sw
            sw_to_sibling().start()

        late = min(1, nk - 1)
        for s in range(n_steps):
            if s >= 1:
                sp = s - 1

                @pl.when((b == s) & (k == late) & (c == half_of(sp)))
                def _():
                    ta, tb = chip_partial(sp)
                    r = rel_of(sp)
                    if r < 3:
                        snd_a[r] = ta.astype(wire)
                        snd_b[r] = tb.astype(wire)
                        for cp in to_owner(r):
                            cp.start()
                    else:
                        own_a[...] = ta
                        own_b[...] = tb

                @pl.when((b == s) & (k == nk - 1) & (c != half_of(sp)))
                def _():
                    for cp in to_sibling(sp):
                        cp.wait_send()

            @pl.when((b == s) & (k == nk - 1) & (c != half_of(s)))
            def _():
                for cp in to_sibling(s):
                    cp.start()

        @pl.when((b == n_steps - 1) & (k == nk - 1))
        def _():
            last = n_steps - 1

            @pl.when(c == half_of(last))
            def _():
                own_a[...], own_b[...] = chip_partial(last)

            @pl.when(c != half_of(last))
            def _():
                for cp in to_sibling(last):
                    cp.wait_send()

            tot_a = own_a[...]
            tot_b = own_b[...]
            for s in range(3):
                copy(8 + s, in_a.at[s], in_a.at[s], me).wait_recv()
                copy(11 + s, in_b.at[s], in_b.at[s], me).wait_recv()
                tot_a = tot_a + in_a[s].astype(F32)
                tot_b = tot_b + in_b[s].astype(F32)
            mine_a = oa_ref.at[pl.ds(pl.multiple_of(c * HALF_A, 8), HALF_A), :]
            mine_b = ob_ref.at[pl.ds(pl.multiple_of(c * HALF_B, 8), HALF_B), :]
            mine_a[...] = tot_a
            mine_b[...] = tot_b
            back = [copy(14, mine_a, fin_a, sibling), copy(15, mine_b, fin_b, sibling)]
            for cp in back:
                cp.start()

            copy(16, small_ref, all_small.at[dev ^ 1], me).wait_recv()
            for r, (cx, cy) in enumerate(steps):
                d = 4 * cx + 2 * cy + (1 - c)
                copy(20 + r, all_small.at[d], all_small.at[d], me).wait_recv()
            tot = all_small[0]
            for d in range(1, N_DEV):
                tot = tot + all_small[d]
            osmall_ref[...] = tot

            copy(14, fin_a, fin_a, me).wait_recv()
            copy(15, fin_b, fin_b, me).wait_recv()
            copy(27, sw_fin, sw_fin, me).wait_recv()
            oa_ref[pl.ds(pl.multiple_of((1 - c) * HALF_A, 8), HALF_A), :] = fin_a[...]
            ob_ref[pl.ds(pl.multiple_of((1 - c) * HALF_B, 8), HALF_B), :] = fin_b[...]
            sw_rows(osw_ref, 1 - c)[...] = sw_fin[...]
            sends = sm_first + vec_forwards() + sw_to_chips() + back + [sw_to_sib, sw_to_sibling()]
            for s in range(3):
                sends += to_owner(s)
            for cp in sends:
                cp.wait_send()

    vmem = pl.BlockSpec(memory_space=pltpu.VMEM)
    vm = pltpu.VMEM
    grid_spec = pltpu.PrefetchScalarGridSpec(
        num_scalar_prefetch=1, grid=(n_steps, nk),
        in_specs=[pl.BlockSpec((HALF_A, tk), lambda b, k, o: (o[b], k)), pl.BlockSpec(memory_space=pl.ANY),
                  pl.BlockSpec((HALF_B, tk), lambda b, k, o: (o[b], k)), pl.BlockSpec(memory_space=pl.ANY),
                  vmem, vmem],
        out_specs=[vmem, vmem, vmem, vmem],
        scratch_shapes=[vm((seq, D_MODEL), h.dtype), vm((seq, D_MODEL), dout.dtype),
                        vm((2, HALF_A, D_MODEL), F32), vm((2, HALF_B, D_MODEL), F32),
                        vm((N_CHIPS, HALF_A, D_MODEL), F32), vm((N_CHIPS, HALF_B, D_MODEL), F32),
                        vm((3, HALF_A, D_MODEL), wire), vm((3, HALF_B, D_MODEL), wire),
                        vm((3, HALF_A, D_MODEL), wire), vm((3, HALF_B, D_MODEL), wire),
                        vm((HALF_A, D_MODEL), F32), vm((HALF_B, D_MODEL), F32),
                        vm((HALF_A, D_MODEL), F32), vm((HALF_B, D_MODEL), F32),
                        vm((N_DEV, VEC_ROWS, 128), F32), vm((half_sw, 128), F32), vm((N_CHIPS, half_sw, 128), wire),
                        vm((half_sw, 128), F32),
                        pltpu.SemaphoreType.DMA((28,)), pltpu.SemaphoreType.DMA((28,)), pltpu.SemaphoreType.DMA((2 * nk,))])
    return pl.pallas_call(
        body, name="wgrad_reduce", grid_spec=grid_spec,
        out_shape=[jax.ShapeDtypeStruct((W_IN_SHARD, D_MODEL), F32), jax.ShapeDtypeStruct((W_OUT_SHARD, D_MODEL), F32),
                   jax.ShapeDtypeStruct((SW_ROWS, 128), F32), jax.ShapeDtypeStruct((VEC_ROWS, 128), F32)],
        compiler_params=pltpu.CompilerParams(dimension_semantics=("arbitrary", "arbitrary"), vmem_limit_bytes=VMEM_LIMIT),
    )(order, dproj_t, h, mixed_t, dout, sw, vec)


def _adamw(w, g, m, v):
    nm = ADAM_B1 * m + (1.0 - ADAM_B1) * g
    nv = ADAM_B2 * v + (1.0 - ADAM_B2) * (g * g)
    m_hat = nm / (1.0 - ADAM_B1 ** ADAM_STEP)
    v_hat = nv / (1.0 - ADAM_B2 ** ADAM_STEP)
    return -ADAM_LR * (m_hat / (jnp.sqrt(v_hat) + ADAM_EPS) + ADAM_WD * w), nm, nv


def _adamw_shards_call(a, b, steps=4):
    def body(*refs):
        for k in range(2):
            w_ref, g_ref, m_ref, v_ref = refs[4 * k:4 * k + 4]
            go_ref, d_ref, nm_ref, nv_ref = refs[8 + 4 * k:12 + 4 * k]
            gg = g_ref[...]
            go_ref[...] = gg
            d_ref[...], nm_ref[...], nv_ref[...] = _adamw(w_ref[...], gg, m_ref[...], v_ref[...])

    specs, shapes = [], []
    for w in (a[0], b[0]):
        rows, cols = w.shape
        specs += [pl.BlockSpec((rows // steps, cols), lambda i: (i, 0))] * 4
        shapes += [jax.ShapeDtypeStruct((rows, cols), F32)] * 4
    return pl.pallas_call(
        body, name="adamw_shards", grid=(steps,), in_specs=specs, out_specs=specs, out_shape=shapes,
        compiler_params=pltpu.CompilerParams(dimension_semantics=("arbitrary",)),
    )(*a, *b)


_SMALL = (("norm_g", (1, D_MODEL), R_G1), ("b_in", (1, IN_W), R_BIN), ("attn_sinks", (1, 8), R_SINK),
          ("sgu_ln_g", (1, SGU_W), R_LNG), ("sgu_ln_b", (1, SGU_W), R_LNB), ("sgu_b", (N_SGU_HEADS, BLOCK), R_SGUB),
          ("b_out", (1, D_MODEL), R_BOUT), ("final_norm_g", (1, D_MODEL), R_G3))


def _adamw_small_call(sw_g, vec_g, sgu_w3, ws, ms, vs):
    n = len(_SMALL)

    def body(*refs):
        sw_ref, vec_ref = refs[0], refs[1]
        w3 = refs[2:5]
        w_refs, m_refs, v_refs = refs[5:5 + n], refs[5 + n:5 + 2 * n], refs[5 + 2 * n:5 + 3 * n]
        outs = refs[5 + 3 * n:]
        outs[0][...] = vec_ref[R_LOSS:R_LOSS + 1, 0:1]
        g = sw_ref[...]
        outs[1][...] = g
        outs[2][...], outs[3][...], outs[4][...] = _adamw(w3[0][...], g, w3[1][...], w3[2][...])
        for k, (_, shape, row) in enumerate(_SMALL):
            if shape[0] == 1 and shape[1] >= 128:
                g = jnp.concatenate([vec_ref[row + q:row + q + 1, :] for q in range(shape[1] // 128)], axis=1)
            else:
                g = vec_ref[row:row + shape[0], 0:shape[1]]
            o = outs[5 + 4 * k:9 + 4 * k]
            o[0][...] = g
            o[1][...], o[2][...], o[3][...] = _adamw(w_refs[k][...], g, m_refs[k][...], v_refs[k][...])

    vmem = pl.BlockSpec(memory_space=pltpu.VMEM)
    out_shape = [jax.ShapeDtypeStruct((1, 1), F32)] + [jax.ShapeDtypeStruct((SW_ROWS, 128), F32)] * 4
    for _, shape, _ in _SMALL:
        out_shape += [jax.ShapeDtypeStruct(shape, F32)] * 4
    args = [sw_g, vec_g, *sgu_w3, *ws, *ms, *vs]
    return pl.pallas_call(
        body, name="adamw_small", in_specs=[vmem] * len(args), out_specs=[vmem] * len(out_shape), out_shape=out_shape,
    )(*args)


def kernel(x, norm_g, w_in, b_in, attn_sinks, sgu_ln_g, sgu_ln_b, sgu_w, sgu_b, w_out, b_out, final_norm_g, loss_target, m_norm_g, m_w_in, m_b_in, m_attn_sinks, m_sgu_ln_g, m_sgu_ln_b, m_sgu_w, m_sgu_b, m_w_out, m_b_out, m_final_norm_g, v_norm_g, v_w_in, v_b_in, v_attn_sinks, v_sgu_ln_g, v_sgu_ln_b, v_sgu_w, v_sgu_b, v_w_out, v_b_out, v_final_norm_g):
    seq = x.shape[1]
    win_t, wout, h, r1 = _gather_call(w_in[0].T, w_out[0], x[0], norm_g)
    dx, dproj_t, mixed_t, dout, sw, vec = _fused_call(
        x[0], h, r1, loss_target[0], attn_sinks[0], norm_g, b_in, sgu_ln_g, sgu_ln_b, sgu_w[0], sgu_b[0], b_out,
        final_norm_g.reshape(1, D_MODEL), win_t, wout)
    ga_t, g_w_out, sw, vec = _wgrad_reduce_call(dproj_t, h, mixed_t, dout, sw, vec)

    names = ["norm_g", "w_in", "b_in", "attn_sinks", "sgu_ln_g", "sgu_ln_b", "sgu_w", "sgu_b", "w_out", "b_out", "final_norm_g"]
    res = {}
    shards = _adamw_shards_call((w_in[0].T, ga_t, m_w_in[0].T, v_w_in[0].T), (w_out[0], g_w_out, m_w_out[0], v_w_out[0]))
    res["w_in"] = [a.T[None] for a in shards[:4]]
    res["w_out"] = [a[None] for a in shards[4:]]
    given = dict(norm_g=(norm_g, m_norm_g, v_norm_g), b_in=(b_in, m_b_in, v_b_in), attn_sinks=(attn_sinks, m_attn_sinks, v_attn_sinks),
                 sgu_ln_g=(sgu_ln_g, m_sgu_ln_g, v_sgu_ln_g), sgu_ln_b=(sgu_ln_b, m_sgu_ln_b, v_sgu_ln_b),
                 sgu_b=(sgu_b, m_sgu_b, v_sgu_b), b_out=(b_out, m_b_out, v_b_out),
                 final_norm_g=(final_norm_g, m_final_norm_g, v_final_norm_g))
    wmv = [[given[n][k].reshape(shape) for n, shape, _ in _SMALL] for k in range(3)]
    outs = _adamw_small_call(sw, vec, [a.reshape(SW_ROWS, BLOCK) for a in (sgu_w, m_sgu_w, v_sgu_w)], *wmv)
    loss = outs[0].reshape(())
    res["sgu_w"] = [a.reshape(sgu_w.shape) for a in outs[1:5]]
    for k, (n, _, _) in enumerate(_SMALL):
        res[n] = [a.reshape(given[n][0].shape) for a in outs[5 + 4 * k:9 + 4 * k]]

    return (loss, dx[None], *[res[n][0] for n in names], *[res[n][1] for n in names], *[res[n][2] for n in names],
            *[res[n][3] for n in names])
```

```python
import functools
import math

import jax
import jax.numpy as jnp
from jax import lax
from jax.experimental import pallas as pl
from jax.experimental.pallas import tpu as pltpu

F32 = jnp.float32
MXU_DTYPE = jnp.bfloat16

D_MODEL = 1024
HEAD_DIM = 64
ATTN_W = 512
SGU_W = 512
N_SGU_HEADS = 8
BLOCK = 128
IN_W = 2816
OFF_K, OFF_V, OFF_ZA, OFF_US, OFF_VS, OFF_ZS = 512, 640, 768, 1280, 1792, 2304
NORM_EPS = 1e-5
NEG_INF = -1e30
SCALE = HEAD_DIM ** -0.5
SQRT_HALF = math.sqrt(0.5)
INV_SQRT_2PI = 1.0 / math.sqrt(2.0 * math.pi)

N_CHIPS = 4
N_DEV = 8
W_IN_SHARD = IN_W // N_CHIPS
W_OUT_SHARD = D_MODEL // N_CHIPS
HALF_A = W_IN_SHARD // 2
HALF_B = W_OUT_SHARD // 2

TILE = 256
VMEM_LIMIT = 56 * 1024 * 1024

ADAM_LR, ADAM_B1, ADAM_B2, ADAM_EPS, ADAM_WD, ADAM_STEP = 0.001, 0.9, 0.999, 1e-08, 0.01, 10

SW_ROWS = N_SGU_HEADS * BLOCK
R_G1, R_BIN, R_SINK, R_LOSS, R_LNG, R_LNB, R_SGUB, R_BOUT, R_G3 = 0, 8, 32, 40, 48, 56, 64, 72, 80
VEC_ROWS = 88

MESH = pl.DeviceIdType.MESH


def _mm(a, b):
    return jnp.dot(a, b, preferred_element_type=F32)


def _mm_nt(a, b):
    return lax.dot_general(a, b, (((1,), (1,)), ((), ())), preferred_element_type=F32)


def _sigmoid(z):
    return 1.0 / (1.0 + jnp.exp(-z))


def _norm_cdf(z):
    return 0.5 * (1.0 + lax.erf(z * SQRT_HALF))


def _norm_pdf(z):
    return jnp.exp(-0.5 * z * z) * INV_SQRT_2PI


def _rows8(v):
    r, n = v.shape
    return jnp.sum(v.reshape(r // 8, 8, n), axis=0)


def _mean_rows(v):
    return jnp.sum(v, axis=1, keepdims=True) * (1.0 / v.shape[1])


def _fused_call(x, h, r1, tgt, sinks, g1, b_in, ln_g, ln_b, sgu_w, sgu_b, b_out, g3, win_t, wout):
    seq = x.shape[0]
    t = TILE
    nt = seq // t
    nb = t // BLOCK
    act = MXU_DTYPE

    def body(sinks_ref, x_ref, h_ref, hh_ref, r1_ref, tgt_ref, g1_ref, bin_ref, lng_ref, lnb_ref, sguw_ref, sgub_ref, bout_ref, g3_ref,
             wint_hbm, wout_hbm,
             dx_ref, dprojt_ref, mixedt_ref, dout_ref, sw_ref, vec_ref,
             dproj_ref, mixed_ref, wint_v, wout_v, wf_v, wb_v, q_s, kf_s, vf_s, k2_s, v2_s, gate_s, p_s, ps_s, o_s, u_s, mix_s, vhat_s, r2_s,
             cdfu_s, cdfv_s, doutf_s, dmix_s, dkf_s, dvf_s, carryk_s, carryv_s,
             acc_bin, acc_g1, acc_bout, acc_g3, acc_lng, acc_lnb, acc_dws, acc_dbs, acc_sink, acc_loss,
             h_s, rhs_s, vlnp_s, dvln_s, bexp_s, sems):
        i = pl.program_id(0)
        tile = nt - 1 - i
        lane128 = lax.broadcasted_iota(jnp.int32, (BLOCK, BLOCK), 1)
        lo = lane128 < HEAD_DIM

        @pl.when(i == 0)
        def _():
            cp_a = pltpu.make_async_copy(wint_hbm, wint_v, sems.at[0])
            cp_b = pltpu.make_async_copy(wout_hbm, wout_v, sems.at[1])
            cp_a.start()
            cp_b.start()
            for acc in (acc_bin, acc_g1, acc_bout, acc_g3, acc_lng, acc_lnb, acc_dws, acc_dbs, acc_sink, acc_loss,
                        carryk_s, carryv_s):
                acc[...] = jnp.zeros(acc.shape, F32)
            tril = lax.broadcasted_iota(jnp.int32, (BLOCK, BLOCK), 0) >= lane128
            for h in range(N_SGU_HEADS):
                w = jnp.where(tril, sguw_ref[h], 0.0)
                wf_v[h // 2, :, (h % 2) * BLOCK:(h % 2 + 1) * BLOCK] = w.astype(act)
                wb_v[h // 2, :, (h % 2) * BLOCK:(h % 2 + 1) * BLOCK] = w.T.astype(act)
            eye = lax.broadcasted_iota(jnp.int32, (BLOCK, BLOCK), 0) == lane128
            for p in range(4):
                col = [jnp.sum(jnp.where(eye, sgub_ref[2 * p + q:2 * p + q + 1, :], 0.0), axis=1, keepdims=True) for q in range(2)]
                bexp_s[:, p * BLOCK:(p + 1) * BLOCK] = jnp.where(lo, col[0], col[1])
            cp_a.wait()
            cp_b.wait()

        g1 = g1_ref[...]

        def rms(v):
            r = lax.rsqrt(_mean_rows(v * v) + NORM_EPS)
            return r, v * r

        h_s[0:BLOCK, :] = hh_ref[...]
        h_s[BLOCK:, :] = h_ref[...]
        r1_s = r1_ref

        h = h_ref[...]
        q = _mm_nt(h, wint_v[0:OFF_K, :]) + bin_ref[:, 0:OFF_K]
        q_s[...] = (q * SCALE).astype(act)
        kv = _mm_nt(h_s[...], wint_v[OFF_K:OFF_ZA, :]) + bin_ref[:, OFF_K:OFF_ZA]
        kf_s[...] = kv[:, :BLOCK]
        vf_s[...] = kv[:, BLOCK:]
        for r in range(4):
            cols = slice(OFF_ZA + r * 512, OFF_ZA + (r + 1) * 512)
            gate_s[r] = _mm_nt(h, wint_v[cols, :]) + bin_ref[:, cols]

        lo_kv = lax.broadcasted_iota(jnp.int32, (t + BLOCK, BLOCK), 1) < HEAD_DIM
        for src, dst in ((kf_s, k2_s), (vf_s, v2_s)):
            v = src[...]
            vr = pltpu.roll(v, HEAD_DIM, 1)
            dst[0] = jnp.where(lo_kv, v, vr).astype(act)
            dst[1] = jnp.where(lo_kv, vr, v).astype(act)

        rowi = lax.broadcasted_iota(jnp.int32, (BLOCK, 2 * BLOCK), 0)
        colj = lax.broadcasted_iota(jnp.int32, (BLOCK, 2 * BLOCK), 1)
        in_band = (colj > rowi) & (colj <= rowi + BLOCK)
        row512 = lax.broadcasted_iota(jnp.int32, (4 * BLOCK, 1), 0)

        def stacked_q(b, g):
            parts = []
            for p in range(2):
                slab = q_s[b * BLOCK:(b + 1) * BLOCK, (2 * g + p) * BLOCK:(2 * g + p + 1) * BLOCK]
                parts += [jnp.where(lo, slab, jnp.zeros_like(slab)), jnp.where(lo, jnp.zeros_like(slab), slab)]
            return jnp.concatenate(parts, axis=0)

        def sink_col(g):
            s = [sinks_ref[4 * g + k] for k in range(4)]
            return jnp.where(row512 < BLOCK, s[0], jnp.where(row512 < 2 * BLOCK, s[1], jnp.where(row512 < 3 * BLOCK, s[2], s[3])))

        for b in range(nb):
            band = slice(b * BLOCK, (b + 2) * BLOCK)
            first_key = jnp.where(tile * nb + b > 0, 0, BLOCK)
            valid = in_band & (colj >= first_key)
            valid4 = jnp.concatenate([valid] * 4, axis=0)
            for g in range(2):
                s = _mm_nt(stacked_q(b, g), k2_s[g, band, :])
                s = jnp.where(valid4, s, NEG_INF)
                sk = sink_col(g)
                m = jnp.maximum(jnp.max(s, axis=1, keepdims=True), sk)
                p = jnp.exp(s - m)
                psk = jnp.exp(sk - m)
                inv = 1.0 / (jnp.sum(p, axis=1, keepdims=True) + psk)
                p = p * inv
                p_s[b * 2 + g] = p
                ps_s[b * 2 + g] = psk * inv
                o2 = _mm(p.astype(act), v2_s[g, band, :])
                for pr in range(2):
                    o_s[b * BLOCK:(b + 1) * BLOCK, (2 * g + pr) * BLOCK:(2 * g + pr + 1) * BLOCK] = jnp.where(
                        lo, o2[(2 * pr) * BLOCK:(2 * pr + 1) * BLOCK], o2[(2 * pr + 1) * BLOCK:(2 * pr + 2) * BLOCK])

        lng = lng_ref[...]
        lnb = lnb_ref[...]

        def split_pairs(val, c):
            for p in range(4):
                slab = val[:, p * BLOCK:(p + 1) * BLOCK]
                rhs_s[p, 0:BLOCK, c * BLOCK:(c + 1) * BLOCK] = jnp.where(lo, slab, 0.0).astype(act)
                rhs_s[p, BLOCK:, c * BLOCK:(c + 1) * BLOCK] = jnp.where(lo, 0.0, slab).astype(act)

        for c in range(nb):
            rows = slice(c * BLOCK, (c + 1) * BLOCK)
            za = gate_s[0, rows, :]
            mixed_ref[rows, 0:ATTN_W] = (o_s[rows, :] * (za * _sigmoid(za))).astype(act)
            us = gate_s[1, rows, :]
            vs = gate_s[2, rows, :]
            cu = _norm_cdf(us)
            cv = _norm_cdf(vs)
            cdfu_s[rows, :] = cu
            cdfv_s[rows, :] = cv
            u = us * cu
            vg = vs * cv
            vc = vg - _mean_rows(vg)
            r2 = lax.rsqrt(_mean_rows(vc * vc) + NORM_EPS)
            vhat = vc * r2
            r2_s[rows, :] = r2
            vhat_s[rows, :] = vhat
            u_s[rows, :] = u
            split_pairs(vhat * lng + lnb, c)
        for p in range(4):
            cols = slice(p * BLOCK, (p + 1) * BLOCK)
            mix = _mm(wf_v[p], rhs_s[p])
            for c in range(nb):
                mix_s[c * BLOCK:(c + 1) * BLOCK, cols] = mix[:, c * BLOCK:(c + 1) * BLOCK] + bexp_s[:, cols]
        for c in range(nb):
            rows = slice(c * BLOCK, (c + 1) * BLOCK)
            zs = gate_s[3, rows, :]
            mixed_ref[rows, ATTN_W:] = (u_s[rows, :] * mix_s[rows, :] * (zs * _sigmoid(zs))).astype(act)

        g3 = g3_ref[...]
        proj_o = _mm(mixed_ref[...], wout_v[...])
        for c in range(nb):
            rows = slice(c * BLOCK, (c + 1) * BLOCK)
            out = x_ref[rows, :] + proj_o[rows, :] + bout_ref[...]
            r3, on = rms(out)
            e = on * g3 - tgt_ref[rows, :]
            e2 = _rows8(e * e)
            acc_loss[...] += sum(e2[:, k * 128:(k + 1) * 128] for k in range(D_MODEL // 128)) * (0.5 / D_MODEL)
            acc_g3[...] += _rows8(e * on)
            don = e * g3
            dout = (r3 * (1.0 / D_MODEL)) * (don - on * _mean_rows(don * on))
            doutf_s[rows, :] = dout
            dout_ref[rows, :] = dout.astype(act)
            acc_bout[...] += _rows8(dout)
        dmix_s[...] = _mm_nt(dout_ref[...], wout_v[...])

        dkf_s[...] = jnp.zeros(dkf_s.shape, F32)
        dvf_s[...] = jnp.zeros(dvf_s.shape, F32)
        for b in range(nb):
            rows = slice(b * BLOCK, (b + 1) * BLOCK)
            band = slice(b * BLOCK, (b + 2) * BLOCK)
            za = gate_s[0, rows, :]
            sg = _sigmoid(za)
            dao = dmix_s[rows, 0:ATTN_W]
            o = o_s[rows, :]
            do = dao * (za * sg)
            dza = dao * o * (sg * (1.0 + za * (1.0 - sg)))
            dproj_ref[rows, OFF_ZA:OFF_US] = dza.astype(act)
            acc_bin[:, OFF_ZA:OFF_US] += _rows8(dza)
            for g in range(2):
                do_parts, delta_parts = [], []
                for pr in range(2):
                    cols = slice((2 * g + pr) * BLOCK, (2 * g + pr + 1) * BLOCK)
                    d_pair = do[:, cols]
                    prod = d_pair * o[:, cols]
                    do_parts += [jnp.where(lo, d_pair, 0.0).astype(act), jnp.where(lo, 0.0, d_pair).astype(act)]
                    delta_parts += [jnp.sum(jnp.where(lo, prod, 0.0), axis=1, keepdims=True),
                                    jnp.sum(jnp.where(lo, 0.0, prod), axis=1, keepdims=True)]
                do_st = jnp.concatenate(do_parts, axis=0)
                delta = jnp.concatenate(delta_parts, axis=0)
                p = p_s[b * 2 + g]
                dp = _mm_nt(do_st, v2_s[g, band, :])
                ds = p * (dp - delta)
                sink_t = ps_s[b * 2 + g] * delta
                for k in range(4):
                    acc_sink[4 * g + k:4 * g + k + 1, :] += -jnp.sum(sink_t[k * BLOCK:(k + 1) * BLOCK], axis=0, keepdims=True)
                ds_a = ds.astype(act)
                dq2 = _mm(ds_a, k2_s[g, band, :]) * SCALE
                for pr in range(2):
                    cols = slice((2 * g + pr) * BLOCK, (2 * g + pr + 1) * BLOCK)
                    dq = jnp.where(lo, dq2[(2 * pr) * BLOCK:(2 * pr + 1) * BLOCK], dq2[(2 * pr + 1) * BLOCK:(2 * pr + 2) * BLOCK])
                    dproj_ref[rows, cols] = dq.astype(act)
                    acc_bin[:, cols] += _rows8(dq)
                for acc, lhs, rhs in ((dkf_s, stacked_q(b, g), ds_a), (dvf_s, do_st, p.astype(act))):
                    d2t = _mm(lhs.T, rhs)
                    acc[g * HEAD_DIM:(g + 1) * HEAD_DIM, band] += d2t[0:HEAD_DIM, :] + d2t[HEAD_DIM:, :]
        for acc, carry, off in ((dkf_s, carryk_s, OFF_K), (dvf_s, carryv_s, OFF_V)):
            acc[:, t:t + BLOCK] += carry[...]
            carry[...] = acc[:, 0:BLOCK]
            d = acc[:, BLOCK:].T
            dproj_ref[:, off:off + BLOCK] = d.astype(act)
            acc_bin[:, off:off + BLOCK] += _rows8(d)

        for c in range(nb):
            rows = slice(c * BLOCK, (c + 1) * BLOCK)
            dso = dmix_s[rows, ATTN_W:]
            u = u_s[rows, :]
            mix = mix_s[rows, :]
            zs = gate_s[3, rows, :]
            sg = _sigmoid(zs)
            sgs = zs * sg
            du = dso * mix * sgs
            dmx = dso * u * sgs
            dzs = dso * u * mix * (sg * (1.0 + zs * (1.0 - sg)))
            us = gate_s[1, rows, :]
            dus = du * (cdfu_s[rows, :] + us * _norm_pdf(us))
            vln = (vhat_s[rows, :] * lng + lnb).astype(act)
            for p in range(4):
                vlnp_s[p, :, c * BLOCK:(c + 1) * BLOCK] = vln[:, p * BLOCK:(p + 1) * BLOCK]
            split_pairs(dmx, c)
            acc_dbs[...] += dmx
            for off, val in ((OFF_US, dus), (OFF_ZS, dzs)):
                dproj_ref[rows, off:off + 512] = val.astype(act)
                acc_bin[:, off:off + 512] += _rows8(val)
        for p in range(4):
            dvln = _mm(wb_v[p], rhs_s[p])
            for c in range(nb):
                dvln_s[c * BLOCK:(c + 1) * BLOCK, p * BLOCK:(p + 1) * BLOCK] = dvln[:, c * BLOCK:(c + 1) * BLOCK]
            acc_dws[(2 * p) * BLOCK:(2 * p + 1) * BLOCK, :] += _mm_nt(rhs_s[p, 0:BLOCK, :], vlnp_s[p])
            acc_dws[(2 * p + 1) * BLOCK:(2 * p + 2) * BLOCK, :] += _mm_nt(rhs_s[p, BLOCK:, :], vlnp_s[p])
        for c in range(nb):
            rows = slice(c * BLOCK, (c + 1) * BLOCK)
            dvln = dvln_s[rows, :]
            vhat = vhat_s[rows, :]
            acc_lng[...] += _rows8(dvln * vhat)
            acc_lnb[...] += _rows8(dvln)
            dvhat = dvln * lng
            dvg = r2_s[rows, :] * (dvhat - _mean_rows(dvhat) - vhat * _mean_rows(dvhat * vhat))
            vs = gate_s[2, rows, :]
            dvs = dvg * (cdfv_s[rows, :] + vs * _norm_pdf(vs))
            dproj_ref[rows, OFF_VS:OFF_VS + 512] = dvs.astype(act)
            acc_bin[:, OFF_VS:OFF_VS + 512] += _rows8(dvs)

        dh = _mm(dproj_ref[...], wint_v[...])
        dprojt_ref[...] = dproj_ref[...].T
        mixedt_ref[...] = mixed_ref[...].T
        for c in range(nb):
            rows = slice(c * BLOCK, (c + 1) * BLOCK)
            r1 = r1_s[rows, :]
            xn = x_ref[rows, :] * r1
            dhc = dh[rows, :]
            acc_g1[...] += _rows8(dhc * xn)
            dxn = dhc * g1
            dx_ref[rows, :] = doutf_s[rows, :] + r1 * (dxn - xn * _mean_rows(dxn * xn))

        @pl.when(i == nt - 1)
        def _():
            tril = lax.broadcasted_iota(jnp.int32, (BLOCK, BLOCK), 0) >= lane128
            for hh in range(N_SGU_HEADS):
                rws = slice(hh * BLOCK, (hh + 1) * BLOCK)
                sw_ref[rws, :] = jnp.where(tril, acc_dws[rws, :], 0.0)
            vec_ref[...] = jnp.zeros((VEC_ROWS, 128), F32)

            def put(row0, acc, scale=1.0):
                s = jnp.sum(acc[...], axis=0, keepdims=True) * scale
                for k in range(acc.shape[1] // 128):
                    vec_ref[row0 + k:row0 + k + 1, :] = s[:, k * 128:(k + 1) * 128]

            put(R_G1, acc_g1)
            put(R_BIN, acc_bin)
            put(R_LNG, acc_lng)
            put(R_LNB, acc_lnb)
            put(R_BOUT, acc_bout)
            put(R_G3, acc_g3, 1.0 / D_MODEL)
            vec_ref[R_SINK:R_SINK + 1, :] = jnp.sum(
                jnp.where(lax.broadcasted_iota(jnp.int32, (8, 128), 0) == lax.broadcasted_iota(jnp.int32, (8, 128), 1),
                          acc_sink[...], 0.0), axis=0, keepdims=True)
            vec_ref[R_LOSS:R_LOSS + 1, :] = jnp.zeros((1, 128), F32) + jnp.sum(acc_loss[...])
            dbs_t = acc_dbs[...].T
            vec_ref[R_SGUB:R_SGUB + 8, :] = jnp.sum(dbs_t.reshape(N_SGU_HEADS, SGU_W // N_SGU_HEADS, BLOCK), axis=1)

    full = lambda shape: pl.BlockSpec(shape, lambda i: (0,) * len(shape))
    tok = lambda w: pl.BlockSpec((t, w), lambda i: (nt - 1 - i, 0))
    in_specs = [
        pl.BlockSpec(memory_space=pltpu.SMEM),
        tok(D_MODEL), tok(D_MODEL),
        pl.BlockSpec((BLOCK, D_MODEL), lambda i: (jnp.maximum((nt - 1 - i) * nb - 1, 0), 0)),
        tok(1), tok(D_MODEL),
        full((1, D_MODEL)), full((1, IN_W)), full((1, SGU_W)), full((1, SGU_W)),
        full((N_SGU_HEADS, BLOCK, BLOCK)), full((N_SGU_HEADS, BLOCK)), full((1, D_MODEL)), full((1, D_MODEL)),
        pl.BlockSpec(memory_space=pl.ANY), pl.BlockSpec(memory_space=pl.ANY),
    ]
    out_shape = [
        jax.ShapeDtypeStruct((seq, D_MODEL), F32),
        jax.ShapeDtypeStruct((IN_W, seq), act),
        jax.ShapeDtypeStruct((D_MODEL, seq), act),
        jax.ShapeDtypeStruct((seq, D_MODEL), act),
        jax.ShapeDtypeStruct((SW_ROWS, 128), F32),
        jax.ShapeDtypeStruct((VEC_ROWS, 128), F32),
    ]
    tok_t = lambda w: pl.BlockSpec((w, t), lambda i: (0, nt - 1 - i))
    out_specs = [tok(D_MODEL), tok_t(IN_W), tok_t(D_MODEL), tok(D_MODEL), full((SW_ROWS, 128)), full((VEC_ROWS, 128))]
    vm = pltpu.VMEM
    scratch = [
        vm((t, IN_W), act), vm((t, D_MODEL), act),
        vm((IN_W, D_MODEL), act), vm((D_MODEL, D_MODEL), act),
        vm((4, BLOCK, 2 * BLOCK), act), vm((4, BLOCK, 2 * BLOCK), act),
        vm((t, ATTN_W), act),
        vm((t + BLOCK, BLOCK), F32), vm((t + BLOCK, BLOCK), F32),
        vm((2, t + BLOCK, BLOCK), act), vm((2, t + BLOCK, BLOCK), act),
        vm((4, t, 512), F32),
        vm((2 * nb, 4 * BLOCK, 2 * BLOCK), F32), vm((2 * nb, 4 * BLOCK, 1), F32),
        vm((t, ATTN_W), F32), vm((t, SGU_W), F32), vm((t, SGU_W), F32), vm((t, SGU_W), F32), vm((t, 1), F32),
        vm((t, SGU_W), F32), vm((t, SGU_W), F32),
        vm((t, D_MODEL), F32), vm((t, D_MODEL), F32),
        vm((BLOCK, t + BLOCK), F32), vm((BLOCK, t + BLOCK), F32), vm((BLOCK, BLOCK), F32), vm((BLOCK, BLOCK), F32),
        vm((8, IN_W), F32), vm((8, D_MODEL), F32), vm((8, D_MODEL), F32), vm((8, D_MODEL), F32),
        vm((8, SGU_W), F32), vm((8, SGU_W), F32), vm((N_SGU_HEADS * BLOCK, BLOCK), F32), vm((BLOCK, SGU_W), F32),
        vm((8, 128), F32), vm((8, 128), F32),
        vm((t + BLOCK, D_MODEL), act), vm((4, 2 * BLOCK, t), act), vm((4, BLOCK, t), act), vm((t, SGU_W), F32),
        vm((BLOCK, SGU_W), F32),
        pltpu.SemaphoreType.DMA((2,)),
    ]
    return pl.pallas_call(
        body, name="fused", grid=(nt,), in_specs=in_specs, out_specs=out_specs, out_shape=out_shape,
        scratch_shapes=scratch,
        compiler_params=pltpu.CompilerParams(dimension_semantics=("arbitrary",), vmem_limit_bytes=VMEM_LIMIT),
    )(sinks, x, h, h, r1, tgt, g1, b_in, ln_g, ln_b, sgu_w, sgu_b, b_out, g3, win_t, wout)


def _gather_call(a32, b32, x, g1):
    act = MXU_DTYPE
    seq = x.shape[0]
    xc = min(256, seq)
    n_xc = seq // xc

    def body(a32_hbm, b32_hbm, x_hbm, g1_ref, ga_hbm, gb_hbm, h_hbm, r1_hbm, a32_v, b32_v, ga_v, gb_v, x_buf, h_buf, r1_v,
             send_sems, recv_sems, loc_sems, norm_sems):
        x, y, c = lax.axis_index("x"), lax.axis_index("y"), lax.axis_index("c")
        me, sibling, x_nbr, y_nbr = (x, y, c), (x, y, 1 - c), (1 - x, y, c), (x, 1 - y, c)
        j, j_x, j_y, j_d = 2 * x + y, 2 * (1 - x) + y, 2 * x + (1 - y), 2 * (1 - x) + (1 - y)
        arrays = ((0, ga_v, ga_hbm, W_IN_SHARD), (1, gb_v, gb_hbm, W_OUT_SHARD))

        loads = [pltpu.make_async_copy(a32_hbm, a32_v, loc_sems.at[0]), pltpu.make_async_copy(b32_hbm, b32_v, loc_sems.at[1])]
        for cp in loads:
            cp.start()
        for cp in loads:
            cp.wait()
        ga_v[pl.ds(pl.multiple_of(j * W_IN_SHARD, 16), W_IN_SHARD), :] = a32_v[...].astype(act)
        gb_v[pl.ds(pl.multiple_of(j * W_OUT_SHARD, 16), W_OUT_SHARD), :] = b32_v[...].astype(act)

        def rows(shard, chip, hf=None, q=None):
            if hf is None:
                return pl.ds(pl.multiple_of(chip * shard, 16), shard)
            return pl.ds(pl.multiple_of(chip * shard + hf * (shard // 2) + q * (shard // 4), 16), shard // 4)

        def copy(k, ref, at, to):
            return pltpu.make_async_remote_copy(src_ref=ref.at[at, :], dst_ref=ref.at[at, :], send_sem=send_sems.at[k],
                                                recv_sem=recv_sems.at[k], device_id=to, device_id_type=MESH)

        stores = []

        def store(vmem, hbm, at):
            stores.append(pltpu.make_async_copy(vmem.at[at, :], hbm.at[at, :], loc_sems.at[len(stores)]))
            stores[-1].start()

        sent = []
        for w, vmem, hbm, shard in arrays:
            for k, (q, to) in enumerate(((0, x_nbr), (1, x_nbr), (1, y_nbr), (0, y_nbr))):
                sent.append(copy(12 * w + k, vmem, rows(shard, j, c, q), to))
                sent[-1].start()
        for w, vmem, hbm, shard in arrays:
            store(vmem, hbm, rows(shard, j))

        def x_load(n):
            return pltpu.make_async_copy(x_hbm.at[pl.ds(n * xc, xc), :], x_buf.at[n % 2], norm_sems.at[n % 2])

        def h_store(n):
            return pltpu.make_async_copy(h_buf.at[n % 2], h_hbm.at[pl.ds(n * xc, xc), :], norm_sems.at[2 + n % 2])

        g1 = g1_ref[...]
        x_load(0).start()
        for n in range(n_xc):
            if n + 1 < n_xc:
                x_load(n + 1).start()
            x_load(n).wait()
            if n >= 2:
                h_store(n - 2).wait()
            xv = x_buf[n % 2]
            r = lax.rsqrt(_mean_rows(xv * xv) + NORM_EPS)
            r1_v[n * xc:(n + 1) * xc, :] = r
            h_buf[n % 2] = (xv * r * g1).astype(act)
            h_store(n).start()
        for n in range(max(n_xc - 2, 0), n_xc):
            h_store(n).wait()
        r1_out = pltpu.make_async_copy(r1_v, r1_hbm, norm_sems.at[4])
        r1_out.start()

        def landed(w, vmem, hbm, shard, k, chip, q, onward=None):
            at = rows(shard, chip, c, q)
            copy(12 * w + k, vmem, at, me).wait_recv()
            if onward is not None:
                sent.append(copy(12 * w + onward[0], vmem, at, onward[1]))
                sent[-1].start()
            sent.append(copy(12 * w + 6 + k, vmem, at, sibling))
            sent[-1].start()
            store(vmem, hbm, at)

        for arr in arrays:
            landed(*arr, 0, j_x, 0, onward=(4, y_nbr))
            landed(*arr, 2, j_y, 1, onward=(5, x_nbr))
        for arr in arrays:
            landed(*arr, 1, j_x, 1)
            landed(*arr, 3, j_y, 0)
            landed(*arr, 4, j_d, 0)
            landed(*arr, 5, j_d, 1)
        for w, vmem, hbm, shard in arrays:
            for k, (chip, q) in enumerate(((j_x, 0), (j_x, 1), (j_y, 1), (j_y, 0), (j_d, 0), (j_d, 1))):
                at = rows(shard, chip, 1 - c, q)
                copy(12 * w + 6 + k, vmem, at, me).wait_recv()
                store(vmem, hbm, at)
        for cp in sent:
            cp.wait_send()
        for cp in stores:
            cp.wait()
        r1_out.wait()

    hbm = pl.BlockSpec(memory_space=pl.ANY)
    vm = pltpu.VMEM
    return pl.pallas_call(
        body, name="gather", in_specs=[hbm, hbm, hbm, pl.BlockSpec(memory_space=vm)], out_specs=[hbm, hbm, hbm, hbm],
        out_shape=[jax.ShapeDtypeStruct((IN_W, D_MODEL), act), jax.ShapeDtypeStruct((D_MODEL, D_MODEL), act),
                   jax.ShapeDtypeStruct((seq, D_MODEL), act), jax.ShapeDtypeStruct((seq, 1), F32)],
        scratch_shapes=[vm(a32.shape, F32), vm(b32.shape, F32), vm((IN_W, D_MODEL), act), vm((D_MODEL, D_MODEL), act),
                        vm((2, xc, D_MODEL), F32), vm((2, xc, D_MODEL), act), vm((seq, 1), F32),
                        pltpu.SemaphoreType.DMA((24,)), pltpu.SemaphoreType.DMA((24,)), pltpu.SemaphoreType.DMA((26,)),
                        pltpu.SemaphoreType.DMA((5,))],
        compiler_params=pltpu.CompilerParams(vmem_limit_bytes=VMEM_LIMIT),
    )(a32, b32, x, g1)


def _wgrad_reduce_call(dproj_t, h, mixed_t, dout, sw, vec):
    wire = jnp.bfloat16
    half_sw = SW_ROWS // 2
    seq = h.shape[0]
    tk = min(2048, seq)
    nk = seq // tk
    n_steps = 2 * N_CHIPS
    rel_of = lambda s: s % 3 if s < 6 else 3
    half_of = lambda s: s // 3 if s < 6 else s - 6
    x, y = lax.axis_index("x"), lax.axis_index("y")
    chip_of = [2 * (1 - x) + (1 - y), 2 * (1 - x) + y, 2 * x + (1 - y), 2 * x + y]
    order = jnp.stack([2 * chip_of[rel_of(s)] + half_of(s) for s in range(n_steps)]).astype(jnp.int32)

    def body(order_ref, dpt_ref, h_hbm, mxt_ref, dout_hbm, sw_ref, small_ref, oa_ref, ob_ref, osw_ref, osmall_ref,
             h_v, dout_v, acc_a, acc_b, sib_a, sib_b, snd_a, snd_b, in_a, in_b, own_a, own_b, fin_a, fin_b,
             all_small, sw_sib, sw_chips, sw_fin, send_sems, recv_sems, loc_sems):
        x, y, c = lax.axis_index("x"), lax.axis_index("y"), lax.axis_index("c")
        me, sibling = (x, y, c), (x, y, 1 - c)
        steps = [(1 - x, 1 - y), (1 - x, y), (x, 1 - y)]
        j = 2 * x + y
        dev = 4 * x + 2 * y + c
        b, k = pl.program_id(0), pl.program_id(1)

        def copy(n, src, dst, to):
            return pltpu.make_async_remote_copy(src_ref=src, dst_ref=dst, send_sem=send_sems.at[n], recv_sem=recv_sems.at[n],
                                                device_id=to, device_id_type=MESH)

        def sw_rows(ref, hf):
            return ref.at[pl.ds(pl.multiple_of(hf * half_sw, 8), half_sw), :]

        sm_first = [copy(16, small_ref, all_small.at[dev], sibling)]
        sm_first += [copy(17 + r, small_ref, all_small.at[dev], (*chip, c)) for r, chip in enumerate(steps)]
        sw_to_sib = copy(23, sw_rows(sw_ref, 1 - c), sw_sib, sibling)

        @pl.when((b == 0) & (k == 0))
        def _():
            all_small[dev] = small_ref[...]
            for cp in sm_first + [sw_to_sib]:
                cp.start()

        def load(kk):
            rows = pl.ds(kk * tk, tk)
            return [pltpu.make_async_copy(h_hbm.at[rows, :], h_v.at[rows, :], loc_sems.at[kk]),
                    pltpu.make_async_copy(dout_hbm.at[rows, :], dout_v.at[rows, :], loc_sems.at[nk + kk])]

        for kk in range(nk):
            @pl.when((b == 0) & (k == 0))
            def _():
                for cp in load(kk):
                    cp.start()

        for kk in range(nk):
            @pl.when((b == 0) & (k == kk))
            def _():
                for cp in load(kk):
                    cp.wait()

        tok = pl.ds(pl.multiple_of(k * tk, tk), tk)
        pa = _mm(dpt_ref[...], h_v[tok, :])
        pb = _mm(mxt_ref[...], dout_v[tok, :])
        slot = b % 2

        @pl.when(k == 0)
        def _():
            acc_a[slot] = pa
            acc_b[slot] = pb

        @pl.when(k != 0)
        def _():
            acc_a[slot] += pa
            acc_b[slot] += pb

        def to_sibling(s):
            r = rel_of(s)
            return [copy(r, acc_a.at[s % 2], sib_a.at[r], sibling), copy(4 + r, acc_b.at[s % 2], sib_b.at[r], sibling)]

        def chip_partial(s):
            r = rel_of(s)
            copy(r, sib_a.at[r], sib_a.at[r], me).wait_recv()
            copy(4 + r, sib_b.at[r], sib_b.at[r], me).wait_recv()
            return acc_a[s % 2] + sib_a[r], acc_b[s % 2] + sib_b[r]

        def to_owner(r):
            return [copy(8 + r, snd_a.at[r], in_a.at[r], (*steps[r], c)), copy(11 + r, snd_b.at[r], in_b.at[r], (*steps[r], c))]

        def vec_forwards():
            return [copy(20 + r, all_small.at[4 * cx + 2 * cy + c], all_small.at[4 * cx + 2 * cy + c], sibling)
                    for r, (cx, cy) in enumerate(steps)]

        def sw_to_chips():
            return [copy(24 + r, sw_chips.at[j], sw_chips.at[j], (*chip, c)) for r, chip in enumerate(steps)]

        def sw_to_sibling():
            return copy(27, sw_rows(osw_ref, c), sw_fin, sibling)

        @pl.when((b == 1) & (k == 0))
        def _():
            for r, (cx, cy) in enumerate(steps):
                d = 4 * cx + 2 * cy + c
                copy(17 + r, all_small.at[d], all_small.at[d], me).wait_recv()
            for cp in vec_forwards():
                cp.start()
            copy(23, sw_sib, sw_sib, me).wait_recv()
            sw_chips[j] = (sw_rows(sw_ref, c)[...] + sw_sib[...]).astype(wire)
            for cp in sw_to_chips():
                cp.start()

        @pl.when((b == 3) & (k == 0))
        def _():
            for r, (cx, cy) in enumerate(steps):
                cj = 2 * cx + cy
                copy(24 + r, sw_chips.at[cj], sw_chips.at[cj], me).wait_recv()
            tot_sw = sw_chips[0].astype(F32)
            for q in range(1, N_CHIPS):
                tot_sw = tot_sw + sw_chips[q].astype(F32)
            sw_rows(osw_ref, c)[...] = tot_sw
            sw_to_sibling().start()

        late = min(1, nk - 1)
        for s in range(n_steps):
            if s >= 1:
                sp = s - 1

                @pl.when((b == s) & (k == late) & (c == half_of(sp)))
                def _():
                    ta, tb = chip_partial(sp)
                    r = rel_of(sp)
                    if r < 3:
                        snd_a[r] = ta.astype(wire)
                        snd_b[r] = tb.astype(wire)
                        for cp in to_owner(r):
                            cp.start()
                    else:
                        own_a[...] = ta
                        own_b[...] = tb

                @pl.when((b == s) & (k == nk - 1) & (c != half_of(sp)))
                def _():
                    for cp in to_sibling(sp):
                        cp.wait_send()

            @pl.when((b == s) & (k == nk - 1) & (c != half_of(s)))
            def _():
                for cp in to_sibling(s):
                    cp.start()

        @pl.when((b == n_steps - 1) & (k == nk - 1))
        def _():
            last = n_steps - 1

            @pl.when(c == half_of(last))
            def _():
                own_a[...], own_b[...] = chip_partial(last)

            @pl.when(c != half_of(last))
            def _():
                for cp in to_sibling(last):
                    cp.wait_send()

            tot_a = own_a[...]
            tot_b = own_b[...]
            for s in range(3):
                copy(8 + s, in_a.at[s], in_a.at[s], me).wait_recv()
                copy(11 + s, in_b.at[s], in_b.at[s], me).wait_recv()
                tot_a = tot_a + in_a[s].astype(F32)
                tot_b = tot_b + in_b[s].astype(F32)
            mine_a = oa_ref.at[pl.ds(pl.multiple_of(c * HALF_A, 8), HALF_A), :]
            mine_b = ob_ref.at[pl.ds(pl.multiple_of(c * HALF_B, 8), HALF_B), :]
            mine_a[...] = tot_a
            mine_b[...] = tot_b
            back = [copy(14, mine_a, fin_a, sibling), copy(15, mine_b, fin_b, sibling)]
            for cp in back:
                cp.start()

            copy(16, small_ref, all_small.at[dev ^ 1], me).wait_recv()
            for r, (cx, cy) in enumerate(steps):
                d = 4 * cx + 2 * cy + (1 - c)
                copy(20 + r, all_small.at[d], all_small.at[d], me).wait_recv()
            tot = all_small[0]
            for d in range(1, N_DEV):
                tot = tot + all_small[d]
            osmall_ref[...] = tot

            copy(14, fin_a, fin_a, me).wait_recv()
            copy(15, fin_b, fin_b, me).wait_recv()
            copy(27, sw_fin, sw_fin, me).wait_recv()
            oa_ref[pl.ds(pl.multiple_of((1 - c) * HALF_A, 8), HALF_A), :] = fin_a[...]
            ob_ref[pl.ds(pl.multiple_of((1 - c) * HALF_B, 8), HALF_B), :] = fin_b[...]
            sw_rows(osw_ref, 1 - c)[...] = sw_fin[...]
            sends = sm_first + vec_forwards() + sw_to_chips() + back + [sw_to_sib, sw_to_sibling()]
            for s in range(3):
                sends += to_owner(s)
            for cp in sends:
                cp.wait_send()

    vmem = pl.BlockSpec(memory_space=pltpu.VMEM)
    vm = pltpu.VMEM
    grid_spec = pltpu.PrefetchScalarGridSpec(
        num_scalar_prefetch=1, grid=(n_steps, nk),
        in_specs=[pl.BlockSpec((HALF_A, tk), lambda b, k, o: (o[b], k)), pl.BlockSpec(memory_space=pl.ANY),
                  pl.BlockSpec((HALF_B, tk), lambda b, k, o: (o[b], k)), pl.BlockSpec(memory_space=pl.ANY),
                  vmem, vmem],
        out_specs=[vmem, vmem, vmem, vmem],
        scratch_shapes=[vm((seq, D_MODEL), h.dtype), vm((seq, D_MODEL), dout.dtype),
                        vm((2, HALF_A, D_MODEL), F32), vm((2, HALF_B, D_MODEL), F32),
                        vm((N_CHIPS, HALF_A, D_MODEL), F32), vm((N_CHIPS, HALF_B, D_MODEL), F32),
                        vm((3, HALF_A, D_MODEL), wire), vm((3, HALF_B, D_MODEL), wire),
                        vm((3, HALF_A, D_MODEL), wire), vm((3, HALF_B, D_MODEL), wire),
                        vm((HALF_A, D_MODEL), F32), vm((HALF_B, D_MODEL), F32),
                        vm((HALF_A, D_MODEL), F32), vm((HALF_B, D_MODEL), F32),
                        vm((N_DEV, VEC_ROWS, 128), F32), vm((half_sw, 128), F32), vm((N_CHIPS, half_sw, 128), wire),
                        vm((half_sw, 128), F32),
                        pltpu.SemaphoreType.DMA((28,)), pltpu.SemaphoreType.DMA((28,)), pltpu.SemaphoreType.DMA((2 * nk,))])
    return pl.pallas_call(
        body, name="wgrad_reduce", grid_spec=grid_spec,
        out_shape=[jax.ShapeDtypeStruct((W_IN_SHARD, D_MODEL), F32), jax.ShapeDtypeStruct((W_OUT_SHARD, D_MODEL), F32),
                   jax.ShapeDtypeStruct((SW_ROWS, 128), F32), jax.ShapeDtypeStruct((VEC_ROWS, 128), F32)],
        compiler_params=pltpu.CompilerParams(dimension_semantics=("arbitrary", "arbitrary"), vmem_limit_bytes=VMEM_LIMIT),
    )(order, dproj_t, h, mixed_t, dout, sw, vec)


def _adamw(w, g, m, v):
    nm = ADAM_B1 * m + (1.0 - ADAM_B1) * g
    nv = ADAM_B2 * v + (1.0 - ADAM_B2) * (g * g)
    m_hat = nm / (1.0 - ADAM_B1 ** ADAM_STEP)
    v_hat = nv / (1.0 - ADAM_B2 ** ADAM_STEP)
    return -ADAM_LR * (m_hat / (jnp.sqrt(v_hat) + ADAM_EPS) + ADAM_WD * w), nm, nv


def _adamw_shards_call(a, b, chunks=8):
    n_rows = [a[0].shape[0], b[0].shape[0]]

    def body(*refs):
        ins, outs, bufs, res, in_sems, out_sems = refs[0:8], refs[8:16], refs[16:24], refs[24:30], refs[30], refs[31]

        def rows(k, ch):
            return pl.ds(ch * (n_rows[k] // chunks), n_rows[k] // chunks)

        def load(j, ch):
            r = rows(j // 4, ch)
            return pltpu.make_async_copy(ins[j].at[r, :], bufs[j].at[r, :], in_sems.at[j * chunks + ch])

        for ch in range(chunks):
            for j in range(8):
                load(j, ch).start()
        stores = []
        for ch in range(chunks):
            for j in range(8):
                load(j, ch).wait()
            for k in range(2):
                r = rows(k, ch)
                w, g, m, v = (bufs[4 * k + q][r, :] for q in range(4))
                res[3 * k][r, :], res[3 * k + 1][r, :], res[3 * k + 2][r, :] = _adamw(w, g, m, v)
                for q, src in enumerate([bufs[4 * k + 1], res[3 * k], res[3 * k + 1], res[3 * k + 2]]):
                    stores.append(pltpu.make_async_copy(src.at[r, :], outs[4 * k + q].at[r, :],
                                                        out_sems.at[(4 * k + q) * chunks + ch]))
                    stores[-1].start()
        for cp in stores:
            cp.wait()

    hbm = pl.BlockSpec(memory_space=pl.ANY)
    shapes = [jax.ShapeDtypeStruct(w.shape, F32) for w in (a[0], b[0]) for _ in range(4)]
    scratch = [pltpu.VMEM(s.shape, F32) for s in shapes] + [pltpu.VMEM(w.shape, F32) for w in (a[0], b[0]) for _ in range(3)]
    scratch += [pltpu.SemaphoreType.DMA((8 * chunks,)), pltpu.SemaphoreType.DMA((8 * chunks,))]
    return pl.pallas_call(
        body, name="adamw_shards", in_specs=[hbm] * 8, out_specs=[hbm] * 8, out_shape=shapes, scratch_shapes=scratch,
        compiler_params=pltpu.CompilerParams(vmem_limit_bytes=VMEM_LIMIT),
    )(*a, *b)


_SMALL = (("norm_g", (1, D_MODEL), R_G1), ("b_in", (1, IN_W), R_BIN), ("attn_sinks", (1, 8), R_SINK),
          ("sgu_ln_g", (1, SGU_W), R_LNG), ("sgu_ln_b", (1, SGU_W), R_LNB), ("sgu_b", (N_SGU_HEADS, BLOCK), R_SGUB),
          ("b_out", (1, D_MODEL), R_BOUT), ("final_norm_g", (1, D_MODEL), R_G3))


def _adamw_small_call(sw_g, vec_g, sgu_w3, ws, ms, vs):
    n = len(_SMALL)

    def body(*refs):
        sw_ref, vec_ref = refs[0], refs[1]
        w3 = refs[2:5]
        w_refs, m_refs, v_refs = refs[5:5 + n], refs[5 + n:5 + 2 * n], refs[5 + 2 * n:5 + 3 * n]
        outs = refs[5 + 3 * n:]
        outs[0][...] = vec_ref[R_LOSS:R_LOSS + 1, 0:1]
        g = sw_ref[...]
        outs[1][...] = g
        outs[2][...], outs[3][...], outs[4][...] = _adamw(w3[0][...], g, w3[1][...], w3[2][...])
        for k, (_, shape, row) in enumerate(_SMALL):
            if shape[0] == 1 and shape[1] >= 128:
                g = jnp.concatenate([vec_ref[row + q:row + q + 1, :] for q in range(shape[1] // 128)], axis=1)
            else:
                g = vec_ref[row:row + shape[0], 0:shape[1]]
            o = outs[5 + 4 * k:9 + 4 * k]
            o[0][...] = g
            o[1][...], o[2][...], o[3][...] = _adamw(w_refs[k][...], g, m_refs[k][...], v_refs[k][...])

    vmem = pl.BlockSpec(memory_space=pltpu.VMEM)
    out_shape = [jax.ShapeDtypeStruct((1, 1), F32)] + [jax.ShapeDtypeStruct((SW_ROWS, 128), F32)] * 4
    for _, shape, _ in _SMALL:
        out_shape += [jax.ShapeDtypeStruct(shape, F32)] * 4
    args = [sw_g, vec_g, *sgu_w3, *ws, *ms, *vs]
    return pl.pallas_call(
        body, name="adamw_small", in_specs=[vmem] * len(args), out_specs=[vmem] * len(out_shape), out_shape=out_shape,
    )(*args)


def kernel(x, norm_g, w_in, b_in, attn_sinks, sgu_ln_g, sgu_ln_b, sgu_w, sgu_b, w_out, b_out, final_norm_g, loss_target, m_norm_g, m_w_in, m_b_in, m_attn_sinks, m_sgu_ln_g, m_sgu_ln_b, m_sgu_w, m_sgu_b, m_w_out, m_b_out, m_final_norm_g, v_norm_g, v_w_in, v_b_in, v_attn_sinks, v_sgu_ln_g, v_sgu_ln_b, v_sgu_w, v_sgu_b, v_w_out, v_b_out, v_final_norm_g):
    seq = x.shape[1]
    win_t, wout, h, r1 = _gather_call(w_in[0].T, w_out[0], x[0], norm_g)
    dx, dproj_t, mixed_t, dout, sw, vec = _fused_call(
        x[0], h, r1, loss_target[0], attn_sinks[0], norm_g, b_in, sgu_ln_g, sgu_ln_b, sgu_w[0], sgu_b[0], b_out,
        final_norm_g.reshape(1, D_MODEL), win_t, wout)
    ga_t, g_w_out, sw, vec = _wgrad_reduce_call(dproj_t, h, mixed_t, dout, sw, vec)

    names = ["norm_g", "w_in", "b_in", "attn_sinks", "sgu_ln_g", "sgu_ln_b", "sgu_w", "sgu_b", "w_out", "b_out", "final_norm_g"]
    res = {}
    shards = _adamw_shards_call((w_in[0].T, ga_t, m_w_in[0].T, v_w_in[0].T), (w_out[0], g_w_out, m_w_out[0], v_w_out[0]))
    res["w_in"] = [a.T[None] for a in shards[:4]]
    res["w_out"] = [a[None] for a in shards[4:]]
    given = dict(norm_g=(norm_g, m_norm_g, v_norm_g), b_in=(b_in, m_b_in, v_b_in), attn_sinks=(attn_sinks, m_attn_sinks, v_attn_sinks),
                 sgu_ln_g=(sgu_ln_g, m_sgu_ln_g, v_sgu_ln_g), sgu_ln_b=(sgu_ln_b, m_sgu_ln_b, v_sgu_ln_b),
                 sgu_b=(sgu_b, m_sgu_b, v_sgu_b), b_out=(b_out, m_b_out, v_b_out),
                 final_norm_g=(final_norm_g, m_final_norm_g, v_final_norm_g))
    wmv = [[given[n][k].reshape(shape) for n, shape, _ in _SMALL] for k in range(3)]
    outs = _adamw_small_call(sw, vec, [a.reshape(SW_ROWS, BLOCK) for a in (sgu_w, m_sgu_w, v_sgu_w)], *wmv)
    loss = outs[0].reshape(())
    res["sgu_w"] = [a.reshape(sgu_w.shape) for a in outs[1:5]]
    for k, (n, _, _) in enumerate(_SMALL):
        res[n] = [a.reshape(given[n][0].shape) for a in outs[5 + 4 * k:9 + 4 * k]]

    return (loss, dx[None], *[res[n][0] for n in names], *[res[n][1] for n in names], *[res[n][2] for n in names],
            *[res[n][3] for n in names])
```

```python
import functools
import math

import jax
import jax.numpy as jnp
from jax import lax
from jax.experimental import pallas as pl
from jax.experimental.pallas import tpu as pltpu

F32 = jnp.float32
MXU_DTYPE = jnp.bfloat16

D_MODEL = 1024
HEAD_DIM = 64
ATTN_W = 512
SGU_W = 512
N_SGU_HEADS = 8
BLOCK = 128
IN_W = 2816
OFF_K, OFF_V, OFF_ZA, OFF_US, OFF_VS, OFF_ZS = 512, 640, 768, 1280, 1792, 2304
NORM_EPS = 1e-5
NEG_INF = -1e30
SCALE = HEAD_DIM ** -0.5
SQRT_HALF = math.sqrt(0.5)
INV_SQRT_2PI = 1.0 / math.sqrt(2.0 * math.pi)

N_CHIPS = 4
N_DEV = 8
W_IN_SHARD = IN_W // N_CHIPS
W_OUT_SHARD = D_MODEL // N_CHIPS
HALF_A = W_IN_SHARD // 2
HALF_B = W_OUT_SHARD // 2

TILE = 256
VMEM_LIMIT = 56 * 1024 * 1024

ADAM_LR, ADAM_B1, ADAM_B2, ADAM_EPS, ADAM_WD, ADAM_STEP = 0.001, 0.9, 0.999, 1e-08, 0.01, 10

SW_ROWS = N_SGU_HEADS * BLOCK
R_G1, R_BIN, R_SINK, R_LOSS, R_LNG, R_LNB, R_SGUB, R_BOUT, R_G3 = 0, 8, 32, 40, 48, 56, 64, 72, 80
VEC_ROWS = 88

MESH = pl.DeviceIdType.MESH


def _mm(a, b):
    return jnp.dot(a, b, preferred_element_type=F32)


def _mm_nt(a, b):
    return lax.dot_general(a, b, (((1,), (1,)), ((), ())), preferred_element_type=F32)


def _sigmoid(z):
    return 1.0 / (1.0 + jnp.exp(-z))


def _norm_cdf(z):
    return 0.5 * (1.0 + lax.erf(z * SQRT_HALF))


def _norm_pdf(z):
    return jnp.exp(-0.5 * z * z) * INV_SQRT_2PI


def _rows8(v):
    r, n = v.shape
    return jnp.sum(v.reshape(r // 8, 8, n), axis=0)


def _mean_rows(v):
    return jnp.sum(v, axis=1, keepdims=True) * (1.0 / v.shape[1])


def _fused_call(x, h, r1, tgt, sinks, g1, b_in, ln_g, ln_b, sgu_w, sgu_b, b_out, g3, win_t, wout):
    seq = x.shape[0]
    t = TILE
    nt = seq // t
    nb = t // BLOCK
    act = MXU_DTYPE

    def body(sinks_ref, x_ref, h_ref, hh_ref, r1_ref, tgt_ref, g1_ref, bin_ref, lng_ref, lnb_ref, sguw_ref, sgub_ref, bout_ref, g3_ref,
             wint_hbm, wout_hbm,
             dx_ref, dprojt_ref, mixedt_ref, dout_ref, sw_ref, vec_ref,
             dproj_ref, mixed_ref, wint_v, wout_v, wf_v, wb_v, q_s, kf_s, vf_s, k2_s, v2_s, gate_s, p_s, ps_s, o_s, u_s, mix_s, vhat_s, r2_s,
             cdfu_s, cdfv_s, doutf_s, dmix_s, dkf_s, dvf_s, carryk_s, carryv_s,
             acc_bin, acc_g1, acc_bout, acc_g3, acc_lng, acc_lnb, acc_dws, acc_dbs, acc_sink, acc_loss,
             h_s, rhs_s, vlnp_s, dvln_s, bexp_s, sems):
        i = pl.program_id(0)
        tile = nt - 1 - i
        lane128 = lax.broadcasted_iota(jnp.int32, (BLOCK, BLOCK), 1)
        lo = lane128 < HEAD_DIM

        @pl.when(i == 0)
        def _():
            cp_a = pltpu.make_async_copy(wint_hbm, wint_v, sems.at[0])
            cp_b = pltpu.make_async_copy(wout_hbm, wout_v, sems.at[1])
            cp_a.start()
            cp_b.start()
            for acc in (acc_bin, acc_g1, acc_bout, acc_g3, acc_lng, acc_lnb, acc_dws, acc_dbs, acc_sink, acc_loss,
                        carryk_s, carryv_s):
                acc[...] = jnp.zeros(acc.shape, F32)
            tril = lax.broadcasted_iota(jnp.int32, (BLOCK, BLOCK), 0) >= lane128
            for h in range(N_SGU_HEADS):
                w = jnp.where(tril, sguw_ref[h], 0.0)
                wf_v[h // 2, :, (h % 2) * BLOCK:(h % 2 + 1) * BLOCK] = w.astype(act)
                wb_v[h // 2, :, (h % 2) * BLOCK:(h % 2 + 1) * BLOCK] = w.T.astype(act)
            eye = lax.broadcasted_iota(jnp.int32, (BLOCK, BLOCK), 0) == lane128
            for p in range(4):
                col = [jnp.sum(jnp.where(eye, sgub_ref[2 * p + q:2 * p + q + 1, :], 0.0), axis=1, keepdims=True) for q in range(2)]
                bexp_s[:, p * BLOCK:(p + 1) * BLOCK] = jnp.where(lo, col[0], col[1])
            cp_a.wait()
            cp_b.wait()

        g1 = g1_ref[...]

        def rms(v):
            r = lax.rsqrt(_mean_rows(v * v) + NORM_EPS)
            return r, v * r

        h_s[0:BLOCK, :] = hh_ref[...]
        h_s[BLOCK:, :] = h_ref[...]
        r1_s = r1_ref

        h = h_ref[...]
        q = _mm_nt(h, wint_v[0:OFF_K, :]) + bin_ref[:, 0:OFF_K]
        q_s[...] = (q * SCALE).astype(act)
        kv = _mm_nt(h_s[...], wint_v[OFF_K:OFF_ZA, :]) + bin_ref[:, OFF_K:OFF_ZA]
        kf_s[...] = kv[:, :BLOCK]
        vf_s[...] = kv[:, BLOCK:]
        for r in range(4):
            cols = slice(OFF_ZA + r * 512, OFF_ZA + (r + 1) * 512)
            gate_s[r] = _mm_nt(h, wint_v[cols, :]) + bin_ref[:, cols]

        lo_kv = lax.broadcasted_iota(jnp.int32, (t + BLOCK, BLOCK), 1) < HEAD_DIM
        for src, dst in ((kf_s, k2_s), (vf_s, v2_s)):
            v = src[...]
            vr = pltpu.roll(v, HEAD_DIM, 1)
            dst[0] = jnp.where(lo_kv, v, vr).astype(act)
            dst[1] = jnp.where(lo_kv, vr, v).astype(act)

        rowi = lax.broadcasted_iota(jnp.int32, (BLOCK, 2 * BLOCK), 0)
        colj = lax.broadcasted_iota(jnp.int32, (BLOCK, 2 * BLOCK), 1)
        in_band = (colj > rowi) & (colj <= rowi + BLOCK)
        row512 = lax.broadcasted_iota(jnp.int32, (4 * BLOCK, 1), 0)

        def stacked_q(b, g):
            parts = []
            for p in range(2):
                slab = q_s[b * BLOCK:(b + 1) * BLOCK, (2 * g + p) * BLOCK:(2 * g + p + 1) * BLOCK]
                parts += [jnp.where(lo, slab, jnp.zeros_like(slab)), jnp.where(lo, jnp.zeros_like(slab), slab)]
            return jnp.concatenate(parts, axis=0)

        def sink_col(g):
            s = [sinks_ref[4 * g + k] for k in range(4)]
            return jnp.where(row512 < BLOCK, s[0], jnp.where(row512 < 2 * BLOCK, s[1], jnp.where(row512 < 3 * BLOCK, s[2], s[3])))

        for b in range(nb):
            band = slice(b * BLOCK, (b + 2) * BLOCK)
            first_key = jnp.where(tile * nb + b > 0, 0, BLOCK)
            valid = in_band & (colj >= first_key)
            valid4 = jnp.concatenate([valid] * 4, axis=0)
            for g in range(2):
                s = _mm_nt(stacked_q(b, g), k2_s[g, band, :])
                s = jnp.where(valid4, s, NEG_INF)
                sk = sink_col(g)
                m = jnp.maximum(jnp.max(s, axis=1, keepdims=True), sk)
                p = jnp.exp(s - m)
                psk = jnp.exp(sk - m)
                inv = 1.0 / (jnp.sum(p, axis=1, keepdims=True) + psk)
                p = p * inv
                p_s[b * 2 + g] = p
                ps_s[b * 2 + g] = psk * inv
                o2 = _mm(p.astype(act), v2_s[g, band, :])
                for pr in range(2):
                    o_s[b * BLOCK:(b + 1) * BLOCK, (2 * g + pr) * BLOCK:(2 * g + pr + 1) * BLOCK] = jnp.where(
                        lo, o2[(2 * pr) * BLOCK:(2 * pr + 1) * BLOCK], o2[(2 * pr + 1) * BLOCK:(2 * pr + 2) * BLOCK])

        lng = lng_ref[...]
        lnb = lnb_ref[...]

        def split_pairs(val, c):
            for p in range(4):
                slab = val[:, p * BLOCK:(p + 1) * BLOCK]
                rhs_s[p, 0:BLOCK, c * BLOCK:(c + 1) * BLOCK] = jnp.where(lo, slab, 0.0).astype(act)
                rhs_s[p, BLOCK:, c * BLOCK:(c + 1) * BLOCK] = jnp.where(lo, 0.0, slab).astype(act)

        for c in range(nb):
            rows = slice(c * BLOCK, (c + 1) * BLOCK)
            za = gate_s[0, rows, :]
            mixed_ref[rows, 0:ATTN_W] = (o_s[rows, :] * (za * _sigmoid(za))).astype(act)
            us = gate_s[1, rows, :]
            vs = gate_s[2, rows, :]
            cu = _norm_cdf(us)
            cv = _norm_cdf(vs)
            cdfu_s[rows, :] = cu
            cdfv_s[rows, :] = cv
            u = us * cu
            vg = vs * cv
            vc = vg - _mean_rows(vg)
            r2 = lax.rsqrt(_mean_rows(vc * vc) + NORM_EPS)
            vhat = vc * r2
            r2_s[rows, :] = r2
            vhat_s[rows, :] = vhat
            u_s[rows, :] = u
            split_pairs(vhat * lng + lnb, c)
        for p in range(4):
            cols = slice(p * BLOCK, (p + 1) * BLOCK)
            mix = _mm(wf_v[p], rhs_s[p])
            for c in range(nb):
                mix_s[c * BLOCK:(c + 1) * BLOCK, cols] = mix[:, c * BLOCK:(c + 1) * BLOCK] + bexp_s[:, cols]
        for c in range(nb):
            rows = slice(c * BLOCK, (c + 1) * BLOCK)
            zs = gate_s[3, rows, :]
            mixed_ref[rows, ATTN_W:] = (u_s[rows, :] * mix_s[rows, :] * (zs * _sigmoid(zs))).astype(act)

        g3 = g3_ref[...]
        proj_o = _mm(mixed_ref[...], wout_v[...])
        for c in range(nb):
            rows = slice(c * BLOCK, (c + 1) * BLOCK)
            out = x_ref[rows, :] + proj_o[rows, :] + bout_ref[...]
            r3, on = rms(out)
            e = on * g3 - tgt_ref[rows, :]
            e2 = _rows8(e * e)
            acc_loss[...] += sum(e2[:, k * 128:(k + 1) * 128] for k in range(D_MODEL // 128)) * (0.5 / D_MODEL)
            acc_g3[...] += _rows8(e * on)
            don = e * g3
            dout = (r3 * (1.0 / D_MODEL)) * (don - on * _mean_rows(don * on))
            doutf_s[rows, :] = dout
            dout_ref[rows, :] = dout.astype(act)
            acc_bout[...] += _rows8(dout)
        dmix_s[...] = _mm_nt(dout_ref[...], wout_v[...])

        dkf_s[...] = jnp.zeros(dkf_s.shape, F32)
        dvf_s[...] = jnp.zeros(dvf_s.shape, F32)
        for b in range(nb):
            rows = slice(b * BLOCK, (b + 1) * BLOCK)
            band = slice(b * BLOCK, (b + 2) * BLOCK)
            za = gate_s[0, rows, :]
            sg = _sigmoid(za)
            dao = dmix_s[rows, 0:ATTN_W]
            o = o_s[rows, :]
            do = dao * (za * sg)
            dza = dao * o * (sg * (1.0 + za * (1.0 - sg)))
            dproj_ref[rows, OFF_ZA:OFF_US] = dza.astype(act)
            acc_bin[:, OFF_ZA:OFF_US] += _rows8(dza)
            for g in range(2):
                do_parts, delta_parts = [], []
                for pr in range(2):
                    cols = slice((2 * g + pr) * BLOCK, (2 * g + pr + 1) * BLOCK)
                    d_pair = do[:, cols]
                    prod = d_pair * o[:, cols]
                    do_parts += [jnp.where(lo, d_pair, 0.0).astype(act), jnp.where(lo, 0.0, d_pair).astype(act)]
                    delta_parts += [jnp.sum(jnp.where(lo, prod, 0.0), axis=1, keepdims=True),
                                    jnp.sum(jnp.where(lo, 0.0, prod), axis=1, keepdims=True)]
                do_st = jnp.concatenate(do_parts, axis=0)
                delta = jnp.concatenate(delta_parts, axis=0)
                p = p_s[b * 2 + g]
                dp = _mm_nt(do_st, v2_s[g, band, :])
                ds = p * (dp - delta)
                sink_t = ps_s[b * 2 + g] * delta
                for k in range(4):
                    acc_sink[4 * g + k:4 * g + k + 1, :] += -jnp.sum(sink_t[k * BLOCK:(k + 1) * BLOCK], axis=0, keepdims=True)
                ds_a = ds.astype(act)
                dq2 = _mm(ds_a, k2_s[g, band, :]) * SCALE
                for pr in range(2):
                    cols = slice((2 * g + pr) * BLOCK, (2 * g + pr + 1) * BLOCK)
                    dq = jnp.where(lo, dq2[(2 * pr) * BLOCK:(2 * pr + 1) * BLOCK], dq2[(2 * pr + 1) * BLOCK:(2 * pr + 2) * BLOCK])
                    dproj_ref[rows, cols] = dq.astype(act)
                    acc_bin[:, cols] += _rows8(dq)
                for acc, lhs, rhs in ((dkf_s, stacked_q(b, g), ds_a), (dvf_s, do_st, p.astype(act))):
                    d2t = _mm(lhs.T, rhs)
                    acc[g * HEAD_DIM:(g + 1) * HEAD_DIM, band] += d2t[0:HEAD_DIM, :] + d2t[HEAD_DIM:, :]
        for acc, carry, off in ((dkf_s, carryk_s, OFF_K), (dvf_s, carryv_s, OFF_V)):
            acc[:, t:t + BLOCK] += carry[...]
            carry[...] = acc[:, 0:BLOCK]
            d = acc[:, BLOCK:].T
            dproj_ref[:, off:off + BLOCK] = d.astype(act)
            acc_bin[:, off:off + BLOCK] += _rows8(d)

        for c in range(nb):
            rows = slice(c * BLOCK, (c + 1) * BLOCK)
            dso = dmix_s[rows, ATTN_W:]
            u = u_s[rows, :]
            mix = mix_s[rows, :]
            zs = gate_s[3, rows, :]
            sg = _sigmoid(zs)
            sgs = zs * sg
            du = dso * mix * sgs
            dmx = dso * u * sgs
            dzs = dso * u * mix * (sg * (1.0 + zs * (1.0 - sg)))
            us = gate_s[1, rows, :]
            dus = du * (cdfu_s[rows, :] + us * _norm_pdf(us))
            vln = (vhat_s[rows, :] * lng + lnb).astype(act)
            for p in range(4):
                vlnp_s[p, :, c * BLOCK:(c + 1) * BLOCK] = vln[:, p * BLOCK:(p + 1) * BLOCK]
            split_pairs(dmx, c)
            acc_dbs[...] += dmx
            for off, val in ((OFF_US, dus), (OFF_ZS, dzs)):
                dproj_ref[rows, off:off + 512] = val.astype(act)
                acc_bin[:, off:off + 512] += _rows8(val)
        for p in range(4):
            dvln = _mm(wb_v[p], rhs_s[p])
            for c in range(nb):
                dvln_s[c * BLOCK:(c + 1) * BLOCK, p * BLOCK:(p + 1) * BLOCK] = dvln[:, c * BLOCK:(c + 1) * BLOCK]
            acc_dws[(2 * p) * BLOCK:(2 * p + 1) * BLOCK, :] += _mm_nt(rhs_s[p, 0:BLOCK, :], vlnp_s[p])
            acc_dws[(2 * p + 1) * BLOCK:(2 * p + 2) * BLOCK, :] += _mm_nt(rhs_s[p, BLOCK:, :], vlnp_s[p])
        for c in range(nb):
            rows = slice(c * BLOCK, (c + 1) * BLOCK)
            dvln = dvln_s[rows, :]
            vhat = vhat_s[rows, :]
            acc_lng[...] += _rows8(dvln * vhat)
            acc_lnb[...] += _rows8(dvln)
            dvhat = dvln * lng
            dvg = r2_s[rows, :] * (dvhat - _mean_rows(dvhat) - vhat * _mean_rows(dvhat * vhat))
            vs = gate_s[2, rows, :]
            dvs = dvg * (cdfv_s[rows, :] + vs * _norm_pdf(vs))
            dproj_ref[rows, OFF_VS:OFF_VS + 512] = dvs.astype(act)
            acc_bin[:, OFF_VS:OFF_VS + 512] += _rows8(dvs)

        dh = _mm(dproj_ref[...], wint_v[...])
        dprojt_ref[...] = dproj_ref[...].T
        mixedt_ref[...] = mixed_ref[...].T
        for c in range(nb):
            rows = slice(c * BLOCK, (c + 1) * BLOCK)
            r1 = r1_s[rows, :]
            xn = x_ref[rows, :] * r1
            dhc = dh[rows, :]
            acc_g1[...] += _rows8(dhc * xn)
            dxn = dhc * g1
            dx_ref[rows, :] = doutf_s[rows, :] + r1 * (dxn - xn * _mean_rows(dxn * xn))

        @pl.when(i == nt - 1)
        def _():
            tril = lax.broadcasted_iota(jnp.int32, (BLOCK, BLOCK), 0) >= lane128
            for hh in range(N_SGU_HEADS):
                rws = slice(hh * BLOCK, (hh + 1) * BLOCK)
                sw_ref[rws, :] = jnp.where(tril, acc_dws[rws, :], 0.0)
            vec_ref[...] = jnp.zeros((VEC_ROWS, 128), F32)

            def put(row0, acc, scale=1.0):
                s = jnp.sum(acc[...], axis=0, keepdims=True) * scale
                for k in range(acc.shape[1] // 128):
                    vec_ref[row0 + k:row0 + k + 1, :] = s[:, k * 128:(k + 1) * 128]

            put(R_G1, acc_g1)
            put(R_BIN, acc_bin)
            put(R_LNG, acc_lng)
            put(R_LNB, acc_lnb)
            put(R_BOUT, acc_bout)
            put(R_G3, acc_g3, 1.0 / D_MODEL)
            vec_ref[R_SINK:R_SINK + 1, :] = jnp.sum(
                jnp.where(lax.broadcasted_iota(jnp.int32, (8, 128), 0) == lax.broadcasted_iota(jnp.int32, (8, 128), 1),
                          acc_sink[...], 0.0), axis=0, keepdims=True)
            vec_ref[R_LOSS:R_LOSS + 1, :] = jnp.zeros((1, 128), F32) + jnp.sum(acc_loss[...])
            dbs_t = acc_dbs[...].T
            vec_ref[R_SGUB:R_SGUB + 8, :] = jnp.sum(dbs_t.reshape(N_SGU_HEADS, SGU_W // N_SGU_HEADS, BLOCK), axis=1)

    full = lambda shape: pl.BlockSpec(shape, lambda i: (0,) * len(shape))
    tok = lambda w: pl.BlockSpec((t, w), lambda i: (nt - 1 - i, 0))
    in_specs = [
        pl.BlockSpec(memory_space=pltpu.SMEM),
        tok(D_MODEL), tok(D_MODEL),
        pl.BlockSpec((BLOCK, D_MODEL), lambda i: (jnp.maximum((nt - 1 - i) * nb - 1, 0), 0)),
        tok(1), tok(D_MODEL),
        full((1, D_MODEL)), full((1, IN_W)), full((1, SGU_W)), full((1, SGU_W)),
        full((N_SGU_HEADS, BLOCK, BLOCK)), full((N_SGU_HEADS, BLOCK)), full((1, D_MODEL)), full((1, D_MODEL)),
        pl.BlockSpec(memory_space=pl.ANY), pl.BlockSpec(memory_space=pl.ANY),
    ]
    out_shape = [
        jax.ShapeDtypeStruct((seq, D_MODEL), F32),
        jax.ShapeDtypeStruct((IN_W, seq), act),
        jax.ShapeDtypeStruct((D_MODEL, seq), act),
        jax.ShapeDtypeStruct((seq, D_MODEL), act),
        jax.ShapeDtypeStruct((SW_ROWS, 128), F32),
        jax.ShapeDtypeStruct((VEC_ROWS, 128), F32),
    ]
    tok_t = lambda w: pl.BlockSpec((w, t), lambda i: (0, nt - 1 - i))
    out_specs = [tok(D_MODEL), tok_t(IN_W), tok_t(D_MODEL), tok(D_MODEL), full((SW_ROWS, 128)), full((VEC_ROWS, 128))]
    vm = pltpu.VMEM
    scratch = [
        vm((t, IN_W), act), vm((t, D_MODEL), act),
        vm((IN_W, D_MODEL), act), vm((D_MODEL, D_MODEL), act),
        vm((4, BLOCK, 2 * BLOCK), act), vm((4, BLOCK, 2 * BLOCK), act),
        vm((t, ATTN_W), act),
        vm((t + BLOCK, BLOCK), F32), vm((t + BLOCK, BLOCK), F32),
        vm((2, t + BLOCK, BLOCK), act), vm((2, t + BLOCK, BLOCK), act),
        vm((4, t, 512), F32),
        vm((2 * nb, 4 * BLOCK, 2 * BLOCK), F32), vm((2 * nb, 4 * BLOCK, 1), F32),
        vm((t, ATTN_W), F32), vm((t, SGU_W), F32), vm((t, SGU_W), F32), vm((t, SGU_W), F32), vm((t, 1), F32),
        vm((t, SGU_W), F32), vm((t, SGU_W), F32),
        vm((t, D_MODEL), F32), vm((t, D_MODEL), F32),
        vm((BLOCK, t + BLOCK), F32), vm((BLOCK, t + BLOCK), F32), vm((BLOCK, BLOCK), F32), vm((BLOCK, BLOCK), F32),
        vm((8, IN_W), F32), vm((8, D_MODEL), F32), vm((8, D_MODEL), F32), vm((8, D_MODEL), F32),
        vm((8, SGU_W), F32), vm((8, SGU_W), F32), vm((N_SGU_HEADS * BLOCK, BLOCK), F32), vm((BLOCK, SGU_W), F32),
        vm((8, 128), F32), vm((8, 128), F32),
        vm((t + BLOCK, D_MODEL), act), vm((4, 2 * BLOCK, t), act), vm((4, BLOCK, t), act), vm((t, SGU_W), F32),
        vm((BLOCK, SGU_W), F32),
        pltpu.SemaphoreType.DMA((2,)),
    ]
    return pl.pallas_call(
        body, name="fused", grid=(nt,), in_specs=in_specs, out_specs=out_specs, out_shape=out_shape,
        scratch_shapes=scratch,
        compiler_params=pltpu.CompilerParams(dimension_semantics=("arbitrary",), vmem_limit_bytes=VMEM_LIMIT),
    )(sinks, x, h, h, r1, tgt, g1, b_in, ln_g, ln_b, sgu_w, sgu_b, b_out, g3, win_t, wout)


def _gather_call(a32, b32, x, g1):
    act = MXU_DTYPE
    seq = x.shape[0]
    xc = min(256, seq)
    n_xc = seq // xc

    def body(a32_hbm, b32_hbm, x_hbm, g1_ref, ga_hbm, gb_hbm, h_hbm, r1_hbm, a32_v, b32_v, ga_v, gb_v, x_buf, h_buf, r1_v,
             send_sems, recv_sems, loc_sems, norm_sems):
        x, y, c = lax.axis_index("x"), lax.axis_index("y"), lax.axis_index("c")
        me, sibling, x_nbr, y_nbr = (x, y, c), (x, y, 1 - c), (1 - x, y, c), (x, 1 - y, c)
        j, j_x, j_y, j_d = 2 * x + y, 2 * (1 - x) + y, 2 * x + (1 - y), 2 * (1 - x) + (1 - y)
        arrays = ((0, ga_v, ga_hbm, W_IN_SHARD), (1, gb_v, gb_hbm, W_OUT_SHARD))

        loads = [pltpu.make_async_copy(a32_hbm, a32_v, loc_sems.at[0]), pltpu.make_async_copy(b32_hbm, b32_v, loc_sems.at[1])]
        for cp in loads:
            cp.start()
        for cp in loads:
            cp.wait()
        ga_v[pl.ds(pl.multiple_of(j * W_IN_SHARD, 16), W_IN_SHARD), :] = a32_v[...].astype(act)
        gb_v[pl.ds(pl.multiple_of(j * W_OUT_SHARD, 16), W_OUT_SHARD), :] = b32_v[...].astype(act)

        def rows(shard, chip, hf=None, q=None):
            if hf is None:
                return pl.ds(pl.multiple_of(chip * shard, 16), shard)
            return pl.ds(pl.multiple_of(chip * shard + hf * (shard // 2) + q * (shard // 4), 16), shard // 4)

        def copy(k, ref, at, to):
            return pltpu.make_async_remote_copy(src_ref=ref.at[at, :], dst_ref=ref.at[at, :], send_sem=send_sems.at[k],
                                                recv_sem=recv_sems.at[k], device_id=to, device_id_type=MESH)

        stores = []

        def store(vmem, hbm, at):
            stores.append(pltpu.make_async_copy(vmem.at[at, :], hbm.at[at, :], loc_sems.at[len(stores)]))
            stores[-1].start()

        sent = []
        for w, vmem, hbm, shard in arrays:
            for k, (q, to) in enumerate(((0, x_nbr), (1, x_nbr), (1, y_nbr), (0, y_nbr))):
                sent.append(copy(12 * w + k, vmem, rows(shard, j, c, q), to))
                sent[-1].start()
        for w, vmem, hbm, shard in arrays:
            store(vmem, hbm, rows(shard, j))

        def x_load(n):
            return pltpu.make_async_copy(x_hbm.at[pl.ds(n * xc, xc), :], x_buf.at[n % 2], norm_sems.at[n % 2])

        def h_store(n):
            return pltpu.make_async_copy(h_buf.at[n % 2], h_hbm.at[pl.ds(n * xc, xc), :], norm_sems.at[2 + n % 2])

        g1 = g1_ref[...]
        x_load(0).start()
        for n in range(n_xc):
            if n + 1 < n_xc:
                x_load(n + 1).start()
            x_load(n).wait()
            if n >= 2:
                h_store(n - 2).wait()
            xv = x_buf[n % 2]
            r = lax.rsqrt(_mean_rows(xv * xv) + NORM_EPS)
            r1_v[n * xc:(n + 1) * xc, :] = r
            h_buf[n % 2] = (xv * r * g1).astype(act)
            h_store(n).start()
        for n in range(max(n_xc - 2, 0), n_xc):
            h_store(n).wait()
        r1_out = pltpu.make_async_copy(r1_v, r1_hbm, norm_sems.at[4])
        r1_out.start()

        def landed(w, vmem, hbm, shard, k, chip, q, onward=None):
            at = rows(shard, chip, c, q)
            copy(12 * w + k, vmem, at, me).wait_recv()
            if onward is not None:
                sent.append(copy(12 * w + onward[0], vmem, at, onward[1]))
                sent[-1].start()
            sent.append(copy(12 * w + 6 + k, vmem, at, sibling))
            sent[-1].start()
            store(vmem, hbm, at)

        for arr in arrays:
            landed(*arr, 0, j_x, 0, onward=(4, y_nbr))
            landed(*arr, 2, j_y, 1, onward=(5, x_nbr))
        for arr in arrays:
            landed(*arr, 1, j_x, 1)
            landed(*arr, 3, j_y, 0)
            landed(*arr, 4, j_d, 0)
            landed(*arr, 5, j_d, 1)
        for w, vmem, hbm, shard in arrays:
            for k, (chip, q) in enumerate(((j_x, 0), (j_x, 1), (j_y, 1), (j_y, 0), (j_d, 0), (j_d, 1))):
                at = rows(shard, chip, 1 - c, q)
                copy(12 * w + 6 + k, vmem, at, me).wait_recv()
                store(vmem, hbm, at)
        for cp in sent:
            cp.wait_send()
        for cp in stores:
            cp.wait()
        r1_out.wait()

    hbm = pl.BlockSpec(memory_space=pl.ANY)
    vm = pltpu.VMEM
    return pl.pallas_call(
        body, name="gather", in_specs=[hbm, hbm, hbm, pl.BlockSpec(memory_space=vm)], out_specs=[hbm, hbm, hbm, hbm],
        out_shape=[jax.ShapeDtypeStruct((IN_W, D_MODEL), act), jax.ShapeDtypeStruct((D_MODEL, D_MODEL), act),
                   jax.ShapeDtypeStruct((seq, D_MODEL), act), jax.ShapeDtypeStruct((seq, 1), F32)],
        scratch_shapes=[vm(a32.shape, F32), vm(b32.shape, F32), vm((IN_W, D_MODEL), act), vm((D_MODEL, D_MODEL), act),
                        vm((2, xc, D_MODEL), F32), vm((2, xc, D_MODEL), act), vm((seq, 1), F32),
                        pltpu.SemaphoreType.DMA((24,)), pltpu.SemaphoreType.DMA((24,)), pltpu.SemaphoreType.DMA((26,)),
                        pltpu.SemaphoreType.DMA((5,))],
        compiler_params=pltpu.CompilerParams(vmem_limit_bytes=VMEM_LIMIT),
    )(a32, b32, x, g1)


def _wgrad_reduce_call(dproj_t, h, mixed_t, dout, sw, vec):
    wire = jnp.bfloat16
    half_sw = SW_ROWS // 2
    seq = h.shape[0]
    tk = min(2048, seq)
    nk = seq // tk
    n_steps = 2 * N_CHIPS
    rel_of = lambda s: s % 3 if s < 6 else 3
    half_of = lambda s: s // 3
    x, y, c = lax.axis_index("x"), lax.axis_index("y"), lax.axis_index("c")
    chip_of = [2 * (1 - x) + (1 - y), 2 * (1 - x) + y, 2 * x + (1 - y), 2 * x + y]
    halves = [half_of(s) for s in range(n_steps - 2)] + [1 - c, c]
    order = jnp.stack([2 * chip_of[rel_of(s)] + halves[s] for s in range(n_steps)]).astype(jnp.int32)

    def body(order_ref, dpt_ref, h_hbm, mxt_ref, dout_hbm, sw_ref, small_ref, oa_ref, ob_ref, osw_ref, osmall_ref,
             h_v, dout_v, acc_a, acc_b, sib_a, sib_b, snd_a, snd_b, in_a, in_b, fin_a, fin_b,
             all_small, sw_sib, sw_chips, sw_fin, send_sems, recv_sems, loc_sems):
        x, y, c = lax.axis_index("x"), lax.axis_index("y"), lax.axis_index("c")
        me, sibling = (x, y, c), (x, y, 1 - c)
        steps = [(1 - x, 1 - y), (1 - x, y), (x, 1 - y)]
        j = 2 * x + y
        dev = 4 * x + 2 * y + c
        b, k = pl.program_id(0), pl.program_id(1)

        def copy(n, src, dst, to):
            return pltpu.make_async_remote_copy(src_ref=src, dst_ref=dst, send_sem=send_sems.at[n], recv_sem=recv_sems.at[n],
                                                device_id=to, device_id_type=MESH)

        def sw_rows(ref, hf):
            return ref.at[pl.ds(pl.multiple_of(hf * half_sw, 8), half_sw), :]

        sm_first = [copy(16, small_ref, all_small.at[dev], sibling)]
        sm_first += [copy(17 + r, small_ref, all_small.at[dev], (*chip, c)) for r, chip in enumerate(steps)]
        sw_to_sib = copy(23, sw_rows(sw_ref, 1 - c), sw_sib, sibling)

        @pl.when((b == 0) & (k == 0))
        def _():
            all_small[dev] = small_ref[...]
            for cp in sm_first + [sw_to_sib]:
                cp.start()

        def load(kk):
            rows = pl.ds(kk * tk, tk)
            return [pltpu.make_async_copy(h_hbm.at[rows, :], h_v.at[rows, :], loc_sems.at[kk]),
                    pltpu.make_async_copy(dout_hbm.at[rows, :], dout_v.at[rows, :], loc_sems.at[nk + kk])]

        for kk in range(nk):
            @pl.when((b == 0) & (k == 0))
            def _():
                for cp in load(kk):
                    cp.start()

        for kk in range(nk):
            @pl.when((b == 0) & (k == kk))
            def _():
                for cp in load(kk):
                    cp.wait()

        tok = pl.ds(pl.multiple_of(k * tk, tk), tk)
        pa = _mm(dpt_ref[...], h_v[tok, :])
        pb = _mm(mxt_ref[...], dout_v[tok, :])
        slot = b % 2

        @pl.when(k == 0)
        def _():
            acc_a[slot] = pa
            acc_b[slot] = pb

        @pl.when(k != 0)
        def _():
            acc_a[slot] += pa
            acc_b[slot] += pb

        def to_sibling(s):
            r = rel_of(s)
            return [copy(r, acc_a.at[s % 2], sib_a.at[r], sibling), copy(4 + r, acc_b.at[s % 2], sib_b.at[r], sibling)]

        def chip_partial(s):
            r = rel_of(s)
            copy(r, sib_a.at[r], sib_a.at[r], me).wait_recv()
            copy(4 + r, sib_b.at[r], sib_b.at[r], me).wait_recv()
            return acc_a[s % 2] + sib_a[r], acc_b[s % 2] + sib_b[r]

        def to_owner(r):
            return [copy(8 + r, snd_a.at[r], in_a.at[r], (*steps[r], c)), copy(11 + r, snd_b.at[r], in_b.at[r], (*steps[r], c))]

        def vec_forwards():
            return [copy(20 + r, all_small.at[4 * cx + 2 * cy + c], all_small.at[4 * cx + 2 * cy + c], sibling)
                    for r, (cx, cy) in enumerate(steps)]

        def sw_to_chips():
            return [copy(24 + r, sw_chips.at[j], sw_chips.at[j], (*chip, c)) for r, chip in enumerate(steps)]

        def sw_to_sibling():
            return copy(27, sw_rows(osw_ref, c), sw_fin, sibling)

        @pl.when((b == 1) & (k == 0))
        def _():
            for r, (cx, cy) in enumerate(steps):
                d = 4 * cx + 2 * cy + c
                copy(17 + r, all_small.at[d], all_small.at[d], me).wait_recv()
            for cp in vec_forwards():
                cp.start()
            copy(23, sw_sib, sw_sib, me).wait_recv()
            sw_chips[j] = (sw_rows(sw_ref, c)[...] + sw_sib[...]).astype(wire)
            for cp in sw_to_chips():
                cp.start()

        @pl.when((b == 3) & (k == 0))
        def _():
            for r, (cx, cy) in enumerate(steps):
                cj = 2 * cx + cy
                copy(24 + r, sw_chips.at[cj], sw_chips.at[cj], me).wait_recv()
            tot_sw = sw_chips[0].astype(F32)
            for q in range(1, N_CHIPS):
                tot_sw = tot_sw + sw_chips[q].astype(F32)
            sw_rows(osw_ref, c)[...] = tot_sw
            sw_to_sibling().start()

        late = min(1, nk - 1)
        n_far = n_steps - 2
        for s in range(n_steps):
            if 1 <= s <= n_far:
                sp = s - 1

                @pl.when((b == s) & (k == late) & (c == half_of(sp)))
                def _():
                    ta, tb = chip_partial(sp)
                    r = rel_of(sp)
                    snd_a[r] = ta.astype(wire)
                    snd_b[r] = tb.astype(wire)
                    for cp in to_owner(r):
                        cp.start()

                @pl.when((b == s) & (k == nk - 1) & (c != half_of(sp)))
                def _():
                    for cp in to_sibling(sp):
                        cp.wait_send()

            if s < n_far:
                @pl.when((b == s) & (k == nk - 1) & (c != half_of(s)))
                def _():
                    for cp in to_sibling(s):
                        cp.start()

        @pl.when((b == n_far) & (k == nk - 1))
        def _():
            for cp in to_sibling(n_far):
                cp.start()

        @pl.when((b == n_steps - 1) & (k == nk - 1))
        def _():
            for cp in to_sibling(n_far):
                cp.wait_send()
            tot_a, tot_b = chip_partial(n_steps - 1)
            for s in range(3):
                copy(8 + s, in_a.at[s], in_a.at[s], me).wait_recv()
                copy(11 + s, in_b.at[s], in_b.at[s], me).wait_recv()
                tot_a = tot_a + in_a[s].astype(F32)
                tot_b = tot_b + in_b[s].astype(F32)
            mine_a = oa_ref.at[pl.ds(pl.multiple_of(c * HALF_A, 8), HALF_A), :]
            mine_b = ob_ref.at[pl.ds(pl.multiple_of(c * HALF_B, 8), HALF_B), :]
            mine_a[...] = tot_a
            mine_b[...] = tot_b
            back = [copy(14, mine_a, fin_a, sibling), copy(15, mine_b, fin_b, sibling)]
            for cp in back:
                cp.start()

            copy(16, small_ref, all_small.at[dev ^ 1], me).wait_recv()
            for r, (cx, cy) in enumerate(steps):
                d = 4 * cx + 2 * cy + (1 - c)
                copy(20 + r, all_small.at[d], all_small.at[d], me).wait_recv()
            tot = all_small[0]
            for d in range(1, N_DEV):
                tot = tot + all_small[d]
            osmall_ref[...] = tot

            copy(14, fin_a, fin_a, me).wait_recv()
            copy(15, fin_b, fin_b, me).wait_recv()
            copy(27, sw_fin, sw_fin, me).wait_recv()
            oa_ref[pl.ds(pl.multiple_of((1 - c) * HALF_A, 8), HALF_A), :] = fin_a[...]
            ob_ref[pl.ds(pl.multiple_of((1 - c) * HALF_B, 8), HALF_B), :] = fin_b[...]
            sw_rows(osw_ref, 1 - c)[...] = sw_fin[...]
            sends = sm_first + vec_forwards() + sw_to_chips() + back + [sw_to_sib, sw_to_sibling()]
            for s in range(3):
                sends += to_owner(s)
            for cp in sends:
                cp.wait_send()

    vmem = pl.BlockSpec(memory_space=pltpu.VMEM)
    vm = pltpu.VMEM
    grid_spec = pltpu.PrefetchScalarGridSpec(
        num_scalar_prefetch=1, grid=(n_steps, nk),
        in_specs=[pl.BlockSpec((HALF_A, tk), lambda b, k, o: (o[b], k)), pl.BlockSpec(memory_space=pl.ANY),
                  pl.BlockSpec((HALF_B, tk), lambda b, k, o: (o[b], k)), pl.BlockSpec(memory_space=pl.ANY),
                  vmem, vmem],
        out_specs=[vmem, vmem, vmem, vmem],
        scratch_shapes=[vm((seq, D_MODEL), h.dtype), vm((seq, D_MODEL), dout.dtype),
                        vm((2, HALF_A, D_MODEL), F32), vm((2, HALF_B, D_MODEL), F32),
                        vm((N_CHIPS, HALF_A, D_MODEL), F32), vm((N_CHIPS, HALF_B, D_MODEL), F32),
                        vm((3, HALF_A, D_MODEL), wire), vm((3, HALF_B, D_MODEL), wire),
                        vm((3, HALF_A, D_MODEL), wire), vm((3, HALF_B, D_MODEL), wire),
                        vm((HALF_A, D_MODEL), F32), vm((HALF_B, D_MODEL), F32),
                        vm((N_DEV, VEC_ROWS, 128), F32), vm((half_sw, 128), F32), vm((N_CHIPS, half_sw, 128), wire),
                        vm((half_sw, 128), F32),
                        pltpu.SemaphoreType.DMA((28,)), pltpu.SemaphoreType.DMA((28,)), pltpu.SemaphoreType.DMA((2 * nk,))])
    return pl.pallas_call(
        body, name="wgrad_reduce", grid_spec=grid_spec,
        out_shape=[jax.ShapeDtypeStruct((W_IN_SHARD, D_MODEL), F32), jax.ShapeDtypeStruct((W_OUT_SHARD, D_MODEL), F32),
                   jax.ShapeDtypeStruct((SW_ROWS, 128), F32), jax.ShapeDtypeStruct((VEC_ROWS, 128), F32)],
        compiler_params=pltpu.CompilerParams(dimension_semantics=("arbitrary", "arbitrary"), vmem_limit_bytes=VMEM_LIMIT),
    )(order, dproj_t, h, mixed_t, dout, sw, vec)


def _adamw(w, g, m, v):
    nm = ADAM_B1 * m + (1.0 - ADAM_B1) * g
    nv = ADAM_B2 * v + (1.0 - ADAM_B2) * (g * g)
    m_hat = nm / (1.0 - ADAM_B1 ** ADAM_STEP)
    v_hat = nv / (1.0 - ADAM_B2 ** ADAM_STEP)
    return -ADAM_LR * (m_hat / (jnp.sqrt(v_hat) + ADAM_EPS) + ADAM_WD * w), nm, nv


def _adamw_shards_call(a, b, steps=4):
    def body(*refs):
        for k in range(2):
            w_ref, g_ref, m_ref, v_ref = refs[4 * k:4 * k + 4]
            go_ref, d_ref, nm_ref, nv_ref = refs[8 + 4 * k:12 + 4 * k]
            gg = g_ref[...]
            go_ref[...] = gg
            d_ref[...], nm_ref[...], nv_ref[...] = _adamw(w_ref[...], gg, m_ref[...], v_ref[...])

    specs, shapes = [], []
    for w in (a[0], b[0]):
        rows, cols = w.shape
        specs += [pl.BlockSpec((rows // steps, cols), lambda i: (i, 0))] * 4
        shapes += [jax.ShapeDtypeStruct((rows, cols), F32)] * 4
    return pl.pallas_call(
        body, name="adamw_shards", grid=(steps,), in_specs=specs, out_specs=specs, out_shape=shapes,
        compiler_params=pltpu.CompilerParams(dimension_semantics=("arbitrary",)),
    )(*a, *b)


_SMALL = (("norm_g", (1, D_MODEL), R_G1), ("b_in", (1, IN_W), R_BIN), ("attn_sinks", (1, 8), R_SINK),
          ("sgu_ln_g", (1, SGU_W), R_LNG), ("sgu_ln_b", (1, SGU_W), R_LNB), ("sgu_b", (N_SGU_HEADS, BLOCK), R_SGUB),
          ("b_out", (1, D_MODEL), R_BOUT), ("final_norm_g", (1, D_MODEL), R_G3))


def _adamw_small_call(sw_g, vec_g, sgu_w3, ws, ms, vs):
    n = len(_SMALL)

    def body(*refs):
        sw_ref, vec_ref = refs[0], refs[1]
        w3 = refs[2:5]
        w_refs, m_refs, v_refs = refs[5:5 + n], refs[5 + n:5 + 2 * n], refs[5 + 2 * n:5 + 3 * n]
        outs = refs[5 + 3 * n:]
        outs[0][...] = vec_ref[R_LOSS:R_LOSS + 1, 0:1]
        g = sw_ref[...]
        outs[1][...] = g
        outs[2][...], outs[3][...], outs[4][...] = _adamw(w3[0][...], g, w3[1][...], w3[2][...])
        for k, (_, shape, row) in enumerate(_SMALL):
            if shape[0] == 1 and shape[1] >= 128:
                g = jnp.concatenate([vec_ref[row + q:row + q + 1, :] for q in range(shape[1] // 128)], axis=1)
            else:
                g = vec_ref[row:row + shape[0], 0:shape[1]]
            o = outs[5 + 4 * k:9 + 4 * k]
            o[0][...] = g
            o[1][...], o[2][...], o[3][...] = _adamw(w_refs[k][...], g, m_refs[k][...], v_refs[k][...])

    vmem = pl.BlockSpec(memory_space=pltpu.VMEM)
    out_shape = [jax.ShapeDtypeStruct((1, 1), F32)] + [jax.ShapeDtypeStruct((SW_ROWS, 128), F32)] * 4
    for _, shape, _ in _SMALL:
        out_shape += [jax.ShapeDtypeStruct(shape, F32)] * 4
    args = [sw_g, vec_g, *sgu_w3, *ws, *ms, *vs]
    return pl.pallas_call(
        body, name="adamw_small", in_specs=[vmem] * len(args), out_specs=[vmem] * len(out_shape), out_shape=out_shape,
    )(*args)


def kernel(x, norm_g, w_in, b_in, attn_sinks, sgu_ln_g, sgu_ln_b, sgu_w, sgu_b, w_out, b_out, final_norm_g, loss_target, m_norm_g, m_w_in, m_b_in, m_attn_sinks, m_sgu_ln_g, m_sgu_ln_b, m_sgu_w, m_sgu_b, m_w_out, m_b_out, m_final_norm_g, v_norm_g, v_w_in, v_b_in, v_attn_sinks, v_sgu_ln_g, v_sgu_ln_b, v_sgu_w, v_sgu_b, v_w_out, v_b_out, v_final_norm_g):
    seq = x.shape[1]
    win_t, wout, h, r1 = _gather_call(w_in[0].T, w_out[0], x[0], norm_g)
    dx, dproj_t, mixed_t, dout, sw, vec = _fused_call(
        x[0], h, r1, loss_target[0], attn_sinks[0], norm_g, b_in, sgu_ln_g, sgu_ln_b, sgu_w[0], sgu_b[0], b_out,
        final_norm_g.reshape(1, D_MODEL), win_t, wout)
    ga_t, g_w_out, sw, vec = _wgrad_reduce_call(dproj_t, h, mixed_t, dout, sw, vec)

    names = ["norm_g", "w_in", "b_in", "attn_sinks", "sgu_ln_g", "sgu_ln_b", "sgu_w", "sgu_b", "w_out", "b_out", "final_norm_g"]
    res = {}
    shards = _adamw_shards_call((w_in[0].T, ga_t, m_w_in[0].T, v_w_in[0].T), (w_out[0], g_w_out, m_w_out[0], v_w_out[0]))
    res["w_in"] = [a.T[None] for a in shards[:4]]
    res["w_out"] = [a[None] for a in shards[4:]]
    given = dict(norm_g=(norm_g, m_norm_g, v_norm_g), b_in=(b_in, m_b_in, v_b_in), attn_sinks=(attn_sinks, m_attn_sinks, v_attn_sinks),
                 sgu_ln_g=(sgu_ln_g, m_sgu_ln_g, v_sgu_ln_g), sgu_ln_b=(sgu_ln_b, m_sgu_ln_b, v_sgu_ln_b),
                 sgu_b=(sgu_b, m_sgu_b, v_sgu_b), b_out=(b_out, m_b_out, v_b_out),
                 final_norm_g=(final_norm_g, m_final_norm_g, v_final_norm_g))
    wmv = [[given[n][k].reshape(shape) for n, shape, _ in _SMALL] for k in range(3)]
    outs = _adamw_small_call(sw, vec, [a.reshape(SW_ROWS, BLOCK) for a in (sgu_w, m_sgu_w, v_sgu_w)], *wmv)
    loss = outs[0].reshape(())
    res["sgu_w"] = [a.reshape(sgu_w.shape) for a in outs[1:5]]
    for k, (n, _, _) in enumerate(_SMALL):
        res[n] = [a.reshape(given[n][0].shape) for a in outs[5 + 4 * k:9 + 4 * k]]

    return (loss, dx[None], *[res[n][0] for n in names], *[res[n][1] for n in names], *[res[n][2] for n in names],
            *[res[n][3] for n in names])
```

```python
import functools
import math

import jax
import jax.numpy as jnp
from jax import lax
from jax.experimental import pallas as pl
from jax.experimental.pallas import tpu as pltpu

F32 = jnp.float32
MXU_DTYPE = jnp.bfloat16

D_MODEL = 1024
HEAD_DIM = 64
ATTN_W = 512
SGU_W = 512
N_SGU_HEADS = 8
BLOCK = 128
IN_W = 2816
OFF_K, OFF_V, OFF_ZA, OFF_US, OFF_VS, OFF_ZS = 512, 640, 768, 1280, 1792, 2304
NORM_EPS = 1e-5
NEG_INF = -1e30
SCALE = HEAD_DIM ** -0.5
SQRT_HALF = math.sqrt(0.5)
INV_SQRT_2PI = 1.0 / math.sqrt(2.0 * math.pi)

N_CHIPS = 4
N_DEV = 8
W_IN_SHARD = IN_W // N_CHIPS
W_OUT_SHARD = D_MODEL // N_CHIPS
HALF_A = W_IN_SHARD // 2
HALF_B = W_OUT_SHARD // 2

TILE = 256
VMEM_LIMIT = 56 * 1024 * 1024

ADAM_LR, ADAM_B1, ADAM_B2, ADAM_EPS, ADAM_WD, ADAM_STEP = 0.001, 0.9, 0.999, 1e-08, 0.01, 10

SW_ROWS = N_SGU_HEADS * BLOCK
R_G1, R_BIN, R_SINK, R_LOSS, R_LNG, R_LNB, R_SGUB, R_BOUT, R_G3 = 0, 8, 32, 40, 48, 56, 64, 72, 80
VEC_ROWS = 88

MESH = pl.DeviceIdType.MESH


def _mm(a, b):
    return jnp.dot(a, b, preferred_element_type=F32)


def _mm_nt(a, b):
    return lax.dot_general(a, b, (((1,), (1,)), ((), ())), preferred_element_type=F32)


def _sigmoid(z):
    return 1.0 / (1.0 + jnp.exp(-z))


def _norm_cdf(z):
    return 0.5 * (1.0 + lax.erf(z * SQRT_HALF))


def _norm_pdf(z):
    return jnp.exp(-0.5 * z * z) * INV_SQRT_2PI


def _rows8(v):
    r, n = v.shape
    return jnp.sum(v.reshape(r // 8, 8, n), axis=0)


def _mean_rows(v):
    return jnp.sum(v, axis=1, keepdims=True) * (1.0 / v.shape[1])


def _fused_call(x, h, r1, tgt, sinks, g1, b_in, ln_g, ln_b, sgu_w, sgu_b, b_out, g3, win_t, wout):
    seq = x.shape[0]
    t = TILE
    nt = seq // t
    nb = t // BLOCK
    act = MXU_DTYPE

    def body(sinks_ref, x_ref, h_ref, hh_ref, r1_ref, tgt_ref, g1_ref, bin_ref, lng_ref, lnb_ref, sguw_ref, sgub_ref, bout_ref, g3_ref,
             wint_hbm, wout_hbm,
             dx_ref, dprojt_ref, mixedt_ref, dout_ref, sw_ref, vec_ref,
             dproj_ref, mixed_ref, wint_v, wout_v, wf_v, wb_v, q_s, kf_s, vf_s, k2_s, v2_s, gate_s, p_s, ps_s, o_s, u_s, mix_s, vhat_s, r2_s,
             cdfu_s, cdfv_s, doutf_s, dmix_s, dkf_s, dvf_s, carryk_s, carryv_s,
             acc_bin, acc_g1, acc_bout, acc_g3, acc_lng, acc_lnb, acc_dws, acc_dbs, acc_sink, acc_loss,
             h_s, rhs_s, vlnp_s, dvln_s, bexp_s, sems):
        i = pl.program_id(0)
        tile = nt - 1 - i
        lane128 = lax.broadcasted_iota(jnp.int32, (BLOCK, BLOCK), 1)
        lo = lane128 < HEAD_DIM

        @pl.when(i == 0)
        def _():
            cp_a = pltpu.make_async_copy(wint_hbm, wint_v, sems.at[0])
            cp_b = pltpu.make_async_copy(wout_hbm, wout_v, sems.at[1])
            cp_a.start()
            cp_b.start()
            for acc in (acc_bin, acc_g1, acc_bout, acc_g3, acc_lng, acc_lnb, acc_dws, acc_dbs, acc_sink, acc_loss,
                        carryk_s, carryv_s):
                acc[...] = jnp.zeros(acc.shape, F32)
            tril = lax.broadcasted_iota(jnp.int32, (BLOCK, BLOCK), 0) >= lane128
            for h in range(N_SGU_HEADS):
                w = jnp.where(tril, sguw_ref[h], 0.0)
                wf_v[h // 2, :, (h % 2) * BLOCK:(h % 2 + 1) * BLOCK] = w.astype(act)
                wb_v[h // 2, :, (h % 2) * BLOCK:(h % 2 + 1) * BLOCK] = w.T.astype(act)
            eye = lax.broadcasted_iota(jnp.int32, (BLOCK, BLOCK), 0) == lane128
            for p in range(4):
                col = [jnp.sum(jnp.where(eye, sgub_ref[2 * p + q:2 * p + q + 1, :], 0.0), axis=1, keepdims=True) for q in range(2)]
                bexp_s[:, p * BLOCK:(p + 1) * BLOCK] = jnp.where(lo, col[0], col[1])
            cp_a.wait()
            cp_b.wait()

        g1 = g1_ref[...]

        def rms(v):
            r = lax.rsqrt(_mean_rows(v * v) + NORM_EPS)
            return r, v * r

        h_s[0:BLOCK, :] = hh_ref[...]
        h_s[BLOCK:, :] = h_ref[...]
        r1_s = r1_ref

        h = h_ref[...]
        q = _mm_nt(h, wint_v[0:OFF_K, :]) + bin_ref[:, 0:OFF_K]
        q_s[...] = (q * SCALE).astype(act)
        kv = _mm_nt(h_s[...], wint_v[OFF_K:OFF_ZA, :]) + bin_ref[:, OFF_K:OFF_ZA]
        kf_s[...] = kv[:, :BLOCK]
        vf_s[...] = kv[:, BLOCK:]
        for r in range(4):
            cols = slice(OFF_ZA + r * 512, OFF_ZA + (r + 1) * 512)
            gate_s[r] = _mm_nt(h, wint_v[cols, :]) + bin_ref[:, cols]

        lo_kv = lax.broadcasted_iota(jnp.int32, (t + BLOCK, BLOCK), 1) < HEAD_DIM
        for src, dst in ((kf_s, k2_s), (vf_s, v2_s)):
            v = src[...]
            vr = pltpu.roll(v, HEAD_DIM, 1)
            dst[0] = jnp.where(lo_kv, v, vr).astype(act)
            dst[1] = jnp.where(lo_kv, vr, v).astype(act)

        rowi = lax.broadcasted_iota(jnp.int32, (BLOCK, 2 * BLOCK), 0)
        colj = lax.broadcasted_iota(jnp.int32, (BLOCK, 2 * BLOCK), 1)
        in_band = (colj > rowi) & (colj <= rowi + BLOCK)
        row512 = lax.broadcasted_iota(jnp.int32, (4 * BLOCK, 1), 0)

        def stacked_q(b, g):
            parts = []
            for p in range(2):
                slab = q_s[b * BLOCK:(b + 1) * BLOCK, (2 * g + p) * BLOCK:(2 * g + p + 1) * BLOCK]
                parts += [jnp.where(lo, slab, jnp.zeros_like(slab)), jnp.where(lo, jnp.zeros_like(slab), slab)]
            return jnp.concatenate(parts, axis=0)

        def sink_col(g):
            s = [sinks_ref[4 * g + k] for k in range(4)]
            return jnp.where(row512 < BLOCK, s[0], jnp.where(row512 < 2 * BLOCK, s[1], jnp.where(row512 < 3 * BLOCK, s[2], s[3])))

        for b in range(nb):
            band = slice(b * BLOCK, (b + 2) * BLOCK)
            first_key = jnp.where(tile * nb + b > 0, 0, BLOCK)
            valid = in_band & (colj >= first_key)
            valid4 = jnp.concatenate([valid] * 4, axis=0)
            for g in range(2):
                s = _mm_nt(stacked_q(b, g), k2_s[g, band, :])
                s = jnp.where(valid4, s, NEG_INF)
                sk = sink_col(g)
                m = jnp.maximum(jnp.max(s, axis=1, keepdims=True), sk)
                p = jnp.exp(s - m)
                psk = jnp.exp(sk - m)
                inv = 1.0 / (jnp.sum(p, axis=1, keepdims=True) + psk)
                p = p * inv
                p_s[b * 2 + g] = p
                ps_s[b * 2 + g] = psk * inv
                o2 = _mm(p.astype(act), v2_s[g, band, :])
                for pr in range(2):
                    o_s[b * BLOCK:(b + 1) * BLOCK, (2 * g + pr) * BLOCK:(2 * g + pr + 1) * BLOCK] = jnp.where(
                        lo, o2[(2 * pr) * BLOCK:(2 * pr + 1) * BLOCK], o2[(2 * pr + 1) * BLOCK:(2 * pr + 2) * BLOCK])

        lng = lng_ref[...]
        lnb = lnb_ref[...]

        def split_pairs(val, c):
            for p in range(4):
                slab = val[:, p * BLOCK:(p + 1) * BLOCK]
                rhs_s[p, 0:BLOCK, c * BLOCK:(c + 1) * BLOCK] = jnp.where(lo, slab, 0.0).astype(act)
                rhs_s[p, BLOCK:, c * BLOCK:(c + 1) * BLOCK] = jnp.where(lo, 0.0, slab).astype(act)

        for c in range(nb):
            rows = slice(c * BLOCK, (c + 1) * BLOCK)
            za = gate_s[0, rows, :]
            mixed_ref[rows, 0:ATTN_W] = (o_s[rows, :] * (za * _sigmoid(za))).astype(act)
            us = gate_s[1, rows, :]
            vs = gate_s[2, rows, :]
            cu = _norm_cdf(us)
            cv = _norm_cdf(vs)
            cdfu_s[rows, :] = cu
            cdfv_s[rows, :] = cv
            u = us * cu
            vg = vs * cv
            vc = vg - _mean_rows(vg)
            r2 = lax.rsqrt(_mean_rows(vc * vc) + NORM_EPS)
            vhat = vc * r2
            r2_s[rows, :] = r2
            vhat_s[rows, :] = vhat
            u_s[rows, :] = u
            split_pairs(vhat * lng + lnb, c)
        for p in range(4):
            cols = slice(p * BLOCK, (p + 1) * BLOCK)
            mix = _mm(wf_v[p], rhs_s[p])
            for c in range(nb):
                mix_s[c * BLOCK:(c + 1) * BLOCK, cols] = mix[:, c * BLOCK:(c + 1) * BLOCK] + bexp_s[:, cols]
        for c in range(nb):
            rows = slice(c * BLOCK, (c + 1) * BLOCK)
            zs = gate_s[3, rows, :]
            mixed_ref[rows, ATTN_W:] = (u_s[rows, :] * mix_s[rows, :] * (zs * _sigmoid(zs))).astype(act)

        g3 = g3_ref[...]
        proj_o = _mm(mixed_ref[...], wout_v[...])
        for c in range(nb):
            rows = slice(c * BLOCK, (c + 1) * BLOCK)
            out = x_ref[rows, :] + proj_o[rows, :] + bout_ref[...]
            r3, on = rms(out)
            e = on * g3 - tgt_ref[rows, :]
            e2 = _rows8(e * e)
            acc_loss[...] += sum(e2[:, k * 128:(k + 1) * 128] for k in range(D_MODEL // 128)) * (0.5 / D_MODEL)
            acc_g3[...] += _rows8(e * on)
            don = e * g3
            dout = (r3 * (1.0 / D_MODEL)) * (don - on * _mean_rows(don * on))
            doutf_s[rows, :] = dout
            dout_ref[rows, :] = dout.astype(act)
            acc_bout[...] += _rows8(dout)
        dmix_s[...] = _mm_nt(dout_ref[...], wout_v[...])

        dkf_s[...] = jnp.zeros(dkf_s.shape, F32)
        dvf_s[...] = jnp.zeros(dvf_s.shape, F32)
        for b in range(nb):
            rows = slice(b * BLOCK, (b + 1) * BLOCK)
            band = slice(b * BLOCK, (b + 2) * BLOCK)
            za = gate_s[0, rows, :]
            sg = _sigmoid(za)
            dao = dmix_s[rows, 0:ATTN_W]
            o = o_s[rows, :]
            do = dao * (za * sg)
            dza = dao * o * (sg * (1.0 + za * (1.0 - sg)))
            dproj_ref[rows, OFF_ZA:OFF_US] = dza.astype(act)
            acc_bin[:, OFF_ZA:OFF_US] += _rows8(dza)
            for g in range(2):
                do_parts, delta_parts = [], []
                for pr in range(2):
                    cols = slice((2 * g + pr) * BLOCK, (2 * g + pr + 1) * BLOCK)
                    d_pair = do[:, cols]
                    prod = d_pair * o[:, cols]
                    do_parts += [jnp.where(lo, d_pair, 0.0).astype(act), jnp.where(lo, 0.0, d_pair).astype(act)]
                    delta_parts += [jnp.sum(jnp.where(lo, prod, 0.0), axis=1, keepdims=True),
                                    jnp.sum(jnp.where(lo, 0.0, prod), axis=1, keepdims=True)]
                do_st = jnp.concatenate(do_parts, axis=0)
                delta = jnp.concatenate(delta_parts, axis=0)
                p = p_s[b * 2 + g]
                dp = _mm_nt(do_st, v2_s[g, band, :])
                ds = p * (dp - delta)
                sink_t = ps_s[b * 2 + g] * delta
                for k in range(4):
                    acc_sink[4 * g + k:4 * g + k + 1, :] += -jnp.sum(sink_t[k * BLOCK:(k + 1) * BLOCK], axis=0, keepdims=True)
                ds_a = ds.astype(act)
                dq2 = _mm(ds_a, k2_s[g, band, :]) * SCALE
                for pr in range(2):
                    cols = slice((2 * g + pr) * BLOCK, (2 * g + pr + 1) * BLOCK)
                    dq = jnp.where(lo, dq2[(2 * pr) * BLOCK:(2 * pr + 1) * BLOCK], dq2[(2 * pr + 1) * BLOCK:(2 * pr + 2) * BLOCK])
                    dproj_ref[rows, cols] = dq.astype(act)
                    acc_bin[:, cols] += _rows8(dq)
                for acc, lhs, rhs in ((dkf_s, stacked_q(b, g), ds_a), (dvf_s, do_st, p.astype(act))):
                    d2t = _mm(lhs.T, rhs)
                    acc[g * HEAD_DIM:(g + 1) * HEAD_DIM, band] += d2t[0:HEAD_DIM, :] + d2t[HEAD_DIM:, :]
        for acc, carry, off in ((dkf_s, carryk_s, OFF_K), (dvf_s, carryv_s, OFF_V)):
            acc[:, t:t + BLOCK] += carry[...]
            carry[...] = acc[:, 0:BLOCK]
            d = acc[:, BLOCK:].T
            dproj_ref[:, off:off + BLOCK] = d.astype(act)
            acc_bin[:, off:off + BLOCK] += _rows8(d)

        for c in range(nb):
            rows = slice(c * BLOCK, (c + 1) * BLOCK)
            dso = dmix_s[rows, ATTN_W:]
            u = u_s[rows, :]
            mix = mix_s[rows, :]
            zs = gate_s[3, rows, :]
            sg = _sigmoid(zs)
            sgs = zs * sg
            du = dso * mix * sgs
            dmx = dso * u * sgs
            dzs = dso * u * mix * (sg * (1.0 + zs * (1.0 - sg)))
            us = gate_s[1, rows, :]
            dus = du * (cdfu_s[rows, :] + us * _norm_pdf(us))
            vln = (vhat_s[rows, :] * lng + lnb).astype(act)
            for p in range(4):
                vlnp_s[p, :, c * BLOCK:(c + 1) * BLOCK] = vln[:, p * BLOCK:(p + 1) * BLOCK]
            split_pairs(dmx, c)
            acc_dbs[...] += dmx
            for off, val in ((OFF_US, dus), (OFF_ZS, dzs)):
                dproj_ref[rows, off:off + 512] = val.astype(act)
                acc_bin[:, off:off + 512] += _rows8(val)
        for p in range(4):
            dvln = _mm(wb_v[p], rhs_s[p])
            for c in range(nb):
                dvln_s[c * BLOCK:(c + 1) * BLOCK, p * BLOCK:(p + 1) * BLOCK] = dvln[:, c * BLOCK:(c + 1) * BLOCK]
            acc_dws[(2 * p) * BLOCK:(2 * p + 1) * BLOCK, :] += _mm_nt(rhs_s[p, 0:BLOCK, :], vlnp_s[p])
            acc_dws[(2 * p + 1) * BLOCK:(2 * p + 2) * BLOCK, :] += _mm_nt(rhs_s[p, BLOCK:, :], vlnp_s[p])
        for c in range(nb):
            rows = slice(c * BLOCK, (c + 1) * BLOCK)
            dvln = dvln_s[rows, :]
            vhat = vhat_s[rows, :]
            acc_lng[...] += _rows8(dvln * vhat)
            acc_lnb[...] += _rows8(dvln)
            dvhat = dvln * lng
            dvg = r2_s[rows, :] * (dvhat - _mean_rows(dvhat) - vhat * _mean_rows(dvhat * vhat))
            vs = gate_s[2, rows, :]
            dvs = dvg * (cdfv_s[rows, :] + vs * _norm_pdf(vs))
            dproj_ref[rows, OFF_VS:OFF_VS + 512] = dvs.astype(act)
            acc_bin[:, OFF_VS:OFF_VS + 512] += _rows8(dvs)

        dh = _mm(dproj_ref[...], wint_v[...])
        dprojt_ref[...] = dproj_ref[...].T
        mixedt_ref[...] = mixed_ref[...].T
        for c in range(nb):
            rows = slice(c * BLOCK, (c + 1) * BLOCK)
            r1 = r1_s[rows, :]
            xn = x_ref[rows, :] * r1
            dhc = dh[rows, :]
            acc_g1[...] += _rows8(dhc * xn)
            dxn = dhc * g1
            dx_ref[rows, :] = doutf_s[rows, :] + r1 * (dxn - xn * _mean_rows(dxn * xn))

        @pl.when(i == nt - 1)
        def _():
            tril = lax.broadcasted_iota(jnp.int32, (BLOCK, BLOCK), 0) >= lane128
            for hh in range(N_SGU_HEADS):
                rws = slice(hh * BLOCK, (hh + 1) * BLOCK)
                sw_ref[rws, :] = jnp.where(tril, acc_dws[rws, :], 0.0)
            vec_ref[...] = jnp.zeros((VEC_ROWS, 128), F32)

            def put(row0, acc, scale=1.0):
                s = jnp.sum(acc[...], axis=0, keepdims=True) * scale
                for k in range(acc.shape[1] // 128):
                    vec_ref[row0 + k:row0 + k + 1, :] = s[:, k * 128:(k + 1) * 128]

            put(R_G1, acc_g1)
            put(R_BIN, acc_bin)
            put(R_LNG, acc_lng)
            put(R_LNB, acc_lnb)
            put(R_BOUT, acc_bout)
            put(R_G3, acc_g3, 1.0 / D_MODEL)
            vec_ref[R_SINK:R_SINK + 1, :] = jnp.sum(
                jnp.where(lax.broadcasted_iota(jnp.int32, (8, 128), 0) == lax.broadcasted_iota(jnp.int32, (8, 128), 1),
                          acc_sink[...], 0.0), axis=0, keepdims=True)
            vec_ref[R_LOSS:R_LOSS + 1, :] = jnp.zeros((1, 128), F32) + jnp.sum(acc_loss[...])
            dbs_t = acc_dbs[...].T
            vec_ref[R_SGUB:R_SGUB + 8, :] = jnp.sum(dbs_t.reshape(N_SGU_HEADS, SGU_W // N_SGU_HEADS, BLOCK), axis=1)

    full = lambda shape: pl.BlockSpec(shape, lambda i: (0,) * len(shape))
    tok = lambda w: pl.BlockSpec((t, w), lambda i: (nt - 1 - i, 0))
    in_specs = [
        pl.BlockSpec(memory_space=pltpu.SMEM),
        tok(D_MODEL), tok(D_MODEL),
        pl.BlockSpec((BLOCK, D_MODEL), lambda i: (jnp.maximum((nt - 1 - i) * nb - 1, 0), 0)),
        tok(1), tok(D_MODEL),
        full((1, D_MODEL)), full((1, IN_W)), full((1, SGU_W)), full((1, SGU_W)),
        full((N_SGU_HEADS, BLOCK, BLOCK)), full((N_SGU_HEADS, BLOCK)), full((1, D_MODEL)), full((1, D_MODEL)),
        pl.BlockSpec(memory_space=pl.ANY), pl.BlockSpec(memory_space=pl.ANY),
    ]
    out_shape = [
        jax.ShapeDtypeStruct((seq, D_MODEL), F32),
        jax.ShapeDtypeStruct((IN_W, seq), act),
        jax.ShapeDtypeStruct((D_MODEL, seq), act),
        jax.ShapeDtypeStruct((seq, D_MODEL), act),
        jax.ShapeDtypeStruct((SW_ROWS, 128), F32),
        jax.ShapeDtypeStruct((VEC_ROWS, 128), F32),
    ]
    tok_t = lambda w: pl.BlockSpec((w, t), lambda i: (0, nt - 1 - i))
    out_specs = [tok(D_MODEL), tok_t(IN_W), tok_t(D_MODEL), tok(D_MODEL), full((SW_ROWS, 128)), full((VEC_ROWS, 128))]
    vm = pltpu.VMEM
    scratch = [
        vm((t, IN_W), act), vm((t, D_MODEL), act),
        vm((IN_W, D_MODEL), act), vm((D_MODEL, D_MODEL), act),
        vm((4, BLOCK, 2 * BLOCK), act), vm((4, BLOCK, 2 * BLOCK), act),
        vm((t, ATTN_W), act),
        vm((t + BLOCK, BLOCK), F32), vm((t + BLOCK, BLOCK), F32),
        vm((2, t + BLOCK, BLOCK), act), vm((2, t + BLOCK, BLOCK), act),
        vm((4, t, 512), F32),
        vm((2 * nb, 4 * BLOCK, 2 * BLOCK), F32), vm((2 * nb, 4 * BLOCK, 1), F32),
        vm((t, ATTN_W), F32), vm((t, SGU_W), F32), vm((t, SGU_W), F32), vm((t, SGU_W), F32), vm((t, 1), F32),
        vm((t, SGU_W), F32), vm((t, SGU_W), F32),
        vm((t, D_MODEL), F32), vm((t, D_MODEL), F32),
        vm((BLOCK, t + BLOCK), F32), vm((BLOCK, t + BLOCK), F32), vm((BLOCK, BLOCK), F32), vm((BLOCK, BLOCK), F32),
        vm((8, IN_W), F32), vm((8, D_MODEL), F32), vm((8, D_MODEL), F32), vm((8, D_MODEL), F32),
        vm((8, SGU_W), F32), vm((8, SGU_W), F32), vm((N_SGU_HEADS * BLOCK, BLOCK), F32), vm((BLOCK, SGU_W), F32),
        vm((8, 128), F32), vm((8, 128), F32),
        vm((t + BLOCK, D_MODEL), act), vm((4, 2 * BLOCK, t), act), vm((4, BLOCK, t), act), vm((t, SGU_W), F32),
        vm((BLOCK, SGU_W), F32),
        pltpu.SemaphoreType.DMA((2,)),
    ]
    return pl.pallas_call(
        body, name="fused", grid=(nt,), in_specs=in_specs, out_specs=out_specs, out_shape=out_shape,
        scratch_shapes=scratch,
        compiler_params=pltpu.CompilerParams(dimension_semantics=("arbitrary",), vmem_limit_bytes=VMEM_LIMIT),
    )(sinks, x, h, h, r1, tgt, g1, b_in, ln_g, ln_b, sgu_w, sgu_b, b_out, g3, win_t, wout)


def _gather_call(a32, b32, x, g1):
    act = MXU_DTYPE
    seq = x.shape[0]
    xc = min(256, seq)
    n_xc = seq // xc

    def body(a32_hbm, b32_hbm, x_hbm, g1_ref, ga_hbm, gb_hbm, h_hbm, r1_hbm, a32_v, b32_v, ga_v, gb_v, x_buf, h_buf, r1_v,
             send_sems, recv_sems, loc_sems, norm_sems):
        x, y, c = lax.axis_index("x"), lax.axis_index("y"), lax.axis_index("c")
        me, sibling, x_nbr, y_nbr = (x, y, c), (x, y, 1 - c), (1 - x, y, c), (x, 1 - y, c)
        j, j_x, j_y, j_d = 2 * x + y, 2 * (1 - x) + y, 2 * x + (1 - y), 2 * (1 - x) + (1 - y)
        arrays = ((0, ga_v, ga_hbm, W_IN_SHARD), (1, gb_v, gb_hbm, W_OUT_SHARD))

        loads = [pltpu.make_async_copy(a32_hbm, a32_v, loc_sems.at[0]), pltpu.make_async_copy(b32_hbm, b32_v, loc_sems.at[1])]
        for cp in reversed(loads):
            cp.start()

        def rows(shard, chip, hf=None, q=None):
            if hf is None:
                return pl.ds(pl.multiple_of(chip * shard, 16), shard)
            return pl.ds(pl.multiple_of(chip * shard + hf * (shard // 2) + q * (shard // 4), 16), shard // 4)

        def copy(k, ref, at, to):
            return pltpu.make_async_remote_copy(src_ref=ref.at[at, :], dst_ref=ref.at[at, :], send_sem=send_sems.at[k],
                                                recv_sem=recv_sems.at[k], device_id=to, device_id_type=MESH)

        stores = []

        def store(vmem, hbm, at):
            stores.append(pltpu.make_async_copy(vmem.at[at, :], hbm.at[at, :], loc_sems.at[len(stores)]))
            stores[-1].start()

        sent = []
        for w, vmem, hbm, shard in reversed(arrays):
            loads[w].wait()
            vmem[rows(shard, j), :] = (a32_v, b32_v)[w][...].astype(act)
            for k, (q, to) in enumerate(((0, x_nbr), (1, x_nbr), (1, y_nbr), (0, y_nbr))):
                sent.append(copy(12 * w + k, vmem, rows(shard, j, c, q), to))
                sent[-1].start()
        for w, vmem, hbm, shard in arrays:
            store(vmem, hbm, rows(shard, j))

        def x_load(n):
            return pltpu.make_async_copy(x_hbm.at[pl.ds(n * xc, xc), :], x_buf.at[n % 2], norm_sems.at[n % 2])

        def h_store(n):
            return pltpu.make_async_copy(h_buf.at[n % 2], h_hbm.at[pl.ds(n * xc, xc), :], norm_sems.at[2 + n % 2])

        g1 = g1_ref[...]
        x_load(0).start()
        for n in range(n_xc):
            if n + 1 < n_xc:
                x_load(n + 1).start()
            x_load(n).wait()
            if n >= 2:
                h_store(n - 2).wait()
            xv = x_buf[n % 2]
            r = lax.rsqrt(_mean_rows(xv * xv) + NORM_EPS)
            r1_v[n * xc:(n + 1) * xc, :] = r
            h_buf[n % 2] = (xv * r * g1).astype(act)
            h_store(n).start()
        for n in range(max(n_xc - 2, 0), n_xc):
            h_store(n).wait()
        r1_out = pltpu.make_async_copy(r1_v, r1_hbm, norm_sems.at[4])
        r1_out.start()

        def landed(w, vmem, hbm, shard, k, chip, q, onward=None):
            at = rows(shard, chip, c, q)
            copy(12 * w + k, vmem, at, me).wait_recv()
            if onward is not None:
                sent.append(copy(12 * w + onward[0], vmem, at, onward[1]))
                sent[-1].start()
            sent.append(copy(12 * w + 6 + k, vmem, at, sibling))
            sent[-1].start()
            store(vmem, hbm, at)

        for arr in arrays:
            landed(*arr, 0, j_x, 0, onward=(4, y_nbr))
            landed(*arr, 2, j_y, 1, onward=(5, x_nbr))
        for arr in arrays:
            landed(*arr, 1, j_x, 1)
            landed(*arr, 3, j_y, 0)
            landed(*arr, 4, j_d, 0)
            landed(*arr, 5, j_d, 1)
        for w, vmem, hbm, shard in arrays:
            for k, (chip, q) in enumerate(((j_x, 0), (j_x, 1), (j_y, 1), (j_y, 0), (j_d, 0), (j_d, 1))):
                at = rows(shard, chip, 1 - c, q)
                copy(12 * w + 6 + k, vmem, at, me).wait_recv()
                store(vmem, hbm, at)
        for cp in sent:
            cp.wait_send()
        for cp in stores:
            cp.wait()
        r1_out.wait()

    hbm = pl.BlockSpec(memory_space=pl.ANY)
    vm = pltpu.VMEM
    return pl.pallas_call(
        body, name="gather", in_specs=[hbm, hbm, hbm, pl.BlockSpec(memory_space=vm)], out_specs=[hbm, hbm, hbm, hbm],
        out_shape=[jax.ShapeDtypeStruct((IN_W, D_MODEL), act), jax.ShapeDtypeStruct((D_MODEL, D_MODEL), act),
                   jax.ShapeDtypeStruct((seq, D_MODEL), act), jax.ShapeDtypeStruct((seq, 1), F32)],
        scratch_shapes=[vm(a32.shape, F32), vm(b32.shape, F32), vm((IN_W, D_MODEL), act), vm((D_MODEL, D_MODEL), act),
                        vm((2, xc, D_MODEL), F32), vm((2, xc, D_MODEL), act), vm((seq, 1), F32),
                        pltpu.SemaphoreType.DMA((24,)), pltpu.SemaphoreType.DMA((24,)), pltpu.SemaphoreType.DMA((26,)),
                        pltpu.SemaphoreType.DMA((5,))],
        compiler_params=pltpu.CompilerParams(vmem_limit_bytes=VMEM_LIMIT),
    )(a32, b32, x, g1)


def _wgrad_reduce_call(dproj_t, h, mixed_t, dout, sw, vec):
    wire = jnp.bfloat16
    half_sw = SW_ROWS // 2
    seq = h.shape[0]
    tk = min(2048, seq)
    nk = seq // tk
    n_steps = 2 * N_CHIPS
    rel_of = lambda s: s % 3 if s < 6 else 3
    half_of = lambda s: s // 3 if s < 6 else s - 6
    x, y = lax.axis_index("x"), lax.axis_index("y")
    chip_of = [2 * (1 - x) + (1 - y), 2 * (1 - x) + y, 2 * x + (1 - y), 2 * x + y]
    order = jnp.stack([2 * chip_of[rel_of(s)] + half_of(s) for s in range(n_steps)]).astype(jnp.int32)

    def body(order_ref, dpt_ref, h_hbm, mxt_ref, dout_hbm, sw_ref, small_ref, oa_ref, ob_ref, osw_ref, osmall_ref,
             h_v, dout_v, acc_a, acc_b, sib_a, sib_b, snd_a, snd_b, in_a, in_b, own_a, own_b, fin_a, fin_b,
             all_small, sw_sib, sw_chips, sw_fin, send_sems, recv_sems, loc_sems):
        x, y, c = lax.axis_index("x"), lax.axis_index("y"), lax.axis_index("c")
        me, sibling = (x, y, c), (x, y, 1 - c)
        steps = [(1 - x, 1 - y), (1 - x, y), (x, 1 - y)]
        j = 2 * x + y
        dev = 4 * x + 2 * y + c
        b, k = pl.program_id(0), pl.program_id(1)

        def copy(n, src, dst, to):
            return pltpu.make_async_remote_copy(src_ref=src, dst_ref=dst, send_sem=send_sems.at[n], recv_sem=recv_sems.at[n],
                                                device_id=to, device_id_type=MESH)

        def sw_rows(ref, hf):
            return ref.at[pl.ds(pl.multiple_of(hf * half_sw, 8), half_sw), :]

        sm_first = [copy(16, small_ref, all_small.at[dev], sibling)]
        sm_first += [copy(17 + r, small_ref, all_small.at[dev], (*chip, c)) for r, chip in enumerate(steps)]
        sw_to_sib = copy(23, sw_rows(sw_ref, 1 - c), sw_sib, sibling)

        @pl.when((b == 0) & (k == 0))
        def _():
            all_small[dev] = small_ref[...]
            for cp in sm_first + [sw_to_sib]:
                cp.start()

        def load(kk):
            rows = pl.ds(kk * tk, tk)
            return [pltpu.make_async_copy(h_hbm.at[rows, :], h_v.at[rows, :], loc_sems.at[kk]),
                    pltpu.make_async_copy(dout_hbm.at[rows, :], dout_v.at[rows, :], loc_sems.at[nk + kk])]

        @pl.when((b == 0) & (k == 0))
        def _():
            for part in range(2):
                for kk in range(nk):
                    load(kk)[part].start()

        for kk in range(nk):
            @pl.when((b == 0) & (k == kk))
            def _():
                load(kk)[0].wait()

        tok = pl.ds(pl.multiple_of(k * tk, tk), tk)
        pa = _mm(dpt_ref[...], h_v[tok, :])
        slot = b % 2

        @pl.when(k == 0)
        def _():
            acc_a[slot] = pa

        @pl.when(k != 0)
        def _():
            acc_a[slot] += pa

        @pl.when(k == nk - 1)
        def _():
            @pl.when(b == 0)
            def _():
                for kk in range(nk):
                    load(kk)[1].wait()

            acc_b[slot] = _mm(mxt_ref[...], dout_v[...])

        def to_sibling(s):
            r = rel_of(s)
            return [copy(r, acc_a.at[s % 2], sib_a.at[r], sibling), copy(4 + r, acc_b.at[s % 2], sib_b.at[r], sibling)]

        def chip_partial(s):
            r = rel_of(s)
            copy(r, sib_a.at[r], sib_a.at[r], me).wait_recv()
            copy(4 + r, sib_b.at[r], sib_b.at[r], me).wait_recv()
            return acc_a[s % 2] + sib_a[r], acc_b[s % 2] + sib_b[r]

        def to_owner(r):
            return [copy(8 + r, snd_a.at[r], in_a.at[r], (*steps[r], c)), copy(11 + r, snd_b.at[r], in_b.at[r], (*steps[r], c))]

        def vec_forwards():
            return [copy(20 + r, all_small.at[4 * cx + 2 * cy + c], all_small.at[4 * cx + 2 * cy + c], sibling)
                    for r, (cx, cy) in enumerate(steps)]

        def sw_to_chips():
            return [copy(24 + r, sw_chips.at[j], sw_chips.at[j], (*chip, c)) for r, chip in enumerate(steps)]

        def sw_to_sibling():
            return copy(27, sw_rows(osw_ref, c), sw_fin, sibling)

        @pl.when((b == 1) & (k == 0))
        def _():
            for r, (cx, cy) in enumerate(steps):
                d = 4 * cx + 2 * cy + c
                copy(17 + r, all_small.at[d], all_small.at[d], me).wait_recv()
            for cp in vec_forwards():
                cp.start()
            copy(23, sw_sib, sw_sib, me).wait_recv()
            sw_chips[j] = (sw_rows(sw_ref, c)[...] + sw_sib[...]).astype(wire)
            for cp in sw_to_chips():
                cp.start()

        @pl.when((b == 3) & (k == 0))
        def _():
            for r, (cx, cy) in enumerate(steps):
                cj = 2 * cx + cy
                copy(24 + r, sw_chips.at[cj], sw_chips.at[cj], me).wait_recv()
            tot_sw = sw_chips[0].astype(F32)
            for q in range(1, N_CHIPS):
                tot_sw = tot_sw + sw_chips[q].astype(F32)
            sw_rows(osw_ref, c)[...] = tot_sw
            sw_to_sibling().start()

        late = min(1, nk - 1)
        for s in range(n_steps):
            if s >= 1:
                sp = s - 1

                @pl.when((b == s) & (k == late) & (c == half_of(sp)))
                def _():
                    ta, tb = chip_partial(sp)
                    r = rel_of(sp)
                    if r < 3:
                        snd_a[r] = ta.astype(wire)
                        snd_b[r] = tb.astype(wire)
                        for cp in to_owner(r):
                            cp.start()
                    else:
                        own_a[...] = ta
                        own_b[...] = tb

                @pl.when((b == s) & (k == nk - 1) & (c != half_of(sp)))
                def _():
                    for cp in to_sibling(sp):
                        cp.wait_send()

            @pl.when((b == s) & (k == nk - 1) & (c != half_of(s)))
            def _():
                for cp in to_sibling(s):
                    cp.start()

        @pl.when((b == n_steps - 1) & (k == nk - 1))
        def _():
            last = n_steps - 1

            @pl.when(c == half_of(last))
            def _():
                own_a[...], own_b[...] = chip_partial(last)

            @pl.when(c != half_of(last))
            def _():
                for cp in to_sibling(last):
                    cp.wait_send()

            tot_a = own_a[...]
            tot_b = own_b[...]
            for s in range(3):
                copy(8 + s, in_a.at[s], in_a.at[s], me).wait_recv()
                copy(11 + s, in_b.at[s], in_b.at[s], me).wait_recv()
                tot_a = tot_a + in_a[s].astype(F32)
                tot_b = tot_b + in_b[s].astype(F32)
            mine_a = oa_ref.at[pl.ds(pl.multiple_of(c * HALF_A, 8), HALF_A), :]
            mine_b = ob_ref.at[pl.ds(pl.multiple_of(c * HALF_B, 8), HALF_B), :]
            mine_a[...] = tot_a
            mine_b[...] = tot_b
            back = [copy(14, mine_a, fin_a, sibling), copy(15, mine_b, fin_b, sibling)]
            for cp in back:
                cp.start()

            copy(16, small_ref, all_small.at[dev ^ 1], me).wait_recv()
            for r, (cx, cy) in enumerate(steps):
                d = 4 * cx + 2 * cy + (1 - c)
                copy(20 + r, all_small.at[d], all_small.at[d], me).wait_recv()
            tot = all_small[0]
            for d in range(1, N_DEV):
                tot = tot + all_small[d]
            osmall_ref[...] = tot

            copy(14, fin_a, fin_a, me).wait_recv()
            copy(15, fin_b, fin_b, me).wait_recv()
            copy(27, sw_fin, sw_fin, me).wait_recv()
            oa_ref[pl.ds(pl.multiple_of((1 - c) * HALF_A, 8), HALF_A), :] = fin_a[...]
            ob_ref[pl.ds(pl.multiple_of((1 - c) * HALF_B, 8), HALF_B), :] = fin_b[...]
            sw_rows(osw_ref, 1 - c)[...] = sw_fin[...]
            sends = sm_first + vec_forwards() + sw_to_chips() + back + [sw_to_sib, sw_to_sibling()]
            for s in range(3):
                sends += to_owner(s)
            for cp in sends:
                cp.wait_send()

    vmem = pl.BlockSpec(memory_space=pltpu.VMEM)
    vm = pltpu.VMEM
    grid_spec = pltpu.PrefetchScalarGridSpec(
        num_scalar_prefetch=1, grid=(n_steps, nk),
        in_specs=[pl.BlockSpec((HALF_A, tk), lambda b, k, o: (o[b], k)), pl.BlockSpec(memory_space=pl.ANY),
                  pl.BlockSpec((HALF_B, seq), lambda b, k, o: (o[b], 0)), pl.BlockSpec(memory_space=pl.ANY),
                  vmem, vmem],
        out_specs=[vmem, vmem, vmem, vmem],
        scratch_shapes=[vm((seq, D_MODEL), h.dtype), vm((seq, D_MODEL), dout.dtype),
                        vm((2, HALF_A, D_MODEL), F32), vm((2, HALF_B, D_MODEL), F32),
                        vm((N_CHIPS, HALF_A, D_MODEL), F32), vm((N_CHIPS, HALF_B, D_MODEL), F32),
                        vm((3, HALF_A, D_MODEL), wire), vm((3, HALF_B, D_MODEL), wire),
                        vm((3, HALF_A, D_MODEL), wire), vm((3, HALF_B, D_MODEL), wire),
                        vm((HALF_A, D_MODEL), F32), vm((HALF_B, D_MODEL), F32),
                        vm((HALF_A, D_MODEL), F32), vm((HALF_B, D_MODEL), F32),
                        vm((N_DEV, VEC_ROWS, 128), F32), vm((half_sw, 128), F32), vm((N_CHIPS, half_sw, 128), wire),
                        vm((half_sw, 128), F32),
                        pltpu.SemaphoreType.DMA((28,)), pltpu.SemaphoreType.DMA((28,)), pltpu.SemaphoreType.DMA((2 * nk,))])
    return pl.pallas_call(
        body, name="wgrad_reduce", grid_spec=grid_spec,
        out_shape=[jax.ShapeDtypeStruct((W_IN_SHARD, D_MODEL), F32), jax.ShapeDtypeStruct((W_OUT_SHARD, D_MODEL), F32),
                   jax.ShapeDtypeStruct((SW_ROWS, 128), F32), jax.ShapeDtypeStruct((VEC_ROWS, 128), F32)],
        compiler_params=pltpu.CompilerParams(dimension_semantics=("arbitrary", "arbitrary"), vmem_limit_bytes=VMEM_LIMIT),
    )(order, dproj_t, h, mixed_t, dout, sw, vec)


def _adamw(w, g, m, v):
    nm = ADAM_B1 * m + (1.0 - ADAM_B1) * g
    nv = ADAM_B2 * v + (1.0 - ADAM_B2) * (g * g)
    m_hat = nm / (1.0 - ADAM_B1 ** ADAM_STEP)
    v_hat = nv / (1.0 - ADAM_B2 ** ADAM_STEP)
    return -ADAM_LR * (m_hat / (jnp.sqrt(v_hat) + ADAM_EPS) + ADAM_WD * w), nm, nv


def _adamw_shards_call(a, b, steps=4):
    def body(*refs):
        for k in range(2):
            w_ref, g_ref, m_ref, v_ref = refs[4 * k:4 * k + 4]
            go_ref, d_ref, nm_ref, nv_ref = refs[8 + 4 * k:12 + 4 * k]
            gg = g_ref[...]
            go_ref[...] = gg
            d_ref[...], nm_ref[...], nv_ref[...] = _adamw(w_ref[...], gg, m_ref[...], v_ref[...])

    specs, shapes = [], []
    for w in (a[0], b[0]):
        rows, cols = w.shape
        specs += [pl.BlockSpec((rows // steps, cols), lambda i: (i, 0))] * 4
        shapes += [jax.ShapeDtypeStruct((rows, cols), F32)] * 4
    return pl.pallas_call(
        body, name="adamw_shards", grid=(steps,), in_specs=specs, out_specs=specs, out_shape=shapes,
        compiler_params=pltpu.CompilerParams(dimension_semantics=("arbitrary",)),
    )(*a, *b)


_SMALL = (("norm_g", (1, D_MODEL), R_G1), ("b_in", (1, IN_W), R_BIN), ("attn_sinks", (1, 8), R_SINK),
          ("sgu_ln_g", (1, SGU_W), R_LNG), ("sgu_ln_b", (1, SGU_W), R_LNB), ("sgu_b", (N_SGU_HEADS, BLOCK), R_SGUB),
          ("b_out", (1, D_MODEL), R_BOUT), ("final_norm_g", (1, D_MODEL), R_G3))


def _adamw_small_call(sw_g, vec_g, sgu_w3, ws, ms, vs):
    n = len(_SMALL)

    def body(*refs):
        sw_ref, vec_ref = refs[0], refs[1]
        w3 = refs[2:5]
        w_refs, m_refs, v_refs = refs[5:5 + n], refs[5 + n:5 + 2 * n], refs[5 + 2 * n:5 + 3 * n]
        outs = refs[5 + 3 * n:]
        outs[0][...] = vec_ref[R_LOSS:R_LOSS + 1, 0:1]
        g = sw_ref[...]
        outs[1][...] = g
        outs[2][...], outs[3][...], outs[4][...] = _adamw(w3[0][...], g, w3[1][...], w3[2][...])
        for k, (_, shape, row) in enumerate(_SMALL):
            if shape[0] == 1 and shape[1] >= 128:
                g = jnp.concatenate([vec_ref[row + q:row + q + 1, :] for q in range(shape[1] // 128)], axis=1)
            else:
                g = vec_ref[row:row + shape[0], 0:shape[1]]
            o = outs[5 + 4 * k:9 + 4 * k]
            o[0][...] = g
            o[1][...], o[2][...], o[3][...] = _adamw(w_refs[k][...], g, m_refs[k][...], v_refs[k][...])

    vmem = pl.BlockSpec(memory_space=pltpu.VMEM)
    out_shape = [jax.ShapeDtypeStruct((1, 1), F32)] + [jax.ShapeDtypeStruct((SW_ROWS, 128), F32)] * 4
    for _, shape, _ in _SMALL:
        out_shape += [jax.ShapeDtypeStruct(shape, F32)] * 4
    args = [sw_g, vec_g, *sgu_w3, *ws, *ms, *vs]
    return pl.pallas_call(
        body, name="adamw_small", in_specs=[vmem] * len(args), out_specs=[vmem] * len(out_shape), out_shape=out_shape,
    )(*args)


def kernel(x, norm_g, w_in, b_in, attn_sinks, sgu_ln_g, sgu_ln_b, sgu_w, sgu_b, w_out, b_out, final_norm_g, loss_target, m_norm_g, m_w_in, m_b_in, m_attn_sinks, m_sgu_ln_g, m_sgu_ln_b, m_sgu_w, m_sgu_b, m_w_out, m_b_out, m_final_norm_g, v_norm_g, v_w_in, v_b_in, v_attn_sinks, v_sgu_ln_g, v_sgu_ln_b, v_sgu_w, v_sgu_b, v_w_out, v_b_out, v_final_norm_g):
    seq = x.shape[1]
    win_t, wout, h, r1 = _gather_call(w_in[0].T, w_out[0], x[0], norm_g)
    dx, dproj_t, mixed_t, dout, sw, vec = _fused_call(
        x[0], h, r1, loss_target[0], attn_sinks[0], norm_g, b_in, sgu_ln_g, sgu_ln_b, sgu_w[0], sgu_b[0], b_out,
        final_norm_g.reshape(1, D_MODEL), win_t, wout)
    ga_t, g_w_out, sw, vec = _wgrad_reduce_call(dproj_t, h, mixed_t, dout, sw, vec)

    names = ["norm_g", "w_in", "b_in", "attn_sinks", "sgu_ln_g", "sgu_ln_b", "sgu_w", "sgu_b", "w_out", "b_out", "final_norm_g"]
    res = {}
    shards = _adamw_shards_call((w_in[0].T, ga_t, m_w_in[0].T, v_w_in[0].T), (w_out[0], g_w_out, m_w_out[0], v_w_out[0]))
    res["w_in"] = [a.T[None] for a in shards[:4]]
    res["w_out"] = [a[None] for a in shards[4:]]
    given = dict(norm_g=(norm_g, m_norm_g, v_norm_g), b_in=(b_in, m_b_in, v_b_in), attn_sinks=(attn_sinks, m_attn_sinks, v_attn_sinks),
                 sgu_ln_g=(sgu_ln_g, m_sgu_ln_g, v_sgu_ln_g), sgu_ln_b=(sgu_ln_b, m_sgu_ln_b, v_sgu_ln_b),
                 sgu_b=(sgu_b, m_sgu_b, v_sgu_b), b_out=(b_out, m_b_out, v_b_out),
                 final_norm_g=(final_norm_g, m_final_norm_g, v_final_norm_g))
    wmv = [[given[n][k].reshape(shape) for n, shape, _ in _SMALL] for k in range(3)]
    outs = _adamw_small_call(sw, vec, [a.reshape(SW_ROWS, BLOCK) for a in (sgu_w, m_sgu_w, v_sgu_w)], *wmv)
    loss = outs[0].reshape(())
    res["sgu_w"] = [a.reshape(sgu_w.shape) for a in outs[1:5]]
    for k, (n, _, _) in enumerate(_SMALL):
        res[n] = [a.reshape(given[n][0].shape) for a in outs[5 + 4 * k:9 + 4 * k]]

    return (loss, dx[None], *[res[n][0] for n in names], *[res[n][1] for n in names], *[res[n][2] for n in names],
            *[res[n][3] for n in names])
```

```python
import functools
import math

import jax
import jax.numpy as jnp
from jax import lax
from jax.experimental import pallas as pl
from jax.experimental.pallas import tpu as pltpu

F32 = jnp.float32
MXU_DTYPE = jnp.bfloat16

D_MODEL = 1024
HEAD_DIM = 64
ATTN_W = 512
SGU_W = 512
N_SGU_HEADS = 8
BLOCK = 128
IN_W = 2816
OFF_K, OFF_V, OFF_ZA, OFF_US, OFF_VS, OFF_ZS = 512, 640, 768, 1280, 1792, 2304
NORM_EPS = 1e-5
NEG_INF = -1e30
SCALE = HEAD_DIM ** -0.5
SQRT_HALF = math.sqrt(0.5)
INV_SQRT_2PI = 1.0 / math.sqrt(2.0 * math.pi)

N_CHIPS = 4
N_DEV = 8
W_IN_SHARD = IN_W // N_CHIPS
W_OUT_SHARD = D_MODEL // N_CHIPS
HALF_A = W_IN_SHARD // 2
HALF_B = W_OUT_SHARD // 2

TILE = 256
VMEM_LIMIT = 56 * 1024 * 1024

ADAM_LR, ADAM_B1, ADAM_B2, ADAM_EPS, ADAM_WD, ADAM_STEP = 0.001, 0.9, 0.999, 1e-08, 0.01, 10

SW_ROWS = N_SGU_HEADS * BLOCK
R_G1, R_BIN, R_SINK, R_LOSS, R_LNG, R_LNB, R_SGUB, R_BOUT, R_G3 = 0, 8, 32, 40, 48, 56, 64, 72, 80
VEC_ROWS = 88

MESH = pl.DeviceIdType.MESH


def _mm(a, b):
    return jnp.dot(a, b, preferred_element_type=F32)


def _mm_nt(a, b):
    return lax.dot_general(a, b, (((1,), (1,)), ((), ())), preferred_element_type=F32)


def _sigmoid(z):
    return 1.0 / (1.0 + jnp.exp(-z))


def _norm_cdf(z):
    return 0.5 * (1.0 + lax.erf(z * SQRT_HALF))


def _norm_pdf(z):
    return jnp.exp(-0.5 * z * z) * INV_SQRT_2PI


def _rows8(v):
    r, n = v.shape
    return jnp.sum(v.reshape(r // 8, 8, n), axis=0)


def _mean_rows(v):
    return jnp.sum(v, axis=1, keepdims=True) * (1.0 / v.shape[1])


def _fused_call(x, h, r1, tgt, sinks, g1, b_in, ln_g, ln_b, sgu_w, sgu_b, b_out, g3, win_t, wout):
    seq = x.shape[0]
    t = TILE
    nt = seq // t
    nb = t // BLOCK
    act = MXU_DTYPE

    def body(sinks_ref, x_ref, h_ref, hh_ref, r1_ref, tgt_ref, g1_ref, bin_ref, lng_ref, lnb_ref, sguw_ref, sgub_ref, bout_ref, g3_ref,
             wint_hbm, wout_hbm,
             dx_ref, dprojt_ref, mixedt_ref, dout_ref, sw_ref, vec_ref,
             dproj_ref, mixed_ref, wint_v, wout_v, wf_v, wb_v, q_s, kf_s, vf_s, k2_s, v2_s, gate_s, p_s, ps_s, o_s, u_s, mix_s, vhat_s, r2_s,
             cdfu_s, cdfv_s, doutf_s, dmix_s, dkf_s, dvf_s, carryk_s, carryv_s,
             acc_bin, acc_g1, acc_bout, acc_g3, acc_lng, acc_lnb, acc_dws, acc_dbs, acc_sink, acc_loss,
             h_s, rhs_s, vlnp_s, dvln_s, bexp_s, sems):
        i = pl.program_id(0)
        tile = nt - 1 - i
        lane128 = lax.broadcasted_iota(jnp.int32, (BLOCK, BLOCK), 1)
        lo = lane128 < HEAD_DIM

        @pl.when(i == 0)
        def _():
            cp_a = pltpu.make_async_copy(wint_hbm, wint_v, sems.at[0])
            cp_b = pltpu.make_async_copy(wout_hbm, wout_v, sems.at[1])
            cp_a.start()
            cp_b.start()
            for acc in (acc_bin, acc_g1, acc_bout, acc_g3, acc_lng, acc_lnb, acc_dws, acc_dbs, acc_sink, acc_loss,
                        carryk_s, carryv_s):
                acc[...] = jnp.zeros(acc.shape, F32)
            tril = lax.broadcasted_iota(jnp.int32, (BLOCK, BLOCK), 0) >= lane128
            for h in range(N_SGU_HEADS):
                w = jnp.where(tril, sguw_ref[h], 0.0)
                wf_v[h // 2, :, (h % 2) * BLOCK:(h % 2 + 1) * BLOCK] = w.astype(act)
                wb_v[h // 2, :, (h % 2) * BLOCK:(h % 2 + 1) * BLOCK] = w.T.astype(act)
            eye = lax.broadcasted_iota(jnp.int32, (BLOCK, BLOCK), 0) == lane128
            for p in range(4):
                col = [jnp.sum(jnp.where(eye, sgub_ref[2 * p + q:2 * p + q + 1, :], 0.0), axis=1, keepdims=True) for q in range(2)]
                bexp_s[:, p * BLOCK:(p + 1) * BLOCK] = jnp.where(lo, col[0], col[1])
            cp_a.wait()
            cp_b.wait()

        g1 = g1_ref[...]

        def rms(v):
            r = lax.rsqrt(_mean_rows(v * v) + NORM_EPS)
            return r, v * r

        h_s[0:BLOCK, :] = hh_ref[...]
        h_s[BLOCK:, :] = h_ref[...]
        r1_s = r1_ref

        h = h_ref[...]
        q = _mm_nt(h, wint_v[0:OFF_K, :]) + bin_ref[:, 0:OFF_K]
        q_s[...] = (q * SCALE).astype(act)
        kv = _mm_nt(h_s[...], wint_v[OFF_K:OFF_ZA, :]) + bin_ref[:, OFF_K:OFF_ZA]
        kf_s[...] = kv[:, :BLOCK]
        vf_s[...] = kv[:, BLOCK:]
        for r in range(4):
            cols = slice(OFF_ZA + r * 512, OFF_ZA + (r + 1) * 512)
            gate_s[r] = _mm_nt(h, wint_v[cols, :]) + bin_ref[:, cols]

        lo_kv = lax.broadcasted_iota(jnp.int32, (t + BLOCK, BLOCK), 1) < HEAD_DIM
        for src, dst in ((kf_s, k2_s), (vf_s, v2_s)):
            v = src[...]
            vr = pltpu.roll(v, HEAD_DIM, 1)
            dst[0] = jnp.where(lo_kv, v, vr).astype(act)
            dst[1] = jnp.where(lo_kv, vr, v).astype(act)

        rowi = lax.broadcasted_iota(jnp.int32, (BLOCK, 2 * BLOCK), 0)
        colj = lax.broadcasted_iota(jnp.int32, (BLOCK, 2 * BLOCK), 1)
        in_band = (colj > rowi) & (colj <= rowi + BLOCK)
        row512 = lax.broadcasted_iota(jnp.int32, (4 * BLOCK, 1), 0)

        def stacked_q(b, g):
            parts = []
            for p in range(2):
                slab = q_s[b * BLOCK:(b + 1) * BLOCK, (2 * g + p) * BLOCK:(2 * g + p + 1) * BLOCK]
                parts += [jnp.where(lo, slab, jnp.zeros_like(slab)), jnp.where(lo, jnp.zeros_like(slab), slab)]
            return jnp.concatenate(parts, axis=0)

        def sink_col(g):
            s = [sinks_ref[4 * g + k] for k in range(4)]
            return jnp.where(row512 < BLOCK, s[0], jnp.where(row512 < 2 * BLOCK, s[1], jnp.where(row512 < 3 * BLOCK, s[2], s[3])))

        for b in range(nb):
            band = slice(b * BLOCK, (b + 2) * BLOCK)
            first_key = jnp.where(tile * nb + b > 0, 0, BLOCK)
            valid = in_band & (colj >= first_key)
            valid4 = jnp.concatenate([valid] * 4, axis=0)
            for g in range(2):
                s = _mm_nt(stacked_q(b, g), k2_s[g, band, :])
                s = jnp.where(valid4, s, NEG_INF)
                sk = sink_col(g)
                m = jnp.maximum(jnp.max(s, axis=1, keepdims=True), sk)
                p = jnp.exp(s - m)
                psk = jnp.exp(sk - m)
                inv = 1.0 / (jnp.sum(p, axis=1, keepdims=True) + psk)
                p = p * inv
                p_s[b * 2 + g] = p
                ps_s[b * 2 + g] = psk * inv
                o2 = _mm(p.astype(act), v2_s[g, band, :])
                for pr in range(2):
                    o_s[b * BLOCK:(b + 1) * BLOCK, (2 * g + pr) * BLOCK:(2 * g + pr + 1) * BLOCK] = jnp.where(
                        lo, o2[(2 * pr) * BLOCK:(2 * pr + 1) * BLOCK], o2[(2 * pr + 1) * BLOCK:(2 * pr + 2) * BLOCK])

        lng = lng_ref[...]
        lnb = lnb_ref[...]

        def split_pairs(val, c):
            for p in range(4):
                slab = val[:, p * BLOCK:(p + 1) * BLOCK]
                rhs_s[p, 0:BLOCK, c * BLOCK:(c + 1) * BLOCK] = jnp.where(lo, slab, 0.0).astype(act)
                rhs_s[p, BLOCK:, c * BLOCK:(c + 1) * BLOCK] = jnp.where(lo, 0.0, slab).astype(act)

        for c in range(nb):
            rows = slice(c * BLOCK, (c + 1) * BLOCK)
            za = gate_s[0, rows, :]
            mixed_ref[rows, 0:ATTN_W] = (o_s[rows, :] * (za * _sigmoid(za))).astype(act)
            us = gate_s[1, rows, :]
            vs = gate_s[2, rows, :]
            cu = _norm_cdf(us)
            cv = _norm_cdf(vs)
            cdfu_s[rows, :] = cu
            cdfv_s[rows, :] = cv
            u = us * cu
            vg = vs * cv
            vc = vg - _mean_rows(vg)
            r2 = lax.rsqrt(_mean_rows(vc * vc) + NORM_EPS)
            vhat = vc * r2
            r2_s[rows, :] = r2
            vhat_s[rows, :] = vhat
            u_s[rows, :] = u
            split_pairs(vhat * lng + lnb, c)
        for p in range(4):
            cols = slice(p * BLOCK, (p + 1) * BLOCK)
            mix = _mm(wf_v[p], rhs_s[p])
            for c in range(nb):
                mix_s[c * BLOCK:(c + 1) * BLOCK, cols] = mix[:, c * BLOCK:(c + 1) * BLOCK] + bexp_s[:, cols]
        for c in range(nb):
            rows = slice(c * BLOCK, (c + 1) * BLOCK)
            zs = gate_s[3, rows, :]
            mixed_ref[rows, ATTN_W:] = (u_s[rows, :] * mix_s[rows, :] * (zs * _sigmoid(zs))).astype(act)

        g3 = g3_ref[...]
        proj_o = _mm(mixed_ref[...], wout_v[...])
        for c in range(nb):
            rows = slice(c * BLOCK, (c + 1) * BLOCK)
            out = x_ref[rows, :] + proj_o[rows, :] + bout_ref[...]
            r3, on = rms(out)
            e = on * g3 - tgt_ref[rows, :]
            e2 = _rows8(e * e)
            acc_loss[...] += sum(e2[:, k * 128:(k + 1) * 128] for k in range(D_MODEL // 128)) * (0.5 / D_MODEL)
            acc_g3[...] += _rows8(e * on)
            don = e * g3
            dout = (r3 * (1.0 / D_MODEL)) * (don - on * _mean_rows(don * on))
            doutf_s[rows, :] = dout
            dout_ref[rows, :] = dout.astype(act)
            acc_bout[...] += _rows8(dout)
        dmix_s[...] = _mm_nt(dout_ref[...], wout_v[...])

        dkf_s[...] = jnp.zeros(dkf_s.shape, F32)
        dvf_s[...] = jnp.zeros(dvf_s.shape, F32)
        for b in range(nb):
            rows = slice(b * BLOCK, (b + 1) * BLOCK)
            band = slice(b * BLOCK, (b + 2) * BLOCK)
            za = gate_s[0, rows, :]
            sg = _sigmoid(za)
            dao = dmix_s[rows, 0:ATTN_W]
            o = o_s[rows, :]
            do = dao * (za * sg)
            dza = dao * o * (sg * (1.0 + za * (1.0 - sg)))
            dproj_ref[rows, OFF_ZA:OFF_US] = dza.astype(act)
            acc_bin[:, OFF_ZA:OFF_US] += _rows8(dza)
            for g in range(2):
                do_parts, delta_parts = [], []
                for pr in range(2):
                    cols = slice((2 * g + pr) * BLOCK, (2 * g + pr + 1) * BLOCK)
                    d_pair = do[:, cols]
                    prod = d_pair * o[:, cols]
                    do_parts += [jnp.where(lo, d_pair, 0.0).astype(act), jnp.where(lo, 0.0, d_pair).astype(act)]
                    delta_parts += [jnp.sum(jnp.where(lo, prod, 0.0), axis=1, keepdims=True),
                                    jnp.sum(jnp.where(lo, 0.0, prod), axis=1, keepdims=True)]
                do_st = jnp.concatenate(do_parts, axis=0)
                delta = jnp.concatenate(delta_parts, axis=0)
                p = p_s[b * 2 + g]
                dp = _mm_nt(do_st, v2_s[g, band, :])
                ds = p * (dp - delta)
                sink_t = ps_s[b * 2 + g] * delta
                for k in range(4):
                    acc_sink[4 * g + k:4 * g + k + 1, :] += -jnp.sum(sink_t[k * BLOCK:(k + 1) * BLOCK], axis=0, keepdims=True)
                ds_a = ds.astype(act)
                dq2 = _mm(ds_a, k2_s[g, band, :]) * SCALE
                for pr in range(2):
                    cols = slice((2 * g + pr) * BLOCK, (2 * g + pr + 1) * BLOCK)
                    dq = jnp.where(lo, dq2[(2 * pr) * BLOCK:(2 * pr + 1) * BLOCK], dq2[(2 * pr + 1) * BLOCK:(2 * pr + 2) * BLOCK])
                    dproj_ref[rows, cols] = dq.astype(act)
                    acc_bin[:, cols] += _rows8(dq)
                for acc, lhs, rhs in ((dkf_s, stacked_q(b, g), ds_a), (dvf_s, do_st, p.astype(act))):
                    d2t = _mm(lhs.T, rhs)
                    acc[g * HEAD_DIM:(g + 1) * HEAD_DIM, band] += d2t[0:HEAD_DIM, :] + d2t[HEAD_DIM:, :]
        for acc, carry, off in ((dkf_s, carryk_s, OFF_K), (dvf_s, carryv_s, OFF_V)):
            acc[:, t:t + BLOCK] += carry[...]
            carry[...] = acc[:, 0:BLOCK]
            d = acc[:, BLOCK:].T
            dproj_ref[:, off:off + BLOCK] = d.astype(act)
            acc_bin[:, off:off + BLOCK] += _rows8(d)

        for c in range(nb):
            rows = slice(c * BLOCK, (c + 1) * BLOCK)
            dso = dmix_s[rows, ATTN_W:]
            u = u_s[rows, :]
            mix = mix_s[rows, :]
            zs = gate_s[3, rows, :]
            sg = _sigmoid(zs)
            sgs = zs * sg
            du = dso * mix * sgs
            dmx = dso * u * sgs
            dzs = dso * u * mix * (sg * (1.0 + zs * (1.0 - sg)))
            us = gate_s[1, rows, :]
            dus = du * (cdfu_s[rows, :] + us * _norm_pdf(us))
            vln = (vhat_s[rows, :] * lng + lnb).astype(act)
            for p in range(4):
                vlnp_s[p, :, c * BLOCK:(c + 1) * BLOCK] = vln[:, p * BLOCK:(p + 1) * BLOCK]
            split_pairs(dmx, c)
            acc_dbs[...] += dmx
            for off, val in ((OFF_US, dus), (OFF_ZS, dzs)):
                dproj_ref[rows, off:off + 512] = val.astype(act)
                acc_bin[:, off:off + 512] += _rows8(val)
        for p in range(4):
            dvln = _mm(wb_v[p], rhs_s[p])
            for c in range(nb):
                dvln_s[c * BLOCK:(c + 1) * BLOCK, p * BLOCK:(p + 1) * BLOCK] = dvln[:, c * BLOCK:(c + 1) * BLOCK]
            acc_dws[(2 * p) * BLOCK:(2 * p + 1) * BLOCK, :] += _mm_nt(rhs_s[p, 0:BLOCK, :], vlnp_s[p])
            acc_dws[(2 * p + 1) * BLOCK:(2 * p + 2) * BLOCK, :] += _mm_nt(rhs_s[p, BLOCK:, :], vlnp_s[p])
        for c in range(nb):
            rows = slice(c * BLOCK, (c + 1) * BLOCK)
            dvln = dvln_s[rows, :]
            vhat = vhat_s[rows, :]
            acc_lng[...] += _rows8(dvln * vhat)
            acc_lnb[...] += _rows8(dvln)
            dvhat = dvln * lng
            dvg = r2_s[rows, :] * (dvhat - _mean_rows(dvhat) - vhat * _mean_rows(dvhat * vhat))
            vs = gate_s[2, rows, :]
            dvs = dvg * (cdfv_s[rows, :] + vs * _norm_pdf(vs))
            dproj_ref[rows, OFF_VS:OFF_VS + 512] = dvs.astype(act)
            acc_bin[:, OFF_VS:OFF_VS + 512] += _rows8(dvs)

        dh = _mm(dproj_ref[...], wint_v[...])
        dprojt_ref[...] = dproj_ref[...].T
        mixedt_ref[...] = mixed_ref[...].T
        for c in range(nb):
            rows = slice(c * BLOCK, (c + 1) * BLOCK)
            r1 = jnp.sum(jnp.where(lax.broadcasted_iota(jnp.int32, (BLOCK, BLOCK), 0) == lane128, r1_s[:, rows], 0.0),
                         axis=1, keepdims=True)
            xn = x_ref[rows, :] * r1
            dhc = dh[rows, :]
            acc_g1[...] += _rows8(dhc * xn)
            dxn = dhc * g1
            dx_ref[rows, :] = doutf_s[rows, :] + r1 * (dxn - xn * _mean_rows(dxn * xn))

        @pl.when(i == nt - 1)
        def _():
            tril = lax.broadcasted_iota(jnp.int32, (BLOCK, BLOCK), 0) >= lane128
            for hh in range(N_SGU_HEADS):
                rws = slice(hh * BLOCK, (hh + 1) * BLOCK)
                sw_ref[rws, :] = jnp.where(tril, acc_dws[rws, :], 0.0)
            vec_ref[...] = jnp.zeros((VEC_ROWS, 128), F32)

            def put(row0, acc, scale=1.0):
                s = jnp.sum(acc[...], axis=0, keepdims=True) * scale
                for k in range(acc.shape[1] // 128):
                    vec_ref[row0 + k:row0 + k + 1, :] = s[:, k * 128:(k + 1) * 128]

            put(R_G1, acc_g1)
            put(R_BIN, acc_bin)
            put(R_LNG, acc_lng)
            put(R_LNB, acc_lnb)
            put(R_BOUT, acc_bout)
            put(R_G3, acc_g3, 1.0 / D_MODEL)
            vec_ref[R_SINK:R_SINK + 1, :] = jnp.sum(
                jnp.where(lax.broadcasted_iota(jnp.int32, (8, 128), 0) == lax.broadcasted_iota(jnp.int32, (8, 128), 1),
                          acc_sink[...], 0.0), axis=0, keepdims=True)
            vec_ref[R_LOSS:R_LOSS + 1, :] = jnp.zeros((1, 128), F32) + jnp.sum(acc_loss[...])
            dbs_t = acc_dbs[...].T
            vec_ref[R_SGUB:R_SGUB + 8, :] = jnp.sum(dbs_t.reshape(N_SGU_HEADS, SGU_W // N_SGU_HEADS, BLOCK), axis=1)

    full = lambda shape: pl.BlockSpec(shape, lambda i: (0,) * len(shape))
    tok = lambda w: pl.BlockSpec((t, w), lambda i: (nt - 1 - i, 0))
    in_specs = [
        pl.BlockSpec(memory_space=pltpu.SMEM),
        tok(D_MODEL), tok(D_MODEL),
        pl.BlockSpec((BLOCK, D_MODEL), lambda i: (jnp.maximum((nt - 1 - i) * nb - 1, 0), 0)),
        pl.BlockSpec((1, t), lambda i: (0, nt - 1 - i)), tok(D_MODEL),
        full((1, D_MODEL)), full((1, IN_W)), full((1, SGU_W)), full((1, SGU_W)),
        full((N_SGU_HEADS, BLOCK, BLOCK)), full((N_SGU_HEADS, BLOCK)), full((1, D_MODEL)), full((1, D_MODEL)),
        pl.BlockSpec(memory_space=pl.ANY), pl.BlockSpec(memory_space=pl.ANY),
    ]
    out_shape = [
        jax.ShapeDtypeStruct((seq, D_MODEL), F32),
        jax.ShapeDtypeStruct((IN_W, seq), act),
        jax.ShapeDtypeStruct((D_MODEL, seq), act),
        jax.ShapeDtypeStruct((seq, D_MODEL), act),
        jax.ShapeDtypeStruct((SW_ROWS, 128), F32),
        jax.ShapeDtypeStruct((VEC_ROWS, 128), F32),
    ]
    tok_t = lambda w: pl.BlockSpec((w, t), lambda i: (0, nt - 1 - i))
    out_specs = [tok(D_MODEL), tok_t(IN_W), tok_t(D_MODEL), tok(D_MODEL), full((SW_ROWS, 128)), full((VEC_ROWS, 128))]
    vm = pltpu.VMEM
    scratch = [
        vm((t, IN_W), act), vm((t, D_MODEL), act),
        vm((IN_W, D_MODEL), act), vm((D_MODEL, D_MODEL), act),
        vm((4, BLOCK, 2 * BLOCK), act), vm((4, BLOCK, 2 * BLOCK), act),
        vm((t, ATTN_W), act),
        vm((t + BLOCK, BLOCK), F32), vm((t + BLOCK, BLOCK), F32),
        vm((2, t + BLOCK, BLOCK), act), vm((2, t + BLOCK, BLOCK), act),
        vm((4, t, 512), F32),
        vm((2 * nb, 4 * BLOCK, 2 * BLOCK), F32), vm((2 * nb, 4 * BLOCK, 1), F32),
        vm((t, ATTN_W), F32), vm((t, SGU_W), F32), vm((t, SGU_W), F32), vm((t, SGU_W), F32), vm((t, 1), F32),
        vm((t, SGU_W), F32), vm((t, SGU_W), F32),
        vm((t, D_MODEL), F32), vm((t, D_MODEL), F32),
        vm((BLOCK, t + BLOCK), F32), vm((BLOCK, t + BLOCK), F32), vm((BLOCK, BLOCK), F32), vm((BLOCK, BLOCK), F32),
        vm((8, IN_W), F32), vm((8, D_MODEL), F32), vm((8, D_MODEL), F32), vm((8, D_MODEL), F32),
        vm((8, SGU_W), F32), vm((8, SGU_W), F32), vm((N_SGU_HEADS * BLOCK, BLOCK), F32), vm((BLOCK, SGU_W), F32),
        vm((8, 128), F32), vm((8, 128), F32),
        vm((t + BLOCK, D_MODEL), act), vm((4, 2 * BLOCK, t), act), vm((4, BLOCK, t), act), vm((t, SGU_W), F32),
        vm((BLOCK, SGU_W), F32),
        pltpu.SemaphoreType.DMA((2,)),
    ]
    return pl.pallas_call(
        body, name="fused", grid=(nt,), in_specs=in_specs, out_specs=out_specs, out_shape=out_shape,
        scratch_shapes=scratch,
        compiler_params=pltpu.CompilerParams(dimension_semantics=("arbitrary",), vmem_limit_bytes=VMEM_LIMIT),
    )(sinks, x, h, h, r1, tgt, g1, b_in, ln_g, ln_b, sgu_w, sgu_b, b_out, g3, win_t, wout)


def _gather_call(a32, b32, x, g1):
    act = MXU_DTYPE
    seq = x.shape[0]
    xc = min(256, seq)
    n_xc = seq // xc

    def body(a32_hbm, b32_hbm, x_hbm, g1_ref, ga_hbm, gb_hbm, h_hbm, r1_hbm, a32_v, b32_v, ga_v, gb_v, x_buf, h_buf, r1_v,
             send_sems, recv_sems, loc_sems, norm_sems):
        x, y, c = lax.axis_index("x"), lax.axis_index("y"), lax.axis_index("c")
        me, sibling, x_nbr, y_nbr = (x, y, c), (x, y, 1 - c), (1 - x, y, c), (x, 1 - y, c)
        j, j_x, j_y, j_d = 2 * x + y, 2 * (1 - x) + y, 2 * x + (1 - y), 2 * (1 - x) + (1 - y)
        arrays = ((0, ga_v, ga_hbm, W_IN_SHARD), (1, gb_v, gb_hbm, W_OUT_SHARD))

        loads = [pltpu.make_async_copy(a32_hbm, a32_v, loc_sems.at[0]), pltpu.make_async_copy(b32_hbm, b32_v, loc_sems.at[1])]
        for cp in reversed(loads):
            cp.start()

        def rows(shard, chip, hf=None, q=None):
            if hf is None:
                return pl.ds(pl.multiple_of(chip * shard, 16), shard)
            return pl.ds(pl.multiple_of(chip * shard + hf * (shard // 2) + q * (shard // 4), 16), shard // 4)

        def copy(k, ref, at, to):
            return pltpu.make_async_remote_copy(src_ref=ref.at[at, :], dst_ref=ref.at[at, :], send_sem=send_sems.at[k],
                                                recv_sem=recv_sems.at[k], device_id=to, device_id_type=MESH)

        stores = []

        def store(vmem, hbm, at):
            stores.append(pltpu.make_async_copy(vmem.at[at, :], hbm.at[at, :], loc_sems.at[len(stores)]))
            stores[-1].start()

        sent = []
        for w, vmem, hbm, shard in reversed(arrays):
            loads[w].wait()
            vmem[rows(shard, j), :] = (a32_v, b32_v)[w][...].astype(act)
            for k, (q, to) in enumerate(((0, x_nbr), (1, x_nbr), (1, y_nbr), (0, y_nbr))):
                sent.append(copy(12 * w + k, vmem, rows(shard, j, c, q), to))
                sent[-1].start()
        for w, vmem, hbm, shard in arrays:
            store(vmem, hbm, rows(shard, j))

        def x_load(n):
            return pltpu.make_async_copy(x_hbm.at[pl.ds(n * xc, xc), :], x_buf.at[n % 2], norm_sems.at[n % 2])

        def h_store(n):
            return pltpu.make_async_copy(h_buf.at[n % 2], h_hbm.at[pl.ds(n * xc, xc), :], norm_sems.at[2 + n % 2])

        g1 = g1_ref[...]
        eye = lax.broadcasted_iota(jnp.int32, (BLOCK, BLOCK), 0) == lax.broadcasted_iota(jnp.int32, (BLOCK, BLOCK), 1)
        x_load(0).start()
        for n in range(n_xc):
            if n + 1 < n_xc:
                x_load(n + 1).start()
            x_load(n).wait()
            if n >= 2:
                h_store(n - 2).wait()
            xv = x_buf[n % 2]
            r = lax.rsqrt(_mean_rows(xv * xv) + NORM_EPS)
            for blk in range(xc // BLOCK):
                r_row = jnp.sum(jnp.where(eye, r[blk * BLOCK:(blk + 1) * BLOCK], 0.0), axis=0, keepdims=True)
                r1_v[:, n * xc + blk * BLOCK:n * xc + (blk + 1) * BLOCK] = r_row
            h_buf[n % 2] = (xv * r * g1).astype(act)
            h_store(n).start()
        for n in range(max(n_xc - 2, 0), n_xc):
            h_store(n).wait()
        r1_out = pltpu.make_async_copy(r1_v, r1_hbm, norm_sems.at[4])
        r1_out.start()

        def landed(w, vmem, hbm, shard, k, chip, q, onward=None):
            at = rows(shard, chip, c, q)
            copy(12 * w + k, vmem, at, me).wait_recv()
            if onward is not None:
                sent.append(copy(12 * w + onward[0], vmem, at, onward[1]))
                sent[-1].start()
            sent.append(copy(12 * w + 6 + k, vmem, at, sibling))
            sent[-1].start()
            store(vmem, hbm, at)

        for arr in arrays:
            landed(*arr, 0, j_x, 0, onward=(4, y_nbr))
            landed(*arr, 2, j_y, 1, onward=(5, x_nbr))
        for arr in arrays:
            landed(*arr, 1, j_x, 1)
            landed(*arr, 3, j_y, 0)
            landed(*arr, 4, j_d, 0)
            landed(*arr, 5, j_d, 1)
        for w, vmem, hbm, shard in arrays:
            for k, (chip, q) in enumerate(((j_x, 0), (j_x, 1), (j_y, 1), (j_y, 0), (j_d, 0), (j_d, 1))):
                at = rows(shard, chip, 1 - c, q)
                copy(12 * w + 6 + k, vmem, at, me).wait_recv()
                store(vmem, hbm, at)
        for cp in sent:
            cp.wait_send()
        for cp in stores:
            cp.wait()
        r1_out.wait()

    hbm = pl.BlockSpec(memory_space=pl.ANY)
    vm = pltpu.VMEM
    return pl.pallas_call(
        body, name="gather", in_specs=[hbm, hbm, hbm, pl.BlockSpec(memory_space=vm)], out_specs=[hbm, hbm, hbm, hbm],
        out_shape=[jax.ShapeDtypeStruct((IN_W, D_MODEL), act), jax.ShapeDtypeStruct((D_MODEL, D_MODEL), act),
                   jax.ShapeDtypeStruct((seq, D_MODEL), act), jax.ShapeDtypeStruct((1, seq), F32)],
        scratch_shapes=[vm(a32.shape, F32), vm(b32.shape, F32), vm((IN_W, D_MODEL), act), vm((D_MODEL, D_MODEL), act),
                        vm((2, xc, D_MODEL), F32), vm((2, xc, D_MODEL), act), vm((1, seq), F32),
                        pltpu.SemaphoreType.DMA((24,)), pltpu.SemaphoreType.DMA((24,)), pltpu.SemaphoreType.DMA((26,)),
                        pltpu.SemaphoreType.DMA((5,))],
        compiler_params=pltpu.CompilerParams(vmem_limit_bytes=VMEM_LIMIT),
    )(a32, b32, x, g1)


def _wgrad_reduce_call(dproj_t, h, mixed_t, dout, sw, vec):
    wire = jnp.bfloat16
    half_sw = SW_ROWS // 2
    seq = h.shape[0]
    tk = min(2048, seq)
    nk = seq // tk
    n_steps = 2 * N_CHIPS
    rel_of = lambda s: s % 3 if s < 6 else 3
    half_of = lambda s: s // 3 if s < 6 else s - 6
    x, y = lax.axis_index("x"), lax.axis_index("y")
    chip_of = [2 * (1 - x) + (1 - y), 2 * (1 - x) + y, 2 * x + (1 - y), 2 * x + y]
    order = jnp.stack([2 * chip_of[rel_of(s)] + half_of(s) for s in range(n_steps)]).astype(jnp.int32)

    def body(order_ref, dpt_ref, h_hbm, mxt_ref, dout_hbm, sw_ref, small_ref, oa_ref, ob_ref, osw_ref, osmall_ref,
             h_v, dout_v, acc_a, acc_b, sib_a, sib_b, snd_a, snd_b, in_a, in_b, own_a, own_b, fin_a, fin_b,
             all_small, sw_sib, sw_chips, sw_fin, send_sems, recv_sems, loc_sems):
        x, y, c = lax.axis_index("x"), lax.axis_index("y"), lax.axis_index("c")
        me, sibling = (x, y, c), (x, y, 1 - c)
        steps = [(1 - x, 1 - y), (1 - x, y), (x, 1 - y)]
        j = 2 * x + y
        dev = 4 * x + 2 * y + c
        b, k = pl.program_id(0), pl.program_id(1)

        def copy(n, src, dst, to):
            return pltpu.make_async_remote_copy(src_ref=src, dst_ref=dst, send_sem=send_sems.at[n], recv_sem=recv_sems.at[n],
                                                device_id=to, device_id_type=MESH)

        def sw_rows(ref, hf):
            return ref.at[pl.ds(pl.multiple_of(hf * half_sw, 8), half_sw), :]

        sm_first = [copy(16, small_ref, all_small.at[dev], sibling)]
        sm_first += [copy(17 + r, small_ref, all_small.at[dev], (*chip, c)) for r, chip in enumerate(steps)]
        sw_to_sib = copy(23, sw_rows(sw_ref, 1 - c), sw_sib, sibling)

        @pl.when((b == 0) & (k == 0))
        def _():
            all_small[dev] = small_ref[...]
            for cp in sm_first + [sw_to_sib]:
                cp.start()

        def load(kk):
            rows = pl.ds(kk * tk, tk)
            return [pltpu.make_async_copy(h_hbm.at[rows, :], h_v.at[rows, :], loc_sems.at[kk]),
                    pltpu.make_async_copy(dout_hbm.at[rows, :], dout_v.at[rows, :], loc_sems.at[nk + kk])]

        @pl.when((b == 0) & (k == 0))
        def _():
            for part in range(2):
                for kk in range(nk):
                    load(kk)[part].start()

        for kk in range(nk):
            @pl.when((b == 0) & (k == kk))
            def _():
                load(kk)[0].wait()

        tok = pl.ds(pl.multiple_of(k * tk, tk), tk)
        pa = _mm(dpt_ref[...], h_v[tok, :])
        slot = b % 2

        @pl.when(k == 0)
        def _():
            acc_a[slot] = pa

        @pl.when(k != 0)
        def _():
            acc_a[slot] += pa

        @pl.when(k == nk - 1)
        def _():
            @pl.when(b == 0)
            def _():
                for kk in range(nk):
                    load(kk)[1].wait()

            acc_b[slot] = _mm(mxt_ref[...], dout_v[...])

        def to_sibling(s):
            r = rel_of(s)
            return [copy(r, acc_a.at[s % 2], sib_a.at[r], sibling), copy(4 + r, acc_b.at[s % 2], sib_b.at[r], sibling)]

        def chip_partial(s):
            r = rel_of(s)
            copy(r, sib_a.at[r], sib_a.at[r], me).wait_recv()
            copy(4 + r, sib_b.at[r], sib_b.at[r], me).wait_recv()
            return acc_a[s % 2] + sib_a[r], acc_b[s % 2] + sib_b[r]

        def to_owner(r):
            return [copy(8 + r, snd_a.at[r], in_a.at[r], (*steps[r], c)), copy(11 + r, snd_b.at[r], in_b.at[r], (*steps[r], c))]

        def vec_forwards():
            return [copy(20 + r, all_small.at[4 * cx + 2 * cy + c], all_small.at[4 * cx + 2 * cy + c], sibling)
                    for r, (cx, cy) in enumerate(steps)]

        def sw_to_chips():
            return [copy(24 + r, sw_chips.at[j], sw_chips.at[j], (*chip, c)) for r, chip in enumerate(steps)]

        def sw_to_sibling():
            return copy(27, sw_rows(osw_ref, c), sw_fin, sibling)

        @pl.when((b == 1) & (k == 0))
        def _():
            for r, (cx, cy) in enumerate(steps):
                d = 4 * cx + 2 * cy + c
                copy(17 + r, all_small.at[d], all_small.at[d], me).wait_recv()
            for cp in vec_forwards():
                cp.start()
            copy(23, sw_sib, sw_sib, me).wait_recv()
            sw_chips[j] = (sw_rows(sw_ref, c)[...] + sw_sib[...]).astype(wire)
            for cp in sw_to_chips():
                cp.start()

        @pl.when((b == 3) & (k == 0))
        def _():
            for r, (cx, cy) in enumerate(steps):
                cj = 2 * cx + cy
                copy(24 + r, sw_chips.at[cj], sw_chips.at[cj], me).wait_recv()
            tot_sw = sw_chips[0].astype(F32)
            for q in range(1, N_CHIPS):
                tot_sw = tot_sw + sw_chips[q].astype(F32)
            sw_rows(osw_ref, c)[...] = tot_sw
            sw_to_sibling().start()

        late = min(1, nk - 1)
        for s in range(n_steps):
            if s >= 1:
                sp = s - 1

                @pl.when((b == s) & (k == late) & (c == half_of(sp)))
                def _():
                    ta, tb = chip_partial(sp)
                    r = rel_of(sp)
                    if r < 3:
                        snd_a[r] = ta.astype(wire)
                        snd_b[r] = tb.astype(wire)
                        for cp in to_owner(r):
                            cp.start()
                    else:
                        own_a[...] = ta
                        own_b[...] = tb

                @pl.when((b == s) & (k == nk - 1) & (c != half_of(sp)))
                def _():
                    for cp in to_sibling(sp):
                        cp.wait_send()

            @pl.when((b == s) & (k == nk - 1) & (c != half_of(s)))
            def _():
                for cp in to_sibling(s):
                    cp.start()

        @pl.when((b == n_steps - 1) & (k == nk - 1))
        def _():
            last = n_steps - 1

            @pl.when(c == half_of(last))
            def _():
                own_a[...], own_b[...] = chip_partial(last)

            @pl.when(c != half_of(last))
            def _():
                for cp in to_sibling(last):
                    cp.wait_send()

            tot_a = own_a[...]
            tot_b = own_b[...]
            for s in range(3):
                copy(8 + s, in_a.at[s], in_a.at[s], me).wait_recv()
                copy(11 + s, in_b.at[s], in_b.at[s], me).wait_recv()
                tot_a = tot_a + in_a[s].astype(F32)
                tot_b = tot_b + in_b[s].astype(F32)
            mine_a = oa_ref.at[pl.ds(pl.multiple_of(c * HALF_A, 8), HALF_A), :]
            mine_b = ob_ref.at[pl.ds(pl.multiple_of(c * HALF_B, 8), HALF_B), :]
            mine_a[...] = tot_a
            mine_b[...] = tot_b
            back = [copy(14, mine_a, fin_a, sibling), copy(15, mine_b, fin_b, sibling)]
            for cp in back:
                cp.start()

            copy(16, small_ref, all_small.at[dev ^ 1], me).wait_recv()
            for r, (cx, cy) in enumerate(steps):
                d = 4 * cx + 2 * cy + (1 - c)
                copy(20 + r, all_small.at[d], all_small.at[d], me).wait_recv()
            tot = all_small[0]
            for d in range(1, N_DEV):
                tot = tot + all_small[d]
            osmall_ref[...] = tot

            copy(14, fin_a, fin_a, me).wait_recv()
            copy(15, fin_b, fin_b, me).wait_recv()
            copy(27, sw_fin, sw_fin, me).wait_recv()
            oa_ref[pl.ds(pl.multiple_of((1 - c) * HALF_A, 8), HALF_A), :] = fin_a[...]
            ob_ref[pl.ds(pl.multiple_of((1 - c) * HALF_B, 8), HALF_B), :] = fin_b[...]
            sw_rows(osw_ref, 1 - c)[...] = sw_fin[...]
            sends = sm_first + vec_forwards() + sw_to_chips() + back + [sw_to_sib, sw_to_sibling()]
            for s in range(3):
                sends += to_owner(s)
            for cp in sends:
                cp.wait_send()

    vmem = pl.BlockSpec(memory_space=pltpu.VMEM)
    vm = pltpu.VMEM
    grid_spec = pltpu.PrefetchScalarGridSpec(
        num_scalar_prefetch=1, grid=(n_steps, nk),
        in_specs=[pl.BlockSpec((HALF_A, tk), lambda b, k, o: (o[b], k)), pl.BlockSpec(memory_space=pl.ANY),
                  pl.BlockSpec((HALF_B, seq), lambda b, k, o: (o[b], 0)), pl.BlockSpec(memory_space=pl.ANY),
                  vmem, vmem],
        out_specs=[vmem, vmem, vmem, vmem],
        scratch_shapes=[vm((seq, D_MODEL), h.dtype), vm((seq, D_MODEL), dout.dtype),
                        vm((2, HALF_A, D_MODEL), F32), vm((2, HALF_B, D_MODEL), F32),
                        vm((N_CHIPS, HALF_A, D_MODEL), F32), vm((N_CHIPS, HALF_B, D_MODEL), F32),
                        vm((3, HALF_A, D_MODEL), wire), vm((3, HALF_B, D_MODEL), wire),
                        vm((3, HALF_A, D_MODEL), wire), vm((3, HALF_B, D_MODEL), wire),
                        vm((HALF_A, D_MODEL), F32), vm((HALF_B, D_MODEL), F32),
                        vm((HALF_A, D_MODEL), F32), vm((HALF_B, D_MODEL), F32),
                        vm((N_DEV, VEC_ROWS, 128), F32), vm((half_sw, 128), F32), vm((N_CHIPS, half_sw, 128), wire),
                        vm((half_sw, 128), F32),
                        pltpu.SemaphoreType.DMA((28,)), pltpu.SemaphoreType.DMA((28,)), pltpu.SemaphoreType.DMA((2 * nk,))])
    return pl.pallas_call(
        body, name="wgrad_reduce", grid_spec=grid_spec,
        out_shape=[jax.ShapeDtypeStruct((W_IN_SHARD, D_MODEL), F32), jax.ShapeDtypeStruct((W_OUT_SHARD, D_MODEL), F32),
                   jax.ShapeDtypeStruct((SW_ROWS, 128), F32), jax.ShapeDtypeStruct((VEC_ROWS, 128), F32)],
        compiler_params=pltpu.CompilerParams(dimension_semantics=("arbitrary", "arbitrary"), vmem_limit_bytes=VMEM_LIMIT),
    )(order, dproj_t, h, mixed_t, dout, sw, vec)


def _adamw(w, g, m, v):
    nm = ADAM_B1 * m + (1.0 - ADAM_B1) * g
    nv = ADAM_B2 * v + (1.0 - ADAM_B2) * (g * g)
    m_hat = nm / (1.0 - ADAM_B1 ** ADAM_STEP)
    v_hat = nv / (1.0 - ADAM_B2 ** ADAM_STEP)
    return -ADAM_LR * (m_hat / (jnp.sqrt(v_hat) + ADAM_EPS) + ADAM_WD * w), nm, nv


def _adamw_shards_call(a, b, steps=4):
    def body(*refs):
        for k in range(2):
            w_ref, g_ref, m_ref, v_ref = refs[4 * k:4 * k + 4]
            go_ref, d_ref, nm_ref, nv_ref = refs[8 + 4 * k:12 + 4 * k]
            gg = g_ref[...]
            go_ref[...] = gg
            d_ref[...], nm_ref[...], nv_ref[...] = _adamw(w_ref[...], gg, m_ref[...], v_ref[...])

    specs, shapes = [], []
    for w in (a[0], b[0]):
        rows, cols = w.shape
        specs += [pl.BlockSpec((rows // steps, cols), lambda i: (i, 0))] * 4
        shapes += [jax.ShapeDtypeStruct((rows, cols), F32)] * 4
    return pl.pallas_call(
        body, name="adamw_shards", grid=(steps,), in_specs=specs, out_specs=specs, out_shape=shapes,
        compiler_params=pltpu.CompilerParams(dimension_semantics=("arbitrary",)),
    )(*a, *b)


_SMALL = (("norm_g", (1, D_MODEL), R_G1), ("b_in", (1, IN_W), R_BIN), ("attn_sinks", (1, 8), R_SINK),
          ("sgu_ln_g", (1, SGU_W), R_LNG), ("sgu_ln_b", (1, SGU_W), R_LNB), ("sgu_b", (N_SGU_HEADS, BLOCK), R_SGUB),
          ("b_out", (1, D_MODEL), R_BOUT), ("final_norm_g", (1, D_MODEL), R_G3))


def _adamw_small_call(sw_g, vec_g, sgu_w3, ws, ms, vs):
    n = len(_SMALL)

    def body(*refs):
        sw_ref, vec_ref = refs[0], refs[1]
        w3 = refs[2:5]
        w_refs, m_refs, v_refs = refs[5:5 + n], refs[5 + n:5 + 2 * n], refs[5 + 2 * n:5 + 3 * n]
        outs = refs[5 + 3 * n:]
        outs[0][...] = vec_ref[R_LOSS:R_LOSS + 1, 0:1]
        g = sw_ref[...]
        outs[1][...] = g
        outs[2][...], outs[3][...], outs[4][...] = _adamw(w3[0][...], g, w3[1][...], w3[2][...])
        for k, (_, shape, row) in enumerate(_SMALL):
            if shape[0] == 1 and shape[1] >= 128:
                g = jnp.concatenate([vec_ref[row + q:row + q + 1, :] for q in range(shape[1] // 128)], axis=1)
            else:
                g = vec_ref[row:row + shape[0], 0:shape[1]]
            o = outs[5 + 4 * k:9 + 4 * k]
            o[0][...] = g
            o[1][...], o[2][...], o[3][...] = _adamw(w_refs[k][...], g, m_refs[k][...], v_refs[k][...])

    vmem = pl.BlockSpec(memory_space=pltpu.VMEM)
    out_shape = [jax.ShapeDtypeStruct((1, 1), F32)] + [jax.ShapeDtypeStruct((SW_ROWS, 128), F32)] * 4
    for _, shape, _ in _SMALL:
        out_shape += [jax.ShapeDtypeStruct(shape, F32)] * 4
    args = [sw_g, vec_g, *sgu_w3, *ws, *ms, *vs]
    return pl.pallas_call(
        body, name="adamw_small", in_specs=[vmem] * len(args), out_specs=[vmem] * len(out_shape), out_shape=out_shape,
    )(*args)


def kernel(x, norm_g, w_in, b_in, attn_sinks, sgu_ln_g, sgu_ln_b, sgu_w, sgu_b, w_out, b_out, final_norm_g, loss_target, m_norm_g, m_w_in, m_b_in, m_attn_sinks, m_sgu_ln_g, m_sgu_ln_b, m_sgu_w, m_sgu_b, m_w_out, m_b_out, m_final_norm_g, v_norm_g, v_w_in, v_b_in, v_attn_sinks, v_sgu_ln_g, v_sgu_ln_b, v_sgu_w, v_sgu_b, v_w_out, v_b_out, v_final_norm_g):
    seq = x.shape[1]
    win_t, wout, h, r1 = _gather_call(w_in[0].T, w_out[0], x[0], norm_g)
    dx, dproj_t, mixed_t, dout, sw, vec = _fused_call(
        x[0], h, r1, loss_target[0], attn_sinks[0], norm_g, b_in, sgu_ln_g, sgu_ln_b, sgu_w[0], sgu_b[0], b_out,
        final_norm_g.reshape(1, D_MODEL), win_t, wout)
    ga_t, g_w_out, sw, vec = _wgrad_reduce_call(dproj_t, h, mixed_t, dout, sw, vec)

    names = ["norm_g", "w_in", "b_in", "attn_sinks", "sgu_ln_g", "sgu_ln_b", "sgu_w", "sgu_b", "w_out", "b_out", "final_norm_g"]
    res = {}
    shards = _adamw_shards_call((w_in[0].T, ga_t, m_w_in[0].T, v_w_in[0].T), (w_out[0], g_w_out, m_w_out[0], v_w_out[0]))
    res["w_in"] = [a.T[None] for a in shards[:4]]
    res["w_out"] = [a[None] for a in shards[4:]]
    given = dict(norm_g=(norm_g, m_norm_g, v_norm_g), b_in=(b_in, m_b_in, v_b_in), attn_sinks=(attn_sinks, m_attn_sinks, v_attn_sinks),
                 sgu_ln_g=(sgu_ln_g, m_sgu_ln_g, v_sgu_ln_g), sgu_ln_b=(sgu_ln_b, m_sgu_ln_b, v_sgu_ln_b),
                 sgu_b=(sgu_b, m_sgu_b, v_sgu_b), b_out=(b_out, m_b_out, v_b_out),
                 final_norm_g=(final_norm_g, m_final_norm_g, v_final_norm_g))
    wmv = [[given[n][k].reshape(shape) for n, shape, _ in _SMALL] for k in range(3)]
    outs = _adamw_small_call(sw, vec, [a.reshape(SW_ROWS, BLOCK) for a in (sgu_w, m_sgu_w, v_sgu_w)], *wmv)
    loss = outs[0].reshape(())
    res["sgu_w"] = [a.reshape(sgu_w.shape) for a in outs[1:5]]
    for k, (n, _, _) in enumerate(_SMALL):
        res[n] = [a.reshape(given[n][0].shape) for a in outs[5 + 4 * k:9 + 4 * k]]

    return (loss, dx[None], *[res[n][0] for n in names], *[res[n][1] for n in names], *[res[n][2] for n in names],
            *[res[n][3] for n in names])
```
